```python
import math
import jax, jax.numpy as jnp
from jax import lax
import numpy as np


D_MODEL = 1024
BATCH = 8
SEQ = 2048
DEPTH = 2
DEC_BATCH = 16
DEC_SEQ = 2048
PAST_LEN = 128

N_EVEN = (DEPTH + 1) // 2
N_ODD = DEPTH // 2
EPS = 1e-6

A_HEADS = 4
A_DK = 128
A_DV = 128
A_WIDTH = A_HEADS * A_DV
A_QKV = 2 * A_HEADS * A_DK + A_HEADS * A_DV
CHUNK = 64
SHORT_CONV = 3

B_WIDTH = D_MODEL - A_WIDTH
HY_ORDER = 2
HY_EMB = 33
HY_BANDS = (HY_EMB - 1) // 2
HY_FFN = 64
HY_TARGET = 1e-2
HY_FAST_PCT = 0.3
HY_SLOW_PCT = 1.5
HY_MIN_DECAY = math.log(HY_TARGET) / HY_SLOW_PCT
HY_MAX_DECAY = math.log(HY_TARGET) / HY_FAST_PCT

EVEN_IN = A_QKV + A_WIDTH + 4 * A_HEADS + 3 * B_WIDTH

C_HEADS = 16
C_KV_HEADS = 4
C_HEAD_DIM = 64
C_GROUP = C_HEADS // C_KV_HEADS
WINDOW = 128
C_BLOCK = 128
ODD_IN = (C_HEADS + 2 * C_KV_HEADS) * C_HEAD_DIM

N_GROUPS = 4
EXPERTS_PER_GROUP = 8
TOP_K_INNER = 2
D_EXPERT = 256

kernel_name = "hybrid_bidir_deltanet_hyena_swa_hmoe"

f32 = jnp.float32


def rmsnorm(x, g):
    xf = x.astype(f32)
    y = xf * lax.rsqrt(jnp.mean(xf * xf, axis=-1, keepdims=True) + EPS)
    return (y * g.astype(f32)).astype(x.dtype)


def l2norm(x):
    return x * lax.rsqrt(jnp.sum(x * x, axis=-1, keepdims=True) + EPS)


def centred_dwconv(x, w):
    L = x.shape[1]
    pad = (SHORT_CONV - 1) // 2
    xp = jnp.pad(x, ((0, 0), (pad, pad), (0, 0)))
    return sum(xp[:, j:j + L] * w[:, j] for j in range(SHORT_CONV))


def chunk_gated_delta(q, k, v, g, beta):
    B, H, L, DK = q.shape
    DV = v.shape[-1]
    N = L // CHUNK
    q = q * (DK ** -0.5)
    q = q.reshape(B, H, N, CHUNK, DK)
    k = k.reshape(B, H, N, CHUNK, DK)
    v = v.reshape(B, H, N, CHUNK, DV)
    beta = beta.reshape(B, H, N, CHUNK)
    gc = jnp.cumsum(g.reshape(B, H, N, CHUNK), axis=-1)
    tril = jnp.tril(jnp.ones((CHUNK, CHUNK), dtype=bool))
    strict = jnp.tril(jnp.ones((CHUNK, CHUNK), dtype=bool), -1)
    diff = gc[..., :, None] - gc[..., None, :]
    decay = jnp.where(tril, jnp.exp(jnp.where(tril, diff, 0.0)), 0.0)
    k_beta = k * beta[..., None]
    v_beta = v * beta[..., None]
    kk = jnp.einsum('bhncd,bhnsd->bhncs', k_beta, k) * decay
    a_mat = jnp.eye(CHUNK, dtype=f32) + jnp.where(strict, kk, 0.0)
    u = lax.linalg.triangular_solve(a_mat, v_beta, left_side=True, lower=True, unit_diagonal=True)
    w = lax.linalg.triangular_solve(a_mat, k_beta * jnp.exp(gc)[..., None], left_side=True, lower=True, unit_diagonal=True)
    qk = jnp.einsum('bhncd,bhnsd->bhncs', q, k) * decay
    q_dec = q * jnp.exp(gc)[..., None]
    k_dec = k * jnp.exp(gc[..., -1:] - gc)[..., None]
    g_last = jnp.exp(gc[..., -1])

    def step(S, xs):
        qk_i, qd_i, kd_i, u_i, w_i, gl_i = xs
        v_new = u_i - jnp.einsum('bhcd,bhde->bhce', w_i, S)
        o = jnp.einsum('bhcd,bhde->bhce', qd_i, S) + jnp.einsum('bhcs,bhse->bhce', qk_i, v_new)
        S = S * gl_i[..., None, None] + jnp.einsum('bhcd,bhce->bhde', kd_i, v_new)
        return S, o

    xs = tuple(jnp.moveaxis(t, 2, 0) for t in (qk, q_dec, k_dec, u, w, g_last))
    S0 = jnp.zeros((B, H, DK, DV), f32)
    _, o = lax.scan(step, S0, xs)
    return jnp.moveaxis(o, 0, 2).reshape(B, H, L, DV)


def gated_deltanet_bidir(qkv, z, ab, conv_a, alog_f, alog_b, dtb_f, dtb_b, onorm):
    B, L, _ = qkv.shape
    qkv = jax.nn.silu(centred_dwconv(qkv, conv_a))
    q, k, v = jnp.split(qkv, [A_HEADS * A_DK, 2 * A_HEADS * A_DK], axis=-1)

    def heads(t, d):
        return t.reshape(B, L, A_HEADS, d).transpose(0, 2, 1, 3).astype(f32)

    q = l2norm(heads(q, A_DK))
    k = l2norm(heads(k, A_DK))
    v = heads(v, A_DV)
    b_f, b_b, a_f, a_b = jnp.split(ab.astype(f32), 4, axis=-1)

    def gates(b_raw, a_raw, alog, dtb):
        beta = jax.nn.sigmoid(b_raw)
        g = -jnp.exp(alog.astype(f32)) * jax.nn.softplus(a_raw + dtb.astype(f32))
        return beta.transpose(0, 2, 1), g.transpose(0, 2, 1)

    beta_f, g_f = gates(b_f, a_f, alog_f, dtb_f)
    beta_b, g_b = gates(b_b, a_b, alog_b, dtb_b)
    flip = lambda t: jnp.flip(t, axis=2)
    o = chunk_gated_delta(q, k, v, g_f, beta_f) + flip(
        chunk_gated_delta(flip(q), flip(k), flip(v), flip(g_b), flip(beta_b)))
    o = o.transpose(0, 2, 1, 3)
    o = rmsnorm(o, onorm) * jax.nn.silu(z.reshape(B, L, A_HEADS, A_DV).astype(f32))
    return o.reshape(B, L, A_WIDTH).astype(qkv.dtype)


def hyena_filters(L, w1, b1, w2, b2, w3, b3, w4, freq):
    t = jnp.linspace(0.0, 1.0, L, dtype=f32)[:, None]
    wpos = (2.0 * math.pi / L) * jnp.arange(L, dtype=f32)[:, None]
    bands = jnp.linspace(1e-4, HY_BANDS - 1, HY_BANDS, dtype=f32)[None, :]
    feats = jnp.concatenate([t, jnp.cos(bands * wpos), -jnp.sin(bands * wpos)], axis=-1)
    fr = freq.astype(f32)
    h = jnp.sin(fr[0] * (feats @ w1.astype(f32) + b1.astype(f32)))
    h = jnp.sin(fr[1] * (h @ w2.astype(f32) + b2.astype(f32)))
    h = jnp.sin(fr[2] * (h @ w3.astype(f32) + b3.astype(f32)))
    h = (h @ w4.astype(f32)).reshape(L, HY_ORDER, 2, B_WIDTH)
    deltas = jnp.abs(jnp.linspace(HY_MIN_DECAY, HY_MAX_DECAY, B_WIDTH, dtype=f32))
    window = jnp.exp(-t * deltas[None, :])
    return h * window[:, None, None, :]


def bidir_fftconv(u, h_fwd, h_bwd, d_bias):
    L, C = h_fwd.shape
    kfull = jnp.concatenate([(h_fwd[0] + h_bwd[0])[None], h_fwd[1:], jnp.zeros((1, C), f32), h_bwd[1:][::-1]], axis=0)
    kf = jnp.fft.rfft(kfull, n=2 * L, axis=0)
    uf = jnp.fft.rfft(u, n=2 * L, axis=1)
    y = jnp.fft.irfft(uf * kf[None], n=2 * L, axis=1)[:, :L]
    return y + u * d_bias.astype(f32)


def hyena_bidir(hx, conv_b, w1, b1, w2, b2, w3, b3, w4, freq, dbias):
    L = hx.shape[1]
    hx = centred_dwconv(hx, conv_b).astype(f32)
    x1, x2, v = jnp.split(hx, 3, axis=-1)
    filt = hyena_filters(L, w1, b1, w2, b2, w3, b3, w4, freq)
    zz = v
    for o, gate in enumerate((x1, x2)):
        zz = gate * bidir_fftconv(zz, filt[:, o, 0], filt[:, o, 1], dbias[o])
    return zz


def even_mixer(h, w_in, conv_a, alog_f, alog_b, dtb_f, dtb_b, onorm, conv_b,
               w1, b1, w2, b2, w3, b3, w4, freq, dbias, w_out):
    p = h @ w_in
    qkv, z, ab, hx = jnp.split(p, [A_QKV, A_QKV + A_WIDTH, A_QKV + A_WIDTH + 4 * A_HEADS], axis=-1)
    o_a = gated_deltanet_bidir(qkv, z, ab, conv_a, alog_f, alog_b, dtb_f, dtb_b, onorm)
    o_b = hyena_bidir(hx, conv_b, w1, b1, w2, b2, w3, b3, w4, freq, dbias).astype(h.dtype)
    return jnp.concatenate([o_a, o_b], axis=-1) @ w_out


def windowed_gqa(h, w_qkv, sinks, w_out):
    B, L, _ = h.shape
    nb = L // C_BLOCK
    qkv = h @ w_qkv
    q, k, v = jnp.split(qkv, [C_HEADS * C_HEAD_DIM, (C_HEADS + C_KV_HEADS) * C_HEAD_DIM], axis=-1)
    q = q.reshape(B, nb, C_BLOCK, C_KV_HEADS, C_GROUP, C_HEAD_DIM)

    def band(t):
        t = t.reshape(B, L, C_KV_HEADS, C_HEAD_DIM)
        tp = jnp.pad(t, ((0, 0), (C_BLOCK, C_BLOCK), (0, 0), (0, 0))).reshape(B, nb + 2, C_BLOCK, C_KV_HEADS, C_HEAD_DIM)
        return jnp.concatenate([tp[:, :-2], tp[:, 1:-1], tp[:, 2:]], axis=2)

    kb, vb = band(k), band(v)
    qi = jnp.arange(C_BLOCK)[:, None]
    kj = jnp.arange(3 * C_BLOCK)[None, :]
    rel = C_BLOCK + qi - kj
    in_win = jnp.abs(rel) <= WINDOW
    slopes = (2.0 ** (-8.0 * jnp.arange(1, C_HEADS + 1, dtype=f32) / C_HEADS)).reshape(C_KV_HEADS, C_GROUP)
    alibi = -slopes[:, :, None, None] * jnp.abs(rel).astype(f32)
    sink = jnp.broadcast_to(sinks.astype(f32).reshape(C_KV_HEADS, C_GROUP)[None, :, :, None, None],
                            (B, C_KV_HEADS, C_GROUP, C_BLOCK, 1))
    scale = C_HEAD_DIM ** -0.5

    def block(args):
        qn, kn, vn, n = args
        s = jnp.einsum('bqhgd,bkhd->bhgqk', qn, kn).astype(f32) * scale + alibi
        key_pos = (n - 1) * C_BLOCK + kj
        valid = in_win & (key_pos >= 0) & (key_pos < L)
        s = jnp.where(valid, s, -jnp.inf)
        p = jax.nn.softmax(jnp.concatenate([s, sink], axis=-1), axis=-1)[..., :-1]
        return jnp.einsum('bhgqk,bkhd->bqhgd', p.astype(vn.dtype), vn)

    o = lax.map(block, (jnp.moveaxis(q, 1, 0), jnp.moveaxis(kb, 1, 0), jnp.moveaxis(vb, 1, 0), jnp.arange(nb)))
    o = jnp.moveaxis(o, 0, 1).reshape(B, L, C_HEADS * C_HEAD_DIM)
    return o @ w_out


def hier_moe(h, w_group, w_expert, w_gate, w_up, w_down):
    B, L, Dm = h.shape
    t = h.reshape(B * L, Dm)
    T = t.shape[0]
    g_logit = (t @ w_group).astype(f32)
    g_prob = jax.nn.softmax(g_logit, axis=-1)
    _, g_sel = lax.top_k(g_logit, 1)
    g_sel = g_sel[:, 0]
    p_sel = jnp.take_along_axis(g_prob, g_sel[:, None], axis=-1)[:, 0]
    e_logit = (t @ w_expert).astype(f32).reshape(T, N_GROUPS, EXPERTS_PER_GROUP)
    e_sel = jnp.take_along_axis(e_logit, g_sel[:, None, None], axis=1)[:, 0]
    top_v, top_i = lax.top_k(e_sel, TOP_K_INNER)
    top_w = jax.nn.softmax(top_v, axis=-1)
    inner = jnp.sum(top_w[..., None] * jax.nn.one_hot(top_i, EXPERTS_PER_GROUP, dtype=f32), axis=1)
    comb = (p_sel[:, None, None] * jax.nn.one_hot(g_sel, N_GROUPS, dtype=f32)[:, :, None] * inner[:, None, :]).astype(t.dtype)
    out = jnp.zeros_like(t)
    for g in range(N_GROUPS):
        hg = jax.nn.silu(jnp.einsum('td,edf->tef', t, w_gate[g])) * jnp.einsum('td,edf->tef', t, w_up[g])
        out = out + jnp.einsum('tef,efd->td', hg * comb[:, g, :, None], w_down[g])
    return out.reshape(B, L, Dm)


def trunk(x, norm_mix, norm_ffn, norm_final,
          ev_w_in, ev_conv_a, ev_alog_f, ev_alog_b, ev_dtb_f, ev_dtb_b, ev_onorm, ev_conv_b,
          hy_w1, hy_b1, hy_w2, hy_b2, hy_w3, hy_b3, hy_w4, hy_freq, hy_dbias, ev_w_out,
          od_w_qkv, od_sinks, od_w_out,
          moe_w_group, moe_w_expert, moe_w_gate, moe_w_up, moe_w_down):
    for layer in range(DEPTH):
        i = layer // 2
        h = rmsnorm(x, norm_mix[layer])
        if layer % 2 == 0:
            x = x + even_mixer(h, ev_w_in[i], ev_conv_a[i], ev_alog_f[i], ev_alog_b[i], ev_dtb_f[i], ev_dtb_b[i],
                               ev_onorm[i], ev_conv_b[i], hy_w1[i], hy_b1[i], hy_w2[i], hy_b2[i], hy_w3[i], hy_b3[i],
                               hy_w4[i], hy_freq[i], hy_dbias[i], ev_w_out[i])
        else:
            x = x + windowed_gqa(h, od_w_qkv[i], od_sinks[i], od_w_out[i])
        h = rmsnorm(x, norm_ffn[layer])
        x = x + hier_moe(h, moe_w_group[layer], moe_w_expert[layer], moe_w_gate[layer], moe_w_up[layer], moe_w_down[layer])
    return rmsnorm(x, norm_final)


def setup_inputs(seed: int = 0) -> dict:
    key = jax.random.key(seed)
    ks = iter(jax.random.split(key, 40))

    def nrm(shape, scale):
        return jax.random.normal(next(ks), shape, f32) * scale

    def gain(shape):
        return 1.0 + 0.02 * jax.random.normal(next(ks), shape, f32)

    def alog(shape):
        return jnp.log(jax.random.uniform(next(ks), shape, f32, minval=1.0, maxval=16.0))

    def dtb(shape):
        u = jax.random.uniform(next(ks), shape, f32)
        dt = jnp.exp(u * (math.log(0.1) - math.log(1e-3)) + math.log(1e-3))
        return dt + jnp.log(-jnp.expm1(-dt))

    D = D_MODEL
    return {
        "x_prompt": nrm((BATCH, SEQ, D), 1.0),
        "x_sample": nrm((DEC_BATCH, DEC_SEQ, D), 1.0),
        "norm_mix": gain((DEPTH, D)),
        "norm_ffn": gain((DEPTH, D)),
        "norm_final": gain((D,)),
        "ev_w_in": nrm((N_EVEN, D, EVEN_IN), D ** -0.5),
        "ev_conv_a": nrm((N_EVEN, A_QKV, SHORT_CONV), SHORT_CONV ** -0.5),
        "ev_alog_f": alog((N_EVEN, A_HEADS)),
        "ev_alog_b": alog((N_EVEN, A_HEADS)),
        "ev_dtb_f": dtb((N_EVEN, A_HEADS)),
        "ev_dtb_b": dtb((N_EVEN, A_HEADS)),
        "ev_onorm": gain((N_EVEN, A_DV)),
        "ev_conv_b": nrm((N_EVEN, 3 * B_WIDTH, SHORT_CONV), SHORT_CONV ** -0.5),
        "hy_w1": nrm((N_EVEN, HY_EMB, HY_FFN), HY_EMB ** -0.5),
        "hy_b1": nrm((N_EVEN, HY_FFN), 0.02),
        "hy_w2": nrm((N_EVEN, HY_FFN, HY_FFN), HY_FFN ** -0.5),
        "hy_b2": nrm((N_EVEN, HY_FFN), 0.02),
        "hy_w3": nrm((N_EVEN, HY_FFN, HY_FFN), HY_FFN ** -0.5),
        "hy_b3": nrm((N_EVEN, HY_FFN), 0.02),
        "hy_w4": nrm((N_EVEN, HY_FFN, HY_ORDER * 2 * B_WIDTH), 0.05 * HY_FFN ** -0.5),
        "hy_freq": gain((N_EVEN, 3, HY_FFN)),
        "hy_dbias": nrm((N_EVEN, HY_ORDER, B_WIDTH), 1.0),
        "ev_w_out": nrm((N_EVEN, A_WIDTH + B_WIDTH, D), (A_WIDTH + B_WIDTH) ** -0.5),
        "od_w_qkv": nrm((N_ODD, D, ODD_IN), D ** -0.5),
        "od_sinks": nrm((N_ODD, C_HEADS), 0.5),
        "od_w_out": nrm((N_ODD, C_HEADS * C_HEAD_DIM, D), (C_HEADS * C_HEAD_DIM) ** -0.5),
        "moe_w_group": nrm((DEPTH, D, N_GROUPS), D ** -0.5),
        "moe_w_expert": nrm((DEPTH, D, N_GROUPS * EXPERTS_PER_GROUP), D ** -0.5),
        "moe_w_gate": nrm((DEPTH, N_GROUPS, EXPERTS_PER_GROUP, D, D_EXPERT), D ** -0.5),
        "moe_w_up": nrm((DEPTH, N_GROUPS, EXPERTS_PER_GROUP, D, D_EXPERT), D ** -0.5),
        "moe_w_down": nrm((DEPTH, N_GROUPS, EXPERTS_PER_GROUP, D_EXPERT, D), D_EXPERT ** -0.5),
    }


def reference(x_prompt, x_sample, norm_mix, norm_ffn, norm_final,
              ev_w_in, ev_conv_a, ev_alog_f, ev_alog_b, ev_dtb_f, ev_dtb_b, ev_onorm, ev_conv_b,
              hy_w1, hy_b1, hy_w2, hy_b2, hy_w3, hy_b3, hy_w4, hy_freq, hy_dbias, ev_w_out,
              od_w_qkv, od_sinks, od_w_out,
              moe_w_group, moe_w_expert, moe_w_gate, moe_w_up, moe_w_down):
    params = (norm_mix, norm_ffn, norm_final,
              ev_w_in, ev_conv_a, ev_alog_f, ev_alog_b, ev_dtb_f, ev_dtb_b, ev_onorm, ev_conv_b,
              hy_w1, hy_b1, hy_w2, hy_b2, hy_w3, hy_b3, hy_w4, hy_freq, hy_dbias, ev_w_out,
              od_w_qkv, od_sinks, od_w_out,
              moe_w_group, moe_w_expert, moe_w_gate, moe_w_up, moe_w_down)
    y_prompt = trunk(x_prompt, *params)
    y_sample = trunk(x_sample, *params)
    return (y_prompt, y_sample)
```

```python
import functools
import math

import jax
import jax.numpy as jnp
from jax import lax
from jax.experimental import pallas as pl
from jax.experimental.pallas import tpu as pltpu

f32 = jnp.float32
bf16 = jnp.bfloat16

EPS = 1e-6
D_MODEL = 1024

A_HEADS = 4
A_DK = 128
A_DV = 128
A_WIDTH = A_HEADS * A_DV
A_QKV = 2 * A_HEADS * A_DK + A_HEADS * A_DV
DELTA_CHUNK = 128
INV_BASE = 8

B_WIDTH = D_MODEL - A_WIDTH
HY_ORDER = 2
HY_EMB = 33
HY_BANDS = (HY_EMB - 1) // 2
HY_TARGET = 1e-2
HY_MIN_DECAY = math.log(HY_TARGET) / 1.5
HY_MAX_DECAY = math.log(HY_TARGET) / 0.3
HY_CT = 256

C_HEADS = 16
C_KV_HEADS = 4
C_HEAD_DIM = 64
C_GROUP = C_HEADS // C_KV_HEADS
WINDOW = 128
C_BLOCK = 128

N_GROUPS = 4
EXPERTS_PER_GROUP = 8
N_EXPERTS = N_GROUPS * EXPERTS_PER_GROUP
D_EXPERT = 256
ROUTER_LANES = 128

LANE = 128
VMEM_LIMIT = 56 * 1024 * 1024


def _cparams(*sem):
    return pltpu.CompilerParams(dimension_semantics=sem, vmem_limit_bytes=VMEM_LIMIT)


def _dot(a, b):
    return jnp.dot(a, b, preferred_element_type=f32)


def _dot_nt(a, b):
    return lax.dot_general(a, b, (((1,), (1,)), ((), ())), preferred_element_type=f32)


def _dot_tn(a, b):
    return lax.dot_general(a, b, (((0,), (0,)), ((), ())), preferred_element_type=f32)


def _sigmoid(x):
    return 1.0 / (1.0 + jnp.exp(-x))


def _rms(x, g):
    return x * lax.rsqrt(jnp.mean(x * x, axis=-1, keepdims=True) + EPS) * g


def _norm_matmul_kernel(x_ref, g_ref, w_ref, *o_refs):
    h = _rms(x_ref[...], g_ref[...]).astype(bf16)
    off = 0
    for o_ref in o_refs:
        n = o_ref.shape[-1]
        o_ref[...] = _dot(h, w_ref[:, off:off + n]).astype(o_ref.dtype)
        off += n


def _norm_matmul(x, gain, w, splits, dtypes, tm=512):
    T, D = x.shape
    n_all = w.shape[1]
    return pl.pallas_call(
        _norm_matmul_kernel,
        grid=(T // tm,),
        in_specs=[pl.BlockSpec((tm, D), lambda i: (i, 0)),
                  pl.BlockSpec((1, D), lambda i: (0, 0)),
                  pl.BlockSpec((D, n_all), lambda i: (0, 0))],
        out_specs=[pl.BlockSpec((tm, n), lambda i: (i, 0)) for n in splits],
        out_shape=[jax.ShapeDtypeStruct((T, n), dt) for n, dt in zip(splits, dtypes)],
        compiler_params=_cparams("parallel"),
    )(x, gain.reshape(1, D), w)


def _tri_inverse(a, ci, si):
    c = a.shape[0]
    eye = jnp.where(ci == si, 1.0, 0.0)
    ab = jnp.where(ci // INV_BASE == si // INV_BASE, a, 0.0)
    t = eye - ab
    p = ab
    n = 2
    while n < INV_BASE:
        p16 = p.astype(bf16)
        p = _dot(p16, p16)
        t = t + _dot(t.astype(bf16), p.astype(bf16))
        n *= 2
    b = INV_BASE
    while b < c:
        r = jnp.where((ci // (2 * b) == si // (2 * b)) & (ci // b != si // b), a, 0.0)
        t16 = t.astype(bf16)
        t = t - _dot(t16, _dot(r.astype(bf16), t16).astype(bf16))
        b *= 2
    return t


def _deltanet_kernel(sc_ref, x_ref, ab_ref, cw_ref, on_ref, o_ref,
                     q_s, k_s, v_s, gt_s, tt_s, of_s, ob_s):
    hd = pl.program_id(1)
    L = x_ref.shape[1]
    C = DELTA_CHUNK
    nchunk = L // C

    xin = x_ref[0, :, 0:3 * A_DK].astype(f32)
    row = lax.broadcasted_iota(jnp.int32, (L, 1), 0)
    prev = jnp.where(row == 0, 0.0, pltpu.roll(xin, 1, 0))
    nxt = jnp.where(row == L - 1, 0.0, pltpu.roll(xin, L - 1, 0))
    cw = cw_ref[0]
    y = prev * cw[0:1, :] + xin * cw[1:2, :] + nxt * cw[2:3, :]
    y = y * _sigmoid(y)
    q = y[:, 0:A_DK]
    k = y[:, A_DK:2 * A_DK]
    q_s[...] = q * lax.rsqrt(jnp.sum(q * q, axis=-1, keepdims=True) + EPS) * (A_DK ** -0.5)
    k_s[...] = k * lax.rsqrt(jnp.sum(k * k, axis=-1, keepdims=True) + EPS)
    v_s[...] = y[:, 2 * A_DK:3 * A_DK]

    ab = ab_ref[0]
    lane = lax.broadcasted_iota(jnp.int32, (1, LANE), 1)
    alog = jnp.where(lane == 2, sc_ref[hd, 0], jnp.where(lane == 3, sc_ref[hd, 1], 0.0))
    dtb = jnp.where(lane == 2, sc_ref[hd, 2], jnp.where(lane == 3, sc_ref[hd, 3], 0.0))
    z = ab + dtb
    softplus = jnp.maximum(z, 0.0) + jnp.log(1.0 + jnp.exp(-jnp.abs(z)))
    g = jnp.where((lane == 2) | (lane == 3), -jnp.exp(alog) * softplus, 0.0)
    rin = row % C
    pre = g
    suf = g
    s = 1
    while s < C:
        pre = pre + jnp.where(rin >= s, pltpu.roll(pre, s, 0), 0.0)
        suf = suf + jnp.where(rin < C - s, pltpu.roll(suf, L - s, 0), 0.0)
        s *= 2
    gc = jnp.where(lane == 2, pre, suf)
    gt_s[...] = jnp.where(lane < 2, _sigmoid(ab), gc)
    tt_s[...] = pre + suf - g

    ci = lax.broadcasted_iota(jnp.int32, (C, C), 0)
    si = lax.broadcasted_iota(jnp.int32, (C, C), 1)

    def chunk_step(r0, rev, S):
        kc = k_s[pl.ds(r0, C), :]
        qc = q_s[pl.ds(r0, C), :]
        vc = v_s[pl.ds(r0, C), :]
        gates = gt_s[pl.ds(r0, C), :]
        tots = tt_s[pl.ds(r0, C), :]
        beta = gates[:, rev:rev + 1]
        gcc = gates[:, 2 + rev:3 + rev]
        tot = tots[:, 2 + rev:3 + rev]
        k16 = kc.astype(bf16)
        kk = _dot_nt(k16, k16)
        qk = _dot_nt(qc.astype(bf16), k16)
        bc = jnp.broadcast_to(gcc, (C, C))
        dif = bc - bc.T
        keep = (si >= ci) if rev else (si <= ci)
        strict = (si > ci) if rev else (si < ci)
        decay = jnp.where(keep, jnp.exp(jnp.where(keep, dif, 0.0)), 0.0)
        a = jnp.where(strict, beta * kk * decay, 0.0)
        t = _tri_inverse(a, ci, si)
        eg = jnp.exp(gcc)
        rhs = jnp.concatenate([vc * beta, kc * (beta * eg)], axis=1).astype(bf16)
        uw = _dot(t.astype(bf16), rhs)
        u = uw[:, 0:A_DV]
        w = uw[:, A_DV:]
        S16 = S.astype(bf16)
        v_new = u - _dot(w.astype(bf16), S16)
        vn16 = v_new.astype(bf16)
        o = _dot((qc * eg).astype(bf16), S16) + _dot((qk * decay).astype(bf16), vn16)
        kd = kc * jnp.exp(tot - gcc)
        S = S * jnp.exp(tot[0:1, :]) + _dot_tn(kd.astype(bf16), vn16)
        return o, S

    def body(i, carry):
        Sf, Sb = carry
        rf = pl.multiple_of(i * C, C)
        rb = pl.multiple_of((nchunk - 1 - i) * C, C)
        of, Sf = chunk_step(rf, 0, Sf)
        ob, Sb = chunk_step(rb, 1, Sb)
        of_s[pl.ds(rf, C), :] = of
        ob_s[pl.ds(rb, C), :] = ob
        return Sf, Sb

    S0 = jnp.zeros((A_DK, A_DV), f32)
    lax.fori_loop(0, nchunk, body, (S0, S0))

    o = of_s[...] + ob_s[...]
    zg = x_ref[0, :, 3 * A_DK:4 * A_DK].astype(f32)
    o_ref[0] = (_rms(o, on_ref[...]) * (zg * _sigmoid(zg))).astype(o_ref.dtype)


def _deltanet(qkvz, abh, conv_w, gate_sc, onorm):
    B, L, _ = qkvz.shape
    hw = 4 * A_DK
    scratch = [pltpu.VMEM((L, A_DK), f32) for _ in range(3)]
    scratch += [pltpu.VMEM((L, LANE), f32) for _ in range(2)]
    scratch += [pltpu.VMEM((L, A_DV), f32) for _ in range(2)]
    return pl.pallas_call(
        _deltanet_kernel,
        grid=(B, A_HEADS),
        in_specs=[pl.BlockSpec(memory_space=pltpu.SMEM),
                  pl.BlockSpec((1, L, hw), lambda b, h: (b, 0, h)),
                  pl.BlockSpec((1, L, LANE), lambda b, h: (b, 0, h)),
                  pl.BlockSpec((1, 8, 3 * A_DK), lambda b, h: (h, 0, 0)),
                  pl.BlockSpec((1, A_DV), lambda b, h: (0, 0))],
        out_specs=pl.BlockSpec((1, L, A_DV), lambda b, h: (b, 0, h)),
        out_shape=jax.ShapeDtypeStruct((B, L, A_WIDTH), bf16),
        scratch_shapes=scratch,
        compiler_params=_cparams("parallel", "parallel"),
    )(gate_sc, qkvz, abh, conv_w, onorm.reshape(1, A_DV))


def _filter_spectrum_kernel(c_ref, s_ref, hs_ref, hd_ref, kr_ref, ki_ref, kn_ref):
    L = c_ref.shape[0]
    n_fft = 2 * L
    hs = hs_ref[...]
    hd = hd_ref[...]

    def split_dot(m_ref, x):
        hi = x.astype(bf16)
        lo = (x - hi.astype(f32)).astype(bf16)
        m = m_ref[...]
        return _dot(m, hi) + _dot(m, lo)

    row = lax.broadcasted_iota(jnp.int32, (L, 1), 0)
    scale = jnp.where(row == 0, 1.0 / n_fft, 2.0 / n_fft)
    kr_ref[...] = split_dot(c_ref, hs) * scale
    ki_ref[...] = split_dot(s_ref, hd) * scale
    alt = jnp.where(row % 2 == 0, 1.0, -1.0)
    kn = jnp.sum(hs * alt, axis=0, keepdims=True) * (1.0 / n_fft)
    kn_ref[...] = jnp.broadcast_to(kn, kn_ref.shape)


def _filter_spectrum(cmat, smat, hsum, hdiff):
    L, n = hsum.shape
    ct = HY_CT
    return pl.pallas_call(
        _filter_spectrum_kernel,
        grid=(n // ct,),
        in_specs=[pl.BlockSpec((L, L), lambda j: (0, 0)),
                  pl.BlockSpec((L, L), lambda j: (0, 0)),
                  pl.BlockSpec((L, ct), lambda j: (0, j)),
                  pl.BlockSpec((L, ct), lambda j: (0, j))],
        out_specs=[pl.BlockSpec((L, ct), lambda j: (0, j)),
                   pl.BlockSpec((L, ct), lambda j: (0, j)),
                   pl.BlockSpec((8, ct), lambda j: (0, j))],
        out_shape=[jax.ShapeDtypeStruct((L, n), f32),
                   jax.ShapeDtypeStruct((L, n), f32),
                   jax.ShapeDtypeStruct((8, n), f32)],
        compiler_params=_cparams("parallel"),
    )(cmat, smat, hsum, hdiff)


def _hyena_kernel(c_ref, s_ref, x1_ref, x2_ref, v_ref, w1_ref, w2_ref, wv_ref,
                  kr0_ref, ki0_ref, kn0_ref, kr1_ref, ki1_ref, kn1_ref, db_ref, o_ref):
    L = c_ref.shape[0]
    row = lax.broadcasted_iota(jnp.int32, (L, 1), 0)
    alt = jnp.where(row % 2 == 0, 1.0, -1.0)

    def dwconv(x_ref, w_ref):
        x = x_ref[0].astype(f32)
        w = w_ref[...]
        prev = jnp.where(row == 0, 0.0, pltpu.roll(x, 1, 0))
        nxt = jnp.where(row == L - 1, 0.0, pltpu.roll(x, L - 1, 0))
        return prev * w[0:1, :] + x * w[1:2, :] + nxt * w[2:3, :]

    zz = dwconv(v_ref, wv_ref)
    stages = ((x1_ref, w1_ref, kr0_ref, ki0_ref, kn0_ref), (x2_ref, w2_ref, kr1_ref, ki1_ref, kn1_ref))
    for o, (g_ref, gw_ref, kr_ref, ki_ref, kn_ref) in enumerate(stages):
        z16 = zz.astype(bf16)
        ur = _dot(c_ref[...], z16)
        ui = _dot(s_ref[...], z16)
        unyq = jnp.sum(zz * alt, axis=0, keepdims=True)
        kr = kr_ref[...]
        ki = ki_ref[...]
        yr = (ur * kr - ui * ki).astype(bf16)
        yi = (ur * ki + ui * kr).astype(bf16)
        y = _dot(c_ref[...], yr) + _dot(s_ref[...], yi) + alt * (unyq * kn_ref[0:1, :])
        zz = dwconv(g_ref, gw_ref) * (y + zz * db_ref[o:o + 1, :])
    o_ref[0] = zz.astype(o_ref.dtype)


def _hyena(hx, conv_w, cmat, smat, kr, ki, kn, dbias):
    B, L, _ = hx.shape
    ct = HY_CT
    nct = B_WIDTH // ct
    once = pl.Buffered(1)

    def xspec(part):
        return pl.BlockSpec((1, L, ct), lambda c, b: (b, 0, part * nct + c))

    def wspec(part):
        return pl.BlockSpec((8, ct), lambda c, b: (0, part * nct + c))

    def kspec(order, rows):
        return pl.BlockSpec((rows, ct), lambda c, b: (0, order * nct + c), pipeline_mode=once)

    return pl.pallas_call(
        _hyena_kernel,
        grid=(nct, B),
        in_specs=[pl.BlockSpec((L, L), lambda c, b: (0, 0), pipeline_mode=once),
                  pl.BlockSpec((L, L), lambda c, b: (0, 0), pipeline_mode=once),
                  xspec(0), xspec(1), xspec(2), wspec(0), wspec(1), wspec(2),
                  kspec(0, L), kspec(0, L), kspec(0, 8), kspec(1, L), kspec(1, L), kspec(1, 8),
                  pl.BlockSpec((8, ct), lambda c, b: (0, c))],
        out_specs=pl.BlockSpec((1, L, ct), lambda c, b: (b, 0, c)),
        out_shape=jax.ShapeDtypeStruct((B, L, B_WIDTH), bf16),
        compiler_params=_cparams("parallel", "parallel"),
    )(cmat, smat, hx, hx, hx, conv_w, conv_w, conv_w, kr, ki, kn, kr, ki, kn, dbias)


def _route(logits):
    lane = lax.broadcasted_iota(jnp.int32, logits.shape, 1)
    neg = -jnp.inf
    big = ROUTER_LANES
    gl = jnp.where(lane < N_GROUPS, logits, neg)
    gmax = jnp.max(gl, axis=-1, keepdims=True)
    gsel = jnp.min(jnp.where(gl == gmax, lane, big), axis=-1, keepdims=True)
    psel = 1.0 / jnp.sum(jnp.exp(gl - gmax), axis=-1, keepdims=True)
    first = N_GROUPS + gsel * EXPERTS_PER_GROUP
    el = jnp.where((lane >= first) & (lane < first + EXPERTS_PER_GROUP), logits, neg)
    t1 = jnp.max(el, axis=-1, keepdims=True)
    i1 = jnp.min(jnp.where(el == t1, lane, big), axis=-1, keepdims=True)
    el2 = jnp.where(lane == i1, neg, el)
    t2 = jnp.max(el2, axis=-1, keepdims=True)
    i2 = jnp.min(jnp.where(el2 == t2, lane, big), axis=-1, keepdims=True)
    e2 = jnp.exp(t2 - t1)
    w1 = 1.0 / (1.0 + e2)
    w2 = e2 * w1
    return psel * (jnp.where(lane == i1, w1, 0.0) + jnp.where(lane == i2, w2, 0.0))


def _proj_route_kernel(*refs, n_in):
    x_ref = refs[0]
    a_refs = refs[1:1 + n_in]
    w_refs = refs[1 + n_in:1 + 2 * n_in]
    g_ref, wr_ref, xo_ref, h_ref, comb_ref = refs[1 + 2 * n_in:]
    x = x_ref[...]
    for a_ref, w_ref in zip(a_refs, w_refs):
        x = x + _dot(a_ref[...], w_ref[...])
    xo_ref[...] = x
    h = _rms(x, g_ref[...])
    h_ref[...] = h.astype(h_ref.dtype)
    logits = jnp.dot(h, wr_ref[...], preferred_element_type=f32, precision=lax.Precision.HIGHEST)
    comb_ref[...] = _route(logits)


def _proj_route(x, acts, ws, gain, w_router, tm=512):
    T, D = x.shape
    n_in = len(acts)
    in_specs = [pl.BlockSpec((tm, D), lambda i: (i, 0))]
    in_specs += [pl.BlockSpec((tm, a.shape[1]), lambda i: (i, 0)) for a in acts]
    in_specs += [pl.BlockSpec(w.shape, lambda i: (0, 0)) for w in ws]
    in_specs += [pl.BlockSpec((1, D), lambda i: (0, 0)),
                 pl.BlockSpec((D, ROUTER_LANES), lambda i: (0, 0))]
    return pl.pallas_call(
        functools.partial(_proj_route_kernel, n_in=n_in),
        grid=(T // tm,),
        in_specs=in_specs,
        out_specs=[pl.BlockSpec((tm, D), lambda i: (i, 0)),
                   pl.BlockSpec((tm, D), lambda i: (i, 0)),
                   pl.BlockSpec((tm, ROUTER_LANES), lambda i: (i, 0))],
        out_shape=[jax.ShapeDtypeStruct((T, D), f32),
                   jax.ShapeDtypeStruct((T, D), bf16),
                   jax.ShapeDtypeStruct((T, ROUTER_LANES), f32)],
        compiler_params=_cparams("parallel"),
    )(x, *acts, *ws, gain.reshape(1, D), w_router)


def _moe_dense_kernel(x_ref, h_ref, comb_ref, wg_ref, wu_ref, wd_ref, gf_ref, o_ref, *, final_norm):
    e = pl.program_id(1)

    @pl.when(e == 0)
    def _():
        o_ref[...] = x_ref[...]

    h = h_ref[...]
    gate = _dot(h, wg_ref[0])
    up = _dot(h, wu_ref[0])
    lane = lax.broadcasted_iota(jnp.int32, (1, ROUTER_LANES), 1)
    cw = jnp.sum(jnp.where(lane == N_GROUPS + e, comb_ref[...], 0.0), axis=-1, keepdims=True)
    hg = (gate * _sigmoid(gate) * up * cw).astype(bf16)
    o_ref[...] += _dot(hg, wd_ref[0])

    if final_norm:
        @pl.when(e == pl.num_programs(1) - 1)
        def _():
            o_ref[...] = _rms(o_ref[...], gf_ref[...])


def _moe_dense(x, h, comb, wg, wu, wd, gain_final, final_norm, tm=1024):
    T, D = x.shape
    tm = min(tm, T)
    return pl.pallas_call(
        functools.partial(_moe_dense_kernel, final_norm=final_norm),
        grid=(T // tm, N_EXPERTS),
        in_specs=[pl.BlockSpec((tm, D), lambda i, e: (i, 0)),
                  pl.BlockSpec((tm, D), lambda i, e: (i, 0)),
                  pl.BlockSpec((tm, ROUTER_LANES), lambda i, e: (i, 0)),
                  pl.BlockSpec((1, D, D_EXPERT), lambda i, e: (e, 0, 0)),
                  pl.BlockSpec((1, D, D_EXPERT), lambda i, e: (e, 0, 0)),
                  pl.BlockSpec((1, D_EXPERT, D), lambda i, e: (e, 0, 0)),
                  pl.BlockSpec((1, D), lambda i, e: (0, 0))],
        out_specs=pl.BlockSpec((tm, D), lambda i, e: (i, 0)),
        out_shape=jax.ShapeDtypeStruct((T, D), f32),
        compiler_params=_cparams("parallel", "arbitrary"),
    )(x, h, comb, wg, wu, wd, gain_final.reshape(1, D))


def _swa_kernel(sink_ref, q_ref, kp_ref, kc_ref, kn_ref, vp_ref, vc_ref, vn_ref, o_ref, *, seq_len):
    n = pl.program_id(1)
    blk = C_BLOCK
    qi = lax.broadcasted_iota(jnp.int32, (blk, 3 * blk), 0)
    kj = lax.broadcasted_iota(jnp.int32, (blk, 3 * blk), 1)
    rel = blk + qi - kj
    arel = jnp.abs(rel)
    key_pos = (n - 1) * blk + kj
    valid = (arel <= WINDOW) & (key_pos >= 0) & (key_pos < seq_len)
    dist = arel.astype(f32)
    scale = C_HEAD_DIM ** -0.5
    kcat = jnp.concatenate([kp_ref[0], kc_ref[0], kn_ref[0]], axis=0)
    vcat = jnp.concatenate([vp_ref[0], vc_ref[0], vn_ref[0]], axis=0)
    q = q_ref[0]
    outs = []
    for hh in range(C_HEADS):
        kvh = hh // C_GROUP
        qh = q[:, hh * C_HEAD_DIM:(hh + 1) * C_HEAD_DIM]
        kh = kcat[:, kvh * C_HEAD_DIM:(kvh + 1) * C_HEAD_DIM]
        vh = vcat[:, kvh * C_HEAD_DIM:(kvh + 1) * C_HEAD_DIM]
        slope = 2.0 ** (-8.0 * (hh + 1) / C_HEADS)
        s = _dot_nt(qh, kh) * scale - slope * dist
        s = jnp.where(valid, s, -jnp.inf)
        sink = sink_ref[hh]
        m = jnp.maximum(jnp.max(s, axis=-1, keepdims=True), sink)
        p = jnp.exp(s - m)
        den = jnp.sum(p, axis=-1, keepdims=True) + jnp.exp(sink - m)
        outs.append(_dot((p / den).astype(bf16), vh))
    o_ref[0] = jnp.concatenate(outs, axis=1).astype(o_ref.dtype)


def _swa(qkv, sinks):
    B, L, _ = qkv.shape
    nb = L // C_BLOCK
    dq = C_HEADS * C_HEAD_DIM
    dkv = C_KV_HEADS * C_HEAD_DIM
    k_col = dq // dkv
    v_col = k_col + 1

    def kv_spec(col, shift):
        return pl.BlockSpec((1, C_BLOCK, dkv),
                            lambda b, n: (b, jnp.clip(n + shift, 0, nb - 1), col))

    return pl.pallas_call(
        functools.partial(_swa_kernel, seq_len=L),
        grid=(B, nb),
        in_specs=[pl.BlockSpec(memory_space=pltpu.SMEM),
                  pl.BlockSpec((1, C_BLOCK, dq), lambda b, n: (b, n, 0)),
                  kv_spec(k_col, -1), kv_spec(k_col, 0), kv_spec(k_col, 1),
                  kv_spec(v_col, -1), kv_spec(v_col, 0), kv_spec(v_col, 1)],
        out_specs=pl.BlockSpec((1, C_BLOCK, dq), lambda b, n: (b, n, 0)),
        out_shape=jax.ShapeDtypeStruct((B, L, dq), bf16),
        compiler_params=_cparams("parallel", "parallel"),
    )(sinks, qkv, qkv, qkv, qkv, qkv, qkv, qkv)


def _hyena_filters(L, w1, b1, w2, b2, w3, b3, w4, freq):
    hp = lax.Precision.HIGHEST
    t = jnp.linspace(0.0, 1.0, L, dtype=f32)[:, None]
    wpos = (2.0 * math.pi / L) * jnp.arange(L, dtype=f32)[:, None]
    bands = jnp.linspace(1e-4, HY_BANDS - 1, HY_BANDS, dtype=f32)[None, :]
    feats = jnp.concatenate([t, jnp.cos(bands * wpos), -jnp.sin(bands * wpos)], axis=-1)
    h = jnp.sin(freq[0] * (jnp.dot(feats, w1, precision=hp) + b1))
    h = jnp.sin(freq[1] * (jnp.dot(h, w2, precision=hp) + b2))
    h = jnp.sin(freq[2] * (jnp.dot(h, w3, precision=hp) + b3))
    h = jnp.dot(h, w4, precision=hp).reshape(L, HY_ORDER, 2, B_WIDTH)
    deltas = jnp.abs(jnp.linspace(HY_MIN_DECAY, HY_MAX_DECAY, B_WIDTH, dtype=f32))
    window = jnp.exp(-t * deltas[None, :])
    return h * window[:, None, None, :]


def _dft_tables(L):
    n_fft = 2 * L
    idx = jnp.arange(L, dtype=jnp.int32)
    ph = (idx[:, None] * idx[None, :]) % n_fft
    ang = ph.astype(f32) * (2.0 * math.pi / n_fft)
    return jnp.cos(ang).astype(bf16), jnp.sin(ang).astype(bf16)


def _pad_rows(w, rows=8):
    return jnp.pad(w, ((0, rows - w.shape[0]), (0, 0)))


def _even_in_weights(w_in):
    hk = A_HEADS * A_DK
    q = w_in[:, 0:hk].reshape(D_MODEL, A_HEADS, A_DK)
    k = w_in[:, hk:2 * hk].reshape(D_MODEL, A_HEADS, A_DK)
    v = w_in[:, 2 * hk:A_QKV].reshape(D_MODEL, A_HEADS, A_DV)
    z = w_in[:, A_QKV:A_QKV + A_WIDTH].reshape(D_MODEL, A_HEADS, A_DV)
    qkvz = jnp.concatenate([q, k, v, z], axis=-1).reshape(D_MODEL, A_HEADS * 4 * A_DK)
    ab = w_in[:, A_QKV + A_WIDTH:A_QKV + A_WIDTH + 4 * A_HEADS].reshape(D_MODEL, 4, A_HEADS)
    ab = jnp.transpose(ab, (0, 2, 1))
    ab = jnp.pad(ab, ((0, 0), (0, 0), (0, LANE - 4))).reshape(D_MODEL, A_HEADS * LANE)
    hx = w_in[:, A_QKV + A_WIDTH + 4 * A_HEADS:]
    return jnp.concatenate([qkvz, ab, hx], axis=1).astype(bf16)


def _router_weights(w_group, w_expert):
    w = jnp.concatenate([w_group, w_expert], axis=1)
    return jnp.pad(w, ((0, 0), (0, ROUTER_LANES - w.shape[1])))


def kernel(x_prompt, x_sample, norm_mix, norm_ffn, norm_final, ev_w_in, ev_conv_a, ev_alog_f, ev_alog_b, ev_dtb_f, ev_dtb_b, ev_onorm, ev_conv_b, hy_w1, hy_b1, hy_w2, hy_b2, hy_w3, hy_b3, hy_w4, hy_freq, hy_dbias, ev_w_out, od_w_qkv, od_sinks, od_w_out, moe_w_group, moe_w_expert, moe_w_gate, moe_w_up, moe_w_down):
    bp = x_prompt.shape[0]
    L = x_prompt.shape[1]
    x = jnp.concatenate([x_prompt, x_sample], axis=0)
    B = x.shape[0]
    T = B * L
    x = x.reshape(T, D_MODEL)

    w_in = _even_in_weights(ev_w_in[0])
    n_qkvz = A_HEADS * 4 * A_DK
    n_ab = A_HEADS * LANE
    n_hx = 3 * B_WIDTH
    qkvz, abh, hx = _norm_matmul(x, norm_mix[0], w_in, (n_qkvz, n_ab, n_hx), (bf16, f32, bf16))

    hk = A_HEADS * A_DK
    ca = ev_conv_a[0]
    conv_a = jnp.concatenate([ca[0:hk].reshape(A_HEADS, A_DK, 3), ca[hk:2 * hk].reshape(A_HEADS, A_DK, 3),
                              ca[2 * hk:].reshape(A_HEADS, A_DV, 3)], axis=1)
    conv_a = jnp.pad(jnp.transpose(conv_a, (0, 2, 1)), ((0, 0), (0, 5), (0, 0)))
    gate_sc = jnp.stack([ev_alog_f[0], ev_alog_b[0], ev_dtb_f[0], ev_dtb_b[0]], axis=1)
    o_a = _deltanet(qkvz.reshape(B, L, n_qkvz), abh.reshape(B, L, n_ab), conv_a, gate_sc, ev_onorm[0])

    filt = _hyena_filters(L, hy_w1[0], hy_b1[0], hy_w2[0], hy_b2[0], hy_w3[0], hy_b3[0], hy_w4[0], hy_freq[0])
    hsum = (filt[:, :, 0] + filt[:, :, 1]).reshape(L, HY_ORDER * B_WIDTH)
    hdiff = (filt[:, :, 0] - filt[:, :, 1]).reshape(L, HY_ORDER * B_WIDTH)
    cmat, smat = _dft_tables(L)
    kr, ki, kn = _filter_spectrum(cmat, smat, hsum, hdiff)
    conv_b = _pad_rows(ev_conv_b[0].T)
    o_b = _hyena(hx.reshape(B, L, n_hx), conv_b, cmat, smat, kr, ki, kn, _pad_rows(hy_dbias[0]))

    w_out = ev_w_out[0].astype(bf16)
    x, h, comb = _proj_route(x, (o_a.reshape(T, A_WIDTH), o_b.reshape(T, B_WIDTH)),
                             (w_out[:A_WIDTH], w_out[A_WIDTH:]), norm_ffn[0],
                             _router_weights(moe_w_group[0], moe_w_expert[0]))
    x = _moe_dense(x, h, comb,
                   moe_w_gate[0].reshape(N_EXPERTS, D_MODEL, D_EXPERT).astype(bf16),
                   moe_w_up[0].reshape(N_EXPERTS, D_MODEL, D_EXPERT).astype(bf16),
                   moe_w_down[0].reshape(N_EXPERTS, D_EXPERT, D_MODEL).astype(bf16),
                   norm_final, final_norm=False)

    n_qkv = od_w_qkv.shape[-1]
    (qkv,) = _norm_matmul(x, norm_mix[1], od_w_qkv[0].astype(bf16), (n_qkv,), (bf16,))
    o_c = _swa(qkv.reshape(B, L, n_qkv), od_sinks[0])
    x, h, comb = _proj_route(x, (o_c.reshape(T, C_HEADS * C_HEAD_DIM),), (od_w_out[0].astype(bf16),),
                             norm_ffn[1], _router_weights(moe_w_group[1], moe_w_expert[1]))
    y = _moe_dense(x, h, comb,
                   moe_w_gate[1].reshape(N_EXPERTS, D_MODEL, D_EXPERT).astype(bf16),
                   moe_w_up[1].reshape(N_EXPERTS, D_MODEL, D_EXPERT).astype(bf16),
                   moe_w_down[1].reshape(N_EXPERTS, D_EXPERT, D_MODEL).astype(bf16),
                   norm_final, final_norm=True)
    y = y.reshape(B, L, D_MODEL)
    return (y[:bp], y[bp:])
```

```python
import functools
import math

import jax
import jax.numpy as jnp
import numpy as np
from jax import lax
from jax.experimental import pallas as pl
from jax.experimental.pallas import tpu as pltpu

f32 = jnp.float32
bf16 = jnp.bfloat16

EPS = 1e-6
D_MODEL = 1024

A_HEADS = 4
A_DK = 128
A_DV = 128
A_WIDTH = A_HEADS * A_DV
A_QKV = 2 * A_HEADS * A_DK + A_HEADS * A_DV
DELTA_CHUNK = 128
INV_BASE = 8

B_WIDTH = D_MODEL - A_WIDTH
HY_ORDER = 2
HY_EMB = 33
HY_BANDS = (HY_EMB - 1) // 2
HY_TARGET = 1e-2
HY_MIN_DECAY = math.log(HY_TARGET) / 1.5
HY_MAX_DECAY = math.log(HY_TARGET) / 0.3
HY_CT = 256

C_HEADS = 16
C_KV_HEADS = 4
C_HEAD_DIM = 64
C_GROUP = C_HEADS // C_KV_HEADS
WINDOW = 128
C_BLOCK = 128

N_GROUPS = 4
EXPERTS_PER_GROUP = 8
N_EXPERTS = N_GROUPS * EXPERTS_PER_GROUP
D_EXPERT = 256
ROUTER_LANES = 128

LANE = 128
VMEM_LIMIT = 56 * 1024 * 1024


def _cparams(*sem):
    return pltpu.CompilerParams(dimension_semantics=sem, vmem_limit_bytes=VMEM_LIMIT)


def _dot(a, b):
    return jnp.dot(a, b, preferred_element_type=f32)


def _dot_nt(a, b):
    return lax.dot_general(a, b, (((1,), (1,)), ((), ())), preferred_element_type=f32)


def _dot_tn(a, b):
    return lax.dot_general(a, b, (((0,), (0,)), ((), ())), preferred_element_type=f32)


def _sigmoid(x):
    return 1.0 / (1.0 + jnp.exp(-x))


def _rms(x, g):
    return x * lax.rsqrt(jnp.mean(x * x, axis=-1, keepdims=True) + EPS) * g


def _norm_matmul_kernel(x_ref, g_ref, w_ref, *o_refs):
    h = _rms(x_ref[...], g_ref[...]).astype(bf16)
    off = 0
    for o_ref in o_refs:
        n = o_ref.shape[-1]
        o_ref[...] = _dot(h, w_ref[:, off:off + n]).astype(o_ref.dtype)
        off += n


def _norm_matmul(x, gain, w, splits, dtypes, tm=512):
    T, D = x.shape
    n_all = w.shape[1]
    return pl.pallas_call(
        _norm_matmul_kernel,
        grid=(T // tm,),
        in_specs=[pl.BlockSpec((tm, D), lambda i: (i, 0)),
                  pl.BlockSpec((1, D), lambda i: (0, 0)),
                  pl.BlockSpec((D, n_all), lambda i: (0, 0))],
        out_specs=[pl.BlockSpec((tm, n), lambda i: (i, 0)) for n in splits],
        out_shape=[jax.ShapeDtypeStruct((T, n), dt) for n, dt in zip(splits, dtypes)],
        compiler_params=_cparams("parallel"),
    )(x, gain.reshape(1, D), w)


MK_LOWER, MK_UPPER, MK_BASE, MK_LEVEL0 = 0, 1, 2, 3
N_LEVELS = int(math.log2(DELTA_CHUNK // INV_BASE))
MK_EYE = MK_LEVEL0 + N_LEVELS
N_MASKS = MK_EYE + 1


def _delta_masks():
    c = np.arange(DELTA_CHUNK)[:, None]
    s = np.arange(DELTA_CHUNK)[None, :]
    m = [s <= c, s >= c, c // INV_BASE == s // INV_BASE]
    b = INV_BASE
    while b < DELTA_CHUNK:
        m.append((c // (2 * b) == s // (2 * b)) & (c // b != s // b))
        b *= 2
    m.append(c == s)
    return jnp.asarray(np.stack(m).astype(np.float32))


def _tri_inverse(a, mk_ref):
    ab = a * mk_ref[MK_BASE]
    t = mk_ref[MK_EYE] - ab
    p = ab
    n = 2
    while n < INV_BASE:
        p16 = p.astype(bf16)
        p = _dot(p16, p16)
        t = t + _dot(t.astype(bf16), p.astype(bf16))
        n *= 2
    for lvl in range(N_LEVELS):
        r16 = (a * mk_ref[MK_LEVEL0 + lvl]).astype(bf16)
        t16 = t.astype(bf16)
        t = t - _dot(t16, _dot(r16, t16).astype(bf16))
    return t


def _deltanet_kernel(sc_ref, x_ref, ab_ref, cw_ref, on_ref, mk_ref, o_ref, mp_s, nr_s, gl_s, o_s):
    hd = pl.program_id(1)
    L = x_ref.shape[1]
    C = DELTA_CHUNK
    nchunk = L // C
    nqkv = 3 * A_DK
    halo = 16

    lane = lax.broadcasted_iota(jnp.int32, (1, LANE), 1)
    alog = jnp.where(lane == 2, sc_ref[hd, 0], jnp.where(lane == 3, sc_ref[hd, 1], 0.0))
    dtb = jnp.where(lane == 2, sc_ref[hd, 2], jnp.where(lane == 3, sc_ref[hd, 3], 0.0))
    neg_a = jnp.where((lane == 2) | (lane == 3), -jnp.exp(alog), 0.0)
    rowi = lax.broadcasted_iota(jnp.int32, (C, 1), 0)
    cw = cw_ref[0]

    def prep(i, carry):
        r0 = pl.multiple_of(i * C, C)
        xc = x_ref[0, pl.ds(r0, C), 0:nqkv].astype(f32)
        rp = pl.multiple_of(jnp.maximum(r0 - halo, 0), halo)
        rn = pl.multiple_of(jnp.minimum(r0 + C, L - halo), halo)
        prow = x_ref[0, pl.ds(rp, halo), 0:nqkv][halo - 1:halo, :].astype(f32)
        nrow = x_ref[0, pl.ds(rn, halo), 0:nqkv][0:1, :].astype(f32)
        prow = jnp.where(i > 0, prow, 0.0)
        nrow = jnp.where(i < nchunk - 1, nrow, 0.0)
        prev = jnp.where(rowi == 0, prow, pltpu.roll(xc, 1, 0))
        nxt = jnp.where(rowi == C - 1, nrow, pltpu.roll(xc, C - 1, 0))
        y = prev * cw[0:1, :] + xc * cw[1:2, :] + nxt * cw[2:3, :]
        y = y * _sigmoid(y)
        q = y[:, 0:A_DK]
        k = y[:, A_DK:2 * A_DK]
        vc = y[:, 2 * A_DK:nqkv]
        qc = q * lax.rsqrt(jnp.sum(q * q, axis=-1, keepdims=True) + EPS) * (A_DK ** -0.5)
        kc = k * lax.rsqrt(jnp.sum(k * k, axis=-1, keepdims=True) + EPS)

        abc = ab_ref[0, pl.ds(r0, C), :]
        zg = abc + dtb
        g = neg_a * (jnp.maximum(zg, 0.0) + jnp.log(1.0 + jnp.exp(-jnp.abs(zg))))
        low16 = mk_ref[MK_LOWER].astype(bf16)
        g1 = g.astype(bf16)
        e1 = g - g1.astype(f32)
        g2 = e1.astype(bf16)
        g3 = (e1 - g2.astype(f32)).astype(bf16)
        pre = _dot(low16, g1) + _dot(low16, g2) + _dot(low16, g3)
        tot = pre[C - 1:C, :]
        suf = tot - pre + g
        beta_all = _sigmoid(abc)

        k16 = kc.astype(bf16)
        kk = _dot_nt(k16, k16)
        qk = _dot_nt(qc.astype(bf16), k16)

        for rev in (0, 1):
            beta = beta_all[:, rev:rev + 1]
            gcc = (suf if rev else pre)[:, 2 + rev:3 + rev]
            tot11 = tot[:, 2 + rev:3 + rev]
            keep = mk_ref[MK_UPPER if rev else MK_LOWER]
            bc = jnp.broadcast_to(gcc, (C, C))
            decay = jnp.exp((bc - bc.T) * keep) * keep
            a = (beta * kk) * (decay * (keep - mk_ref[MK_EYE]))
            t = _tri_inverse(a, mk_ref)
            eg = jnp.exp(gcc)
            rhs = jnp.concatenate([vc * beta, kc * (beta * eg)], axis=1).astype(bf16)
            uw16 = _dot(t.astype(bf16), rhs).astype(bf16)
            kd16 = (kc * jnp.exp(tot11 - gcc)).astype(bf16)
            kt = _dot_tn(kd16, uw16)
            pr = _dot((qk * decay).astype(bf16), uw16)
            mp_s[rev, i, 0:A_DK, :] = (-kt[:, A_DV:]).astype(bf16)
            mp_s[rev, i, A_DK:A_DK + C, :] = (qc * eg - pr[:, A_DV:]).astype(bf16)
            nr_s[rev, i, 0:A_DK, :] = kt[:, 0:A_DV]
            nr_s[rev, i, A_DK:A_DK + C, :] = pr[:, 0:A_DV]
            gl_s[rev, i] = jnp.broadcast_to(jnp.exp(tot11), (8, LANE))
        return carry

    lax.fori_loop(0, nchunk, prep, 0)

    def scan(i, carry):
        out = []
        for rev, S in enumerate(carry):
            j = (nchunk - 1 - i) if rev else i
            y = _dot(mp_s[rev, j], S.astype(bf16))
            nr = nr_s[rev, j]
            o_s[rev, pl.ds(pl.multiple_of(j * C, C), C), :] = y[A_DK:, :] + nr[A_DK:, :]
            out.append(S * gl_s[rev, j][0:1, :] + y[0:A_DK, :] + nr[0:A_DK, :])
        return tuple(out)

    S0 = jnp.zeros((A_DK, A_DV), f32)
    lax.fori_loop(0, nchunk, scan, (S0, S0))

    def finish(i, carry):
        r0 = pl.multiple_of(i * C, C)
        o = o_s[0, pl.ds(r0, C), :] + o_s[1, pl.ds(r0, C), :]
        zg = x_ref[0, pl.ds(r0, C), nqkv:nqkv + A_DV].astype(f32)
        o_ref[0, pl.ds(r0, C), :] = (_rms(o, on_ref[...]) * (zg * _sigmoid(zg))).astype(o_ref.dtype)
        return carry

    lax.fori_loop(0, nchunk, finish, 0)


def _deltanet(qkvz, abh, conv_w, gate_sc, onorm):
    B, L, _ = qkvz.shape
    C = DELTA_CHUNK
    nchunk = L // C
    hw = 4 * A_DK
    scratch = [pltpu.VMEM((2, nchunk, A_DK + C, A_DV), bf16),
               pltpu.VMEM((2, nchunk, A_DK + C, A_DV), f32),
               pltpu.VMEM((2, nchunk, 8, LANE), f32),
               pltpu.VMEM((2, L, A_DV), f32)]
    return pl.pallas_call(
        _deltanet_kernel,
        grid=(B, A_HEADS),
        in_specs=[pl.BlockSpec(memory_space=pltpu.SMEM),
                  pl.BlockSpec((1, L, hw), lambda b, h: (b, 0, h)),
                  pl.BlockSpec((1, L, LANE), lambda b, h: (b, 0, h)),
                  pl.BlockSpec((1, 8, 3 * A_DK), lambda b, h: (h, 0, 0)),
                  pl.BlockSpec((1, A_DV), lambda b, h: (0, 0)),
                  pl.BlockSpec((N_MASKS, C, C), lambda b, h: (0, 0, 0))],
        out_specs=pl.BlockSpec((1, L, A_DV), lambda b, h: (b, 0, h)),
        out_shape=jax.ShapeDtypeStruct((B, L, A_WIDTH), bf16),
        scratch_shapes=scratch,
        compiler_params=_cparams("parallel", "parallel"),
        name="deltanet",
    )(gate_sc, qkvz, abh, conv_w, onorm.reshape(1, A_DV), _delta_masks())


def _filter_spectrum_kernel(c_ref, s_ref, hs_ref, hd_ref, kr_ref, ki_ref, kn_ref):
    L = c_ref.shape[0]
    n_fft = 2 * L
    hs = hs_ref[...]
    hd = hd_ref[...]

    def split_dot(m_ref, x):
        hi = x.astype(bf16)
        lo = (x - hi.astype(f32)).astype(bf16)
        m = m_ref[...]
        return _dot(m, hi) + _dot(m, lo)

    row = lax.broadcasted_iota(jnp.int32, (L, 1), 0)
    scale = jnp.where(row == 0, 1.0 / n_fft, 2.0 / n_fft)
    kr_ref[...] = split_dot(c_ref, hs) * scale
    ki_ref[...] = split_dot(s_ref, hd) * scale
    alt = jnp.where(row % 2 == 0, 1.0, -1.0)
    kn = jnp.sum(hs * alt, axis=0, keepdims=True) * (1.0 / n_fft)
    kn_ref[...] = jnp.broadcast_to(kn, kn_ref.shape)


def _filter_spectrum(cmat, smat, hsum, hdiff):
    L, n = hsum.shape
    ct = HY_CT
    return pl.pallas_call(
        _filter_spectrum_kernel,
        grid=(n // ct,),
        in_specs=[pl.BlockSpec((L, L), lambda j: (0, 0)),
                  pl.BlockSpec((L, L), lambda j: (0, 0)),
                  pl.BlockSpec((L, ct), lambda j: (0, j)),
                  pl.BlockSpec((L, ct), lambda j: (0, j))],
        out_specs=[pl.BlockSpec((L, ct), lambda j: (0, j)),
                   pl.BlockSpec((L, ct), lambda j: (0, j)),
                   pl.BlockSpec((8, ct), lambda j: (0, j))],
        out_shape=[jax.ShapeDtypeStruct((L, n), f32),
                   jax.ShapeDtypeStruct((L, n), f32),
                   jax.ShapeDtypeStruct((8, n), f32)],
        compiler_params=_cparams("parallel"),
    )(cmat, smat, hsum, hdiff)


def _hyena_kernel(c_ref, s_ref, x1_ref, x2_ref, v_ref, w1_ref, w2_ref, wv_ref,
                  kr0_ref, ki0_ref, kn0_ref, kr1_ref, ki1_ref, kn1_ref, db_ref, o_ref):
    L = c_ref.shape[0]
    row = lax.broadcasted_iota(jnp.int32, (L, 1), 0)
    alt = jnp.where(row % 2 == 0, 1.0, -1.0)

    def dwconv(x_ref, w_ref):
        x = x_ref[0].astype(f32)
        w = w_ref[...]
        prev = jnp.where(row == 0, 0.0, pltpu.roll(x, 1, 0))
        nxt = jnp.where(row == L - 1, 0.0, pltpu.roll(x, L - 1, 0))
        return prev * w[0:1, :] + x * w[1:2, :] + nxt * w[2:3, :]

    zz = dwconv(v_ref, wv_ref)
    stages = ((x1_ref, w1_ref, kr0_ref, ki0_ref, kn0_ref), (x2_ref, w2_ref, kr1_ref, ki1_ref, kn1_ref))
    for o, (g_ref, gw_ref, kr_ref, ki_ref, kn_ref) in enumerate(stages):
        z16 = zz.astype(bf16)
        ur = _dot(c_ref[...], z16)
        ui = _dot(s_ref[...], z16)
        unyq = jnp.sum(zz * alt, axis=0, keepdims=True)
        kr = kr_ref[...]
        ki = ki_ref[...]
        yr = (ur * kr - ui * ki).astype(bf16)
        yi = (ur * ki + ui * kr).astype(bf16)
        y = _dot(c_ref[...], yr) + _dot(s_ref[...], yi) + alt * (unyq * kn_ref[0:1, :])
        zz = dwconv(g_ref, gw_ref) * (y + zz * db_ref[o:o + 1, :])
    o_ref[0] = zz.astype(o_ref.dtype)


def _hyena(hx, conv_w, cmat, smat, kr, ki, kn, dbias):
    B, L, _ = hx.shape
    ct = HY_CT
    nct = B_WIDTH // ct
    once = pl.Buffered(1)

    def xspec(part):
        return pl.BlockSpec((1, L, ct), lambda c, b: (b, 0, part * nct + c))

    def wspec(part):
        return pl.BlockSpec((8, ct), lambda c, b: (0, part * nct + c))

    def kspec(order, rows):
        return pl.BlockSpec((rows, ct), lambda c, b: (0, order * nct + c), pipeline_mode=once)

    return pl.pallas_call(
        _hyena_kernel,
        grid=(nct, B),
        in_specs=[pl.BlockSpec((L, L), lambda c, b: (0, 0), pipeline_mode=once),
                  pl.BlockSpec((L, L), lambda c, b: (0, 0), pipeline_mode=once),
                  xspec(0), xspec(1), xspec(2), wspec(0), wspec(1), wspec(2),
                  kspec(0, L), kspec(0, L), kspec(0, 8), kspec(1, L), kspec(1, L), kspec(1, 8),
                  pl.BlockSpec((8, ct), lambda c, b: (0, c))],
        out_specs=pl.BlockSpec((1, L, ct), lambda c, b: (b, 0, c)),
        out_shape=jax.ShapeDtypeStruct((B, L, B_WIDTH), bf16),
        compiler_params=_cparams("parallel", "parallel"),
    )(cmat, smat, hx, hx, hx, conv_w, conv_w, conv_w, kr, ki, kn, kr, ki, kn, dbias)


def _route(logits):
    lane = lax.broadcasted_iota(jnp.int32, logits.shape, 1)
    neg = -jnp.inf
    big = ROUTER_LANES
    gl = jnp.where(lane < N_GROUPS, logits, neg)
    gmax = jnp.max(gl, axis=-1, keepdims=True)
    gsel = jnp.min(jnp.where(gl == gmax, lane, big), axis=-1, keepdims=True)
    psel = 1.0 / jnp.sum(jnp.exp(gl - gmax), axis=-1, keepdims=True)
    first = N_GROUPS + gsel * EXPERTS_PER_GROUP
    el = jnp.where((lane >= first) & (lane < first + EXPERTS_PER_GROUP), logits, neg)
    t1 = jnp.max(el, axis=-1, keepdims=True)
    i1 = jnp.min(jnp.where(el == t1, lane, big), axis=-1, keepdims=True)
    el2 = jnp.where(lane == i1, neg, el)
    t2 = jnp.max(el2, axis=-1, keepdims=True)
    i2 = jnp.min(jnp.where(el2 == t2, lane, big), axis=-1, keepdims=True)
    e2 = jnp.exp(t2 - t1)
    w1 = 1.0 / (1.0 + e2)
    w2 = e2 * w1
    return psel * (jnp.where(lane == i1, w1, 0.0) + jnp.where(lane == i2, w2, 0.0))


def _proj_route_kernel(*refs, n_in):
    x_ref = refs[0]
    a_refs = refs[1:1 + n_in]
    w_refs = refs[1 + n_in:1 + 2 * n_in]
    g_ref, wr_ref, xo_ref, h_ref, comb_ref = refs[1 + 2 * n_in:]
    x = x_ref[...]
    for a_ref, w_ref in zip(a_refs, w_refs):
        x = x + _dot(a_ref[...], w_ref[...])
    xo_ref[...] = x
    h = _rms(x, g_ref[...])
    h_ref[...] = h.astype(h_ref.dtype)
    logits = jnp.dot(h, wr_ref[...], preferred_element_type=f32, precision=lax.Precision.HIGHEST)
    comb_ref[...] = _route(logits)


def _proj_route(x, acts, ws, gain, w_router, tm=512):
    T, D = x.shape
    n_in = len(acts)
    in_specs = [pl.BlockSpec((tm, D), lambda i: (i, 0))]
    in_specs += [pl.BlockSpec((tm, a.shape[1]), lambda i: (i, 0)) for a in acts]
    in_specs += [pl.BlockSpec(w.shape, lambda i: (0, 0)) for w in ws]
    in_specs += [pl.BlockSpec((1, D), lambda i: (0, 0)),
                 pl.BlockSpec((D, ROUTER_LANES), lambda i: (0, 0))]
    return pl.pallas_call(
        functools.partial(_proj_route_kernel, n_in=n_in),
        grid=(T // tm,),
        in_specs=in_specs,
        out_specs=[pl.BlockSpec((tm, D), lambda i: (i, 0)),
                   pl.BlockSpec((tm, D), lambda i: (i, 0)),
                   pl.BlockSpec((tm, ROUTER_LANES), lambda i: (i, 0))],
        out_shape=[jax.ShapeDtypeStruct((T, D), f32),
                   jax.ShapeDtypeStruct((T, D), bf16),
                   jax.ShapeDtypeStruct((T, ROUTER_LANES), f32)],
        compiler_params=_cparams("parallel"),
    )(x, *acts, *ws, gain.reshape(1, D), w_router)


def _moe_dense_kernel(x_ref, h_ref, comb_ref, wg_ref, wu_ref, wd_ref, gf_ref, o_ref, *, final_norm):
    e = pl.program_id(1)

    @pl.when(e == 0)
    def _():
        o_ref[...] = x_ref[...]

    h = h_ref[...]
    gate = _dot(h, wg_ref[0])
    up = _dot(h, wu_ref[0])
    lane = lax.broadcasted_iota(jnp.int32, (1, ROUTER_LANES), 1)
    cw = jnp.sum(jnp.where(lane == N_GROUPS + e, comb_ref[...], 0.0), axis=-1, keepdims=True)
    hg = (gate * _sigmoid(gate) * up * cw).astype(bf16)
    o_ref[...] += _dot(hg, wd_ref[0])

    if final_norm:
        @pl.when(e == pl.num_programs(1) - 1)
        def _():
            o_ref[...] = _rms(o_ref[...], gf_ref[...])


def _moe_dense(x, h, comb, wg, wu, wd, gain_final, final_norm, tm=1024):
    T, D = x.shape
    tm = min(tm, T)
    return pl.pallas_call(
        functools.partial(_moe_dense_kernel, final_norm=final_norm),
        grid=(T // tm, N_EXPERTS),
        in_specs=[pl.BlockSpec((tm, D), lambda i, e: (i, 0)),
                  pl.BlockSpec((tm, D), lambda i, e: (i, 0)),
                  pl.BlockSpec((tm, ROUTER_LANES), lambda i, e: (i, 0)),
                  pl.BlockSpec((1, D, D_EXPERT), lambda i, e: (e, 0, 0)),
                  pl.BlockSpec((1, D, D_EXPERT), lambda i, e: (e, 0, 0)),
                  pl.BlockSpec((1, D_EXPERT, D), lambda i, e: (e, 0, 0)),
                  pl.BlockSpec((1, D), lambda i, e: (0, 0))],
        out_specs=pl.BlockSpec((tm, D), lambda i, e: (i, 0)),
        out_shape=jax.ShapeDtypeStruct((T, D), f32),
        compiler_params=_cparams("parallel", "arbitrary"),
    )(x, h, comb, wg, wu, wd, gain_final.reshape(1, D))


def _swa_kernel(sink_ref, q_ref, kp_ref, kc_ref, kn_ref, vp_ref, vc_ref, vn_ref, o_ref, *, seq_len):
    n = pl.program_id(1)
    blk = C_BLOCK
    qi = lax.broadcasted_iota(jnp.int32, (blk, 3 * blk), 0)
    kj = lax.broadcasted_iota(jnp.int32, (blk, 3 * blk), 1)
    rel = blk + qi - kj
    arel = jnp.abs(rel)
    key_pos = (n - 1) * blk + kj
    valid = (arel <= WINDOW) & (key_pos >= 0) & (key_pos < seq_len)
    dist = arel.astype(f32)
    scale = C_HEAD_DIM ** -0.5
    kcat = jnp.concatenate([kp_ref[0], kc_ref[0], kn_ref[0]], axis=0)
    vcat = jnp.concatenate([vp_ref[0], vc_ref[0], vn_ref[0]], axis=0)
    q = q_ref[0]
    outs = []
    for hh in range(C_HEADS):
        kvh = hh // C_GROUP
        qh = q[:, hh * C_HEAD_DIM:(hh + 1) * C_HEAD_DIM]
        kh = kcat[:, kvh * C_HEAD_DIM:(kvh + 1) * C_HEAD_DIM]
        vh = vcat[:, kvh * C_HEAD_DIM:(kvh + 1) * C_HEAD_DIM]
        slope = 2.0 ** (-8.0 * (hh + 1) / C_HEADS)
        s = _dot_nt(qh, kh) * scale - slope * dist
        s = jnp.where(valid, s, -jnp.inf)
        sink = sink_ref[hh]
        m = jnp.maximum(jnp.max(s, axis=-1, keepdims=True), sink)
        p = jnp.exp(s - m)
        den = jnp.sum(p, axis=-1, keepdims=True) + jnp.exp(sink - m)
        outs.append(_dot((p / den).astype(bf16), vh))
    o_ref[0] = jnp.concatenate(outs, axis=1).astype(o_ref.dtype)


def _swa(qkv, sinks):
    B, L, _ = qkv.shape
    nb = L // C_BLOCK
    dq = C_HEADS * C_HEAD_DIM
    dkv = C_KV_HEADS * C_HEAD_DIM
    k_col = dq // dkv
    v_col = k_col + 1

    def kv_spec(col, shift):
        return pl.BlockSpec((1, C_BLOCK, dkv),
                            lambda b, n: (b, jnp.clip(n + shift, 0, nb - 1), col))

    return pl.pallas_call(
        functools.partial(_swa_kernel, seq_len=L),
        grid=(B, nb),
        in_specs=[pl.BlockSpec(memory_space=pltpu.SMEM),
                  pl.BlockSpec((1, C_BLOCK, dq), lambda b, n: (b, n, 0)),
                  kv_spec(k_col, -1), kv_spec(k_col, 0), kv_spec(k_col, 1),
                  kv_spec(v_col, -1), kv_spec(v_col, 0), kv_spec(v_col, 1)],
        out_specs=pl.BlockSpec((1, C_BLOCK, dq), lambda b, n: (b, n, 0)),
        out_shape=jax.ShapeDtypeStruct((B, L, dq), bf16),
        compiler_params=_cparams("parallel", "parallel"),
    )(sinks, qkv, qkv, qkv, qkv, qkv, qkv, qkv)


def _hyena_filters(L, w1, b1, w2, b2, w3, b3, w4, freq):
    hp = lax.Precision.HIGHEST
    t = jnp.linspace(0.0, 1.0, L, dtype=f32)[:, None]
    wpos = (2.0 * math.pi / L) * jnp.arange(L, dtype=f32)[:, None]
    bands = jnp.linspace(1e-4, HY_BANDS - 1, HY_BANDS, dtype=f32)[None, :]
    feats = jnp.concatenate([t, jnp.cos(bands * wpos), -jnp.sin(bands * wpos)], axis=-1)
    h = jnp.sin(freq[0] * (jnp.dot(feats, w1, precision=hp) + b1))
    h = jnp.sin(freq[1] * (jnp.dot(h, w2, precision=hp) + b2))
    h = jnp.sin(freq[2] * (jnp.dot(h, w3, precision=hp) + b3))
    h = jnp.dot(h, w4, precision=hp).reshape(L, HY_ORDER, 2, B_WIDTH)
    deltas = jnp.abs(jnp.linspace(HY_MIN_DECAY, HY_MAX_DECAY, B_WIDTH, dtype=f32))
    window = jnp.exp(-t * deltas[None, :])
    return h * window[:, None, None, :]


def _dft_tables(L):
    n_fft = 2 * L
    idx = jnp.arange(L, dtype=jnp.int32)
    ph = (idx[:, None] * idx[None, :]) % n_fft
    ang = ph.astype(f32) * (2.0 * math.pi / n_fft)
    return jnp.cos(ang).astype(bf16), jnp.sin(ang).astype(bf16)


def _pad_rows(w, rows=8):
    return jnp.pad(w, ((0, rows - w.shape[0]), (0, 0)))


def _even_in_weights(w_in):
    hk = A_HEADS * A_DK
    q = w_in[:, 0:hk].reshape(D_MODEL, A_HEADS, A_DK)
    k = w_in[:, hk:2 * hk].reshape(D_MODEL, A_HEADS, A_DK)
    v = w_in[:, 2 * hk:A_QKV].reshape(D_MODEL, A_HEADS, A_DV)
    z = w_in[:, A_QKV:A_QKV + A_WIDTH].reshape(D_MODEL, A_HEADS, A_DV)
    qkvz = jnp.concatenate([q, k, v, z], axis=-1).reshape(D_MODEL, A_HEADS * 4 * A_DK)
    ab = w_in[:, A_QKV + A_WIDTH:A_QKV + A_WIDTH + 4 * A_HEADS].reshape(D_MODEL, 4, A_HEADS)
    ab = jnp.transpose(ab, (0, 2, 1))
    ab = jnp.pad(ab, ((0, 0), (0, 0), (0, LANE - 4))).reshape(D_MODEL, A_HEADS * LANE)
    hx = w_in[:, A_QKV + A_WIDTH + 4 * A_HEADS:]
    return jnp.concatenate([qkvz, ab, hx], axis=1).astype(bf16)


def _router_weights(w_group, w_expert):
    w = jnp.concatenate([w_group, w_expert], axis=1)
    return jnp.pad(w, ((0, 0), (0, ROUTER_LANES - w.shape[1])))


def kernel(x_prompt, x_sample, norm_mix, norm_ffn, norm_final, ev_w_in, ev_conv_a, ev_alog_f, ev_alog_b, ev_dtb_f, ev_dtb_b, ev_onorm, ev_conv_b, hy_w1, hy_b1, hy_w2, hy_b2, hy_w3, hy_b3, hy_w4, hy_freq, hy_dbias, ev_w_out, od_w_qkv, od_sinks, od_w_out, moe_w_group, moe_w_expert, moe_w_gate, moe_w_up, moe_w_down):
    bp = x_prompt.shape[0]
    L = x_prompt.shape[1]
    x = jnp.concatenate([x_prompt, x_sample], axis=0)
    B = x.shape[0]
    T = B * L
    x = x.reshape(T, D_MODEL)

    w_in = _even_in_weights(ev_w_in[0])
    n_qkvz = A_HEADS * 4 * A_DK
    n_ab = A_HEADS * LANE
    n_hx = 3 * B_WIDTH
    qkvz, abh, hx = _norm_matmul(x, norm_mix[0], w_in, (n_qkvz, n_ab, n_hx), (bf16, f32, bf16))

    hk = A_HEADS * A_DK
    ca = ev_conv_a[0]
    conv_a = jnp.concatenate([ca[0:hk].reshape(A_HEADS, A_DK, 3), ca[hk:2 * hk].reshape(A_HEADS, A_DK, 3),
                              ca[2 * hk:].reshape(A_HEADS, A_DV, 3)], axis=1)
    conv_a = jnp.pad(jnp.transpose(conv_a, (0, 2, 1)), ((0, 0), (0, 5), (0, 0)))
    gate_sc = jnp.stack([ev_alog_f[0], ev_alog_b[0], ev_dtb_f[0], ev_dtb_b[0]], axis=1)
    o_a = _deltanet(qkvz.reshape(B, L, n_qkvz), abh.reshape(B, L, n_ab), conv_a, gate_sc, ev_onorm[0])

    filt = _hyena_filters(L, hy_w1[0], hy_b1[0], hy_w2[0], hy_b2[0], hy_w3[0], hy_b3[0], hy_w4[0], hy_freq[0])
    hsum = (filt[:, :, 0] + filt[:, :, 1]).reshape(L, HY_ORDER * B_WIDTH)
    hdiff = (filt[:, :, 0] - filt[:, :, 1]).reshape(L, HY_ORDER * B_WIDTH)
    cmat, smat = _dft_tables(L)
    kr, ki, kn = _filter_spectrum(cmat, smat, hsum, hdiff)
    conv_b = _pad_rows(ev_conv_b[0].T)
    o_b = _hyena(hx.reshape(B, L, n_hx), conv_b, cmat, smat, kr, ki, kn, _pad_rows(hy_dbias[0]))

    w_out = ev_w_out[0].astype(bf16)
    x, h, comb = _proj_route(x, (o_a.reshape(T, A_WIDTH), o_b.reshape(T, B_WIDTH)),
                             (w_out[:A_WIDTH], w_out[A_WIDTH:]), norm_ffn[0],
                             _router_weights(moe_w_group[0], moe_w_expert[0]))
    x = _moe_dense(x, h, comb,
                   moe_w_gate[0].reshape(N_EXPERTS, D_MODEL, D_EXPERT).astype(bf16),
                   moe_w_up[0].reshape(N_EXPERTS, D_MODEL, D_EXPERT).astype(bf16),
                   moe_w_down[0].reshape(N_EXPERTS, D_EXPERT, D_MODEL).astype(bf16),
                   norm_final, final_norm=False)

    n_qkv = od_w_qkv.shape[-1]
    (qkv,) = _norm_matmul(x, norm_mix[1], od_w_qkv[0].astype(bf16), (n_qkv,), (bf16,))
    o_c = _swa(qkv.reshape(B, L, n_qkv), od_sinks[0])
    x, h, comb = _proj_route(x, (o_c.reshape(T, C_HEADS * C_HEAD_DIM),), (od_w_out[0].astype(bf16),),
                             norm_ffn[1], _router_weights(moe_w_group[1], moe_w_expert[1]))
    y = _moe_dense(x, h, comb,
                   moe_w_gate[1].reshape(N_EXPERTS, D_MODEL, D_EXPERT).astype(bf16),
                   moe_w_up[1].reshape(N_EXPERTS, D_MODEL, D_EXPERT).astype(bf16),
                   moe_w_down[1].reshape(N_EXPERTS, D_EXPERT, D_MODEL).astype(bf16),
                   norm_final, final_norm=True)
    y = y.reshape(B, L, D_MODEL)
    return (y[:bp], y[bp:])
```

```python
import functools
import math

import jax
import jax.numpy as jnp
import numpy as np
from jax import lax
from jax.experimental import pallas as pl
from jax.experimental.pallas import tpu as pltpu

f32 = jnp.float32
bf16 = jnp.bfloat16

EPS = 1e-6
D_MODEL = 1024

A_HEADS = 4
A_DK = 128
A_DV = 128
A_WIDTH = A_HEADS * A_DV
A_QKV = 2 * A_HEADS * A_DK + A_HEADS * A_DV
DELTA_CHUNK = 128
INV_BASE = 8
DELTA_GROUP = 8

B_WIDTH = D_MODEL - A_WIDTH
HY_ORDER = 2
HY_EMB = 33
HY_BANDS = (HY_EMB - 1) // 2
HY_TARGET = 1e-2
HY_MIN_DECAY = math.log(HY_TARGET) / 1.5
HY_MAX_DECAY = math.log(HY_TARGET) / 0.3
HY_CT = 256

C_HEADS = 16
C_KV_HEADS = 4
C_HEAD_DIM = 64
C_GROUP = C_HEADS // C_KV_HEADS
WINDOW = 128
C_BLOCK = 128

N_GROUPS = 4
EXPERTS_PER_GROUP = 8
N_EXPERTS = N_GROUPS * EXPERTS_PER_GROUP
D_EXPERT = 256
ROUTER_LANES = 128

LANE = 128
VMEM_LIMIT = 56 * 1024 * 1024


def _cparams(*sem):
    return pltpu.CompilerParams(dimension_semantics=sem, vmem_limit_bytes=VMEM_LIMIT)


def _dot(a, b):
    return jnp.dot(a, b, preferred_element_type=f32)


def _dot_nt(a, b):
    return lax.dot_general(a, b, (((1,), (1,)), ((), ())), preferred_element_type=f32)


def _dot_tn(a, b):
    return lax.dot_general(a, b, (((0,), (0,)), ((), ())), preferred_element_type=f32)


def _sigmoid(x):
    return 1.0 / (1.0 + jnp.exp(-x))


def _rms(x, g):
    return x * lax.rsqrt(jnp.mean(x * x, axis=-1, keepdims=True) + EPS) * g


def _norm_matmul_kernel(x_ref, g_ref, w_ref, *o_refs):
    h = _rms(x_ref[...], g_ref[...]).astype(bf16)
    off = 0
    for o_ref in o_refs:
        n = o_ref.shape[-1]
        o_ref[...] = _dot(h, w_ref[:, off:off + n]).astype(o_ref.dtype)
        off += n


def _norm_matmul(x, gain, w, splits, dtypes, tm=512):
    T, D = x.shape
    n_all = w.shape[1]
    return pl.pallas_call(
        _norm_matmul_kernel,
        grid=(T // tm,),
        in_specs=[pl.BlockSpec((tm, D), lambda i: (i, 0)),
                  pl.BlockSpec((1, D), lambda i: (0, 0)),
                  pl.BlockSpec((D, n_all), lambda i: (0, 0))],
        out_specs=[pl.BlockSpec((tm, n), lambda i: (i, 0)) for n in splits],
        out_shape=[jax.ShapeDtypeStruct((T, n), dt) for n, dt in zip(splits, dtypes)],
        compiler_params=_cparams("parallel"),
    )(x, gain.reshape(1, D), w)


MK_LOWER, MK_UPPER, MK_BASE, MK_LEVEL0 = 0, 1, 2, 3
N_LEVELS = int(math.log2(DELTA_CHUNK // INV_BASE))
MK_EYE = MK_LEVEL0 + N_LEVELS
N_MASKS = MK_EYE + 1


def _delta_masks():
    c = np.arange(DELTA_CHUNK)[:, None]
    s = np.arange(DELTA_CHUNK)[None, :]
    m = [s <= c, s >= c, c // INV_BASE == s // INV_BASE]
    b = INV_BASE
    while b < DELTA_CHUNK:
        m.append((c // (2 * b) == s // (2 * b)) & (c // b != s // b))
        b *= 2
    m.append(c == s)
    return jnp.asarray(np.stack(m).astype(np.float32))


def _tri_inverse(mats, mk_ref):
    ps = [a * mk_ref[MK_BASE] for a in mats]
    ts = [mk_ref[MK_EYE] - p for p in ps]
    n = 2
    while n < INV_BASE:
        p16s = [p.astype(bf16) for p in ps]
        ps = [_dot(p16, p16) for p16 in p16s]
        ts = [t + _dot(t.astype(bf16), p.astype(bf16)) for t, p in zip(ts, ps)]
        n *= 2
    for lvl in range(N_LEVELS):
        t16s = [t.astype(bf16) for t in ts]
        xs = [_dot((a * mk_ref[MK_LEVEL0 + lvl]).astype(bf16), t16) for a, t16 in zip(mats, t16s)]
        ts = [t - _dot(t16, x.astype(bf16)) for t, t16, x in zip(ts, t16s, xs)]
    return ts


def _deltanet_kernel(sc_ref, x_ref, ab_ref, cw_ref, on_ref, mk_ref, o_ref, mp_s, nr_s, gl_s, o_s):
    hd = pl.program_id(1)
    L = x_ref.shape[1]
    C = DELTA_CHUNK
    nchunk = L // C
    group = math.gcd(DELTA_GROUP, nchunk)
    nqkv = 3 * A_DK
    halo = 16

    lane = lax.broadcasted_iota(jnp.int32, (1, LANE), 1)
    alog = jnp.where(lane == 2, sc_ref[hd, 0], jnp.where(lane == 3, sc_ref[hd, 1], 0.0))
    dtb = jnp.where(lane == 2, sc_ref[hd, 2], jnp.where(lane == 3, sc_ref[hd, 3], 0.0))
    neg_a = jnp.where((lane == 2) | (lane == 3), -jnp.exp(alog), 0.0)
    rowi = lax.broadcasted_iota(jnp.int32, (C, 1), 0)
    cw = cw_ref[0]

    def chunk_inputs(i):
        r0 = pl.multiple_of(i * C, C)
        xc = x_ref[0, pl.ds(r0, C), 0:nqkv].astype(f32)
        rp = pl.multiple_of(jnp.maximum(r0 - halo, 0), halo)
        rn = pl.multiple_of(jnp.minimum(r0 + C, L - halo), halo)
        prow = x_ref[0, pl.ds(rp, halo), 0:nqkv][halo - 1:halo, :].astype(f32)
        nrow = x_ref[0, pl.ds(rn, halo), 0:nqkv][0:1, :].astype(f32)
        prow = jnp.where(i > 0, prow, 0.0)
        nrow = jnp.where(i < nchunk - 1, nrow, 0.0)
        prev = jnp.where(rowi == 0, prow, pltpu.roll(xc, 1, 0))
        nxt = jnp.where(rowi == C - 1, nrow, pltpu.roll(xc, C - 1, 0))
        y = prev * cw[0:1, :] + xc * cw[1:2, :] + nxt * cw[2:3, :]
        y = y * _sigmoid(y)
        q = y[:, 0:A_DK]
        k = y[:, A_DK:2 * A_DK]
        vc = y[:, 2 * A_DK:nqkv]
        qc = q * lax.rsqrt(jnp.sum(q * q, axis=-1, keepdims=True) + EPS) * (A_DK ** -0.5)
        kc = k * lax.rsqrt(jnp.sum(k * k, axis=-1, keepdims=True) + EPS)

        abc = ab_ref[0, pl.ds(r0, C), :]
        zg = abc + dtb
        g = neg_a * (jnp.maximum(zg, 0.0) + jnp.log(1.0 + jnp.exp(-jnp.abs(zg))))
        low16 = mk_ref[MK_LOWER].astype(bf16)
        g1 = g.astype(bf16)
        e1 = g - g1.astype(f32)
        g2 = e1.astype(bf16)
        g3 = (e1 - g2.astype(f32)).astype(bf16)
        pre = _dot(low16, g1) + _dot(low16, g2) + _dot(low16, g3)
        tot = pre[C - 1:C, :]
        suf = tot - pre + g
        beta_all = _sigmoid(abc)

        k16 = kc.astype(bf16)
        kk = _dot_nt(k16, k16)
        qk = _dot_nt(qc.astype(bf16), k16)

        chains = []
        for rev in (0, 1):
            beta = beta_all[:, rev:rev + 1]
            gcc = (suf if rev else pre)[:, 2 + rev:3 + rev]
            tot11 = tot[:, 2 + rev:3 + rev]
            keep = mk_ref[MK_UPPER if rev else MK_LOWER]
            bc = jnp.broadcast_to(gcc, (C, C))
            decay = jnp.exp((bc - bc.T) * keep) * keep
            a = (beta * kk) * (decay * (keep - mk_ref[MK_EYE]))
            eg = jnp.exp(gcc)
            chains.append(dict(
                i=i, rev=rev, a=a, gl=jnp.exp(tot11), qd=qc * eg,
                rhs=jnp.concatenate([vc * beta, kc * (beta * eg)], axis=1).astype(bf16),
                kd16=(kc * jnp.exp(tot11 - gcc)).astype(bf16),
                qk16=(qk * decay).astype(bf16)))
        return chains

    def prep(it, carry):
        chains = []
        for gi in range(group):
            chains += chunk_inputs(it * group + gi)
        ts = _tri_inverse([c["a"] for c in chains], mk_ref)
        uws = [_dot(t.astype(bf16), c["rhs"]).astype(bf16) for t, c in zip(ts, chains)]
        kts = [_dot_tn(c["kd16"], uw) for c, uw in zip(chains, uws)]
        prs = [_dot(c["qk16"], uw) for c, uw in zip(chains, uws)]
        for c, kt, pr in zip(chains, kts, prs):
            rev, i = c["rev"], c["i"]
            mp_s[rev, i, 0:A_DK, :] = (-kt[:, A_DV:]).astype(bf16)
            mp_s[rev, i, A_DK:A_DK + C, :] = (c["qd"] - pr[:, A_DV:]).astype(bf16)
            nr_s[rev, i, 0:A_DK, :] = kt[:, 0:A_DV]
            nr_s[rev, i, A_DK:A_DK + C, :] = pr[:, 0:A_DV]
            gl_s[rev, i] = jnp.broadcast_to(c["gl"], (8, LANE))
        return carry

    lax.fori_loop(0, nchunk // group, prep, 0)

    def scan(i, carry):
        out = []
        for rev, S in enumerate(carry):
            j = (nchunk - 1 - i) if rev else i
            y = _dot(mp_s[rev, j], S.astype(bf16))
            nr = nr_s[rev, j]
            o_s[rev, pl.ds(pl.multiple_of(j * C, C), C), :] = y[A_DK:, :] + nr[A_DK:, :]
            out.append(S * gl_s[rev, j][0:1, :] + y[0:A_DK, :] + nr[0:A_DK, :])
        return tuple(out)

    S0 = jnp.zeros((A_DK, A_DV), f32)
    lax.fori_loop(0, nchunk, scan, (S0, S0))

    def finish(i, carry):
        r0 = pl.multiple_of(i * C, C)
        o = o_s[0, pl.ds(r0, C), :] + o_s[1, pl.ds(r0, C), :]
        zg = x_ref[0, pl.ds(r0, C), nqkv:nqkv + A_DV].astype(f32)
        o_ref[0, pl.ds(r0, C), :] = (_rms(o, on_ref[...]) * (zg * _sigmoid(zg))).astype(o_ref.dtype)
        return carry

    lax.fori_loop(0, nchunk, finish, 0)


def _deltanet(qkvz, abh, conv_w, gate_sc, onorm):
    B, L, _ = qkvz.shape
    C = DELTA_CHUNK
    nchunk = L // C
    hw = 4 * A_DK
    scratch = [pltpu.VMEM((2, nchunk, A_DK + C, A_DV), bf16),
               pltpu.VMEM((2, nchunk, A_DK + C, A_DV), f32),
               pltpu.VMEM((2, nchunk, 8, LANE), f32),
               pltpu.VMEM((2, L, A_DV), f32)]
    return pl.pallas_call(
        _deltanet_kernel,
        grid=(B, A_HEADS),
        in_specs=[pl.BlockSpec(memory_space=pltpu.SMEM),
                  pl.BlockSpec((1, L, hw), lambda b, h: (b, 0, h)),
                  pl.BlockSpec((1, L, LANE), lambda b, h: (b, 0, h)),
                  pl.BlockSpec((1, 8, 3 * A_DK), lambda b, h: (h, 0, 0)),
                  pl.BlockSpec((1, A_DV), lambda b, h: (0, 0)),
                  pl.BlockSpec((N_MASKS, C, C), lambda b, h: (0, 0, 0))],
        out_specs=pl.BlockSpec((1, L, A_DV), lambda b, h: (b, 0, h)),
        out_shape=jax.ShapeDtypeStruct((B, L, A_WIDTH), bf16),
        scratch_shapes=scratch,
        compiler_params=_cparams("parallel", "parallel"),
        name="deltanet",
    )(gate_sc, qkvz, abh, conv_w, onorm.reshape(1, A_DV), _delta_masks())


def _filter_spectrum_kernel(c_ref, s_ref, hs_ref, hd_ref, kr_ref, ki_ref, kn_ref):
    L = c_ref.shape[0]
    n_fft = 2 * L
    hs = hs_ref[...]
    hd = hd_ref[...]

    def split_dot(m_ref, x):
        hi = x.astype(bf16)
        lo = (x - hi.astype(f32)).astype(bf16)
        m = m_ref[...]
        return _dot(m, hi) + _dot(m, lo)

    row = lax.broadcasted_iota(jnp.int32, (L, 1), 0)
    scale = jnp.where(row == 0, 1.0 / n_fft, 2.0 / n_fft)
    kr_ref[...] = split_dot(c_ref, hs) * scale
    ki_ref[...] = split_dot(s_ref, hd) * scale
    alt = jnp.where(row % 2 == 0, 1.0, -1.0)
    kn = jnp.sum(hs * alt, axis=0, keepdims=True) * (1.0 / n_fft)
    kn_ref[...] = jnp.broadcast_to(kn, kn_ref.shape)


def _filter_spectrum(cmat, smat, hsum, hdiff):
    L, n = hsum.shape
    ct = HY_CT
    return pl.pallas_call(
        _filter_spectrum_kernel,
        grid=(n // ct,),
        in_specs=[pl.BlockSpec((L, L), lambda j: (0, 0)),
                  pl.BlockSpec((L, L), lambda j: (0, 0)),
                  pl.BlockSpec((L, ct), lambda j: (0, j)),
                  pl.BlockSpec((L, ct), lambda j: (0, j))],
        out_specs=[pl.BlockSpec((L, ct), lambda j: (0, j)),
                   pl.BlockSpec((L, ct), lambda j: (0, j)),
                   pl.BlockSpec((8, ct), lambda j: (0, j))],
        out_shape=[jax.ShapeDtypeStruct((L, n), f32),
                   jax.ShapeDtypeStruct((L, n), f32),
                   jax.ShapeDtypeStruct((8, n), f32)],
        compiler_params=_cparams("parallel"),
    )(cmat, smat, hsum, hdiff)


def _hyena_kernel(c_ref, s_ref, x1_ref, x2_ref, v_ref, w1_ref, w2_ref, wv_ref,
                  kr0_ref, ki0_ref, kn0_ref, kr1_ref, ki1_ref, kn1_ref, db_ref, o_ref):
    L = c_ref.shape[0]
    row = lax.broadcasted_iota(jnp.int32, (L, 1), 0)
    alt = jnp.where(row % 2 == 0, 1.0, -1.0)

    def dwconv(x_ref, w_ref):
        x = x_ref[0].astype(f32)
        w = w_ref[...]
        prev = jnp.where(row == 0, 0.0, pltpu.roll(x, 1, 0))
        nxt = jnp.where(row == L - 1, 0.0, pltpu.roll(x, L - 1, 0))
        return prev * w[0:1, :] + x * w[1:2, :] + nxt * w[2:3, :]

    zz = dwconv(v_ref, wv_ref)
    stages = ((x1_ref, w1_ref, kr0_ref, ki0_ref, kn0_ref), (x2_ref, w2_ref, kr1_ref, ki1_ref, kn1_ref))
    for o, (g_ref, gw_ref, kr_ref, ki_ref, kn_ref) in enumerate(stages):
        z16 = zz.astype(bf16)
        ur = _dot(c_ref[...], z16)
        ui = _dot(s_ref[...], z16)
        unyq = jnp.sum(zz * alt, axis=0, keepdims=True)
        kr = kr_ref[...]
        ki = ki_ref[...]
        yr = (ur * kr - ui * ki).astype(bf16)
        yi = (ur * ki + ui * kr).astype(bf16)
        y = _dot(c_ref[...], yr) + _dot(s_ref[...], yi) + alt * (unyq * kn_ref[0:1, :])
        zz = dwconv(g_ref, gw_ref) * (y + zz * db_ref[o:o + 1, :])
    o_ref[0] = zz.astype(o_ref.dtype)


def _hyena(hx, conv_w, cmat, smat, kr, ki, kn, dbias):
    B, L, _ = hx.shape
    ct = HY_CT
    nct = B_WIDTH // ct
    once = pl.Buffered(1)

    def xspec(part):
        return pl.BlockSpec((1, L, ct), lambda c, b: (b, 0, part * nct + c))

    def wspec(part):
        return pl.BlockSpec((8, ct), lambda c, b: (0, part * nct + c))

    def kspec(order, rows):
        return pl.BlockSpec((rows, ct), lambda c, b: (0, order * nct + c), pipeline_mode=once)

    return pl.pallas_call(
        _hyena_kernel,
        grid=(nct, B),
        in_specs=[pl.BlockSpec((L, L), lambda c, b: (0, 0), pipeline_mode=once),
                  pl.BlockSpec((L, L), lambda c, b: (0, 0), pipeline_mode=once),
                  xspec(0), xspec(1), xspec(2), wspec(0), wspec(1), wspec(2),
                  kspec(0, L), kspec(0, L), kspec(0, 8), kspec(1, L), kspec(1, L), kspec(1, 8),
                  pl.BlockSpec((8, ct), lambda c, b: (0, c))],
        out_specs=pl.BlockSpec((1, L, ct), lambda c, b: (b, 0, c)),
        out_shape=jax.ShapeDtypeStruct((B, L, B_WIDTH), bf16),
        compiler_params=_cparams("parallel", "parallel"),
    )(cmat, smat, hx, hx, hx, conv_w, conv_w, conv_w, kr, ki, kn, kr, ki, kn, dbias)


RT_E1, RT_E2, RT_W1, RT_W2, RT_R1, RT_R2 = 0, 1, 2, 3, 4, 5
ROUTE_TM = 512
EXPERT_TM = 256


def _pack_bf16_pairs(x):
    n = x.shape[1] // 2
    bits = lax.bitcast_convert_type(x.astype(bf16).astype(f32), jnp.uint32)
    return bits[:, :n] | (bits[:, n:] >> 16)


def _unpack_bf16_pairs(p):
    hi = lax.bitcast_convert_type(p & jnp.uint32(0xFFFF0000), f32)
    lo = lax.bitcast_convert_type(p << 16, f32)
    return hi, lo


def _route(logits, low_ref):
    lane = lax.broadcasted_iota(jnp.int32, logits.shape, 1)
    neg = -jnp.inf
    big = ROUTER_LANES
    gl = jnp.where(lane < N_GROUPS, logits, neg)
    gmax = jnp.max(gl, axis=-1, keepdims=True)
    gsel = jnp.min(jnp.where(gl == gmax, lane, big), axis=-1, keepdims=True)
    psel = 1.0 / jnp.sum(jnp.exp(gl - gmax), axis=-1, keepdims=True)
    first = N_GROUPS + gsel * EXPERTS_PER_GROUP
    el = jnp.where((lane >= first) & (lane < first + EXPERTS_PER_GROUP), logits, neg)
    t1 = jnp.max(el, axis=-1, keepdims=True)
    i1 = jnp.min(jnp.where(el == t1, lane, big), axis=-1, keepdims=True)
    el2 = jnp.where(lane == i1, neg, el)
    t2 = jnp.max(el2, axis=-1, keepdims=True)
    i2 = jnp.min(jnp.where(el2 == t2, lane, big), axis=-1, keepdims=True)
    ex = jnp.exp(t2 - t1)
    w1 = psel / (1.0 + ex)
    w2 = ex * w1
    e1 = i1 - N_GROUPS
    e2 = i2 - N_GROUPS
    onehot = jnp.where((lane == e1) | (lane == e2), 1.0, 0.0)
    before = _dot(low_ref[...], onehot.astype(bf16))
    r1 = jnp.sum(jnp.where(lane == e1, before, 0.0), axis=-1, keepdims=True)
    r2 = jnp.sum(jnp.where(lane == e2, before, 0.0), axis=-1, keepdims=True)
    tm = logits.shape[0]
    counts = before[tm - 1:tm, :] + onehot[tm - 1:tm, :]
    rec = jnp.zeros(logits.shape, f32)
    for ln, val in ((RT_E1, e1.astype(f32)), (RT_E2, e2.astype(f32)), (RT_W1, w1), (RT_W2, w2), (RT_R1, r1), (RT_R2, r2)):
        rec = jnp.where(lane == ln, val, rec)
    return rec, counts


def _proj_route_kernel(*refs, n_in):
    x_ref = refs[0]
    a_refs = refs[1:1 + n_in]
    w_refs = refs[1 + n_in:1 + 2 * n_in]
    g_ref, wr_ref, low_ref, xo_ref, hp_ref, rt_ref, cnt_ref = refs[1 + 2 * n_in:]
    x = x_ref[...]
    for a_ref, w_ref in zip(a_refs, w_refs):
        x = x + _dot(a_ref[...], w_ref[...])
    xo_ref[...] = x
    h = _rms(x, g_ref[...])
    hp_ref[...] = _pack_bf16_pairs(h)
    logits = jnp.dot(h, wr_ref[...], preferred_element_type=f32, precision=lax.Precision.HIGHEST)
    rec, counts = _route(logits, low_ref)
    rt_ref[...] = rec
    cnt_ref[0] = jnp.broadcast_to(counts, cnt_ref.shape[1:])


def _proj_route(x, acts, ws, gain, w_router):
    T, D = x.shape
    tm = ROUTE_TM
    n_in = len(acts)
    low = jnp.asarray(np.tril(np.ones((tm, tm), np.float32), -1), dtype=bf16)
    in_specs = [pl.BlockSpec((tm, D), lambda i: (i, 0))]
    in_specs += [pl.BlockSpec((tm, a.shape[1]), lambda i: (i, 0)) for a in acts]
    in_specs += [pl.BlockSpec(w.shape, lambda i: (0, 0)) for w in ws]
    in_specs += [pl.BlockSpec((1, D), lambda i: (0, 0)),
                 pl.BlockSpec((D, ROUTER_LANES), lambda i: (0, 0)),
                 pl.BlockSpec((tm, tm), lambda i: (0, 0))]
    return pl.pallas_call(
        functools.partial(_proj_route_kernel, n_in=n_in),
        grid=(T // tm,),
        in_specs=in_specs,
        out_specs=[pl.BlockSpec((tm, D), lambda i: (i, 0)),
                   pl.BlockSpec((tm, D // 2), lambda i: (i, 0)),
                   pl.BlockSpec((tm, ROUTER_LANES), lambda i: (i, 0)),
                   pl.BlockSpec((1, 8, ROUTER_LANES), lambda i: (i, 0, 0))],
        out_shape=[jax.ShapeDtypeStruct((T, D), f32),
                   jax.ShapeDtypeStruct((T, D // 2), jnp.uint32),
                   jax.ShapeDtypeStruct((T, ROUTER_LANES), f32),
                   jax.ShapeDtypeStruct((T // tm, 8, ROUTER_LANES), f32)],
        compiler_params=_cparams("parallel"),
        name="proj_route",
    )(x, *acts, *ws, gain.reshape(1, D), w_router, low)


def _moe_rows(T):
    return 2 * T + N_EXPERTS * EXPERT_TM


def _moe_plan(route, counts):
    T = route.shape[0]
    tm = ROUTE_TM
    e = route[:, RT_E1:RT_E2 + 1].astype(jnp.int32)
    rank = route[:, RT_R1:RT_R2 + 1].astype(jnp.int32)
    cnt = counts[:, 0, :N_EXPERTS].astype(jnp.int32)
    total = jnp.sum(cnt, axis=0)
    padded = ((total + EXPERT_TM - 1) // EXPERT_TM) * EXPERT_TM
    ends = jnp.cumsum(padded)
    tile_base = (ends - padded)[None, :] + jnp.cumsum(cnt, axis=0) - cnt
    base = jnp.broadcast_to(tile_base[:, None, :], (T // tm, tm, N_EXPERTS)).reshape(T, N_EXPERTS)
    ids = jnp.arange(N_EXPERTS, dtype=jnp.int32)[None, :]
    dest = jnp.stack([jnp.sum(jnp.where(ids == e[:, j:j + 1], base, 0), axis=1) + rank[:, j] for j in (0, 1)], axis=1)
    n_tiles = _moe_rows(T) // EXPERT_TM
    n_used = ends[-1] // EXPERT_TM
    starts = jnp.arange(n_tiles, dtype=jnp.int32) * EXPERT_TM
    tile_expert = jnp.sum((starts[:, None] >= ends[None, :]).astype(jnp.int32), axis=1)
    last_expert = jnp.sum((jnp.maximum(n_used - 1, 0) * EXPERT_TM >= ends).astype(jnp.int32))
    tile_expert = jnp.minimum(tile_expert, last_expert)
    return dest.reshape(2 * T), tile_expert, n_used.reshape(1).astype(jnp.int32)


def _dispatch_kernel(dest_ref, hp_ref, xs_in_ref, xs_ref, buf, sem):
    del xs_in_ref
    i = pl.program_id(0)
    tm = hp_ref.shape[0]
    slot = i % 2
    buf[slot, 0:tm, :] = hp_ref[...]

    def row_copy(r, j, s):
        return pltpu.make_async_copy(buf.at[s, pl.ds(r, 1), :], xs_ref.at[pl.ds(dest_ref[2 * r + j], 1), :], sem.at[s])

    def body(r, c):
        row_copy(r, 0, slot).start()
        row_copy(r, 1, slot).start()
        return c

    lax.fori_loop(0, tm, body, 0)

    def drain(s):
        pltpu.make_async_copy(buf.at[s], xs_ref.at[pl.ds(0, 2 * tm), :], sem.at[s]).wait()

    @pl.when(i > 0)
    def _():
        drain(1 - slot)

    @pl.when(i == pl.num_programs(0) - 1)
    def _():
        drain(slot)


def _dispatch(dest, hp):
    T, half = hp.shape
    tm = ROUTE_TM
    n_rows = _moe_rows(T)
    return pl.pallas_call(
        _dispatch_kernel,
        grid=(T // tm,),
        in_specs=[pl.BlockSpec((2 * tm,), lambda i: (i,), memory_space=pltpu.SMEM),
                  pl.BlockSpec((tm, half), lambda i: (i, 0)),
                  pl.BlockSpec(memory_space=pl.ANY)],
        out_specs=pl.BlockSpec(memory_space=pl.ANY),
        out_shape=jax.ShapeDtypeStruct((n_rows, half), jnp.uint32),
        scratch_shapes=[pltpu.VMEM((2, 2 * tm, half), jnp.uint32), pltpu.SemaphoreType.DMA((2,))],
        input_output_aliases={2: 0},
        compiler_params=_cparams("arbitrary"),
        name="moe_dispatch",
    )(dest, hp, jnp.zeros((n_rows, half), jnp.uint32))


def _expert_kernel(te_ref, nu_ref, xs_ref, wg_ref, wu_ref, wd_ref, ys_ref):
    @pl.when(pl.program_id(0) < nu_ref[0])
    def _():
        hi, lo = _unpack_bf16_pairs(xs_ref[...])
        x = jnp.concatenate([hi, lo], axis=1).astype(bf16)
        gate = _dot(x, wg_ref[0].astype(bf16))
        up = _dot(x, wu_ref[0].astype(bf16))
        hg = (gate * _sigmoid(gate) * up).astype(bf16)
        ys_ref[...] = _pack_bf16_pairs(_dot(hg, wd_ref[0].astype(bf16)))

    @pl.when(pl.program_id(0) >= nu_ref[0])
    def _():
        ys_ref[...] = jnp.zeros(ys_ref.shape, ys_ref.dtype)


def _experts(tile_expert, n_used, xs, wg, wu, wd):
    n_rows, half = xs.shape
    D = 2 * half
    tm = EXPERT_TM
    grid_spec = pltpu.PrefetchScalarGridSpec(
        num_scalar_prefetch=2,
        grid=(n_rows // tm,),
        in_specs=[pl.BlockSpec((tm, half), lambda i, te, nu: (i, 0)),
                  pl.BlockSpec((1, D, D_EXPERT), lambda i, te, nu: (te[i], 0, 0)),
                  pl.BlockSpec((1, D, D_EXPERT), lambda i, te, nu: (te[i], 0, 0)),
                  pl.BlockSpec((1, D_EXPERT, D), lambda i, te, nu: (te[i], 0, 0))],
        out_specs=pl.BlockSpec((tm, half), lambda i, te, nu: (i, 0)))
    return pl.pallas_call(
        _expert_kernel,
        grid_spec=grid_spec,
        out_shape=jax.ShapeDtypeStruct((n_rows, half), jnp.uint32),
        compiler_params=_cparams("arbitrary"),
        name="moe_experts",
    )(tile_expert, n_used, xs, wg, wu, wd)


def _combine_kernel(dcur_ref, dnxt_ref, x_ref, rt_ref, gf_ref, ys_ref, o_ref, buf, sem, *, final_norm):
    i = pl.program_id(0)
    n = pl.num_programs(0)
    tm = x_ref.shape[0]
    half = x_ref.shape[1] // 2
    slot = i % 2

    def issue(d_ref, s):
        def body(r, c):
            for j in (0, 1):
                pltpu.make_async_copy(ys_ref.at[pl.ds(d_ref[2 * r + j], 1), :],
                                      buf.at[s, pl.ds(j * tm + r, 1), :], sem.at[s]).start()
            return c
        lax.fori_loop(0, tm, body, 0)

    @pl.when(i == 0)
    def _():
        issue(dcur_ref, 0)

    @pl.when(i + 1 < n)
    def _():
        issue(dnxt_ref, 1 - slot)

    pltpu.make_async_copy(ys_ref.at[pl.ds(0, 2 * tm), :], buf.at[slot], sem.at[slot]).wait()
    h1, l1 = _unpack_bf16_pairs(buf[slot, 0:tm, :])
    h2, l2 = _unpack_bf16_pairs(buf[slot, tm:2 * tm, :])
    w1 = rt_ref[:, RT_W1:RT_W1 + 1]
    w2 = rt_ref[:, RT_W2:RT_W2 + 1]
    out = jnp.concatenate([x_ref[:, 0:half] + w1 * h1 + w2 * h2, x_ref[:, half:] + w1 * l1 + w2 * l2], axis=1)
    if final_norm:
        out = _rms(out, gf_ref[...])
    o_ref[...] = out


def _combine(dest, x, route, ys, gain_final, final_norm):
    T, D = x.shape
    tm = ROUTE_TM
    n = T // tm
    return pl.pallas_call(
        functools.partial(_combine_kernel, final_norm=final_norm),
        grid=(n,),
        in_specs=[pl.BlockSpec((2 * tm,), lambda i: (i,), memory_space=pltpu.SMEM),
                  pl.BlockSpec((2 * tm,), lambda i: (jnp.minimum(i + 1, n - 1),), memory_space=pltpu.SMEM),
                  pl.BlockSpec((tm, D), lambda i: (i, 0)),
                  pl.BlockSpec((tm, ROUTER_LANES), lambda i: (i, 0)),
                  pl.BlockSpec((1, D), lambda i: (0, 0)),
                  pl.BlockSpec(memory_space=pl.ANY)],
        out_specs=pl.BlockSpec((tm, D), lambda i: (i, 0)),
        out_shape=jax.ShapeDtypeStruct((T, D), f32),
        scratch_shapes=[pltpu.VMEM((2, 2 * tm, D // 2), jnp.uint32), pltpu.SemaphoreType.DMA((2,))],
        compiler_params=_cparams("arbitrary"),
        name="moe_combine",
    )(dest, dest, x, route, gain_final.reshape(1, D), ys)


def _moe(x, hp, route, counts, wg, wu, wd, gain_final, final_norm):
    dest, tile_expert, n_used = _moe_plan(route, counts)
    xs = _dispatch(dest, hp)
    ys = _experts(tile_expert, n_used, xs,
                  wg.reshape(N_EXPERTS, D_MODEL, D_EXPERT), wu.reshape(N_EXPERTS, D_MODEL, D_EXPERT),
                  wd.reshape(N_EXPERTS, D_EXPERT, D_MODEL))
    return _combine(dest, x, route, ys, gain_final, final_norm)


LOG2E = 1.4426950408889634


def _swa_bias_tables():
    blk = C_BLOCK
    qi = np.arange(blk)[:, None]
    kj = np.arange(3 * blk)[None, :]
    dist = np.abs(blk + qi - kj)
    tabs = np.full((3, C_KV_HEADS, C_GROUP * blk, 3 * blk), -np.inf, np.float32)
    for case in range(3):
        ok = dist <= WINDOW
        if case == 0:
            ok = ok & (kj >= blk)
        if case == 2:
            ok = ok & (kj < 2 * blk)
        for hh in range(C_HEADS):
            slope = 2.0 ** (-8.0 * (hh + 1) / C_HEADS)
            t = np.where(ok, -slope * LOG2E * dist, -np.inf)
            kvh, g = divmod(hh, C_GROUP)
            tabs[case, kvh, g * blk:(g + 1) * blk] = t
    return jnp.asarray(tabs)


def _swa_kernel(sink_ref, q_ref, kp_ref, kc_ref, kn_ref, vp_ref, vc_ref, vn_ref, bias_ref, o_ref):
    blk = C_BLOCK
    hd = C_HEAD_DIM
    kcat = jnp.concatenate([kp_ref[0], kc_ref[0], kn_ref[0]], axis=0)
    vcat = jnp.concatenate([vp_ref[0], vc_ref[0], vn_ref[0]], axis=0)
    q = q_ref[0]
    rblk = lax.broadcasted_iota(jnp.int32, (C_GROUP * blk, 1), 0) // blk
    lane = lax.broadcasted_iota(jnp.int32, (3 * blk, LANE - hd), 1)
    ones_pad = jnp.where(lane == 0, 1.0, 0.0).astype(bf16)
    q4s, sinks = [], []
    for kvh in range(C_KV_HEADS):
        q4s.append(jnp.concatenate(
            [q[:, (kvh * C_GROUP + g) * hd:(kvh * C_GROUP + g + 1) * hd] for g in range(C_GROUP)], axis=0))
        sk = jnp.zeros((C_GROUP * blk, 1), f32)
        for g in range(C_GROUP):
            sk = jnp.where(rblk == g, sink_ref[kvh * C_GROUP + g] * LOG2E, sk)
        sinks.append(sk)
    ss = [_dot_nt(q4s[h], kcat[:, h * hd:(h + 1) * hd]) + bias_ref[0, h] for h in range(C_KV_HEADS)]
    ms = [jnp.maximum(jnp.max(s, axis=-1, keepdims=True), sk) for s, sk in zip(ss, sinks)]
    ps = [jnp.exp2(s - m).astype(bf16) for s, m in zip(ss, ms)]
    pvs = [_dot(ps[h], jnp.concatenate([vcat[:, h * hd:(h + 1) * hd], ones_pad], axis=1)) for h in range(C_KV_HEADS)]
    outs = []
    for h in range(C_KV_HEADS):
        den = pvs[h][:, hd:hd + 1] + jnp.exp2(sinks[h] - ms[h])
        o4 = pvs[h][:, 0:hd] / den
        outs += [o4[g * blk:(g + 1) * blk, :] for g in range(C_GROUP)]
    o_ref[0] = jnp.concatenate(outs, axis=1).astype(o_ref.dtype)


def _swa(qkv, sinks):
    B, L, _ = qkv.shape
    nb = L // C_BLOCK
    dq = C_HEADS * C_HEAD_DIM
    dkv = C_KV_HEADS * C_HEAD_DIM
    k_col = dq // dkv
    v_col = k_col + 1

    def kv_spec(col, shift):
        return pl.BlockSpec((1, C_BLOCK, dkv),
                            lambda n, b: (b, jnp.clip(n + shift, 0, nb - 1), col))

    def edge_case(n, b):
        return (jnp.where(n == 0, 0, jnp.where(n == nb - 1, 2, 1)), 0, 0, 0)

    return pl.pallas_call(
        _swa_kernel,
        grid=(nb, B),
        in_specs=[pl.BlockSpec(memory_space=pltpu.SMEM),
                  pl.BlockSpec((1, C_BLOCK, dq), lambda n, b: (b, n, 0)),
                  kv_spec(k_col, -1), kv_spec(k_col, 0), kv_spec(k_col, 1),
                  kv_spec(v_col, -1), kv_spec(v_col, 0), kv_spec(v_col, 1),
                  pl.BlockSpec((1, C_KV_HEADS, C_GROUP * C_BLOCK, 3 * C_BLOCK), edge_case)],
        out_specs=pl.BlockSpec((1, C_BLOCK, dq), lambda n, b: (b, n, 0)),
        out_shape=jax.ShapeDtypeStruct((B, L, dq), bf16),
        compiler_params=_cparams("parallel", "parallel"),
        name="swa",
    )(sinks, qkv, qkv, qkv, qkv, qkv, qkv, qkv, _swa_bias_tables())


def _hyena_filters(L, w1, b1, w2, b2, w3, b3, w4, freq):
    hp = lax.Precision.HIGHEST
    t = jnp.linspace(0.0, 1.0, L, dtype=f32)[:, None]
    wpos = (2.0 * math.pi / L) * jnp.arange(L, dtype=f32)[:, None]
    bands = jnp.linspace(1e-4, HY_BANDS - 1, HY_BANDS, dtype=f32)[None, :]
    feats = jnp.concatenate([t, jnp.cos(bands * wpos), -jnp.sin(bands * wpos)], axis=-1)
    h = jnp.sin(freq[0] * (jnp.dot(feats, w1, precision=hp) + b1))
    h = jnp.sin(freq[1] * (jnp.dot(h, w2, precision=hp) + b2))
    h = jnp.sin(freq[2] * (jnp.dot(h, w3, precision=hp) + b3))
    h = jnp.dot(h, w4, precision=hp).reshape(L, HY_ORDER, 2, B_WIDTH)
    deltas = jnp.abs(jnp.linspace(HY_MIN_DECAY, HY_MAX_DECAY, B_WIDTH, dtype=f32))
    window = jnp.exp(-t * deltas[None, :])
    return h * window[:, None, None, :]


def _dft_tables(L):
    n_fft = 2 * L
    idx = jnp.arange(L, dtype=jnp.int32)
    ph = (idx[:, None] * idx[None, :]) % n_fft
    ang = ph.astype(f32) * (2.0 * math.pi / n_fft)
    return jnp.cos(ang).astype(bf16), jnp.sin(ang).astype(bf16)


def _pad_rows(w, rows=8):
    return jnp.pad(w, ((0, rows - w.shape[0]), (0, 0)))


def _even_in_weights(w_in):
    hk = A_HEADS * A_DK
    q = w_in[:, 0:hk].reshape(D_MODEL, A_HEADS, A_DK)
    k = w_in[:, hk:2 * hk].reshape(D_MODEL, A_HEADS, A_DK)
    v = w_in[:, 2 * hk:A_QKV].reshape(D_MODEL, A_HEADS, A_DV)
    z = w_in[:, A_QKV:A_QKV + A_WIDTH].reshape(D_MODEL, A_HEADS, A_DV)
    qkvz = jnp.concatenate([q, k, v, z], axis=-1).reshape(D_MODEL, A_HEADS * 4 * A_DK)
    ab = w_in[:, A_QKV + A_WIDTH:A_QKV + A_WIDTH + 4 * A_HEADS].reshape(D_MODEL, 4, A_HEADS)
    ab = jnp.transpose(ab, (0, 2, 1))
    ab = jnp.pad(ab, ((0, 0), (0, 0), (0, LANE - 4))).reshape(D_MODEL, A_HEADS * LANE)
    hx = w_in[:, A_QKV + A_WIDTH + 4 * A_HEADS:]
    return jnp.concatenate([qkvz, ab, hx], axis=1).astype(bf16)


def _router_weights(w_group, w_expert):
    w = jnp.concatenate([w_group, w_expert], axis=1)
    return jnp.pad(w, ((0, 0), (0, ROUTER_LANES - w.shape[1])))


def kernel(x_prompt, x_sample, norm_mix, norm_ffn, norm_final, ev_w_in, ev_conv_a, ev_alog_f, ev_alog_b, ev_dtb_f, ev_dtb_b, ev_onorm, ev_conv_b, hy_w1, hy_b1, hy_w2, hy_b2, hy_w3, hy_b3, hy_w4, hy_freq, hy_dbias, ev_w_out, od_w_qkv, od_sinks, od_w_out, moe_w_group, moe_w_expert, moe_w_gate, moe_w_up, moe_w_down):
    bp = x_prompt.shape[0]
    L = x_prompt.shape[1]
    x = jnp.concatenate([x_prompt, x_sample], axis=0)
    B = x.shape[0]
    T = B * L
    x = x.reshape(T, D_MODEL)

    w_in = _even_in_weights(ev_w_in[0])
    n_qkvz = A_HEADS * 4 * A_DK
    n_ab = A_HEADS * LANE
    n_hx = 3 * B_WIDTH
    qkvz, abh, hx = _norm_matmul(x, norm_mix[0], w_in, (n_qkvz, n_ab, n_hx), (bf16, f32, bf16))

    hk = A_HEADS * A_DK
    ca = ev_conv_a[0]
    conv_a = jnp.concatenate([ca[0:hk].reshape(A_HEADS, A_DK, 3), ca[hk:2 * hk].reshape(A_HEADS, A_DK, 3),
                              ca[2 * hk:].reshape(A_HEADS, A_DV, 3)], axis=1)
    conv_a = jnp.pad(jnp.transpose(conv_a, (0, 2, 1)), ((0, 0), (0, 5), (0, 0)))
    gate_sc = jnp.stack([ev_alog_f[0], ev_alog_b[0], ev_dtb_f[0], ev_dtb_b[0]], axis=1)
    o_a = _deltanet(qkvz.reshape(B, L, n_qkvz), abh.reshape(B, L, n_ab), conv_a, gate_sc, ev_onorm[0])

    filt = _hyena_filters(L, hy_w1[0], hy_b1[0], hy_w2[0], hy_b2[0], hy_w3[0], hy_b3[0], hy_w4[0], hy_freq[0])
    hsum = (filt[:, :, 0] + filt[:, :, 1]).reshape(L, HY_ORDER * B_WIDTH)
    hdiff = (filt[:, :, 0] - filt[:, :, 1]).reshape(L, HY_ORDER * B_WIDTH)
    cmat, smat = _dft_tables(L)
    kr, ki, kn = _filter_spectrum(cmat, smat, hsum, hdiff)
    conv_b = _pad_rows(ev_conv_b[0].T)
    o_b = _hyena(hx.reshape(B, L, n_hx), conv_b, cmat, smat, kr, ki, kn, _pad_rows(hy_dbias[0]))

    w_out = ev_w_out[0].astype(bf16)
    x, hp, route, counts = _proj_route(x, (o_a.reshape(T, A_WIDTH), o_b.reshape(T, B_WIDTH)),
                                       (w_out[:A_WIDTH], w_out[A_WIDTH:]), norm_ffn[0],
                                       _router_weights(moe_w_group[0], moe_w_expert[0]))
    x = _moe(x, hp, route, counts, moe_w_gate[0], moe_w_up[0], moe_w_down[0], norm_final, final_norm=False)

    n_qkv = od_w_qkv.shape[-1]
    dq = C_HEADS * C_HEAD_DIM
    w_qkv = jnp.concatenate([od_w_qkv[0][:, :dq] * (LOG2E * C_HEAD_DIM ** -0.5), od_w_qkv[0][:, dq:]], axis=1)
    (qkv,) = _norm_matmul(x, norm_mix[1], w_qkv.astype(bf16), (n_qkv,), (bf16,))
    o_c = _swa(qkv.reshape(B, L, n_qkv), od_sinks[0])
    x, hp, route, counts = _proj_route(x, (o_c.reshape(T, C_HEADS * C_HEAD_DIM),), (od_w_out[0].astype(bf16),),
                                       norm_ffn[1], _router_weights(moe_w_group[1], moe_w_expert[1]))
    y = _moe(x, hp, route, counts, moe_w_gate[1], moe_w_up[1], moe_w_down[1], norm_final, final_norm=True)
    y = y.reshape(B, L, D_MODEL)
    return (y[:bp], y[bp:])
```

```python
import functools
import math

import jax
import jax.numpy as jnp
import numpy as np
from jax import lax
from jax.experimental import pallas as pl
from jax.experimental.pallas import tpu as pltpu
from jax.experimental.pallas import tpu_sc as plsc

f32 = jnp.float32
bf16 = jnp.bfloat16

EPS = 1e-6
D_MODEL = 1024

A_HEADS = 4
A_DK = 128
A_DV = 128
A_WIDTH = A_HEADS * A_DV
A_QKV = 2 * A_HEADS * A_DK + A_HEADS * A_DV
DELTA_CHUNK = 128
INV_BASE = 8
DELTA_GROUP = 8

B_WIDTH = D_MODEL - A_WIDTH
HY_ORDER = 2
HY_EMB = 33
HY_BANDS = (HY_EMB - 1) // 2
HY_TARGET = 1e-2
HY_MIN_DECAY = math.log(HY_TARGET) / 1.5
HY_MAX_DECAY = math.log(HY_TARGET) / 0.3
HY_CT = 256
HY_ROWS = 256

C_HEADS = 16
C_KV_HEADS = 4
C_HEAD_DIM = 64
C_GROUP = C_HEADS // C_KV_HEADS
WINDOW = 128
C_BLOCK = 128

N_GROUPS = 4
EXPERTS_PER_GROUP = 8
N_EXPERTS = N_GROUPS * EXPERTS_PER_GROUP
D_EXPERT = 256
ROUTER_LANES = 128

LANE = 128
VMEM_LIMIT = 56 * 1024 * 1024


def _cparams(*sem):
    return pltpu.CompilerParams(dimension_semantics=sem, vmem_limit_bytes=VMEM_LIMIT)


def _dot(a, b):
    return jnp.dot(a, b, preferred_element_type=f32)


def _dot_nt(a, b):
    return lax.dot_general(a, b, (((1,), (1,)), ((), ())), preferred_element_type=f32)


def _dot_tn(a, b):
    return lax.dot_general(a, b, (((0,), (0,)), ((), ())), preferred_element_type=f32)


def _sigmoid(x):
    return 1.0 / (1.0 + jnp.exp(-x))


def _rms(x, g):
    return x * lax.rsqrt(jnp.mean(x * x, axis=-1, keepdims=True) + EPS) * g


def _norm_matmul_kernel(x_ref, g_ref, w_ref, *o_refs):
    h = _rms(x_ref[...], g_ref[...]).astype(bf16)
    off = 0
    for o_ref in o_refs:
        n = o_ref.shape[-1]
        o_ref[...] = _dot(h, w_ref[:, off:off + n]).astype(o_ref.dtype)
        off += n


def _norm_matmul(x, gain, w, splits, dtypes, tm=512):
    T, D = x.shape
    n_all = w.shape[1]
    return pl.pallas_call(
        _norm_matmul_kernel,
        grid=(T // tm,),
        in_specs=[pl.BlockSpec((tm, D), lambda i: (i, 0)),
                  pl.BlockSpec((1, D), lambda i: (0, 0)),
                  pl.BlockSpec((D, n_all), lambda i: (0, 0))],
        out_specs=[pl.BlockSpec((tm, n), lambda i: (i, 0)) for n in splits],
        out_shape=[jax.ShapeDtypeStruct((T, n), dt) for n, dt in zip(splits, dtypes)],
        compiler_params=_cparams("parallel"),
    )(x, gain.reshape(1, D), w)


MK_LOWER, MK_UPPER, MK_BASE, MK_LEVEL0 = 0, 1, 2, 3
N_LEVELS = int(math.log2(DELTA_CHUNK // INV_BASE))
MK_EYE = MK_LEVEL0 + N_LEVELS
N_MASKS = MK_EYE + 1


def _delta_masks():
    c = np.arange(DELTA_CHUNK)[:, None]
    s = np.arange(DELTA_CHUNK)[None, :]
    m = [s <= c, s >= c, c // INV_BASE == s // INV_BASE]
    b = INV_BASE
    while b < DELTA_CHUNK:
        m.append((c // (2 * b) == s // (2 * b)) & (c // b != s // b))
        b *= 2
    m.append(c == s)
    return jnp.asarray(np.stack(m).astype(np.float32))


def _tri_inverse(mats, mk_ref):
    ps = [a * mk_ref[MK_BASE] for a in mats]
    ts = [mk_ref[MK_EYE] - p for p in ps]
    n = 2
    while n < INV_BASE:
        p16s = [p.astype(bf16) for p in ps]
        ps = [_dot(p16, p16) for p16 in p16s]
        ts = [t + _dot(t.astype(bf16), p.astype(bf16)) for t, p in zip(ts, ps)]
        n *= 2
    for lvl in range(N_LEVELS):
        t16s = [t.astype(bf16) for t in ts]
        xs = [_dot((a * mk_ref[MK_LEVEL0 + lvl]).astype(bf16), t16) for a, t16 in zip(mats, t16s)]
        ts = [t - _dot(t16, x.astype(bf16)) for t, t16, x in zip(ts, t16s, xs)]
    return ts


def _deltanet_kernel(sc_ref, x_ref, ab_ref, cw_ref, on_ref, mk_ref, o_ref, mp_s, nr_s, gl_s, o_s):
    hd = pl.program_id(1)
    L = x_ref.shape[1]
    C = DELTA_CHUNK
    nchunk = L // C
    group = math.gcd(DELTA_GROUP, nchunk)
    nqkv = 3 * A_DK
    halo = 16

    lane = lax.broadcasted_iota(jnp.int32, (1, LANE), 1)
    alog = jnp.where(lane == 2, sc_ref[hd, 0], jnp.where(lane == 3, sc_ref[hd, 1], 0.0))
    dtb = jnp.where(lane == 2, sc_ref[hd, 2], jnp.where(lane == 3, sc_ref[hd, 3], 0.0))
    neg_a = jnp.where((lane == 2) | (lane == 3), -jnp.exp(alog), 0.0)
    rowi = lax.broadcasted_iota(jnp.int32, (C, 1), 0)
    cw = cw_ref[0]

    def chunk_inputs(i):
        r0 = pl.multiple_of(i * C, C)
        xc = x_ref[0, pl.ds(r0, C), 0:nqkv].astype(f32)
        rp = pl.multiple_of(jnp.maximum(r0 - halo, 0), halo)
        rn = pl.multiple_of(jnp.minimum(r0 + C, L - halo), halo)
        prow = x_ref[0, pl.ds(rp, halo), 0:nqkv][halo - 1:halo, :].astype(f32)
        nrow = x_ref[0, pl.ds(rn, halo), 0:nqkv][0:1, :].astype(f32)
        prow = jnp.where(i > 0, prow, 0.0)
        nrow = jnp.where(i < nchunk - 1, nrow, 0.0)
        prev = jnp.where(rowi == 0, prow, pltpu.roll(xc, 1, 0))
        nxt = jnp.where(rowi == C - 1, nrow, pltpu.roll(xc, C - 1, 0))
        y = prev * cw[0:1, :] + xc * cw[1:2, :] + nxt * cw[2:3, :]
        y = y * _sigmoid(y)
        q = y[:, 0:A_DK]
        k = y[:, A_DK:2 * A_DK]
        vc = y[:, 2 * A_DK:nqkv]
        qc = q * lax.rsqrt(jnp.sum(q * q, axis=-1, keepdims=True) + EPS) * (A_DK ** -0.5)
        kc = k * lax.rsqrt(jnp.sum(k * k, axis=-1, keepdims=True) + EPS)

        abc = ab_ref[0, pl.ds(r0, C), :]
        zg = abc + dtb
        g = neg_a * (jnp.maximum(zg, 0.0) + jnp.log(1.0 + jnp.exp(-jnp.abs(zg))))
        low16 = mk_ref[MK_LOWER].astype(bf16)
        g1 = g.astype(bf16)
        e1 = g - g1.astype(f32)
        g2 = e1.astype(bf16)
        g3 = (e1 - g2.astype(f32)).astype(bf16)
        pre = _dot(low16, g1) + _dot(low16, g2) + _dot(low16, g3)
        tot = pre[C - 1:C, :]
        suf = tot - pre + g
        beta_all = _sigmoid(abc)

        k16 = kc.astype(bf16)
        kk = _dot_nt(k16, k16)
        qk = _dot_nt(qc.astype(bf16), k16)

        chains = []
        for rev in (0, 1):
            beta = beta_all[:, rev:rev + 1]
            gcc = (suf if rev else pre)[:, 2 + rev:3 + rev]
            tot11 = tot[:, 2 + rev:3 + rev]
            keep = mk_ref[MK_UPPER if rev else MK_LOWER]
            bc = jnp.broadcast_to(gcc, (C, C))
            decay = jnp.exp((bc - bc.T) * keep) * keep
            a = (beta * kk) * (decay * (keep - mk_ref[MK_EYE]))
            eg = jnp.exp(gcc)
            chains.append(dict(
                i=i, rev=rev, a=a, gl=jnp.exp(tot11), qd=qc * eg,
                rhs=jnp.concatenate([vc * beta, kc * (beta * eg)], axis=1).astype(bf16),
                kd16=(kc * jnp.exp(tot11 - gcc)).astype(bf16),
                qk16=(qk * decay).astype(bf16)))
        return chains

    def prep(it, carry):
        chains = []
        for gi in range(group):
            chains += chunk_inputs(it * group + gi)
        ts = _tri_inverse([c["a"] for c in chains], mk_ref)
        uws = [_dot(t.astype(bf16), c["rhs"]).astype(bf16) for t, c in zip(ts, chains)]
        kts = [_dot_tn(c["kd16"], uw) for c, uw in zip(chains, uws)]
        prs = [_dot(c["qk16"], uw) for c, uw in zip(chains, uws)]
        for c, kt, pr in zip(chains, kts, prs):
            rev, i = c["rev"], c["i"]
            mp_s[rev, i, 0:A_DK, :] = (-kt[:, A_DV:]).astype(bf16)
            mp_s[rev, i, A_DK:A_DK + C, :] = (c["qd"] - pr[:, A_DV:]).astype(bf16)
            nr_s[rev, i, 0:A_DK, :] = kt[:, 0:A_DV]
            nr_s[rev, i, A_DK:A_DK + C, :] = pr[:, 0:A_DV]
            gl_s[rev, i] = jnp.broadcast_to(c["gl"], (8, LANE))
        return carry

    lax.fori_loop(0, nchunk // group, prep, 0)

    def scan(i, carry):
        out = []
        for rev, S in enumerate(carry):
            j = (nchunk - 1 - i) if rev else i
            y = _dot(mp_s[rev, j], S.astype(bf16))
            nr = nr_s[rev, j]
            o_s[rev, pl.ds(pl.multiple_of(j * C, C), C), :] = y[A_DK:, :] + nr[A_DK:, :]
            out.append(S * gl_s[rev, j][0:1, :] + y[0:A_DK, :] + nr[0:A_DK, :])
        return tuple(out)

    S0 = jnp.zeros((A_DK, A_DV), f32)
    lax.fori_loop(0, nchunk, scan, (S0, S0))

    def finish(i, carry):
        r0 = pl.multiple_of(i * C, C)
        o = o_s[0, pl.ds(r0, C), :] + o_s[1, pl.ds(r0, C), :]
        zg = x_ref[0, pl.ds(r0, C), nqkv:nqkv + A_DV].astype(f32)
        o_ref[0, pl.ds(r0, C), :] = (_rms(o, on_ref[...]) * (zg * _sigmoid(zg))).astype(o_ref.dtype)
        return carry

    lax.fori_loop(0, nchunk, finish, 0)


def _deltanet(qkvz, abh, conv_w, gate_sc, onorm):
    B, L, _ = qkvz.shape
    C = DELTA_CHUNK
    nchunk = L // C
    hw = 4 * A_DK
    scratch = [pltpu.VMEM((2, nchunk, A_DK + C, A_DV), bf16),
               pltpu.VMEM((2, nchunk, A_DK + C, A_DV), f32),
               pltpu.VMEM((2, nchunk, 8, LANE), f32),
               pltpu.VMEM((2, L, A_DV), f32)]
    return pl.pallas_call(
        _deltanet_kernel,
        grid=(B, A_HEADS),
        in_specs=[pl.BlockSpec(memory_space=pltpu.SMEM),
                  pl.BlockSpec((1, L, hw), lambda b, h: (b, 0, h)),
                  pl.BlockSpec((1, L, LANE), lambda b, h: (b, 0, h)),
                  pl.BlockSpec((1, 8, 3 * A_DK), lambda b, h: (h, 0, 0)),
                  pl.BlockSpec((1, A_DV), lambda b, h: (0, 0)),
                  pl.BlockSpec((N_MASKS, C, C), lambda b, h: (0, 0, 0))],
        out_specs=pl.BlockSpec((1, L, A_DV), lambda b, h: (b, 0, h)),
        out_shape=jax.ShapeDtypeStruct((B, L, A_WIDTH), bf16),
        scratch_shapes=scratch,
        compiler_params=_cparams("parallel", "parallel"),
        name="deltanet",
    )(gate_sc, qkvz, abh, conv_w, onorm.reshape(1, A_DV), _delta_masks())


def _filter_spectrum_kernel(c_ref, s_ref, hs_ref, hd_ref, kr_ref, ki_ref, kn_ref):
    L = c_ref.shape[0]
    n_fft = 2 * L
    hs = hs_ref[...]
    hd = hd_ref[...]

    def split_dot(m_ref, x):
        hi = x.astype(bf16)
        lo = (x - hi.astype(f32)).astype(bf16)
        m = m_ref[...]
        return _dot(m, hi) + _dot(m, lo)

    row = lax.broadcasted_iota(jnp.int32, (L, 1), 0)
    scale = jnp.where(row == 0, 1.0 / n_fft, 2.0 / n_fft)
    kr_ref[...] = split_dot(c_ref, hs) * scale
    ki_ref[...] = split_dot(s_ref, hd) * scale
    alt = jnp.where(row % 2 == 0, 1.0, -1.0)
    kn = jnp.sum(hs * alt, axis=0, keepdims=True) * (1.0 / n_fft)
    kn_ref[...] = jnp.broadcast_to(kn, kn_ref.shape)


def _filter_spectrum(cmat, smat, hsum, hdiff):
    L, n = hsum.shape
    ct = HY_CT
    return pl.pallas_call(
        _filter_spectrum_kernel,
        grid=(n // ct,),
        in_specs=[pl.BlockSpec((L, L), lambda j: (0, 0)),
                  pl.BlockSpec((L, L), lambda j: (0, 0)),
                  pl.BlockSpec((L, ct), lambda j: (0, j)),
                  pl.BlockSpec((L, ct), lambda j: (0, j))],
        out_specs=[pl.BlockSpec((L, ct), lambda j: (0, j)),
                   pl.BlockSpec((L, ct), lambda j: (0, j)),
                   pl.BlockSpec((8, ct), lambda j: (0, j))],
        out_shape=[jax.ShapeDtypeStruct((L, n), f32),
                   jax.ShapeDtypeStruct((L, n), f32),
                   jax.ShapeDtypeStruct((8, n), f32)],
        compiler_params=_cparams("parallel"),
    )(cmat, smat, hsum, hdiff)


def _hyena_kernel(c_ref, s_ref, x1_ref, x2_ref, v_ref, w1_ref, w2_ref, wv_ref,
                  kr0_ref, ki0_ref, kn0_ref, kr1_ref, ki1_ref, kn1_ref, db_ref, o_ref,
                  z_s, z16_s, yr_s, yi_s, nyq_s):
    L = c_ref.shape[0]
    R = HY_ROWS
    nt = L // R
    halo = 16
    rowi = lax.broadcasted_iota(jnp.int32, (R, 1), 0)
    alt = jnp.where(rowi % 2 == 0, 1.0, -1.0)

    def dwconv_rows(x_ref, w_ref, t, r0):
        x = x_ref[0, pl.ds(r0, R), :].astype(f32)
        rp = pl.multiple_of(jnp.maximum(r0 - halo, 0), halo)
        rn = pl.multiple_of(jnp.minimum(r0 + R, L - halo), halo)
        prow = jnp.where(t > 0, x_ref[0, pl.ds(rp, halo), :][halo - 1:halo, :].astype(f32), 0.0)
        nrow = jnp.where(t < nt - 1, x_ref[0, pl.ds(rn, halo), :][0:1, :].astype(f32), 0.0)
        prev = jnp.where(rowi == 0, prow, pltpu.roll(x, 1, 0))
        nxt = jnp.where(rowi == R - 1, nrow, pltpu.roll(x, R - 1, 0))
        w = w_ref[...]
        return prev * w[0:1, :] + x * w[1:2, :] + nxt * w[2:3, :]

    def load_v(t, acc):
        r0 = pl.multiple_of(t * R, R)
        z = dwconv_rows(v_ref, wv_ref, t, r0)
        z_s[pl.ds(r0, R), :] = z
        z16_s[pl.ds(r0, R), :] = z.astype(bf16)
        return acc + jnp.sum(z * alt, axis=0, keepdims=True)

    nyq = lax.fori_loop(0, nt, load_v, jnp.zeros((1, z_s.shape[1]), f32))

    stages = ((x1_ref, w1_ref, kr0_ref, ki0_ref, kn0_ref), (x2_ref, w2_ref, kr1_ref, ki1_ref, kn1_ref))
    for o, (g_ref, gw_ref, kr_ref, ki_ref, kn_ref) in enumerate(stages):
        nyq_s[...] = jnp.broadcast_to(nyq * kn_ref[0:1, :], nyq_s.shape)

        def spectrum(t, carry):
            r0 = pl.multiple_of(t * R, R)
            z16 = z16_s[...]
            ur = _dot(c_ref[pl.ds(r0, R), :], z16)
            ui = _dot(s_ref[pl.ds(r0, R), :], z16)
            kr = kr_ref[pl.ds(r0, R), :]
            ki = ki_ref[pl.ds(r0, R), :]
            yr_s[pl.ds(r0, R), :] = (ur * kr - ui * ki).astype(bf16)
            yi_s[pl.ds(r0, R), :] = (ur * ki + ui * kr).astype(bf16)
            return carry

        lax.fori_loop(0, nt, spectrum, 0)

        def signal(t, acc):
            r0 = pl.multiple_of(t * R, R)
            y = _dot(c_ref[pl.ds(r0, R), :], yr_s[...]) + _dot(s_ref[pl.ds(r0, R), :], yi_s[...])
            y = y + alt * nyq_s[0:1, :]
            z = dwconv_rows(g_ref, gw_ref, t, r0) * (y + z_s[pl.ds(r0, R), :] * db_ref[o:o + 1, :])
            if o + 1 < len(stages):
                z_s[pl.ds(r0, R), :] = z
                z16_s[pl.ds(r0, R), :] = z.astype(bf16)
            else:
                o_ref[0, pl.ds(r0, R), :] = z.astype(o_ref.dtype)
            return acc + jnp.sum(z * alt, axis=0, keepdims=True)

        nyq = lax.fori_loop(0, nt, signal, jnp.zeros((1, z_s.shape[1]), f32))


def _hyena(hx, conv_w, cmat, smat, kr, ki, kn, dbias):
    B, L, _ = hx.shape
    ct = HY_CT
    nct = B_WIDTH // ct
    once = pl.Buffered(1)

    def xspec(part):
        return pl.BlockSpec((1, L, ct), lambda c, b: (b, 0, part * nct + c))

    def wspec(part):
        return pl.BlockSpec((8, ct), lambda c, b: (0, part * nct + c))

    def kspec(order, rows):
        return pl.BlockSpec((rows, ct), lambda c, b: (0, order * nct + c), pipeline_mode=once)

    return pl.pallas_call(
        _hyena_kernel,
        grid=(nct, B),
        in_specs=[pl.BlockSpec((L, L), lambda c, b: (0, 0), pipeline_mode=once),
                  pl.BlockSpec((L, L), lambda c, b: (0, 0), pipeline_mode=once),
                  xspec(0), xspec(1), xspec(2), wspec(0), wspec(1), wspec(2),
                  kspec(0, L), kspec(0, L), kspec(0, 8), kspec(1, L), kspec(1, L), kspec(1, 8),
                  pl.BlockSpec((8, ct), lambda c, b: (0, c))],
        out_specs=pl.BlockSpec((1, L, ct), lambda c, b: (b, 0, c)),
        out_shape=jax.ShapeDtypeStruct((B, L, B_WIDTH), bf16),
        scratch_shapes=[pltpu.VMEM((L, ct), f32), pltpu.VMEM((L, ct), bf16),
                        pltpu.VMEM((L, ct), bf16), pltpu.VMEM((L, ct), bf16), pltpu.VMEM((8, ct), f32)],
        compiler_params=_cparams("parallel", "parallel"),
        name="hyena",
    )(cmat, smat, hx, hx, hx, conv_w, conv_w, conv_w, kr, ki, kn, kr, ki, kn, dbias)


RT_E1, RT_E2, RT_W1, RT_W2, RT_R1, RT_R2 = 0, 1, 2, 3, 4, 5
ROUTE_TM = 512
EXPERT_TM = 256
SC_WINDOW = 128


def _pack_bf16_pairs(x):
    n = x.shape[1] // 2
    bits = lax.bitcast_convert_type(x.astype(bf16).astype(f32), jnp.uint32)
    return bits[:, :n] | (bits[:, n:] >> 16)


def _unpack_bf16_pairs(p):
    hi = lax.bitcast_convert_type(p & jnp.uint32(0xFFFF0000), f32)
    lo = lax.bitcast_convert_type(p << 16, f32)
    return hi, lo


def _route(logits, low_ref):
    lane = lax.broadcasted_iota(jnp.int32, logits.shape, 1)
    neg = -jnp.inf
    big = ROUTER_LANES
    gl = jnp.where(lane < N_GROUPS, logits, neg)
    gmax = jnp.max(gl, axis=-1, keepdims=True)
    gsel = jnp.min(jnp.where(gl == gmax, lane, big), axis=-1, keepdims=True)
    psel = 1.0 / jnp.sum(jnp.exp(gl - gmax), axis=-1, keepdims=True)
    first = N_GROUPS + gsel * EXPERTS_PER_GROUP
    el = jnp.where((lane >= first) & (lane < first + EXPERTS_PER_GROUP), logits, neg)
    t1 = jnp.max(el, axis=-1, keepdims=True)
    i1 = jnp.min(jnp.where(el == t1, lane, big), axis=-1, keepdims=True)
    el2 = jnp.where(lane == i1, neg, el)
    t2 = jnp.max(el2, axis=-1, keepdims=True)
    i2 = jnp.min(jnp.where(el2 == t2, lane, big), axis=-1, keepdims=True)
    ex = jnp.exp(t2 - t1)
    w1 = psel / (1.0 + ex)
    w2 = ex * w1
    e1 = i1 - N_GROUPS
    e2 = i2 - N_GROUPS
    onehot = jnp.where((lane == e1) | (lane == e2), 1.0, 0.0)
    before = _dot(low_ref[...], onehot.astype(bf16))
    r1 = jnp.sum(jnp.where(lane == e1, before, 0.0), axis=-1, keepdims=True)
    r2 = jnp.sum(jnp.where(lane == e2, before, 0.0), axis=-1, keepdims=True)
    tm = logits.shape[0]
    counts = before[tm - 1:tm, :] + onehot[tm - 1:tm, :]
    rec = jnp.zeros(logits.shape, f32)
    for ln, val in ((RT_E1, e1.astype(f32)), (RT_E2, e2.astype(f32)), (RT_W1, w1), (RT_W2, w2), (RT_R1, r1), (RT_R2, r2)):
        rec = jnp.where(lane == ln, val, rec)
    return rec, counts


def _proj_route_kernel(*refs, n_in):
    x_ref = refs[0]
    a_refs = refs[1:1 + n_in]
    w_refs = refs[1 + n_in:1 + 2 * n_in]
    g_ref, wr_ref, low_ref, xo_ref, hpa_ref, hpb_ref, rt_ref, cnt_ref = refs[1 + 2 * n_in:]
    x = x_ref[...]
    for a_ref, w_ref in zip(a_refs, w_refs):
        x = x + _dot(a_ref[...], w_ref[...])
    xo_ref[...] = x
    h = _rms(x, g_ref[...])
    hp = _pack_bf16_pairs(h)
    quarter = hp.shape[1] // 2
    hpa_ref[...] = hp[:, :quarter]
    hpb_ref[...] = hp[:, quarter:]
    h_hi = h.astype(bf16)
    h_lo = (h - h_hi.astype(f32)).astype(bf16)
    both = _dot(h_hi, wr_ref[...])
    logits = both[:, :ROUTER_LANES] + both[:, ROUTER_LANES:] + _dot(h_lo, wr_ref[:, :ROUTER_LANES])
    rec, counts = _route(logits, low_ref)
    rt_ref[...] = rec
    cnt_ref[0] = jnp.broadcast_to(counts, cnt_ref.shape[1:])


def _proj_route(x, acts, ws, gain, w_router):
    T, D = x.shape
    tm = ROUTE_TM
    n_in = len(acts)
    low = jnp.asarray(np.tril(np.ones((tm, tm), np.float32), -1), dtype=bf16)
    in_specs = [pl.BlockSpec((tm, D), lambda i: (i, 0))]
    in_specs += [pl.BlockSpec((tm, a.shape[1]), lambda i: (i, 0)) for a in acts]
    in_specs += [pl.BlockSpec(w.shape, lambda i: (0, 0)) for w in ws]
    in_specs += [pl.BlockSpec((1, D), lambda i: (0, 0)),
                 pl.BlockSpec((D, 2 * ROUTER_LANES), lambda i: (0, 0)),
                 pl.BlockSpec((tm, tm), lambda i: (0, 0))]
    return pl.pallas_call(
        functools.partial(_proj_route_kernel, n_in=n_in),
        grid=(T // tm,),
        in_specs=in_specs,
        out_specs=[pl.BlockSpec((tm, D), lambda i: (i, 0)),
                   pl.BlockSpec((tm, D // 4), lambda i: (i, 0)),
                   pl.BlockSpec((tm, D // 4), lambda i: (i, 0)),
                   pl.BlockSpec((tm, ROUTER_LANES), lambda i: (i, 0)),
                   pl.BlockSpec((1, 8, ROUTER_LANES), lambda i: (i, 0, 0))],
        out_shape=[jax.ShapeDtypeStruct((T, D), f32),
                   jax.ShapeDtypeStruct((T, D // 4), jnp.uint32),
                   jax.ShapeDtypeStruct((T, D // 4), jnp.uint32),
                   jax.ShapeDtypeStruct((T, ROUTER_LANES), f32),
                   jax.ShapeDtypeStruct((T // tm, 8, ROUTER_LANES), f32)],
        compiler_params=_cparams("parallel"),
        name="proj_route",
    )(x, *acts, *ws, gain.reshape(1, D), w_router, low)


def _moe_rows(T):
    return 2 * T + N_EXPERTS * EXPERT_TM


def _moe_plan(route, counts):
    T = route.shape[0]
    tm = ROUTE_TM
    e = route[:, RT_E1:RT_E2 + 1].astype(jnp.int32)
    rank = route[:, RT_R1:RT_R2 + 1].astype(jnp.int32)
    cnt = counts[:, 0, :N_EXPERTS].astype(jnp.int32)
    total = jnp.sum(cnt, axis=0)
    padded = ((total + EXPERT_TM - 1) // EXPERT_TM) * EXPERT_TM
    ends = jnp.cumsum(padded)
    tile_base = (ends - padded)[None, :] + jnp.cumsum(cnt, axis=0) - cnt
    base = jnp.broadcast_to(tile_base[:, None, :], (T // tm, tm, N_EXPERTS)).reshape(T, N_EXPERTS)
    ids = jnp.arange(N_EXPERTS, dtype=jnp.int32)[None, :]
    dest = [jnp.sum(jnp.where(ids == e[:, j:j + 1], base, 0), axis=1) + rank[:, j] for j in (0, 1)]
    n_tiles = _moe_rows(T) // EXPERT_TM
    n_used = ends[-1] // EXPERT_TM
    starts = jnp.arange(n_tiles, dtype=jnp.int32) * EXPERT_TM
    tile_expert = jnp.sum((starts[:, None] >= ends[None, :]).astype(jnp.int32), axis=1)
    last_expert = jnp.sum((jnp.maximum(n_used - 1, 0) * EXPERT_TM >= ends).astype(jnp.int32))
    tile_expert = jnp.minimum(tile_expert, last_expert)
    return dest, tile_expert, n_used.reshape(1).astype(jnp.int32)


def _sc_mesh():
    return plsc.VectorSubcoreMesh(core_axis_name="core", subcore_axis_name="subcore")


def _sc_scatter_pair(rows, d0, d1, n_out):
    T, W = rows.shape
    win = SC_WINDOW

    @pl.kernel(out_type=jax.ShapeDtypeStruct((n_out, W), rows.dtype), mesh=_sc_mesh(), name="moe_sc_scatter")
    def scatter(x_hbm, i0_hbm, i1_hbm, o_hbm):
        def body(x_vmem, i0_vmem, i1_vmem):
            pltpu.sync_copy(x_vmem, o_hbm.at[i0_vmem.at[0]])
            pltpu.sync_copy(x_vmem, o_hbm.at[i1_vmem.at[0]])

        pltpu.emit_pipeline(
            body, grid=(T // win,),
            in_specs=[pl.BlockSpec((win, W), index_map=lambda i: (i, 0)),
                      pl.BlockSpec((1, win), index_map=lambda i: (0, i)),
                      pl.BlockSpec((1, win), index_map=lambda i: (0, i))],
            out_specs=[],
            core_axis_name=("core", "subcore"), dimension_semantics=(pltpu.PARALLEL,),
        )(x_hbm, i0_hbm, i1_hbm)

    return scatter(rows, d0.reshape(1, T), d1.reshape(1, T))


def _sc_gather(table, idx):
    n = idx.shape[0]
    W = table.shape[1]
    win = SC_WINDOW

    @pl.kernel(out_type=jax.ShapeDtypeStruct((n, W), table.dtype), mesh=_sc_mesh(), name="moe_sc_gather")
    def gather(t_hbm, i_hbm, o_hbm):
        def body(i_vmem, o_vmem):
            pltpu.sync_copy(t_hbm.at[i_vmem.at[0]], o_vmem)

        pltpu.emit_pipeline(
            body, grid=(n // win,),
            in_specs=[pl.BlockSpec((1, win), index_map=lambda i: (0, i))],
            out_specs=[pl.BlockSpec((win, W), index_map=lambda i: (i, 0))],
            core_axis_name=("core", "subcore"), dimension_semantics=(pltpu.PARALLEL,),
        )(i_hbm, o_hbm)

    return gather(table, idx.reshape(1, n))


def _expert_kernel(te_ref, nu_ref, xa_ref, xb_ref, wg_ref, wu_ref, wd_ref, ya_ref, yb_ref):
    @pl.when(pl.program_id(0) < nu_ref[0])
    def _():
        hi, lo = _unpack_bf16_pairs(jnp.concatenate([xa_ref[...], xb_ref[...]], axis=1))
        x = jnp.concatenate([hi, lo], axis=1).astype(bf16)
        gate = _dot(x, wg_ref[0].astype(bf16))
        up = _dot(x, wu_ref[0].astype(bf16))
        hg = (gate * _sigmoid(gate) * up).astype(bf16)
        y = _pack_bf16_pairs(_dot(hg, wd_ref[0].astype(bf16)))
        quarter = y.shape[1] // 2
        ya_ref[...] = y[:, :quarter]
        yb_ref[...] = y[:, quarter:]

    @pl.when(pl.program_id(0) >= nu_ref[0])
    def _():
        ya_ref[...] = jnp.zeros(ya_ref.shape, ya_ref.dtype)
        yb_ref[...] = jnp.zeros(yb_ref.shape, yb_ref.dtype)


def _experts(tile_expert, n_used, xa, xb, wg, wu, wd):
    n_rows, quarter = xa.shape
    D = 4 * quarter
    tm = EXPERT_TM
    grid_spec = pltpu.PrefetchScalarGridSpec(
        num_scalar_prefetch=2,
        grid=(n_rows // tm,),
        in_specs=[pl.BlockSpec((tm, quarter), lambda i, te, nu: (i, 0)),
                  pl.BlockSpec((tm, quarter), lambda i, te, nu: (i, 0)),
                  pl.BlockSpec((1, D, D_EXPERT), lambda i, te, nu: (te[i], 0, 0)),
                  pl.BlockSpec((1, D, D_EXPERT), lambda i, te, nu: (te[i], 0, 0)),
                  pl.BlockSpec((1, D_EXPERT, D), lambda i, te, nu: (te[i], 0, 0))],
        out_specs=[pl.BlockSpec((tm, quarter), lambda i, te, nu: (i, 0)),
                   pl.BlockSpec((tm, quarter), lambda i, te, nu: (i, 0))])
    return pl.pallas_call(
        _expert_kernel,
        grid_spec=grid_spec,
        out_shape=[jax.ShapeDtypeStruct((n_rows, quarter), jnp.uint32)] * 2,
        compiler_params=_cparams("arbitrary"),
        name="moe_experts",
    )(tile_expert, n_used, xa, xb, wg, wu, wd)


def _combine_kernel(x_ref, rt_ref, gf_ref, a1_ref, b1_ref, a2_ref, b2_ref, o_ref, *, final_norm):
    half = x_ref.shape[1] // 2
    h1, l1 = _unpack_bf16_pairs(jnp.concatenate([a1_ref[...], b1_ref[...]], axis=1))
    h2, l2 = _unpack_bf16_pairs(jnp.concatenate([a2_ref[...], b2_ref[...]], axis=1))
    w1 = rt_ref[:, RT_W1:RT_W1 + 1]
    w2 = rt_ref[:, RT_W2:RT_W2 + 1]
    out = jnp.concatenate([x_ref[:, 0:half] + w1 * h1 + w2 * h2, x_ref[:, half:] + w1 * l1 + w2 * l2], axis=1)
    if final_norm:
        out = _rms(out, gf_ref[...])
    o_ref[...] = out


def _combine(x, route, ga, gb, gain_final, final_norm):
    T, D = x.shape
    tm = ROUTE_TM
    n = T // tm
    quarter = D // 4
    slot1 = pl.BlockSpec((tm, quarter), lambda i: (i, 0))
    slot2 = pl.BlockSpec((tm, quarter), lambda i: (i + n, 0))
    return pl.pallas_call(
        functools.partial(_combine_kernel, final_norm=final_norm),
        grid=(n,),
        in_specs=[pl.BlockSpec((tm, D), lambda i: (i, 0)),
                  pl.BlockSpec((tm, ROUTER_LANES), lambda i: (i, 0)),
                  pl.BlockSpec((1, D), lambda i: (0, 0)),
                  slot1, slot1, slot2, slot2],
        out_specs=pl.BlockSpec((tm, D), lambda i: (i, 0)),
        out_shape=jax.ShapeDtypeStruct((T, D), f32),
        compiler_params=_cparams("parallel"),
        name="moe_combine",
    )(x, route, gain_final.reshape(1, D), ga, gb, ga, gb)


def _moe(x, hpa, hpb, route, counts, wg, wu, wd, gain_final, final_norm):
    (d0, d1), tile_expert, n_used = _moe_plan(route, counts)
    n_rows = _moe_rows(x.shape[0])
    xa = _sc_scatter_pair(hpa, d0, d1, n_rows)
    xb = _sc_scatter_pair(hpb, d0, d1, n_rows)
    ya, yb = _experts(tile_expert, n_used, xa, xb,
                      wg.reshape(N_EXPERTS, D_MODEL, D_EXPERT), wu.reshape(N_EXPERTS, D_MODEL, D_EXPERT),
                      wd.reshape(N_EXPERTS, D_EXPERT, D_MODEL))
    dcat = jnp.concatenate([d0, d1])
    return _combine(x, route, _sc_gather(ya, dcat), _sc_gather(yb, dcat), gain_final, final_norm)


LOG2E = 1.4426950408889634


def _swa_bias_tables():
    blk = C_BLOCK
    qi = np.arange(blk)[:, None]
    kj = np.arange(3 * blk)[None, :]
    dist = np.abs(blk + qi - kj)
    tabs = np.full((3, C_KV_HEADS, C_GROUP * blk, 3 * blk), -np.inf, np.float32)
    for case in range(3):
        ok = dist <= WINDOW
        if case == 0:
            ok = ok & (kj >= blk)
        if case == 2:
            ok = ok & (kj < 2 * blk)
        for hh in range(C_HEADS):
            slope = 2.0 ** (-8.0 * (hh + 1) / C_HEADS)
            t = np.where(ok, -slope * LOG2E * dist, -np.inf)
            kvh, g = divmod(hh, C_GROUP)
            tabs[case, kvh, g * blk:(g + 1) * blk] = t
    return jnp.asarray(tabs)


def _swa_kernel(sink_ref, q_ref, kp_ref, kc_ref, kn_ref, vp_ref, vc_ref, vn_ref, bias_ref, o_ref):
    blk = C_BLOCK
    hd = C_HEAD_DIM
    kcat = jnp.concatenate([kp_ref[0], kc_ref[0], kn_ref[0]], axis=0)
    vcat = jnp.concatenate([vp_ref[0], vc_ref[0], vn_ref[0]], axis=0)
    q = q_ref[0]
    rblk = lax.broadcasted_iota(jnp.int32, (C_GROUP * blk, 1), 0) // blk
    lane = lax.broadcasted_iota(jnp.int32, (3 * blk, LANE - hd), 1)
    ones_pad = jnp.where(lane == 0, 1.0, 0.0).astype(bf16)
    q4s, sinks = [], []
    for kvh in range(C_KV_HEADS):
        q4s.append(jnp.concatenate(
            [q[:, (kvh * C_GROUP + g) * hd:(kvh * C_GROUP + g + 1) * hd] for g in range(C_GROUP)], axis=0))
        sk = jnp.zeros((C_GROUP * blk, 1), f32)
        for g in range(C_GROUP):
            sk = jnp.where(rblk == g, sink_ref[kvh * C_GROUP + g] * LOG2E, sk)
        sinks.append(sk)
    ss = [_dot_nt(q4s[h], kcat[:, h * hd:(h + 1) * hd]) + bias_ref[0, h] for h in range(C_KV_HEADS)]
    ms = [jnp.maximum(jnp.max(s, axis=-1, keepdims=True), sk) for s, sk in zip(ss, sinks)]
    ps = [jnp.exp2(s - m).astype(bf16) for s, m in zip(ss, ms)]
    pvs = [_dot(ps[h], jnp.concatenate([vcat[:, h * hd:(h + 1) * hd], ones_pad], axis=1)) for h in range(C_KV_HEADS)]
    outs = []
    for h in range(C_KV_HEADS):
        den = pvs[h][:, hd:hd + 1] + jnp.exp2(sinks[h] - ms[h])
        o4 = pvs[h][:, 0:hd] / den
        outs += [o4[g * blk:(g + 1) * blk, :] for g in range(C_GROUP)]
    o_ref[0] = jnp.concatenate(outs, axis=1).astype(o_ref.dtype)


def _swa(qkv, sinks):
    B, L, _ = qkv.shape
    nb = L // C_BLOCK
    dq = C_HEADS * C_HEAD_DIM
    dkv = C_KV_HEADS * C_HEAD_DIM
    k_col = dq // dkv
    v_col = k_col + 1

    def kv_spec(col, shift):
        return pl.BlockSpec((1, C_BLOCK, dkv),
                            lambda n, b: (b, jnp.clip(n + shift, 0, nb - 1), col))

    def edge_case(n, b):
        return (jnp.where(n == 0, 0, jnp.where(n == nb - 1, 2, 1)), 0, 0, 0)

    return pl.pallas_call(
        _swa_kernel,
        grid=(nb, B),
        in_specs=[pl.BlockSpec(memory_space=pltpu.SMEM),
                  pl.BlockSpec((1, C_BLOCK, dq), lambda n, b: (b, n, 0)),
                  kv_spec(k_col, -1), kv_spec(k_col, 0), kv_spec(k_col, 1),
                  kv_spec(v_col, -1), kv_spec(v_col, 0), kv_spec(v_col, 1),
                  pl.BlockSpec((1, C_KV_HEADS, C_GROUP * C_BLOCK, 3 * C_BLOCK), edge_case)],
        out_specs=pl.BlockSpec((1, C_BLOCK, dq), lambda n, b: (b, n, 0)),
        out_shape=jax.ShapeDtypeStruct((B, L, dq), bf16),
        compiler_params=_cparams("parallel", "parallel"),
        name="swa",
    )(sinks, qkv, qkv, qkv, qkv, qkv, qkv, qkv, _swa_bias_tables())


def _hyena_filters(L, w1, b1, w2, b2, w3, b3, w4, freq):
    hp = lax.Precision.HIGHEST
    t = jnp.linspace(0.0, 1.0, L, dtype=f32)[:, None]
    wpos = (2.0 * math.pi / L) * jnp.arange(L, dtype=f32)[:, None]
    bands = jnp.linspace(1e-4, HY_BANDS - 1, HY_BANDS, dtype=f32)[None, :]
    feats = jnp.concatenate([t, jnp.cos(bands * wpos), -jnp.sin(bands * wpos)], axis=-1)
    h = jnp.sin(freq[0] * (jnp.dot(feats, w1, precision=hp) + b1))
    h = jnp.sin(freq[1] * (jnp.dot(h, w2, precision=hp) + b2))
    h = jnp.sin(freq[2] * (jnp.dot(h, w3, precision=hp) + b3))
    h = jnp.dot(h, w4, precision=hp).reshape(L, HY_ORDER, 2, B_WIDTH)
    deltas = jnp.abs(jnp.linspace(HY_MIN_DECAY, HY_MAX_DECAY, B_WIDTH, dtype=f32))
    window = jnp.exp(-t * deltas[None, :])
    return h * window[:, None, None, :]


def _dft_tables(L):
    n_fft = 2 * L
    idx = jnp.arange(L, dtype=jnp.int32)
    ph = (idx[:, None] * idx[None, :]) % n_fft
    ang = ph.astype(f32) * (2.0 * math.pi / n_fft)
    return jnp.cos(ang).astype(bf16), jnp.sin(ang).astype(bf16)


def _pad_rows(w, rows=8):
    return jnp.pad(w, ((0, rows - w.shape[0]), (0, 0)))


def _even_in_weights(w_in):
    hk = A_HEADS * A_DK
    q = w_in[:, 0:hk].reshape(D_MODEL, A_HEADS, A_DK)
    k = w_in[:, hk:2 * hk].reshape(D_MODEL, A_HEADS, A_DK)
    v = w_in[:, 2 * hk:A_QKV].reshape(D_MODEL, A_HEADS, A_DV)
    z = w_in[:, A_QKV:A_QKV + A_WIDTH].reshape(D_MODEL, A_HEADS, A_DV)
    qkvz = jnp.concatenate([q, k, v, z], axis=-1).reshape(D_MODEL, A_HEADS * 4 * A_DK)
    ab = w_in[:, A_QKV + A_WIDTH:A_QKV + A_WIDTH + 4 * A_HEADS].reshape(D_MODEL, 4, A_HEADS)
    ab = jnp.transpose(ab, (0, 2, 1))
    ab = jnp.pad(ab, ((0, 0), (0, 0), (0, LANE - 4))).reshape(D_MODEL, A_HEADS * LANE)
    hx = w_in[:, A_QKV + A_WIDTH + 4 * A_HEADS:]
    return jnp.concatenate([qkvz, ab, hx], axis=1).astype(bf16)


def _router_weights(w_group, w_expert):
    w = jnp.concatenate([w_group, w_expert], axis=1)
    w = jnp.pad(w, ((0, 0), (0, ROUTER_LANES - w.shape[1])))
    hi = w.astype(bf16)
    lo = (w - hi.astype(f32)).astype(bf16)
    return jnp.concatenate([hi, lo], axis=1)


def kernel(x_prompt, x_sample, norm_mix, norm_ffn, norm_final, ev_w_in, ev_conv_a, ev_alog_f, ev_alog_b, ev_dtb_f, ev_dtb_b, ev_onorm, ev_conv_b, hy_w1, hy_b1, hy_w2, hy_b2, hy_w3, hy_b3, hy_w4, hy_freq, hy_dbias, ev_w_out, od_w_qkv, od_sinks, od_w_out, moe_w_group, moe_w_expert, moe_w_gate, moe_w_up, moe_w_down):
    bp = x_prompt.shape[0]
    L = x_prompt.shape[1]
    x = jnp.concatenate([x_prompt, x_sample], axis=0)
    B = x.shape[0]
    T = B * L
    x = x.reshape(T, D_MODEL)

    w_in = _even_in_weights(ev_w_in[0])
    n_qkvz = A_HEADS * 4 * A_DK
    n_ab = A_HEADS * LANE
    n_hx = 3 * B_WIDTH
    qkvz, abh, hx = _norm_matmul(x, norm_mix[0], w_in, (n_qkvz, n_ab, n_hx), (bf16, f32, bf16))

    hk = A_HEADS * A_DK
    ca = ev_conv_a[0]
    conv_a = jnp.concatenate([ca[0:hk].reshape(A_HEADS, A_DK, 3), ca[hk:2 * hk].reshape(A_HEADS, A_DK, 3),
                              ca[2 * hk:].reshape(A_HEADS, A_DV, 3)], axis=1)
    conv_a = jnp.pad(jnp.transpose(conv_a, (0, 2, 1)), ((0, 0), (0, 5), (0, 0)))
    gate_sc = jnp.stack([ev_alog_f[0], ev_alog_b[0], ev_dtb_f[0], ev_dtb_b[0]], axis=1)
    o_a = _deltanet(qkvz.reshape(B, L, n_qkvz), abh.reshape(B, L, n_ab), conv_a, gate_sc, ev_onorm[0])

    filt = _hyena_filters(L, hy_w1[0], hy_b1[0], hy_w2[0], hy_b2[0], hy_w3[0], hy_b3[0], hy_w4[0], hy_freq[0])
    hsum = (filt[:, :, 0] + filt[:, :, 1]).reshape(L, HY_ORDER * B_WIDTH)
    hdiff = (filt[:, :, 0] - filt[:, :, 1]).reshape(L, HY_ORDER * B_WIDTH)
    cmat, smat = _dft_tables(L)
    kr, ki, kn = _filter_spectrum(cmat, smat, hsum, hdiff)
    conv_b = _pad_rows(ev_conv_b[0].T)
    o_b = _hyena(hx.reshape(B, L, n_hx), conv_b, cmat, smat, kr, ki, kn, _pad_rows(hy_dbias[0]))

    w_out = ev_w_out[0].astype(bf16)
    x, hpa, hpb, route, counts = _proj_route(x, (o_a.reshape(T, A_WIDTH), o_b.reshape(T, B_WIDTH)),
                                       (w_out[:A_WIDTH], w_out[A_WIDTH:]), norm_ffn[0],
                                       _router_weights(moe_w_group[0], moe_w_expert[0]))
    x = _moe(x, hpa, hpb, route, counts, moe_w_gate[0], moe_w_up[0], moe_w_down[0], norm_final, final_norm=False)

    n_qkv = od_w_qkv.shape[-1]
    dq = C_HEADS * C_HEAD_DIM
    w_qkv = jnp.concatenate([od_w_qkv[0][:, :dq] * (LOG2E * C_HEAD_DIM ** -0.5), od_w_qkv[0][:, dq:]], axis=1)
    (qkv,) = _norm_matmul(x, norm_mix[1], w_qkv.astype(bf16), (n_qkv,), (bf16,))
    o_c = _swa(qkv.reshape(B, L, n_qkv), od_sinks[0])
    x, hpa, hpb, route, counts = _proj_route(x, (o_c.reshape(T, C_HEADS * C_HEAD_DIM),), (od_w_out[0].astype(bf16),),
                                       norm_ffn[1], _router_weights(moe_w_group[1], moe_w_expert[1]))
    y = _moe(x, hpa, hpb, route, counts, moe_w_gate[1], moe_w_up[1], moe_w_down[1], norm_final, final_norm=True)
    y = y.reshape(B, L, D_MODEL)
    return (y[:bp], y[bp:])
```

```python
import functools
import math

import jax
import jax.numpy as jnp
import numpy as np
from jax import lax
from jax.experimental import pallas as pl
from jax.experimental.pallas import tpu as pltpu
from jax.experimental.pallas import tpu_sc as plsc

f32 = jnp.float32
bf16 = jnp.bfloat16

EPS = 1e-6
D_MODEL = 1024

A_HEADS = 4
A_DK = 128
A_DV = 128
A_WIDTH = A_HEADS * A_DV
A_QKV = 2 * A_HEADS * A_DK + A_HEADS * A_DV
DELTA_CHUNK = 128
INV_BASE = 8
DELTA_GROUP = 8

B_WIDTH = D_MODEL - A_WIDTH
HY_ORDER = 2
HY_EMB = 33
HY_BANDS = (HY_EMB - 1) // 2
HY_TARGET = 1e-2
HY_MIN_DECAY = math.log(HY_TARGET) / 1.5
HY_MAX_DECAY = math.log(HY_TARGET) / 0.3
HY_CT = 256
HY_ROWS = 256

C_HEADS = 16
C_KV_HEADS = 4
C_HEAD_DIM = 64
C_GROUP = C_HEADS // C_KV_HEADS
WINDOW = 128
C_BLOCK = 128

N_GROUPS = 4
EXPERTS_PER_GROUP = 8
N_EXPERTS = N_GROUPS * EXPERTS_PER_GROUP
D_EXPERT = 256
ROUTER_LANES = 128

LANE = 128
VMEM_LIMIT = 56 * 1024 * 1024


def _cparams(*sem):
    return pltpu.CompilerParams(dimension_semantics=sem, vmem_limit_bytes=VMEM_LIMIT)


def _dot(a, b):
    return jnp.dot(a, b, preferred_element_type=f32)


def _dot_nt(a, b):
    return lax.dot_general(a, b, (((1,), (1,)), ((), ())), preferred_element_type=f32)


def _dot_tn(a, b):
    return lax.dot_general(a, b, (((0,), (0,)), ((), ())), preferred_element_type=f32)


def _sigmoid(x):
    return 1.0 / (1.0 + jnp.exp(-x))


def _rms(x, g):
    return x * lax.rsqrt(jnp.mean(x * x, axis=-1, keepdims=True) + EPS) * g


def _row_sources(xs, tm):
    tiles = [x.shape[0] // tm for x in xs]
    starts = [sum(tiles[:k]) for k in range(len(xs))]

    def spec(x, lo, n):
        return pl.BlockSpec((tm, x.shape[1]), lambda i: (jnp.clip(i - lo, 0, n - 1), 0))

    return [spec(x, lo, n) for x, lo, n in zip(xs, starts, tiles)], tuple(starts[1:])


def _select_rows(refs, starts):
    i = pl.program_id(0)
    x = refs[0][...]
    for ref, lo in zip(refs[1:], starts):
        x = jnp.where(i >= lo, ref[...], x)
    return x


def _norm_matmul_kernel(*refs, starts):
    n_src = len(starts) + 1
    g_ref, w_ref = refs[n_src:n_src + 2]
    o_refs = refs[n_src + 2:]
    h = _rms(_select_rows(refs[:n_src], starts), g_ref[...]).astype(bf16)
    off = 0
    for o_ref in o_refs:
        n = o_ref.shape[-1]
        o_ref[...] = _dot(h, w_ref[:, off:off + n]).astype(o_ref.dtype)
        off += n


def _norm_matmul(xs, gain, w, splits, dtypes, tm=512):
    T = sum(x.shape[0] for x in xs)
    D = xs[0].shape[1]
    n_all = w.shape[1]
    x_specs, starts = _row_sources(xs, tm)
    return pl.pallas_call(
        functools.partial(_norm_matmul_kernel, starts=starts),
        grid=(T // tm,),
        in_specs=x_specs + [pl.BlockSpec((1, D), lambda i: (0, 0)),
                            pl.BlockSpec((D, n_all), lambda i: (0, 0))],
        out_specs=[pl.BlockSpec((tm, n), lambda i: (i, 0)) for n in splits],
        out_shape=[jax.ShapeDtypeStruct((T, n), dt) for n, dt in zip(splits, dtypes)],
        compiler_params=_cparams("parallel"),
        name="norm_matmul",
    )(*xs, gain.reshape(1, D), w)


MK_LOWER, MK_UPPER, MK_BASE, MK_LEVEL0 = 0, 1, 2, 3
N_LEVELS = int(math.log2(DELTA_CHUNK // INV_BASE))
MK_EYE = MK_LEVEL0 + N_LEVELS
N_MASKS = MK_EYE + 1


def _delta_masks():
    c = np.arange(DELTA_CHUNK)[:, None]
    s = np.arange(DELTA_CHUNK)[None, :]
    m = [s <= c, s >= c, c // INV_BASE == s // INV_BASE]
    b = INV_BASE
    while b < DELTA_CHUNK:
        m.append((c // (2 * b) == s // (2 * b)) & (c // b != s // b))
        b *= 2
    m.append(c == s)
    return jnp.asarray(np.stack(m).astype(np.float32))


def _tri_inverse(mats, mk_ref):
    ps = [a * mk_ref[MK_BASE] for a in mats]
    ts = [mk_ref[MK_EYE] - p for p in ps]
    n = 2
    while n < INV_BASE:
        p16s = [p.astype(bf16) for p in ps]
        ps = [_dot(p16, p16) for p16 in p16s]
        ts = [t + _dot(t.astype(bf16), p.astype(bf16)) for t, p in zip(ts, ps)]
        n *= 2
    for lvl in range(N_LEVELS):
        t16s = [t.astype(bf16) for t in ts]
        xs = [_dot((a * mk_ref[MK_LEVEL0 + lvl]).astype(bf16), t16) for a, t16 in zip(mats, t16s)]
        ts = [t - _dot(t16, x.astype(bf16)) for t, t16, x in zip(ts, t16s, xs)]
    return ts


def _deltanet_kernel(sc_ref, x_ref, ab_ref, cw_ref, on_ref, mk_ref, o_ref, mp_s, nr_s, gl_s, o_s):
    hd = pl.program_id(1)
    L = x_ref.shape[1]
    C = DELTA_CHUNK
    nchunk = L // C
    group = math.gcd(DELTA_GROUP, nchunk)
    nqkv = 3 * A_DK
    halo = 16

    lane = lax.broadcasted_iota(jnp.int32, (1, LANE), 1)
    alog = jnp.where(lane == 2, sc_ref[hd, 0], jnp.where(lane == 3, sc_ref[hd, 1], 0.0))
    dtb = jnp.where(lane == 2, sc_ref[hd, 2], jnp.where(lane == 3, sc_ref[hd, 3], 0.0))
    neg_a = jnp.where((lane == 2) | (lane == 3), -jnp.exp(alog), 0.0)
    rowi = lax.broadcasted_iota(jnp.int32, (C, 1), 0)
    cw = cw_ref[0]

    def chunk_inputs(i):
        r0 = pl.multiple_of(i * C, C)
        xc = x_ref[0, pl.ds(r0, C), 0:nqkv].astype(f32)
        rp = pl.multiple_of(jnp.maximum(r0 - halo, 0), halo)
        rn = pl.multiple_of(jnp.minimum(r0 + C, L - halo), halo)
        prow = x_ref[0, pl.ds(rp, halo), 0:nqkv][halo - 1:halo, :].astype(f32)
        nrow = x_ref[0, pl.ds(rn, halo), 0:nqkv][0:1, :].astype(f32)
        prow = jnp.where(i > 0, prow, 0.0)
        nrow = jnp.where(i < nchunk - 1, nrow, 0.0)
        prev = jnp.where(rowi == 0, prow, pltpu.roll(xc, 1, 0))
        nxt = jnp.where(rowi == C - 1, nrow, pltpu.roll(xc, C - 1, 0))
        y = prev * cw[0:1, :] + xc * cw[1:2, :] + nxt * cw[2:3, :]
        y = y * _sigmoid(y)
        q = y[:, 0:A_DK]
        k = y[:, A_DK:2 * A_DK]
        vc = y[:, 2 * A_DK:nqkv]
        qc = q * lax.rsqrt(jnp.sum(q * q, axis=-1, keepdims=True) + EPS) * (A_DK ** -0.5)
        kc = k * lax.rsqrt(jnp.sum(k * k, axis=-1, keepdims=True) + EPS)

        abc = ab_ref[0, pl.ds(r0, C), :]
        zg = abc + dtb
        g = neg_a * (jnp.maximum(zg, 0.0) + jnp.log(1.0 + jnp.exp(-jnp.abs(zg))))
        low16 = mk_ref[MK_LOWER].astype(bf16)
        g1 = g.astype(bf16)
        e1 = g - g1.astype(f32)
        g2 = e1.astype(bf16)
        g3 = (e1 - g2.astype(f32)).astype(bf16)
        pre = _dot(low16, g1) + _dot(low16, g2) + _dot(low16, g3)
        tot = pre[C - 1:C, :]
        suf = tot - pre + g
        beta_all = _sigmoid(abc)

        k16 = kc.astype(bf16)
        kk = _dot_nt(k16, k16)
        qk = _dot_nt(qc.astype(bf16), k16)

        chains = []
        for rev in (0, 1):
            beta = beta_all[:, rev:rev + 1]
            gcc = (suf if rev else pre)[:, 2 + rev:3 + rev]
            tot11 = tot[:, 2 + rev:3 + rev]
            keep = mk_ref[MK_UPPER if rev else MK_LOWER]
            bc = jnp.broadcast_to(gcc, (C, C))
            decay = jnp.exp((bc - bc.T) * keep) * keep
            a = (beta * kk) * (decay * (keep - mk_ref[MK_EYE]))
            eg = jnp.exp(gcc)
            chains.append(dict(
                i=i, rev=rev, a=a, gl=jnp.exp(tot11), qd=qc * eg,
                rhs=jnp.concatenate([vc * beta, kc * (beta * eg)], axis=1).astype(bf16),
                kd16=(kc * jnp.exp(tot11 - gcc)).astype(bf16),
                qk16=(qk * decay).astype(bf16)))
        return chains

    def prep(it, carry):
        chains = []
        for gi in range(group):
            chains += chunk_inputs(it * group + gi)
        ts = _tri_inverse([c["a"] for c in chains], mk_ref)
        uws = [_dot(t.astype(bf16), c["rhs"]).astype(bf16) for t, c in zip(ts, chains)]
        kts = [_dot_tn(c["kd16"], uw) for c, uw in zip(chains, uws)]
        prs = [_dot(c["qk16"], uw) for c, uw in zip(chains, uws)]
        for c, kt, pr in zip(chains, kts, prs):
            rev, i = c["rev"], c["i"]
            mp_s[rev, i, 0:A_DK, :] = (-kt[:, A_DV:]).astype(bf16)
            mp_s[rev, i, A_DK:A_DK + C, :] = (c["qd"] - pr[:, A_DV:]).astype(bf16)
            nr_s[rev, i, 0:A_DK, :] = kt[:, 0:A_DV]
            nr_s[rev, i, A_DK:A_DK + C, :] = pr[:, 0:A_DV]
            gl_s[rev, i] = jnp.broadcast_to(c["gl"], (8, LANE))
        return carry

    lax.fori_loop(0, nchunk // group, prep, 0)

    def scan(i, carry):
        out = []
        for rev, S in enumerate(carry):
            j = (nchunk - 1 - i) if rev else i
            y = _dot(mp_s[rev, j], S.astype(bf16))
            nr = nr_s[rev, j]
            o_s[rev, pl.ds(pl.multiple_of(j * C, C), C), :] = y[A_DK:, :] + nr[A_DK:, :]
            out.append(S * gl_s[rev, j][0:1, :] + y[0:A_DK, :] + nr[0:A_DK, :])
        return tuple(out)

    S0 = jnp.zeros((A_DK, A_DV), f32)
    lax.fori_loop(0, nchunk, scan, (S0, S0))

    def finish(i, carry):
        r0 = pl.multiple_of(i * C, C)
        o = o_s[0, pl.ds(r0, C), :] + o_s[1, pl.ds(r0, C), :]
        zg = x_ref[0, pl.ds(r0, C), nqkv:nqkv + A_DV].astype(f32)
        o_ref[0, pl.ds(r0, C), :] = (_rms(o, on_ref[...]) * (zg * _sigmoid(zg))).astype(o_ref.dtype)
        return carry

    lax.fori_loop(0, nchunk, finish, 0)


def _deltanet(qkvz, abh, conv_w, gate_sc, onorm):
    B, L, _ = qkvz.shape
    C = DELTA_CHUNK
    nchunk = L // C
    hw = 4 * A_DK
    scratch = [pltpu.VMEM((2, nchunk, A_DK + C, A_DV), bf16),
               pltpu.VMEM((2, nchunk, A_DK + C, A_DV), f32),
               pltpu.VMEM((2, nchunk, 8, LANE), f32),
               pltpu.VMEM((2, L, A_DV), f32)]
    return pl.pallas_call(
        _deltanet_kernel,
        grid=(B, A_HEADS),
        in_specs=[pl.BlockSpec(memory_space=pltpu.SMEM),
                  pl.BlockSpec((1, L, hw), lambda b, h: (b, 0, h)),
                  pl.BlockSpec((1, L, LANE), lambda b, h: (b, 0, h)),
                  pl.BlockSpec((1, 8, 3 * A_DK), lambda b, h: (h, 0, 0)),
                  pl.BlockSpec((1, A_DV), lambda b, h: (0, 0)),
                  pl.BlockSpec((N_MASKS, C, C), lambda b, h: (0, 0, 0))],
        out_specs=pl.BlockSpec((1, L, A_DV), lambda b, h: (b, 0, h)),
        out_shape=jax.ShapeDtypeStruct((B, L, A_WIDTH), bf16),
        scratch_shapes=scratch,
        compiler_params=_cparams("parallel", "parallel"),
        name="deltanet",
    )(gate_sc, qkvz, abh, conv_w, onorm.reshape(1, A_DV), _delta_masks())


def _filter_spectrum_kernel(c_ref, s_ref, hs_ref, hd_ref, kr_ref, ki_ref, kn_ref):
    L = c_ref.shape[0]
    n_fft = 2 * L
    hs = hs_ref[...]
    hd = hd_ref[...]

    def split_dot(m_ref, x):
        hi = x.astype(bf16)
        lo = (x - hi.astype(f32)).astype(bf16)
        m = m_ref[...]
        return _dot(m, hi) + _dot(m, lo)

    row = lax.broadcasted_iota(jnp.int32, (L, 1), 0)
    scale = jnp.where(row == 0, 1.0 / n_fft, 2.0 / n_fft)
    kr_ref[...] = split_dot(c_ref, hs) * scale
    ki_ref[...] = split_dot(s_ref, hd) * scale
    alt = jnp.where(row % 2 == 0, 1.0, -1.0)
    kn = jnp.sum(hs * alt, axis=0, keepdims=True) * (1.0 / n_fft)
    kn_ref[...] = jnp.broadcast_to(kn, kn_ref.shape)


def _filter_spectrum(cmat, smat, hsum, hdiff):
    L, n = hsum.shape
    ct = HY_CT
    return pl.pallas_call(
        _filter_spectrum_kernel,
        grid=(n // ct,),
        in_specs=[pl.BlockSpec((L, L), lambda j: (0, 0)),
                  pl.BlockSpec((L, L), lambda j: (0, 0)),
                  pl.BlockSpec((L, ct), lambda j: (0, j)),
                  pl.BlockSpec((L, ct), lambda j: (0, j))],
        out_specs=[pl.BlockSpec((L, ct), lambda j: (0, j)),
                   pl.BlockSpec((L, ct), lambda j: (0, j)),
                   pl.BlockSpec((8, ct), lambda j: (0, j))],
        out_shape=[jax.ShapeDtypeStruct((L, n), f32),
                   jax.ShapeDtypeStruct((L, n), f32),
                   jax.ShapeDtypeStruct((8, n), f32)],
        compiler_params=_cparams("parallel"),
    )(cmat, smat, hsum, hdiff)


def _hyena_kernel(c_ref, s_ref, x1_ref, x2_ref, v_ref, w1_ref, w2_ref, wv_ref,
                  kr0_ref, ki0_ref, kn0_ref, kr1_ref, ki1_ref, kn1_ref, db_ref, o_ref,
                  z_s, z16_s, yr_s, yi_s, nyq_s):
    L = c_ref.shape[0]
    R = HY_ROWS
    nt = L // R
    halo = 16
    rowi = lax.broadcasted_iota(jnp.int32, (R, 1), 0)
    alt = jnp.where(rowi % 2 == 0, 1.0, -1.0)

    def dwconv_rows(x_ref, w_ref, t, r0):
        x = x_ref[0, pl.ds(r0, R), :].astype(f32)
        rp = pl.multiple_of(jnp.maximum(r0 - halo, 0), halo)
        rn = pl.multiple_of(jnp.minimum(r0 + R, L - halo), halo)
        prow = jnp.where(t > 0, x_ref[0, pl.ds(rp, halo), :][halo - 1:halo, :].astype(f32), 0.0)
        nrow = jnp.where(t < nt - 1, x_ref[0, pl.ds(rn, halo), :][0:1, :].astype(f32), 0.0)
        prev = jnp.where(rowi == 0, prow, pltpu.roll(x, 1, 0))
        nxt = jnp.where(rowi == R - 1, nrow, pltpu.roll(x, R - 1, 0))
        w = w_ref[...]
        return prev * w[0:1, :] + x * w[1:2, :] + nxt * w[2:3, :]

    def load_v(t, acc):
        r0 = pl.multiple_of(t * R, R)
        z = dwconv_rows(v_ref, wv_ref, t, r0)
        z_s[pl.ds(r0, R), :] = z
        z16_s[pl.ds(r0, R), :] = z.astype(bf16)
        return acc + jnp.sum(z * alt, axis=0, keepdims=True)

    nyq = lax.fori_loop(0, nt, load_v, jnp.zeros((1, z_s.shape[1]), f32))

    stages = ((x1_ref, w1_ref, kr0_ref, ki0_ref, kn0_ref), (x2_ref, w2_ref, kr1_ref, ki1_ref, kn1_ref))
    for o, (g_ref, gw_ref, kr_ref, ki_ref, kn_ref) in enumerate(stages):
        nyq_s[...] = jnp.broadcast_to(nyq * kn_ref[0:1, :], nyq_s.shape)

        def spectrum(t, carry):
            r0 = pl.multiple_of(t * R, R)
            z16 = z16_s[...]
            ur = _dot(c_ref[pl.ds(r0, R), :], z16)
            ui = _dot(s_ref[pl.ds(r0, R), :], z16)
            kr = kr_ref[pl.ds(r0, R), :]
            ki = ki_ref[pl.ds(r0, R), :]
            yr_s[pl.ds(r0, R), :] = (ur * kr - ui * ki).astype(bf16)
            yi_s[pl.ds(r0, R), :] = (ur * ki + ui * kr).astype(bf16)
            return carry

        lax.fori_loop(0, nt, spectrum, 0)

        def signal(t, acc):
            r0 = pl.multiple_of(t * R, R)
            y = _dot(c_ref[pl.ds(r0, R), :], yr_s[...]) + _dot(s_ref[pl.ds(r0, R), :], yi_s[...])
            y = y + alt * nyq_s[0:1, :]
            z = dwconv_rows(g_ref, gw_ref, t, r0) * (y + z_s[pl.ds(r0, R), :] * db_ref[o:o + 1, :])
            if o + 1 < len(stages):
                z_s[pl.ds(r0, R), :] = z
                z16_s[pl.ds(r0, R), :] = z.astype(bf16)
            else:
                o_ref[0, pl.ds(r0, R), :] = z.astype(o_ref.dtype)
            return acc + jnp.sum(z * alt, axis=0, keepdims=True)

        nyq = lax.fori_loop(0, nt, signal, jnp.zeros((1, z_s.shape[1]), f32))


def _hyena(hx, conv_w, cmat, smat, kr, ki, kn, dbias):
    B, L, _ = hx.shape
    ct = HY_CT
    nct = B_WIDTH // ct
    once = pl.Buffered(1)

    def xspec(part):
        return pl.BlockSpec((1, L, ct), lambda c, b: (b, 0, part * nct + c))

    def wspec(part):
        return pl.BlockSpec((8, ct), lambda c, b: (0, part * nct + c))

    def kspec(order, rows):
        return pl.BlockSpec((rows, ct), lambda c, b: (0, order * nct + c), pipeline_mode=once)

    return pl.pallas_call(
        _hyena_kernel,
        grid=(nct, B),
        in_specs=[pl.BlockSpec((L, L), lambda c, b: (0, 0), pipeline_mode=once),
                  pl.BlockSpec((L, L), lambda c, b: (0, 0), pipeline_mode=once),
                  xspec(0), xspec(1), xspec(2), wspec(0), wspec(1), wspec(2),
                  kspec(0, L), kspec(0, L), kspec(0, 8), kspec(1, L), kspec(1, L), kspec(1, 8),
                  pl.BlockSpec((8, ct), lambda c, b: (0, c))],
        out_specs=pl.BlockSpec((1, L, ct), lambda c, b: (b, 0, c)),
        out_shape=jax.ShapeDtypeStruct((B, L, B_WIDTH), bf16),
        scratch_shapes=[pltpu.VMEM((L, ct), f32), pltpu.VMEM((L, ct), bf16),
                        pltpu.VMEM((L, ct), bf16), pltpu.VMEM((L, ct), bf16), pltpu.VMEM((8, ct), f32)],
        compiler_params=_cparams("parallel", "parallel"),
        name="hyena",
    )(cmat, smat, hx, hx, hx, conv_w, conv_w, conv_w, kr, ki, kn, kr, ki, kn, dbias)


RT_E1, RT_E2, RT_W1, RT_W2, RT_R1, RT_R2 = 0, 1, 2, 3, 4, 5
ROUTE_TM = 512
EXPERT_TM = 512
SC_WINDOW = 128


def _pack_bf16_pairs(x):
    n = x.shape[1] // 2
    bits = lax.bitcast_convert_type(x.astype(bf16).astype(f32), jnp.uint32)
    return bits[:, :n] | (bits[:, n:] >> 16)


def _unpack_bf16_pairs(p):
    hi = lax.bitcast_convert_type(p & jnp.uint32(0xFFFF0000), f32)
    lo = lax.bitcast_convert_type(p << 16, f32)
    return hi, lo


def _route(logits, low_ref):
    lane = lax.broadcasted_iota(jnp.int32, logits.shape, 1)
    neg = -jnp.inf
    big = ROUTER_LANES
    gl = jnp.where(lane < N_GROUPS, logits, neg)
    gmax = jnp.max(gl, axis=-1, keepdims=True)
    gsel = jnp.min(jnp.where(gl == gmax, lane, big), axis=-1, keepdims=True)
    psel = 1.0 / jnp.sum(jnp.exp(gl - gmax), axis=-1, keepdims=True)
    first = N_GROUPS + gsel * EXPERTS_PER_GROUP
    el = jnp.where((lane >= first) & (lane < first + EXPERTS_PER_GROUP), logits, neg)
    t1 = jnp.max(el, axis=-1, keepdims=True)
    i1 = jnp.min(jnp.where(el == t1, lane, big), axis=-1, keepdims=True)
    el2 = jnp.where(lane == i1, neg, el)
    t2 = jnp.max(el2, axis=-1, keepdims=True)
    i2 = jnp.min(jnp.where(el2 == t2, lane, big), axis=-1, keepdims=True)
    ex = jnp.exp(t2 - t1)
    w1 = psel / (1.0 + ex)
    w2 = ex * w1
    e1 = i1 - N_GROUPS
    e2 = i2 - N_GROUPS
    onehot = jnp.where((lane == e1) | (lane == e2), 1.0, 0.0)
    before = _dot(low_ref[...], onehot.astype(bf16))
    r1 = jnp.sum(jnp.where(lane == e1, before, 0.0), axis=-1, keepdims=True)
    r2 = jnp.sum(jnp.where(lane == e2, before, 0.0), axis=-1, keepdims=True)
    tm = logits.shape[0]
    counts = before[tm - 1:tm, :] + onehot[tm - 1:tm, :]
    rec = jnp.zeros(logits.shape, f32)
    for ln, val in ((RT_E1, e1.astype(f32)), (RT_E2, e2.astype(f32)), (RT_W1, w1), (RT_W2, w2), (RT_R1, r1), (RT_R2, r2)):
        rec = jnp.where(lane == ln, val, rec)
    return rec, counts


def _proj_route_kernel(*refs, n_in, starts):
    n_src = len(starts) + 1
    x = _select_rows(refs[:n_src], starts)
    refs = refs[n_src:]
    a_refs = refs[0:n_in]
    w_refs = refs[n_in:2 * n_in]
    g_ref, wr_ref, low_ref, xo_ref, hpa_ref, hpb_ref, rt_ref, rtt_ref, cnt_ref = refs[2 * n_in:]
    for a_ref, w_ref in zip(a_refs, w_refs):
        x = x + _dot(a_ref[...], w_ref[...])
    xo_ref[...] = x
    h = _rms(x, g_ref[...])
    hp = _pack_bf16_pairs(h)
    quarter = hp.shape[1] // 2
    hpa_ref[...] = hp[:, :quarter]
    hpb_ref[...] = hp[:, quarter:]
    h_hi = h.astype(bf16)
    h_lo = (h - h_hi.astype(f32)).astype(bf16)
    both = _dot(h_hi, wr_ref[...])
    logits = both[:, :ROUTER_LANES] + both[:, ROUTER_LANES:] + _dot(h_lo, wr_ref[:, :ROUTER_LANES])
    rec, counts = _route(logits, low_ref)
    rt_ref[...] = rec
    rtt_ref[...] = rec.T[0:rtt_ref.shape[0], :]
    cnt_ref[0] = jnp.broadcast_to(counts, cnt_ref.shape[1:])


def _proj_route(xs, acts, ws, gain, w_router):
    T = sum(x.shape[0] for x in xs)
    D = xs[0].shape[1]
    tm = ROUTE_TM
    n_in = len(acts)
    low = jnp.asarray(np.tril(np.ones((tm, tm), np.float32), -1), dtype=bf16)
    in_specs, starts = _row_sources(xs, tm)
    in_specs += [pl.BlockSpec((tm, a.shape[1]), lambda i: (i, 0)) for a in acts]
    in_specs += [pl.BlockSpec(w.shape, lambda i: (0, 0)) for w in ws]
    in_specs += [pl.BlockSpec((1, D), lambda i: (0, 0)),
                 pl.BlockSpec((D, 2 * ROUTER_LANES), lambda i: (0, 0)),
                 pl.BlockSpec((tm, tm), lambda i: (0, 0))]
    return pl.pallas_call(
        functools.partial(_proj_route_kernel, n_in=n_in, starts=starts),
        grid=(T // tm,),
        in_specs=in_specs,
        out_specs=[pl.BlockSpec((tm, D), lambda i: (i, 0)),
                   pl.BlockSpec((tm, D // 4), lambda i: (i, 0)),
                   pl.BlockSpec((tm, D // 4), lambda i: (i, 0)),
                   pl.BlockSpec((tm, ROUTER_LANES), lambda i: (i, 0)),
                   pl.BlockSpec((8, tm), lambda i: (0, i)),
                   pl.BlockSpec((1, 8, ROUTER_LANES), lambda i: (i, 0, 0))],
        out_shape=[jax.ShapeDtypeStruct((T, D), f32),
                   jax.ShapeDtypeStruct((T, D // 4), jnp.uint32),
                   jax.ShapeDtypeStruct((T, D // 4), jnp.uint32),
                   jax.ShapeDtypeStruct((T, ROUTER_LANES), f32),
                   jax.ShapeDtypeStruct((8, T), f32),
                   jax.ShapeDtypeStruct((T // tm, 8, ROUTER_LANES), f32)],
        compiler_params=_cparams("parallel"),
        name="proj_route",
    )(*xs, *acts, *ws, gain.reshape(1, D), w_router, low)


def _moe_rows(T):
    return 2 * T + N_EXPERTS * EXPERT_TM


def _moe_plan(route_t, counts):
    T = route_t.shape[1]
    tm = ROUTE_TM
    cnt = counts[:, 0, :N_EXPERTS].astype(jnp.int32)
    total = jnp.sum(cnt, axis=0)
    padded = ((total + EXPERT_TM - 1) // EXPERT_TM) * EXPERT_TM
    ends = jnp.cumsum(padded)
    tile_base = (ends - padded)[None, :] + jnp.cumsum(cnt, axis=0) - cnt
    base = jnp.broadcast_to(tile_base.T[:, :, None], (N_EXPERTS, T // tm, tm)).reshape(N_EXPERTS, T)
    ids = jnp.arange(N_EXPERTS, dtype=jnp.int32)[:, None]
    dest = []
    for e_row, r_row in ((RT_E1, RT_R1), (RT_E2, RT_R2)):
        e = route_t[e_row].astype(jnp.int32)
        dest.append(jnp.sum(jnp.where(ids == e[None, :], base, 0), axis=0) + route_t[r_row].astype(jnp.int32))
    n_tiles = _moe_rows(T) // EXPERT_TM
    n_used = ends[-1] // EXPERT_TM
    starts = jnp.arange(n_tiles, dtype=jnp.int32) * EXPERT_TM
    tile_expert = jnp.sum((starts[:, None] >= ends[None, :]).astype(jnp.int32), axis=1)
    last_expert = jnp.sum((jnp.maximum(n_used - 1, 0) * EXPERT_TM >= ends).astype(jnp.int32))
    tile_expert = jnp.minimum(tile_expert, last_expert)
    return dest, tile_expert, n_used.reshape(1).astype(jnp.int32)


def _sc_mesh():
    return plsc.VectorSubcoreMesh(core_axis_name="core", subcore_axis_name="subcore")


def _sc_scatter_pair(rows, d0, d1, n_out):
    T, W = rows.shape
    win = SC_WINDOW

    @pl.kernel(out_type=jax.ShapeDtypeStruct((n_out, W), rows.dtype), mesh=_sc_mesh(), name="moe_sc_scatter")
    def scatter(x_hbm, i0_hbm, i1_hbm, o_hbm):
        def body(x_vmem, i0_vmem, i1_vmem):
            pltpu.sync_copy(x_vmem, o_hbm.at[i0_vmem.at[0]])
            pltpu.sync_copy(x_vmem, o_hbm.at[i1_vmem.at[0]])

        pltpu.emit_pipeline(
            body, grid=(T // win,),
            in_specs=[pl.BlockSpec((win, W), index_map=lambda i: (i, 0)),
                      pl.BlockSpec((1, win), index_map=lambda i: (0, i)),
                      pl.BlockSpec((1, win), index_map=lambda i: (0, i))],
            out_specs=[],
            core_axis_name=("core", "subcore"), dimension_semantics=(pltpu.PARALLEL,),
        )(x_hbm, i0_hbm, i1_hbm)

    return scatter(rows, d0.reshape(1, T), d1.reshape(1, T))


def _sc_gather(table, idx):
    n = idx.shape[0]
    W = table.shape[1]
    win = SC_WINDOW

    @pl.kernel(out_type=jax.ShapeDtypeStruct((n, W), table.dtype), mesh=_sc_mesh(), name="moe_sc_gather")
    def gather(t_hbm, i_hbm, o_hbm):
        def body(i_vmem, o_vmem):
            pltpu.sync_copy(t_hbm.at[i_vmem.at[0]], o_vmem)

        pltpu.emit_pipeline(
            body, grid=(n // win,),
            in_specs=[pl.BlockSpec((1, win), index_map=lambda i: (0, i))],
            out_specs=[pl.BlockSpec((win, W), index_map=lambda i: (i, 0))],
            core_axis_name=("core", "subcore"), dimension_semantics=(pltpu.PARALLEL,),
        )(i_hbm, o_hbm)

    return gather(table, idx.reshape(1, n))


def _expert_kernel(te_ref, nu_ref, xa_ref, xb_ref, wg_ref, wu_ref, wd_ref, ya_ref, yb_ref):
    @pl.when(pl.program_id(0) < nu_ref[0])
    def _():
        hi, lo = _unpack_bf16_pairs(jnp.concatenate([xa_ref[...], xb_ref[...]], axis=1))
        x = jnp.concatenate([hi, lo], axis=1).astype(bf16)
        gate = _dot(x, wg_ref[0].astype(bf16))
        up = _dot(x, wu_ref[0].astype(bf16))
        hg = (gate * _sigmoid(gate) * up).astype(bf16)
        y = _pack_bf16_pairs(_dot(hg, wd_ref[0].astype(bf16)))
        quarter = y.shape[1] // 2
        ya_ref[...] = y[:, :quarter]
        yb_ref[...] = y[:, quarter:]

    @pl.when(pl.program_id(0) >= nu_ref[0])
    def _():
        ya_ref[...] = jnp.zeros(ya_ref.shape, ya_ref.dtype)
        yb_ref[...] = jnp.zeros(yb_ref.shape, yb_ref.dtype)


def _experts(tile_expert, n_used, xa, xb, wg, wu, wd):
    n_rows, quarter = xa.shape
    D = 4 * quarter
    tm = EXPERT_TM
    grid_spec = pltpu.PrefetchScalarGridSpec(
        num_scalar_prefetch=2,
        grid=(n_rows // tm,),
        in_specs=[pl.BlockSpec((tm, quarter), lambda i, te, nu: (i, 0)),
                  pl.BlockSpec((tm, quarter), lambda i, te, nu: (i, 0)),
                  pl.BlockSpec((1, D, D_EXPERT), lambda i, te, nu: (te[i], 0, 0)),
                  pl.BlockSpec((1, D, D_EXPERT), lambda i, te, nu: (te[i], 0, 0)),
                  pl.BlockSpec((1, D_EXPERT, D), lambda i, te, nu: (te[i], 0, 0))],
        out_specs=[pl.BlockSpec((tm, quarter), lambda i, te, nu: (i, 0)),
                   pl.BlockSpec((tm, quarter), lambda i, te, nu: (i, 0))])
    return pl.pallas_call(
        _expert_kernel,
        grid_spec=grid_spec,
        out_shape=[jax.ShapeDtypeStruct((n_rows, quarter), jnp.uint32)] * 2,
        compiler_params=_cparams("arbitrary"),
        name="moe_experts",
    )(tile_expert, n_used, xa, xb, wg, wu, wd)


def _combine_kernel(x_ref, rt_ref, gf_ref, a1_ref, b1_ref, a2_ref, b2_ref, o_ref, *, final_norm):
    half = x_ref.shape[1] // 2
    h1, l1 = _unpack_bf16_pairs(jnp.concatenate([a1_ref[...], b1_ref[...]], axis=1))
    h2, l2 = _unpack_bf16_pairs(jnp.concatenate([a2_ref[...], b2_ref[...]], axis=1))
    w1 = rt_ref[:, RT_W1:RT_W1 + 1]
    w2 = rt_ref[:, RT_W2:RT_W2 + 1]
    out = jnp.concatenate([x_ref[:, 0:half] + w1 * h1 + w2 * h2, x_ref[:, half:] + w1 * l1 + w2 * l2], axis=1)
    if final_norm:
        out = _rms(out, gf_ref[...])
    o_ref[...] = out


def _combine(x, route, ga, gb, gain_final, final_norm, row_ranges=None):
    T, D = x.shape
    tm = ROUTE_TM
    n = T // tm
    quarter = D // 4
    outs = []
    for start, size in (row_ranges or ((0, T),)):
        lo = start // tm
        slot1 = pl.BlockSpec((tm, quarter), lambda i, lo=lo: (i + lo, 0))
        slot2 = pl.BlockSpec((tm, quarter), lambda i, lo=lo: (i + lo + n, 0))
        outs.append(pl.pallas_call(
            functools.partial(_combine_kernel, final_norm=final_norm),
            grid=(size // tm,),
            in_specs=[pl.BlockSpec((tm, D), lambda i, lo=lo: (i + lo, 0)),
                      pl.BlockSpec((tm, ROUTER_LANES), lambda i, lo=lo: (i + lo, 0)),
                      pl.BlockSpec((1, D), lambda i: (0, 0)),
                      slot1, slot1, slot2, slot2],
            out_specs=pl.BlockSpec((tm, D), lambda i: (i, 0)),
            out_shape=jax.ShapeDtypeStruct((size, D), f32),
            compiler_params=_cparams("parallel"),
            name="moe_combine",
        )(x, route, gain_final.reshape(1, D), ga, gb, ga, gb))
    return outs if row_ranges else outs[0]


def _moe(layer, x, hpa, hpb, route, route_t, counts, wg, wu, wd, gain_final, final_norm, row_ranges=None):
    (d0, d1), tile_expert, n_used = _moe_plan(route_t, counts)
    n_rows = _moe_rows(x.shape[0])
    xa = _sc_scatter_pair(hpa, d0, d1, n_rows)
    xb = _sc_scatter_pair(hpb, d0, d1, n_rows)
    ya, yb = _experts(tile_expert + layer * N_EXPERTS, n_used, xa, xb, wg, wu, wd)
    dcat = jnp.concatenate([d0, d1])
    return _combine(x, route, _sc_gather(ya, dcat), _sc_gather(yb, dcat), gain_final, final_norm, row_ranges)


LOG2E = 1.4426950408889634


def _swa_bias_tables():
    blk = C_BLOCK
    qi = np.arange(blk)[:, None]
    kj = np.arange(3 * blk)[None, :]
    dist = np.abs(blk + qi - kj)
    tabs = np.full((3, C_KV_HEADS, C_GROUP * blk, 3 * blk), -np.inf, np.float32)
    for case in range(3):
        ok = dist <= WINDOW
        if case == 0:
            ok = ok & (kj >= blk)
        if case == 2:
            ok = ok & (kj < 2 * blk)
        for hh in range(C_HEADS):
            slope = 2.0 ** (-8.0 * (hh + 1) / C_HEADS)
            t = np.where(ok, -slope * LOG2E * dist, -np.inf)
            kvh, g = divmod(hh, C_GROUP)
            tabs[case, kvh, g * blk:(g + 1) * blk] = t
    return jnp.asarray(tabs)


def _swa_kernel(sink_ref, q_ref, kp_ref, kc_ref, kn_ref, vp_ref, vc_ref, vn_ref, bias_ref, o_ref):
    blk = C_BLOCK
    hd = C_HEAD_DIM
    kcat = jnp.concatenate([kp_ref[0], kc_ref[0], kn_ref[0]], axis=0)
    vcat = jnp.concatenate([vp_ref[0], vc_ref[0], vn_ref[0]], axis=0)
    q = q_ref[0]
    rblk = lax.broadcasted_iota(jnp.int32, (C_GROUP * blk, 1), 0) // blk
    lane = lax.broadcasted_iota(jnp.int32, (3 * blk, LANE - hd), 1)
    ones_pad = jnp.where(lane == 0, 1.0, 0.0).astype(bf16)
    q4s, sinks = [], []
    for kvh in range(C_KV_HEADS):
        q4s.append(jnp.concatenate(
            [q[:, (kvh * C_GROUP + g) * hd:(kvh * C_GROUP + g + 1) * hd] for g in range(C_GROUP)], axis=0))
        sk = jnp.zeros((C_GROUP * blk, 1), f32)
        for g in range(C_GROUP):
            sk = jnp.where(rblk == g, sink_ref[kvh * C_GROUP + g] * LOG2E, sk)
        sinks.append(sk)
    ss = [_dot_nt(q4s[h], kcat[:, h * hd:(h + 1) * hd]) + bias_ref[0, h] for h in range(C_KV_HEADS)]
    ms = [jnp.maximum(jnp.max(s, axis=-1, keepdims=True), sk) for s, sk in zip(ss, sinks)]
    ps = [jnp.exp2(s - m).astype(bf16) for s, m in zip(ss, ms)]
    pvs = [_dot(ps[h], jnp.concatenate([vcat[:, h * hd:(h + 1) * hd], ones_pad], axis=1)) for h in range(C_KV_HEADS)]
    outs = []
    for h in range(C_KV_HEADS):
        den = pvs[h][:, hd:hd + 1] + jnp.exp2(sinks[h] - ms[h])
        o4 = pvs[h][:, 0:hd] / den
        outs += [o4[g * blk:(g + 1) * blk, :] for g in range(C_GROUP)]
    o_ref[0] = jnp.concatenate(outs, axis=1).astype(o_ref.dtype)


def _swa(qkv, sinks):
    B, L, _ = qkv.shape
    nb = L // C_BLOCK
    dq = C_HEADS * C_HEAD_DIM
    dkv = C_KV_HEADS * C_HEAD_DIM
    k_col = dq // dkv
    v_col = k_col + 1

    def kv_spec(col, shift):
        return pl.BlockSpec((1, C_BLOCK, dkv),
                            lambda n, b: (b, jnp.clip(n + shift, 0, nb - 1), col))

    def edge_case(n, b):
        return (jnp.where(n == 0, 0, jnp.where(n == nb - 1, 2, 1)), 0, 0, 0)

    return pl.pallas_call(
        _swa_kernel,
        grid=(nb, B),
        in_specs=[pl.BlockSpec(memory_space=pltpu.SMEM),
                  pl.BlockSpec((1, C_BLOCK, dq), lambda n, b: (b, n, 0)),
                  kv_spec(k_col, -1), kv_spec(k_col, 0), kv_spec(k_col, 1),
                  kv_spec(v_col, -1), kv_spec(v_col, 0), kv_spec(v_col, 1),
                  pl.BlockSpec((1, C_KV_HEADS, C_GROUP * C_BLOCK, 3 * C_BLOCK), edge_case)],
        out_specs=pl.BlockSpec((1, C_BLOCK, dq), lambda n, b: (b, n, 0)),
        out_shape=jax.ShapeDtypeStruct((B, L, dq), bf16),
        compiler_params=_cparams("parallel", "parallel"),
        name="swa",
    )(sinks, qkv, qkv, qkv, qkv, qkv, qkv, qkv, _swa_bias_tables())


def _hyena_filters(L, w1, b1, w2, b2, w3, b3, w4, freq):
    hp = lax.Precision.HIGHEST
    t = jnp.linspace(0.0, 1.0, L, dtype=f32)[:, None]
    wpos = (2.0 * math.pi / L) * jnp.arange(L, dtype=f32)[:, None]
    bands = jnp.linspace(1e-4, HY_BANDS - 1, HY_BANDS, dtype=f32)[None, :]
    feats = jnp.concatenate([t, jnp.cos(bands * wpos), -jnp.sin(bands * wpos)], axis=-1)
    h = jnp.sin(freq[0] * (jnp.dot(feats, w1, precision=hp) + b1))
    h = jnp.sin(freq[1] * (jnp.dot(h, w2, precision=hp) + b2))
    h = jnp.sin(freq[2] * (jnp.dot(h, w3, precision=hp) + b3))
    h = jnp.dot(h, w4, precision=hp).reshape(L, HY_ORDER, 2, B_WIDTH)
    deltas = jnp.abs(jnp.linspace(HY_MIN_DECAY, HY_MAX_DECAY, B_WIDTH, dtype=f32))
    window = jnp.exp(-t * deltas[None, :])
    return h * window[:, None, None, :]


def _dft_tables(L):
    n_fft = 2 * L
    idx = jnp.arange(L, dtype=jnp.int32)
    ph = (idx[:, None] * idx[None, :]) % n_fft
    ang = ph.astype(f32) * (2.0 * math.pi / n_fft)
    return jnp.cos(ang).astype(bf16), jnp.sin(ang).astype(bf16)


def _pad_rows(w, rows=8):
    return jnp.pad(w, ((0, rows - w.shape[0]), (0, 0)))


def _even_in_weights(w_in):
    hk = A_HEADS * A_DK
    q = w_in[:, 0:hk].reshape(D_MODEL, A_HEADS, A_DK)
    k = w_in[:, hk:2 * hk].reshape(D_MODEL, A_HEADS, A_DK)
    v = w_in[:, 2 * hk:A_QKV].reshape(D_MODEL, A_HEADS, A_DV)
    z = w_in[:, A_QKV:A_QKV + A_WIDTH].reshape(D_MODEL, A_HEADS, A_DV)
    qkvz = jnp.concatenate([q, k, v, z], axis=-1).reshape(D_MODEL, A_HEADS * 4 * A_DK)
    ab = w_in[:, A_QKV + A_WIDTH:A_QKV + A_WIDTH + 4 * A_HEADS].reshape(D_MODEL, 4, A_HEADS)
    ab = jnp.transpose(ab, (0, 2, 1))
    ab = jnp.pad(ab, ((0, 0), (0, 0), (0, LANE - 4))).reshape(D_MODEL, A_HEADS * LANE)
    hx = w_in[:, A_QKV + A_WIDTH + 4 * A_HEADS:]
    return jnp.concatenate([qkvz, ab, hx], axis=1).astype(bf16)


def _router_weights(w_group, w_expert):
    w = jnp.concatenate([w_group, w_expert], axis=1)
    w = jnp.pad(w, ((0, 0), (0, ROUTER_LANES - w.shape[1])))
    hi = w.astype(bf16)
    lo = (w - hi.astype(f32)).astype(bf16)
    return jnp.concatenate([hi, lo], axis=1)


def kernel(x_prompt, x_sample, norm_mix, norm_ffn, norm_final, ev_w_in, ev_conv_a, ev_alog_f, ev_alog_b, ev_dtb_f, ev_dtb_b, ev_onorm, ev_conv_b, hy_w1, hy_b1, hy_w2, hy_b2, hy_w3, hy_b3, hy_w4, hy_freq, hy_dbias, ev_w_out, od_w_qkv, od_sinks, od_w_out, moe_w_group, moe_w_expert, moe_w_gate, moe_w_up, moe_w_down):
    bp, L, _ = x_prompt.shape
    bs = x_sample.shape[0]
    B = bp + bs
    T = B * L
    xs = (x_prompt.reshape(bp * L, D_MODEL), x_sample.reshape(bs * L, D_MODEL))
    n_layers = moe_w_gate.shape[0]
    wg = moe_w_gate.reshape(n_layers * N_EXPERTS, D_MODEL, D_EXPERT)
    wu = moe_w_up.reshape(n_layers * N_EXPERTS, D_MODEL, D_EXPERT)
    wd = moe_w_down.reshape(n_layers * N_EXPERTS, D_EXPERT, D_MODEL)

    w_in = _even_in_weights(ev_w_in[0])
    n_qkvz = A_HEADS * 4 * A_DK
    n_ab = A_HEADS * LANE
    n_hx = 3 * B_WIDTH
    qkvz, abh, hx = _norm_matmul(xs, norm_mix[0], w_in, (n_qkvz, n_ab, n_hx), (bf16, f32, bf16))

    hk = A_HEADS * A_DK
    ca = ev_conv_a[0]
    conv_a = jnp.concatenate([ca[0:hk].reshape(A_HEADS, A_DK, 3), ca[hk:2 * hk].reshape(A_HEADS, A_DK, 3),
                              ca[2 * hk:].reshape(A_HEADS, A_DV, 3)], axis=1)
    conv_a = jnp.pad(jnp.transpose(conv_a, (0, 2, 1)), ((0, 0), (0, 5), (0, 0)))
    gate_sc = jnp.stack([ev_alog_f[0], ev_alog_b[0], ev_dtb_f[0], ev_dtb_b[0]], axis=1)
    o_a = _deltanet(qkvz.reshape(B, L, n_qkvz), abh.reshape(B, L, n_ab), conv_a, gate_sc, ev_onorm[0])

    filt = _hyena_filters(L, hy_w1[0], hy_b1[0], hy_w2[0], hy_b2[0], hy_w3[0], hy_b3[0], hy_w4[0], hy_freq[0])
    hsum = (filt[:, :, 0] + filt[:, :, 1]).reshape(L, HY_ORDER * B_WIDTH)
    hdiff = (filt[:, :, 0] - filt[:, :, 1]).reshape(L, HY_ORDER * B_WIDTH)
    cmat, smat = _dft_tables(L)
    kr, ki, kn = _filter_spectrum(cmat, smat, hsum, hdiff)
    conv_b = _pad_rows(ev_conv_b[0].T)
    o_b = _hyena(hx.reshape(B, L, n_hx), conv_b, cmat, smat, kr, ki, kn, _pad_rows(hy_dbias[0]))

    w_out = ev_w_out[0].astype(bf16)
    x, hpa, hpb, route, route_t, counts = _proj_route(
        xs, (o_a.reshape(T, A_WIDTH), o_b.reshape(T, B_WIDTH)), (w_out[:A_WIDTH], w_out[A_WIDTH:]), norm_ffn[0],
        _router_weights(moe_w_group[0], moe_w_expert[0]))
    x = _moe(0, x, hpa, hpb, route, route_t, counts, wg, wu, wd, norm_final, final_norm=False)

    n_qkv = od_w_qkv.shape[-1]
    dq = C_HEADS * C_HEAD_DIM
    w_qkv = jnp.concatenate([od_w_qkv[0][:, :dq] * (LOG2E * C_HEAD_DIM ** -0.5), od_w_qkv[0][:, dq:]], axis=1)
    (qkv,) = _norm_matmul((x,), norm_mix[1], w_qkv.astype(bf16), (n_qkv,), (bf16,))
    o_c = _swa(qkv.reshape(B, L, n_qkv), od_sinks[0])
    x, hpa, hpb, route, route_t, counts = _proj_route(
        (x,), (o_c.reshape(T, C_HEADS * C_HEAD_DIM),), (od_w_out[0].astype(bf16),), norm_ffn[1],
        _router_weights(moe_w_group[1], moe_w_expert[1]))
    y_prompt, y_sample = _moe(1, x, hpa, hpb, route, route_t, counts, wg, wu, wd, norm_final, final_norm=True,
                              row_ranges=((0, bp * L), (bp * L, bs * L)))
    return (y_prompt.reshape(bp, L, D_MODEL), y_sample.reshape(bs, L, D_MODEL))
```

```python
import functools
import math

import jax
import jax.numpy as jnp
import numpy as np
from jax import lax
from jax.experimental import pallas as pl
from jax.experimental.pallas import tpu as pltpu
from jax.experimental.pallas import tpu_sc as plsc

f32 = jnp.float32
bf16 = jnp.bfloat16

EPS = 1e-6
D_MODEL = 1024

A_HEADS = 4
A_DK = 128
A_DV = 128
A_WIDTH = A_HEADS * A_DV
A_QKV = 2 * A_HEADS * A_DK + A_HEADS * A_DV
DELTA_CHUNK = 128
INV_BASE = 8
DELTA_GROUP = 8

B_WIDTH = D_MODEL - A_WIDTH
HY_ORDER = 2
HY_EMB = 33
HY_BANDS = (HY_EMB - 1) // 2
HY_TARGET = 1e-2
HY_MIN_DECAY = math.log(HY_TARGET) / 1.5
HY_MAX_DECAY = math.log(HY_TARGET) / 0.3
HY_CT = 256
HY_ROWS = 256

C_HEADS = 16
C_KV_HEADS = 4
C_HEAD_DIM = 64
C_GROUP = C_HEADS // C_KV_HEADS
WINDOW = 128
C_BLOCK = 128

N_GROUPS = 4
EXPERTS_PER_GROUP = 8
N_EXPERTS = N_GROUPS * EXPERTS_PER_GROUP
D_EXPERT = 256
ROUTER_LANES = 128

LANE = 128
VMEM_LIMIT = 56 * 1024 * 1024


def _cparams(*sem):
    return pltpu.CompilerParams(dimension_semantics=sem, vmem_limit_bytes=VMEM_LIMIT)


def _dot(a, b):
    return jnp.dot(a, b, preferred_element_type=f32)


def _dot_nt(a, b):
    return lax.dot_general(a, b, (((1,), (1,)), ((), ())), preferred_element_type=f32)


def _dot_tn(a, b):
    return lax.dot_general(a, b, (((0,), (0,)), ((), ())), preferred_element_type=f32)


def _sigmoid(x):
    return 1.0 / (1.0 + jnp.exp(-x))


def _rms(x, g):
    return x * lax.rsqrt(jnp.mean(x * x, axis=-1, keepdims=True) + EPS) * g


def _row_sources(xs, tm):
    tiles = [x.shape[0] // tm for x in xs]
    starts = [sum(tiles[:k]) for k in range(len(xs))]

    def spec(x, lo, n):
        return pl.BlockSpec((tm, x.shape[1]), lambda i: (jnp.clip(i - lo, 0, n - 1), 0))

    return [spec(x, lo, n) for x, lo, n in zip(xs, starts, tiles)], tuple(starts[1:])


def _select_rows(refs, starts):
    i = pl.program_id(0)
    x = refs[0][...]
    for ref, lo in zip(refs[1:], starts):
        x = jnp.where(i >= lo, ref[...], x)
    return x


def _norm_matmul_kernel(*refs, starts):
    n_src = len(starts) + 1
    g_ref, w_ref = refs[n_src:n_src + 2]
    o_refs = refs[n_src + 2:]
    h = _rms(_select_rows(refs[:n_src], starts), g_ref[...]).astype(bf16)
    off = 0
    for o_ref in o_refs:
        n = o_ref.shape[-1]
        o_ref[...] = _dot(h, w_ref[:, off:off + n]).astype(o_ref.dtype)
        off += n


def _norm_matmul(xs, gain, w, splits, dtypes, tm=512):
    T = sum(x.shape[0] for x in xs)
    D = xs[0].shape[1]
    n_all = w.shape[1]
    x_specs, starts = _row_sources(xs, tm)
    return pl.pallas_call(
        functools.partial(_norm_matmul_kernel, starts=starts),
        grid=(T // tm,),
        in_specs=x_specs + [pl.BlockSpec((1, D), lambda i: (0, 0)),
                            pl.BlockSpec((D, n_all), lambda i: (0, 0))],
        out_specs=[pl.BlockSpec((tm, n), lambda i: (i, 0)) for n in splits],
        out_shape=[jax.ShapeDtypeStruct((T, n), dt) for n, dt in zip(splits, dtypes)],
        compiler_params=_cparams("parallel"),
        name="norm_matmul",
    )(*xs, gain.reshape(1, D), w)


MK_LOWER, MK_UPPER, MK_BASE, MK_LEVEL0 = 0, 1, 2, 3
N_LEVELS = int(math.log2(DELTA_CHUNK // INV_BASE))
MK_EYE = MK_LEVEL0 + N_LEVELS
N_MASKS = MK_EYE + 1


def _delta_masks():
    c = np.arange(DELTA_CHUNK)[:, None]
    s = np.arange(DELTA_CHUNK)[None, :]
    m = [s <= c, s >= c, (c // INV_BASE == s // INV_BASE) & (c != s)]
    b = INV_BASE
    while b < DELTA_CHUNK:
        m.append((c // (2 * b) == s // (2 * b)) & (c // b != s // b))
        b *= 2
    m.append(c == s)
    m = np.stack(m).astype(np.float32)
    return jnp.asarray(m), jnp.asarray(m, dtype=bf16)


def _tri_inverse(mats, mk_ref, mk16_ref):
    ps = [a * mk_ref[MK_BASE] for a in mats]
    ts = [mk_ref[MK_EYE] - p for p in ps]
    n = 2
    while n < INV_BASE:
        p16s = [p.astype(bf16) for p in ps]
        ps = [_dot(p16, p16) for p16 in p16s]
        ts = [t + _dot(t.astype(bf16), p.astype(bf16)) for t, p in zip(ts, ps)]
        n *= 2
    a16s = [a.astype(bf16) for a in mats]
    t16s = [t.astype(bf16) for t in ts]
    for lvl in range(N_LEVELS):
        xs = [_dot(a16 * mk16_ref[MK_LEVEL0 + lvl], t16) for a16, t16 in zip(a16s, t16s)]
        t16s = [t16 - _dot(t16, x.astype(bf16)).astype(bf16) for t16, x in zip(t16s, xs)]
    return t16s


def _deltanet_kernel(sc_ref, x_ref, ab_ref, cw_ref, on_ref, mk_ref, mk16_ref, o_ref, mp_s, nr_s, gl_s, o_s):
    hd = pl.program_id(1)
    L = x_ref.shape[1]
    C = DELTA_CHUNK
    nchunk = L // C
    group = math.gcd(DELTA_GROUP, nchunk)
    nqkv = 3 * A_DK
    halo = 16

    lane = lax.broadcasted_iota(jnp.int32, (1, LANE), 1)
    alog = jnp.where(lane == 2, sc_ref[hd, 0], jnp.where(lane == 3, sc_ref[hd, 1], 0.0))
    dtb = jnp.where(lane == 2, sc_ref[hd, 2], jnp.where(lane == 3, sc_ref[hd, 3], 0.0))
    neg_a = jnp.where((lane == 2) | (lane == 3), -jnp.exp(alog), 0.0)
    rowi = lax.broadcasted_iota(jnp.int32, (C, 1), 0)
    cw = cw_ref[0]

    def chunk_inputs(i):
        r0 = pl.multiple_of(i * C, C)
        xc = x_ref[0, pl.ds(r0, C), 0:nqkv].astype(f32)
        rp = pl.multiple_of(jnp.maximum(r0 - halo, 0), halo)
        rn = pl.multiple_of(jnp.minimum(r0 + C, L - halo), halo)
        prow = x_ref[0, pl.ds(rp, halo), 0:nqkv][halo - 1:halo, :].astype(f32)
        nrow = x_ref[0, pl.ds(rn, halo), 0:nqkv][0:1, :].astype(f32)
        prow = jnp.where(i > 0, prow, 0.0)
        nrow = jnp.where(i < nchunk - 1, nrow, 0.0)
        prev = jnp.where(rowi == 0, prow, pltpu.roll(xc, 1, 0))
        nxt = jnp.where(rowi == C - 1, nrow, pltpu.roll(xc, C - 1, 0))
        y = prev * cw[0:1, :] + xc * cw[1:2, :] + nxt * cw[2:3, :]
        y = y * _sigmoid(y)
        q = y[:, 0:A_DK]
        k = y[:, A_DK:2 * A_DK]
        vc = y[:, 2 * A_DK:nqkv]
        qc = q * lax.rsqrt(jnp.sum(q * q, axis=-1, keepdims=True) + EPS) * (A_DK ** -0.5)
        kc = k * lax.rsqrt(jnp.sum(k * k, axis=-1, keepdims=True) + EPS)

        abc = ab_ref[0, pl.ds(r0, C), :]
        zg = abc + dtb
        g = neg_a * (jnp.maximum(zg, 0.0) + jnp.log(1.0 + jnp.exp(-jnp.abs(zg))))
        low16 = mk_ref[MK_LOWER].astype(bf16)
        g1 = g.astype(bf16)
        e1 = g - g1.astype(f32)
        g2 = e1.astype(bf16)
        g3 = (e1 - g2.astype(f32)).astype(bf16)
        pre = _dot(low16, g1) + _dot(low16, g2) + _dot(low16, g3)
        tot = pre[C - 1:C, :]
        suf = tot - pre + g
        beta_all = _sigmoid(abc)

        k16 = kc.astype(bf16)
        kk = _dot_nt(k16, k16)
        qk = _dot_nt(qc.astype(bf16), k16)

        chains = []
        for rev in (0, 1):
            beta = beta_all[:, rev:rev + 1]
            gcc = (suf if rev else pre)[:, 2 + rev:3 + rev]
            tot11 = tot[:, 2 + rev:3 + rev]
            keep = mk_ref[MK_UPPER if rev else MK_LOWER]
            bc = jnp.broadcast_to(gcc * LOG2E, (C, C))
            decay = jnp.exp2((bc - bc.T) * keep) * keep
            a = (beta * kk) * decay
            eg = jnp.exp(gcc)
            chains.append(dict(
                i=i, rev=rev, a=a, gl=jnp.exp(tot11), qd=qc * eg,
                rhs=jnp.concatenate([vc * beta, kc * (beta * eg)], axis=1).astype(bf16),
                kd16=(kc * jnp.exp(tot11 - gcc)).astype(bf16),
                qk16=(qk * decay).astype(bf16)))
        return chains

    def prep(it, carry):
        chains = []
        for gi in range(group):
            chains += chunk_inputs(it * group + gi)
        t16s = _tri_inverse([c["a"] for c in chains], mk_ref, mk16_ref)
        uws = [_dot(t16, c["rhs"]).astype(bf16) for t16, c in zip(t16s, chains)]
        kts = [_dot_tn(c["kd16"], uw) for c, uw in zip(chains, uws)]
        prs = [_dot(c["qk16"], uw) for c, uw in zip(chains, uws)]
        for c, kt, pr in zip(chains, kts, prs):
            rev, i = c["rev"], c["i"]
            mp_s[rev, i, 0:A_DK, :] = (-kt[:, A_DV:]).astype(bf16)
            mp_s[rev, i, A_DK:A_DK + C, :] = (c["qd"] - pr[:, A_DV:]).astype(bf16)
            nr_s[rev, i, 0:A_DK, :] = kt[:, 0:A_DV]
            nr_s[rev, i, A_DK:A_DK + C, :] = pr[:, 0:A_DV]
            gl_s[rev, i] = jnp.broadcast_to(c["gl"], (8, LANE))
        return carry

    lax.fori_loop(0, nchunk // group, prep, 0)

    def scan(i, carry):
        out = []
        for rev, S in enumerate(carry):
            j = (nchunk - 1 - i) if rev else i
            y = _dot(mp_s[rev, j], S.astype(bf16))
            nr = nr_s[rev, j]
            o_s[rev, pl.ds(pl.multiple_of(j * C, C), C), :] = y[A_DK:, :] + nr[A_DK:, :]
            out.append(S * gl_s[rev, j][0:1, :] + y[0:A_DK, :] + nr[0:A_DK, :])
        return tuple(out)

    S0 = jnp.zeros((A_DK, A_DV), f32)
    lax.fori_loop(0, nchunk, scan, (S0, S0))

    def finish(i, carry):
        r0 = pl.multiple_of(i * C, C)
        o = o_s[0, pl.ds(r0, C), :] + o_s[1, pl.ds(r0, C), :]
        zg = x_ref[0, pl.ds(r0, C), nqkv:nqkv + A_DV].astype(f32)
        o_ref[0, pl.ds(r0, C), :] = (_rms(o, on_ref[...]) * (zg * _sigmoid(zg))).astype(o_ref.dtype)
        return carry

    lax.fori_loop(0, nchunk, finish, 0)


def _deltanet(qkvz, abh, conv_w, gate_sc, onorm):
    B, L, _ = qkvz.shape
    C = DELTA_CHUNK
    nchunk = L // C
    hw = 4 * A_DK
    scratch = [pltpu.VMEM((2, nchunk, A_DK + C, A_DV), bf16),
               pltpu.VMEM((2, nchunk, A_DK + C, A_DV), f32),
               pltpu.VMEM((2, nchunk, 8, LANE), f32),
               pltpu.VMEM((2, L, A_DV), f32)]
    return pl.pallas_call(
        _deltanet_kernel,
        grid=(B, A_HEADS),
        in_specs=[pl.BlockSpec(memory_space=pltpu.SMEM),
                  pl.BlockSpec((1, L, hw), lambda b, h: (b, 0, h)),
                  pl.BlockSpec((1, L, LANE), lambda b, h: (b, 0, h)),
                  pl.BlockSpec((1, 8, 3 * A_DK), lambda b, h: (h, 0, 0)),
                  pl.BlockSpec((1, A_DV), lambda b, h: (0, 0)),
                  pl.BlockSpec((N_MASKS, C, C), lambda b, h: (0, 0, 0)),
                  pl.BlockSpec((N_MASKS, C, C), lambda b, h: (0, 0, 0))],
        out_specs=pl.BlockSpec((1, L, A_DV), lambda b, h: (b, 0, h)),
        out_shape=jax.ShapeDtypeStruct((B, L, A_WIDTH), bf16),
        scratch_shapes=scratch,
        compiler_params=_cparams("parallel", "parallel"),
        name="deltanet",
    )(gate_sc, qkvz, abh, conv_w, onorm.reshape(1, A_DV), *_delta_masks())


def _filter_spectrum_kernel(c_ref, s_ref, hs_ref, hd_ref, kr_ref, ki_ref, kn_ref):
    L = c_ref.shape[0]
    n_fft = 2 * L
    hs = hs_ref[...]
    hd = hd_ref[...]

    def split_dot(m_ref, x):
        hi = x.astype(bf16)
        lo = (x - hi.astype(f32)).astype(bf16)
        m = m_ref[...]
        return _dot(m, hi) + _dot(m, lo)

    row = lax.broadcasted_iota(jnp.int32, (L, 1), 0)
    kr_ref[...] = split_dot(c_ref, hs)
    ki_ref[...] = split_dot(s_ref, hd)
    alt = jnp.where(row % 2 == 0, 1.0, -1.0)
    kn = jnp.sum(hs * alt, axis=0, keepdims=True)
    kn_ref[...] = jnp.broadcast_to(kn, kn_ref.shape)


def _filter_spectrum(cmat, smat, hsum, hdiff):
    L, n = hsum.shape
    ct = HY_CT
    return pl.pallas_call(
        _filter_spectrum_kernel,
        grid=(n // ct,),
        in_specs=[pl.BlockSpec((L, L), lambda j: (0, 0)),
                  pl.BlockSpec((L, L), lambda j: (0, 0)),
                  pl.BlockSpec((L, ct), lambda j: (0, j)),
                  pl.BlockSpec((L, ct), lambda j: (0, j))],
        out_specs=[pl.BlockSpec((L, ct), lambda j: (0, j)),
                   pl.BlockSpec((L, ct), lambda j: (0, j)),
                   pl.BlockSpec((8, ct), lambda j: (0, j))],
        out_shape=[jax.ShapeDtypeStruct((L, n), f32),
                   jax.ShapeDtypeStruct((L, n), f32),
                   jax.ShapeDtypeStruct((8, n), f32)],
        compiler_params=_cparams("parallel"),
    )(cmat, smat, hsum, hdiff)


def _hyena_kernel(c_ref, s_ref, x1_ref, x2_ref, v_ref, w1_ref, w2_ref, wv_ref,
                  kr0_ref, ki0_ref, kn0_ref, kr1_ref, ki1_ref, kn1_ref, db_ref, o_ref,
                  z_s, z16_s, yr_s, yi_s, nyq_s):
    L = c_ref.shape[0]
    R = HY_ROWS
    nt = L // R
    halo = 16
    rowi = lax.broadcasted_iota(jnp.int32, (R, 1), 0)
    alt = jnp.where(rowi % 2 == 0, 1.0, -1.0)

    def dwconv_rows(x_ref, w_ref, t, r0):
        x = x_ref[0, pl.ds(r0, R), :].astype(f32)
        rp = pl.multiple_of(jnp.maximum(r0 - halo, 0), halo)
        rn = pl.multiple_of(jnp.minimum(r0 + R, L - halo), halo)
        prow = jnp.where(t > 0, x_ref[0, pl.ds(rp, halo), :][halo - 1:halo, :].astype(f32), 0.0)
        nrow = jnp.where(t < nt - 1, x_ref[0, pl.ds(rn, halo), :][0:1, :].astype(f32), 0.0)
        prev = jnp.where(rowi == 0, prow, pltpu.roll(x, 1, 0))
        nxt = jnp.where(rowi == R - 1, nrow, pltpu.roll(x, R - 1, 0))
        w = w_ref[...]
        return prev * w[0:1, :] + x * w[1:2, :] + nxt * w[2:3, :]

    def load_v(t, acc):
        r0 = pl.multiple_of(t * R, R)
        z = dwconv_rows(v_ref, wv_ref, t, r0)
        z_s[pl.ds(r0, R), :] = z
        z16_s[pl.ds(r0, R), :] = z.astype(bf16)
        return acc + jnp.sum(z * alt, axis=0, keepdims=True)

    nyq = lax.fori_loop(0, nt, load_v, jnp.zeros((1, z_s.shape[1]), f32))

    stages = ((x1_ref, w1_ref, kr0_ref, ki0_ref, kn0_ref), (x2_ref, w2_ref, kr1_ref, ki1_ref, kn1_ref))
    for o, (g_ref, gw_ref, kr_ref, ki_ref, kn_ref) in enumerate(stages):
        nyq_s[...] = jnp.broadcast_to(nyq * kn_ref[0:1, :], nyq_s.shape)

        def spectrum(t, carry):
            r0 = pl.multiple_of(t * R, R)
            z16 = z16_s[...]
            ur = _dot(c_ref[pl.ds(r0, R), :], z16)
            ui = _dot(s_ref[pl.ds(r0, R), :], z16)
            kr = kr_ref[pl.ds(r0, R), :]
            ki = ki_ref[pl.ds(r0, R), :]
            yr_s[pl.ds(r0, R), :] = (ur * kr - ui * ki).astype(bf16)
            yi_s[pl.ds(r0, R), :] = (ur * ki + ui * kr).astype(bf16)
            return carry

        lax.fori_loop(0, nt, spectrum, 0)

        def signal(t, acc):
            r0 = pl.multiple_of(t * R, R)
            y = _dot(c_ref[pl.ds(r0, R), :], yr_s[...]) + _dot(s_ref[pl.ds(r0, R), :], yi_s[...])
            y = y + alt * nyq_s[0:1, :]
            z = dwconv_rows(g_ref, gw_ref, t, r0) * (y + z_s[pl.ds(r0, R), :] * db_ref[o:o + 1, :])
            if o + 1 < len(stages):
                z_s[pl.ds(r0, R), :] = z
                z16_s[pl.ds(r0, R), :] = z.astype(bf16)
            else:
                o_ref[0, pl.ds(r0, R), :] = z.astype(o_ref.dtype)
            return acc + jnp.sum(z * alt, axis=0, keepdims=True)

        nyq = lax.fori_loop(0, nt, signal, jnp.zeros((1, z_s.shape[1]), f32))


def _hyena(hx, conv_w, cmat, smat, kr, ki, kn, dbias):
    B, L, _ = hx.shape
    ct = HY_CT
    nct = B_WIDTH // ct
    once = pl.Buffered(1)

    def xspec(part):
        return pl.BlockSpec((1, L, ct), lambda c, b: (b, 0, part * nct + c))

    def wspec(part):
        return pl.BlockSpec((8, ct), lambda c, b: (0, part * nct + c))

    def kspec(order, rows):
        return pl.BlockSpec((rows, ct), lambda c, b: (0, order * nct + c), pipeline_mode=once)

    return pl.pallas_call(
        _hyena_kernel,
        grid=(nct, B),
        in_specs=[pl.BlockSpec((L, L), lambda c, b: (0, 0), pipeline_mode=once),
                  pl.BlockSpec((L, L), lambda c, b: (0, 0), pipeline_mode=once),
                  xspec(0), xspec(1), xspec(2), wspec(0), wspec(1), wspec(2),
                  kspec(0, L), kspec(0, L), kspec(0, 8), kspec(1, L), kspec(1, L), kspec(1, 8),
                  pl.BlockSpec((8, ct), lambda c, b: (0, c))],
        out_specs=pl.BlockSpec((1, L, ct), lambda c, b: (b, 0, c)),
        out_shape=jax.ShapeDtypeStruct((B, L, B_WIDTH), bf16),
        scratch_shapes=[pltpu.VMEM((L, ct), f32), pltpu.VMEM((L, ct), bf16),
                        pltpu.VMEM((L, ct), bf16), pltpu.VMEM((L, ct), bf16), pltpu.VMEM((8, ct), f32)],
        compiler_params=_cparams("parallel", "parallel"),
        name="hyena",
    )(cmat, smat, hx, hx, hx, conv_w, conv_w, conv_w, kr, ki, kn, kr, ki, kn, dbias)


FFT_N1 = 16
FFT_K1 = FFT_N1 // 2 + 1
FFT_Z1 = FFT_N1 // 2
FFT_ROWS = 32


def _lincomb(terms):
    groups = {}
    for c, a in terms:
        if abs(c) > 1e-9:
            groups.setdefault(round(abs(c), 9), []).append((c > 0, a))
    total = None
    for mag, items in groups.items():
        pos = [a for p, a in items if p]
        neg = [a for p, a in items if not p]
        s = None
        for a in pos:
            s = a if s is None else s + a
        for a in neg:
            s = -a if s is None else s - a
        if mag != 1.0:
            s = s * mag
        total = s if total is None else total + s
    return total


def _hyena_fft_kernel(m2f_ref, m2i_ref, twr_ref, twi_ref, x1_ref, x2_ref, v_ref, w1_ref, w2_ref, wv_ref,
                      kr0_ref, ki0_ref, kr1_ref, ki1_ref, db_ref, o_ref, z_s, g_s, d_s, y_s, b_s):
    L, ct = z_s.shape
    n2 = L // FFT_Z1
    R = min(HY_ROWS, L)
    nt = L // R
    r1 = min(FFT_ROWS, n2)
    halo = 16
    rowi = lax.broadcasted_iota(jnp.int32, (R, 1), 0)
    ang = 2.0 * math.pi / FFT_N1
    c16 = [[math.cos(ang * a * b) for b in range(FFT_K1)] for a in range(FFT_Z1)]
    s16 = [[math.sin(ang * a * b) for b in range(FFT_K1)] for a in range(FFT_Z1)]

    def conv_into(dst, x_ref, w_ref):
        def body(t, carry):
            r0 = pl.multiple_of(t * R, R)
            x = x_ref[0, pl.ds(r0, R), :].astype(f32)
            rp = pl.multiple_of(jnp.maximum(r0 - halo, 0), halo)
            rn = pl.multiple_of(jnp.minimum(r0 + R, L - halo), halo)
            prow = jnp.where(t > 0, x_ref[0, pl.ds(rp, halo), :][halo - 1:halo, :].astype(f32), 0.0)
            nrow = jnp.where(t < nt - 1, x_ref[0, pl.ds(rn, halo), :][0:1, :].astype(f32), 0.0)
            prev = jnp.where(rowi == 0, prow, pltpu.roll(x, 1, 0))
            nxt = jnp.where(rowi == R - 1, nrow, pltpu.roll(x, R - 1, 0))
            w = w_ref[...]
            dst[pl.ds(r0, R), :] = prev * w[0:1, :] + x * w[1:2, :] + nxt * w[2:3, :]
            return carry
        lax.fori_loop(0, nt, body, 0)

    def cols(k1):
        return slice(k1 * ct, (k1 + 1) * ct)

    conv_into(z_s, v_ref, wv_ref)
    stages = ((x1_ref, w1_ref, kr0_ref, ki0_ref), (x2_ref, w2_ref, kr1_ref, ki1_ref))
    for o, (g_ref, gw_ref, kr_ref, ki_ref) in enumerate(stages):
        conv_into(g_s, g_ref, gw_ref)

        def forward1(rt, carry):
            r0 = pl.multiple_of(rt * r1, r1)
            zs = [z_s[pl.ds(a * n2 + r0, r1), :] for a in range(FFT_Z1)]
            for k1 in range(FFT_K1):
                ar = _lincomb([(c16[a][k1], zs[a]) for a in range(FFT_Z1)])
                ai = _lincomb([(-s16[a][k1], zs[a]) for a in range(FFT_Z1)])
                twr = twr_ref[pl.ds(r0, r1), cols(k1)]
                twi = twi_ref[pl.ds(r0, r1), cols(k1)]
                if ai is None:
                    pr, pi = ar * twr, -(ar * twi)
                else:
                    pr, pi = ar * twr + ai * twi, ai * twr - ar * twi
                d_s[pl.ds(r0, r1), cols(k1)] = pr.astype(bf16)
                d_s[pl.ds(n2 + r0, r1), cols(k1)] = pi.astype(bf16)
            return carry

        lax.fori_loop(0, n2 // r1, forward1, 0)

        for k1 in range(FFT_K1):
            xk = _dot(m2f_ref[...], d_s[:, cols(k1)])
            xr, xi = xk[0:n2, :], xk[n2:, :]
            kr, ki = kr_ref[:, cols(k1)], ki_ref[:, cols(k1)]
            y_s[0:n2, cols(k1)] = (xr * kr - xi * ki).astype(bf16)
            y_s[n2:2 * n2, cols(k1)] = (xr * ki + xi * kr).astype(bf16)

        for k1 in range(FFT_K1):
            bk = _dot(m2i_ref[...], y_s[:, cols(k1)])
            br, bi = bk[0:n2, :], bk[n2:, :]
            twr, twi = twr_ref[:, cols(k1)], twi_ref[:, cols(k1)]
            b_s[0:n2, cols(k1)] = br * twr - bi * twi
            b_s[n2:2 * n2, cols(k1)] = br * twi + bi * twr

        def inverse1(rt, carry):
            r0 = pl.multiple_of(rt * r1, r1)
            brs = [b_s[pl.ds(r0, r1), cols(j)] for j in range(FFT_K1)]
            bis = [b_s[pl.ds(n2 + r0, r1), cols(j)] for j in range(FFT_K1)]
            for a in range(FFT_Z1):
                terms = [(1.0, brs[0]), (-1.0 if a % 2 else 1.0, brs[FFT_K1 - 1])]
                for j in range(1, FFT_K1 - 1):
                    terms += [(2.0 * c16[a][j], brs[j]), (-2.0 * s16[a][j], bis[j])]
                rows = pl.ds(a * n2 + r0, r1)
                z = g_s[rows, :] * (_lincomb(terms) + z_s[rows, :] * db_ref[o:o + 1, :])
                if o + 1 < len(stages):
                    z_s[rows, :] = z
                else:
                    o_ref[0, rows, :] = z.astype(o_ref.dtype)
            return carry

        lax.fori_loop(0, n2 // r1, inverse1, 0)


def _hyena_tables(L, kr, ki, kn, nct):
    n_fft = 2 * L
    n2 = L // FFT_Z1
    ct = HY_CT
    idx = jnp.arange(n2, dtype=jnp.int32)
    a2 = ((idx[:, None] * idx[None, :]) % n2).astype(f32) * (2.0 * math.pi / n2)
    c2, s2 = jnp.cos(a2), jnp.sin(a2)
    m2f = jnp.block([[c2, s2], [-s2, c2]]).astype(bf16)
    m2i = jnp.block([[c2, -s2], [s2, c2]]).astype(bf16)
    at = (idx[:, None] * jnp.arange(FFT_K1, dtype=jnp.int32)[None, :]).astype(f32) * (2.0 * math.pi / n_fft)
    twr = jnp.broadcast_to(jnp.cos(at)[:, :, None], (n2, FFT_K1, ct)).reshape(n2, FFT_K1 * ct)
    twi = jnp.broadcast_to(jnp.sin(at)[:, :, None], (n2, FFT_K1, ct)).reshape(n2, FFT_K1 * ct)
    n = kr.shape[1]
    full_r = jnp.concatenate([kr, kn[0:1], kr[1:][::-1]], axis=0) * (1.0 / n_fft)
    full_i = jnp.concatenate([-ki, jnp.zeros((1, n), f32), ki[1:][::-1]], axis=0) * (1.0 / n_fft)

    def arrange(t):
        t = t.reshape(n2, FFT_N1, n // ct, ct)[:, :FFT_K1]
        return jnp.transpose(t, (0, 2, 1, 3)).reshape(n2, (n // ct) * FFT_K1 * ct)

    return m2f, m2i, twr, twi, arrange(full_r), arrange(full_i)


def _hyena_fft(hx, conv_w, tables, dbias):
    B, L, _ = hx.shape
    ct = HY_CT
    nct = B_WIDTH // ct
    n2 = L // FFT_Z1
    wide = FFT_K1 * ct
    once = pl.Buffered(1)
    m2f, m2i, twr, twi, kr, ki = tables

    def xspec(part):
        return pl.BlockSpec((1, L, ct), lambda c, b: (b, 0, part * nct + c))

    def wspec(part):
        return pl.BlockSpec((8, ct), lambda c, b: (0, part * nct + c))

    def kspec(order):
        return pl.BlockSpec((n2, wide), lambda c, b: (0, order * nct + c), pipeline_mode=once)

    def const(shape):
        return pl.BlockSpec(shape, lambda c, b: (0, 0), pipeline_mode=once)

    return pl.pallas_call(
        _hyena_fft_kernel,
        grid=(nct, B),
        in_specs=[const((2 * n2, 2 * n2)), const((2 * n2, 2 * n2)), const((n2, wide)), const((n2, wide)),
                  xspec(0), xspec(1), xspec(2), wspec(0), wspec(1), wspec(2),
                  kspec(0), kspec(0), kspec(1), kspec(1),
                  pl.BlockSpec((8, ct), lambda c, b: (0, c))],
        out_specs=pl.BlockSpec((1, L, ct), lambda c, b: (b, 0, c)),
        out_shape=jax.ShapeDtypeStruct((B, L, B_WIDTH), bf16),
        scratch_shapes=[pltpu.VMEM((L, ct), f32), pltpu.VMEM((L, ct), f32),
                        pltpu.VMEM((2 * n2, wide), bf16), pltpu.VMEM((2 * n2, wide), bf16),
                        pltpu.VMEM((2 * n2, wide), f32)],
        compiler_params=_cparams("parallel", "parallel"),
        name="hyena_fft",
    )(m2f, m2i, twr, twi, hx, hx, hx, conv_w, conv_w, conv_w, kr, ki, kr, ki, dbias)


RT_E1, RT_E2, RT_W1, RT_W2, RT_R1, RT_R2 = 0, 1, 2, 3, 4, 5
ROUTE_TM = 512
EXPERT_TM = 512
SC_WINDOW = 128


def _pack_bf16_pairs(x):
    n = x.shape[1] // 2
    bits = lax.bitcast_convert_type(x.astype(bf16).astype(f32), jnp.uint32)
    return bits[:, :n] | (bits[:, n:] >> 16)


def _unpack_bf16_pairs(p):
    hi = lax.bitcast_convert_type(p & jnp.uint32(0xFFFF0000), f32)
    lo = lax.bitcast_convert_type(p << 16, f32)
    return hi, lo


def _route(logits, low_ref):
    lane = lax.broadcasted_iota(jnp.int32, logits.shape, 1)
    neg = -jnp.inf
    big = ROUTER_LANES
    gl = jnp.where(lane < N_GROUPS, logits, neg)
    gmax = jnp.max(gl, axis=-1, keepdims=True)
    gsel = jnp.min(jnp.where(gl == gmax, lane, big), axis=-1, keepdims=True)
    psel = 1.0 / jnp.sum(jnp.exp(gl - gmax), axis=-1, keepdims=True)
    first = N_GROUPS + gsel * EXPERTS_PER_GROUP
    el = jnp.where((lane >= first) & (lane < first + EXPERTS_PER_GROUP), logits, neg)
    t1 = jnp.max(el, axis=-1, keepdims=True)
    i1 = jnp.min(jnp.where(el == t1, lane, big), axis=-1, keepdims=True)
    el2 = jnp.where(lane == i1, neg, el)
    t2 = jnp.max(el2, axis=-1, keepdims=True)
    i2 = jnp.min(jnp.where(el2 == t2, lane, big), axis=-1, keepdims=True)
    ex = jnp.exp(t2 - t1)
    w1 = psel / (1.0 + ex)
    w2 = ex * w1
    e1 = i1 - N_GROUPS
    e2 = i2 - N_GROUPS
    onehot = jnp.where((lane == e1) | (lane == e2), 1.0, 0.0)
    before = _dot(low_ref[...], onehot.astype(bf16))
    r1 = jnp.sum(jnp.where(lane == e1, before, 0.0), axis=-1, keepdims=True)
    r2 = jnp.sum(jnp.where(lane == e2, before, 0.0), axis=-1, keepdims=True)
    tm = logits.shape[0]
    counts = before[tm - 1:tm, :] + onehot[tm - 1:tm, :]
    rec = jnp.zeros(logits.shape, f32)
    for ln, val in ((RT_E1, e1.astype(f32)), (RT_E2, e2.astype(f32)), (RT_W1, w1), (RT_W2, w2), (RT_R1, r1), (RT_R2, r2)):
        rec = jnp.where(lane == ln, val, rec)
    return rec, counts


def _proj_route_kernel(*refs, n_in, starts):
    n_src = len(starts) + 1
    x = _select_rows(refs[:n_src], starts)
    refs = refs[n_src:]
    a_refs = refs[0:n_in]
    w_refs = refs[n_in:2 * n_in]
    g_ref, wr_ref, low_ref, xo_ref, hpa_ref, hpb_ref, rt_ref, rtt_ref, cnt_ref = refs[2 * n_in:]
    for a_ref, w_ref in zip(a_refs, w_refs):
        x = x + _dot(a_ref[...], w_ref[...])
    xo_ref[...] = x
    h = _rms(x, g_ref[...])
    hp = _pack_bf16_pairs(h)
    quarter = hp.shape[1] // 2
    hpa_ref[...] = hp[:, :quarter]
    hpb_ref[...] = hp[:, quarter:]
    h_hi = h.astype(bf16)
    h_lo = (h - h_hi.astype(f32)).astype(bf16)
    both = _dot(h_hi, wr_ref[...])
    logits = both[:, :ROUTER_LANES] + both[:, ROUTER_LANES:] + _dot(h_lo, wr_ref[:, :ROUTER_LANES])
    rec, counts = _route(logits, low_ref)
    rt_ref[...] = rec
    rtt_ref[...] = rec.T[0:rtt_ref.shape[0], :]
    cnt_ref[0] = jnp.broadcast_to(counts, cnt_ref.shape[1:])


def _proj_route(xs, acts, ws, gain, w_router):
    T = sum(x.shape[0] for x in xs)
    D = xs[0].shape[1]
    tm = ROUTE_TM
    n_in = len(acts)
    low = jnp.asarray(np.tril(np.ones((tm, tm), np.float32), -1), dtype=bf16)
    in_specs, starts = _row_sources(xs, tm)
    in_specs += [pl.BlockSpec((tm, a.shape[1]), lambda i: (i, 0)) for a in acts]
    in_specs += [pl.BlockSpec(w.shape, lambda i: (0, 0)) for w in ws]
    in_specs += [pl.BlockSpec((1, D), lambda i: (0, 0)),
                 pl.BlockSpec((D, 2 * ROUTER_LANES), lambda i: (0, 0)),
                 pl.BlockSpec((tm, tm), lambda i: (0, 0))]
    return pl.pallas_call(
        functools.partial(_proj_route_kernel, n_in=n_in, starts=starts),
        grid=(T // tm,),
        in_specs=in_specs,
        out_specs=[pl.BlockSpec((tm, D), lambda i: (i, 0)),
                   pl.BlockSpec((tm, D // 4), lambda i: (i, 0)),
                   pl.BlockSpec((tm, D // 4), lambda i: (i, 0)),
                   pl.BlockSpec((tm, ROUTER_LANES), lambda i: (i, 0)),
                   pl.BlockSpec((8, tm), lambda i: (0, i)),
                   pl.BlockSpec((1, 8, ROUTER_LANES), lambda i: (i, 0, 0))],
        out_shape=[jax.ShapeDtypeStruct((T, D), f32),
                   jax.ShapeDtypeStruct((T, D // 4), jnp.uint32),
                   jax.ShapeDtypeStruct((T, D // 4), jnp.uint32),
                   jax.ShapeDtypeStruct((T, ROUTER_LANES), f32),
                   jax.ShapeDtypeStruct((8, T), f32),
                   jax.ShapeDtypeStruct((T // tm, 8, ROUTER_LANES), f32)],
        compiler_params=_cparams("parallel"),
        name="proj_route",
    )(*xs, *acts, *ws, gain.reshape(1, D), w_router, low)


def _moe_rows(T):
    return 2 * T + N_EXPERTS * EXPERT_TM


def _moe_plan(route_t, counts):
    T = route_t.shape[1]
    tm = ROUTE_TM
    cnt = counts[:, 0, :N_EXPERTS].astype(jnp.int32)
    total = jnp.sum(cnt, axis=0)
    padded = ((total + EXPERT_TM - 1) // EXPERT_TM) * EXPERT_TM
    ends = jnp.cumsum(padded)
    tile_base = (ends - padded)[None, :] + jnp.cumsum(cnt, axis=0) - cnt
    base = jnp.broadcast_to(tile_base.T[:, :, None], (N_EXPERTS, T // tm, tm)).reshape(N_EXPERTS, T)
    ids = jnp.arange(N_EXPERTS, dtype=jnp.int32)[:, None]
    dest = []
    for e_row, r_row in ((RT_E1, RT_R1), (RT_E2, RT_R2)):
        e = route_t[e_row].astype(jnp.int32)
        dest.append(jnp.sum(jnp.where(ids == e[None, :], base, 0), axis=0) + route_t[r_row].astype(jnp.int32))
    n_tiles = _moe_rows(T) // EXPERT_TM
    n_used = ends[-1] // EXPERT_TM
    starts = jnp.arange(n_tiles, dtype=jnp.int32) * EXPERT_TM
    tile_expert = jnp.sum((starts[:, None] >= ends[None, :]).astype(jnp.int32), axis=1)
    last_expert = jnp.sum((jnp.maximum(n_used - 1, 0) * EXPERT_TM >= ends).astype(jnp.int32))
    tile_expert = jnp.minimum(tile_expert, last_expert)
    return dest, tile_expert, n_used.reshape(1).astype(jnp.int32)


def _sc_mesh():
    return plsc.VectorSubcoreMesh(core_axis_name="core", subcore_axis_name="subcore")


def _sc_scatter_pair(rows, d0, d1, n_out):
    T, W = rows.shape
    win = SC_WINDOW

    @pl.kernel(out_type=jax.ShapeDtypeStruct((n_out, W), rows.dtype), mesh=_sc_mesh(), name="moe_sc_scatter")
    def scatter(x_hbm, i0_hbm, i1_hbm, o_hbm):
        def body(x_vmem, i0_vmem, i1_vmem):
            pltpu.sync_copy(x_vmem, o_hbm.at[i0_vmem.at[0]])
            pltpu.sync_copy(x_vmem, o_hbm.at[i1_vmem.at[0]])

        pltpu.emit_pipeline(
            body, grid=(T // win,),
            in_specs=[pl.BlockSpec((win, W), index_map=lambda i: (i, 0)),
                      pl.BlockSpec((1, win), index_map=lambda i: (0, i)),
                      pl.BlockSpec((1, win), index_map=lambda i: (0, i))],
            out_specs=[],
            core_axis_name=("core", "subcore"), dimension_semantics=(pltpu.PARALLEL,),
        )(x_hbm, i0_hbm, i1_hbm)

    return scatter(rows, d0.reshape(1, T), d1.reshape(1, T))


def _sc_gather(table, idx):
    n = idx.shape[0]
    W = table.shape[1]
    win = SC_WINDOW

    @pl.kernel(out_type=jax.ShapeDtypeStruct((n, W), table.dtype), mesh=_sc_mesh(), name="moe_sc_gather")
    def gather(t_hbm, i_hbm, o_hbm):
        def body(i_vmem, o_vmem):
            pltpu.sync_copy(t_hbm.at[i_vmem.at[0]], o_vmem)

        pltpu.emit_pipeline(
            body, grid=(n // win,),
            in_specs=[pl.BlockSpec((1, win), index_map=lambda i: (0, i))],
            out_specs=[pl.BlockSpec((win, W), index_map=lambda i: (i, 0))],
            core_axis_name=("core", "subcore"), dimension_semantics=(pltpu.PARALLEL,),
        )(i_hbm, o_hbm)

    return gather(table, idx.reshape(1, n))


def _expert_kernel(te_ref, nu_ref, xa_ref, xb_ref, wg_ref, wu_ref, wd_ref, ya_ref, yb_ref):
    @pl.when(pl.program_id(0) < nu_ref[0])
    def _():
        hi, lo = _unpack_bf16_pairs(jnp.concatenate([xa_ref[...], xb_ref[...]], axis=1))
        x = jnp.concatenate([hi, lo], axis=1).astype(bf16)
        gate = _dot(x, wg_ref[0].astype(bf16))
        up = _dot(x, wu_ref[0].astype(bf16))
        hg = (gate * _sigmoid(gate) * up).astype(bf16)
        y = _pack_bf16_pairs(_dot(hg, wd_ref[0].astype(bf16)))
        quarter = y.shape[1] // 2
        ya_ref[...] = y[:, :quarter]
        yb_ref[...] = y[:, quarter:]

    @pl.when(pl.program_id(0) >= nu_ref[0])
    def _():
        ya_ref[...] = jnp.zeros(ya_ref.shape, ya_ref.dtype)
        yb_ref[...] = jnp.zeros(yb_ref.shape, yb_ref.dtype)


def _experts(tile_expert, n_used, xa, xb, wg, wu, wd):
    n_rows, quarter = xa.shape
    D = 4 * quarter
    tm = EXPERT_TM
    grid_spec = pltpu.PrefetchScalarGridSpec(
        num_scalar_prefetch=2,
        grid=(n_rows // tm,),
        in_specs=[pl.BlockSpec((tm, quarter), lambda i, te, nu: (i, 0)),
                  pl.BlockSpec((tm, quarter), lambda i, te, nu: (i, 0)),
                  pl.BlockSpec((1, D, D_EXPERT), lambda i, te, nu: (te[i], 0, 0)),
                  pl.BlockSpec((1, D, D_EXPERT), lambda i, te, nu: (te[i], 0, 0)),
                  pl.BlockSpec((1, D_EXPERT, D), lambda i, te, nu: (te[i], 0, 0))],
        out_specs=[pl.BlockSpec((tm, quarter), lambda i, te, nu: (i, 0)),
                   pl.BlockSpec((tm, quarter), lambda i, te, nu: (i, 0))])
    return pl.pallas_call(
        _expert_kernel,
        grid_spec=grid_spec,
        out_shape=[jax.ShapeDtypeStruct((n_rows, quarter), jnp.uint32)] * 2,
        compiler_params=_cparams("arbitrary"),
        name="moe_experts",
    )(tile_expert, n_used, xa, xb, wg, wu, wd)


def _combine_kernel(x_ref, rt_ref, gf_ref, a1_ref, b1_ref, a2_ref, b2_ref, o_ref, *, final_norm):
    half = x_ref.shape[1] // 2
    h1, l1 = _unpack_bf16_pairs(jnp.concatenate([a1_ref[...], b1_ref[...]], axis=1))
    h2, l2 = _unpack_bf16_pairs(jnp.concatenate([a2_ref[...], b2_ref[...]], axis=1))
    w1 = rt_ref[:, RT_W1:RT_W1 + 1]
    w2 = rt_ref[:, RT_W2:RT_W2 + 1]
    out = jnp.concatenate([x_ref[:, 0:half] + w1 * h1 + w2 * h2, x_ref[:, half:] + w1 * l1 + w2 * l2], axis=1)
    if final_norm:
        out = _rms(out, gf_ref[...])
    o_ref[...] = out


def _combine(x, route, ga, gb, gain_final, final_norm, row_ranges=None):
    T, D = x.shape
    tm = ROUTE_TM
    n = T // tm
    quarter = D // 4
    outs = []
    for start, size in (row_ranges or ((0, T),)):
        lo = start // tm
        slot1 = pl.BlockSpec((tm, quarter), lambda i, lo=lo: (i + lo, 0))
        slot2 = pl.BlockSpec((tm, quarter), lambda i, lo=lo: (i + lo + n, 0))
        outs.append(pl.pallas_call(
            functools.partial(_combine_kernel, final_norm=final_norm),
            grid=(size // tm,),
            in_specs=[pl.BlockSpec((tm, D), lambda i, lo=lo: (i + lo, 0)),
                      pl.BlockSpec((tm, ROUTER_LANES), lambda i, lo=lo: (i + lo, 0)),
                      pl.BlockSpec((1, D), lambda i: (0, 0)),
                      slot1, slot1, slot2, slot2],
            out_specs=pl.BlockSpec((tm, D), lambda i: (i, 0)),
            out_shape=jax.ShapeDtypeStruct((size, D), f32),
            compiler_params=_cparams("parallel"),
            name="moe_combine",
        )(x, route, gain_final.reshape(1, D), ga, gb, ga, gb))
    return outs if row_ranges else outs[0]


def _moe(layer, x, hpa, hpb, route, route_t, counts, wg, wu, wd, gain_final, final_norm, row_ranges=None):
    (d0, d1), tile_expert, n_used = _moe_plan(route_t, counts)
    n_rows = _moe_rows(x.shape[0])
    xa = _sc_scatter_pair(hpa, d0, d1, n_rows)
    xb = _sc_scatter_pair(hpb, d0, d1, n_rows)
    ya, yb = _experts(tile_expert + layer * N_EXPERTS, n_used, xa, xb, wg, wu, wd)
    dcat = jnp.concatenate([d0, d1])
    return _combine(x, route, _sc_gather(ya, dcat), _sc_gather(yb, dcat), gain_final, final_norm, row_ranges)


LOG2E = 1.4426950408889634


def _swa_bias_tables():
    blk = C_BLOCK
    qi = np.arange(blk)[:, None]
    kj = np.arange(3 * blk)[None, :]
    dist = np.abs(blk + qi - kj)
    tabs = np.full((3, C_KV_HEADS, C_GROUP * blk, 3 * blk), -np.inf, np.float32)
    for case in range(3):
        ok = dist <= WINDOW
        if case == 0:
            ok = ok & (kj >= blk)
        if case == 2:
            ok = ok & (kj < 2 * blk)
        for hh in range(C_HEADS):
            slope = 2.0 ** (-8.0 * (hh + 1) / C_HEADS)
            t = np.where(ok, -slope * LOG2E * dist, -np.inf)
            kvh, g = divmod(hh, C_GROUP)
            tabs[case, kvh, g * blk:(g + 1) * blk] = t
    return jnp.asarray(tabs)


def _swa_kernel(sink_ref, q_ref, kp_ref, kc_ref, kn_ref, vp_ref, vc_ref, vn_ref, bias_ref, o_ref):
    blk = C_BLOCK
    hd = C_HEAD_DIM
    kcat = jnp.concatenate([kp_ref[0], kc_ref[0], kn_ref[0]], axis=0)
    vcat = jnp.concatenate([vp_ref[0], vc_ref[0], vn_ref[0]], axis=0)
    q = q_ref[0]
    rblk = lax.broadcasted_iota(jnp.int32, (C_GROUP * blk, 1), 0) // blk
    lane = lax.broadcasted_iota(jnp.int32, (3 * blk, LANE - hd), 1)
    ones_pad = jnp.where(lane == 0, 1.0, 0.0).astype(bf16)
    q4s, sinks = [], []
    for kvh in range(C_KV_HEADS):
        q4s.append(jnp.concatenate(
            [q[:, (kvh * C_GROUP + g) * hd:(kvh * C_GROUP + g + 1) * hd] for g in range(C_GROUP)], axis=0))
        sk = jnp.zeros((C_GROUP * blk, 1), f32)
        for g in range(C_GROUP):
            sk = jnp.where(rblk == g, sink_ref[kvh * C_GROUP + g] * LOG2E, sk)
        sinks.append(sk)
    ss = [_dot_nt(q4s[h], kcat[:, h * hd:(h + 1) * hd]) + bias_ref[0, h] for h in range(C_KV_HEADS)]
    ms = [jnp.maximum(jnp.max(s, axis=-1, keepdims=True), sk) for s, sk in zip(ss, sinks)]
    ps = [jnp.exp2(s - m).astype(bf16) for s, m in zip(ss, ms)]
    pvs = [_dot(ps[h], jnp.concatenate([vcat[:, h * hd:(h + 1) * hd], ones_pad], axis=1)) for h in range(C_KV_HEADS)]
    outs = []
    for h in range(C_KV_HEADS):
        den = pvs[h][:, hd:hd + 1] + jnp.exp2(sinks[h] - ms[h])
        o4 = pvs[h][:, 0:hd] / den
        outs += [o4[g * blk:(g + 1) * blk, :] for g in range(C_GROUP)]
    o_ref[0] = jnp.concatenate(outs, axis=1).astype(o_ref.dtype)


def _swa(qkv, sinks):
    B, L, _ = qkv.shape
    nb = L // C_BLOCK
    dq = C_HEADS * C_HEAD_DIM
    dkv = C_KV_HEADS * C_HEAD_DIM
    k_col = dq // dkv
    v_col = k_col + 1

    def kv_spec(col, shift):
        return pl.BlockSpec((1, C_BLOCK, dkv),
                            lambda n, b: (b, jnp.clip(n + shift, 0, nb - 1), col))

    def edge_case(n, b):
        return (jnp.where(n == 0, 0, jnp.where(n == nb - 1, 2, 1)), 0, 0, 0)

    return pl.pallas_call(
        _swa_kernel,
        grid=(nb, B),
        in_specs=[pl.BlockSpec(memory_space=pltpu.SMEM),
                  pl.BlockSpec((1, C_BLOCK, dq), lambda n, b: (b, n, 0)),
                  kv_spec(k_col, -1), kv_spec(k_col, 0), kv_spec(k_col, 1),
                  kv_spec(v_col, -1), kv_spec(v_col, 0), kv_spec(v_col, 1),
                  pl.BlockSpec((1, C_KV_HEADS, C_GROUP * C_BLOCK, 3 * C_BLOCK), edge_case)],
        out_specs=pl.BlockSpec((1, C_BLOCK, dq), lambda n, b: (b, n, 0)),
        out_shape=jax.ShapeDtypeStruct((B, L, dq), bf16),
        compiler_params=_cparams("parallel", "parallel"),
        name="swa",
    )(sinks, qkv, qkv, qkv, qkv, qkv, qkv, qkv, _swa_bias_tables())


def _hyena_filters(L, w1, b1, w2, b2, w3, b3, w4, freq):
    hp = lax.Precision.HIGHEST
    t = jnp.linspace(0.0, 1.0, L, dtype=f32)[:, None]
    wpos = (2.0 * math.pi / L) * jnp.arange(L, dtype=f32)[:, None]
    bands = jnp.linspace(1e-4, HY_BANDS - 1, HY_BANDS, dtype=f32)[None, :]
    feats = jnp.concatenate([t, jnp.cos(bands * wpos), -jnp.sin(bands * wpos)], axis=-1)
    h = jnp.sin(freq[0] * (jnp.dot(feats, w1, precision=hp) + b1))
    h = jnp.sin(freq[1] * (jnp.dot(h, w2, precision=hp) + b2))
    h = jnp.sin(freq[2] * (jnp.dot(h, w3, precision=hp) + b3))
    h = jnp.dot(h, w4, precision=hp).reshape(L, HY_ORDER, 2, B_WIDTH)
    deltas = jnp.abs(jnp.linspace(HY_MIN_DECAY, HY_MAX_DECAY, B_WIDTH, dtype=f32))
    window = jnp.exp(-t * deltas[None, :])
    return h * window[:, None, None, :]


def _dft_tables(L):
    n_fft = 2 * L
    idx = jnp.arange(L, dtype=jnp.int32)
    ph = (idx[:, None] * idx[None, :]) % n_fft
    ang = ph.astype(f32) * (2.0 * math.pi / n_fft)
    return jnp.cos(ang).astype(bf16), jnp.sin(ang).astype(bf16)


def _pad_rows(w, rows=8):
    return jnp.pad(w, ((0, rows - w.shape[0]), (0, 0)))


def _even_in_weights(w_in):
    hk = A_HEADS * A_DK
    q = w_in[:, 0:hk].reshape(D_MODEL, A_HEADS, A_DK)
    k = w_in[:, hk:2 * hk].reshape(D_MODEL, A_HEADS, A_DK)
    v = w_in[:, 2 * hk:A_QKV].reshape(D_MODEL, A_HEADS, A_DV)
    z = w_in[:, A_QKV:A_QKV + A_WIDTH].reshape(D_MODEL, A_HEADS, A_DV)
    qkvz = jnp.concatenate([q, k, v, z], axis=-1).reshape(D_MODEL, A_HEADS * 4 * A_DK)
    ab = w_in[:, A_QKV + A_WIDTH:A_QKV + A_WIDTH + 4 * A_HEADS].reshape(D_MODEL, 4, A_HEADS)
    ab = jnp.transpose(ab, (0, 2, 1))
    ab = jnp.pad(ab, ((0, 0), (0, 0), (0, LANE - 4))).reshape(D_MODEL, A_HEADS * LANE)
    hx = w_in[:, A_QKV + A_WIDTH + 4 * A_HEADS:]
    return jnp.concatenate([qkvz, ab, hx], axis=1).astype(bf16)


def _router_weights(w_group, w_expert):
    w = jnp.concatenate([w_group, w_expert], axis=1)
    w = jnp.pad(w, ((0, 0), (0, ROUTER_LANES - w.shape[1])))
    hi = w.astype(bf16)
    lo = (w - hi.astype(f32)).astype(bf16)
    return jnp.concatenate([hi, lo], axis=1)


def kernel(x_prompt, x_sample, norm_mix, norm_ffn, norm_final, ev_w_in, ev_conv_a, ev_alog_f, ev_alog_b, ev_dtb_f, ev_dtb_b, ev_onorm, ev_conv_b, hy_w1, hy_b1, hy_w2, hy_b2, hy_w3, hy_b3, hy_w4, hy_freq, hy_dbias, ev_w_out, od_w_qkv, od_sinks, od_w_out, moe_w_group, moe_w_expert, moe_w_gate, moe_w_up, moe_w_down):
    bp, L, _ = x_prompt.shape
    bs = x_sample.shape[0]
    B = bp + bs
    T = B * L
    xs = (x_prompt.reshape(bp * L, D_MODEL), x_sample.reshape(bs * L, D_MODEL))
    n_layers = moe_w_gate.shape[0]
    wg = moe_w_gate.reshape(n_layers * N_EXPERTS, D_MODEL, D_EXPERT)
    wu = moe_w_up.reshape(n_layers * N_EXPERTS, D_MODEL, D_EXPERT)
    wd = moe_w_down.reshape(n_layers * N_EXPERTS, D_EXPERT, D_MODEL)

    w_in = _even_in_weights(ev_w_in[0])
    n_qkvz = A_HEADS * 4 * A_DK
    n_ab = A_HEADS * LANE
    n_hx = 3 * B_WIDTH
    qkvz, abh, hx = _norm_matmul(xs, norm_mix[0], w_in, (n_qkvz, n_ab, n_hx), (bf16, f32, bf16))

    hk = A_HEADS * A_DK
    ca = ev_conv_a[0]
    conv_a = jnp.concatenate([ca[0:hk].reshape(A_HEADS, A_DK, 3), ca[hk:2 * hk].reshape(A_HEADS, A_DK, 3),
                              ca[2 * hk:].reshape(A_HEADS, A_DV, 3)], axis=1)
    conv_a = jnp.pad(jnp.transpose(conv_a, (0, 2, 1)), ((0, 0), (0, 5), (0, 0)))
    gate_sc = jnp.stack([ev_alog_f[0], ev_alog_b[0], ev_dtb_f[0], ev_dtb_b[0]], axis=1)
    o_a = _deltanet(qkvz.reshape(B, L, n_qkvz), abh.reshape(B, L, n_ab), conv_a, gate_sc, ev_onorm[0])

    filt = _hyena_filters(L, hy_w1[0], hy_b1[0], hy_w2[0], hy_b2[0], hy_w3[0], hy_b3[0], hy_w4[0], hy_freq[0])
    hsum = (filt[:, :, 0] + filt[:, :, 1]).reshape(L, HY_ORDER * B_WIDTH)
    hdiff = (filt[:, :, 0] - filt[:, :, 1]).reshape(L, HY_ORDER * B_WIDTH)
    cmat, smat = _dft_tables(L)
    kr, ki, kn = _filter_spectrum(cmat, smat, hsum, hdiff)
    conv_b = _pad_rows(ev_conv_b[0].T)
    o_b = _hyena_fft(hx.reshape(B, L, n_hx), conv_b, _hyena_tables(L, kr, ki, kn, B_WIDTH // HY_CT),
                     _pad_rows(hy_dbias[0]))

    w_out = ev_w_out[0].astype(bf16)
    x, hpa, hpb, route, route_t, counts = _proj_route(
        xs, (o_a.reshape(T, A_WIDTH), o_b.reshape(T, B_WIDTH)), (w_out[:A_WIDTH], w_out[A_WIDTH:]), norm_ffn[0],
        _router_weights(moe_w_group[0], moe_w_expert[0]))
    x = _moe(0, x, hpa, hpb, route, route_t, counts, wg, wu, wd, norm_final, final_norm=False)

    n_qkv = od_w_qkv.shape[-1]
    dq = C_HEADS * C_HEAD_DIM
    w_qkv = jnp.concatenate([od_w_qkv[0][:, :dq] * (LOG2E * C_HEAD_DIM ** -0.5), od_w_qkv[0][:, dq:]], axis=1)
    (qkv,) = _norm_matmul((x,), norm_mix[1], w_qkv.astype(bf16), (n_qkv,), (bf16,))
    o_c = _swa(qkv.reshape(B, L, n_qkv), od_sinks[0])
    x, hpa, hpb, route, route_t, counts = _proj_route(
        (x,), (o_c.reshape(T, C_HEADS * C_HEAD_DIM),), (od_w_out[0].astype(bf16),), norm_ffn[1],
        _router_weights(moe_w_group[1], moe_w_expert[1]))
    y_prompt, y_sample = _moe(1, x, hpa, hpb, route, route_t, counts, wg, wu, wd, norm_final, final_norm=True,
                              row_ranges=((0, bp * L), (bp * L, bs * L)))
    return (y_prompt.reshape(bp, L, D_MODEL), y_sample.reshape(bs, L, D_MODEL))
```

```python
import functools
import math

import jax
import jax.numpy as jnp
import numpy as np
from jax import lax
from jax.experimental import pallas as pl
from jax.experimental.pallas import tpu as pltpu
from jax.experimental.pallas import tpu_sc as plsc

f32 = jnp.float32
bf16 = jnp.bfloat16

EPS = 1e-6
D_MODEL = 1024

A_HEADS = 4
A_DK = 128
A_DV = 128
A_WIDTH = A_HEADS * A_DV
A_QKV = 2 * A_HEADS * A_DK + A_HEADS * A_DV
DELTA_CHUNK = 128
INV_BASE = 8
DELTA_GROUP = 8

B_WIDTH = D_MODEL - A_WIDTH
HY_ORDER = 2
HY_EMB = 33
HY_BANDS = (HY_EMB - 1) // 2
HY_TARGET = 1e-2
HY_MIN_DECAY = math.log(HY_TARGET) / 1.5
HY_MAX_DECAY = math.log(HY_TARGET) / 0.3
HY_CT = 256
HY_ROWS = 256

C_HEADS = 16
C_KV_HEADS = 4
C_HEAD_DIM = 64
C_GROUP = C_HEADS // C_KV_HEADS
WINDOW = 128
C_BLOCK = 128

N_GROUPS = 4
EXPERTS_PER_GROUP = 8
N_EXPERTS = N_GROUPS * EXPERTS_PER_GROUP
D_EXPERT = 256
ROUTER_LANES = 128

LANE = 128
VMEM_LIMIT = 56 * 1024 * 1024


def _cparams(*sem):
    return pltpu.CompilerParams(dimension_semantics=sem, vmem_limit_bytes=VMEM_LIMIT)


def _dot(a, b):
    return jnp.dot(a, b, preferred_element_type=f32)


def _dot_nt(a, b):
    return lax.dot_general(a, b, (((1,), (1,)), ((), ())), preferred_element_type=f32)


def _dot_tn(a, b):
    return lax.dot_general(a, b, (((0,), (0,)), ((), ())), preferred_element_type=f32)


def _sigmoid(x):
    return 1.0 / (1.0 + jnp.exp(-x))


def _rms(x, g):
    return x * lax.rsqrt(jnp.mean(x * x, axis=-1, keepdims=True) + EPS) * g


def _row_sources(xs, tm):
    tiles = [x.shape[0] // tm for x in xs]
    starts = [sum(tiles[:k]) for k in range(len(xs))]

    def spec(x, lo, n):
        return pl.BlockSpec((tm, x.shape[1]), lambda i: (jnp.clip(i - lo, 0, n - 1), 0))

    return [spec(x, lo, n) for x, lo, n in zip(xs, starts, tiles)], tuple(starts[1:])


def _select_rows(refs, starts):
    i = pl.program_id(0)
    x = refs[0][...]
    for ref, lo in zip(refs[1:], starts):
        x = jnp.where(i >= lo, ref[...], x)
    return x


def _norm_matmul_kernel(*refs, starts):
    n_src = len(starts) + 1
    g_ref, w_ref = refs[n_src:n_src + 2]
    o_refs = refs[n_src + 2:]
    h = _rms(_select_rows(refs[:n_src], starts), g_ref[...]).astype(bf16)
    off = 0
    for o_ref in o_refs:
        n = o_ref.shape[-1]
        o_ref[...] = _dot(h, w_ref[:, off:off + n]).astype(o_ref.dtype)
        off += n


def _norm_matmul(xs, gain, w, splits, dtypes, tm=512):
    T = sum(x.shape[0] for x in xs)
    D = xs[0].shape[1]
    n_all = w.shape[1]
    x_specs, starts = _row_sources(xs, tm)
    return pl.pallas_call(
        functools.partial(_norm_matmul_kernel, starts=starts),
        grid=(T // tm,),
        in_specs=x_specs + [pl.BlockSpec((1, D), lambda i: (0, 0)),
                            pl.BlockSpec((D, n_all), lambda i: (0, 0))],
        out_specs=[pl.BlockSpec((tm, n), lambda i: (i, 0)) for n in splits],
        out_shape=[jax.ShapeDtypeStruct((T, n), dt) for n, dt in zip(splits, dtypes)],
        compiler_params=_cparams("parallel"),
        name="norm_matmul",
    )(*xs, gain.reshape(1, D), w)


MK_LOWER, MK_UPPER, MK_BASE, MK_LEVEL0 = 0, 1, 2, 3
N_LEVELS = int(math.log2(DELTA_CHUNK // INV_BASE))
MK_EYE = MK_LEVEL0 + N_LEVELS
N_MASKS = MK_EYE + 1


def _delta_masks():
    c = np.arange(DELTA_CHUNK)[:, None]
    s = np.arange(DELTA_CHUNK)[None, :]
    m = [s <= c, s >= c, (c // INV_BASE == s // INV_BASE) & (c != s)]
    b = INV_BASE
    while b < DELTA_CHUNK:
        m.append((c // (2 * b) == s // (2 * b)) & (c // b != s // b))
        b *= 2
    m.append(c == s)
    m = np.stack(m).astype(np.float32)
    return jnp.asarray(m), jnp.asarray(m, dtype=bf16)


def _tri_inverse(mats, mk_ref, mk16_ref):
    ps = [a * mk_ref[MK_BASE] for a in mats]
    ts = [mk_ref[MK_EYE] - p for p in ps]
    n = 2
    while n < INV_BASE:
        p16s = [p.astype(bf16) for p in ps]
        ps = [_dot(p16, p16) for p16 in p16s]
        ts = [t + _dot(t.astype(bf16), p.astype(bf16)) for t, p in zip(ts, ps)]
        n *= 2
    a16s = [a.astype(bf16) for a in mats]
    t16s = [t.astype(bf16) for t in ts]
    for lvl in range(N_LEVELS):
        xs = [_dot(a16 * mk16_ref[MK_LEVEL0 + lvl], t16) for a16, t16 in zip(a16s, t16s)]
        t16s = [t16 - _dot(t16, x.astype(bf16)).astype(bf16) for t16, x in zip(t16s, xs)]
    return t16s


def _deltanet_kernel(sc_ref, x_ref, ab_ref, cw_ref, on_ref, mk_ref, mk16_ref, o_ref, mp_s, nr_s, gl_s, o_s):
    hd = pl.program_id(1)
    L = x_ref.shape[1]
    C = DELTA_CHUNK
    nchunk = L // C
    group = math.gcd(DELTA_GROUP, nchunk)
    nqkv = 3 * A_DK
    halo = 16

    lane = lax.broadcasted_iota(jnp.int32, (1, LANE), 1)
    alog = jnp.where(lane == 2, sc_ref[hd, 0], jnp.where(lane == 3, sc_ref[hd, 1], 0.0))
    dtb = jnp.where(lane == 2, sc_ref[hd, 2], jnp.where(lane == 3, sc_ref[hd, 3], 0.0))
    neg_a = jnp.where((lane == 2) | (lane == 3), -jnp.exp(alog), 0.0)
    rowi = lax.broadcasted_iota(jnp.int32, (C, 1), 0)
    cw = cw_ref[0]

    def chunk_inputs(i):
        r0 = pl.multiple_of(i * C, C)
        xc = x_ref[0, pl.ds(r0, C), 0:nqkv].astype(f32)
        rp = pl.multiple_of(jnp.maximum(r0 - halo, 0), halo)
        rn = pl.multiple_of(jnp.minimum(r0 + C, L - halo), halo)
        prow = x_ref[0, pl.ds(rp, halo), 0:nqkv][halo - 1:halo, :].astype(f32)
        nrow = x_ref[0, pl.ds(rn, halo), 0:nqkv][0:1, :].astype(f32)
        prow = jnp.where(i > 0, prow, 0.0)
        nrow = jnp.where(i < nchunk - 1, nrow, 0.0)
        prev = jnp.where(rowi == 0, prow, pltpu.roll(xc, 1, 0))
        nxt = jnp.where(rowi == C - 1, nrow, pltpu.roll(xc, C - 1, 0))
        y = prev * cw[0:1, :] + xc * cw[1:2, :] + nxt * cw[2:3, :]
        y = y * _sigmoid(y)
        q = y[:, 0:A_DK]
        k = y[:, A_DK:2 * A_DK]
        vc = y[:, 2 * A_DK:nqkv]
        qc = q * lax.rsqrt(jnp.sum(q * q, axis=-1, keepdims=True) + EPS) * (A_DK ** -0.5)
        kc = k * lax.rsqrt(jnp.sum(k * k, axis=-1, keepdims=True) + EPS)

        abc = ab_ref[0, pl.ds(r0, C), :]
        zg = abc + dtb
        g = neg_a * (jnp.maximum(zg, 0.0) + jnp.log(1.0 + jnp.exp(-jnp.abs(zg))))
        low16 = mk16_ref[MK_LOWER]
        g1 = g.astype(bf16)
        e1 = g - g1.astype(f32)
        g2 = e1.astype(bf16)
        g3 = (e1 - g2.astype(f32)).astype(bf16)
        pre = _dot(low16, g1) + _dot(low16, g2) + _dot(low16, g3)
        tot = pre[C - 1:C, :]
        suf = tot - pre + g
        beta_all = _sigmoid(abc)

        k16 = kc.astype(bf16)
        kk = _dot_nt(k16, k16)
        qk = _dot_nt(qc.astype(bf16), k16)

        chains = []
        for rev in (0, 1):
            beta = beta_all[:, rev:rev + 1]
            gcc = (suf if rev else pre)[:, 2 + rev:3 + rev]
            tot11 = tot[:, 2 + rev:3 + rev]
            keep = mk_ref[MK_UPPER if rev else MK_LOWER]
            bc = jnp.broadcast_to(gcc * LOG2E, (C, C))
            decay = jnp.exp2((bc - bc.T) * keep) * keep
            a = (beta * kk) * decay
            eg = jnp.exp(gcc)
            chains.append(dict(
                i=i, rev=rev, a=a, gl=jnp.exp(tot11), qd=qc * eg,
                rhs=jnp.concatenate([vc * beta, kc * (beta * eg)], axis=1).astype(bf16),
                kd16=(kc * jnp.exp(tot11 - gcc)).astype(bf16),
                qk16=(qk * decay).astype(bf16)))
        return chains

    def prep(it, carry):
        chains = []
        for gi in range(group):
            chains += chunk_inputs(it * group + gi)
        t16s = _tri_inverse([c["a"] for c in chains], mk_ref, mk16_ref)
        uws = [_dot(t16, c["rhs"]).astype(bf16) for t16, c in zip(t16s, chains)]
        kts = [_dot_tn(c["kd16"], uw) for c, uw in zip(chains, uws)]
        prs = [_dot(c["qk16"], uw) for c, uw in zip(chains, uws)]
        for c, kt, pr in zip(chains, kts, prs):
            rev, i = c["rev"], c["i"]
            mp_s[rev, i, 0:A_DK, :] = (-kt[:, A_DV:]).astype(bf16)
            mp_s[rev, i, A_DK:A_DK + C, :] = (c["qd"] - pr[:, A_DV:]).astype(bf16)
            nr_s[rev, i, 0:A_DK, :] = kt[:, 0:A_DV]
            nr_s[rev, i, A_DK:A_DK + C, :] = pr[:, 0:A_DV]
            gl_s[rev, i] = jnp.broadcast_to(c["gl"], (8, LANE))
        return carry

    lax.fori_loop(0, nchunk // group, prep, 0)

    def scan(i, carry):
        out = []
        for rev, S in enumerate(carry):
            j = (nchunk - 1 - i) if rev else i
            y = _dot(mp_s[rev, j], S.astype(bf16))
            nr = nr_s[rev, j]
            o_s[rev, pl.ds(pl.multiple_of(j * C, C), C), :] = y[A_DK:, :] + nr[A_DK:, :]
            out.append(S * gl_s[rev, j][0:1, :] + y[0:A_DK, :] + nr[0:A_DK, :])
        return tuple(out)

    S0 = jnp.zeros((A_DK, A_DV), f32)
    lax.fori_loop(0, nchunk, scan, (S0, S0))

    def finish(it, carry):
        for gi in range(group):
            r0 = pl.multiple_of((it * group + gi) * C, C)
            o = o_s[0, pl.ds(r0, C), :] + o_s[1, pl.ds(r0, C), :]
            zg = x_ref[0, pl.ds(r0, C), nqkv:nqkv + A_DV].astype(f32)
            o_ref[0, pl.ds(r0, C), :] = (_rms(o, on_ref[...]) * (zg * _sigmoid(zg))).astype(o_ref.dtype)
        return carry

    lax.fori_loop(0, nchunk // group, finish, 0)


def _deltanet(qkvz, abh, conv_w, gate_sc, onorm):
    B, L, _ = qkvz.shape
    C = DELTA_CHUNK
    nchunk = L // C
    hw = 4 * A_DK
    scratch = [pltpu.VMEM((2, nchunk, A_DK + C, A_DV), bf16),
               pltpu.VMEM((2, nchunk, A_DK + C, A_DV), f32),
               pltpu.VMEM((2, nchunk, 8, LANE), f32),
               pltpu.VMEM((2, L, A_DV), f32)]
    return pl.pallas_call(
        _deltanet_kernel,
        grid=(B, A_HEADS),
        in_specs=[pl.BlockSpec(memory_space=pltpu.SMEM),
                  pl.BlockSpec((1, L, hw), lambda b, h: (b, 0, h)),
                  pl.BlockSpec((1, L, LANE), lambda b, h: (b, 0, h)),
                  pl.BlockSpec((1, 8, 3 * A_DK), lambda b, h: (h, 0, 0)),
                  pl.BlockSpec((1, A_DV), lambda b, h: (0, 0)),
                  pl.BlockSpec((N_MASKS, C, C), lambda b, h: (0, 0, 0)),
                  pl.BlockSpec((N_MASKS, C, C), lambda b, h: (0, 0, 0))],
        out_specs=pl.BlockSpec((1, L, A_DV), lambda b, h: (b, 0, h)),
        out_shape=jax.ShapeDtypeStruct((B, L, A_WIDTH), bf16),
        scratch_shapes=scratch,
        compiler_params=_cparams("parallel", "parallel"),
        name="deltanet",
    )(gate_sc, qkvz, abh, conv_w, onorm.reshape(1, A_DV), *_delta_masks())


def _filter_spectrum_kernel(c_ref, s_ref, hs_ref, hd_ref, kr_ref, ki_ref):
    L = c_ref.shape[1]
    rows = kr_ref.shape[1]
    scale = 1.0 / (2 * L)

    def split(x):
        hi = x.astype(bf16)
        return hi, (x - hi.astype(f32)).astype(bf16)

    hs_hi, hs_lo = split(hs_ref[...])
    hd_hi, hd_lo = split(hd_ref[...])
    for k1 in range(kr_ref.shape[0]):
        c = c_ref[k1 * rows:(k1 + 1) * rows, :]
        s = s_ref[k1 * rows:(k1 + 1) * rows, :]
        kr_ref[k1] = (_dot(c, hs_hi) + _dot(c, hs_lo)) * scale
        ki_ref[k1] = (_dot(s, hd_hi) + _dot(s, hd_lo)) * (-scale)


def _filter_spectrum(hsum, hdiff):
    L, n = hsum.shape
    ct = HY_CT
    n_fft = 2 * L
    n2 = L // FFT_Z1
    d = jnp.arange(L, dtype=jnp.int32)[None, :]
    a1 = ((jnp.arange(FFT_K1, dtype=jnp.int32)[:, None] * d) % n_fft).astype(f32) * (2.0 * math.pi / n_fft)
    a2 = ((jnp.arange(n2, dtype=jnp.int32)[:, None] * d) % n2).astype(f32) * (2.0 * math.pi / n2)
    c1, s1 = jnp.cos(a1)[:, None, :], jnp.sin(a1)[:, None, :]
    c2, s2 = jnp.cos(a2)[None, :, :], jnp.sin(a2)[None, :, :]
    cmat = (c1 * c2 - s1 * s2).reshape(FFT_K1 * n2, L).astype(bf16)
    smat = (s1 * c2 + c1 * s2).reshape(FFT_K1 * n2, L).astype(bf16)
    once = pl.Buffered(1)
    return pl.pallas_call(
        _filter_spectrum_kernel,
        grid=(n // ct,),
        in_specs=[pl.BlockSpec((FFT_K1 * n2, L), lambda j: (0, 0), pipeline_mode=once),
                  pl.BlockSpec((FFT_K1 * n2, L), lambda j: (0, 0), pipeline_mode=once),
                  pl.BlockSpec((L, ct), lambda j: (0, j)),
                  pl.BlockSpec((L, ct), lambda j: (0, j))],
        out_specs=[pl.BlockSpec((FFT_K1, n2, ct), lambda j: (0, 0, j)),
                   pl.BlockSpec((FFT_K1, n2, ct), lambda j: (0, 0, j))],
        out_shape=[jax.ShapeDtypeStruct((FFT_K1, n2, n), f32),
                   jax.ShapeDtypeStruct((FFT_K1, n2, n), f32)],
        compiler_params=_cparams("parallel"),
        name="filter_spectrum",
    )(cmat, smat, hsum, hdiff)


FFT_N1 = 16
FFT_K1 = FFT_N1 // 2 + 1
FFT_Z1 = FFT_N1 // 2
FFT_ROWS = 32


def _lincomb(terms):
    groups = {}
    for c, a in terms:
        if abs(c) > 1e-9:
            groups.setdefault(round(abs(c), 9), []).append((c > 0, a))
    total = None
    for mag, items in groups.items():
        pos = [a for p, a in items if p]
        neg = [a for p, a in items if not p]
        s = None
        for a in pos:
            s = a if s is None else s + a
        for a in neg:
            s = -a if s is None else s - a
        if mag != 1.0:
            s = s * mag
        total = s if total is None else total + s
    return total


def _hyena_fft_kernel(m2f_ref, m2i_ref, twr_ref, twi_ref, x1_ref, x2_ref, v_ref, w1_ref, w2_ref, wv_ref,
                      kr0_ref, ki0_ref, kr1_ref, ki1_ref, db_ref, o_ref, z_s, g_s, d_s, y_s, b_s):
    L, ct = z_s.shape
    n2 = L // FFT_Z1
    R = min(HY_ROWS, L)
    nt = L // R
    r1 = min(FFT_ROWS, n2)
    halo = 16
    rowi = lax.broadcasted_iota(jnp.int32, (R, 1), 0)
    ang = 2.0 * math.pi / FFT_N1
    c16 = [[math.cos(ang * a * b) for b in range(FFT_K1)] for a in range(FFT_Z1)]
    s16 = [[math.sin(ang * a * b) for b in range(FFT_K1)] for a in range(FFT_Z1)]

    def conv_into(dst, x_ref, w_ref):
        def body(t, carry):
            r0 = pl.multiple_of(t * R, R)
            x = x_ref[0, pl.ds(r0, R), :].astype(f32)
            rp = pl.multiple_of(jnp.maximum(r0 - halo, 0), halo)
            rn = pl.multiple_of(jnp.minimum(r0 + R, L - halo), halo)
            prow = jnp.where(t > 0, x_ref[0, pl.ds(rp, halo), :][halo - 1:halo, :].astype(f32), 0.0)
            nrow = jnp.where(t < nt - 1, x_ref[0, pl.ds(rn, halo), :][0:1, :].astype(f32), 0.0)
            prev = jnp.where(rowi == 0, prow, pltpu.roll(x, 1, 0))
            nxt = jnp.where(rowi == R - 1, nrow, pltpu.roll(x, R - 1, 0))
            w = w_ref[...]
            dst[pl.ds(r0, R), :] = prev * w[0:1, :] + x * w[1:2, :] + nxt * w[2:3, :]
            return carry
        lax.fori_loop(0, nt, body, 0)

    def cols(k1):
        return slice(k1 * ct, (k1 + 1) * ct)

    conv_into(z_s, v_ref, wv_ref)
    stages = ((x1_ref, w1_ref, kr0_ref, ki0_ref), (x2_ref, w2_ref, kr1_ref, ki1_ref))
    for o, (g_ref, gw_ref, kr_ref, ki_ref) in enumerate(stages):
        conv_into(g_s, g_ref, gw_ref)

        def forward1(rt, carry):
            r0 = pl.multiple_of(rt * r1, r1)
            zs = [z_s[pl.ds(a * n2 + r0, r1), :] for a in range(FFT_Z1)]
            for k1 in range(FFT_K1):
                ar = _lincomb([(c16[a][k1], zs[a]) for a in range(FFT_Z1)])
                ai = _lincomb([(-s16[a][k1], zs[a]) for a in range(FFT_Z1)])
                twr = twr_ref[pl.ds(r0, r1), cols(k1)]
                twi = twi_ref[pl.ds(r0, r1), cols(k1)]
                if ai is None:
                    pr, pi = ar * twr, -(ar * twi)
                else:
                    pr, pi = ar * twr + ai * twi, ai * twr - ar * twi
                d_s[pl.ds(r0, r1), cols(k1)] = pr.astype(bf16)
                d_s[pl.ds(n2 + r0, r1), cols(k1)] = pi.astype(bf16)
            return carry

        lax.fori_loop(0, n2 // r1, forward1, 0)

        for k1 in range(FFT_K1):
            xk = _dot(m2f_ref[...], d_s[:, cols(k1)])
            xr, xi = xk[0:n2, :], xk[n2:, :]
            kr, ki = kr_ref[k1], ki_ref[k1]
            y_s[0:n2, cols(k1)] = (xr * kr - xi * ki).astype(bf16)
            y_s[n2:2 * n2, cols(k1)] = (xr * ki + xi * kr).astype(bf16)

        for k1 in range(FFT_K1):
            bk = _dot(m2i_ref[...], y_s[:, cols(k1)])
            br, bi = bk[0:n2, :], bk[n2:, :]
            twr, twi = twr_ref[:, cols(k1)], twi_ref[:, cols(k1)]
            b_s[0:n2, cols(k1)] = br * twr - bi * twi
            b_s[n2:2 * n2, cols(k1)] = br * twi + bi * twr

        def inverse1(rt, carry):
            r0 = pl.multiple_of(rt * r1, r1)
            brs = [b_s[pl.ds(r0, r1), cols(j)] for j in range(FFT_K1)]
            bis = [b_s[pl.ds(n2 + r0, r1), cols(j)] for j in range(FFT_K1)]
            for a in range(FFT_Z1):
                terms = [(1.0, brs[0]), (-1.0 if a % 2 else 1.0, brs[FFT_K1 - 1])]
                for j in range(1, FFT_K1 - 1):
                    terms += [(2.0 * c16[a][j], brs[j]), (-2.0 * s16[a][j], bis[j])]
                rows = pl.ds(a * n2 + r0, r1)
                z = g_s[rows, :] * (_lincomb(terms) + z_s[rows, :] * db_ref[o:o + 1, :])
                if o + 1 < len(stages):
                    z_s[rows, :] = z
                else:
                    o_ref[0, rows, :] = z.astype(o_ref.dtype)
            return carry

        lax.fori_loop(0, n2 // r1, inverse1, 0)


def _hyena_tables(L):
    n_fft = 2 * L
    n2 = L // FFT_Z1
    ct = HY_CT
    idx = jnp.arange(n2, dtype=jnp.int32)
    a2 = ((idx[:, None] * idx[None, :]) % n2).astype(f32) * (2.0 * math.pi / n2)
    c2, s2 = jnp.cos(a2), jnp.sin(a2)
    m2f = jnp.block([[c2, s2], [-s2, c2]]).astype(bf16)
    m2i = jnp.block([[c2, -s2], [s2, c2]]).astype(bf16)
    at = (idx[:, None] * jnp.arange(FFT_K1, dtype=jnp.int32)[None, :]).astype(f32) * (2.0 * math.pi / n_fft)
    twr = jnp.broadcast_to(jnp.cos(at)[:, :, None], (n2, FFT_K1, ct)).reshape(n2, FFT_K1 * ct)
    twi = jnp.broadcast_to(jnp.sin(at)[:, :, None], (n2, FFT_K1, ct)).reshape(n2, FFT_K1 * ct)
    return m2f, m2i, twr, twi


def _hyena_fft(hx, conv_w, tables, kr, ki, dbias):
    B, L, _ = hx.shape
    ct = HY_CT
    nct = B_WIDTH // ct
    n2 = L // FFT_Z1
    wide = FFT_K1 * ct
    once = pl.Buffered(1)
    m2f, m2i, twr, twi = tables

    def xspec(part):
        return pl.BlockSpec((1, L, ct), lambda c, b: (b, 0, part * nct + c))

    def wspec(part):
        return pl.BlockSpec((8, ct), lambda c, b: (0, part * nct + c))

    def kspec(order):
        return pl.BlockSpec((FFT_K1, n2, ct), lambda c, b: (0, 0, order * nct + c), pipeline_mode=once)

    def const(shape):
        return pl.BlockSpec(shape, lambda c, b: (0, 0), pipeline_mode=once)

    return pl.pallas_call(
        _hyena_fft_kernel,
        grid=(nct, B),
        in_specs=[const((2 * n2, 2 * n2)), const((2 * n2, 2 * n2)), const((n2, wide)), const((n2, wide)),
                  xspec(0), xspec(1), xspec(2), wspec(0), wspec(1), wspec(2),
                  kspec(0), kspec(0), kspec(1), kspec(1),
                  pl.BlockSpec((8, ct), lambda c, b: (0, c))],
        out_specs=pl.BlockSpec((1, L, ct), lambda c, b: (b, 0, c)),
        out_shape=jax.ShapeDtypeStruct((B, L, B_WIDTH), bf16),
        scratch_shapes=[pltpu.VMEM((L, ct), f32), pltpu.VMEM((L, ct), f32),
                        pltpu.VMEM((2 * n2, wide), bf16), pltpu.VMEM((2 * n2, wide), bf16),
                        pltpu.VMEM((2 * n2, wide), f32)],
        compiler_params=_cparams("parallel", "parallel"),
        name="hyena_fft",
    )(m2f, m2i, twr, twi, hx, hx, hx, conv_w, conv_w, conv_w, kr, ki, kr, ki, dbias)


RT_E1, RT_E2, RT_W1, RT_W2, RT_R1, RT_R2 = 0, 1, 2, 3, 4, 5
ROUTE_TM = 512
EXPERT_TM = 512
SC_WINDOW = 128


def _pack_bf16_pairs(x):
    n = x.shape[1] // 2
    bits = lax.bitcast_convert_type(x.astype(bf16).astype(f32), jnp.uint32)
    return bits[:, :n] | (bits[:, n:] >> 16)


def _unpack_bf16_pairs(p):
    hi = lax.bitcast_convert_type(p & jnp.uint32(0xFFFF0000), f32)
    lo = lax.bitcast_convert_type(p << 16, f32)
    return hi, lo


def _route(logits, low_ref):
    lane = lax.broadcasted_iota(jnp.int32, logits.shape, 1)
    neg = -jnp.inf
    big = ROUTER_LANES
    gl = jnp.where(lane < N_GROUPS, logits, neg)
    gmax = jnp.max(gl, axis=-1, keepdims=True)
    gsel = jnp.min(jnp.where(gl == gmax, lane, big), axis=-1, keepdims=True)
    psel = 1.0 / jnp.sum(jnp.exp(gl - gmax), axis=-1, keepdims=True)
    first = N_GROUPS + gsel * EXPERTS_PER_GROUP
    el = jnp.where((lane >= first) & (lane < first + EXPERTS_PER_GROUP), logits, neg)
    t1 = jnp.max(el, axis=-1, keepdims=True)
    i1 = jnp.min(jnp.where(el == t1, lane, big), axis=-1, keepdims=True)
    el2 = jnp.where(lane == i1, neg, el)
    t2 = jnp.max(el2, axis=-1, keepdims=True)
    i2 = jnp.min(jnp.where(el2 == t2, lane, big), axis=-1, keepdims=True)
    ex = jnp.exp(t2 - t1)
    w1 = psel / (1.0 + ex)
    w2 = ex * w1
    e1 = i1 - N_GROUPS
    e2 = i2 - N_GROUPS
    onehot = jnp.where((lane == e1) | (lane == e2), 1.0, 0.0)
    before = _dot(low_ref[...], onehot.astype(bf16))
    r1 = jnp.sum(jnp.where(lane == e1, before, 0.0), axis=-1, keepdims=True)
    r2 = jnp.sum(jnp.where(lane == e2, before, 0.0), axis=-1, keepdims=True)
    tm = logits.shape[0]
    counts = before[tm - 1:tm, :] + onehot[tm - 1:tm, :]
    rec = jnp.zeros(logits.shape, f32)
    for ln, val in ((RT_E1, e1.astype(f32)), (RT_E2, e2.astype(f32)), (RT_W1, w1), (RT_W2, w2), (RT_R1, r1), (RT_R2, r2)):
        rec = jnp.where(lane == ln, val, rec)
    return rec, counts


def _proj_route_kernel(*refs, n_in, starts):
    n_src = len(starts) + 1
    x = _select_rows(refs[:n_src], starts)
    refs = refs[n_src:]
    a_refs = refs[0:n_in]
    w_refs = refs[n_in:2 * n_in]
    g_ref, wr_ref, low_ref, xo_ref, hpa_ref, hpb_ref, rt_ref, rtt_ref, cnt_ref = refs[2 * n_in:]
    for a_ref, w_ref in zip(a_refs, w_refs):
        x = x + _dot(a_ref[...], w_ref[...])
    xo_ref[...] = x
    h = _rms(x, g_ref[...])
    hp = _pack_bf16_pairs(h)
    quarter = hp.shape[1] // 2
    hpa_ref[...] = hp[:, :quarter]
    hpb_ref[...] = hp[:, quarter:]
    h_hi = h.astype(bf16)
    h_lo = (h - h_hi.astype(f32)).astype(bf16)
    both = _dot(h_hi, wr_ref[...])
    logits = both[:, :ROUTER_LANES] + both[:, ROUTER_LANES:] + _dot(h_lo, wr_ref[:, :ROUTER_LANES])
    rec, counts = _route(logits, low_ref)
    rt_ref[...] = rec
    rtt_ref[...] = rec.T[0:rtt_ref.shape[0], :]
    cnt_ref[0] = jnp.broadcast_to(counts, cnt_ref.shape[1:])


def _proj_route(xs, acts, ws, gain, w_router):
    T = sum(x.shape[0] for x in xs)
    D = xs[0].shape[1]
    tm = ROUTE_TM
    n_in = len(acts)
    low = jnp.asarray(np.tril(np.ones((tm, tm), np.float32), -1), dtype=bf16)
    in_specs, starts = _row_sources(xs, tm)
    in_specs += [pl.BlockSpec((tm, a.shape[1]), lambda i: (i, 0)) for a in acts]
    in_specs += [pl.BlockSpec(w.shape, lambda i: (0, 0)) for w in ws]
    in_specs += [pl.BlockSpec((1, D), lambda i: (0, 0)),
                 pl.BlockSpec((D, 2 * ROUTER_LANES), lambda i: (0, 0)),
                 pl.BlockSpec((tm, tm), lambda i: (0, 0))]
    return pl.pallas_call(
        functools.partial(_proj_route_kernel, n_in=n_in, starts=starts),
        grid=(T // tm,),
        in_specs=in_specs,
        out_specs=[pl.BlockSpec((tm, D), lambda i: (i, 0)),
                   pl.BlockSpec((tm, D // 4), lambda i: (i, 0)),
                   pl.BlockSpec((tm, D // 4), lambda i: (i, 0)),
                   pl.BlockSpec((tm, ROUTER_LANES), lambda i: (i, 0)),
                   pl.BlockSpec((8, tm), lambda i: (0, i)),
                   pl.BlockSpec((1, 8, ROUTER_LANES), lambda i: (i, 0, 0))],
        out_shape=[jax.ShapeDtypeStruct((T, D), f32),
                   jax.ShapeDtypeStruct((T, D // 4), jnp.uint32),
                   jax.ShapeDtypeStruct((T, D // 4), jnp.uint32),
                   jax.ShapeDtypeStruct((T, ROUTER_LANES), f32),
                   jax.ShapeDtypeStruct((8, T), f32),
                   jax.ShapeDtypeStruct((T // tm, 8, ROUTER_LANES), f32)],
        compiler_params=_cparams("parallel"),
        name="proj_route",
    )(*xs, *acts, *ws, gain.reshape(1, D), w_router, low)


def _moe_rows(T):
    return 2 * T + N_EXPERTS * EXPERT_TM


def _moe_plan(route_t, counts):
    T = route_t.shape[1]
    tm = ROUTE_TM
    cnt = counts[:, 0, :N_EXPERTS].astype(jnp.int32)
    total = jnp.sum(cnt, axis=0)
    padded = ((total + EXPERT_TM - 1) // EXPERT_TM) * EXPERT_TM
    ends = jnp.cumsum(padded)
    tile_base = (ends - padded)[None, :] + jnp.cumsum(cnt, axis=0) - cnt
    base = jnp.broadcast_to(tile_base.T[:, :, None], (N_EXPERTS, T // tm, tm)).reshape(N_EXPERTS, T)
    ids = jnp.arange(N_EXPERTS, dtype=jnp.int32)[:, None]
    dest = []
    for e_row, r_row in ((RT_E1, RT_R1), (RT_E2, RT_R2)):
        e = route_t[e_row].astype(jnp.int32)
        dest.append(jnp.sum(jnp.where(ids == e[None, :], base, 0), axis=0) + route_t[r_row].astype(jnp.int32))
    n_tiles = _moe_rows(T) // EXPERT_TM
    n_used = ends[-1] // EXPERT_TM
    starts = jnp.arange(n_tiles, dtype=jnp.int32) * EXPERT_TM
    tile_expert = jnp.sum((starts[:, None] >= ends[None, :]).astype(jnp.int32), axis=1)
    last_expert = jnp.sum((jnp.maximum(n_used - 1, 0) * EXPERT_TM >= ends).astype(jnp.int32))
    tile_expert = jnp.minimum(tile_expert, last_expert)
    return dest, tile_expert, n_used.reshape(1).astype(jnp.int32)


def _sc_mesh():
    return plsc.VectorSubcoreMesh(core_axis_name="core", subcore_axis_name="subcore")


def _sc_scatter_pair(rows, d0, d1, n_out):
    T, W = rows.shape
    win = SC_WINDOW

    @pl.kernel(out_type=jax.ShapeDtypeStruct((n_out, W), rows.dtype), mesh=_sc_mesh(), name="moe_sc_scatter")
    def scatter(x_hbm, i0_hbm, i1_hbm, o_hbm):
        def body(x_vmem, i0_vmem, i1_vmem):
            pltpu.sync_copy(x_vmem, o_hbm.at[i0_vmem.at[0]])
            pltpu.sync_copy(x_vmem, o_hbm.at[i1_vmem.at[0]])

        pltpu.emit_pipeline(
            body, grid=(T // win,),
            in_specs=[pl.BlockSpec((win, W), index_map=lambda i: (i, 0)),
                      pl.BlockSpec((1, win), index_map=lambda i: (0, i)),
                      pl.BlockSpec((1, win), index_map=lambda i: (0, i))],
            out_specs=[],
            core_axis_name=("core", "subcore"), dimension_semantics=(pltpu.PARALLEL,),
        )(x_hbm, i0_hbm, i1_hbm)

    return scatter(rows, d0.reshape(1, T), d1.reshape(1, T))


def _sc_gather(table, idx):
    n = idx.shape[0]
    W = table.shape[1]
    win = SC_WINDOW

    @pl.kernel(out_type=jax.ShapeDtypeStruct((n, W), table.dtype), mesh=_sc_mesh(), name="moe_sc_gather")
    def gather(t_hbm, i_hbm, o_hbm):
        def body(i_vmem, o_vmem):
            pltpu.sync_copy(t_hbm.at[i_vmem.at[0]], o_vmem)

        pltpu.emit_pipeline(
            body, grid=(n // win,),
            in_specs=[pl.BlockSpec((1, win), index_map=lambda i: (0, i))],
            out_specs=[pl.BlockSpec((win, W), index_map=lambda i: (i, 0))],
            core_axis_name=("core", "subcore"), dimension_semantics=(pltpu.PARALLEL,),
        )(i_hbm, o_hbm)

    return gather(table, idx.reshape(1, n))


def _expert_kernel(te_ref, nu_ref, xa_ref, xb_ref, wg_ref, wu_ref, wd_ref, ya_ref, yb_ref):
    @pl.when(pl.program_id(0) < nu_ref[0])
    def _():
        hi, lo = _unpack_bf16_pairs(jnp.concatenate([xa_ref[...], xb_ref[...]], axis=1))
        x = jnp.concatenate([hi, lo], axis=1).astype(bf16)
        gate = _dot(x, wg_ref[0].astype(bf16))
        up = _dot(x, wu_ref[0].astype(bf16))
        hg = (gate * _sigmoid(gate) * up).astype(bf16)
        y = _pack_bf16_pairs(_dot(hg, wd_ref[0].astype(bf16)))
        quarter = y.shape[1] // 2
        ya_ref[...] = y[:, :quarter]
        yb_ref[...] = y[:, quarter:]

    @pl.when(pl.program_id(0) >= nu_ref[0])
    def _():
        ya_ref[...] = jnp.zeros(ya_ref.shape, ya_ref.dtype)
        yb_ref[...] = jnp.zeros(yb_ref.shape, yb_ref.dtype)


def _experts(tile_expert, n_used, xa, xb, wg, wu, wd):
    n_rows, quarter = xa.shape
    D = 4 * quarter
    tm = EXPERT_TM
    grid_spec = pltpu.PrefetchScalarGridSpec(
        num_scalar_prefetch=2,
        grid=(n_rows // tm,),
        in_specs=[pl.BlockSpec((tm, quarter), lambda i, te, nu: (i, 0)),
                  pl.BlockSpec((tm, quarter), lambda i, te, nu: (i, 0)),
                  pl.BlockSpec((1, D, D_EXPERT), lambda i, te, nu: (te[i], 0, 0)),
                  pl.BlockSpec((1, D, D_EXPERT), lambda i, te, nu: (te[i], 0, 0)),
                  pl.BlockSpec((1, D_EXPERT, D), lambda i, te, nu: (te[i], 0, 0))],
        out_specs=[pl.BlockSpec((tm, quarter), lambda i, te, nu: (i, 0)),
                   pl.BlockSpec((tm, quarter), lambda i, te, nu: (i, 0))])
    return pl.pallas_call(
        _expert_kernel,
        grid_spec=grid_spec,
        out_shape=[jax.ShapeDtypeStruct((n_rows, quarter), jnp.uint32)] * 2,
        compiler_params=_cparams("arbitrary"),
        name="moe_experts",
    )(tile_expert, n_used, xa, xb, wg, wu, wd)


def _combine_kernel(x_ref, rt_ref, gf_ref, a1_ref, b1_ref, a2_ref, b2_ref, o_ref, *, final_norm):
    half = x_ref.shape[1] // 2
    h1, l1 = _unpack_bf16_pairs(jnp.concatenate([a1_ref[...], b1_ref[...]], axis=1))
    h2, l2 = _unpack_bf16_pairs(jnp.concatenate([a2_ref[...], b2_ref[...]], axis=1))
    w1 = rt_ref[:, RT_W1:RT_W1 + 1]
    w2 = rt_ref[:, RT_W2:RT_W2 + 1]
    out = jnp.concatenate([x_ref[:, 0:half] + w1 * h1 + w2 * h2, x_ref[:, half:] + w1 * l1 + w2 * l2], axis=1)
    if final_norm:
        out = _rms(out, gf_ref[...])
    o_ref[...] = out


def _combine(x, route, ga, gb, gain_final, final_norm, row_ranges=None):
    T, D = x.shape
    tm = ROUTE_TM
    n = T // tm
    quarter = D // 4
    outs = []
    for start, size in (row_ranges or ((0, T),)):
        lo = start // tm
        slot1 = pl.BlockSpec((tm, quarter), lambda i, lo=lo: (i + lo, 0))
        slot2 = pl.BlockSpec((tm, quarter), lambda i, lo=lo: (i + lo + n, 0))
        outs.append(pl.pallas_call(
            functools.partial(_combine_kernel, final_norm=final_norm),
            grid=(size // tm,),
            in_specs=[pl.BlockSpec((tm, D), lambda i, lo=lo: (i + lo, 0)),
                      pl.BlockSpec((tm, ROUTER_LANES), lambda i, lo=lo: (i + lo, 0)),
                      pl.BlockSpec((1, D), lambda i: (0, 0)),
                      slot1, slot1, slot2, slot2],
            out_specs=pl.BlockSpec((tm, D), lambda i: (i, 0)),
            out_shape=jax.ShapeDtypeStruct((size, D), f32),
            compiler_params=_cparams("parallel"),
            name="moe_combine",
        )(x, route, gain_final.reshape(1, D), ga, gb, ga, gb))
    return outs if row_ranges else outs[0]


def _moe(layer, x, hpa, hpb, route, route_t, counts, wg, wu, wd, gain_final, final_norm, row_ranges=None):
    (d0, d1), tile_expert, n_used = _moe_plan(route_t, counts)
    n_rows = _moe_rows(x.shape[0])
    xa = _sc_scatter_pair(hpa, d0, d1, n_rows)
    xb = _sc_scatter_pair(hpb, d0, d1, n_rows)
    ya, yb = _experts(tile_expert + layer * N_EXPERTS, n_used, xa, xb, wg, wu, wd)
    dcat = jnp.concatenate([d0, d1])
    return _combine(x, route, _sc_gather(ya, dcat), _sc_gather(yb, dcat), gain_final, final_norm, row_ranges)


LOG2E = 1.4426950408889634
SWA_LOCKSTEP = 2


def _swa_bias_tables():
    blk = C_BLOCK
    qi = np.arange(blk)[:, None]
    kj = np.arange(3 * blk)[None, :]
    dist = np.abs(blk + qi - kj)
    tabs = np.full((3, C_KV_HEADS, C_GROUP * blk, 3 * blk), -np.inf, np.float32)
    for case in range(3):
        ok = dist <= WINDOW
        if case == 0:
            ok = ok & (kj >= blk)
        if case == 2:
            ok = ok & (kj < 2 * blk)
        for hh in range(C_HEADS):
            slope = 2.0 ** (-8.0 * (hh + 1) / C_HEADS)
            t = np.where(ok, -slope * LOG2E * dist, -np.inf)
            kvh, g = divmod(hh, C_GROUP)
            tabs[case, kvh, g * blk:(g + 1) * blk] = t
    return jnp.asarray(tabs)


def _swa_kernel(sink_ref, q_ref, kp_ref, kc_ref, kn_ref, vp_ref, vc_ref, vn_ref, bias_ref, o_ref):
    blk = C_BLOCK
    hd = C_HEAD_DIM
    kcat = jnp.concatenate([kp_ref[0], kc_ref[0], kn_ref[0]], axis=0)
    vcat = jnp.concatenate([vp_ref[0], vc_ref[0], vn_ref[0]], axis=0)
    q = q_ref[0]
    rblk = lax.broadcasted_iota(jnp.int32, (C_GROUP * blk, 1), 0) // blk
    lane = lax.broadcasted_iota(jnp.int32, (3 * blk, LANE - hd), 1)
    ones_pad = jnp.where(lane == 0, 1.0, 0.0).astype(bf16)
    q4s, sinks = [], []
    for kvh in range(C_KV_HEADS):
        q4s.append(jnp.concatenate(
            [q[:, (kvh * C_GROUP + g) * hd:(kvh * C_GROUP + g + 1) * hd] for g in range(C_GROUP)], axis=0))
        sk = jnp.zeros((C_GROUP * blk, 1), f32)
        for g in range(C_GROUP):
            sk = jnp.where(rblk == g, sink_ref[kvh * C_GROUP + g] * LOG2E, sk)
        sinks.append(sk)
    outs = []
    for h0 in range(0, C_KV_HEADS, SWA_LOCKSTEP):
        hs = range(h0, h0 + SWA_LOCKSTEP)
        ss = [_dot_nt(q4s[h], kcat[:, h * hd:(h + 1) * hd]) + bias_ref[0, h] for h in hs]
        ms = [jnp.maximum(jnp.max(s, axis=-1, keepdims=True), sinks[h]) for s, h in zip(ss, hs)]
        ps = [jnp.exp2(s - m).astype(bf16) for s, m in zip(ss, ms)]
        pvs = [_dot(p, jnp.concatenate([vcat[:, h * hd:(h + 1) * hd], ones_pad], axis=1)) for p, h in zip(ps, hs)]
        for pv, m, h in zip(pvs, ms, hs):
            den = pv[:, hd:hd + 1] + jnp.exp2(sinks[h] - m)
            o4 = pv[:, 0:hd] / den
            outs += [o4[g * blk:(g + 1) * blk, :] for g in range(C_GROUP)]
    o_ref[0] = jnp.concatenate(outs, axis=1).astype(o_ref.dtype)


def _swa(qkv, sinks):
    B, L, _ = qkv.shape
    nb = L // C_BLOCK
    dq = C_HEADS * C_HEAD_DIM
    dkv = C_KV_HEADS * C_HEAD_DIM
    k_col = dq // dkv
    v_col = k_col + 1

    def kv_spec(col, shift):
        return pl.BlockSpec((1, C_BLOCK, dkv),
                            lambda n, b: (b, jnp.clip(n + shift, 0, nb - 1), col))

    def edge_case(n, b):
        return (jnp.where(n == 0, 0, jnp.where(n == nb - 1, 2, 1)), 0, 0, 0)

    return pl.pallas_call(
        _swa_kernel,
        grid=(nb, B),
        in_specs=[pl.BlockSpec(memory_space=pltpu.SMEM),
                  pl.BlockSpec((1, C_BLOCK, dq), lambda n, b: (b, n, 0)),
                  kv_spec(k_col, -1), kv_spec(k_col, 0), kv_spec(k_col, 1),
                  kv_spec(v_col, -1), kv_spec(v_col, 0), kv_spec(v_col, 1),
                  pl.BlockSpec((1, C_KV_HEADS, C_GROUP * C_BLOCK, 3 * C_BLOCK), edge_case)],
        out_specs=pl.BlockSpec((1, C_BLOCK, dq), lambda n, b: (b, n, 0)),
        out_shape=jax.ShapeDtypeStruct((B, L, dq), bf16),
        compiler_params=_cparams("parallel", "parallel"),
        name="swa",
    )(sinks, qkv, qkv, qkv, qkv, qkv, qkv, qkv, _swa_bias_tables())


def _hyena_filters(L, w1, b1, w2, b2, w3, b3, w4, freq):
    hp = lax.Precision.HIGHEST
    t = jnp.linspace(0.0, 1.0, L, dtype=f32)[:, None]
    wpos = (2.0 * math.pi / L) * jnp.arange(L, dtype=f32)[:, None]
    bands = jnp.linspace(1e-4, HY_BANDS - 1, HY_BANDS, dtype=f32)[None, :]
    feats = jnp.concatenate([t, jnp.cos(bands * wpos), -jnp.sin(bands * wpos)], axis=-1)
    h = jnp.sin(freq[0] * (jnp.dot(feats, w1, precision=hp) + b1))
    h = jnp.sin(freq[1] * (jnp.dot(h, w2, precision=hp) + b2))
    h = jnp.sin(freq[2] * (jnp.dot(h, w3, precision=hp) + b3))
    h = jnp.dot(h, w4, precision=hp).reshape(L, HY_ORDER, 2, B_WIDTH)
    deltas = jnp.abs(jnp.linspace(HY_MIN_DECAY, HY_MAX_DECAY, B_WIDTH, dtype=f32))
    window = jnp.exp(-t * deltas[None, :])
    return h * window[:, None, None, :]


def _pad_rows(w, rows=8):
    return jnp.pad(w, ((0, rows - w.shape[0]), (0, 0)))


def _even_in_weights(w_in):
    hk = A_HEADS * A_DK
    q = w_in[:, 0:hk].reshape(D_MODEL, A_HEADS, A_DK)
    k = w_in[:, hk:2 * hk].reshape(D_MODEL, A_HEADS, A_DK)
    v = w_in[:, 2 * hk:A_QKV].reshape(D_MODEL, A_HEADS, A_DV)
    z = w_in[:, A_QKV:A_QKV + A_WIDTH].reshape(D_MODEL, A_HEADS, A_DV)
    qkvz = jnp.concatenate([q, k, v, z], axis=-1).reshape(D_MODEL, A_HEADS * 4 * A_DK)
    ab = w_in[:, A_QKV + A_WIDTH:A_QKV + A_WIDTH + 4 * A_HEADS].reshape(D_MODEL, 4, A_HEADS)
    ab = jnp.transpose(ab, (0, 2, 1))
    ab = jnp.pad(ab, ((0, 0), (0, 0), (0, LANE - 4))).reshape(D_MODEL, A_HEADS * LANE)
    hx = w_in[:, A_QKV + A_WIDTH + 4 * A_HEADS:]
    return jnp.concatenate([qkvz, ab, hx], axis=1).astype(bf16)


def _router_weights(w_group, w_expert):
    w = jnp.concatenate([w_group, w_expert], axis=1)
    w = jnp.pad(w, ((0, 0), (0, ROUTER_LANES - w.shape[1])))
    hi = w.astype(bf16)
    lo = (w - hi.astype(f32)).astype(bf16)
    return jnp.concatenate([hi, lo], axis=1)


def kernel(x_prompt, x_sample, norm_mix, norm_ffn, norm_final, ev_w_in, ev_conv_a, ev_alog_f, ev_alog_b, ev_dtb_f, ev_dtb_b, ev_onorm, ev_conv_b, hy_w1, hy_b1, hy_w2, hy_b2, hy_w3, hy_b3, hy_w4, hy_freq, hy_dbias, ev_w_out, od_w_qkv, od_sinks, od_w_out, moe_w_group, moe_w_expert, moe_w_gate, moe_w_up, moe_w_down):
    bp, L, _ = x_prompt.shape
    bs = x_sample.shape[0]
    B = bp + bs
    T = B * L
    xs = (x_prompt.reshape(bp * L, D_MODEL), x_sample.reshape(bs * L, D_MODEL))
    n_layers = moe_w_gate.shape[0]
    wg = moe_w_gate.reshape(n_layers * N_EXPERTS, D_MODEL, D_EXPERT)
    wu = moe_w_up.reshape(n_layers * N_EXPERTS, D_MODEL, D_EXPERT)
    wd = moe_w_down.reshape(n_layers * N_EXPERTS, D_EXPERT, D_MODEL)

    w_in = _even_in_weights(ev_w_in[0])
    n_qkvz = A_HEADS * 4 * A_DK
    n_ab = A_HEADS * LANE
    n_hx = 3 * B_WIDTH
    qkvz, abh, hx = _norm_matmul(xs, norm_mix[0], w_in, (n_qkvz, n_ab, n_hx), (bf16, f32, bf16))

    hk = A_HEADS * A_DK
    ca = ev_conv_a[0]
    conv_a = jnp.concatenate([ca[0:hk].reshape(A_HEADS, A_DK, 3), ca[hk:2 * hk].reshape(A_HEADS, A_DK, 3),
                              ca[2 * hk:].reshape(A_HEADS, A_DV, 3)], axis=1)
    conv_a = jnp.pad(jnp.transpose(conv_a, (0, 2, 1)), ((0, 0), (0, 5), (0, 0)))
    gate_sc = jnp.stack([ev_alog_f[0], ev_alog_b[0], ev_dtb_f[0], ev_dtb_b[0]], axis=1)
    o_a = _deltanet(qkvz.reshape(B, L, n_qkvz), abh.reshape(B, L, n_ab), conv_a, gate_sc, ev_onorm[0])

    filt = _hyena_filters(L, hy_w1[0], hy_b1[0], hy_w2[0], hy_b2[0], hy_w3[0], hy_b3[0], hy_w4[0], hy_freq[0])
    hsum = (filt[:, :, 0] + filt[:, :, 1]).reshape(L, HY_ORDER * B_WIDTH)
    hdiff = (filt[:, :, 0] - filt[:, :, 1]).reshape(L, HY_ORDER * B_WIDTH)
    kr, ki = _filter_spectrum(hsum, hdiff)
    conv_b = _pad_rows(ev_conv_b[0].T)
    o_b = _hyena_fft(hx.reshape(B, L, n_hx), conv_b, _hyena_tables(L), kr, ki, _pad_rows(hy_dbias[0]))

    w_out = ev_w_out[0].astype(bf16)
    x, hpa, hpb, route, route_t, counts = _proj_route(
        xs, (o_a.reshape(T, A_WIDTH), o_b.reshape(T, B_WIDTH)), (w_out[:A_WIDTH], w_out[A_WIDTH:]), norm_ffn[0],
        _router_weights(moe_w_group[0], moe_w_expert[0]))
    x = _moe(0, x, hpa, hpb, route, route_t, counts, wg, wu, wd, norm_final, final_norm=False)

    n_qkv = od_w_qkv.shape[-1]
    dq = C_HEADS * C_HEAD_DIM
    w_qkv = jnp.concatenate([od_w_qkv[0][:, :dq] * (LOG2E * C_HEAD_DIM ** -0.5), od_w_qkv[0][:, dq:]], axis=1)
    (qkv,) = _norm_matmul((x,), norm_mix[1], w_qkv.astype(bf16), (n_qkv,), (bf16,))
    o_c = _swa(qkv.reshape(B, L, n_qkv), od_sinks[0])
    x, hpa, hpb, route, route_t, counts = _proj_route(
        (x,), (o_c.reshape(T, C_HEADS * C_HEAD_DIM),), (od_w_out[0].astype(bf16),), norm_ffn[1],
        _router_weights(moe_w_group[1], moe_w_expert[1]))
    y_prompt, y_sample = _moe(1, x, hpa, hpb, route, route_t, counts, wg, wu, wd, norm_final, final_norm=True,
                              row_ranges=((0, bp * L), (bp * L, bs * L)))
    return (y_prompt.reshape(bp, L, D_MODEL), y_sample.reshape(bs, L, D_MODEL))
```

```python
import functools
import math

import jax
import jax.numpy as jnp
import numpy as np
from jax import lax
from jax.experimental import pallas as pl
from jax.experimental.pallas import tpu as pltpu
from jax.experimental.pallas import tpu_sc as plsc

f32 = jnp.float32
bf16 = jnp.bfloat16

EPS = 1e-6
D_MODEL = 1024

A_HEADS = 4
A_DK = 128
A_DV = 128
A_WIDTH = A_HEADS * A_DV
A_QKV = 2 * A_HEADS * A_DK + A_HEADS * A_DV
DELTA_CHUNK = 128
INV_BASE = 8
DELTA_GROUP = 8

B_WIDTH = D_MODEL - A_WIDTH
HY_ORDER = 2
HY_EMB = 33
HY_BANDS = (HY_EMB - 1) // 2
HY_TARGET = 1e-2
HY_MIN_DECAY = math.log(HY_TARGET) / 1.5
HY_MAX_DECAY = math.log(HY_TARGET) / 0.3
HY_CT = 256
HY_ROWS = 256

C_HEADS = 16
C_KV_HEADS = 4
C_HEAD_DIM = 64
C_GROUP = C_HEADS // C_KV_HEADS
WINDOW = 128
C_BLOCK = 128

N_GROUPS = 4
EXPERTS_PER_GROUP = 8
N_EXPERTS = N_GROUPS * EXPERTS_PER_GROUP
D_EXPERT = 256
ROUTER_LANES = 128

LANE = 128
VMEM_LIMIT = 56 * 1024 * 1024


def _cparams(*sem):
    return pltpu.CompilerParams(dimension_semantics=sem, vmem_limit_bytes=VMEM_LIMIT)


def _dot(a, b):
    return jnp.dot(a, b, preferred_element_type=f32)


def _dot_nt(a, b):
    return lax.dot_general(a, b, (((1,), (1,)), ((), ())), preferred_element_type=f32)


def _dot_tn(a, b):
    return lax.dot_general(a, b, (((0,), (0,)), ((), ())), preferred_element_type=f32)


def _sigmoid(x):
    return 1.0 / (1.0 + jnp.exp(-x))


def _rms(x, g):
    return x * lax.rsqrt(jnp.mean(x * x, axis=-1, keepdims=True) + EPS) * g


def _row_sources(xs, tm):
    tiles = [x.shape[0] // tm for x in xs]
    starts = [sum(tiles[:k]) for k in range(len(xs))]

    def spec(x, lo, n):
        return pl.BlockSpec((tm, x.shape[1]), lambda i: (jnp.clip(i - lo, 0, n - 1), 0))

    return [spec(x, lo, n) for x, lo, n in zip(xs, starts, tiles)], tuple(starts[1:])


def _select_rows(refs, starts):
    i = pl.program_id(0)
    x = refs[0][...]
    for ref, lo in zip(refs[1:], starts):
        x = jnp.where(i >= lo, ref[...], x)
    return x


def _norm_matmul_kernel(*refs, starts):
    n_src = len(starts) + 1
    g_ref, w_ref = refs[n_src:n_src + 2]
    o_refs = refs[n_src + 2:]
    h = _rms(_select_rows(refs[:n_src], starts), g_ref[...]).astype(bf16)
    off = 0
    for o_ref in o_refs:
        n = o_ref.shape[-1]
        o_ref[...] = _dot(h, w_ref[:, off:off + n]).astype(o_ref.dtype)
        off += n


def _norm_matmul(xs, gain, w, splits, dtypes, tm=512):
    T = sum(x.shape[0] for x in xs)
    D = xs[0].shape[1]
    n_all = w.shape[1]
    x_specs, starts = _row_sources(xs, tm)
    return pl.pallas_call(
        functools.partial(_norm_matmul_kernel, starts=starts),
        grid=(T // tm,),
        in_specs=x_specs + [pl.BlockSpec((1, D), lambda i: (0, 0)),
                            pl.BlockSpec((D, n_all), lambda i: (0, 0))],
        out_specs=[pl.BlockSpec((tm, n), lambda i: (i, 0)) for n in splits],
        out_shape=[jax.ShapeDtypeStruct((T, n), dt) for n, dt in zip(splits, dtypes)],
        compiler_params=_cparams("parallel"),
        name="norm_matmul",
    )(*xs, gain.reshape(1, D), w)


MK_LOWER, MK_UPPER, MK_BASE, MK_LEVEL0 = 0, 1, 2, 3
N_LEVELS = int(math.log2(DELTA_CHUNK // INV_BASE))
MK_EYE = MK_LEVEL0 + N_LEVELS
N_MASKS = MK_EYE + 1


def _delta_masks():
    c = np.arange(DELTA_CHUNK)[:, None]
    s = np.arange(DELTA_CHUNK)[None, :]
    m = [s <= c, s >= c, (c // INV_BASE == s // INV_BASE) & (c != s)]
    b = INV_BASE
    while b < DELTA_CHUNK:
        m.append((c // (2 * b) == s // (2 * b)) & (c // b != s // b))
        b *= 2
    m.append(c == s)
    m = np.stack(m).astype(np.float32)
    return jnp.asarray(m), jnp.asarray(m, dtype=bf16)


def _tri_inverse(mats, mk_ref, mk16_ref):
    ps = [a * mk_ref[MK_BASE] for a in mats]
    ts = [mk_ref[MK_EYE] - p for p in ps]
    n = 2
    while n < INV_BASE:
        p16s = [p.astype(bf16) for p in ps]
        ps = [_dot(p16, p16) for p16 in p16s]
        ts = [t + _dot(t.astype(bf16), p.astype(bf16)) for t, p in zip(ts, ps)]
        n *= 2
    a16s = [a.astype(bf16) for a in mats]
    t16s = [t.astype(bf16) for t in ts]
    for lvl in range(N_LEVELS):
        xs = [_dot(a16 * mk16_ref[MK_LEVEL0 + lvl], t16) for a16, t16 in zip(a16s, t16s)]
        t16s = [t16 - _dot(t16, x.astype(bf16)).astype(bf16) for t16, x in zip(t16s, xs)]
    return t16s


def _deltanet_kernel(sc_ref, x_ref, ab_ref, cw_ref, on_ref, mk_ref, mk16_ref, o_ref, mp_s, nr_s, gl_s, o_s):
    hd = pl.program_id(1)
    L = x_ref.shape[1]
    C = DELTA_CHUNK
    nchunk = L // C
    group = math.gcd(DELTA_GROUP, nchunk)
    nqkv = 3 * A_DK
    halo = 16

    lane = lax.broadcasted_iota(jnp.int32, (1, LANE), 1)
    alog = jnp.where(lane == 2, sc_ref[hd, 0], jnp.where(lane == 3, sc_ref[hd, 1], 0.0))
    dtb = jnp.where(lane == 2, sc_ref[hd, 2], jnp.where(lane == 3, sc_ref[hd, 3], 0.0))
    neg_a = jnp.where((lane == 2) | (lane == 3), -jnp.exp(alog), 0.0)
    rowi = lax.broadcasted_iota(jnp.int32, (C, 1), 0)
    cw = cw_ref[0]

    def chunk_inputs(i):
        r0 = pl.multiple_of(i * C, C)
        xc = x_ref[0, pl.ds(r0, C), 0:nqkv].astype(f32)
        rp = pl.multiple_of(jnp.maximum(r0 - halo, 0), halo)
        rn = pl.multiple_of(jnp.minimum(r0 + C, L - halo), halo)
        prow = x_ref[0, pl.ds(rp, halo), 0:nqkv][halo - 1:halo, :].astype(f32)
        nrow = x_ref[0, pl.ds(rn, halo), 0:nqkv][0:1, :].astype(f32)
        prow = jnp.where(i > 0, prow, 0.0)
        nrow = jnp.where(i < nchunk - 1, nrow, 0.0)
        prev = jnp.where(rowi == 0, prow, pltpu.roll(xc, 1, 0))
        nxt = jnp.where(rowi == C - 1, nrow, pltpu.roll(xc, C - 1, 0))
        y = prev * cw[0:1, :] + xc * cw[1:2, :] + nxt * cw[2:3, :]
        y = y * _sigmoid(y)
        q = y[:, 0:A_DK]
        k = y[:, A_DK:2 * A_DK]
        vc = y[:, 2 * A_DK:nqkv]
        qc = q * lax.rsqrt(jnp.sum(q * q, axis=-1, keepdims=True) + EPS) * (A_DK ** -0.5)
        kc = k * lax.rsqrt(jnp.sum(k * k, axis=-1, keepdims=True) + EPS)

        abc = ab_ref[0, pl.ds(r0, C), :]
        zg = abc + dtb
        g = neg_a * (jnp.maximum(zg, 0.0) + jnp.log(1.0 + jnp.exp(-jnp.abs(zg))))
        low16 = mk16_ref[MK_LOWER]
        g1 = g.astype(bf16)
        e1 = g - g1.astype(f32)
        g2 = e1.astype(bf16)
        g3 = (e1 - g2.astype(f32)).astype(bf16)
        pre = _dot(low16, g1) + _dot(low16, g2) + _dot(low16, g3)
        tot = pre[C - 1:C, :]
        suf = tot - pre + g
        beta_all = _sigmoid(abc)

        k16 = kc.astype(bf16)
        kk = _dot_nt(k16, k16)
        qk = _dot_nt(qc.astype(bf16), k16)

        chains = []
        for rev in (0, 1):
            beta = beta_all[:, rev:rev + 1]
            gcc = (suf if rev else pre)[:, 2 + rev:3 + rev]
            tot11 = tot[:, 2 + rev:3 + rev]
            keep = mk_ref[MK_UPPER if rev else MK_LOWER]
            bc = jnp.broadcast_to(gcc * LOG2E, (C, C))
            decay = jnp.exp2((bc - bc.T) * keep) * keep
            a = (beta * kk) * decay
            eg = jnp.exp(gcc)
            chains.append(dict(
                i=i, rev=rev, a=a, gl=jnp.exp(tot11), qd=qc * eg,
                rhs=jnp.concatenate([vc * beta, kc * (beta * eg)], axis=1).astype(bf16),
                kd16=(kc * jnp.exp(tot11 - gcc)).astype(bf16),
                qk16=(qk * decay).astype(bf16)))
        return chains

    def prep(it, carry):
        chains = []
        for gi in range(group):
            chains += chunk_inputs(it * group + gi)
        t16s = _tri_inverse([c["a"] for c in chains], mk_ref, mk16_ref)
        uws = [_dot(t16, c["rhs"]).astype(bf16) for t16, c in zip(t16s, chains)]
        kts = [_dot_tn(c["kd16"], uw) for c, uw in zip(chains, uws)]
        prs = [_dot(c["qk16"], uw) for c, uw in zip(chains, uws)]
        for c, kt, pr in zip(chains, kts, prs):
            rev, i = c["rev"], c["i"]
            mp_s[rev, i, 0:A_DK, :] = (-kt[:, A_DV:]).astype(bf16)
            mp_s[rev, i, A_DK:A_DK + C, :] = (c["qd"] - pr[:, A_DV:]).astype(bf16)
            nr_s[rev, i, 0:A_DK, :] = kt[:, 0:A_DV]
            nr_s[rev, i, A_DK:A_DK + C, :] = pr[:, 0:A_DV]
            gl_s[rev, i] = jnp.broadcast_to(c["gl"], (8, LANE))
        return carry

    lax.fori_loop(0, nchunk // group, prep, 0)

    def scan(i, carry):
        out = []
        for rev, S in enumerate(carry):
            j = (nchunk - 1 - i) if rev else i
            y = _dot(mp_s[rev, j], S.astype(bf16))
            nr = nr_s[rev, j]
            o_s[rev, pl.ds(pl.multiple_of(j * C, C), C), :] = y[A_DK:, :] + nr[A_DK:, :]
            out.append(S * gl_s[rev, j][0:1, :] + y[0:A_DK, :] + nr[0:A_DK, :])
        return tuple(out)

    S0 = jnp.zeros((A_DK, A_DV), f32)
    lax.fori_loop(0, nchunk, scan, (S0, S0))

    def finish(it, carry):
        for gi in range(group):
            r0 = pl.multiple_of((it * group + gi) * C, C)
            o = o_s[0, pl.ds(r0, C), :] + o_s[1, pl.ds(r0, C), :]
            zg = x_ref[0, pl.ds(r0, C), nqkv:nqkv + A_DV].astype(f32)
            o_ref[0, pl.ds(r0, C), :] = (_rms(o, on_ref[...]) * (zg * _sigmoid(zg))).astype(o_ref.dtype)
        return carry

    lax.fori_loop(0, nchunk // group, finish, 0)


def _deltanet(qkvz, abh, conv_w, gate_sc, onorm):
    B, L, _ = qkvz.shape
    C = DELTA_CHUNK
    nchunk = L // C
    hw = 4 * A_DK
    scratch = [pltpu.VMEM((2, nchunk, A_DK + C, A_DV), bf16),
               pltpu.VMEM((2, nchunk, A_DK + C, A_DV), f32),
               pltpu.VMEM((2, nchunk, 8, LANE), f32),
               pltpu.VMEM((2, L, A_DV), f32)]
    return pl.pallas_call(
        _deltanet_kernel,
        grid=(B, A_HEADS),
        in_specs=[pl.BlockSpec(memory_space=pltpu.SMEM),
                  pl.BlockSpec((1, L, hw), lambda b, h: (b, 0, h)),
                  pl.BlockSpec((1, L, LANE), lambda b, h: (b, 0, h)),
                  pl.BlockSpec((1, 8, 3 * A_DK), lambda b, h: (h, 0, 0)),
                  pl.BlockSpec((1, A_DV), lambda b, h: (0, 0)),
                  pl.BlockSpec((N_MASKS, C, C), lambda b, h: (0, 0, 0)),
                  pl.BlockSpec((N_MASKS, C, C), lambda b, h: (0, 0, 0))],
        out_specs=pl.BlockSpec((1, L, A_DV), lambda b, h: (b, 0, h)),
        out_shape=jax.ShapeDtypeStruct((B, L, A_WIDTH), bf16),
        scratch_shapes=scratch,
        compiler_params=_cparams("parallel", "parallel"),
        name="deltanet",
    )(gate_sc, qkvz, abh, conv_w, onorm.reshape(1, A_DV), *_delta_masks())


def _filter_spectrum_kernel(c_ref, s_ref, hs_ref, hd_ref, kr_ref, ki_ref):
    L = c_ref.shape[1]
    rows = kr_ref.shape[1]
    scale = 1.0 / (2 * L)

    def split(x):
        hi = x.astype(bf16)
        return hi, (x - hi.astype(f32)).astype(bf16)

    hs_hi, hs_lo = split(hs_ref[...])
    hd_hi, hd_lo = split(hd_ref[...])
    for k1 in range(kr_ref.shape[0]):
        c = c_ref[k1 * rows:(k1 + 1) * rows, :]
        s = s_ref[k1 * rows:(k1 + 1) * rows, :]
        kr_ref[k1] = (_dot(c, hs_hi) + _dot(c, hs_lo)) * scale
        ki_ref[k1] = (_dot(s, hd_hi) + _dot(s, hd_lo)) * (-scale)


def _filter_spectrum(hsum, hdiff):
    L, n = hsum.shape
    ct = HY_CT
    n_fft = 2 * L
    n2 = L // FFT_Z1
    d = jnp.arange(L, dtype=jnp.int32)[None, :]
    a1 = ((jnp.arange(FFT_K1, dtype=jnp.int32)[:, None] * d) % n_fft).astype(f32) * (2.0 * math.pi / n_fft)
    a2 = ((jnp.arange(n2, dtype=jnp.int32)[:, None] * d) % n2).astype(f32) * (2.0 * math.pi / n2)
    c1, s1 = jnp.cos(a1)[:, None, :], jnp.sin(a1)[:, None, :]
    c2, s2 = jnp.cos(a2)[None, :, :], jnp.sin(a2)[None, :, :]
    cmat = (c1 * c2 - s1 * s2).reshape(FFT_K1 * n2, L).astype(bf16)
    smat = (s1 * c2 + c1 * s2).reshape(FFT_K1 * n2, L).astype(bf16)
    once = pl.Buffered(1)
    return pl.pallas_call(
        _filter_spectrum_kernel,
        grid=(n // ct,),
        in_specs=[pl.BlockSpec((FFT_K1 * n2, L), lambda j: (0, 0), pipeline_mode=once),
                  pl.BlockSpec((FFT_K1 * n2, L), lambda j: (0, 0), pipeline_mode=once),
                  pl.BlockSpec((L, ct), lambda j: (0, j)),
                  pl.BlockSpec((L, ct), lambda j: (0, j))],
        out_specs=[pl.BlockSpec((FFT_K1, n2, ct), lambda j: (0, 0, j)),
                   pl.BlockSpec((FFT_K1, n2, ct), lambda j: (0, 0, j))],
        out_shape=[jax.ShapeDtypeStruct((FFT_K1, n2, n), f32),
                   jax.ShapeDtypeStruct((FFT_K1, n2, n), f32)],
        compiler_params=_cparams("parallel"),
        name="filter_spectrum",
    )(cmat, smat, hsum, hdiff)


FFT_N1 = 16
FFT_K1 = FFT_N1 // 2 + 1
FFT_Z1 = FFT_N1 // 2
FFT_ROWS = 32


def _lincomb(terms):
    groups = {}
    for c, a in terms:
        if abs(c) > 1e-9:
            groups.setdefault(round(abs(c), 9), []).append((c > 0, a))
    total = None
    for mag, items in groups.items():
        pos = [a for p, a in items if p]
        neg = [a for p, a in items if not p]
        s = None
        for a in pos:
            s = a if s is None else s + a
        for a in neg:
            s = -a if s is None else s - a
        if mag != 1.0:
            s = s * mag
        total = s if total is None else total + s
    return total


def _hyena_fft_kernel(m2f_ref, m2i_ref, twr_ref, twi_ref, x1_ref, x2_ref, v_ref, w1_ref, w2_ref, wv_ref,
                      kr0_ref, ki0_ref, kr1_ref, ki1_ref, db_ref, o_ref, z_s, g_s, d_s, y_s, b_s):
    L, ct = z_s.shape
    n2 = L // FFT_Z1
    R = min(HY_ROWS, L)
    nt = L // R
    r1 = min(FFT_ROWS, n2)
    halo = 16
    rowi = lax.broadcasted_iota(jnp.int32, (R, 1), 0)
    ang = 2.0 * math.pi / FFT_N1
    c16 = [[math.cos(ang * a * b) for b in range(FFT_K1)] for a in range(FFT_Z1)]
    s16 = [[math.sin(ang * a * b) for b in range(FFT_K1)] for a in range(FFT_Z1)]

    def conv_into(dst, x_ref, w_ref):
        def body(t, carry):
            r0 = pl.multiple_of(t * R, R)
            x = x_ref[0, pl.ds(r0, R), :].astype(f32)
            rp = pl.multiple_of(jnp.maximum(r0 - halo, 0), halo)
            rn = pl.multiple_of(jnp.minimum(r0 + R, L - halo), halo)
            prow = jnp.where(t > 0, x_ref[0, pl.ds(rp, halo), :][halo - 1:halo, :].astype(f32), 0.0)
            nrow = jnp.where(t < nt - 1, x_ref[0, pl.ds(rn, halo), :][0:1, :].astype(f32), 0.0)
            prev = jnp.where(rowi == 0, prow, pltpu.roll(x, 1, 0))
            nxt = jnp.where(rowi == R - 1, nrow, pltpu.roll(x, R - 1, 0))
            w = w_ref[...]
            dst[pl.ds(r0, R), :] = prev * w[0:1, :] + x * w[1:2, :] + nxt * w[2:3, :]
            return carry
        lax.fori_loop(0, nt, body, 0)

    def cols(k1):
        return slice(k1 * ct, (k1 + 1) * ct)

    conv_into(z_s, v_ref, wv_ref)
    stages = ((x1_ref, w1_ref, kr0_ref, ki0_ref), (x2_ref, w2_ref, kr1_ref, ki1_ref))
    for o, (g_ref, gw_ref, kr_ref, ki_ref) in enumerate(stages):
        conv_into(g_s, g_ref, gw_ref)

        def forward1(rt, carry):
            r0 = pl.multiple_of(rt * r1, r1)
            zs = [z_s[pl.ds(a * n2 + r0, r1), :] for a in range(FFT_Z1)]
            for k1 in range(FFT_K1):
                ar = _lincomb([(c16[a][k1], zs[a]) for a in range(FFT_Z1)])
                ai = _lincomb([(-s16[a][k1], zs[a]) for a in range(FFT_Z1)])
                twr = twr_ref[pl.ds(r0, r1), cols(k1)]
                twi = twi_ref[pl.ds(r0, r1), cols(k1)]
                if ai is None:
                    pr, pi = ar * twr, -(ar * twi)
                else:
                    pr, pi = ar * twr + ai * twi, ai * twr - ar * twi
                d_s[pl.ds(r0, r1), cols(k1)] = pr.astype(bf16)
                d_s[pl.ds(n2 + r0, r1), cols(k1)] = pi.astype(bf16)
            return carry

        lax.fori_loop(0, n2 // r1, forward1, 0)

        for k1 in range(FFT_K1):
            xk = _dot(m2f_ref[...], d_s[:, cols(k1)])
            xr, xi = xk[0:n2, :], xk[n2:, :]
            kr, ki = kr_ref[k1], ki_ref[k1]
            y_s[0:n2, cols(k1)] = (xr * kr - xi * ki).astype(bf16)
            y_s[n2:2 * n2, cols(k1)] = (xr * ki + xi * kr).astype(bf16)

        for k1 in range(FFT_K1):
            bk = _dot(m2i_ref[...], y_s[:, cols(k1)])
            br, bi = bk[0:n2, :], bk[n2:, :]
            twr, twi = twr_ref[:, cols(k1)], twi_ref[:, cols(k1)]
            b_s[0:n2, cols(k1)] = br * twr - bi * twi
            b_s[n2:2 * n2, cols(k1)] = br * twi + bi * twr

        def inverse1(rt, carry):
            r0 = pl.multiple_of(rt * r1, r1)
            brs = [b_s[pl.ds(r0, r1), cols(j)] for j in range(FFT_K1)]
            bis = [b_s[pl.ds(n2 + r0, r1), cols(j)] for j in range(FFT_K1)]
            for a in range(FFT_Z1):
                terms = [(1.0, brs[0]), (-1.0 if a % 2 else 1.0, brs[FFT_K1 - 1])]
                for j in range(1, FFT_K1 - 1):
                    terms += [(2.0 * c16[a][j], brs[j]), (-2.0 * s16[a][j], bis[j])]
                rows = pl.ds(a * n2 + r0, r1)
                z = g_s[rows, :] * (_lincomb(terms) + z_s[rows, :] * db_ref[o:o + 1, :])
                if o + 1 < len(stages):
                    z_s[rows, :] = z
                else:
                    o_ref[0, rows, :] = z.astype(o_ref.dtype)
            return carry

        lax.fori_loop(0, n2 // r1, inverse1, 0)


def _hyena_tables(L):
    n_fft = 2 * L
    n2 = L // FFT_Z1
    ct = HY_CT
    idx = jnp.arange(n2, dtype=jnp.int32)
    a2 = ((idx[:, None] * idx[None, :]) % n2).astype(f32) * (2.0 * math.pi / n2)
    c2, s2 = jnp.cos(a2), jnp.sin(a2)
    m2f = jnp.block([[c2, s2], [-s2, c2]]).astype(bf16)
    m2i = jnp.block([[c2, -s2], [s2, c2]]).astype(bf16)
    at = (idx[:, None] * jnp.arange(FFT_K1, dtype=jnp.int32)[None, :]).astype(f32) * (2.0 * math.pi / n_fft)
    twr = jnp.broadcast_to(jnp.cos(at)[:, :, None], (n2, FFT_K1, ct)).reshape(n2, FFT_K1 * ct)
    twi = jnp.broadcast_to(jnp.sin(at)[:, :, None], (n2, FFT_K1, ct)).reshape(n2, FFT_K1 * ct)
    return m2f, m2i, twr, twi


def _hyena_fft(hx, conv_w, tables, kr, ki, dbias):
    B, L, _ = hx.shape
    ct = HY_CT
    nct = B_WIDTH // ct
    n2 = L // FFT_Z1
    wide = FFT_K1 * ct
    once = pl.Buffered(1)
    m2f, m2i, twr, twi = tables

    def xspec(part):
        return pl.BlockSpec((1, L, ct), lambda c, b: (b, 0, part * nct + c))

    def wspec(part):
        return pl.BlockSpec((8, ct), lambda c, b: (0, part * nct + c))

    def kspec(order):
        return pl.BlockSpec((FFT_K1, n2, ct), lambda c, b: (0, 0, order * nct + c), pipeline_mode=once)

    def const(shape):
        return pl.BlockSpec(shape, lambda c, b: (0, 0), pipeline_mode=once)

    return pl.pallas_call(
        _hyena_fft_kernel,
        grid=(nct, B),
        in_specs=[const((2 * n2, 2 * n2)), const((2 * n2, 2 * n2)), const((n2, wide)), const((n2, wide)),
                  xspec(0), xspec(1), xspec(2), wspec(0), wspec(1), wspec(2),
                  kspec(0), kspec(0), kspec(1), kspec(1),
                  pl.BlockSpec((8, ct), lambda c, b: (0, c))],
        out_specs=pl.BlockSpec((1, L, ct), lambda c, b: (b, 0, c)),
        out_shape=jax.ShapeDtypeStruct((B, L, B_WIDTH), bf16),
        scratch_shapes=[pltpu.VMEM((L, ct), f32), pltpu.VMEM((L, ct), f32),
                        pltpu.VMEM((2 * n2, wide), bf16), pltpu.VMEM((2 * n2, wide), bf16),
                        pltpu.VMEM((2 * n2, wide), f32)],
        compiler_params=_cparams("parallel", "parallel"),
        name="hyena_fft",
    )(m2f, m2i, twr, twi, hx, hx, hx, conv_w, conv_w, conv_w, kr, ki, kr, ki, dbias)


RT_E1, RT_E2, RT_W1, RT_W2, RT_R1, RT_R2 = 0, 1, 2, 3, 4, 5
ROUTE_TM = 512
EXPERT_TM = 512
SC_WINDOW = 128


def _pack_bf16_pairs(x):
    n = x.shape[1] // 2
    bits = lax.bitcast_convert_type(x.astype(bf16).astype(f32), jnp.uint32)
    return bits[:, :n] | (bits[:, n:] >> 16)


def _unpack_bf16_pairs(p):
    hi = lax.bitcast_convert_type(p & jnp.uint32(0xFFFF0000), f32)
    lo = lax.bitcast_convert_type(p << 16, f32)
    return hi, lo


def _route(logits, low_ref):
    lane = lax.broadcasted_iota(jnp.int32, logits.shape, 1)
    neg = -jnp.inf
    big = ROUTER_LANES
    gl = jnp.where(lane < N_GROUPS, logits, neg)
    gmax = jnp.max(gl, axis=-1, keepdims=True)
    gsel = jnp.min(jnp.where(gl == gmax, lane, big), axis=-1, keepdims=True)
    psel = 1.0 / jnp.sum(jnp.exp(gl - gmax), axis=-1, keepdims=True)
    first = N_GROUPS + gsel * EXPERTS_PER_GROUP
    el = jnp.where((lane >= first) & (lane < first + EXPERTS_PER_GROUP), logits, neg)
    t1 = jnp.max(el, axis=-1, keepdims=True)
    i1 = jnp.min(jnp.where(el == t1, lane, big), axis=-1, keepdims=True)
    el2 = jnp.where(lane == i1, neg, el)
    t2 = jnp.max(el2, axis=-1, keepdims=True)
    i2 = jnp.min(jnp.where(el2 == t2, lane, big), axis=-1, keepdims=True)
    ex = jnp.exp(t2 - t1)
    w1 = psel / (1.0 + ex)
    w2 = ex * w1
    e1 = i1 - N_GROUPS
    e2 = i2 - N_GROUPS
    onehot = jnp.where((lane == e1) | (lane == e2), 1.0, 0.0)
    before = _dot(low_ref[...], onehot.astype(bf16))
    r1 = jnp.sum(jnp.where(lane == e1, before, 0.0), axis=-1, keepdims=True)
    r2 = jnp.sum(jnp.where(lane == e2, before, 0.0), axis=-1, keepdims=True)
    tm = logits.shape[0]
    counts = before[tm - 1:tm, :] + onehot[tm - 1:tm, :]
    rec = jnp.zeros(logits.shape, f32)
    for ln, val in ((RT_E1, e1.astype(f32)), (RT_E2, e2.astype(f32)), (RT_W1, w1), (RT_W2, w2), (RT_R1, r1), (RT_R2, r2)):
        rec = jnp.where(lane == ln, val, rec)
    return rec, counts


def _proj_route_kernel(*refs, n_in, starts):
    n_src = len(starts) + 1
    x = _select_rows(refs[:n_src], starts)
    refs = refs[n_src:]
    a_refs = refs[0:n_in]
    w_refs = refs[n_in:2 * n_in]
    g_ref, wr_ref, low_ref, xo_ref, hpa_ref, hpb_ref, rt_ref, rtt_ref, cnt_ref = refs[2 * n_in:]
    for a_ref, w_ref in zip(a_refs, w_refs):
        x = x + _dot(a_ref[...], w_ref[...])
    xo_ref[...] = x
    h = _rms(x, g_ref[...])
    hp = _pack_bf16_pairs(h)
    quarter = hp.shape[1] // 2
    hpa_ref[...] = hp[:, :quarter]
    hpb_ref[...] = hp[:, quarter:]
    h_hi = h.astype(bf16)
    h_lo = (h - h_hi.astype(f32)).astype(bf16)
    both = _dot(h_hi, wr_ref[...])
    logits = both[:, :ROUTER_LANES] + both[:, ROUTER_LANES:] + _dot(h_lo, wr_ref[:, :ROUTER_LANES])
    rec, counts = _route(logits, low_ref)
    rt_ref[...] = rec
    rtt_ref[...] = rec.T[0:rtt_ref.shape[0], :]
    cnt_ref[0] = jnp.broadcast_to(counts, cnt_ref.shape[1:])


def _proj_route(xs, acts, ws, gain, w_router):
    T = sum(x.shape[0] for x in xs)
    D = xs[0].shape[1]
    tm = ROUTE_TM
    n_in = len(acts)
    low = jnp.asarray(np.tril(np.ones((tm, tm), np.float32), -1), dtype=bf16)
    in_specs, starts = _row_sources(xs, tm)
    in_specs += [pl.BlockSpec((tm, a.shape[1]), lambda i: (i, 0)) for a in acts]
    in_specs += [pl.BlockSpec(w.shape, lambda i: (0, 0)) for w in ws]
    in_specs += [pl.BlockSpec((1, D), lambda i: (0, 0)),
                 pl.BlockSpec((D, 2 * ROUTER_LANES), lambda i: (0, 0)),
                 pl.BlockSpec((tm, tm), lambda i: (0, 0))]
    return pl.pallas_call(
        functools.partial(_proj_route_kernel, n_in=n_in, starts=starts),
        grid=(T // tm,),
        in_specs=in_specs,
        out_specs=[pl.BlockSpec((tm, D), lambda i: (i, 0)),
                   pl.BlockSpec((tm, D // 4), lambda i: (i, 0)),
                   pl.BlockSpec((tm, D // 4), lambda i: (i, 0)),
                   pl.BlockSpec((tm, ROUTER_LANES), lambda i: (i, 0)),
                   pl.BlockSpec((8, tm), lambda i: (0, i)),
                   pl.BlockSpec((1, 8, ROUTER_LANES), lambda i: (i, 0, 0))],
        out_shape=[jax.ShapeDtypeStruct((T, D), f32),
                   jax.ShapeDtypeStruct((T, D // 4), jnp.uint32),
                   jax.ShapeDtypeStruct((T, D // 4), jnp.uint32),
                   jax.ShapeDtypeStruct((T, ROUTER_LANES), f32),
                   jax.ShapeDtypeStruct((8, T), f32),
                   jax.ShapeDtypeStruct((T // tm, 8, ROUTER_LANES), f32)],
        compiler_params=_cparams("parallel"),
        name="proj_route",
    )(*xs, *acts, *ws, gain.reshape(1, D), w_router, low)


def _moe_rows(T):
    return 2 * T + N_EXPERTS * EXPERT_TM


def _moe_plan(route_t, counts):
    T = route_t.shape[1]
    tm = ROUTE_TM
    cnt = counts[:, 0, :N_EXPERTS].astype(jnp.int32)
    total = jnp.sum(cnt, axis=0)
    padded = ((total + EXPERT_TM - 1) // EXPERT_TM) * EXPERT_TM
    ends = jnp.cumsum(padded)
    tile_base = (ends - padded)[None, :] + jnp.cumsum(cnt, axis=0) - cnt
    base = jnp.broadcast_to(tile_base.T[:, :, None], (N_EXPERTS, T // tm, tm)).reshape(N_EXPERTS, T)
    ids = jnp.arange(N_EXPERTS, dtype=jnp.int32)[:, None]
    dest = []
    for e_row, r_row in ((RT_E1, RT_R1), (RT_E2, RT_R2)):
        e = route_t[e_row].astype(jnp.int32)
        dest.append(jnp.sum(jnp.where(ids == e[None, :], base, 0), axis=0) + route_t[r_row].astype(jnp.int32))
    n_tiles = _moe_rows(T) // EXPERT_TM
    n_used = ends[-1] // EXPERT_TM
    starts = jnp.arange(n_tiles, dtype=jnp.int32) * EXPERT_TM
    tile_expert = jnp.sum((starts[:, None] >= ends[None, :]).astype(jnp.int32), axis=1)
    last_expert = jnp.sum((jnp.maximum(n_used - 1, 0) * EXPERT_TM >= ends).astype(jnp.int32))
    tile_expert = jnp.minimum(tile_expert, last_expert)
    return dest, tile_expert, n_used.reshape(1).astype(jnp.int32)


def _sc_mesh():
    return plsc.VectorSubcoreMesh(core_axis_name="core", subcore_axis_name="subcore")


def _sc_scatter_pair(rows, d0, d1, n_out):
    T, W = rows.shape
    win = SC_WINDOW

    @pl.kernel(out_type=jax.ShapeDtypeStruct((n_out, W), rows.dtype), mesh=_sc_mesh(), name="moe_sc_scatter")
    def scatter(x_hbm, i0_hbm, i1_hbm, o_hbm):
        def body(x_vmem, i0_vmem, i1_vmem):
            pltpu.sync_copy(x_vmem, o_hbm.at[i0_vmem.at[0]])
            pltpu.sync_copy(x_vmem, o_hbm.at[i1_vmem.at[0]])

        pltpu.emit_pipeline(
            body, grid=(T // win,),
            in_specs=[pl.BlockSpec((win, W), index_map=lambda i: (i, 0)),
                      pl.BlockSpec((1, win), index_map=lambda i: (0, i)),
                      pl.BlockSpec((1, win), index_map=lambda i: (0, i))],
            out_specs=[],
            core_axis_name=("core", "subcore"), dimension_semantics=(pltpu.PARALLEL,),
        )(x_hbm, i0_hbm, i1_hbm)

    return scatter(rows, d0.reshape(1, T), d1.reshape(1, T))


def _sc_gather(table, idx):
    n = idx.shape[0]
    W = table.shape[1]
    win = SC_WINDOW

    @pl.kernel(out_type=jax.ShapeDtypeStruct((n, W), table.dtype), mesh=_sc_mesh(), name="moe_sc_gather")
    def gather(t_hbm, i_hbm, o_hbm):
        def body(i_vmem, o_vmem):
            pltpu.sync_copy(t_hbm.at[i_vmem.at[0]], o_vmem)

        pltpu.emit_pipeline(
            body, grid=(n // win,),
            in_specs=[pl.BlockSpec((1, win), index_map=lambda i: (0, i))],
            out_specs=[pl.BlockSpec((win, W), index_map=lambda i: (i, 0))],
            core_axis_name=("core", "subcore"), dimension_semantics=(pltpu.PARALLEL,),
        )(i_hbm, o_hbm)

    return gather(table, idx.reshape(1, n))


def _expert_kernel(te_ref, nu_ref, xa_ref, xb_ref, wg_ref, wu_ref, wd_ref, ya_ref, yb_ref):
    @pl.when(pl.program_id(0) < nu_ref[0])
    def _():
        hi, lo = _unpack_bf16_pairs(jnp.concatenate([xa_ref[...], xb_ref[...]], axis=1))
        x = jnp.concatenate([hi, lo], axis=1).astype(bf16)
        gate = _dot(x, wg_ref[0].astype(bf16))
        up = _dot(x, wu_ref[0].astype(bf16))
        hg = (gate * _sigmoid(gate) * up).astype(bf16)
        y = _pack_bf16_pairs(_dot(hg, wd_ref[0].astype(bf16)))
        quarter = y.shape[1] // 2
        ya_ref[...] = y[:, :quarter]
        yb_ref[...] = y[:, quarter:]

    @pl.when(pl.program_id(0) >= nu_ref[0])
    def _():
        ya_ref[...] = jnp.zeros(ya_ref.shape, ya_ref.dtype)
        yb_ref[...] = jnp.zeros(yb_ref.shape, yb_ref.dtype)


def _experts(tile_expert, n_used, xa, xb, wg, wu, wd):
    n_rows, quarter = xa.shape
    D = 4 * quarter
    tm = EXPERT_TM
    grid_spec = pltpu.PrefetchScalarGridSpec(
        num_scalar_prefetch=2,
        grid=(n_rows // tm,),
        in_specs=[pl.BlockSpec((tm, quarter), lambda i, te, nu: (i, 0)),
                  pl.BlockSpec((tm, quarter), lambda i, te, nu: (i, 0)),
                  pl.BlockSpec((1, D, D_EXPERT), lambda i, te, nu: (te[i], 0, 0)),
                  pl.BlockSpec((1, D, D_EXPERT), lambda i, te, nu: (te[i], 0, 0)),
                  pl.BlockSpec((1, D_EXPERT, D), lambda i, te, nu: (te[i], 0, 0))],
        out_specs=[pl.BlockSpec((tm, quarter), lambda i, te, nu: (i, 0)),
                   pl.BlockSpec((tm, quarter), lambda i, te, nu: (i, 0))])
    return pl.pallas_call(
        _expert_kernel,
        grid_spec=grid_spec,
        out_shape=[jax.ShapeDtypeStruct((n_rows, quarter), jnp.uint32)] * 2,
        compiler_params=_cparams("arbitrary"),
        name="moe_experts",
    )(tile_expert, n_used, xa, xb, wg, wu, wd)


def _combine_kernel(x_ref, rt_ref, gf_ref, a1_ref, b1_ref, a2_ref, b2_ref, o_ref, *, final_norm):
    half = x_ref.shape[1] // 2
    h1, l1 = _unpack_bf16_pairs(jnp.concatenate([a1_ref[...], b1_ref[...]], axis=1))
    h2, l2 = _unpack_bf16_pairs(jnp.concatenate([a2_ref[...], b2_ref[...]], axis=1))
    w1 = rt_ref[:, RT_W1:RT_W1 + 1]
    w2 = rt_ref[:, RT_W2:RT_W2 + 1]
    out = jnp.concatenate([x_ref[:, 0:half] + w1 * h1 + w2 * h2, x_ref[:, half:] + w1 * l1 + w2 * l2], axis=1)
    if final_norm:
        out = _rms(out, gf_ref[...])
    o_ref[...] = out


def _combine(x, route, ga, gb, gain_final, final_norm, row_ranges=None):
    T, D = x.shape
    tm = ROUTE_TM
    n = T // tm
    quarter = D // 4
    outs = []
    for start, size in (row_ranges or ((0, T),)):
        lo = start // tm
        slot1 = pl.BlockSpec((tm, quarter), lambda i, lo=lo: (i + lo, 0))
        slot2 = pl.BlockSpec((tm, quarter), lambda i, lo=lo: (i + lo + n, 0))
        outs.append(pl.pallas_call(
            functools.partial(_combine_kernel, final_norm=final_norm),
            grid=(size // tm,),
            in_specs=[pl.BlockSpec((tm, D), lambda i, lo=lo: (i + lo, 0)),
                      pl.BlockSpec((tm, ROUTER_LANES), lambda i, lo=lo: (i + lo, 0)),
                      pl.BlockSpec((1, D), lambda i: (0, 0)),
                      slot1, slot1, slot2, slot2],
            out_specs=pl.BlockSpec((tm, D), lambda i: (i, 0)),
            out_shape=jax.ShapeDtypeStruct((size, D), f32),
            compiler_params=_cparams("parallel"),
            name="moe_combine",
        )(x, route, gain_final.reshape(1, D), ga, gb, ga, gb))
    return outs if row_ranges else outs[0]


def _moe(layer, x, hpa, hpb, route, route_t, counts, wg, wu, wd, gain_final, final_norm, row_ranges=None):
    (d0, d1), tile_expert, n_used = _moe_plan(route_t, counts)
    n_rows = _moe_rows(x.shape[0])
    xa = _sc_scatter_pair(hpa, d0, d1, n_rows)
    xb = _sc_scatter_pair(hpb, d0, d1, n_rows)
    ya, yb = _experts(tile_expert + layer * N_EXPERTS, n_used, xa, xb, wg, wu, wd)
    dcat = jnp.concatenate([d0, d1])
    return _combine(x, route, _sc_gather(ya, dcat), _sc_gather(yb, dcat), gain_final, final_norm, row_ranges)


LOG2E = 1.4426950408889634
SWA_LOCKSTEP = 2


def _swa_bias_tables():
    blk = C_BLOCK
    qi = np.arange(blk)[:, None]
    kj = np.arange(3 * blk)[None, :]
    dist = np.abs(blk + qi - kj)
    tabs = np.full((3, C_KV_HEADS, C_GROUP * blk, 3 * blk), -np.inf, np.float32)
    for case in range(3):
        ok = dist <= WINDOW
        if case == 0:
            ok = ok & (kj >= blk)
        if case == 2:
            ok = ok & (kj < 2 * blk)
        for hh in range(C_HEADS):
            slope = 2.0 ** (-8.0 * (hh + 1) / C_HEADS)
            t = np.where(ok, -slope * LOG2E * dist, -np.inf)
            kvh, g = divmod(hh, C_GROUP)
            tabs[case, kvh, g * blk:(g + 1) * blk] = t
    return jnp.asarray(tabs)


def _swa_kernel(sink_ref, q_ref, kp_ref, kc_ref, kn_ref, vp_ref, vc_ref, vn_ref, bias_ref, o_ref):
    blk = C_BLOCK
    hd = C_HEAD_DIM
    kcat = jnp.concatenate([kp_ref[0], kc_ref[0], kn_ref[0]], axis=0)
    vcat = jnp.concatenate([vp_ref[0], vc_ref[0], vn_ref[0]], axis=0)
    q = q_ref[0]
    rblk = lax.broadcasted_iota(jnp.int32, (C_GROUP * blk, 1), 0) // blk
    lane = lax.broadcasted_iota(jnp.int32, (3 * blk, LANE - hd), 1)
    ones_pad = jnp.where(lane == 0, 1.0, 0.0).astype(bf16)
    q4s, sinks = [], []
    for kvh in range(C_KV_HEADS):
        q4s.append(jnp.concatenate(
            [q[:, (kvh * C_GROUP + g) * hd:(kvh * C_GROUP + g + 1) * hd] for g in range(C_GROUP)], axis=0))
        sk = jnp.zeros((C_GROUP * blk, 1), f32)
        for g in range(C_GROUP):
            sk = jnp.where(rblk == g, sink_ref[kvh * C_GROUP + g] * LOG2E, sk)
        sinks.append(sk)
    outs = []
    for h0 in range(0, C_KV_HEADS, SWA_LOCKSTEP):
        hs = range(h0, h0 + SWA_LOCKSTEP)
        ss = [_dot_nt(q4s[h], kcat[:, h * hd:(h + 1) * hd]) + bias_ref[0, h] for h in hs]
        ms = [jnp.maximum(jnp.max(s, axis=-1, keepdims=True), sinks[h]) for s, h in zip(ss, hs)]
        ps = [jnp.exp2(s - m).astype(bf16) for s, m in zip(ss, ms)]
        pvs = [_dot(p, jnp.concatenate([vcat[:, h * hd:(h + 1) * hd], ones_pad], axis=1)) for p, h in zip(ps, hs)]
        for pv, m, h in zip(pvs, ms, hs):
            den = pv[:, hd:hd + 1] + jnp.exp2(sinks[h] - m)
            o4 = pv[:, 0:hd] / den
            outs += [o4[g * blk:(g + 1) * blk, :] for g in range(C_GROUP)]
    o_ref[0] = jnp.concatenate(outs, axis=1).astype(o_ref.dtype)


def _swa(qkv, sinks):
    B, L, _ = qkv.shape
    nb = L // C_BLOCK
    dq = C_HEADS * C_HEAD_DIM
    dkv = C_KV_HEADS * C_HEAD_DIM
    k_col = dq // dkv
    v_col = k_col + 1

    def kv_spec(col, shift):
        return pl.BlockSpec((1, C_BLOCK, dkv),
                            lambda n, b: (b, jnp.clip(n + shift, 0, nb - 1), col))

    def edge_case(n, b):
        return (jnp.where(n == 0, 0, jnp.where(n == nb - 1, 2, 1)), 0, 0, 0)

    return pl.pallas_call(
        _swa_kernel,
        grid=(nb, B),
        in_specs=[pl.BlockSpec(memory_space=pltpu.SMEM),
                  pl.BlockSpec((1, C_BLOCK, dq), lambda n, b: (b, n, 0)),
                  kv_spec(k_col, -1), kv_spec(k_col, 0), kv_spec(k_col, 1),
                  kv_spec(v_col, -1), kv_spec(v_col, 0), kv_spec(v_col, 1),
                  pl.BlockSpec((1, C_KV_HEADS, C_GROUP * C_BLOCK, 3 * C_BLOCK), edge_case)],
        out_specs=pl.BlockSpec((1, C_BLOCK, dq), lambda n, b: (b, n, 0)),
        out_shape=jax.ShapeDtypeStruct((B, L, dq), bf16),
        compiler_params=_cparams("parallel", "parallel"),
        name="swa",
    )(sinks, qkv, qkv, qkv, qkv, qkv, qkv, qkv, _swa_bias_tables())


def _hyena_filters(L, w1, b1, w2, b2, w3, b3, w4, freq):
    hp = lax.Precision.HIGHEST
    t = jnp.linspace(0.0, 1.0, L, dtype=f32)[:, None]
    wpos = (2.0 * math.pi / L) * jnp.arange(L, dtype=f32)[:, None]
    bands = jnp.linspace(1e-4, HY_BANDS - 1, HY_BANDS, dtype=f32)[None, :]
    feats = jnp.concatenate([t, jnp.cos(bands * wpos), -jnp.sin(bands * wpos)], axis=-1)
    h = jnp.sin(freq[0] * (jnp.dot(feats, w1, precision=hp) + b1))
    h = jnp.sin(freq[1] * (jnp.dot(h, w2, precision=hp) + b2))
    h = jnp.sin(freq[2] * (jnp.dot(h, w3, precision=hp) + b3))
    h = jnp.dot(h, w4, precision=hp).reshape(L, HY_ORDER, 2, B_WIDTH)
    deltas = jnp.abs(jnp.linspace(HY_MIN_DECAY, HY_MAX_DECAY, B_WIDTH, dtype=f32))
    window = jnp.exp(-t * deltas[None, :])
    return h * window[:, None, None, :]


def _pad_rows(w, rows=8):
    return jnp.pad(w, ((0, rows - w.shape[0]), (0, 0)))


def _even_in_weights(w_in):
    hk = A_HEADS * A_DK
    q = w_in[:, 0:hk].reshape(D_MODEL, A_HEADS, A_DK)
    k = w_in[:, hk:2 * hk].reshape(D_MODEL, A_HEADS, A_DK)
    v = w_in[:, 2 * hk:A_QKV].reshape(D_MODEL, A_HEADS, A_DV)
    z = w_in[:, A_QKV:A_QKV + A_WIDTH].reshape(D_MODEL, A_HEADS, A_DV)
    qkvz = jnp.concatenate([q, k, v, z], axis=-1).reshape(D_MODEL, A_HEADS * 4 * A_DK)
    ab = w_in[:, A_QKV + A_WIDTH:A_QKV + A_WIDTH + 4 * A_HEADS].reshape(D_MODEL, 4, A_HEADS)
    ab = jnp.transpose(ab, (0, 2, 1))
    ab = jnp.pad(ab, ((0, 0), (0, 0), (0, LANE - 4))).reshape(D_MODEL, A_HEADS * LANE)
    hx = w_in[:, A_QKV + A_WIDTH + 4 * A_HEADS:]
    return jnp.concatenate([qkvz, ab, hx], axis=1).astype(bf16)


def _router_weights(w_group, w_expert):
    w = jnp.concatenate([w_group, w_expert], axis=1)
    w = jnp.pad(w, ((0, 0), (0, ROUTER_LANES - w.shape[1])))
    hi = w.astype(bf16)
    lo = (w - hi.astype(f32)).astype(bf16)
    return jnp.concatenate([hi, lo], axis=1)


def kernel(x_prompt, x_sample, norm_mix, norm_ffn, norm_final, ev_w_in, ev_conv_a, ev_alog_f, ev_alog_b, ev_dtb_f, ev_dtb_b, ev_onorm, ev_conv_b, hy_w1, hy_b1, hy_w2, hy_b2, hy_w3, hy_b3, hy_w4, hy_freq, hy_dbias, ev_w_out, od_w_qkv, od_sinks, od_w_out, moe_w_group, moe_w_expert, moe_w_gate, moe_w_up, moe_w_down):
    L = x_prompt.shape[1]
    n_layers = moe_w_gate.shape[0]
    wg = moe_w_gate.reshape(n_layers * N_EXPERTS, D_MODEL, D_EXPERT)
    wu = moe_w_up.reshape(n_layers * N_EXPERTS, D_MODEL, D_EXPERT)
    wd = moe_w_down.reshape(n_layers * N_EXPERTS, D_EXPERT, D_MODEL)

    w_in = _even_in_weights(ev_w_in[0])
    n_qkvz = A_HEADS * 4 * A_DK
    n_ab = A_HEADS * LANE
    n_hx = 3 * B_WIDTH
    hk = A_HEADS * A_DK
    ca = ev_conv_a[0]
    conv_a = jnp.concatenate([ca[0:hk].reshape(A_HEADS, A_DK, 3), ca[hk:2 * hk].reshape(A_HEADS, A_DK, 3),
                              ca[2 * hk:].reshape(A_HEADS, A_DV, 3)], axis=1)
    conv_a = jnp.pad(jnp.transpose(conv_a, (0, 2, 1)), ((0, 0), (0, 5), (0, 0)))
    gate_sc = jnp.stack([ev_alog_f[0], ev_alog_b[0], ev_dtb_f[0], ev_dtb_b[0]], axis=1)
    filt = _hyena_filters(L, hy_w1[0], hy_b1[0], hy_w2[0], hy_b2[0], hy_w3[0], hy_b3[0], hy_w4[0], hy_freq[0])
    hsum = (filt[:, :, 0] + filt[:, :, 1]).reshape(L, HY_ORDER * B_WIDTH)
    hdiff = (filt[:, :, 0] - filt[:, :, 1]).reshape(L, HY_ORDER * B_WIDTH)
    kr, ki = _filter_spectrum(hsum, hdiff)
    fft_tables = _hyena_tables(L)
    conv_b = _pad_rows(ev_conv_b[0].T)
    dbias = _pad_rows(hy_dbias[0])
    w_out = ev_w_out[0].astype(bf16)
    routers = [_router_weights(moe_w_group[l], moe_w_expert[l]) for l in range(n_layers)]
    n_qkv = od_w_qkv.shape[-1]
    dq = C_HEADS * C_HEAD_DIM
    w_qkv = jnp.concatenate([od_w_qkv[0][:, :dq] * (LOG2E * C_HEAD_DIM ** -0.5), od_w_qkv[0][:, dq:]], axis=1)
    w_qkv = w_qkv.astype(bf16)
    w_att_out = od_w_out[0].astype(bf16)

    def trunk(xb):
        B = xb.shape[0]
        T = B * L
        x = xb.reshape(T, D_MODEL)
        qkvz, abh, hx = _norm_matmul((x,), norm_mix[0], w_in, (n_qkvz, n_ab, n_hx), (bf16, f32, bf16))
        o_a = _deltanet(qkvz.reshape(B, L, n_qkvz), abh.reshape(B, L, n_ab), conv_a, gate_sc, ev_onorm[0])
        o_b = _hyena_fft(hx.reshape(B, L, n_hx), conv_b, fft_tables, kr, ki, dbias)
        x, hpa, hpb, route, route_t, counts = _proj_route(
            (x,), (o_a.reshape(T, A_WIDTH), o_b.reshape(T, B_WIDTH)), (w_out[:A_WIDTH], w_out[A_WIDTH:]),
            norm_ffn[0], routers[0])
        x = _moe(0, x, hpa, hpb, route, route_t, counts, wg, wu, wd, norm_final, final_norm=False)
        (qkv,) = _norm_matmul((x,), norm_mix[1], w_qkv, (n_qkv,), (bf16,))
        o_c = _swa(qkv.reshape(B, L, n_qkv), od_sinks[0])
        x, hpa, hpb, route, route_t, counts = _proj_route(
            (x,), (o_c.reshape(T, C_HEADS * C_HEAD_DIM),), (w_att_out,), norm_ffn[1], routers[1])
        y = _moe(1, x, hpa, hpb, route, route_t, counts, wg, wu, wd, norm_final, final_norm=True)
        return y.reshape(B, L, D_MODEL)

    return (trunk(x_prompt), trunk(x_sample))
```

```python
import functools
import math

import jax
import jax.numpy as jnp
import numpy as np
from jax import lax
from jax.experimental import pallas as pl
from jax.experimental.pallas import tpu as pltpu
from jax.experimental.pallas import tpu_sc as plsc

f32 = jnp.float32
bf16 = jnp.bfloat16

EPS = 1e-6
D_MODEL = 1024

A_HEADS = 4
A_DK = 128
A_DV = 128
A_WIDTH = A_HEADS * A_DV
A_QKV = 2 * A_HEADS * A_DK + A_HEADS * A_DV
DELTA_CHUNK = 128
INV_BASE = 8
DELTA_GROUP = 8

B_WIDTH = D_MODEL - A_WIDTH
HY_ORDER = 2
HY_EMB = 33
HY_BANDS = (HY_EMB - 1) // 2
HY_TARGET = 1e-2
HY_MIN_DECAY = math.log(HY_TARGET) / 1.5
HY_MAX_DECAY = math.log(HY_TARGET) / 0.3
HY_CT = 256
HY_ROWS = 256

C_HEADS = 16
C_KV_HEADS = 4
C_HEAD_DIM = 64
C_GROUP = C_HEADS // C_KV_HEADS
WINDOW = 128
C_BLOCK = 128

N_GROUPS = 4
EXPERTS_PER_GROUP = 8
N_EXPERTS = N_GROUPS * EXPERTS_PER_GROUP
D_EXPERT = 256
ROUTER_LANES = 128

LANE = 128
VMEM_LIMIT = 56 * 1024 * 1024


def _cparams(*sem):
    return pltpu.CompilerParams(dimension_semantics=sem, vmem_limit_bytes=VMEM_LIMIT)


def _dot(a, b):
    return jnp.dot(a, b, preferred_element_type=f32)


def _dot_nt(a, b):
    return lax.dot_general(a, b, (((1,), (1,)), ((), ())), preferred_element_type=f32)


def _dot_tn(a, b):
    return lax.dot_general(a, b, (((0,), (0,)), ((), ())), preferred_element_type=f32)


def _sigmoid(x):
    return 1.0 / (1.0 + jnp.exp(-x))


def _rms(x, g):
    return x * lax.rsqrt(jnp.mean(x * x, axis=-1, keepdims=True) + EPS) * g


def _row_sources(xs, tm):
    tiles = [x.shape[0] // tm for x in xs]
    starts = [sum(tiles[:k]) for k in range(len(xs))]

    def spec(x, lo, n):
        return pl.BlockSpec((tm, x.shape[1]), lambda i: (jnp.clip(i - lo, 0, n - 1), 0))

    return [spec(x, lo, n) for x, lo, n in zip(xs, starts, tiles)], tuple(starts[1:])


def _select_rows(refs, starts, rows=slice(None)):
    i = pl.program_id(0)
    x = refs[0][rows, :]
    for ref, lo in zip(refs[1:], starts):
        x = jnp.where(i >= lo, ref[rows, :], x)
    return x


def _norm_matmul_kernel(*refs, starts):
    n_src = len(starts) + 1
    g_ref, w_ref = refs[n_src:n_src + 2]
    o_refs = refs[n_src + 2:]
    h = _rms(_select_rows(refs[:n_src], starts), g_ref[...]).astype(bf16)
    off = 0
    for o_ref in o_refs:
        n = o_ref.shape[-1]
        o_ref[...] = _dot(h, w_ref[:, off:off + n]).astype(o_ref.dtype)
        off += n


def _norm_matmul(xs, gain, w, splits, dtypes, tm=512):
    T = sum(x.shape[0] for x in xs)
    D = xs[0].shape[1]
    n_all = w.shape[1]
    x_specs, starts = _row_sources(xs, tm)
    return pl.pallas_call(
        functools.partial(_norm_matmul_kernel, starts=starts),
        grid=(T // tm,),
        in_specs=x_specs + [pl.BlockSpec((1, D), lambda i: (0, 0)),
                            pl.BlockSpec((D, n_all), lambda i: (0, 0))],
        out_specs=[pl.BlockSpec((tm, n), lambda i: (i, 0)) for n in splits],
        out_shape=[jax.ShapeDtypeStruct((T, n), dt) for n, dt in zip(splits, dtypes)],
        compiler_params=_cparams("parallel"),
        name="norm_matmul",
    )(*xs, gain.reshape(1, D), w)


MK_LOWER, MK_UPPER, MK_BASE, MK_LEVEL0 = 0, 1, 2, 3
N_LEVELS = int(math.log2(DELTA_CHUNK // INV_BASE))
MK_EYE = MK_LEVEL0 + N_LEVELS
N_MASKS = MK_EYE + 1


def _delta_masks():
    c = np.arange(DELTA_CHUNK)[:, None]
    s = np.arange(DELTA_CHUNK)[None, :]
    m = [s <= c, s >= c, (c // INV_BASE == s // INV_BASE) & (c != s)]
    b = INV_BASE
    while b < DELTA_CHUNK:
        m.append((c // (2 * b) == s // (2 * b)) & (c // b != s // b))
        b *= 2
    m.append(c == s)
    m = np.stack(m).astype(np.float32)
    return jnp.asarray(m), jnp.asarray(m, dtype=bf16)


def _tri_inverse(mats, mk_ref, mk16_ref):
    ps = [a * mk_ref[MK_BASE] for a in mats]
    ts = [mk_ref[MK_EYE] - p for p in ps]
    n = 2
    while n < INV_BASE:
        p16s = [p.astype(bf16) for p in ps]
        ps = [_dot(p16, p16) for p16 in p16s]
        ts = [t + _dot(t.astype(bf16), p.astype(bf16)) for t, p in zip(ts, ps)]
        n *= 2
    a16s = [a.astype(bf16) for a in mats]
    t16s = [t.astype(bf16) for t in ts]
    for lvl in range(N_LEVELS):
        xs = [_dot(a16 * mk16_ref[MK_LEVEL0 + lvl], t16) for a16, t16 in zip(a16s, t16s)]
        t16s = [t16 - _dot(t16, x.astype(bf16)).astype(bf16) for t16, x in zip(t16s, xs)]
    return t16s


def _deltanet_kernel(sc_ref, x_ref, ab_ref, cw_ref, on_ref, mk_ref, mk16_ref, o_ref, mp_s, nr_s, gl_s, o_s):
    hd = pl.program_id(1)
    L = x_ref.shape[1]
    C = DELTA_CHUNK
    nchunk = L // C
    group = math.gcd(DELTA_GROUP, nchunk)
    nqkv = 3 * A_DK
    halo = 16

    lane = lax.broadcasted_iota(jnp.int32, (1, LANE), 1)
    alog = jnp.where(lane == 2, sc_ref[hd, 0], jnp.where(lane == 3, sc_ref[hd, 1], 0.0))
    dtb = jnp.where(lane == 2, sc_ref[hd, 2], jnp.where(lane == 3, sc_ref[hd, 3], 0.0))
    neg_a = jnp.where((lane == 2) | (lane == 3), -jnp.exp(alog), 0.0)
    rowi = lax.broadcasted_iota(jnp.int32, (C, 1), 0)
    cw = cw_ref[0]

    def chunk_inputs(i):
        r0 = pl.multiple_of(i * C, C)
        xc = x_ref[0, pl.ds(r0, C), 0:nqkv].astype(f32)
        rp = pl.multiple_of(jnp.maximum(r0 - halo, 0), halo)
        rn = pl.multiple_of(jnp.minimum(r0 + C, L - halo), halo)
        prow = x_ref[0, pl.ds(rp, halo), 0:nqkv][halo - 1:halo, :].astype(f32)
        nrow = x_ref[0, pl.ds(rn, halo), 0:nqkv][0:1, :].astype(f32)
        prow = jnp.where(i > 0, prow, 0.0)
        nrow = jnp.where(i < nchunk - 1, nrow, 0.0)
        prev = jnp.where(rowi == 0, prow, pltpu.roll(xc, 1, 0))
        nxt = jnp.where(rowi == C - 1, nrow, pltpu.roll(xc, C - 1, 0))
        y = prev * cw[0:1, :] + xc * cw[1:2, :] + nxt * cw[2:3, :]
        y = y * _sigmoid(y)
        q = y[:, 0:A_DK]
        k = y[:, A_DK:2 * A_DK]
        vc = y[:, 2 * A_DK:nqkv]
        qc = q * lax.rsqrt(jnp.sum(q * q, axis=-1, keepdims=True) + EPS) * (A_DK ** -0.5)
        kc = k * lax.rsqrt(jnp.sum(k * k, axis=-1, keepdims=True) + EPS)

        abc = ab_ref[0, pl.ds(r0, C), :]
        zg = abc + dtb
        g = neg_a * (jnp.maximum(zg, 0.0) + jnp.log(1.0 + jnp.exp(-jnp.abs(zg))))
        low16 = mk16_ref[MK_LOWER]
        g1 = g.astype(bf16)
        e1 = g - g1.astype(f32)
        g2 = e1.astype(bf16)
        g3 = (e1 - g2.astype(f32)).astype(bf16)
        pre = _dot(low16, g1) + _dot(low16, g2) + _dot(low16, g3)
        tot = pre[C - 1:C, :]
        suf = tot - pre + g
        beta_all = _sigmoid(abc)

        k16 = kc.astype(bf16)
        kk = _dot_nt(k16, k16)
        qk = _dot_nt(qc.astype(bf16), k16)

        chains = []
        for rev in (0, 1):
            beta = beta_all[:, rev:rev + 1]
            gcc = (suf if rev else pre)[:, 2 + rev:3 + rev]
            tot11 = tot[:, 2 + rev:3 + rev]
            keep = mk_ref[MK_UPPER if rev else MK_LOWER]
            bc = jnp.broadcast_to(gcc * LOG2E, (C, C))
            decay = jnp.exp2((bc - bc.T) * keep) * keep
            a = (beta * kk) * decay
            eg = jnp.exp(gcc)
            chains.append(dict(
                i=i, rev=rev, a=a, gl=jnp.exp(tot11), qd=qc * eg,
                rhs=jnp.concatenate([vc * beta, kc * (beta * eg)], axis=1).astype(bf16),
                kd16=(kc * jnp.exp(tot11 - gcc)).astype(bf16),
                qk16=(qk * decay).astype(bf16)))
        return chains

    def prep(it, carry):
        chains = []
        for gi in range(group):
            chains += chunk_inputs(it * group + gi)
        t16s = _tri_inverse([c["a"] for c in chains], mk_ref, mk16_ref)
        uws = [_dot(t16, c["rhs"]).astype(bf16) for t16, c in zip(t16s, chains)]
        kts = [_dot_tn(c["kd16"], uw) for c, uw in zip(chains, uws)]
        prs = [_dot(c["qk16"], uw) for c, uw in zip(chains, uws)]
        for c, kt, pr in zip(chains, kts, prs):
            rev, i = c["rev"], c["i"]
            mp_s[rev, i, 0:A_DK, :] = (-kt[:, A_DV:]).astype(bf16)
            mp_s[rev, i, A_DK:A_DK + C, :] = (c["qd"] - pr[:, A_DV:]).astype(bf16)
            nr_s[rev, i, 0:A_DK, :] = kt[:, 0:A_DV]
            nr_s[rev, i, A_DK:A_DK + C, :] = pr[:, 0:A_DV]
            gl_s[rev, i] = jnp.broadcast_to(c["gl"], (8, LANE))
        return carry

    lax.fori_loop(0, nchunk // group, prep, 0)

    def scan(i, carry):
        out = []
        for rev, S in enumerate(carry):
            j = (nchunk - 1 - i) if rev else i
            y = _dot(mp_s[rev, j], S.astype(bf16))
            nr = nr_s[rev, j]
            o_s[rev, pl.ds(pl.multiple_of(j * C, C), C), :] = y[A_DK:, :] + nr[A_DK:, :]
            out.append(S * gl_s[rev, j][0:1, :] + y[0:A_DK, :] + nr[0:A_DK, :])
        return tuple(out)

    S0 = jnp.zeros((A_DK, A_DV), f32)
    lax.fori_loop(0, nchunk, scan, (S0, S0))

    def finish(it, carry):
        for gi in range(group):
            r0 = pl.multiple_of((it * group + gi) * C, C)
            o = o_s[0, pl.ds(r0, C), :] + o_s[1, pl.ds(r0, C), :]
            zg = x_ref[0, pl.ds(r0, C), nqkv:nqkv + A_DV].astype(f32)
            o_ref[0, pl.ds(r0, C), :] = (_rms(o, on_ref[...]) * (zg * _sigmoid(zg))).astype(o_ref.dtype)
        return carry

    lax.fori_loop(0, nchunk // group, finish, 0)


def _deltanet(qkvz, abh, conv_w, gate_sc, onorm):
    B, L, _ = qkvz.shape
    C = DELTA_CHUNK
    nchunk = L // C
    hw = 4 * A_DK
    scratch = [pltpu.VMEM((2, nchunk, A_DK + C, A_DV), bf16),
               pltpu.VMEM((2, nchunk, A_DK + C, A_DV), f32),
               pltpu.VMEM((2, nchunk, 8, LANE), f32),
               pltpu.VMEM((2, L, A_DV), f32)]
    return pl.pallas_call(
        _deltanet_kernel,
        grid=(B, A_HEADS),
        in_specs=[pl.BlockSpec(memory_space=pltpu.SMEM),
                  pl.BlockSpec((1, L, hw), lambda b, h: (b, 0, h)),
                  pl.BlockSpec((1, L, LANE), lambda b, h: (b, 0, h)),
                  pl.BlockSpec((1, 8, 3 * A_DK), lambda b, h: (h, 0, 0)),
                  pl.BlockSpec((1, A_DV), lambda b, h: (0, 0)),
                  pl.BlockSpec((N_MASKS, C, C), lambda b, h: (0, 0, 0)),
                  pl.BlockSpec((N_MASKS, C, C), lambda b, h: (0, 0, 0))],
        out_specs=pl.BlockSpec((1, L, A_DV), lambda b, h: (b, 0, h)),
        out_shape=jax.ShapeDtypeStruct((B, L, A_WIDTH), bf16),
        scratch_shapes=scratch,
        compiler_params=_cparams("parallel", "parallel"),
        name="deltanet",
    )(gate_sc, qkvz, abh, conv_w, onorm.reshape(1, A_DV), *_delta_masks())


def _filter_spectrum_kernel(c_ref, s_ref, hs_ref, hd_ref, kr_ref, ki_ref):
    L = c_ref.shape[1]
    rows = kr_ref.shape[1]
    scale = 1.0 / (2 * L)

    def split(x):
        hi = x.astype(bf16)
        return hi, (x - hi.astype(f32)).astype(bf16)

    hs_hi, hs_lo = split(hs_ref[...])
    hd_hi, hd_lo = split(hd_ref[...])
    for k1 in range(kr_ref.shape[0]):
        c = c_ref[k1 * rows:(k1 + 1) * rows, :]
        s = s_ref[k1 * rows:(k1 + 1) * rows, :]
        kr_ref[k1] = (_dot(c, hs_hi) + _dot(c, hs_lo)) * scale
        ki_ref[k1] = (_dot(s, hd_hi) + _dot(s, hd_lo)) * (-scale)


def _filter_spectrum(hsum, hdiff):
    L, n = hsum.shape
    ct = HY_CT
    n_fft = 2 * L
    n2 = L // FFT_Z1
    d = jnp.arange(L, dtype=jnp.int32)[None, :]
    a1 = ((jnp.arange(FFT_K1, dtype=jnp.int32)[:, None] * d) % n_fft).astype(f32) * (2.0 * math.pi / n_fft)
    a2 = ((jnp.arange(n2, dtype=jnp.int32)[:, None] * d) % n2).astype(f32) * (2.0 * math.pi / n2)
    c1, s1 = jnp.cos(a1)[:, None, :], jnp.sin(a1)[:, None, :]
    c2, s2 = jnp.cos(a2)[None, :, :], jnp.sin(a2)[None, :, :]
    cmat = (c1 * c2 - s1 * s2).reshape(FFT_K1 * n2, L).astype(bf16)
    smat = (s1 * c2 + c1 * s2).reshape(FFT_K1 * n2, L).astype(bf16)
    once = pl.Buffered(1)
    return pl.pallas_call(
        _filter_spectrum_kernel,
        grid=(n // ct,),
        in_specs=[pl.BlockSpec((FFT_K1 * n2, L), lambda j: (0, 0), pipeline_mode=once),
                  pl.BlockSpec((FFT_K1 * n2, L), lambda j: (0, 0), pipeline_mode=once),
                  pl.BlockSpec((L, ct), lambda j: (0, j)),
                  pl.BlockSpec((L, ct), lambda j: (0, j))],
        out_specs=[pl.BlockSpec((FFT_K1, n2, ct), lambda j: (0, 0, j)),
                   pl.BlockSpec((FFT_K1, n2, ct), lambda j: (0, 0, j))],
        out_shape=[jax.ShapeDtypeStruct((FFT_K1, n2, n), f32),
                   jax.ShapeDtypeStruct((FFT_K1, n2, n), f32)],
        compiler_params=_cparams("parallel"),
        name="filter_spectrum",
    )(cmat, smat, hsum, hdiff)


FFT_N1 = 16
FFT_K1 = FFT_N1 // 2 + 1
FFT_Z1 = FFT_N1 // 2
FFT_ROWS = 32


def _lincomb(terms):
    groups = {}
    for c, a in terms:
        if abs(c) > 1e-9:
            groups.setdefault(round(abs(c), 9), []).append((c > 0, a))
    total = None
    for mag, items in groups.items():
        pos = [a for p, a in items if p]
        neg = [a for p, a in items if not p]
        s = None
        for a in pos:
            s = a if s is None else s + a
        for a in neg:
            s = -a if s is None else s - a
        if mag != 1.0:
            s = s * mag
        total = s if total is None else total + s
    return total


def _hyena_fft_kernel(m2f_ref, m2i_ref, twr_ref, twi_ref, x1_ref, x2_ref, v_ref, w1_ref, w2_ref, wv_ref,
                      kr0_ref, ki0_ref, kr1_ref, ki1_ref, db_ref, o_ref, z_s, g_s, d_s, y_s, b_s):
    L, ct = z_s.shape
    n2 = L // FFT_Z1
    R = min(HY_ROWS, L)
    nt = L // R
    r1 = min(FFT_ROWS, n2)
    halo = 16
    rowi = lax.broadcasted_iota(jnp.int32, (R, 1), 0)
    ang = 2.0 * math.pi / FFT_N1
    c16 = [[math.cos(ang * a * b) for b in range(FFT_K1)] for a in range(FFT_Z1)]
    s16 = [[math.sin(ang * a * b) for b in range(FFT_K1)] for a in range(FFT_Z1)]

    def conv_into(dst, x_ref, w_ref):
        def body(t, carry):
            r0 = pl.multiple_of(t * R, R)
            x = x_ref[0, pl.ds(r0, R), :].astype(f32)
            rp = pl.multiple_of(jnp.maximum(r0 - halo, 0), halo)
            rn = pl.multiple_of(jnp.minimum(r0 + R, L - halo), halo)
            prow = jnp.where(t > 0, x_ref[0, pl.ds(rp, halo), :][halo - 1:halo, :].astype(f32), 0.0)
            nrow = jnp.where(t < nt - 1, x_ref[0, pl.ds(rn, halo), :][0:1, :].astype(f32), 0.0)
            prev = jnp.where(rowi == 0, prow, pltpu.roll(x, 1, 0))
            nxt = jnp.where(rowi == R - 1, nrow, pltpu.roll(x, R - 1, 0))
            w = w_ref[...]
            dst[pl.ds(r0, R), :] = prev * w[0:1, :] + x * w[1:2, :] + nxt * w[2:3, :]
            return carry
        lax.fori_loop(0, nt, body, 0)

    def cols(k1):
        return slice(k1 * ct, (k1 + 1) * ct)

    conv_into(z_s, v_ref, wv_ref)
    stages = ((x1_ref, w1_ref, kr0_ref, ki0_ref), (x2_ref, w2_ref, kr1_ref, ki1_ref))
    for o, (g_ref, gw_ref, kr_ref, ki_ref) in enumerate(stages):
        conv_into(g_s, g_ref, gw_ref)

        def forward1(rt, carry):
            r0 = pl.multiple_of(rt * r1, r1)
            zs = [z_s[pl.ds(a * n2 + r0, r1), :] for a in range(FFT_Z1)]
            for k1 in range(FFT_K1):
                ar = _lincomb([(c16[a][k1], zs[a]) for a in range(FFT_Z1)])
                ai = _lincomb([(-s16[a][k1], zs[a]) for a in range(FFT_Z1)])
                twr = twr_ref[pl.ds(r0, r1), cols(k1)]
                twi = twi_ref[pl.ds(r0, r1), cols(k1)]
                if ai is None:
                    pr, pi = ar * twr, -(ar * twi)
                else:
                    pr, pi = ar * twr + ai * twi, ai * twr - ar * twi
                d_s[pl.ds(r0, r1), cols(k1)] = pr.astype(bf16)
                d_s[pl.ds(n2 + r0, r1), cols(k1)] = pi.astype(bf16)
            return carry

        lax.fori_loop(0, n2 // r1, forward1, 0)

        for k1 in range(FFT_K1):
            xk = _dot(m2f_ref[...], d_s[:, cols(k1)])
            xr, xi = xk[0:n2, :], xk[n2:, :]
            kr, ki = kr_ref[k1], ki_ref[k1]
            y_s[0:n2, cols(k1)] = (xr * kr - xi * ki).astype(bf16)
            y_s[n2:2 * n2, cols(k1)] = (xr * ki + xi * kr).astype(bf16)

        for k1 in range(FFT_K1):
            bk = _dot(m2i_ref[...], y_s[:, cols(k1)])
            br, bi = bk[0:n2, :], bk[n2:, :]
            twr, twi = twr_ref[:, cols(k1)], twi_ref[:, cols(k1)]
            b_s[0:n2, cols(k1)] = br * twr - bi * twi
            b_s[n2:2 * n2, cols(k1)] = br * twi + bi * twr

        def inverse1(rt, carry):
            r0 = pl.multiple_of(rt * r1, r1)
            brs = [b_s[pl.ds(r0, r1), cols(j)] for j in range(FFT_K1)]
            bis = [b_s[pl.ds(n2 + r0, r1), cols(j)] for j in range(FFT_K1)]
            for a in range(FFT_Z1):
                terms = [(1.0, brs[0]), (-1.0 if a % 2 else 1.0, brs[FFT_K1 - 1])]
                for j in range(1, FFT_K1 - 1):
                    terms += [(2.0 * c16[a][j], brs[j]), (-2.0 * s16[a][j], bis[j])]
                rows = pl.ds(a * n2 + r0, r1)
                z = g_s[rows, :] * (_lincomb(terms) + z_s[rows, :] * db_ref[o:o + 1, :])
                if o + 1 < len(stages):
                    z_s[rows, :] = z
                else:
                    o_ref[0, rows, :] = z.astype(o_ref.dtype)
            return carry

        lax.fori_loop(0, n2 // r1, inverse1, 0)


def _hyena_tables(L):
    n_fft = 2 * L
    n2 = L // FFT_Z1
    ct = HY_CT
    idx = jnp.arange(n2, dtype=jnp.int32)
    a2 = ((idx[:, None] * idx[None, :]) % n2).astype(f32) * (2.0 * math.pi / n2)
    c2, s2 = jnp.cos(a2), jnp.sin(a2)
    m2f = jnp.block([[c2, s2], [-s2, c2]]).astype(bf16)
    m2i = jnp.block([[c2, -s2], [s2, c2]]).astype(bf16)
    at = (idx[:, None] * jnp.arange(FFT_K1, dtype=jnp.int32)[None, :]).astype(f32) * (2.0 * math.pi / n_fft)
    twr = jnp.broadcast_to(jnp.cos(at)[:, :, None], (n2, FFT_K1, ct)).reshape(n2, FFT_K1 * ct)
    twi = jnp.broadcast_to(jnp.sin(at)[:, :, None], (n2, FFT_K1, ct)).reshape(n2, FFT_K1 * ct)
    return m2f, m2i, twr, twi


def _hyena_fft(hx, conv_w, tables, kr, ki, dbias):
    B, L, _ = hx.shape
    ct = HY_CT
    nct = B_WIDTH // ct
    n2 = L // FFT_Z1
    wide = FFT_K1 * ct
    once = pl.Buffered(1)
    m2f, m2i, twr, twi = tables

    def xspec(part):
        return pl.BlockSpec((1, L, ct), lambda c, b: (b, 0, part * nct + c))

    def wspec(part):
        return pl.BlockSpec((8, ct), lambda c, b: (0, part * nct + c))

    def kspec(order):
        return pl.BlockSpec((FFT_K1, n2, ct), lambda c, b: (0, 0, order * nct + c), pipeline_mode=once)

    def const(shape):
        return pl.BlockSpec(shape, lambda c, b: (0, 0), pipeline_mode=once)

    return pl.pallas_call(
        _hyena_fft_kernel,
        grid=(nct, B),
        in_specs=[const((2 * n2, 2 * n2)), const((2 * n2, 2 * n2)), const((n2, wide)), const((n2, wide)),
                  xspec(0), xspec(1), xspec(2), wspec(0), wspec(1), wspec(2),
                  kspec(0), kspec(0), kspec(1), kspec(1),
                  pl.BlockSpec((8, ct), lambda c, b: (0, c))],
        out_specs=pl.BlockSpec((1, L, ct), lambda c, b: (b, 0, c)),
        out_shape=jax.ShapeDtypeStruct((B, L, B_WIDTH), bf16),
        scratch_shapes=[pltpu.VMEM((L, ct), f32), pltpu.VMEM((L, ct), f32),
                        pltpu.VMEM((2 * n2, wide), bf16), pltpu.VMEM((2 * n2, wide), bf16),
                        pltpu.VMEM((2 * n2, wide), f32)],
        compiler_params=_cparams("parallel", "parallel"),
        name="hyena_fft",
    )(m2f, m2i, twr, twi, hx, hx, hx, conv_w, conv_w, conv_w, kr, ki, kr, ki, dbias)


RT_E1, RT_E2, RT_W1, RT_W2, RT_R1, RT_R2 = 0, 1, 2, 3, 4, 5
ROUTE_TM = 512
PROJ_PARTS = 2
EXPERT_PARTS = 1
EXPERT_TM = 512
SC_WINDOW = 128


def _pack_bf16_pairs(x):
    n = x.shape[1] // 2
    bits = lax.bitcast_convert_type(x.astype(bf16).astype(f32), jnp.uint32)
    return bits[:, :n] | (bits[:, n:] >> 16)


def _unpack_bf16_pairs(p):
    hi = lax.bitcast_convert_type(p & jnp.uint32(0xFFFF0000), f32)
    lo = lax.bitcast_convert_type(p << 16, f32)
    return hi, lo


def _route_topk(logits):
    lane = lax.broadcasted_iota(jnp.int32, logits.shape, 1)
    neg = -jnp.inf
    big = ROUTER_LANES
    gl = jnp.where(lane < N_GROUPS, logits, neg)
    gmax = jnp.max(gl, axis=-1, keepdims=True)
    gsel = jnp.min(jnp.where(gl == gmax, lane, big), axis=-1, keepdims=True)
    psel = 1.0 / jnp.sum(jnp.exp(gl - gmax), axis=-1, keepdims=True)
    first = N_GROUPS + gsel * EXPERTS_PER_GROUP
    el = jnp.where((lane >= first) & (lane < first + EXPERTS_PER_GROUP), logits, neg)
    t1 = jnp.max(el, axis=-1, keepdims=True)
    i1 = jnp.min(jnp.where(el == t1, lane, big), axis=-1, keepdims=True)
    el2 = jnp.where(lane == i1, neg, el)
    t2 = jnp.max(el2, axis=-1, keepdims=True)
    i2 = jnp.min(jnp.where(el2 == t2, lane, big), axis=-1, keepdims=True)
    ex = jnp.exp(t2 - t1)
    w1 = psel / (1.0 + ex)
    w2 = ex * w1
    e1 = i1 - N_GROUPS
    e2 = i2 - N_GROUPS
    onehot = jnp.where((lane == e1) | (lane == e2), 1.0, 0.0)
    return e1, e2, w1, w2, onehot


def _route_record(e1, e2, w1, w2, before):
    lane = lax.broadcasted_iota(jnp.int32, before.shape, 1)
    r1 = jnp.sum(jnp.where(lane == e1, before, 0.0), axis=-1, keepdims=True)
    r2 = jnp.sum(jnp.where(lane == e2, before, 0.0), axis=-1, keepdims=True)
    rec = jnp.zeros(before.shape, f32)
    for ln, val in ((RT_E1, e1.astype(f32)), (RT_E2, e2.astype(f32)), (RT_W1, w1), (RT_W2, w2), (RT_R1, r1), (RT_R2, r2)):
        rec = jnp.where(lane == ln, val, rec)
    return rec


def _proj_route_kernel(*refs, n_in, starts):
    n_src = len(starts) + 1
    src_refs = refs[:n_src]
    refs = refs[n_src:]
    a_refs = refs[0:n_in]
    w_refs = refs[n_in:2 * n_in]
    g_ref, wr_ref, low_ref, xo_ref, hpa_ref, hpb_ref, rt_ref, rtt_ref, cnt_ref = refs[2 * n_in:]
    tm = xo_ref.shape[0]
    rp = tm // PROJ_PARTS
    parts = [slice(p * rp, (p + 1) * rp) for p in range(PROJ_PARTS)]
    xs = []
    for r in parts:
        x = _select_rows(src_refs, starts, r)
        for a_ref, w_ref in zip(a_refs, w_refs):
            x = x + _dot(a_ref[r, :], w_ref[...])
        xo_ref[r, :] = x
        xs.append(x)
    hs = [_rms(x, g_ref[...]) for x in xs]
    for r, h in zip(parts, hs):
        hp = _pack_bf16_pairs(h)
        quarter = hp.shape[1] // 2
        hpa_ref[r, :] = hp[:, :quarter]
        hpb_ref[r, :] = hp[:, quarter:]
    tops = []
    for h in hs:
        h_hi = h.astype(bf16)
        h_lo = (h - h_hi.astype(f32)).astype(bf16)
        both = _dot(h_hi, wr_ref[...])
        logits = both[:, :ROUTER_LANES] + both[:, ROUTER_LANES:] + _dot(h_lo, wr_ref[:, :ROUTER_LANES])
        tops.append(_route_topk(logits))
    onehot16 = jnp.concatenate([t[4] for t in tops], axis=0).astype(bf16)
    for r, (e1, e2, w1, w2, onehot) in zip(parts, tops):
        before = _dot(low_ref[r, :], onehot16)
        rec = _route_record(e1, e2, w1, w2, before)
        rt_ref[r, :] = rec
        rtt_ref[:, r] = rec.T[0:rtt_ref.shape[0], :]
    cnt_ref[0] = jnp.broadcast_to(before[rp - 1:rp, :] + onehot[rp - 1:rp, :], cnt_ref.shape[1:])


def _proj_route(xs, acts, ws, gain, w_router):
    T = sum(x.shape[0] for x in xs)
    D = xs[0].shape[1]
    tm = ROUTE_TM
    n_in = len(acts)
    low = jnp.asarray(np.tril(np.ones((tm, tm), np.float32), -1), dtype=bf16)
    in_specs, starts = _row_sources(xs, tm)
    in_specs += [pl.BlockSpec((tm, a.shape[1]), lambda i: (i, 0)) for a in acts]
    in_specs += [pl.BlockSpec(w.shape, lambda i: (0, 0)) for w in ws]
    in_specs += [pl.BlockSpec((1, D), lambda i: (0, 0)),
                 pl.BlockSpec((D, 2 * ROUTER_LANES), lambda i: (0, 0)),
                 pl.BlockSpec((tm, tm), lambda i: (0, 0))]
    return pl.pallas_call(
        functools.partial(_proj_route_kernel, n_in=n_in, starts=starts),
        grid=(T // tm,),
        in_specs=in_specs,
        out_specs=[pl.BlockSpec((tm, D), lambda i: (i, 0)),
                   pl.BlockSpec((tm, D // 4), lambda i: (i, 0)),
                   pl.BlockSpec((tm, D // 4), lambda i: (i, 0)),
                   pl.BlockSpec((tm, ROUTER_LANES), lambda i: (i, 0)),
                   pl.BlockSpec((8, tm), lambda i: (0, i)),
                   pl.BlockSpec((1, 8, ROUTER_LANES), lambda i: (i, 0, 0))],
        out_shape=[jax.ShapeDtypeStruct((T, D), f32),
                   jax.ShapeDtypeStruct((T, D // 4), jnp.uint32),
                   jax.ShapeDtypeStruct((T, D // 4), jnp.uint32),
                   jax.ShapeDtypeStruct((T, ROUTER_LANES), f32),
                   jax.ShapeDtypeStruct((8, T), f32),
                   jax.ShapeDtypeStruct((T // tm, 8, ROUTER_LANES), f32)],
        compiler_params=_cparams("parallel"),
        name="proj_route",
    )(*xs, *acts, *ws, gain.reshape(1, D), w_router, low)


def _moe_rows(T):
    return 2 * T + N_EXPERTS * EXPERT_TM


def _moe_plan(route_t, counts):
    T = route_t.shape[1]
    tm = ROUTE_TM
    cnt = counts[:, 0, :N_EXPERTS].astype(jnp.int32)
    total = jnp.sum(cnt, axis=0)
    padded = ((total + EXPERT_TM - 1) // EXPERT_TM) * EXPERT_TM
    ends = jnp.cumsum(padded)
    tile_base = (ends - padded)[None, :] + jnp.cumsum(cnt, axis=0) - cnt
    base = jnp.broadcast_to(tile_base.T[:, :, None], (N_EXPERTS, T // tm, tm)).reshape(N_EXPERTS, T)
    ids = jnp.arange(N_EXPERTS, dtype=jnp.int32)[:, None]
    dest = []
    for e_row, r_row in ((RT_E1, RT_R1), (RT_E2, RT_R2)):
        e = route_t[e_row].astype(jnp.int32)
        dest.append(jnp.sum(jnp.where(ids == e[None, :], base, 0), axis=0) + route_t[r_row].astype(jnp.int32))
    n_tiles = _moe_rows(T) // EXPERT_TM
    n_used = ends[-1] // EXPERT_TM
    starts = jnp.arange(n_tiles, dtype=jnp.int32) * EXPERT_TM
    tile_expert = jnp.sum((starts[:, None] >= ends[None, :]).astype(jnp.int32), axis=1)
    last_expert = jnp.sum((jnp.maximum(n_used - 1, 0) * EXPERT_TM >= ends).astype(jnp.int32))
    tile_expert = jnp.minimum(tile_expert, last_expert)
    return dest, tile_expert, n_used.reshape(1).astype(jnp.int32)


def _sc_mesh():
    return plsc.VectorSubcoreMesh(core_axis_name="core", subcore_axis_name="subcore")


def _sc_scatter_pair(rows, d0, d1, n_out):
    T, W = rows.shape
    win = SC_WINDOW

    @pl.kernel(out_type=jax.ShapeDtypeStruct((n_out, W), rows.dtype), mesh=_sc_mesh(), name="moe_sc_scatter")
    def scatter(x_hbm, i0_hbm, i1_hbm, o_hbm):
        def body(x_vmem, i0_vmem, i1_vmem):
            pltpu.sync_copy(x_vmem, o_hbm.at[i0_vmem.at[0]])
            pltpu.sync_copy(x_vmem, o_hbm.at[i1_vmem.at[0]])

        pltpu.emit_pipeline(
            body, grid=(T // win,),
            in_specs=[pl.BlockSpec((win, W), index_map=lambda i: (i, 0)),
                      pl.BlockSpec((1, win), index_map=lambda i: (0, i)),
                      pl.BlockSpec((1, win), index_map=lambda i: (0, i))],
            out_specs=[],
            core_axis_name=("core", "subcore"), dimension_semantics=(pltpu.PARALLEL,),
        )(x_hbm, i0_hbm, i1_hbm)

    return scatter(rows, d0.reshape(1, T), d1.reshape(1, T))


def _sc_gather(table, idx):
    n = idx.shape[0]
    W = table.shape[1]
    win = SC_WINDOW

    @pl.kernel(out_type=jax.ShapeDtypeStruct((n, W), table.dtype), mesh=_sc_mesh(), name="moe_sc_gather")
    def gather(t_hbm, i_hbm, o_hbm):
        def body(i_vmem, o_vmem):
            pltpu.sync_copy(t_hbm.at[i_vmem.at[0]], o_vmem)

        pltpu.emit_pipeline(
            body, grid=(n // win,),
            in_specs=[pl.BlockSpec((1, win), index_map=lambda i: (0, i))],
            out_specs=[pl.BlockSpec((win, W), index_map=lambda i: (i, 0))],
            core_axis_name=("core", "subcore"), dimension_semantics=(pltpu.PARALLEL,),
        )(i_hbm, o_hbm)

    return gather(table, idx.reshape(1, n))


def _expert_kernel(te_ref, nu_ref, xa_ref, xb_ref, wg_ref, wu_ref, wd_ref, ya_ref, yb_ref):
    @pl.when(pl.program_id(0) < nu_ref[0])
    def _():
        wg = wg_ref[0].astype(bf16)
        wu = wu_ref[0].astype(bf16)
        wd = wd_ref[0].astype(bf16)
        rp = xa_ref.shape[0] // EXPERT_PARTS
        parts = [slice(p * rp, (p + 1) * rp) for p in range(EXPERT_PARTS)]
        xs = []
        for r in parts:
            hi, lo = _unpack_bf16_pairs(jnp.concatenate([xa_ref[r, :], xb_ref[r, :]], axis=1))
            xs.append(jnp.concatenate([hi, lo], axis=1).astype(bf16))
        gates = [_dot(x, wg) for x in xs]
        ups = [_dot(x, wu) for x in xs]
        hgs = [(g * _sigmoid(g) * u).astype(bf16) for g, u in zip(gates, ups)]
        ys = [_dot(hg, wd) for hg in hgs]
        for r, y in zip(parts, ys):
            y = _pack_bf16_pairs(y)
            quarter = y.shape[1] // 2
            ya_ref[r, :] = y[:, :quarter]
            yb_ref[r, :] = y[:, quarter:]

    @pl.when(pl.program_id(0) >= nu_ref[0])
    def _():
        ya_ref[...] = jnp.zeros(ya_ref.shape, ya_ref.dtype)
        yb_ref[...] = jnp.zeros(yb_ref.shape, yb_ref.dtype)


def _experts(tile_expert, n_used, xa, xb, wg, wu, wd):
    n_rows, quarter = xa.shape
    D = 4 * quarter
    tm = EXPERT_TM
    grid_spec = pltpu.PrefetchScalarGridSpec(
        num_scalar_prefetch=2,
        grid=(n_rows // tm,),
        in_specs=[pl.BlockSpec((tm, quarter), lambda i, te, nu: (i, 0)),
                  pl.BlockSpec((tm, quarter), lambda i, te, nu: (i, 0)),
                  pl.BlockSpec((1, D, D_EXPERT), lambda i, te, nu: (te[i], 0, 0)),
                  pl.BlockSpec((1, D, D_EXPERT), lambda i, te, nu: (te[i], 0, 0)),
                  pl.BlockSpec((1, D_EXPERT, D), lambda i, te, nu: (te[i], 0, 0))],
        out_specs=[pl.BlockSpec((tm, quarter), lambda i, te, nu: (i, 0)),
                   pl.BlockSpec((tm, quarter), lambda i, te, nu: (i, 0))])
    return pl.pallas_call(
        _expert_kernel,
        grid_spec=grid_spec,
        out_shape=[jax.ShapeDtypeStruct((n_rows, quarter), jnp.uint32)] * 2,
        compiler_params=_cparams("arbitrary"),
        name="moe_experts",
    )(tile_expert, n_used, xa, xb, wg, wu, wd)


def _combine_kernel(x_ref, rt_ref, gf_ref, a1_ref, b1_ref, a2_ref, b2_ref, o_ref, *, final_norm):
    half = x_ref.shape[1] // 2
    h1, l1 = _unpack_bf16_pairs(jnp.concatenate([a1_ref[...], b1_ref[...]], axis=1))
    h2, l2 = _unpack_bf16_pairs(jnp.concatenate([a2_ref[...], b2_ref[...]], axis=1))
    w1 = rt_ref[:, RT_W1:RT_W1 + 1]
    w2 = rt_ref[:, RT_W2:RT_W2 + 1]
    out = jnp.concatenate([x_ref[:, 0:half] + w1 * h1 + w2 * h2, x_ref[:, half:] + w1 * l1 + w2 * l2], axis=1)
    if final_norm:
        out = _rms(out, gf_ref[...])
    o_ref[...] = out


def _combine(x, route, ga, gb, gain_final, final_norm, row_ranges=None):
    T, D = x.shape
    tm = ROUTE_TM
    n = T // tm
    quarter = D // 4
    outs = []
    for start, size in (row_ranges or ((0, T),)):
        lo = start // tm
        slot1 = pl.BlockSpec((tm, quarter), lambda i, lo=lo: (i + lo, 0))
        slot2 = pl.BlockSpec((tm, quarter), lambda i, lo=lo: (i + lo + n, 0))
        outs.append(pl.pallas_call(
            functools.partial(_combine_kernel, final_norm=final_norm),
            grid=(size // tm,),
            in_specs=[pl.BlockSpec((tm, D), lambda i, lo=lo: (i + lo, 0)),
                      pl.BlockSpec((tm, ROUTER_LANES), lambda i, lo=lo: (i + lo, 0)),
                      pl.BlockSpec((1, D), lambda i: (0, 0)),
                      slot1, slot1, slot2, slot2],
            out_specs=pl.BlockSpec((tm, D), lambda i: (i, 0)),
            out_shape=jax.ShapeDtypeStruct((size, D), f32),
            compiler_params=_cparams("parallel"),
            name="moe_combine",
        )(x, route, gain_final.reshape(1, D), ga, gb, ga, gb))
    return outs if row_ranges else outs[0]


def _moe(layer, x, hpa, hpb, route, route_t, counts, wg, wu, wd, gain_final, final_norm, row_ranges=None):
    (d0, d1), tile_expert, n_used = _moe_plan(route_t, counts)
    n_rows = _moe_rows(x.shape[0])
    xa = _sc_scatter_pair(hpa, d0, d1, n_rows)
    xb = _sc_scatter_pair(hpb, d0, d1, n_rows)
    ya, yb = _experts(tile_expert + layer * N_EXPERTS, n_used, xa, xb, wg, wu, wd)
    dcat = jnp.concatenate([d0, d1])
    return _combine(x, route, _sc_gather(ya, dcat), _sc_gather(yb, dcat), gain_final, final_norm, row_ranges)


LOG2E = 1.4426950408889634
SWA_LOCKSTEP = 2


def _swa_bias_tables():
    blk = C_BLOCK
    qi = np.arange(blk)[:, None]
    kj = np.arange(3 * blk)[None, :]
    dist = np.abs(blk + qi - kj)
    tabs = np.full((3, C_KV_HEADS, C_GROUP * blk, 3 * blk), -np.inf, np.float32)
    for case in range(3):
        ok = dist <= WINDOW
        if case == 0:
            ok = ok & (kj >= blk)
        if case == 2:
            ok = ok & (kj < 2 * blk)
        for hh in range(C_HEADS):
            slope = 2.0 ** (-8.0 * (hh + 1) / C_HEADS)
            t = np.where(ok, -slope * LOG2E * dist, -np.inf)
            kvh, g = divmod(hh, C_GROUP)
            tabs[case, kvh, g * blk:(g + 1) * blk] = t
    return jnp.asarray(tabs)


def _swa_kernel(sink_ref, q_ref, kp_ref, kc_ref, kn_ref, vp_ref, vc_ref, vn_ref, bias_ref, o_ref):
    blk = C_BLOCK
    hd = C_HEAD_DIM
    kcat = jnp.concatenate([kp_ref[0], kc_ref[0], kn_ref[0]], axis=0)
    vcat = jnp.concatenate([vp_ref[0], vc_ref[0], vn_ref[0]], axis=0)
    q = q_ref[0]
    rblk = lax.broadcasted_iota(jnp.int32, (C_GROUP * blk, 1), 0) // blk
    lane = lax.broadcasted_iota(jnp.int32, (3 * blk, LANE - hd), 1)
    ones_pad = jnp.where(lane == 0, 1.0, 0.0).astype(bf16)
    q4s, sinks = [], []
    for kvh in range(C_KV_HEADS):
        q4s.append(jnp.concatenate(
            [q[:, (kvh * C_GROUP + g) * hd:(kvh * C_GROUP + g + 1) * hd] for g in range(C_GROUP)], axis=0))
        sk = jnp.zeros((C_GROUP * blk, 1), f32)
        for g in range(C_GROUP):
            sk = jnp.where(rblk == g, sink_ref[kvh * C_GROUP + g] * LOG2E, sk)
        sinks.append(sk)
    outs = []
    for h0 in range(0, C_KV_HEADS, SWA_LOCKSTEP):
        hs = range(h0, h0 + SWA_LOCKSTEP)
        ss = [_dot_nt(q4s[h], kcat[:, h * hd:(h + 1) * hd]) + bias_ref[0, h] for h in hs]
        ms = [jnp.maximum(jnp.max(s, axis=-1, keepdims=True), sinks[h]) for s, h in zip(ss, hs)]
        ps = [jnp.exp2(s - m).astype(bf16) for s, m in zip(ss, ms)]
        pvs = [_dot(p, jnp.concatenate([vcat[:, h * hd:(h + 1) * hd], ones_pad], axis=1)) for p, h in zip(ps, hs)]
        for pv, m, h in zip(pvs, ms, hs):
            den = pv[:, hd:hd + 1] + jnp.exp2(sinks[h] - m)
            o4 = pv[:, 0:hd] / den
            outs += [o4[g * blk:(g + 1) * blk, :] for g in range(C_GROUP)]
    o_ref[0] = jnp.concatenate(outs, axis=1).astype(o_ref.dtype)


def _swa(qkv, sinks):
    B, L, _ = qkv.shape
    nb = L // C_BLOCK
    dq = C_HEADS * C_HEAD_DIM
    dkv = C_KV_HEADS * C_HEAD_DIM
    k_col = dq // dkv
    v_col = k_col + 1

    def kv_spec(col, shift):
        return pl.BlockSpec((1, C_BLOCK, dkv),
                            lambda n, b: (b, jnp.clip(n + shift, 0, nb - 1), col))

    def edge_case(n, b):
        return (jnp.where(n == 0, 0, jnp.where(n == nb - 1, 2, 1)), 0, 0, 0)

    return pl.pallas_call(
        _swa_kernel,
        grid=(nb, B),
        in_specs=[pl.BlockSpec(memory_space=pltpu.SMEM),
                  pl.BlockSpec((1, C_BLOCK, dq), lambda n, b: (b, n, 0)),
                  kv_spec(k_col, -1), kv_spec(k_col, 0), kv_spec(k_col, 1),
                  kv_spec(v_col, -1), kv_spec(v_col, 0), kv_spec(v_col, 1),
                  pl.BlockSpec((1, C_KV_HEADS, C_GROUP * C_BLOCK, 3 * C_BLOCK), edge_case)],
        out_specs=pl.BlockSpec((1, C_BLOCK, dq), lambda n, b: (b, n, 0)),
        out_shape=jax.ShapeDtypeStruct((B, L, dq), bf16),
        compiler_params=_cparams("parallel", "parallel"),
        name="swa",
    )(sinks, qkv, qkv, qkv, qkv, qkv, qkv, qkv, _swa_bias_tables())


def _hyena_filters(L, w1, b1, w2, b2, w3, b3, w4, freq):
    hp = lax.Precision.HIGHEST
    t = jnp.linspace(0.0, 1.0, L, dtype=f32)[:, None]
    wpos = (2.0 * math.pi / L) * jnp.arange(L, dtype=f32)[:, None]
    bands = jnp.linspace(1e-4, HY_BANDS - 1, HY_BANDS, dtype=f32)[None, :]
    feats = jnp.concatenate([t, jnp.cos(bands * wpos), -jnp.sin(bands * wpos)], axis=-1)
    h = jnp.sin(freq[0] * (jnp.dot(feats, w1, precision=hp) + b1))
    h = jnp.sin(freq[1] * (jnp.dot(h, w2, precision=hp) + b2))
    h = jnp.sin(freq[2] * (jnp.dot(h, w3, precision=hp) + b3))
    h = jnp.dot(h, w4, precision=hp).reshape(L, HY_ORDER, 2, B_WIDTH)
    deltas = jnp.abs(jnp.linspace(HY_MIN_DECAY, HY_MAX_DECAY, B_WIDTH, dtype=f32))
    window = jnp.exp(-t * deltas[None, :])
    return h * window[:, None, None, :]


def _pad_rows(w, rows=8):
    return jnp.pad(w, ((0, rows - w.shape[0]), (0, 0)))


def _even_in_weights(w_in):
    hk = A_HEADS * A_DK
    q = w_in[:, 0:hk].reshape(D_MODEL, A_HEADS, A_DK)
    k = w_in[:, hk:2 * hk].reshape(D_MODEL, A_HEADS, A_DK)
    v = w_in[:, 2 * hk:A_QKV].reshape(D_MODEL, A_HEADS, A_DV)
    z = w_in[:, A_QKV:A_QKV + A_WIDTH].reshape(D_MODEL, A_HEADS, A_DV)
    qkvz = jnp.concatenate([q, k, v, z], axis=-1).reshape(D_MODEL, A_HEADS * 4 * A_DK)
    ab = w_in[:, A_QKV + A_WIDTH:A_QKV + A_WIDTH + 4 * A_HEADS].reshape(D_MODEL, 4, A_HEADS)
    ab = jnp.transpose(ab, (0, 2, 1))
    ab = jnp.pad(ab, ((0, 0), (0, 0), (0, LANE - 4))).reshape(D_MODEL, A_HEADS * LANE)
    hx = w_in[:, A_QKV + A_WIDTH + 4 * A_HEADS:]
    return jnp.concatenate([qkvz, ab, hx], axis=1).astype(bf16)


def _router_weights(w_group, w_expert):
    w = jnp.concatenate([w_group, w_expert], axis=1)
    w = jnp.pad(w, ((0, 0), (0, ROUTER_LANES - w.shape[1])))
    hi = w.astype(bf16)
    lo = (w - hi.astype(f32)).astype(bf16)
    return jnp.concatenate([hi, lo], axis=1)


def kernel(x_prompt, x_sample, norm_mix, norm_ffn, norm_final, ev_w_in, ev_conv_a, ev_alog_f, ev_alog_b, ev_dtb_f, ev_dtb_b, ev_onorm, ev_conv_b, hy_w1, hy_b1, hy_w2, hy_b2, hy_w3, hy_b3, hy_w4, hy_freq, hy_dbias, ev_w_out, od_w_qkv, od_sinks, od_w_out, moe_w_group, moe_w_expert, moe_w_gate, moe_w_up, moe_w_down):
    L = x_prompt.shape[1]
    n_layers = moe_w_gate.shape[0]
    wg = moe_w_gate.reshape(n_layers * N_EXPERTS, D_MODEL, D_EXPERT)
    wu = moe_w_up.reshape(n_layers * N_EXPERTS, D_MODEL, D_EXPERT)
    wd = moe_w_down.reshape(n_layers * N_EXPERTS, D_EXPERT, D_MODEL)

    w_in = _even_in_weights(ev_w_in[0])
    n_qkvz = A_HEADS * 4 * A_DK
    n_ab = A_HEADS * LANE
    n_hx = 3 * B_WIDTH
    hk = A_HEADS * A_DK
    ca = ev_conv_a[0]
    conv_a = jnp.concatenate([ca[0:hk].reshape(A_HEADS, A_DK, 3), ca[hk:2 * hk].reshape(A_HEADS, A_DK, 3),
                              ca[2 * hk:].reshape(A_HEADS, A_DV, 3)], axis=1)
    conv_a = jnp.pad(jnp.transpose(conv_a, (0, 2, 1)), ((0, 0), (0, 5), (0, 0)))
    gate_sc = jnp.stack([ev_alog_f[0], ev_alog_b[0], ev_dtb_f[0], ev_dtb_b[0]], axis=1)
    filt = _hyena_filters(L, hy_w1[0], hy_b1[0], hy_w2[0], hy_b2[0], hy_w3[0], hy_b3[0], hy_w4[0], hy_freq[0])
    hsum = (filt[:, :, 0] + filt[:, :, 1]).reshape(L, HY_ORDER * B_WIDTH)
    hdiff = (filt[:, :, 0] - filt[:, :, 1]).reshape(L, HY_ORDER * B_WIDTH)
    kr, ki = _filter_spectrum(hsum, hdiff)
    fft_tables = _hyena_tables(L)
    conv_b = _pad_rows(ev_conv_b[0].T)
    dbias = _pad_rows(hy_dbias[0])
    w_out = ev_w_out[0].astype(bf16)
    routers = [_router_weights(moe_w_group[l], moe_w_expert[l]) for l in range(n_layers)]
    n_qkv = od_w_qkv.shape[-1]
    dq = C_HEADS * C_HEAD_DIM
    w_qkv = jnp.concatenate([od_w_qkv[0][:, :dq] * (LOG2E * C_HEAD_DIM ** -0.5), od_w_qkv[0][:, dq:]], axis=1)
    w_qkv = w_qkv.astype(bf16)
    w_att_out = od_w_out[0].astype(bf16)

    def trunk(xb):
        B = xb.shape[0]
        T = B * L
        x = xb.reshape(T, D_MODEL)
        qkvz, abh, hx = _norm_matmul((x,), norm_mix[0], w_in, (n_qkvz, n_ab, n_hx), (bf16, f32, bf16))
        o_a = _deltanet(qkvz.reshape(B, L, n_qkvz), abh.reshape(B, L, n_ab), conv_a, gate_sc, ev_onorm[0])
        o_b = _hyena_fft(hx.reshape(B, L, n_hx), conv_b, fft_tables, kr, ki, dbias)
        x, hpa, hpb, route, route_t, counts = _proj_route(
            (x,), (o_a.reshape(T, A_WIDTH), o_b.reshape(T, B_WIDTH)), (w_out[:A_WIDTH], w_out[A_WIDTH:]),
            norm_ffn[0], routers[0])
        x = _moe(0, x, hpa, hpb, route, route_t, counts, wg, wu, wd, norm_final, final_norm=False)
        (qkv,) = _norm_matmul((x,), norm_mix[1], w_qkv, (n_qkv,), (bf16,))
        o_c = _swa(qkv.reshape(B, L, n_qkv), od_sinks[0])
        x, hpa, hpb, route, route_t, counts = _proj_route(
            (x,), (o_c.reshape(T, C_HEADS * C_HEAD_DIM),), (w_att_out,), norm_ffn[1], routers[1])
        y = _moe(1, x, hpa, hpb, route, route_t, counts, wg, wu, wd, norm_final, final_norm=True)
        return y.reshape(B, L, D_MODEL)

    return (trunk(x_prompt), trunk(x_sample))
```

```python
import functools
import math

import jax
import jax.numpy as jnp
import numpy as np
from jax import lax
from jax.experimental import pallas as pl
from jax.experimental.pallas import tpu as pltpu
from jax.experimental.pallas import tpu_sc as plsc

f32 = jnp.float32
bf16 = jnp.bfloat16

EPS = 1e-6
D_MODEL = 1024

A_HEADS = 4
A_DK = 128
A_DV = 128
A_WIDTH = A_HEADS * A_DV
A_QKV = 2 * A_HEADS * A_DK + A_HEADS * A_DV
DELTA_CHUNK = 128
INV_BASE = 8
DELTA_GROUP = 8

B_WIDTH = D_MODEL - A_WIDTH
HY_ORDER = 2
HY_EMB = 33
HY_BANDS = (HY_EMB - 1) // 2
HY_TARGET = 1e-2
HY_MIN_DECAY = math.log(HY_TARGET) / 1.5
HY_MAX_DECAY = math.log(HY_TARGET) / 0.3
HY_CT = 256
HY_ROWS = 256

C_HEADS = 16
C_KV_HEADS = 4
C_HEAD_DIM = 64
C_GROUP = C_HEADS // C_KV_HEADS
WINDOW = 128
C_BLOCK = 128

N_GROUPS = 4
EXPERTS_PER_GROUP = 8
N_EXPERTS = N_GROUPS * EXPERTS_PER_GROUP
D_EXPERT = 256
ROUTER_LANES = 128

LANE = 128
HALO = 8
VMEM_LIMIT = 56 * 1024 * 1024


def _cparams(*sem):
    return pltpu.CompilerParams(dimension_semantics=sem, vmem_limit_bytes=VMEM_LIMIT)


def _dot(a, b):
    return jnp.dot(a, b, preferred_element_type=f32)


def _dot_nt(a, b):
    return lax.dot_general(a, b, (((1,), (1,)), ((), ())), preferred_element_type=f32)


def _dot_tn(a, b):
    return lax.dot_general(a, b, (((0,), (0,)), ((), ())), preferred_element_type=f32)


def _sigmoid(x):
    return 1.0 / (1.0 + jnp.exp(-x))


def _rms(x, g):
    return x * lax.rsqrt(jnp.mean(x * x, axis=-1, keepdims=True) + EPS) * g


def _row_sources(xs, tm):
    tiles = [x.shape[0] // tm for x in xs]
    starts = [sum(tiles[:k]) for k in range(len(xs))]

    def spec(x, lo, n):
        return pl.BlockSpec((tm, x.shape[1]), lambda i: (jnp.clip(i - lo, 0, n - 1), 0))

    return [spec(x, lo, n) for x, lo, n in zip(xs, starts, tiles)], tuple(starts[1:])


def _select_rows(refs, starts, rows=slice(None)):
    i = pl.program_id(0)
    x = refs[0][rows, :]
    for ref, lo in zip(refs[1:], starts):
        x = jnp.where(i >= lo, ref[rows, :], x)
    return x


def _norm_matmul_kernel(*refs, starts):
    n_src = len(starts) + 1
    g_ref, w_ref = refs[n_src:n_src + 2]
    o_refs = refs[n_src + 2:]
    h = _rms(_select_rows(refs[:n_src], starts), g_ref[...]).astype(bf16)
    off = 0
    for o_ref in o_refs:
        n = o_ref.shape[-1]
        o_ref[...] = _dot(h, w_ref[:, off:off + n]).astype(o_ref.dtype)
        off += n


def _norm_matmul(xs, gain, w, splits, dtypes, tm=512):
    T = sum(x.shape[0] for x in xs)
    D = xs[0].shape[1]
    n_all = w.shape[1]
    x_specs, starts = _row_sources(xs, tm)
    return pl.pallas_call(
        functools.partial(_norm_matmul_kernel, starts=starts),
        grid=(T // tm,),
        in_specs=x_specs + [pl.BlockSpec((1, D), lambda i: (0, 0)),
                            pl.BlockSpec((D, n_all), lambda i: (0, 0))],
        out_specs=[pl.BlockSpec((tm, n), lambda i: (i, 0)) for n in splits],
        out_shape=[jax.ShapeDtypeStruct((T, n), dt) for n, dt in zip(splits, dtypes)],
        compiler_params=_cparams("parallel"),
        name="norm_matmul",
    )(*xs, gain.reshape(1, D), w)


def _softplus(z):
    return jnp.maximum(z, 0.0) + jnp.log(1.0 + jnp.exp(-jnp.abs(z)))


def _even_in_kernel(x_ref, xp_ref, xn_ref, g_ref, w_ref, cwa_ref, cwb_ref, alog_ref, dtb_ref,
                    qkvz_ref, gates_ref, hx_ref, *, tiles_per_seq):
    i = pl.program_id(0)
    tm = x_ref.shape[0]
    rows = tm + 2 * HALO
    first = (i % tiles_per_seq) == 0
    last = (i % tiles_per_seq) == tiles_per_seq - 1
    gain = g_ref[...]
    h = _rms(x_ref[...], gain).astype(bf16)
    hcat = jnp.concatenate([_rms(xp_ref[...], gain).astype(bf16), h, _rms(xn_ref[...], gain).astype(bf16)], axis=0)
    rowi = lax.broadcasted_iota(jnp.int32, (rows, 1), 0)
    keep = jnp.where(((rowi < HALO) & first) | ((rowi >= tm + HALO) & last), 0.0, 1.0)

    hw = 4 * A_DK
    n_qkvz = A_HEADS * hw
    n_ab = A_HEADS * LANE
    n_hx = hx_ref.shape[1]
    part = n_hx // 3
    col_groups = [(hd * hw, hw) for hd in range(A_HEADS)] + [(n_qkvz + n_ab + c * part, part) for c in range(3)]

    def project(j):
        col0, width = col_groups[j]
        return _dot(hcat, w_ref[:, col0:col0 + width])

    def conv(p, cw):
        p = p * keep
        y = pltpu.roll(p, 1, 0) * cw[0:1, :] + p * cw[1:2, :] + pltpu.roll(p, rows - 1, 0) * cw[2:3, :]
        return y[HALO:HALO + tm, :]

    p_next = project(0)
    for hd in range(A_HEADS):
        p_cur, p_next = p_next, project(hd + 1)
        y = conv(p_cur, cwa_ref[:, hd * hw:(hd + 1) * hw])
        qkv = y[:, 0:3 * A_DK]
        qkv = qkv * _sigmoid(qkv)
        q = qkv[:, 0:A_DK]
        k = qkv[:, A_DK:2 * A_DK]
        q = q * lax.rsqrt(jnp.sum(q * q, axis=-1, keepdims=True) + EPS) * (A_DK ** -0.5)
        k = k * lax.rsqrt(jnp.sum(k * k, axis=-1, keepdims=True) + EPS)
        qkvz_ref[:, hd * hw:(hd + 1) * hw] = jnp.concatenate(
            [q, k, qkv[:, 2 * A_DK:], y[:, 3 * A_DK:]], axis=1).astype(qkvz_ref.dtype)

    for c in range(3):
        p_cur = p_next
        if c + 1 < 3:
            p_next = project(A_HEADS + c + 1)
        hx_ref[:, c * part:(c + 1) * part] = conv(p_cur, cwb_ref[:, c * part:(c + 1) * part]).astype(hx_ref.dtype)

    ab = _dot(h, w_ref[:, n_qkvz:n_qkvz + n_ab])
    lane = lax.broadcasted_iota(jnp.int32, (1, n_ab), 1) % LANE
    g = -jnp.exp(alog_ref[...]) * _softplus(ab + dtb_ref[...])
    gates_ref[...] = jnp.where(lane < 2, _sigmoid(ab), jnp.where(lane < 4, g, 0.0))


def _even_in_proj(x, seq_len, gain, w, conv_a, conv_b, alog, dtb, tm=512):
    T, D = x.shape
    assert seq_len % tm == 0, "a row tile must not straddle two sequences"
    n_qkvz = A_HEADS * 4 * A_DK
    n_ab = A_HEADS * LANE
    n_hx = 3 * B_WIDTH
    per_tile = tm // HALO
    last_block = T // HALO - 1
    return pl.pallas_call(
        functools.partial(_even_in_kernel, tiles_per_seq=seq_len // tm),
        grid=(T // tm,),
        in_specs=[pl.BlockSpec((tm, D), lambda i: (i, 0)),
                  pl.BlockSpec((HALO, D), lambda i: (jnp.maximum(i * per_tile - 1, 0), 0)),
                  pl.BlockSpec((HALO, D), lambda i: (jnp.minimum((i + 1) * per_tile, last_block), 0)),
                  pl.BlockSpec((1, D), lambda i: (0, 0)),
                  pl.BlockSpec(w.shape, lambda i: (0, 0)),
                  pl.BlockSpec(conv_a.shape, lambda i: (0, 0)),
                  pl.BlockSpec(conv_b.shape, lambda i: (0, 0)),
                  pl.BlockSpec((1, n_ab), lambda i: (0, 0)),
                  pl.BlockSpec((1, n_ab), lambda i: (0, 0))],
        out_specs=[pl.BlockSpec((tm, n_qkvz), lambda i: (i, 0)),
                   pl.BlockSpec((tm, n_ab), lambda i: (i, 0)),
                   pl.BlockSpec((tm, n_hx), lambda i: (i, 0))],
        out_shape=[jax.ShapeDtypeStruct((T, n_qkvz), bf16),
                   jax.ShapeDtypeStruct((T, n_ab), f32),
                   jax.ShapeDtypeStruct((T, n_hx), bf16)],
        compiler_params=_cparams("parallel"),
        name="even_in_proj",
    )(x, x, x, gain.reshape(1, D), w, conv_a, conv_b, alog, dtb)


MK_LOWER, MK_UPPER, MK_BASE, MK_LEVEL0 = 0, 1, 2, 3
N_LEVELS = int(math.log2(DELTA_CHUNK // INV_BASE))
MK_EYE = MK_LEVEL0 + N_LEVELS
N_MASKS = MK_EYE + 1


def _delta_masks():
    c = np.arange(DELTA_CHUNK)[:, None]
    s = np.arange(DELTA_CHUNK)[None, :]
    m = [s <= c, s >= c, (c // INV_BASE == s // INV_BASE) & (c != s)]
    b = INV_BASE
    while b < DELTA_CHUNK:
        m.append((c // (2 * b) == s // (2 * b)) & (c // b != s // b))
        b *= 2
    m.append(c == s)
    m = np.stack(m).astype(np.float32)
    return jnp.asarray(m), jnp.asarray(m, dtype=bf16)


def _tri_inverse(mats, mk_ref, mk16_ref):
    ps = [a * mk_ref[MK_BASE] for a in mats]
    ts = [mk_ref[MK_EYE] - p for p in ps]
    n = 2
    while n < INV_BASE:
        p16s = [p.astype(bf16) for p in ps]
        ps = [_dot(p16, p16) for p16 in p16s]
        ts = [t + _dot(t.astype(bf16), p.astype(bf16)) for t, p in zip(ts, ps)]
        n *= 2
    a16s = [a.astype(bf16) for a in mats]
    t16s = [t.astype(bf16) for t in ts]
    for lvl in range(N_LEVELS):
        xs = [_dot(a16 * mk16_ref[MK_LEVEL0 + lvl], t16) for a16, t16 in zip(a16s, t16s)]
        t16s = [t16 - _dot(t16, x.astype(bf16)).astype(bf16) for t16, x in zip(t16s, xs)]
    return t16s


def _deltanet_kernel(x_ref, gt_ref, on_ref, mk_ref, mk16_ref, o_ref, mp_s, nr_s, gl_s, o_s):
    L = x_ref.shape[1]
    C = DELTA_CHUNK
    nchunk = L // C
    group = math.gcd(DELTA_GROUP, nchunk)
    nqkv = 3 * A_DK
    lane = lax.broadcasted_iota(jnp.int32, (1, LANE), 1)

    def chunk_inputs(i):
        r0 = pl.multiple_of(i * C, C)
        qc = x_ref[0, pl.ds(r0, C), 0:A_DK].astype(f32)
        kc = x_ref[0, pl.ds(r0, C), A_DK:2 * A_DK].astype(f32)
        vc = x_ref[0, pl.ds(r0, C), 2 * A_DK:nqkv].astype(f32)
        beta_all = gt_ref[0, pl.ds(r0, C), :]
        g = jnp.where((lane == 2) | (lane == 3), beta_all, 0.0)
        low16 = mk16_ref[MK_LOWER]
        g1 = g.astype(bf16)
        e1 = g - g1.astype(f32)
        g2 = e1.astype(bf16)
        g3 = (e1 - g2.astype(f32)).astype(bf16)
        pre = _dot(low16, g1) + _dot(low16, g2) + _dot(low16, g3)
        tot = pre[C - 1:C, :]
        suf = tot - pre + g

        k16 = kc.astype(bf16)
        kk = _dot_nt(k16, k16)
        qk = _dot_nt(qc.astype(bf16), k16)

        chains = []
        for rev in (0, 1):
            beta = beta_all[:, rev:rev + 1]
            gcc = (suf if rev else pre)[:, 2 + rev:3 + rev]
            tot11 = tot[:, 2 + rev:3 + rev]
            keep = mk_ref[MK_UPPER if rev else MK_LOWER]
            bc = jnp.broadcast_to(gcc * LOG2E, (C, C))
            decay = jnp.exp2((bc - bc.T) * keep) * keep
            a = (beta * kk) * decay
            eg = jnp.exp(gcc)
            chains.append(dict(
                i=i, rev=rev, a=a, gl=jnp.exp(tot11), qd=qc * eg,
                rhs=jnp.concatenate([vc * beta, kc * (beta * eg)], axis=1).astype(bf16),
                kd16=(kc * jnp.exp(tot11 - gcc)).astype(bf16),
                qk16=(qk * decay).astype(bf16)))
        return chains

    def prep(it, carry):
        chains = []
        for gi in range(group):
            chains += chunk_inputs(it * group + gi)
        t16s = _tri_inverse([c["a"] for c in chains], mk_ref, mk16_ref)
        uws = [_dot(t16, c["rhs"]).astype(bf16) for t16, c in zip(t16s, chains)]
        kts = [_dot_tn(c["kd16"], uw) for c, uw in zip(chains, uws)]
        prs = [_dot(c["qk16"], uw) for c, uw in zip(chains, uws)]
        for c, kt, pr in zip(chains, kts, prs):
            rev, i = c["rev"], c["i"]
            mp_s[rev, i, 0:A_DK, :] = (-kt[:, A_DV:]).astype(bf16)
            mp_s[rev, i, A_DK:A_DK + C, :] = (c["qd"] - pr[:, A_DV:]).astype(bf16)
            nr_s[rev, i, 0:A_DK, :] = kt[:, 0:A_DV]
            nr_s[rev, i, A_DK:A_DK + C, :] = pr[:, 0:A_DV]
            gl_s[rev, i] = jnp.broadcast_to(c["gl"], (8, LANE))
        return carry

    lax.fori_loop(0, nchunk // group, prep, 0)

    def scan(i, carry):
        out = []
        for rev, S in enumerate(carry):
            j = (nchunk - 1 - i) if rev else i
            y = _dot(mp_s[rev, j], S.astype(bf16))
            nr = nr_s[rev, j]
            o_s[rev, pl.ds(pl.multiple_of(j * C, C), C), :] = y[A_DK:, :] + nr[A_DK:, :]
            out.append(S * gl_s[rev, j][0:1, :] + y[0:A_DK, :] + nr[0:A_DK, :])
        return tuple(out)

    S0 = jnp.zeros((A_DK, A_DV), f32)
    lax.fori_loop(0, nchunk, scan, (S0, S0))

    def finish(it, carry):
        for gi in range(group):
            r0 = pl.multiple_of((it * group + gi) * C, C)
            o = o_s[0, pl.ds(r0, C), :] + o_s[1, pl.ds(r0, C), :]
            zg = x_ref[0, pl.ds(r0, C), nqkv:nqkv + A_DV].astype(f32)
            o_ref[0, pl.ds(r0, C), :] = (_rms(o, on_ref[...]) * (zg * _sigmoid(zg))).astype(o_ref.dtype)
        return carry

    lax.fori_loop(0, nchunk // group, finish, 0)


def _deltanet(qkvz, gates, onorm):
    B, L, _ = qkvz.shape
    C = DELTA_CHUNK
    nchunk = L // C
    hw = 4 * A_DK
    scratch = [pltpu.VMEM((2, nchunk, A_DK + C, A_DV), bf16),
               pltpu.VMEM((2, nchunk, A_DK + C, A_DV), f32),
               pltpu.VMEM((2, nchunk, 8, LANE), f32),
               pltpu.VMEM((2, L, A_DV), f32)]
    return pl.pallas_call(
        _deltanet_kernel,
        grid=(B, A_HEADS),
        in_specs=[pl.BlockSpec((1, L, hw), lambda b, h: (b, 0, h)),
                  pl.BlockSpec((1, L, LANE), lambda b, h: (b, 0, h)),
                  pl.BlockSpec((1, A_DV), lambda b, h: (0, 0)),
                  pl.BlockSpec((N_MASKS, C, C), lambda b, h: (0, 0, 0)),
                  pl.BlockSpec((N_MASKS, C, C), lambda b, h: (0, 0, 0))],
        out_specs=pl.BlockSpec((1, L, A_DV), lambda b, h: (b, 0, h)),
        out_shape=jax.ShapeDtypeStruct((B, L, A_WIDTH), bf16),
        scratch_shapes=scratch,
        compiler_params=_cparams("parallel", "parallel"),
        name="deltanet",
    )(qkvz, gates, onorm.reshape(1, A_DV), *_delta_masks())


def _filter_spectrum_kernel(c_ref, s_ref, hs_ref, hd_ref, kr_ref, ki_ref):
    L = c_ref.shape[1]
    rows = kr_ref.shape[1]
    scale = 1.0 / (2 * L)

    def split(x):
        hi = x.astype(bf16)
        return hi, (x - hi.astype(f32)).astype(bf16)

    hs_hi, hs_lo = split(hs_ref[...])
    hd_hi, hd_lo = split(hd_ref[...])
    for k1 in range(kr_ref.shape[0]):
        c = c_ref[k1 * rows:(k1 + 1) * rows, :]
        s = s_ref[k1 * rows:(k1 + 1) * rows, :]
        kr_ref[k1] = (_dot(c, hs_hi) + _dot(c, hs_lo)) * scale
        ki_ref[k1] = (_dot(s, hd_hi) + _dot(s, hd_lo)) * (-scale)


def _filter_spectrum(hsum, hdiff):
    L, n = hsum.shape
    ct = HY_CT
    n_fft = 2 * L
    n2 = L // FFT_Z1
    d = jnp.arange(L, dtype=jnp.int32)[None, :]
    a1 = ((jnp.arange(FFT_K1, dtype=jnp.int32)[:, None] * d) % n_fft).astype(f32) * (2.0 * math.pi / n_fft)
    a2 = ((jnp.arange(n2, dtype=jnp.int32)[:, None] * d) % n2).astype(f32) * (2.0 * math.pi / n2)
    c1, s1 = jnp.cos(a1)[:, None, :], jnp.sin(a1)[:, None, :]
    c2, s2 = jnp.cos(a2)[None, :, :], jnp.sin(a2)[None, :, :]
    cmat = (c1 * c2 - s1 * s2).reshape(FFT_K1 * n2, L).astype(bf16)
    smat = (s1 * c2 + c1 * s2).reshape(FFT_K1 * n2, L).astype(bf16)
    once = pl.Buffered(1)
    return pl.pallas_call(
        _filter_spectrum_kernel,
        grid=(n // ct,),
        in_specs=[pl.BlockSpec((FFT_K1 * n2, L), lambda j: (0, 0), pipeline_mode=once),
                  pl.BlockSpec((FFT_K1 * n2, L), lambda j: (0, 0), pipeline_mode=once),
                  pl.BlockSpec((L, ct), lambda j: (0, j)),
                  pl.BlockSpec((L, ct), lambda j: (0, j))],
        out_specs=[pl.BlockSpec((FFT_K1, n2, ct), lambda j: (0, 0, j)),
                   pl.BlockSpec((FFT_K1, n2, ct), lambda j: (0, 0, j))],
        out_shape=[jax.ShapeDtypeStruct((FFT_K1, n2, n), f32),
                   jax.ShapeDtypeStruct((FFT_K1, n2, n), f32)],
        compiler_params=_cparams("parallel"),
        name="filter_spectrum",
    )(cmat, smat, hsum, hdiff)


FFT_N1 = 16
FFT_K1 = FFT_N1 // 2 + 1
FFT_Z1 = FFT_N1 // 2
FFT_ROWS = 32


def _lincomb(terms):
    groups = {}
    for c, a in terms:
        if abs(c) > 1e-9:
            groups.setdefault(round(abs(c), 9), []).append((c > 0, a))
    total = None
    for mag, items in groups.items():
        pos = [a for p, a in items if p]
        neg = [a for p, a in items if not p]
        s = None
        for a in pos:
            s = a if s is None else s + a
        for a in neg:
            s = -a if s is None else s - a
        if mag != 1.0:
            s = s * mag
        total = s if total is None else total + s
    return total


def _hyena_fft_kernel(m2f_ref, m2i_ref, twr_ref, twi_ref, x1_ref, x2_ref, v_ref,
                      kr0_ref, ki0_ref, kr1_ref, ki1_ref, db_ref, o_ref, z_s, d_s, y_s, b_s):
    L, ct = z_s.shape
    n2 = L // FFT_Z1
    R = min(HY_ROWS, L)
    r1 = min(FFT_ROWS, n2)
    ang = 2.0 * math.pi / FFT_N1
    c16 = [[math.cos(ang * a * b) for b in range(FFT_K1)] for a in range(FFT_Z1)]
    s16 = [[math.sin(ang * a * b) for b in range(FFT_K1)] for a in range(FFT_Z1)]

    def cols(k1):
        return slice(k1 * ct, (k1 + 1) * ct)

    def load_v(t, carry):
        r0 = pl.multiple_of(t * R, R)
        z_s[pl.ds(r0, R), :] = v_ref[0, pl.ds(r0, R), :].astype(f32)
        return carry

    lax.fori_loop(0, L // R, load_v, 0)
    stages = ((x1_ref, kr0_ref, ki0_ref), (x2_ref, kr1_ref, ki1_ref))
    for o, (g_ref, kr_ref, ki_ref) in enumerate(stages):

        def forward1(rt, carry):
            r0 = pl.multiple_of(rt * r1, r1)
            zs = [z_s[pl.ds(a * n2 + r0, r1), :] for a in range(FFT_Z1)]
            for k1 in range(FFT_K1):
                ar = _lincomb([(c16[a][k1], zs[a]) for a in range(FFT_Z1)])
                ai = _lincomb([(-s16[a][k1], zs[a]) for a in range(FFT_Z1)])
                twr = twr_ref[pl.ds(r0, r1), cols(k1)]
                twi = twi_ref[pl.ds(r0, r1), cols(k1)]
                if ai is None:
                    pr, pi = ar * twr, -(ar * twi)
                else:
                    pr, pi = ar * twr + ai * twi, ai * twr - ar * twi
                d_s[pl.ds(r0, r1), cols(k1)] = pr.astype(bf16)
                d_s[pl.ds(n2 + r0, r1), cols(k1)] = pi.astype(bf16)
            return carry

        lax.fori_loop(0, n2 // r1, forward1, 0)

        for k1 in range(FFT_K1):
            xk = _dot(m2f_ref[...], d_s[:, cols(k1)])
            xr, xi = xk[0:n2, :], xk[n2:, :]
            kr, ki = kr_ref[k1], ki_ref[k1]
            y_s[0:n2, cols(k1)] = (xr * kr - xi * ki).astype(bf16)
            y_s[n2:2 * n2, cols(k1)] = (xr * ki + xi * kr).astype(bf16)

        for k1 in range(FFT_K1):
            bk = _dot(m2i_ref[...], y_s[:, cols(k1)])
            br, bi = bk[0:n2, :], bk[n2:, :]
            twr, twi = twr_ref[:, cols(k1)], twi_ref[:, cols(k1)]
            b_s[0:n2, cols(k1)] = br * twr - bi * twi
            b_s[n2:2 * n2, cols(k1)] = br * twi + bi * twr

        def inverse1(rt, carry):
            r0 = pl.multiple_of(rt * r1, r1)
            brs = [b_s[pl.ds(r0, r1), cols(j)] for j in range(FFT_K1)]
            bis = [b_s[pl.ds(n2 + r0, r1), cols(j)] for j in range(FFT_K1)]
            for a in range(FFT_Z1):
                terms = [(1.0, brs[0]), (-1.0 if a % 2 else 1.0, brs[FFT_K1 - 1])]
                for j in range(1, FFT_K1 - 1):
                    terms += [(2.0 * c16[a][j], brs[j]), (-2.0 * s16[a][j], bis[j])]
                rows = pl.ds(a * n2 + r0, r1)
                z = g_ref[0, rows, :].astype(f32) * (_lincomb(terms) + z_s[rows, :] * db_ref[o:o + 1, :])
                if o + 1 < len(stages):
                    z_s[rows, :] = z
                else:
                    o_ref[0, rows, :] = z.astype(o_ref.dtype)
            return carry

        lax.fori_loop(0, n2 // r1, inverse1, 0)


def _hyena_tables(L):
    n_fft = 2 * L
    n2 = L // FFT_Z1
    ct = HY_CT
    idx = jnp.arange(n2, dtype=jnp.int32)
    a2 = ((idx[:, None] * idx[None, :]) % n2).astype(f32) * (2.0 * math.pi / n2)
    c2, s2 = jnp.cos(a2), jnp.sin(a2)
    m2f = jnp.block([[c2, s2], [-s2, c2]]).astype(bf16)
    m2i = jnp.block([[c2, -s2], [s2, c2]]).astype(bf16)
    at = (idx[:, None] * jnp.arange(FFT_K1, dtype=jnp.int32)[None, :]).astype(f32) * (2.0 * math.pi / n_fft)
    twr = jnp.broadcast_to(jnp.cos(at)[:, :, None], (n2, FFT_K1, ct)).reshape(n2, FFT_K1 * ct)
    twi = jnp.broadcast_to(jnp.sin(at)[:, :, None], (n2, FFT_K1, ct)).reshape(n2, FFT_K1 * ct)
    return m2f, m2i, twr, twi


def _hyena_fft(hx, tables, kr, ki, dbias):
    B, L, _ = hx.shape
    ct = HY_CT
    nct = B_WIDTH // ct
    n2 = L // FFT_Z1
    wide = FFT_K1 * ct
    once = pl.Buffered(1)
    m2f, m2i, twr, twi = tables

    def xspec(part):
        return pl.BlockSpec((1, L, ct), lambda c, b: (b, 0, part * nct + c))

    def kspec(order):
        return pl.BlockSpec((FFT_K1, n2, ct), lambda c, b: (0, 0, order * nct + c), pipeline_mode=once)

    def const(shape):
        return pl.BlockSpec(shape, lambda c, b: (0, 0), pipeline_mode=once)

    return pl.pallas_call(
        _hyena_fft_kernel,
        grid=(nct, B),
        in_specs=[const((2 * n2, 2 * n2)), const((2 * n2, 2 * n2)), const((n2, wide)), const((n2, wide)),
                  xspec(0), xspec(1), xspec(2),
                  kspec(0), kspec(0), kspec(1), kspec(1),
                  pl.BlockSpec((8, ct), lambda c, b: (0, c))],
        out_specs=pl.BlockSpec((1, L, ct), lambda c, b: (b, 0, c)),
        out_shape=jax.ShapeDtypeStruct((B, L, B_WIDTH), bf16),
        scratch_shapes=[pltpu.VMEM((L, ct), f32),
                        pltpu.VMEM((2 * n2, wide), bf16), pltpu.VMEM((2 * n2, wide), bf16),
                        pltpu.VMEM((2 * n2, wide), f32)],
        compiler_params=_cparams("parallel", "parallel"),
        name="hyena_fft",
    )(m2f, m2i, twr, twi, hx, hx, hx, kr, ki, kr, ki, dbias)


RT_E1, RT_E2, RT_W1, RT_W2, RT_R1, RT_R2 = 0, 1, 2, 3, 4, 5
ROUTE_TM = 512
PROJ_PARTS = 2
EXPERT_PARTS = 1
EXPERT_TM = 512
SC_WINDOW = 128


def _pack_bf16_pairs(x):
    n = x.shape[1] // 2
    bits = lax.bitcast_convert_type(x.astype(bf16).astype(f32), jnp.uint32)
    return bits[:, :n] | (bits[:, n:] >> 16)


def _unpack_bf16_pairs(p):
    hi = lax.bitcast_convert_type(p & jnp.uint32(0xFFFF0000), f32)
    lo = lax.bitcast_convert_type(p << 16, f32)
    return hi, lo


def _route_topk(logits):
    lane = lax.broadcasted_iota(jnp.int32, logits.shape, 1)
    neg = -jnp.inf
    big = ROUTER_LANES
    gl = jnp.where(lane < N_GROUPS, logits, neg)
    gmax = jnp.max(gl, axis=-1, keepdims=True)
    gsel = jnp.min(jnp.where(gl == gmax, lane, big), axis=-1, keepdims=True)
    psel = 1.0 / jnp.sum(jnp.exp(gl - gmax), axis=-1, keepdims=True)
    first = N_GROUPS + gsel * EXPERTS_PER_GROUP
    el = jnp.where((lane >= first) & (lane < first + EXPERTS_PER_GROUP), logits, neg)
    t1 = jnp.max(el, axis=-1, keepdims=True)
    i1 = jnp.min(jnp.where(el == t1, lane, big), axis=-1, keepdims=True)
    el2 = jnp.where(lane == i1, neg, el)
    t2 = jnp.max(el2, axis=-1, keepdims=True)
    i2 = jnp.min(jnp.where(el2 == t2, lane, big), axis=-1, keepdims=True)
    ex = jnp.exp(t2 - t1)
    w1 = psel / (1.0 + ex)
    w2 = ex * w1
    e1 = i1 - N_GROUPS
    e2 = i2 - N_GROUPS
    onehot = jnp.where((lane == e1) | (lane == e2), 1.0, 0.0)
    return e1, e2, w1, w2, onehot


def _route_record(e1, e2, w1, w2, before):
    lane = lax.broadcasted_iota(jnp.int32, before.shape, 1)
    r1 = jnp.sum(jnp.where(lane == e1, before, 0.0), axis=-1, keepdims=True)
    r2 = jnp.sum(jnp.where(lane == e2, before, 0.0), axis=-1, keepdims=True)
    rec = jnp.zeros(before.shape, f32)
    for ln, val in ((RT_E1, e1.astype(f32)), (RT_E2, e2.astype(f32)), (RT_W1, w1), (RT_W2, w2), (RT_R1, r1), (RT_R2, r2)):
        rec = jnp.where(lane == ln, val, rec)
    return rec


def _proj_route_kernel(*refs, n_in, starts):
    n_src = len(starts) + 1
    src_refs = refs[:n_src]
    refs = refs[n_src:]
    a_refs = refs[0:n_in]
    w_refs = refs[n_in:2 * n_in]
    g_ref, wr_ref, low_ref, xo_ref, hpa_ref, hpb_ref, rt_ref, rtt_ref, cnt_ref = refs[2 * n_in:]
    tm = xo_ref.shape[0]
    rp = tm // PROJ_PARTS
    parts = [slice(p * rp, (p + 1) * rp) for p in range(PROJ_PARTS)]
    xs = []
    for r in parts:
        x = _select_rows(src_refs, starts, r)
        for a_ref, w_ref in zip(a_refs, w_refs):
            x = x + _dot(a_ref[r, :], w_ref[...])
        xo_ref[r, :] = x
        xs.append(x)
    hs = [_rms(x, g_ref[...]) for x in xs]
    for r, h in zip(parts, hs):
        hp = _pack_bf16_pairs(h)
        quarter = hp.shape[1] // 2
        hpa_ref[r, :] = hp[:, :quarter]
        hpb_ref[r, :] = hp[:, quarter:]
    tops = []
    for h in hs:
        h_hi = h.astype(bf16)
        h_lo = (h - h_hi.astype(f32)).astype(bf16)
        both = _dot(h_hi, wr_ref[...])
        logits = both[:, :ROUTER_LANES] + both[:, ROUTER_LANES:] + _dot(h_lo, wr_ref[:, :ROUTER_LANES])
        tops.append(_route_topk(logits))
    onehot16 = jnp.concatenate([t[4] for t in tops], axis=0).astype(bf16)
    for r, (e1, e2, w1, w2, onehot) in zip(parts, tops):
        before = _dot(low_ref[r, :], onehot16)
        rec = _route_record(e1, e2, w1, w2, before)
        rt_ref[r, :] = rec
        rtt_ref[:, r] = rec.T[0:rtt_ref.shape[0], :]
    cnt_ref[0] = jnp.broadcast_to(before[rp - 1:rp, :] + onehot[rp - 1:rp, :], cnt_ref.shape[1:])


def _proj_route(xs, acts, ws, gain, w_router):
    T = sum(x.shape[0] for x in xs)
    D = xs[0].shape[1]
    tm = ROUTE_TM
    n_in = len(acts)
    low = jnp.asarray(np.tril(np.ones((tm, tm), np.float32), -1), dtype=bf16)
    in_specs, starts = _row_sources(xs, tm)
    in_specs += [pl.BlockSpec((tm, a.shape[1]), lambda i: (i, 0)) for a in acts]
    in_specs += [pl.BlockSpec(w.shape, lambda i: (0, 0)) for w in ws]
    in_specs += [pl.BlockSpec((1, D), lambda i: (0, 0)),
                 pl.BlockSpec((D, 2 * ROUTER_LANES), lambda i: (0, 0)),
                 pl.BlockSpec((tm, tm), lambda i: (0, 0))]
    return pl.pallas_call(
        functools.partial(_proj_route_kernel, n_in=n_in, starts=starts),
        grid=(T // tm,),
        in_specs=in_specs,
        out_specs=[pl.BlockSpec((tm, D), lambda i: (i, 0)),
                   pl.BlockSpec((tm, D // 4), lambda i: (i, 0)),
                   pl.BlockSpec((tm, D // 4), lambda i: (i, 0)),
                   pl.BlockSpec((tm, ROUTER_LANES), lambda i: (i, 0)),
                   pl.BlockSpec((8, tm), lambda i: (0, i)),
                   pl.BlockSpec((1, 8, ROUTER_LANES), lambda i: (i, 0, 0))],
        out_shape=[jax.ShapeDtypeStruct((T, D), f32),
                   jax.ShapeDtypeStruct((T, D // 4), jnp.uint32),
                   jax.ShapeDtypeStruct((T, D // 4), jnp.uint32),
                   jax.ShapeDtypeStruct((T, ROUTER_LANES), f32),
                   jax.ShapeDtypeStruct((8, T), f32),
                   jax.ShapeDtypeStruct((T // tm, 8, ROUTER_LANES), f32)],
        compiler_params=_cparams("parallel"),
        name="proj_route",
    )(*xs, *acts, *ws, gain.reshape(1, D), w_router, low)


def _moe_rows(T):
    return 2 * T + N_EXPERTS * EXPERT_TM


def _moe_plan(route_t, counts):
    T = route_t.shape[1]
    tm = ROUTE_TM
    cnt = counts[:, 0, :N_EXPERTS].astype(jnp.int32)
    total = jnp.sum(cnt, axis=0)
    padded = ((total + EXPERT_TM - 1) // EXPERT_TM) * EXPERT_TM
    ends = jnp.cumsum(padded)
    tile_base = (ends - padded)[None, :] + jnp.cumsum(cnt, axis=0) - cnt
    base = jnp.broadcast_to(tile_base.T[:, :, None], (N_EXPERTS, T // tm, tm)).reshape(N_EXPERTS, T)
    ids = jnp.arange(N_EXPERTS, dtype=jnp.int32)[:, None]
    dest = []
    for e_row, r_row in ((RT_E1, RT_R1), (RT_E2, RT_R2)):
        e = route_t[e_row].astype(jnp.int32)
        dest.append(jnp.sum(jnp.where(ids == e[None, :], base, 0), axis=0) + route_t[r_row].astype(jnp.int32))
    n_tiles = _moe_rows(T) // EXPERT_TM
    n_used = ends[-1] // EXPERT_TM
    starts = jnp.arange(n_tiles, dtype=jnp.int32) * EXPERT_TM
    tile_expert = jnp.sum((starts[:, None] >= ends[None, :]).astype(jnp.int32), axis=1)
    last_expert = jnp.sum((jnp.maximum(n_used - 1, 0) * EXPERT_TM >= ends).astype(jnp.int32))
    tile_expert = jnp.minimum(tile_expert, last_expert)
    return dest, tile_expert, n_used.reshape(1).astype(jnp.int32)


def _sc_mesh():
    return plsc.VectorSubcoreMesh(core_axis_name="core", subcore_axis_name="subcore")


def _sc_scatter_pair(rows, d0, d1, n_out):
    T, W = rows.shape
    win = SC_WINDOW

    @pl.kernel(out_type=jax.ShapeDtypeStruct((n_out, W), rows.dtype), mesh=_sc_mesh(), name="moe_sc_scatter")
    def scatter(x_hbm, i0_hbm, i1_hbm, o_hbm):
        def body(x_vmem, i0_vmem, i1_vmem):
            pltpu.sync_copy(x_vmem, o_hbm.at[i0_vmem.at[0]])
            pltpu.sync_copy(x_vmem, o_hbm.at[i1_vmem.at[0]])

        pltpu.emit_pipeline(
            body, grid=(T // win,),
            in_specs=[pl.BlockSpec((win, W), index_map=lambda i: (i, 0)),
                      pl.BlockSpec((1, win), index_map=lambda i: (0, i)),
                      pl.BlockSpec((1, win), index_map=lambda i: (0, i))],
            out_specs=[],
            core_axis_name=("core", "subcore"), dimension_semantics=(pltpu.PARALLEL,),
        )(x_hbm, i0_hbm, i1_hbm)

    return scatter(rows, d0.reshape(1, T), d1.reshape(1, T))


def _sc_gather(table, idx):
    n = idx.shape[0]
    W = table.shape[1]
    win = SC_WINDOW

    @pl.kernel(out_type=jax.ShapeDtypeStruct((n, W), table.dtype), mesh=_sc_mesh(), name="moe_sc_gather")
    def gather(t_hbm, i_hbm, o_hbm):
        def body(i_vmem, o_vmem):
            pltpu.sync_copy(t_hbm.at[i_vmem.at[0]], o_vmem)

        pltpu.emit_pipeline(
            body, grid=(n // win,),
            in_specs=[pl.BlockSpec((1, win), index_map=lambda i: (0, i))],
            out_specs=[pl.BlockSpec((win, W), index_map=lambda i: (i, 0))],
            core_axis_name=("core", "subcore"), dimension_semantics=(pltpu.PARALLEL,),
        )(i_hbm, o_hbm)

    return gather(table, idx.reshape(1, n))


def _expert_kernel(te_ref, nu_ref, xa_ref, xb_ref, wg_ref, wu_ref, wd_ref, ya_ref, yb_ref):
    @pl.when(pl.program_id(0) < nu_ref[0])
    def _():
        wg = wg_ref[0].astype(bf16)
        wu = wu_ref[0].astype(bf16)
        wd = wd_ref[0].astype(bf16)
        rp = xa_ref.shape[0] // EXPERT_PARTS
        parts = [slice(p * rp, (p + 1) * rp) for p in range(EXPERT_PARTS)]
        xs = []
        for r in parts:
            hi, lo = _unpack_bf16_pairs(jnp.concatenate([xa_ref[r, :], xb_ref[r, :]], axis=1))
            xs.append(jnp.concatenate([hi, lo], axis=1).astype(bf16))
        gates = [_dot(x, wg) for x in xs]
        ups = [_dot(x, wu) for x in xs]
        hgs = [(g * _sigmoid(g) * u).astype(bf16) for g, u in zip(gates, ups)]
        ys = [_dot(hg, wd) for hg in hgs]
        for r, y in zip(parts, ys):
            y = _pack_bf16_pairs(y)
            quarter = y.shape[1] // 2
            ya_ref[r, :] = y[:, :quarter]
            yb_ref[r, :] = y[:, quarter:]

    @pl.when(pl.program_id(0) >= nu_ref[0])
    def _():
        ya_ref[...] = jnp.zeros(ya_ref.shape, ya_ref.dtype)
        yb_ref[...] = jnp.zeros(yb_ref.shape, yb_ref.dtype)


def _experts(tile_expert, n_used, xa, xb, wg, wu, wd):
    n_rows, quarter = xa.shape
    D = 4 * quarter
    tm = EXPERT_TM
    grid_spec = pltpu.PrefetchScalarGridSpec(
        num_scalar_prefetch=2,
        grid=(n_rows // tm,),
        in_specs=[pl.BlockSpec((tm, quarter), lambda i, te, nu: (i, 0)),
                  pl.BlockSpec((tm, quarter), lambda i, te, nu: (i, 0)),
                  pl.BlockSpec((1, D, D_EXPERT), lambda i, te, nu: (te[i], 0, 0)),
                  pl.BlockSpec((1, D, D_EXPERT), lambda i, te, nu: (te[i], 0, 0)),
                  pl.BlockSpec((1, D_EXPERT, D), lambda i, te, nu: (te[i], 0, 0))],
        out_specs=[pl.BlockSpec((tm, quarter), lambda i, te, nu: (i, 0)),
                   pl.BlockSpec((tm, quarter), lambda i, te, nu: (i, 0))])
    return pl.pallas_call(
        _expert_kernel,
        grid_spec=grid_spec,
        out_shape=[jax.ShapeDtypeStruct((n_rows, quarter), jnp.uint32)] * 2,
        compiler_params=_cparams("arbitrary"),
        name="moe_experts",
    )(tile_expert, n_used, xa, xb, wg, wu, wd)


def _combine_kernel(x_ref, rt_ref, gf_ref, a1_ref, b1_ref, a2_ref, b2_ref, o_ref, *, final_norm):
    half = x_ref.shape[1] // 2
    h1, l1 = _unpack_bf16_pairs(jnp.concatenate([a1_ref[...], b1_ref[...]], axis=1))
    h2, l2 = _unpack_bf16_pairs(jnp.concatenate([a2_ref[...], b2_ref[...]], axis=1))
    w1 = rt_ref[:, RT_W1:RT_W1 + 1]
    w2 = rt_ref[:, RT_W2:RT_W2 + 1]
    out = jnp.concatenate([x_ref[:, 0:half] + w1 * h1 + w2 * h2, x_ref[:, half:] + w1 * l1 + w2 * l2], axis=1)
    if final_norm:
        out = _rms(out, gf_ref[...])
    o_ref[...] = out


def _combine(x, route, ga, gb, gain_final, final_norm, row_ranges=None):
    T, D = x.shape
    tm = ROUTE_TM
    n = T // tm
    quarter = D // 4
    outs = []
    for start, size in (row_ranges or ((0, T),)):
        lo = start // tm
        slot1 = pl.BlockSpec((tm, quarter), lambda i, lo=lo: (i + lo, 0))
        slot2 = pl.BlockSpec((tm, quarter), lambda i, lo=lo: (i + lo + n, 0))
        outs.append(pl.pallas_call(
            functools.partial(_combine_kernel, final_norm=final_norm),
            grid=(size // tm,),
            in_specs=[pl.BlockSpec((tm, D), lambda i, lo=lo: (i + lo, 0)),
                      pl.BlockSpec((tm, ROUTER_LANES), lambda i, lo=lo: (i + lo, 0)),
                      pl.BlockSpec((1, D), lambda i: (0, 0)),
                      slot1, slot1, slot2, slot2],
            out_specs=pl.BlockSpec((tm, D), lambda i: (i, 0)),
            out_shape=jax.ShapeDtypeStruct((size, D), f32),
            compiler_params=_cparams("parallel"),
            name="moe_combine",
        )(x, route, gain_final.reshape(1, D), ga, gb, ga, gb))
    return outs if row_ranges else outs[0]


def _moe(layer, x, hpa, hpb, route, route_t, counts, wg, wu, wd, gain_final, final_norm, row_ranges=None):
    (d0, d1), tile_expert, n_used = _moe_plan(route_t, counts)
    n_rows = _moe_rows(x.shape[0])
    xa = _sc_scatter_pair(hpa, d0, d1, n_rows)
    xb = _sc_scatter_pair(hpb, d0, d1, n_rows)
    ya, yb = _experts(tile_expert + layer * N_EXPERTS, n_used, xa, xb, wg, wu, wd)
    dcat = jnp.concatenate([d0, d1])
    return _combine(x, route, _sc_gather(ya, dcat), _sc_gather(yb, dcat), gain_final, final_norm, row_ranges)


LOG2E = 1.4426950408889634
SWA_LOCKSTEP = 2


def _swa_bias_tables():
    blk = C_BLOCK
    qi = np.arange(blk)[:, None]
    kj = np.arange(3 * blk)[None, :]
    dist = np.abs(blk + qi - kj)
    tabs = np.full((3, C_KV_HEADS, C_GROUP * blk, 3 * blk), -np.inf, np.float32)
    for case in range(3):
        ok = dist <= WINDOW
        if case == 0:
            ok = ok & (kj >= blk)
        if case == 2:
            ok = ok & (kj < 2 * blk)
        for hh in range(C_HEADS):
            slope = 2.0 ** (-8.0 * (hh + 1) / C_HEADS)
            t = np.where(ok, -slope * LOG2E * dist, -np.inf)
            kvh, g = divmod(hh, C_GROUP)
            tabs[case, kvh, g * blk:(g + 1) * blk] = t
    return jnp.asarray(tabs)


def _swa_kernel(sink_ref, q_ref, kp_ref, kc_ref, kn_ref, vp_ref, vc_ref, vn_ref, bias_ref, o_ref):
    blk = C_BLOCK
    hd = C_HEAD_DIM
    kcat = jnp.concatenate([kp_ref[0], kc_ref[0], kn_ref[0]], axis=0)
    vcat = jnp.concatenate([vp_ref[0], vc_ref[0], vn_ref[0]], axis=0)
    q = q_ref[0]
    rblk = lax.broadcasted_iota(jnp.int32, (C_GROUP * blk, 1), 0) // blk
    lane = lax.broadcasted_iota(jnp.int32, (3 * blk, LANE - hd), 1)
    ones_pad = jnp.where(lane == 0, 1.0, 0.0).astype(bf16)
    q4s, sinks = [], []
    for kvh in range(C_KV_HEADS):
        q4s.append(jnp.concatenate(
            [q[:, (kvh * C_GROUP + g) * hd:(kvh * C_GROUP + g + 1) * hd] for g in range(C_GROUP)], axis=0))
        sk = jnp.zeros((C_GROUP * blk, 1), f32)
        for g in range(C_GROUP):
            sk = jnp.where(rblk == g, sink_ref[kvh * C_GROUP + g] * LOG2E, sk)
        sinks.append(sk)
    outs = []
    for h0 in range(0, C_KV_HEADS, SWA_LOCKSTEP):
        hs = range(h0, h0 + SWA_LOCKSTEP)
        ss = [_dot_nt(q4s[h], kcat[:, h * hd:(h + 1) * hd]) + bias_ref[0, h] for h in hs]
        ms = [jnp.maximum(jnp.max(s, axis=-1, keepdims=True), sinks[h]) for s, h in zip(ss, hs)]
        ps = [jnp.exp2(s - m).astype(bf16) for s, m in zip(ss, ms)]
        pvs = [_dot(p, jnp.concatenate([vcat[:, h * hd:(h + 1) * hd], ones_pad], axis=1)) for p, h in zip(ps, hs)]
        for pv, m, h in zip(pvs, ms, hs):
            den = pv[:, hd:hd + 1] + jnp.exp2(sinks[h] - m)
            o4 = pv[:, 0:hd] / den
            outs += [o4[g * blk:(g + 1) * blk, :] for g in range(C_GROUP)]
    o_ref[0] = jnp.concatenate(outs, axis=1).astype(o_ref.dtype)


def _swa(qkv, sinks):
    B, L, _ = qkv.shape
    nb = L // C_BLOCK
    dq = C_HEADS * C_HEAD_DIM
    dkv = C_KV_HEADS * C_HEAD_DIM
    k_col = dq // dkv
    v_col = k_col + 1

    def kv_spec(col, shift):
        return pl.BlockSpec((1, C_BLOCK, dkv),
                            lambda n, b: (b, jnp.clip(n + shift, 0, nb - 1), col))

    def edge_case(n, b):
        return (jnp.where(n == 0, 0, jnp.where(n == nb - 1, 2, 1)), 0, 0, 0)

    return pl.pallas_call(
        _swa_kernel,
        grid=(nb, B),
        in_specs=[pl.BlockSpec(memory_space=pltpu.SMEM),
                  pl.BlockSpec((1, C_BLOCK, dq), lambda n, b: (b, n, 0)),
                  kv_spec(k_col, -1), kv_spec(k_col, 0), kv_spec(k_col, 1),
                  kv_spec(v_col, -1), kv_spec(v_col, 0), kv_spec(v_col, 1),
                  pl.BlockSpec((1, C_KV_HEADS, C_GROUP * C_BLOCK, 3 * C_BLOCK), edge_case)],
        out_specs=pl.BlockSpec((1, C_BLOCK, dq), lambda n, b: (b, n, 0)),
        out_shape=jax.ShapeDtypeStruct((B, L, dq), bf16),
        compiler_params=_cparams("parallel", "parallel"),
        name="swa",
    )(sinks, qkv, qkv, qkv, qkv, qkv, qkv, qkv, _swa_bias_tables())


def _hyena_filters(L, w1, b1, w2, b2, w3, b3, w4, freq):
    hp = lax.Precision.HIGHEST
    t = jnp.linspace(0.0, 1.0, L, dtype=f32)[:, None]
    wpos = (2.0 * math.pi / L) * jnp.arange(L, dtype=f32)[:, None]
    bands = jnp.linspace(1e-4, HY_BANDS - 1, HY_BANDS, dtype=f32)[None, :]
    feats = jnp.concatenate([t, jnp.cos(bands * wpos), -jnp.sin(bands * wpos)], axis=-1)
    h = jnp.sin(freq[0] * (jnp.dot(feats, w1, precision=hp) + b1))
    h = jnp.sin(freq[1] * (jnp.dot(h, w2, precision=hp) + b2))
    h = jnp.sin(freq[2] * (jnp.dot(h, w3, precision=hp) + b3))
    h = jnp.dot(h, w4, precision=hp).reshape(L, HY_ORDER, 2, B_WIDTH)
    deltas = jnp.abs(jnp.linspace(HY_MIN_DECAY, HY_MAX_DECAY, B_WIDTH, dtype=f32))
    window = jnp.exp(-t * deltas[None, :])
    return h * window[:, None, None, :]


def _pad_rows(w, rows=8):
    return jnp.pad(w, ((0, rows - w.shape[0]), (0, 0)))


def _even_in_weights(w_in):
    hk = A_HEADS * A_DK
    q = w_in[:, 0:hk].reshape(D_MODEL, A_HEADS, A_DK)
    k = w_in[:, hk:2 * hk].reshape(D_MODEL, A_HEADS, A_DK)
    v = w_in[:, 2 * hk:A_QKV].reshape(D_MODEL, A_HEADS, A_DV)
    z = w_in[:, A_QKV:A_QKV + A_WIDTH].reshape(D_MODEL, A_HEADS, A_DV)
    qkvz = jnp.concatenate([q, k, v, z], axis=-1).reshape(D_MODEL, A_HEADS * 4 * A_DK)
    ab = w_in[:, A_QKV + A_WIDTH:A_QKV + A_WIDTH + 4 * A_HEADS].reshape(D_MODEL, 4, A_HEADS)
    ab = jnp.transpose(ab, (0, 2, 1))
    ab = jnp.pad(ab, ((0, 0), (0, 0), (0, LANE - 4))).reshape(D_MODEL, A_HEADS * LANE)
    hx = w_in[:, A_QKV + A_WIDTH + 4 * A_HEADS:]
    return jnp.concatenate([qkvz, ab, hx], axis=1).astype(bf16)


def _router_weights(w_group, w_expert):
    w = jnp.concatenate([w_group, w_expert], axis=1)
    w = jnp.pad(w, ((0, 0), (0, ROUTER_LANES - w.shape[1])))
    hi = w.astype(bf16)
    lo = (w - hi.astype(f32)).astype(bf16)
    return jnp.concatenate([hi, lo], axis=1)


def kernel(x_prompt, x_sample, norm_mix, norm_ffn, norm_final, ev_w_in, ev_conv_a, ev_alog_f, ev_alog_b, ev_dtb_f, ev_dtb_b, ev_onorm, ev_conv_b, hy_w1, hy_b1, hy_w2, hy_b2, hy_w3, hy_b3, hy_w4, hy_freq, hy_dbias, ev_w_out, od_w_qkv, od_sinks, od_w_out, moe_w_group, moe_w_expert, moe_w_gate, moe_w_up, moe_w_down):
    L = x_prompt.shape[1]
    n_layers = moe_w_gate.shape[0]
    wg = moe_w_gate.reshape(n_layers * N_EXPERTS, D_MODEL, D_EXPERT)
    wu = moe_w_up.reshape(n_layers * N_EXPERTS, D_MODEL, D_EXPERT)
    wd = moe_w_down.reshape(n_layers * N_EXPERTS, D_EXPERT, D_MODEL)

    w_in = _even_in_weights(ev_w_in[0])
    n_qkvz = A_HEADS * 4 * A_DK
    n_ab = A_HEADS * LANE
    n_hx = 3 * B_WIDTH
    hk = A_HEADS * A_DK
    ca = ev_conv_a[0]
    pass_z = jnp.broadcast_to(jnp.asarray([0.0, 1.0, 0.0], f32), (A_HEADS, A_DV, 3))
    conv_a = jnp.concatenate([ca[0:hk].reshape(A_HEADS, A_DK, 3), ca[hk:2 * hk].reshape(A_HEADS, A_DK, 3),
                              ca[2 * hk:].reshape(A_HEADS, A_DV, 3), pass_z], axis=1)
    conv_a = _pad_rows(conv_a.reshape(A_HEADS * 4 * A_DK, 3).T)

    def gate_lanes(fwd, bwd):
        t = jnp.stack([fwd, bwd], axis=1)
        return jnp.pad(t, ((0, 0), (2, LANE - 4))).reshape(1, A_HEADS * LANE)

    gate_alog = gate_lanes(ev_alog_f[0], ev_alog_b[0])
    gate_dtb = gate_lanes(ev_dtb_f[0], ev_dtb_b[0])
    filt = _hyena_filters(L, hy_w1[0], hy_b1[0], hy_w2[0], hy_b2[0], hy_w3[0], hy_b3[0], hy_w4[0], hy_freq[0])
    hsum = (filt[:, :, 0] + filt[:, :, 1]).reshape(L, HY_ORDER * B_WIDTH)
    hdiff = (filt[:, :, 0] - filt[:, :, 1]).reshape(L, HY_ORDER * B_WIDTH)
    kr, ki = _filter_spectrum(hsum, hdiff)
    fft_tables = _hyena_tables(L)
    conv_b = _pad_rows(ev_conv_b[0].T)
    dbias = _pad_rows(hy_dbias[0])
    w_out = ev_w_out[0].astype(bf16)
    routers = [_router_weights(moe_w_group[l], moe_w_expert[l]) for l in range(n_layers)]
    n_qkv = od_w_qkv.shape[-1]
    dq = C_HEADS * C_HEAD_DIM
    w_qkv = jnp.concatenate([od_w_qkv[0][:, :dq] * (LOG2E * C_HEAD_DIM ** -0.5), od_w_qkv[0][:, dq:]], axis=1)
    w_qkv = w_qkv.astype(bf16)
    w_att_out = od_w_out[0].astype(bf16)

    def trunk(xb):
        B = xb.shape[0]
        T = B * L
        x = xb.reshape(T, D_MODEL)
        qkvz, gates, hx = _even_in_proj(x, L, norm_mix[0], w_in, conv_a, conv_b, gate_alog, gate_dtb)
        o_a = _deltanet(qkvz.reshape(B, L, n_qkvz), gates.reshape(B, L, n_ab), ev_onorm[0])
        o_b = _hyena_fft(hx.reshape(B, L, n_hx), fft_tables, kr, ki, dbias)
        x, hpa, hpb, route, route_t, counts = _proj_route(
            (x,), (o_a.reshape(T, A_WIDTH), o_b.reshape(T, B_WIDTH)), (w_out[:A_WIDTH], w_out[A_WIDTH:]),
            norm_ffn[0], routers[0])
        x = _moe(0, x, hpa, hpb, route, route_t, counts, wg, wu, wd, norm_final, final_norm=False)
        (qkv,) = _norm_matmul((x,), norm_mix[1], w_qkv, (n_qkv,), (bf16,))
        o_c = _swa(qkv.reshape(B, L, n_qkv), od_sinks[0])
        x, hpa, hpb, route, route_t, counts = _proj_route(
            (x,), (o_c.reshape(T, C_HEADS * C_HEAD_DIM),), (w_att_out,), norm_ffn[1], routers[1])
        y = _moe(1, x, hpa, hpb, route, route_t, counts, wg, wu, wd, norm_final, final_norm=True)
        return y.reshape(B, L, D_MODEL)

    return (trunk(x_prompt), trunk(x_sample))
```

```python
import functools
import math

import jax
import jax.numpy as jnp
import numpy as np
from jax import lax
from jax.experimental import pallas as pl
from jax.experimental.pallas import tpu as pltpu
from jax.experimental.pallas import tpu_sc as plsc

f32 = jnp.float32
bf16 = jnp.bfloat16

EPS = 1e-6
D_MODEL = 1024

A_HEADS = 4
A_DK = 128
A_DV = 128
A_WIDTH = A_HEADS * A_DV
A_QKV = 2 * A_HEADS * A_DK + A_HEADS * A_DV
DELTA_CHUNK = 128
INV_BASE = 8
DELTA_GROUP = 8

B_WIDTH = D_MODEL - A_WIDTH
HY_ORDER = 2
HY_EMB = 33
HY_BANDS = (HY_EMB - 1) // 2
HY_TARGET = 1e-2
HY_MIN_DECAY = math.log(HY_TARGET) / 1.5
HY_MAX_DECAY = math.log(HY_TARGET) / 0.3
HY_CT = 256
HY_ROWS = 256

C_HEADS = 16
C_KV_HEADS = 4
C_HEAD_DIM = 64
C_GROUP = C_HEADS // C_KV_HEADS
WINDOW = 128
C_BLOCK = 128

N_GROUPS = 4
EXPERTS_PER_GROUP = 8
N_EXPERTS = N_GROUPS * EXPERTS_PER_GROUP
D_EXPERT = 256
ROUTER_LANES = 128

LANE = 128
ROW_TM = 512
VMEM_LIMIT = 56 * 1024 * 1024
LOG2E = 1.4426950408889634


def _cparams(*sem):
    return pltpu.CompilerParams(dimension_semantics=sem, vmem_limit_bytes=VMEM_LIMIT)


def _dot(a, b):
    return jnp.dot(a, b, preferred_element_type=f32)


def _dot_nt(a, b):
    return lax.dot_general(a, b, (((1,), (1,)), ((), ())), preferred_element_type=f32)


def _dot_tn(a, b):
    return lax.dot_general(a, b, (((0,), (0,)), ((), ())), preferred_element_type=f32)


def _sigmoid(x):
    return 1.0 / (1.0 + jnp.exp(-x))


def _rms(x, g):
    return x * lax.rsqrt(jnp.mean(x * x, axis=-1, keepdims=True) + EPS) * g


def _norm_matmul_kernel(x_ref, g_ref, w_ref, *o_refs):
    h = _rms(x_ref[...], g_ref[...]).astype(bf16)
    off = 0
    for o_ref in o_refs:
        n = o_ref.shape[-1]
        o_ref[...] = _dot(h, w_ref[:, off:off + n]).astype(o_ref.dtype)
        off += n


def _norm_matmul(x, gain, w, splits, dtypes):
    T, D = x.shape
    tm = ROW_TM
    n_all = w.shape[1]
    return pl.pallas_call(
        _norm_matmul_kernel,
        grid=(T // tm,),
        in_specs=[pl.BlockSpec((tm, D), lambda i: (i, 0)),
                  pl.BlockSpec((1, D), lambda i: (0, 0)),
                  pl.BlockSpec((D, n_all), lambda i: (0, 0))],
        out_specs=[pl.BlockSpec((tm, n), lambda i: (i, 0)) for n in splits],
        out_shape=[jax.ShapeDtypeStruct((T, n), dt) for n, dt in zip(splits, dtypes)],
        compiler_params=_cparams("parallel"),
        name="norm_matmul",
    )(x, gain.reshape(1, D), w)


MK_LOWER, MK_UPPER, MK_BASE, MK_LEVEL0 = 0, 1, 2, 3
N_LEVELS = int(math.log2(DELTA_CHUNK // INV_BASE))
MK_EYE = MK_LEVEL0 + N_LEVELS
N_MASKS = MK_EYE + 1


def _delta_masks():
    c = np.arange(DELTA_CHUNK)[:, None]
    s = np.arange(DELTA_CHUNK)[None, :]
    m = [s <= c, s >= c, (c // INV_BASE == s // INV_BASE) & (c != s)]
    b = INV_BASE
    while b < DELTA_CHUNK:
        m.append((c // (2 * b) == s // (2 * b)) & (c // b != s // b))
        b *= 2
    m.append(c == s)
    m = np.stack(m).astype(np.float32)
    return jnp.asarray(m), jnp.asarray(m, dtype=bf16)


def _tri_inverse(mats, mk_ref, mk16_ref):
    ps = [a * mk_ref[MK_BASE] for a in mats]
    ts = [mk_ref[MK_EYE] - p for p in ps]
    n = 2
    while n < INV_BASE:
        p16s = [p.astype(bf16) for p in ps]
        ps = [_dot(p16, p16) for p16 in p16s]
        ts = [t + _dot(t.astype(bf16), p.astype(bf16)) for t, p in zip(ts, ps)]
        n *= 2
    a16s = [a.astype(bf16) for a in mats]
    t16s = [t.astype(bf16) for t in ts]
    for lvl in range(N_LEVELS):
        xs = [_dot(a16 * mk16_ref[MK_LEVEL0 + lvl], t16) for a16, t16 in zip(a16s, t16s)]
        t16s = [t16 - _dot(t16, x.astype(bf16)).astype(bf16) for t16, x in zip(t16s, xs)]
    return t16s


def _deltanet_kernel(sc_ref, x_ref, ab_ref, cw_ref, on_ref, mk_ref, mk16_ref, o_ref, mp_s, nr_s, gl_s, o_s):
    hd = pl.program_id(1)
    L = x_ref.shape[1]
    C = DELTA_CHUNK
    nchunk = L // C
    group = math.gcd(DELTA_GROUP, nchunk)
    nqkv = 3 * A_DK
    halo = 16

    lane = lax.broadcasted_iota(jnp.int32, (1, LANE), 1)
    alog = jnp.where(lane == 2, sc_ref[hd, 0], jnp.where(lane == 3, sc_ref[hd, 1], 0.0))
    dtb = jnp.where(lane == 2, sc_ref[hd, 2], jnp.where(lane == 3, sc_ref[hd, 3], 0.0))
    neg_a = jnp.where((lane == 2) | (lane == 3), -jnp.exp(alog), 0.0)
    rowi = lax.broadcasted_iota(jnp.int32, (C, 1), 0)
    cw = cw_ref[0]

    def chunk_inputs(i):
        r0 = pl.multiple_of(i * C, C)
        xc = x_ref[0, pl.ds(r0, C), 0:nqkv].astype(f32)
        rp = pl.multiple_of(jnp.maximum(r0 - halo, 0), halo)
        rn = pl.multiple_of(jnp.minimum(r0 + C, L - halo), halo)
        prow = x_ref[0, pl.ds(rp, halo), 0:nqkv][halo - 1:halo, :].astype(f32)
        nrow = x_ref[0, pl.ds(rn, halo), 0:nqkv][0:1, :].astype(f32)
        prow = jnp.where(i > 0, prow, 0.0)
        nrow = jnp.where(i < nchunk - 1, nrow, 0.0)
        prev = jnp.where(rowi == 0, prow, pltpu.roll(xc, 1, 0))
        nxt = jnp.where(rowi == C - 1, nrow, pltpu.roll(xc, C - 1, 0))
        y = prev * cw[0:1, :] + xc * cw[1:2, :] + nxt * cw[2:3, :]
        y = y * _sigmoid(y)
        q = y[:, 0:A_DK]
        k = y[:, A_DK:2 * A_DK]
        vc = y[:, 2 * A_DK:nqkv]
        qc = q * lax.rsqrt(jnp.sum(q * q, axis=-1, keepdims=True) + EPS) * (A_DK ** -0.5)
        kc = k * lax.rsqrt(jnp.sum(k * k, axis=-1, keepdims=True) + EPS)

        abc = ab_ref[0, pl.ds(r0, C), :]
        zg = abc + dtb
        g = neg_a * (jnp.maximum(zg, 0.0) + jnp.log(1.0 + jnp.exp(-jnp.abs(zg))))
        low16 = mk16_ref[MK_LOWER]
        g1 = g.astype(bf16)
        e1 = g - g1.astype(f32)
        g2 = e1.astype(bf16)
        g3 = (e1 - g2.astype(f32)).astype(bf16)
        pre = _dot(low16, g1) + _dot(low16, g2) + _dot(low16, g3)
        tot = pre[C - 1:C, :]
        suf = tot - pre + g
        beta_all = _sigmoid(abc)

        k16 = kc.astype(bf16)
        kk = _dot_nt(k16, k16)
        qk = _dot_nt(qc.astype(bf16), k16)

        chains = []
        for rev in (0, 1):
            beta = beta_all[:, rev:rev + 1]
            gcc = (suf if rev else pre)[:, 2 + rev:3 + rev]
            tot11 = tot[:, 2 + rev:3 + rev]
            keep = mk_ref[MK_UPPER if rev else MK_LOWER]
            bc = jnp.broadcast_to(gcc * LOG2E, (C, C))
            decay = jnp.exp2((bc - bc.T) * keep) * keep
            a = (beta * kk) * decay
            eg = jnp.exp(gcc)
            chains.append(dict(
                i=i, rev=rev, a=a, gl=jnp.exp(tot11), qd=qc * eg,
                rhs=jnp.concatenate([vc * beta, kc * (beta * eg)], axis=1).astype(bf16),
                kd16=(kc * jnp.exp(tot11 - gcc)).astype(bf16),
                qk16=(qk * decay).astype(bf16)))
        return chains

    def prep(it, carry):
        chains = []
        for gi in range(group):
            chains += chunk_inputs(it * group + gi)
        t16s = _tri_inverse([c["a"] for c in chains], mk_ref, mk16_ref)
        uws = [_dot(t16, c["rhs"]).astype(bf16) for t16, c in zip(t16s, chains)]
        kts = [_dot_tn(c["kd16"], uw) for c, uw in zip(chains, uws)]
        prs = [_dot(c["qk16"], uw) for c, uw in zip(chains, uws)]
        for c, kt, pr in zip(chains, kts, prs):
            rev, i = c["rev"], c["i"]
            mp_s[rev, i, 0:A_DK, :] = (-kt[:, A_DV:]).astype(bf16)
            mp_s[rev, i, A_DK:A_DK + C, :] = (c["qd"] - pr[:, A_DV:]).astype(bf16)
            nr_s[rev, i, 0:A_DK, :] = kt[:, 0:A_DV]
            nr_s[rev, i, A_DK:A_DK + C, :] = pr[:, 0:A_DV]
            gl_s[rev, i] = jnp.broadcast_to(c["gl"], (8, LANE))
        return carry

    lax.fori_loop(0, nchunk // group, prep, 0)

    def scan(i, carry):
        out = []
        for rev, S in enumerate(carry):
            j = (nchunk - 1 - i) if rev else i
            y = _dot(mp_s[rev, j], S.astype(bf16))
            nr = nr_s[rev, j]
            o_s[rev, pl.ds(pl.multiple_of(j * C, C), C), :] = y[A_DK:, :] + nr[A_DK:, :]
            out.append(S * gl_s[rev, j][0:1, :] + y[0:A_DK, :] + nr[0:A_DK, :])
        return tuple(out)

    S0 = jnp.zeros((A_DK, A_DV), f32)
    lax.fori_loop(0, nchunk, scan, (S0, S0))

    def finish(it, carry):
        for gi in range(group):
            r0 = pl.multiple_of((it * group + gi) * C, C)
            o = o_s[0, pl.ds(r0, C), :] + o_s[1, pl.ds(r0, C), :]
            zg = x_ref[0, pl.ds(r0, C), nqkv:nqkv + A_DV].astype(f32)
            o_ref[0, pl.ds(r0, C), :] = (_rms(o, on_ref[...]) * (zg * _sigmoid(zg))).astype(o_ref.dtype)
        return carry

    lax.fori_loop(0, nchunk // group, finish, 0)


def _deltanet(qkvz, abh, conv_w, gate_sc, onorm):
    B, L, _ = qkvz.shape
    C = DELTA_CHUNK
    nchunk = L // C
    hw = 4 * A_DK
    scratch = [pltpu.VMEM((2, nchunk, A_DK + C, A_DV), bf16),
               pltpu.VMEM((2, nchunk, A_DK + C, A_DV), f32),
               pltpu.VMEM((2, nchunk, 8, LANE), f32),
               pltpu.VMEM((2, L, A_DV), f32)]
    return pl.pallas_call(
        _deltanet_kernel,
        grid=(B, A_HEADS),
        in_specs=[pl.BlockSpec(memory_space=pltpu.SMEM),
                  pl.BlockSpec((1, L, hw), lambda b, h: (b, 0, h)),
                  pl.BlockSpec((1, L, LANE), lambda b, h: (b, 0, h)),
                  pl.BlockSpec((1, 8, 3 * A_DK), lambda b, h: (h, 0, 0)),
                  pl.BlockSpec((1, A_DV), lambda b, h: (0, 0)),
                  pl.BlockSpec((N_MASKS, C, C), lambda b, h: (0, 0, 0)),
                  pl.BlockSpec((N_MASKS, C, C), lambda b, h: (0, 0, 0))],
        out_specs=pl.BlockSpec((1, L, A_DV), lambda b, h: (b, 0, h)),
        out_shape=jax.ShapeDtypeStruct((B, L, A_WIDTH), bf16),
        scratch_shapes=scratch,
        compiler_params=_cparams("parallel", "parallel"),
        name="deltanet",
    )(gate_sc, qkvz, abh, conv_w, onorm.reshape(1, A_DV), *_delta_masks())


def _filter_spectrum_kernel(c_ref, s_ref, hs_ref, hd_ref, kr_ref, ki_ref):
    L = c_ref.shape[1]
    rows = kr_ref.shape[1]
    scale = 1.0 / (2 * L)

    def split(x):
        hi = x.astype(bf16)
        return hi, (x - hi.astype(f32)).astype(bf16)

    hs_hi, hs_lo = split(hs_ref[...])
    hd_hi, hd_lo = split(hd_ref[...])
    for k1 in range(kr_ref.shape[0]):
        c = c_ref[k1 * rows:(k1 + 1) * rows, :]
        s = s_ref[k1 * rows:(k1 + 1) * rows, :]
        kr_ref[k1] = (_dot(c, hs_hi) + _dot(c, hs_lo)) * scale
        ki_ref[k1] = (_dot(s, hd_hi) + _dot(s, hd_lo)) * (-scale)


def _filter_spectrum(hsum, hdiff):
    L, n = hsum.shape
    ct = HY_CT
    n_fft = 2 * L
    n2 = L // FFT_Z1
    d = jnp.arange(L, dtype=jnp.int32)[None, :]
    a1 = ((jnp.arange(FFT_K1, dtype=jnp.int32)[:, None] * d) % n_fft).astype(f32) * (2.0 * math.pi / n_fft)
    a2 = ((jnp.arange(n2, dtype=jnp.int32)[:, None] * d) % n2).astype(f32) * (2.0 * math.pi / n2)
    c1, s1 = jnp.cos(a1)[:, None, :], jnp.sin(a1)[:, None, :]
    c2, s2 = jnp.cos(a2)[None, :, :], jnp.sin(a2)[None, :, :]
    cmat = (c1 * c2 - s1 * s2).reshape(FFT_K1 * n2, L).astype(bf16)
    smat = (s1 * c2 + c1 * s2).reshape(FFT_K1 * n2, L).astype(bf16)
    once = pl.Buffered(1)
    return pl.pallas_call(
        _filter_spectrum_kernel,
        grid=(n // ct,),
        in_specs=[pl.BlockSpec((FFT_K1 * n2, L), lambda j: (0, 0), pipeline_mode=once),
                  pl.BlockSpec((FFT_K1 * n2, L), lambda j: (0, 0), pipeline_mode=once),
                  pl.BlockSpec((L, ct), lambda j: (0, j)),
                  pl.BlockSpec((L, ct), lambda j: (0, j))],
        out_specs=[pl.BlockSpec((FFT_K1, n2, ct), lambda j: (0, 0, j)),
                   pl.BlockSpec((FFT_K1, n2, ct), lambda j: (0, 0, j))],
        out_shape=[jax.ShapeDtypeStruct((FFT_K1, n2, n), f32),
                   jax.ShapeDtypeStruct((FFT_K1, n2, n), f32)],
        compiler_params=_cparams("parallel"),
        name="filter_spectrum",
    )(cmat, smat, hsum, hdiff)


FFT_N1 = 16
FFT_K1 = FFT_N1 // 2 + 1
FFT_Z1 = FFT_N1 // 2
FFT_ROWS = 32


def _lincomb(terms):
    groups = {}
    for c, a in terms:
        if abs(c) > 1e-9:
            groups.setdefault(round(abs(c), 9), []).append((c > 0, a))
    total = None
    for mag, items in groups.items():
        pos = [a for p, a in items if p]
        neg = [a for p, a in items if not p]
        s = None
        for a in pos:
            s = a if s is None else s + a
        for a in neg:
            s = -a if s is None else s - a
        if mag != 1.0:
            s = s * mag
        total = s if total is None else total + s
    return total


def _hyena_fft_kernel(m2f_ref, m2i_ref, twr_ref, twi_ref, x1_ref, x2_ref, v_ref, w1_ref, w2_ref, wv_ref,
                      kr0_ref, ki0_ref, kr1_ref, ki1_ref, db_ref, o_ref, z_s, g_s, d_s, y_s, b_s):
    L, ct = z_s.shape
    n2 = L // FFT_Z1
    R = min(HY_ROWS, L)
    nt = L // R
    r1 = min(FFT_ROWS, n2)
    halo = 16
    rowi = lax.broadcasted_iota(jnp.int32, (R, 1), 0)
    ang = 2.0 * math.pi / FFT_N1
    c16 = [[math.cos(ang * a * b) for b in range(FFT_K1)] for a in range(FFT_Z1)]
    s16 = [[math.sin(ang * a * b) for b in range(FFT_K1)] for a in range(FFT_Z1)]

    def conv_into(dst, x_ref, w_ref):
        def body(t, carry):
            r0 = pl.multiple_of(t * R, R)
            x = x_ref[0, pl.ds(r0, R), :].astype(f32)
            rp = pl.multiple_of(jnp.maximum(r0 - halo, 0), halo)
            rn = pl.multiple_of(jnp.minimum(r0 + R, L - halo), halo)
            prow = jnp.where(t > 0, x_ref[0, pl.ds(rp, halo), :][halo - 1:halo, :].astype(f32), 0.0)
            nrow = jnp.where(t < nt - 1, x_ref[0, pl.ds(rn, halo), :][0:1, :].astype(f32), 0.0)
            prev = jnp.where(rowi == 0, prow, pltpu.roll(x, 1, 0))
            nxt = jnp.where(rowi == R - 1, nrow, pltpu.roll(x, R - 1, 0))
            w = w_ref[...]
            dst[pl.ds(r0, R), :] = prev * w[0:1, :] + x * w[1:2, :] + nxt * w[2:3, :]
            return carry
        lax.fori_loop(0, nt, body, 0)

    def cols(k1):
        return slice(k1 * ct, (k1 + 1) * ct)

    conv_into(z_s, v_ref, wv_ref)
    stages = ((x1_ref, w1_ref, kr0_ref, ki0_ref), (x2_ref, w2_ref, kr1_ref, ki1_ref))
    for o, (g_ref, gw_ref, kr_ref, ki_ref) in enumerate(stages):
        conv_into(g_s, g_ref, gw_ref)

        def forward1(rt, carry):
            r0 = pl.multiple_of(rt * r1, r1)
            zs = [z_s[pl.ds(a * n2 + r0, r1), :] for a in range(FFT_Z1)]
            for k1 in range(FFT_K1):
                ar = _lincomb([(c16[a][k1], zs[a]) for a in range(FFT_Z1)])
                ai = _lincomb([(-s16[a][k1], zs[a]) for a in range(FFT_Z1)])
                twr = twr_ref[pl.ds(r0, r1), cols(k1)]
                twi = twi_ref[pl.ds(r0, r1), cols(k1)]
                if ai is None:
                    pr, pi = ar * twr, -(ar * twi)
                else:
                    pr, pi = ar * twr + ai * twi, ai * twr - ar * twi
                d_s[pl.ds(r0, r1), cols(k1)] = pr.astype(bf16)
                d_s[pl.ds(n2 + r0, r1), cols(k1)] = pi.astype(bf16)
            return carry

        lax.fori_loop(0, n2 // r1, forward1, 0)

        for k1 in range(FFT_K1):
            xk = _dot(m2f_ref[...], d_s[:, cols(k1)])
            xr, xi = xk[0:n2, :], xk[n2:, :]
            kr, ki = kr_ref[k1], ki_ref[k1]
            y_s[0:n2, cols(k1)] = (xr * kr - xi * ki).astype(bf16)
            y_s[n2:2 * n2, cols(k1)] = (xr * ki + xi * kr).astype(bf16)

        for k1 in range(FFT_K1):
            bk = _dot(m2i_ref[...], y_s[:, cols(k1)])
            br, bi = bk[0:n2, :], bk[n2:, :]
            twr, twi = twr_ref[:, cols(k1)], twi_ref[:, cols(k1)]
            b_s[0:n2, cols(k1)] = br * twr - bi * twi
            b_s[n2:2 * n2, cols(k1)] = br * twi + bi * twr

        def inverse1(rt, carry):
            r0 = pl.multiple_of(rt * r1, r1)
            brs = [b_s[pl.ds(r0, r1), cols(j)] for j in range(FFT_K1)]
            bis = [b_s[pl.ds(n2 + r0, r1), cols(j)] for j in range(FFT_K1)]
            for a in range(FFT_Z1):
                terms = [(1.0, brs[0]), (-1.0 if a % 2 else 1.0, brs[FFT_K1 - 1])]
                for j in range(1, FFT_K1 - 1):
                    terms += [(2.0 * c16[a][j], brs[j]), (-2.0 * s16[a][j], bis[j])]
                rows = pl.ds(a * n2 + r0, r1)
                z = g_s[rows, :] * (_lincomb(terms) + z_s[rows, :] * db_ref[o:o + 1, :])
                if o + 1 < len(stages):
                    z_s[rows, :] = z
                else:
                    o_ref[0, rows, :] = z.astype(o_ref.dtype)
            return carry

        lax.fori_loop(0, n2 // r1, inverse1, 0)


def _hyena_tables(L):
    n_fft = 2 * L
    n2 = L // FFT_Z1
    ct = HY_CT
    idx = jnp.arange(n2, dtype=jnp.int32)
    a2 = ((idx[:, None] * idx[None, :]) % n2).astype(f32) * (2.0 * math.pi / n2)
    c2, s2 = jnp.cos(a2), jnp.sin(a2)
    m2f = jnp.block([[c2, s2], [-s2, c2]]).astype(bf16)
    m2i = jnp.block([[c2, -s2], [s2, c2]]).astype(bf16)
    at = (idx[:, None] * jnp.arange(FFT_K1, dtype=jnp.int32)[None, :]).astype(f32) * (2.0 * math.pi / n_fft)
    twr = jnp.broadcast_to(jnp.cos(at)[:, :, None], (n2, FFT_K1, ct)).reshape(n2, FFT_K1 * ct)
    twi = jnp.broadcast_to(jnp.sin(at)[:, :, None], (n2, FFT_K1, ct)).reshape(n2, FFT_K1 * ct)
    return m2f, m2i, twr, twi


def _hyena_fft(hx, conv_w, tables, kr, ki, dbias):
    B, L, _ = hx.shape
    ct = HY_CT
    nct = B_WIDTH // ct
    n2 = L // FFT_Z1
    wide = FFT_K1 * ct
    once = pl.Buffered(1)
    m2f, m2i, twr, twi = tables

    def xspec(part):
        return pl.BlockSpec((1, L, ct), lambda c, b: (b, 0, part * nct + c))

    def wspec(part):
        return pl.BlockSpec((8, ct), lambda c, b: (0, part * nct + c))

    def kspec(order):
        return pl.BlockSpec((FFT_K1, n2, ct), lambda c, b: (0, 0, order * nct + c), pipeline_mode=once)

    def const(shape):
        return pl.BlockSpec(shape, lambda c, b: (0, 0), pipeline_mode=once)

    return pl.pallas_call(
        _hyena_fft_kernel,
        grid=(nct, B),
        in_specs=[const((2 * n2, 2 * n2)), const((2 * n2, 2 * n2)), const((n2, wide)), const((n2, wide)),
                  xspec(0), xspec(1), xspec(2), wspec(0), wspec(1), wspec(2),
                  kspec(0), kspec(0), kspec(1), kspec(1),
                  pl.BlockSpec((8, ct), lambda c, b: (0, c))],
        out_specs=pl.BlockSpec((1, L, ct), lambda c, b: (b, 0, c)),
        out_shape=jax.ShapeDtypeStruct((B, L, B_WIDTH), bf16),
        scratch_shapes=[pltpu.VMEM((L, ct), f32), pltpu.VMEM((L, ct), f32),
                        pltpu.VMEM((2 * n2, wide), bf16), pltpu.VMEM((2 * n2, wide), bf16),
                        pltpu.VMEM((2 * n2, wide), f32)],
        compiler_params=_cparams("parallel", "parallel"),
        name="hyena_fft",
    )(m2f, m2i, twr, twi, hx, hx, hx, conv_w, conv_w, conv_w, kr, ki, kr, ki, dbias)


RT_E1, RT_E2, RT_W1, RT_W2, RT_R1, RT_R2 = 0, 1, 2, 3, 4, 5
ROUTE_TM = ROW_TM
PROJ_PARTS = 2
EXPERT_TM = 512
SC_WINDOW = 128


def _pack_bf16_pairs(x):
    n = x.shape[1] // 2
    bits = lax.bitcast_convert_type(x.astype(bf16).astype(f32), jnp.uint32)
    return bits[:, :n] | (bits[:, n:] >> 16)


def _unpack_bf16_pairs(p):
    hi = lax.bitcast_convert_type(p & jnp.uint32(0xFFFF0000), f32)
    lo = lax.bitcast_convert_type(p << 16, f32)
    return hi, lo


def _route_topk(logits):
    lane = lax.broadcasted_iota(jnp.int32, logits.shape, 1)
    neg = -jnp.inf
    big = ROUTER_LANES
    gl = jnp.where(lane < N_GROUPS, logits, neg)
    gmax = jnp.max(gl, axis=-1, keepdims=True)
    gsel = jnp.min(jnp.where(gl == gmax, lane, big), axis=-1, keepdims=True)
    psel = 1.0 / jnp.sum(jnp.exp(gl - gmax), axis=-1, keepdims=True)
    first = N_GROUPS + gsel * EXPERTS_PER_GROUP
    el = jnp.where((lane >= first) & (lane < first + EXPERTS_PER_GROUP), logits, neg)
    t1 = jnp.max(el, axis=-1, keepdims=True)
    i1 = jnp.min(jnp.where(el == t1, lane, big), axis=-1, keepdims=True)
    el2 = jnp.where(lane == i1, neg, el)
    t2 = jnp.max(el2, axis=-1, keepdims=True)
    i2 = jnp.min(jnp.where(el2 == t2, lane, big), axis=-1, keepdims=True)
    ex = jnp.exp(t2 - t1)
    w1 = psel / (1.0 + ex)
    w2 = ex * w1
    e1 = i1 - N_GROUPS
    e2 = i2 - N_GROUPS
    onehot = jnp.where((lane == e1) | (lane == e2), 1.0, 0.0)
    return e1, e2, w1, w2, onehot


def _route_record(e1, e2, w1, w2, before):
    lane = lax.broadcasted_iota(jnp.int32, before.shape, 1)
    r1 = jnp.sum(jnp.where(lane == e1, before, 0.0), axis=-1, keepdims=True)
    r2 = jnp.sum(jnp.where(lane == e2, before, 0.0), axis=-1, keepdims=True)
    rec = jnp.zeros(before.shape, f32)
    for ln, val in ((RT_E1, e1.astype(f32)), (RT_E2, e2.astype(f32)), (RT_W1, w1), (RT_W2, w2), (RT_R1, r1), (RT_R2, r2)):
        rec = jnp.where(lane == ln, val, rec)
    return rec


def _proj_route_kernel(x_ref, *refs, n_in):
    a_refs = refs[0:n_in]
    w_refs = refs[n_in:2 * n_in]
    g_ref, wr_ref, low_ref, xo_ref, hpa_ref, hpb_ref, rt_ref, rtt_ref, cnt_ref = refs[2 * n_in:]
    tm = xo_ref.shape[0]
    rp = tm // PROJ_PARTS
    parts = [slice(p * rp, (p + 1) * rp) for p in range(PROJ_PARTS)]
    xs = []
    for r in parts:
        x = x_ref[r, :]
        for a_ref, w_ref in zip(a_refs, w_refs):
            x = x + _dot(a_ref[r, :], w_ref[...])
        xo_ref[r, :] = x
        xs.append(x)
    hs = [_rms(x, g_ref[...]) for x in xs]
    for r, h in zip(parts, hs):
        hp = _pack_bf16_pairs(h)
        quarter = hp.shape[1] // 2
        hpa_ref[r, :] = hp[:, :quarter]
        hpb_ref[r, :] = hp[:, quarter:]
    tops = []
    for h in hs:
        h_hi = h.astype(bf16)
        h_lo = (h - h_hi.astype(f32)).astype(bf16)
        both = _dot(h_hi, wr_ref[...])
        logits = both[:, :ROUTER_LANES] + both[:, ROUTER_LANES:] + _dot(h_lo, wr_ref[:, :ROUTER_LANES])
        tops.append(_route_topk(logits))
    onehot16 = jnp.concatenate([t[4] for t in tops], axis=0).astype(bf16)
    for r, (e1, e2, w1, w2, onehot) in zip(parts, tops):
        before = _dot(low_ref[r, :], onehot16)
        rec = _route_record(e1, e2, w1, w2, before)
        rt_ref[r, :] = rec
        rtt_ref[:, r] = rec.T[0:rtt_ref.shape[0], :]
    cnt_ref[0] = jnp.broadcast_to(before[rp - 1:rp, :] + onehot[rp - 1:rp, :], cnt_ref.shape[1:])


def _proj_route(x, acts, ws, gain, w_router):
    T, D = x.shape
    tm = ROUTE_TM
    n_in = len(acts)
    low = jnp.asarray(np.tril(np.ones((tm, tm), np.float32), -1), dtype=bf16)
    in_specs = [pl.BlockSpec((tm, D), lambda i: (i, 0))]
    in_specs += [pl.BlockSpec((tm, a.shape[1]), lambda i: (i, 0)) for a in acts]
    in_specs += [pl.BlockSpec(w.shape, lambda i: (0, 0)) for w in ws]
    in_specs += [pl.BlockSpec((1, D), lambda i: (0, 0)),
                 pl.BlockSpec((D, 2 * ROUTER_LANES), lambda i: (0, 0)),
                 pl.BlockSpec((tm, tm), lambda i: (0, 0))]
    return pl.pallas_call(
        functools.partial(_proj_route_kernel, n_in=n_in),
        grid=(T // tm,),
        in_specs=in_specs,
        out_specs=[pl.BlockSpec((tm, D), lambda i: (i, 0)),
                   pl.BlockSpec((tm, D // 4), lambda i: (i, 0)),
                   pl.BlockSpec((tm, D // 4), lambda i: (i, 0)),
                   pl.BlockSpec((tm, ROUTER_LANES), lambda i: (i, 0)),
                   pl.BlockSpec((8, tm), lambda i: (0, i)),
                   pl.BlockSpec((1, 8, ROUTER_LANES), lambda i: (i, 0, 0))],
        out_shape=[jax.ShapeDtypeStruct((T, D), f32),
                   jax.ShapeDtypeStruct((T, D // 4), jnp.uint32),
                   jax.ShapeDtypeStruct((T, D // 4), jnp.uint32),
                   jax.ShapeDtypeStruct((T, ROUTER_LANES), f32),
                   jax.ShapeDtypeStruct((8, T), f32),
                   jax.ShapeDtypeStruct((T // tm, 8, ROUTER_LANES), f32)],
        compiler_params=_cparams("parallel"),
        name="proj_route",
    )(x, *acts, *ws, gain.reshape(1, D), w_router, low)


def _moe_rows(T):
    return 2 * T + N_EXPERTS * EXPERT_TM


def _moe_plan(route_t, counts):
    T = route_t.shape[1]
    tm = ROUTE_TM
    cnt = counts[:, 0, :N_EXPERTS].astype(jnp.int32)
    total = jnp.sum(cnt, axis=0)
    padded = ((total + EXPERT_TM - 1) // EXPERT_TM) * EXPERT_TM
    ends = jnp.cumsum(padded)
    tile_base = (ends - padded)[None, :] + jnp.cumsum(cnt, axis=0) - cnt
    base = jnp.broadcast_to(tile_base.T[:, :, None], (N_EXPERTS, T // tm, tm)).reshape(N_EXPERTS, T)
    ids = jnp.arange(N_EXPERTS, dtype=jnp.int32)[:, None]
    dest = []
    for e_row, r_row in ((RT_E1, RT_R1), (RT_E2, RT_R2)):
        e = route_t[e_row].astype(jnp.int32)
        dest.append(jnp.sum(jnp.where(ids == e[None, :], base, 0), axis=0) + route_t[r_row].astype(jnp.int32))
    n_tiles = _moe_rows(T) // EXPERT_TM
    n_used = ends[-1] // EXPERT_TM
    starts = jnp.arange(n_tiles, dtype=jnp.int32) * EXPERT_TM
    tile_expert = jnp.sum((starts[:, None] >= ends[None, :]).astype(jnp.int32), axis=1)
    last_expert = jnp.sum((jnp.maximum(n_used - 1, 0) * EXPERT_TM >= ends).astype(jnp.int32))
    tile_expert = jnp.minimum(tile_expert, last_expert)
    return dest, tile_expert, n_used.reshape(1).astype(jnp.int32)


def _sc_mesh():
    return plsc.VectorSubcoreMesh(core_axis_name="core", subcore_axis_name="subcore")


def _sc_scatter_pair(rows, d0, d1, n_out):
    T, W = rows.shape
    win = SC_WINDOW

    @pl.kernel(out_type=jax.ShapeDtypeStruct((n_out, W), rows.dtype), mesh=_sc_mesh(), name="moe_sc_scatter")
    def scatter(x_hbm, i0_hbm, i1_hbm, o_hbm):
        def body(x_vmem, i0_vmem, i1_vmem):
            pltpu.sync_copy(x_vmem, o_hbm.at[i0_vmem.at[0]])
            pltpu.sync_copy(x_vmem, o_hbm.at[i1_vmem.at[0]])

        pltpu.emit_pipeline(
            body, grid=(T // win,),
            in_specs=[pl.BlockSpec((win, W), index_map=lambda i: (i, 0)),
                      pl.BlockSpec((1, win), index_map=lambda i: (0, i)),
                      pl.BlockSpec((1, win), index_map=lambda i: (0, i))],
            out_specs=[],
            core_axis_name=("core", "subcore"), dimension_semantics=(pltpu.PARALLEL,),
        )(x_hbm, i0_hbm, i1_hbm)

    return scatter(rows, d0.reshape(1, T), d1.reshape(1, T))


def _sc_gather(table, idx):
    n = idx.shape[0]
    W = table.shape[1]
    win = SC_WINDOW

    @pl.kernel(out_type=jax.ShapeDtypeStruct((n, W), table.dtype), mesh=_sc_mesh(), name="moe_sc_gather")
    def gather(t_hbm, i_hbm, o_hbm):
        def body(i_vmem, o_vmem):
            pltpu.sync_copy(t_hbm.at[i_vmem.at[0]], o_vmem)

        pltpu.emit_pipeline(
            body, grid=(n // win,),
            in_specs=[pl.BlockSpec((1, win), index_map=lambda i: (0, i))],
            out_specs=[pl.BlockSpec((win, W), index_map=lambda i: (i, 0))],
            core_axis_name=("core", "subcore"), dimension_semantics=(pltpu.PARALLEL,),
        )(i_hbm, o_hbm)

    return gather(table, idx.reshape(1, n))


def _expert_kernel(te_ref, nu_ref, xa_ref, xb_ref, wg_ref, wu_ref, wd_ref, ya_ref, yb_ref):
    @pl.when(pl.program_id(0) < nu_ref[0])
    def _():
        hi, lo = _unpack_bf16_pairs(jnp.concatenate([xa_ref[...], xb_ref[...]], axis=1))
        x = jnp.concatenate([hi, lo], axis=1).astype(bf16)
        gate = _dot(x, wg_ref[0].astype(bf16))
        up = _dot(x, wu_ref[0].astype(bf16))
        hg = (gate * _sigmoid(gate) * up).astype(bf16)
        y = _pack_bf16_pairs(_dot(hg, wd_ref[0].astype(bf16)))
        quarter = y.shape[1] // 2
        ya_ref[...] = y[:, :quarter]
        yb_ref[...] = y[:, quarter:]

    @pl.when(pl.program_id(0) >= nu_ref[0])
    def _():
        ya_ref[...] = jnp.zeros(ya_ref.shape, ya_ref.dtype)
        yb_ref[...] = jnp.zeros(yb_ref.shape, yb_ref.dtype)


def _experts(tile_expert, n_used, xa, xb, wg, wu, wd):
    n_rows, quarter = xa.shape
    D = 4 * quarter
    tm = EXPERT_TM
    grid_spec = pltpu.PrefetchScalarGridSpec(
        num_scalar_prefetch=2,
        grid=(n_rows // tm,),
        in_specs=[pl.BlockSpec((tm, quarter), lambda i, te, nu: (i, 0)),
                  pl.BlockSpec((tm, quarter), lambda i, te, nu: (i, 0)),
                  pl.BlockSpec((1, D, D_EXPERT), lambda i, te, nu: (te[i], 0, 0)),
                  pl.BlockSpec((1, D, D_EXPERT), lambda i, te, nu: (te[i], 0, 0)),
                  pl.BlockSpec((1, D_EXPERT, D), lambda i, te, nu: (te[i], 0, 0))],
        out_specs=[pl.BlockSpec((tm, quarter), lambda i, te, nu: (i, 0)),
                   pl.BlockSpec((tm, quarter), lambda i, te, nu: (i, 0))])
    return pl.pallas_call(
        _expert_kernel,
        grid_spec=grid_spec,
        out_shape=[jax.ShapeDtypeStruct((n_rows, quarter), jnp.uint32)] * 2,
        compiler_params=_cparams("arbitrary"),
        name="moe_experts",
    )(tile_expert, n_used, xa, xb, wg, wu, wd)


def _combine_kernel(x_ref, rt_ref, g_ref, a1_ref, b1_ref, a2_ref, b2_ref, *rest, final):
    half = x_ref.shape[1] // 2
    h1, l1 = _unpack_bf16_pairs(jnp.concatenate([a1_ref[...], b1_ref[...]], axis=1))
    h2, l2 = _unpack_bf16_pairs(jnp.concatenate([a2_ref[...], b2_ref[...]], axis=1))
    w1 = rt_ref[:, RT_W1:RT_W1 + 1]
    w2 = rt_ref[:, RT_W2:RT_W2 + 1]
    out = jnp.concatenate([x_ref[:, 0:half] + w1 * h1 + w2 * h2, x_ref[:, half:] + w1 * l1 + w2 * l2], axis=1)
    if final:
        (o_ref,) = rest
        o_ref[...] = _rms(out, g_ref[...])
    else:
        w_ref, o_ref, p_ref = rest
        o_ref[...] = out
        p_ref[...] = _dot(_rms(out, g_ref[...]).astype(bf16), w_ref[...]).astype(p_ref.dtype)


def _combine(x, route, ga, gb, gain, w_next=None):
    T, D = x.shape
    tm = ROUTE_TM
    n = T // tm
    quarter = D // 4
    final = w_next is None
    slot1 = pl.BlockSpec((tm, quarter), lambda i: (i, 0))
    slot2 = pl.BlockSpec((tm, quarter), lambda i: (i + n, 0))
    row_spec = pl.BlockSpec((tm, D), lambda i: (i, 0))
    in_specs = [row_spec, pl.BlockSpec((tm, ROUTER_LANES), lambda i: (i, 0)), pl.BlockSpec((1, D), lambda i: (0, 0)),
                slot1, slot1, slot2, slot2]
    args = [x, route, gain.reshape(1, D), ga, gb, ga, gb]
    out_specs = [row_spec]
    out_shape = [jax.ShapeDtypeStruct((T, D), f32)]
    if not final:
        n_next = w_next.shape[1]
        in_specs.append(pl.BlockSpec((D, n_next), lambda i: (0, 0)))
        args.append(w_next)
        out_specs.append(pl.BlockSpec((tm, n_next), lambda i: (i, 0)))
        out_shape.append(jax.ShapeDtypeStruct((T, n_next), bf16))
    outs = pl.pallas_call(
        functools.partial(_combine_kernel, final=final),
        grid=(n,),
        in_specs=in_specs,
        out_specs=out_specs,
        out_shape=out_shape,
        compiler_params=_cparams("parallel"),
        name="moe_combine",
    )(*args)
    return outs[0] if final else outs


def _moe(layer, x, hpa, hpb, route, route_t, counts, wg, wu, wd, gain, w_next=None):
    (d0, d1), tile_expert, n_used = _moe_plan(route_t, counts)
    n_rows = _moe_rows(x.shape[0])
    xa = _sc_scatter_pair(hpa, d0, d1, n_rows)
    xb = _sc_scatter_pair(hpb, d0, d1, n_rows)
    ya, yb = _experts(tile_expert + layer * N_EXPERTS, n_used, xa, xb, wg, wu, wd)
    dcat = jnp.concatenate([d0, d1])
    return _combine(x, route, _sc_gather(ya, dcat), _sc_gather(yb, dcat), gain, w_next)


SWA_LOCKSTEP = 2


def _swa_bias_tables():
    blk = C_BLOCK
    qi = np.arange(blk)[:, None]
    kj = np.arange(3 * blk)[None, :]
    dist = np.abs(blk + qi - kj)
    tabs = np.full((3, C_KV_HEADS, C_GROUP * blk, 3 * blk), -np.inf, np.float32)
    for case in range(3):
        ok = dist <= WINDOW
        if case == 0:
            ok = ok & (kj >= blk)
        if case == 2:
            ok = ok & (kj < 2 * blk)
        for hh in range(C_HEADS):
            slope = 2.0 ** (-8.0 * (hh + 1) / C_HEADS)
            t = np.where(ok, -slope * LOG2E * dist, -np.inf)
            kvh, g = divmod(hh, C_GROUP)
            tabs[case, kvh, g * blk:(g + 1) * blk] = t
    return jnp.asarray(tabs)


def _swa_kernel(sink_ref, q_ref, kp_ref, kc_ref, kn_ref, vp_ref, vc_ref, vn_ref, bias_ref, o_ref):
    blk = C_BLOCK
    hd = C_HEAD_DIM
    kcat = jnp.concatenate([kp_ref[0], kc_ref[0], kn_ref[0]], axis=0)
    vcat = jnp.concatenate([vp_ref[0], vc_ref[0], vn_ref[0]], axis=0)
    q = q_ref[0]
    rblk = lax.broadcasted_iota(jnp.int32, (C_GROUP * blk, 1), 0) // blk
    lane = lax.broadcasted_iota(jnp.int32, (3 * blk, LANE - hd), 1)
    ones_pad = jnp.where(lane == 0, 1.0, 0.0).astype(bf16)
    q4s, sinks = [], []
    for kvh in range(C_KV_HEADS):
        q4s.append(jnp.concatenate(
            [q[:, (kvh * C_GROUP + g) * hd:(kvh * C_GROUP + g + 1) * hd] for g in range(C_GROUP)], axis=0))
        sk = jnp.zeros((C_GROUP * blk, 1), f32)
        for g in range(C_GROUP):
            sk = jnp.where(rblk == g, sink_ref[kvh * C_GROUP + g] * LOG2E, sk)
        sinks.append(sk)
    outs = []
    for h0 in range(0, C_KV_HEADS, SWA_LOCKSTEP):
        hs = range(h0, h0 + SWA_LOCKSTEP)
        ss = [_dot_nt(q4s[h], kcat[:, h * hd:(h + 1) * hd]) + bias_ref[0, h] for h in hs]
        ms = [jnp.maximum(jnp.max(s, axis=-1, keepdims=True), sinks[h]) for s, h in zip(ss, hs)]
        ps = [jnp.exp2(s - m).astype(bf16) for s, m in zip(ss, ms)]
        pvs = [_dot(p, jnp.concatenate([vcat[:, h * hd:(h + 1) * hd], ones_pad], axis=1)) for p, h in zip(ps, hs)]
        for pv, m, h in zip(pvs, ms, hs):
            den = pv[:, hd:hd + 1] + jnp.exp2(sinks[h] - m)
            o4 = pv[:, 0:hd] / den
            outs += [o4[g * blk:(g + 1) * blk, :] for g in range(C_GROUP)]
    o_ref[0] = jnp.concatenate(outs, axis=1).astype(o_ref.dtype)


def _swa(qkv, sinks):
    B, L, _ = qkv.shape
    nb = L // C_BLOCK
    dq = C_HEADS * C_HEAD_DIM
    dkv = C_KV_HEADS * C_HEAD_DIM
    k_col = dq // dkv
    v_col = k_col + 1

    def kv_spec(col, shift):
        return pl.BlockSpec((1, C_BLOCK, dkv),
                            lambda n, b: (b, jnp.clip(n + shift, 0, nb - 1), col))

    def edge_case(n, b):
        return (jnp.where(n == 0, 0, jnp.where(n == nb - 1, 2, 1)), 0, 0, 0)

    return pl.pallas_call(
        _swa_kernel,
        grid=(nb, B),
        in_specs=[pl.BlockSpec(memory_space=pltpu.SMEM),
                  pl.BlockSpec((1, C_BLOCK, dq), lambda n, b: (b, n, 0)),
                  kv_spec(k_col, -1), kv_spec(k_col, 0), kv_spec(k_col, 1),
                  kv_spec(v_col, -1), kv_spec(v_col, 0), kv_spec(v_col, 1),
                  pl.BlockSpec((1, C_KV_HEADS, C_GROUP * C_BLOCK, 3 * C_BLOCK), edge_case)],
        out_specs=pl.BlockSpec((1, C_BLOCK, dq), lambda n, b: (b, n, 0)),
        out_shape=jax.ShapeDtypeStruct((B, L, dq), bf16),
        compiler_params=_cparams("parallel", "parallel"),
        name="swa",
    )(sinks, qkv, qkv, qkv, qkv, qkv, qkv, qkv, _swa_bias_tables())


def _hyena_filters(L, w1, b1, w2, b2, w3, b3, w4, freq):
    hp = lax.Precision.HIGHEST
    t = jnp.linspace(0.0, 1.0, L, dtype=f32)[:, None]
    wpos = (2.0 * math.pi / L) * jnp.arange(L, dtype=f32)[:, None]
    bands = jnp.linspace(1e-4, HY_BANDS - 1, HY_BANDS, dtype=f32)[None, :]
    feats = jnp.concatenate([t, jnp.cos(bands * wpos), -jnp.sin(bands * wpos)], axis=-1)
    h = jnp.sin(freq[0] * (jnp.dot(feats, w1, precision=hp) + b1))
    h = jnp.sin(freq[1] * (jnp.dot(h, w2, precision=hp) + b2))
    h = jnp.sin(freq[2] * (jnp.dot(h, w3, precision=hp) + b3))
    h = jnp.dot(h, w4, precision=hp).reshape(L, HY_ORDER, 2, B_WIDTH)
    deltas = jnp.abs(jnp.linspace(HY_MIN_DECAY, HY_MAX_DECAY, B_WIDTH, dtype=f32))
    window = jnp.exp(-t * deltas[None, :])
    return h * window[:, None, None, :]


def _pad_rows(w, rows=8):
    return jnp.pad(w, ((0, rows - w.shape[0]), (0, 0)))


def _even_in_weights(w_in):
    hk = A_HEADS * A_DK
    q = w_in[:, 0:hk].reshape(D_MODEL, A_HEADS, A_DK)
    k = w_in[:, hk:2 * hk].reshape(D_MODEL, A_HEADS, A_DK)
    v = w_in[:, 2 * hk:A_QKV].reshape(D_MODEL, A_HEADS, A_DV)
    z = w_in[:, A_QKV:A_QKV + A_WIDTH].reshape(D_MODEL, A_HEADS, A_DV)
    qkvz = jnp.concatenate([q, k, v, z], axis=-1).reshape(D_MODEL, A_HEADS * 4 * A_DK)
    ab = w_in[:, A_QKV + A_WIDTH:A_QKV + A_WIDTH + 4 * A_HEADS].reshape(D_MODEL, 4, A_HEADS)
    ab = jnp.transpose(ab, (0, 2, 1))
    ab = jnp.pad(ab, ((0, 0), (0, 0), (0, LANE - 4))).reshape(D_MODEL, A_HEADS * LANE)
    hx = w_in[:, A_QKV + A_WIDTH + 4 * A_HEADS:]
    return jnp.concatenate([qkvz, ab, hx], axis=1).astype(bf16)


def _router_weights(w_group, w_expert):
    w = jnp.concatenate([w_group, w_expert], axis=1)
    w = jnp.pad(w, ((0, 0), (0, ROUTER_LANES - w.shape[1])))
    hi = w.astype(bf16)
    lo = (w - hi.astype(f32)).astype(bf16)
    return jnp.concatenate([hi, lo], axis=1)


def kernel(x_prompt, x_sample, norm_mix, norm_ffn, norm_final, ev_w_in, ev_conv_a, ev_alog_f, ev_alog_b, ev_dtb_f, ev_dtb_b, ev_onorm, ev_conv_b, hy_w1, hy_b1, hy_w2, hy_b2, hy_w3, hy_b3, hy_w4, hy_freq, hy_dbias, ev_w_out, od_w_qkv, od_sinks, od_w_out, moe_w_group, moe_w_expert, moe_w_gate, moe_w_up, moe_w_down):
    L = x_prompt.shape[1]
    n_layers = moe_w_gate.shape[0]
    wg = moe_w_gate.reshape(n_layers * N_EXPERTS, D_MODEL, D_EXPERT)
    wu = moe_w_up.reshape(n_layers * N_EXPERTS, D_MODEL, D_EXPERT)
    wd = moe_w_down.reshape(n_layers * N_EXPERTS, D_EXPERT, D_MODEL)

    w_in = _even_in_weights(ev_w_in[0])
    n_qkvz = A_HEADS * 4 * A_DK
    n_ab = A_HEADS * LANE
    n_hx = 3 * B_WIDTH
    hk = A_HEADS * A_DK
    ca = ev_conv_a[0]
    conv_a = jnp.concatenate([ca[0:hk].reshape(A_HEADS, A_DK, 3), ca[hk:2 * hk].reshape(A_HEADS, A_DK, 3),
                              ca[2 * hk:].reshape(A_HEADS, A_DV, 3)], axis=1)
    conv_a = jnp.pad(jnp.transpose(conv_a, (0, 2, 1)), ((0, 0), (0, 5), (0, 0)))
    gate_sc = jnp.stack([ev_alog_f[0], ev_alog_b[0], ev_dtb_f[0], ev_dtb_b[0]], axis=1)
    filt = _hyena_filters(L, hy_w1[0], hy_b1[0], hy_w2[0], hy_b2[0], hy_w3[0], hy_b3[0], hy_w4[0], hy_freq[0])
    hsum = (filt[:, :, 0] + filt[:, :, 1]).reshape(L, HY_ORDER * B_WIDTH)
    hdiff = (filt[:, :, 0] - filt[:, :, 1]).reshape(L, HY_ORDER * B_WIDTH)
    kr, ki = _filter_spectrum(hsum, hdiff)
    fft_tables = _hyena_tables(L)
    conv_b = _pad_rows(ev_conv_b[0].T)
    dbias = _pad_rows(hy_dbias[0])
    w_out = ev_w_out[0].astype(bf16)
    routers = [_router_weights(moe_w_group[l], moe_w_expert[l]) for l in range(n_layers)]
    n_qkv = od_w_qkv.shape[-1]
    dq = C_HEADS * C_HEAD_DIM
    w_qkv = jnp.concatenate([od_w_qkv[0][:, :dq] * (LOG2E * C_HEAD_DIM ** -0.5), od_w_qkv[0][:, dq:]], axis=1)
    w_qkv = w_qkv.astype(bf16)
    w_att_out = od_w_out[0].astype(bf16)

    def trunk(xb):
        B = xb.shape[0]
        T = B * L
        x = xb.reshape(T, D_MODEL)
        qkvz, abh, hx = _norm_matmul(x, norm_mix[0], w_in, (n_qkvz, n_ab, n_hx), (bf16, f32, bf16))
        o_a = _deltanet(qkvz.reshape(B, L, n_qkvz), abh.reshape(B, L, n_ab), conv_a, gate_sc, ev_onorm[0])
        o_b = _hyena_fft(hx.reshape(B, L, n_hx), conv_b, fft_tables, kr, ki, dbias)
        x, hpa, hpb, route, route_t, counts = _proj_route(
            x, (o_a.reshape(T, A_WIDTH), o_b.reshape(T, B_WIDTH)), (w_out[:A_WIDTH], w_out[A_WIDTH:]),
            norm_ffn[0], routers[0])
        x, qkv = _moe(0, x, hpa, hpb, route, route_t, counts, wg, wu, wd, norm_mix[1], w_qkv)
        o_c = _swa(qkv.reshape(B, L, n_qkv), od_sinks[0])
        x, hpa, hpb, route, route_t, counts = _proj_route(
            x, (o_c.reshape(T, C_HEADS * C_HEAD_DIM),), (w_att_out,), norm_ffn[1], routers[1])
        y = _moe(1, x, hpa, hpb, route, route_t, counts, wg, wu, wd, norm_final)
        return y.reshape(B, L, D_MODEL)

    return (trunk(x_prompt), trunk(x_sample))
```

```python
import functools
import math

import jax
import jax.numpy as jnp
import numpy as np
from jax import lax
from jax.experimental import pallas as pl
from jax.experimental.pallas import tpu as pltpu
from jax.experimental.pallas import tpu_sc as plsc

f32 = jnp.float32
bf16 = jnp.bfloat16

EPS = 1e-6
D_MODEL = 1024

A_HEADS = 4
A_DK = 128
A_DV = 128
A_WIDTH = A_HEADS * A_DV
A_QKV = 2 * A_HEADS * A_DK + A_HEADS * A_DV
DELTA_CHUNK = 128
INV_BASE = 8
DELTA_GROUP = 8

B_WIDTH = D_MODEL - A_WIDTH
HY_ORDER = 2
HY_EMB = 33
HY_BANDS = (HY_EMB - 1) // 2
HY_TARGET = 1e-2
HY_MIN_DECAY = math.log(HY_TARGET) / 1.5
HY_MAX_DECAY = math.log(HY_TARGET) / 0.3
HY_CT = 256
HY_ROWS = 256

C_HEADS = 16
C_KV_HEADS = 4
C_HEAD_DIM = 64
C_GROUP = C_HEADS // C_KV_HEADS
WINDOW = 128
C_BLOCK = 128

N_GROUPS = 4
EXPERTS_PER_GROUP = 8
N_EXPERTS = N_GROUPS * EXPERTS_PER_GROUP
D_EXPERT = 256
ROUTER_LANES = 128

LANE = 128
ROW_TM = 512
VMEM_LIMIT = 56 * 1024 * 1024
LOG2E = 1.4426950408889634


def _cparams(*sem):
    return pltpu.CompilerParams(dimension_semantics=sem, vmem_limit_bytes=VMEM_LIMIT)


def _dot(a, b):
    return jnp.dot(a, b, preferred_element_type=f32)


def _dot_nt(a, b):
    return lax.dot_general(a, b, (((1,), (1,)), ((), ())), preferred_element_type=f32)


def _dot_tn(a, b):
    return lax.dot_general(a, b, (((0,), (0,)), ((), ())), preferred_element_type=f32)


def _sigmoid(x):
    return 1.0 / (1.0 + jnp.exp(-x))


def _rms(x, g):
    return x * lax.rsqrt(jnp.mean(x * x, axis=-1, keepdims=True) + EPS) * g


def _norm_matmul_kernel(x_ref, g_ref, w_ref, *o_refs):
    h = _rms(x_ref[...], g_ref[...]).astype(bf16)
    off = 0
    for o_ref in o_refs:
        n = o_ref.shape[-1]
        o_ref[...] = _dot(h, w_ref[:, off:off + n]).astype(o_ref.dtype)
        off += n


def _norm_matmul(x, gain, w, splits, dtypes):
    T, D = x.shape
    tm = ROW_TM
    n_all = w.shape[1]
    return pl.pallas_call(
        _norm_matmul_kernel,
        grid=(T // tm,),
        in_specs=[pl.BlockSpec((tm, D), lambda i: (i, 0)),
                  pl.BlockSpec((1, D), lambda i: (0, 0)),
                  pl.BlockSpec((D, n_all), lambda i: (0, 0))],
        out_specs=[pl.BlockSpec((tm, n), lambda i: (i, 0)) for n in splits],
        out_shape=[jax.ShapeDtypeStruct((T, n), dt) for n, dt in zip(splits, dtypes)],
        compiler_params=_cparams("parallel"),
        name="norm_matmul",
    )(x, gain.reshape(1, D), w)


MK_LOWER, MK_UPPER, MK_BASE, MK_LEVEL0 = 0, 1, 2, 3
N_LEVELS = int(math.log2(DELTA_CHUNK // INV_BASE))
MK_EYE = MK_LEVEL0 + N_LEVELS
N_MASKS = MK_EYE + 1


def _delta_masks():
    c = np.arange(DELTA_CHUNK)[:, None]
    s = np.arange(DELTA_CHUNK)[None, :]
    m = [s <= c, s >= c, (c // INV_BASE == s // INV_BASE) & (c != s)]
    b = INV_BASE
    while b < DELTA_CHUNK:
        m.append((c // (2 * b) == s // (2 * b)) & (c // b != s // b))
        b *= 2
    m.append(c == s)
    m = np.stack(m).astype(np.float32)
    return jnp.asarray(m), jnp.asarray(m, dtype=bf16)


def _tri_inverse(mats, mk_ref, mk16_ref):
    ps = [a * mk_ref[MK_BASE] for a in mats]
    ts = [mk_ref[MK_EYE] - p for p in ps]
    n = 2
    while n < INV_BASE:
        p16s = [p.astype(bf16) for p in ps]
        ps = [_dot(p16, p16) for p16 in p16s]
        ts = [t + _dot(t.astype(bf16), p.astype(bf16)) for t, p in zip(ts, ps)]
        n *= 2
    a16s = [a.astype(bf16) for a in mats]
    t16s = [t.astype(bf16) for t in ts]
    for lvl in range(N_LEVELS):
        xs = [_dot(a16 * mk16_ref[MK_LEVEL0 + lvl], t16) for a16, t16 in zip(a16s, t16s)]
        t16s = [t16 - _dot(t16, x.astype(bf16)).astype(bf16) for t16, x in zip(t16s, xs)]
    return t16s


def _deltanet_kernel(sc_ref, x_ref, ab_ref, cw_ref, on_ref, mk_ref, mk16_ref, o_ref, mp_s, nr_s, gl_s, o_s):
    hd = pl.program_id(1)
    L = x_ref.shape[1]
    C = DELTA_CHUNK
    nchunk = L // C
    group = math.gcd(DELTA_GROUP, nchunk)
    nqkv = 3 * A_DK
    halo = 16

    lane = lax.broadcasted_iota(jnp.int32, (1, LANE), 1)
    alog = jnp.where(lane == 2, sc_ref[hd, 0], jnp.where(lane == 3, sc_ref[hd, 1], 0.0))
    dtb = jnp.where(lane == 2, sc_ref[hd, 2], jnp.where(lane == 3, sc_ref[hd, 3], 0.0))
    neg_a = jnp.where((lane == 2) | (lane == 3), -jnp.exp(alog), 0.0)
    rowi = lax.broadcasted_iota(jnp.int32, (C, 1), 0)
    cw = cw_ref[0]

    def chunk_inputs(i):
        r0 = pl.multiple_of(i * C, C)
        xc = x_ref[0, pl.ds(r0, C), 0:nqkv].astype(f32)
        rp = pl.multiple_of(jnp.maximum(r0 - halo, 0), halo)
        rn = pl.multiple_of(jnp.minimum(r0 + C, L - halo), halo)
        prow = x_ref[0, pl.ds(rp, halo), 0:nqkv][halo - 1:halo, :].astype(f32)
        nrow = x_ref[0, pl.ds(rn, halo), 0:nqkv][0:1, :].astype(f32)
        prow = jnp.where(i > 0, prow, 0.0)
        nrow = jnp.where(i < nchunk - 1, nrow, 0.0)
        prev = jnp.where(rowi == 0, prow, pltpu.roll(xc, 1, 0))
        nxt = jnp.where(rowi == C - 1, nrow, pltpu.roll(xc, C - 1, 0))
        y = prev * cw[0:1, :] + xc * cw[1:2, :] + nxt * cw[2:3, :]
        y = y * _sigmoid(y)
        q = y[:, 0:A_DK]
        k = y[:, A_DK:2 * A_DK]
        vc = y[:, 2 * A_DK:nqkv]
        qc = q * lax.rsqrt(jnp.sum(q * q, axis=-1, keepdims=True) + EPS) * (A_DK ** -0.5)
        kc = k * lax.rsqrt(jnp.sum(k * k, axis=-1, keepdims=True) + EPS)

        abc = ab_ref[0, pl.ds(r0, C), :]
        zg = abc + dtb
        g = neg_a * (jnp.maximum(zg, 0.0) + jnp.log(1.0 + jnp.exp(-jnp.abs(zg))))
        low16 = mk16_ref[MK_LOWER]
        g1 = g.astype(bf16)
        e1 = g - g1.astype(f32)
        g2 = e1.astype(bf16)
        g3 = (e1 - g2.astype(f32)).astype(bf16)
        pre = _dot(low16, g1) + _dot(low16, g2) + _dot(low16, g3)
        tot = pre[C - 1:C, :]
        suf = tot - pre + g
        beta_all = _sigmoid(abc)

        k16 = kc.astype(bf16)
        kk = _dot_nt(k16, k16)
        qk = _dot_nt(qc.astype(bf16), k16)

        chains = []
        for rev in (0, 1):
            beta = beta_all[:, rev:rev + 1]
            gcc = (suf if rev else pre)[:, 2 + rev:3 + rev]
            tot11 = tot[:, 2 + rev:3 + rev]
            keep = mk_ref[MK_UPPER if rev else MK_LOWER]
            bc = jnp.broadcast_to(gcc * LOG2E, (C, C))
            decay = jnp.exp2((bc - bc.T) * keep) * keep
            a = (beta * kk) * decay
            eg = jnp.exp(gcc)
            chains.append(dict(
                i=i, rev=rev, a=a, gl=jnp.exp(tot11), qd=qc * eg,
                rhs=jnp.concatenate([vc * beta, kc * (beta * eg)], axis=1).astype(bf16),
                kd16=(kc * jnp.exp(tot11 - gcc)).astype(bf16),
                qk16=(qk * decay).astype(bf16)))
        return chains

    def prep(it, carry):
        chains = []
        for gi in range(group):
            chains += chunk_inputs(it * group + gi)
        t16s = _tri_inverse([c["a"] for c in chains], mk_ref, mk16_ref)
        uws = [_dot(t16, c["rhs"]).astype(bf16) for t16, c in zip(t16s, chains)]
        kts = [_dot_tn(c["kd16"], uw) for c, uw in zip(chains, uws)]
        prs = [_dot(c["qk16"], uw) for c, uw in zip(chains, uws)]
        for c, kt, pr in zip(chains, kts, prs):
            rev, i = c["rev"], c["i"]
            mp_s[rev, i, 0:A_DK, :] = (-kt[:, A_DV:]).astype(bf16)
            mp_s[rev, i, A_DK:A_DK + C, :] = (c["qd"] - pr[:, A_DV:]).astype(bf16)
            nr_s[rev, i, 0:A_DK, :] = kt[:, 0:A_DV]
            nr_s[rev, i, A_DK:A_DK + C, :] = pr[:, 0:A_DV]
            gl_s[rev, i] = jnp.broadcast_to(c["gl"], (8, LANE))
        return carry

    lax.fori_loop(0, nchunk // group, prep, 0)

    def scan(i, carry):
        out = []
        for rev, S in enumerate(carry):
            j = (nchunk - 1 - i) if rev else i
            y = _dot(mp_s[rev, j], S.astype(bf16))
            nr = nr_s[rev, j]
            o_s[rev, pl.ds(pl.multiple_of(j * C, C), C), :] = y[A_DK:, :] + nr[A_DK:, :]
            out.append(S * gl_s[rev, j][0:1, :] + y[0:A_DK, :] + nr[0:A_DK, :])
        return tuple(out)

    S0 = jnp.zeros((A_DK, A_DV), f32)
    lax.fori_loop(0, nchunk, scan, (S0, S0))

    def finish(it, carry):
        for gi in range(group):
            r0 = pl.multiple_of((it * group + gi) * C, C)
            o = o_s[0, pl.ds(r0, C), :] + o_s[1, pl.ds(r0, C), :]
            zg = x_ref[0, pl.ds(r0, C), nqkv:nqkv + A_DV].astype(f32)
            o_ref[0, pl.ds(r0, C), :] = (_rms(o, on_ref[...]) * (zg * _sigmoid(zg))).astype(o_ref.dtype)
        return carry

    lax.fori_loop(0, nchunk // group, finish, 0)


def _deltanet(qkvz, abh, conv_w, gate_sc, onorm):
    B, L, _ = qkvz.shape
    C = DELTA_CHUNK
    nchunk = L // C
    hw = 4 * A_DK
    scratch = [pltpu.VMEM((2, nchunk, A_DK + C, A_DV), bf16),
               pltpu.VMEM((2, nchunk, A_DK + C, A_DV), f32),
               pltpu.VMEM((2, nchunk, 8, LANE), f32),
               pltpu.VMEM((2, L, A_DV), f32)]
    return pl.pallas_call(
        _deltanet_kernel,
        grid=(B, A_HEADS),
        in_specs=[pl.BlockSpec(memory_space=pltpu.SMEM),
                  pl.BlockSpec((1, L, hw), lambda b, h: (b, 0, h)),
                  pl.BlockSpec((1, L, LANE), lambda b, h: (b, 0, h)),
                  pl.BlockSpec((1, 8, 3 * A_DK), lambda b, h: (h, 0, 0)),
                  pl.BlockSpec((1, A_DV), lambda b, h: (0, 0)),
                  pl.BlockSpec((N_MASKS, C, C), lambda b, h: (0, 0, 0)),
                  pl.BlockSpec((N_MASKS, C, C), lambda b, h: (0, 0, 0))],
        out_specs=pl.BlockSpec((1, L, A_DV), lambda b, h: (b, 0, h)),
        out_shape=jax.ShapeDtypeStruct((B, L, A_WIDTH), bf16),
        scratch_shapes=scratch,
        compiler_params=_cparams("parallel", "parallel"),
        name="deltanet",
    )(gate_sc, qkvz, abh, conv_w, onorm.reshape(1, A_DV), *_delta_masks())


def _filter_spectrum_kernel(c_ref, s_ref, hs_ref, hd_ref, kr_ref, ki_ref):
    L = c_ref.shape[1]
    rows = kr_ref.shape[1]
    scale = 1.0 / (2 * L)

    def split(x):
        hi = x.astype(bf16)
        return hi, (x - hi.astype(f32)).astype(bf16)

    hs_hi, hs_lo = split(hs_ref[...])
    hd_hi, hd_lo = split(hd_ref[...])
    for k1 in range(kr_ref.shape[0]):
        c = c_ref[k1 * rows:(k1 + 1) * rows, :]
        s = s_ref[k1 * rows:(k1 + 1) * rows, :]
        kr_ref[k1] = (_dot(c, hs_hi) + _dot(c, hs_lo)) * scale
        ki_ref[k1] = (_dot(s, hd_hi) + _dot(s, hd_lo)) * (-scale)


def _filter_spectrum(hsum, hdiff):
    L, n = hsum.shape
    ct = HY_CT
    n_fft = 2 * L
    n2 = L // FFT_Z1
    d = jnp.arange(L, dtype=jnp.int32)[None, :]
    a1 = ((jnp.arange(FFT_K1, dtype=jnp.int32)[:, None] * d) % n_fft).astype(f32) * (2.0 * math.pi / n_fft)
    a2 = ((jnp.arange(n2, dtype=jnp.int32)[:, None] * d) % n2).astype(f32) * (2.0 * math.pi / n2)
    c1, s1 = jnp.cos(a1)[:, None, :], jnp.sin(a1)[:, None, :]
    c2, s2 = jnp.cos(a2)[None, :, :], jnp.sin(a2)[None, :, :]
    cmat = (c1 * c2 - s1 * s2).reshape(FFT_K1 * n2, L).astype(bf16)
    smat = (s1 * c2 + c1 * s2).reshape(FFT_K1 * n2, L).astype(bf16)
    once = pl.Buffered(1)
    return pl.pallas_call(
        _filter_spectrum_kernel,
        grid=(n // ct,),
        in_specs=[pl.BlockSpec((FFT_K1 * n2, L), lambda j: (0, 0), pipeline_mode=once),
                  pl.BlockSpec((FFT_K1 * n2, L), lambda j: (0, 0), pipeline_mode=once),
                  pl.BlockSpec((L, ct), lambda j: (0, j)),
                  pl.BlockSpec((L, ct), lambda j: (0, j))],
        out_specs=[pl.BlockSpec((FFT_K1, n2, ct), lambda j: (0, 0, j)),
                   pl.BlockSpec((FFT_K1, n2, ct), lambda j: (0, 0, j))],
        out_shape=[jax.ShapeDtypeStruct((FFT_K1, n2, n), f32),
                   jax.ShapeDtypeStruct((FFT_K1, n2, n), f32)],
        compiler_params=_cparams("parallel"),
        name="filter_spectrum",
    )(cmat, smat, hsum, hdiff)


FFT_N1 = 16
FFT_K1 = FFT_N1 // 2 + 1
FFT_Z1 = FFT_N1 // 2
FFT_ROWS = 32


def _lincomb(terms):
    groups = {}
    for c, a in terms:
        if abs(c) > 1e-9:
            groups.setdefault(round(abs(c), 9), []).append((c > 0, a))
    total = None
    for mag, items in groups.items():
        pos = [a for p, a in items if p]
        neg = [a for p, a in items if not p]
        s = None
        for a in pos:
            s = a if s is None else s + a
        for a in neg:
            s = -a if s is None else s - a
        if mag != 1.0:
            s = s * mag
        total = s if total is None else total + s
    return total


def _hyena_fft_kernel(m2f_ref, m2i_ref, twr_ref, twi_ref, x1_ref, x2_ref, v_ref, w1_ref, w2_ref, wv_ref,
                      kr0_ref, ki0_ref, kr1_ref, ki1_ref, db_ref, o_ref, z_s, g_s, d_s, y_s, b_s):
    L, ct = z_s.shape
    n2 = L // FFT_Z1
    R = min(HY_ROWS, L)
    nt = L // R
    r1 = min(FFT_ROWS, n2)
    halo = 16
    rowi = lax.broadcasted_iota(jnp.int32, (R, 1), 0)
    ang = 2.0 * math.pi / FFT_N1
    c16 = [[math.cos(ang * a * b) for b in range(FFT_K1)] for a in range(FFT_Z1)]
    s16 = [[math.sin(ang * a * b) for b in range(FFT_K1)] for a in range(FFT_Z1)]

    def conv_into(dst, x_ref, w_ref):
        def body(t, carry):
            r0 = pl.multiple_of(t * R, R)
            x = x_ref[0, pl.ds(r0, R), :].astype(f32)
            rp = pl.multiple_of(jnp.maximum(r0 - halo, 0), halo)
            rn = pl.multiple_of(jnp.minimum(r0 + R, L - halo), halo)
            prow = jnp.where(t > 0, x_ref[0, pl.ds(rp, halo), :][halo - 1:halo, :].astype(f32), 0.0)
            nrow = jnp.where(t < nt - 1, x_ref[0, pl.ds(rn, halo), :][0:1, :].astype(f32), 0.0)
            prev = jnp.where(rowi == 0, prow, pltpu.roll(x, 1, 0))
            nxt = jnp.where(rowi == R - 1, nrow, pltpu.roll(x, R - 1, 0))
            w = w_ref[...]
            dst[pl.ds(r0, R), :] = prev * w[0:1, :] + x * w[1:2, :] + nxt * w[2:3, :]
            return carry
        lax.fori_loop(0, nt, body, 0)

    def cols(k1):
        return slice(k1 * ct, (k1 + 1) * ct)

    conv_into(z_s, v_ref, wv_ref)
    stages = ((x1_ref, w1_ref, kr0_ref, ki0_ref), (x2_ref, w2_ref, kr1_ref, ki1_ref))
    for o, (g_ref, gw_ref, kr_ref, ki_ref) in enumerate(stages):
        conv_into(g_s, g_ref, gw_ref)

        def forward1(rt, carry):
            r0 = pl.multiple_of(rt * r1, r1)
            zs = [z_s[pl.ds(a * n2 + r0, r1), :] for a in range(FFT_Z1)]
            for k1 in range(FFT_K1):
                ar = _lincomb([(c16[a][k1], zs[a]) for a in range(FFT_Z1)])
                ai = _lincomb([(-s16[a][k1], zs[a]) for a in range(FFT_Z1)])
                twr = twr_ref[pl.ds(r0, r1), cols(k1)]
                twi = twi_ref[pl.ds(r0, r1), cols(k1)]
                if ai is None:
                    pr, pi = ar * twr, -(ar * twi)
                else:
                    pr, pi = ar * twr + ai * twi, ai * twr - ar * twi
                d_s[pl.ds(r0, r1), cols(k1)] = pr.astype(bf16)
                d_s[pl.ds(n2 + r0, r1), cols(k1)] = pi.astype(bf16)
            return carry

        lax.fori_loop(0, n2 // r1, forward1, 0)

        for k1 in range(FFT_K1):
            xk = _dot(m2f_ref[...], d_s[:, cols(k1)])
            xr, xi = xk[0:n2, :], xk[n2:, :]
            kr, ki = kr_ref[k1], ki_ref[k1]
            y_s[0:n2, cols(k1)] = (xr * kr - xi * ki).astype(bf16)
            y_s[n2:2 * n2, cols(k1)] = (xr * ki + xi * kr).astype(bf16)

        for k1 in range(FFT_K1):
            bk = _dot(m2i_ref[...], y_s[:, cols(k1)])
            br, bi = bk[0:n2, :], bk[n2:, :]
            twr, twi = twr_ref[:, cols(k1)], twi_ref[:, cols(k1)]
            b_s[0:n2, cols(k1)] = br * twr - bi * twi
            b_s[n2:2 * n2, cols(k1)] = br * twi + bi * twr

        def inverse1(rt, carry):
            r0 = pl.multiple_of(rt * r1, r1)
            brs = [b_s[pl.ds(r0, r1), cols(j)] for j in range(FFT_K1)]
            bis = [b_s[pl.ds(n2 + r0, r1), cols(j)] for j in range(FFT_K1)]
            for a in range(FFT_Z1):
                terms = [(1.0, brs[0]), (-1.0 if a % 2 else 1.0, brs[FFT_K1 - 1])]
                for j in range(1, FFT_K1 - 1):
                    terms += [(2.0 * c16[a][j], brs[j]), (-2.0 * s16[a][j], bis[j])]
                rows = pl.ds(a * n2 + r0, r1)
                z = g_s[rows, :] * (_lincomb(terms) + z_s[rows, :] * db_ref[o:o + 1, :])
                if o + 1 < len(stages):
                    z_s[rows, :] = z
                else:
                    o_ref[0, rows, :] = z.astype(o_ref.dtype)
            return carry

        lax.fori_loop(0, n2 // r1, inverse1, 0)


def _hyena_tables(L):
    n_fft = 2 * L
    n2 = L // FFT_Z1
    ct = HY_CT
    idx = jnp.arange(n2, dtype=jnp.int32)
    a2 = ((idx[:, None] * idx[None, :]) % n2).astype(f32) * (2.0 * math.pi / n2)
    c2, s2 = jnp.cos(a2), jnp.sin(a2)
    m2f = jnp.block([[c2, s2], [-s2, c2]]).astype(bf16)
    m2i = jnp.block([[c2, -s2], [s2, c2]]).astype(bf16)
    at = (idx[:, None] * jnp.arange(FFT_K1, dtype=jnp.int32)[None, :]).astype(f32) * (2.0 * math.pi / n_fft)
    twr = jnp.broadcast_to(jnp.cos(at)[:, :, None], (n2, FFT_K1, ct)).reshape(n2, FFT_K1 * ct)
    twi = jnp.broadcast_to(jnp.sin(at)[:, :, None], (n2, FFT_K1, ct)).reshape(n2, FFT_K1 * ct)
    return m2f, m2i, twr, twi


def _hyena_fft(hx, conv_w, tables, kr, ki, dbias):
    B, L, _ = hx.shape
    ct = HY_CT
    nct = B_WIDTH // ct
    n2 = L // FFT_Z1
    wide = FFT_K1 * ct
    once = pl.Buffered(1)
    m2f, m2i, twr, twi = tables

    def xspec(part):
        return pl.BlockSpec((1, L, ct), lambda c, b: (b, 0, part * nct + c))

    def wspec(part):
        return pl.BlockSpec((8, ct), lambda c, b: (0, part * nct + c))

    def kspec(order):
        return pl.BlockSpec((FFT_K1, n2, ct), lambda c, b: (0, 0, order * nct + c), pipeline_mode=once)

    def const(shape):
        return pl.BlockSpec(shape, lambda c, b: (0, 0), pipeline_mode=once)

    return pl.pallas_call(
        _hyena_fft_kernel,
        grid=(nct, B),
        in_specs=[const((2 * n2, 2 * n2)), const((2 * n2, 2 * n2)), const((n2, wide)), const((n2, wide)),
                  xspec(0), xspec(1), xspec(2), wspec(0), wspec(1), wspec(2),
                  kspec(0), kspec(0), kspec(1), kspec(1),
                  pl.BlockSpec((8, ct), lambda c, b: (0, c))],
        out_specs=pl.BlockSpec((1, L, ct), lambda c, b: (b, 0, c)),
        out_shape=jax.ShapeDtypeStruct((B, L, B_WIDTH), bf16),
        scratch_shapes=[pltpu.VMEM((L, ct), f32), pltpu.VMEM((L, ct), f32),
                        pltpu.VMEM((2 * n2, wide), bf16), pltpu.VMEM((2 * n2, wide), bf16),
                        pltpu.VMEM((2 * n2, wide), f32)],
        compiler_params=_cparams("parallel", "parallel"),
        name="hyena_fft",
    )(m2f, m2i, twr, twi, hx, hx, hx, conv_w, conv_w, conv_w, kr, ki, kr, ki, dbias)


RT_E1, RT_E2, RT_W1, RT_W2, RT_R1, RT_R2 = 0, 1, 2, 3, 4, 5
ROUTE_TM = ROW_TM
PROJ_PARTS = 2
EXPERT_TM = 512
SC_WINDOW = 128


def _pack_bf16_pairs(x):
    n = x.shape[1] // 2
    bits = lax.bitcast_convert_type(x.astype(bf16).astype(f32), jnp.uint32)
    return bits[:, :n] | (bits[:, n:] >> 16)


def _unpack_bf16_pairs(p):
    hi = lax.bitcast_convert_type(p & jnp.uint32(0xFFFF0000), f32)
    lo = lax.bitcast_convert_type(p << 16, f32)
    return hi, lo


ROUTER_ROWS = 40


def _route_topk(lt):
    row = lax.broadcasted_iota(jnp.int32, lt.shape, 0)
    neg = -jnp.inf
    big = ROUTER_LANES
    gl = jnp.where(row < N_GROUPS, lt, neg)
    gmax = jnp.max(gl, axis=0, keepdims=True)
    gsel = jnp.min(jnp.where(gl == gmax, row, big), axis=0, keepdims=True)
    psel = 1.0 / jnp.sum(jnp.exp(gl - gmax), axis=0, keepdims=True)
    first = N_GROUPS + gsel * EXPERTS_PER_GROUP
    el = jnp.where((row >= first) & (row < first + EXPERTS_PER_GROUP), lt, neg)
    t1 = jnp.max(el, axis=0, keepdims=True)
    i1 = jnp.min(jnp.where(el == t1, row, big), axis=0, keepdims=True)
    el2 = jnp.where(row == i1, neg, el)
    t2 = jnp.max(el2, axis=0, keepdims=True)
    i2 = jnp.min(jnp.where(el2 == t2, row, big), axis=0, keepdims=True)
    ex = jnp.exp(t2 - t1)
    w1 = psel / (1.0 + ex)
    w2 = ex * w1
    return i1 - N_GROUPS, i2 - N_GROUPS, w1, w2


def _proj_route_kernel(x_ref, *refs, n_in):
    a_refs = refs[0:n_in]
    w_refs = refs[n_in:2 * n_in]
    g_ref, wr_ref, up_ref, xo_ref, hpa_ref, hpb_ref, rt_ref, rtt_ref, cnt_ref = refs[2 * n_in:]
    tm = xo_ref.shape[0]
    rp = tm // PROJ_PARTS
    parts = [slice(p * rp, (p + 1) * rp) for p in range(PROJ_PARTS)]
    xs = []
    for r in parts:
        x = x_ref[r, :]
        for a_ref, w_ref in zip(a_refs, w_refs):
            x = x + _dot(a_ref[r, :], w_ref[...])
        xo_ref[r, :] = x
        xs.append(x)
    hs = [_rms(x, g_ref[...]) for x in xs]
    for r, h in zip(parts, hs):
        hp = _pack_bf16_pairs(h)
        quarter = hp.shape[1] // 2
        hpa_ref[r, :] = hp[:, :quarter]
        hpb_ref[r, :] = hp[:, quarter:]
    tops = []
    for h in hs:
        h_hi = h.astype(bf16)
        h_lo = (h - h_hi.astype(f32)).astype(bf16)
        both = _dot(h_hi, wr_ref[...])
        logits = both[:, :ROUTER_LANES] + both[:, ROUTER_LANES:] + _dot(h_lo, wr_ref[:, :ROUTER_LANES])
        tops.append(_route_topk(logits.T[0:ROUTER_ROWS, :]))
    row = lax.broadcasted_iota(jnp.int32, (ROUTER_LANES, rp), 0)
    onehots = [jnp.where((row == e1) | (row == e2), 1.0, 0.0) for e1, e2, _, _ in tops]
    onehot16 = jnp.concatenate(onehots, axis=1).astype(bf16)
    before = _dot(onehot16, up_ref[...])
    for p, (r, (e1, e2, w1, w2)) in enumerate(zip(parts, tops)):
        bp = before[:, r]
        r1 = jnp.sum(jnp.where(row == e1, bp, 0.0), axis=0, keepdims=True)
        r2 = jnp.sum(jnp.where(row == e2, bp, 0.0), axis=0, keepdims=True)
        rec = jnp.zeros((ROUTER_LANES, rp), f32)
        for ln, val in ((RT_E1, e1.astype(f32)), (RT_E2, e2.astype(f32)), (RT_W1, w1), (RT_W2, w2),
                        (RT_R1, r1), (RT_R2, r2)):
            rec = jnp.where(row == ln, val, rec)
        rtt_ref[:, r] = rec[0:rtt_ref.shape[0], :]
        rt_ref[r, :] = rec.T
    ones = jnp.ones((cnt_ref.shape[1], tm), bf16)
    cnt_ref[0] = _dot_nt(ones, onehot16)


def _proj_route(x, acts, ws, gain, w_router):
    T, D = x.shape
    tm = ROUTE_TM
    n_in = len(acts)
    up = jnp.asarray(np.triu(np.ones((tm, tm), np.float32), 1), dtype=bf16)
    in_specs = [pl.BlockSpec((tm, D), lambda i: (i, 0))]
    in_specs += [pl.BlockSpec((tm, a.shape[1]), lambda i: (i, 0)) for a in acts]
    in_specs += [pl.BlockSpec(w.shape, lambda i: (0, 0)) for w in ws]
    in_specs += [pl.BlockSpec((1, D), lambda i: (0, 0)),
                 pl.BlockSpec((D, 2 * ROUTER_LANES), lambda i: (0, 0)),
                 pl.BlockSpec((tm, tm), lambda i: (0, 0))]
    return pl.pallas_call(
        functools.partial(_proj_route_kernel, n_in=n_in),
        grid=(T // tm,),
        in_specs=in_specs,
        out_specs=[pl.BlockSpec((tm, D), lambda i: (i, 0)),
                   pl.BlockSpec((tm, D // 4), lambda i: (i, 0)),
                   pl.BlockSpec((tm, D // 4), lambda i: (i, 0)),
                   pl.BlockSpec((tm, ROUTER_LANES), lambda i: (i, 0)),
                   pl.BlockSpec((8, tm), lambda i: (0, i)),
                   pl.BlockSpec((1, 8, ROUTER_LANES), lambda i: (i, 0, 0))],
        out_shape=[jax.ShapeDtypeStruct((T, D), f32),
                   jax.ShapeDtypeStruct((T, D // 4), jnp.uint32),
                   jax.ShapeDtypeStruct((T, D // 4), jnp.uint32),
                   jax.ShapeDtypeStruct((T, ROUTER_LANES), f32),
                   jax.ShapeDtypeStruct((8, T), f32),
                   jax.ShapeDtypeStruct((T // tm, 8, ROUTER_LANES), f32)],
        compiler_params=_cparams("parallel"),
        name="proj_route",
    )(x, *acts, *ws, gain.reshape(1, D), w_router, up)


def _moe_rows(T):
    return 2 * T + N_EXPERTS * EXPERT_TM


def _moe_plan(route_t, counts):
    T = route_t.shape[1]
    tm = ROUTE_TM
    cnt = counts[:, 0, :N_EXPERTS].astype(jnp.int32)
    total = jnp.sum(cnt, axis=0)
    padded = ((total + EXPERT_TM - 1) // EXPERT_TM) * EXPERT_TM
    ends = jnp.cumsum(padded)
    tile_base = (ends - padded)[None, :] + jnp.cumsum(cnt, axis=0) - cnt
    base = jnp.broadcast_to(tile_base.T[:, :, None], (N_EXPERTS, T // tm, tm)).reshape(N_EXPERTS, T)
    ids = jnp.arange(N_EXPERTS, dtype=jnp.int32)[:, None]
    dest = []
    for e_row, r_row in ((RT_E1, RT_R1), (RT_E2, RT_R2)):
        e = route_t[e_row].astype(jnp.int32)
        dest.append(jnp.sum(jnp.where(ids == e[None, :], base, 0), axis=0) + route_t[r_row].astype(jnp.int32))
    n_tiles = _moe_rows(T) // EXPERT_TM
    n_used = ends[-1] // EXPERT_TM
    starts = jnp.arange(n_tiles, dtype=jnp.int32) * EXPERT_TM
    tile_expert = jnp.sum((starts[:, None] >= ends[None, :]).astype(jnp.int32), axis=1)
    last_expert = jnp.sum((jnp.maximum(n_used - 1, 0) * EXPERT_TM >= ends).astype(jnp.int32))
    tile_expert = jnp.minimum(tile_expert, last_expert)
    return dest, tile_expert, n_used.reshape(1).astype(jnp.int32)


def _sc_mesh():
    return plsc.VectorSubcoreMesh(core_axis_name="core", subcore_axis_name="subcore")


def _sc_scatter_pair(rows, d0, d1, n_out):
    T, W = rows.shape
    win = SC_WINDOW

    @pl.kernel(out_type=jax.ShapeDtypeStruct((n_out, W), rows.dtype), mesh=_sc_mesh(), name="moe_sc_scatter")
    def scatter(x_hbm, i0_hbm, i1_hbm, o_hbm):
        def body(x_vmem, i0_vmem, i1_vmem):
            pltpu.sync_copy(x_vmem, o_hbm.at[i0_vmem.at[0]])
            pltpu.sync_copy(x_vmem, o_hbm.at[i1_vmem.at[0]])

        pltpu.emit_pipeline(
            body, grid=(T // win,),
            in_specs=[pl.BlockSpec((win, W), index_map=lambda i: (i, 0)),
                      pl.BlockSpec((1, win), index_map=lambda i: (0, i)),
                      pl.BlockSpec((1, win), index_map=lambda i: (0, i))],
            out_specs=[],
            core_axis_name=("core", "subcore"), dimension_semantics=(pltpu.PARALLEL,),
        )(x_hbm, i0_hbm, i1_hbm)

    return scatter(rows, d0.reshape(1, T), d1.reshape(1, T))


def _sc_gather(table, idx):
    n = idx.shape[0]
    W = table.shape[1]
    win = SC_WINDOW

    @pl.kernel(out_type=jax.ShapeDtypeStruct((n, W), table.dtype), mesh=_sc_mesh(), name="moe_sc_gather")
    def gather(t_hbm, i_hbm, o_hbm):
        def body(i_vmem, o_vmem):
            pltpu.sync_copy(t_hbm.at[i_vmem.at[0]], o_vmem)

        pltpu.emit_pipeline(
            body, grid=(n // win,),
            in_specs=[pl.BlockSpec((1, win), index_map=lambda i: (0, i))],
            out_specs=[pl.BlockSpec((win, W), index_map=lambda i: (i, 0))],
            core_axis_name=("core", "subcore"), dimension_semantics=(pltpu.PARALLEL,),
        )(i_hbm, o_hbm)

    return gather(table, idx.reshape(1, n))


def _expert_kernel(te_ref, nu_ref, xa_ref, xb_ref, wg_ref, wu_ref, wd_ref, ya_ref, yb_ref):
    @pl.when(pl.program_id(0) < nu_ref[0])
    def _():
        hi, lo = _unpack_bf16_pairs(jnp.concatenate([xa_ref[...], xb_ref[...]], axis=1))
        x = jnp.concatenate([hi, lo], axis=1).astype(bf16)
        gate = _dot(x, wg_ref[0].astype(bf16))
        up = _dot(x, wu_ref[0].astype(bf16))
        hg = (gate * _sigmoid(gate) * up).astype(bf16)
        y = _pack_bf16_pairs(_dot(hg, wd_ref[0].astype(bf16)))
        quarter = y.shape[1] // 2
        ya_ref[...] = y[:, :quarter]
        yb_ref[...] = y[:, quarter:]

    @pl.when(pl.program_id(0) >= nu_ref[0])
    def _():
        ya_ref[...] = jnp.zeros(ya_ref.shape, ya_ref.dtype)
        yb_ref[...] = jnp.zeros(yb_ref.shape, yb_ref.dtype)


def _experts(tile_expert, n_used, xa, xb, wg, wu, wd):
    n_rows, quarter = xa.shape
    D = 4 * quarter
    tm = EXPERT_TM
    grid_spec = pltpu.PrefetchScalarGridSpec(
        num_scalar_prefetch=2,
        grid=(n_rows // tm,),
        in_specs=[pl.BlockSpec((tm, quarter), lambda i, te, nu: (i, 0)),
                  pl.BlockSpec((tm, quarter), lambda i, te, nu: (i, 0)),
                  pl.BlockSpec((1, D, D_EXPERT), lambda i, te, nu: (te[i], 0, 0)),
                  pl.BlockSpec((1, D, D_EXPERT), lambda i, te, nu: (te[i], 0, 0)),
                  pl.BlockSpec((1, D_EXPERT, D), lambda i, te, nu: (te[i], 0, 0))],
        out_specs=[pl.BlockSpec((tm, quarter), lambda i, te, nu: (i, 0)),
                   pl.BlockSpec((tm, quarter), lambda i, te, nu: (i, 0))])
    return pl.pallas_call(
        _expert_kernel,
        grid_spec=grid_spec,
        out_shape=[jax.ShapeDtypeStruct((n_rows, quarter), jnp.uint32)] * 2,
        compiler_params=_cparams("arbitrary"),
        name="moe_experts",
    )(tile_expert, n_used, xa, xb, wg, wu, wd)


def _combine_kernel(x_ref, rt_ref, g_ref, a1_ref, b1_ref, a2_ref, b2_ref, *rest, final):
    half = x_ref.shape[1] // 2
    h1, l1 = _unpack_bf16_pairs(jnp.concatenate([a1_ref[...], b1_ref[...]], axis=1))
    h2, l2 = _unpack_bf16_pairs(jnp.concatenate([a2_ref[...], b2_ref[...]], axis=1))
    w1 = rt_ref[:, RT_W1:RT_W1 + 1]
    w2 = rt_ref[:, RT_W2:RT_W2 + 1]
    out = jnp.concatenate([x_ref[:, 0:half] + w1 * h1 + w2 * h2, x_ref[:, half:] + w1 * l1 + w2 * l2], axis=1)
    if final:
        (o_ref,) = rest
        o_ref[...] = _rms(out, g_ref[...])
    else:
        w_ref, o_ref, p_ref = rest
        o_ref[...] = out
        p_ref[...] = _dot(_rms(out, g_ref[...]).astype(bf16), w_ref[...]).astype(p_ref.dtype)


def _combine(x, route, ga, gb, gain, w_next=None):
    T, D = x.shape
    tm = ROUTE_TM
    n = T // tm
    quarter = D // 4
    final = w_next is None
    slot1 = pl.BlockSpec((tm, quarter), lambda i: (i, 0))
    slot2 = pl.BlockSpec((tm, quarter), lambda i: (i + n, 0))
    row_spec = pl.BlockSpec((tm, D), lambda i: (i, 0))
    in_specs = [row_spec, pl.BlockSpec((tm, ROUTER_LANES), lambda i: (i, 0)), pl.BlockSpec((1, D), lambda i: (0, 0)),
                slot1, slot1, slot2, slot2]
    args = [x, route, gain.reshape(1, D), ga, gb, ga, gb]
    out_specs = [row_spec]
    out_shape = [jax.ShapeDtypeStruct((T, D), f32)]
    if not final:
        n_next = w_next.shape[1]
        in_specs.append(pl.BlockSpec((D, n_next), lambda i: (0, 0)))
        args.append(w_next)
        out_specs.append(pl.BlockSpec((tm, n_next), lambda i: (i, 0)))
        out_shape.append(jax.ShapeDtypeStruct((T, n_next), bf16))
    outs = pl.pallas_call(
        functools.partial(_combine_kernel, final=final),
        grid=(n,),
        in_specs=in_specs,
        out_specs=out_specs,
        out_shape=out_shape,
        compiler_params=_cparams("parallel"),
        name="moe_combine",
    )(*args)
    return outs[0] if final else outs


def _moe(layer, x, hpa, hpb, route, route_t, counts, wg, wu, wd, gain, w_next=None):
    (d0, d1), tile_expert, n_used = _moe_plan(route_t, counts)
    n_rows = _moe_rows(x.shape[0])
    xa = _sc_scatter_pair(hpa, d0, d1, n_rows)
    xb = _sc_scatter_pair(hpb, d0, d1, n_rows)
    ya, yb = _experts(tile_expert + layer * N_EXPERTS, n_used, xa, xb, wg, wu, wd)
    dcat = jnp.concatenate([d0, d1])
    return _combine(x, route, _sc_gather(ya, dcat), _sc_gather(yb, dcat), gain, w_next)


SWA_LOCKSTEP = 2


def _swa_bias_tables():
    blk = C_BLOCK
    qi = np.arange(blk)[:, None]
    kj = np.arange(3 * blk)[None, :]
    dist = np.abs(blk + qi - kj)
    tabs = np.full((3, C_KV_HEADS, C_GROUP * blk, 3 * blk), -np.inf, np.float32)
    for case in range(3):
        ok = dist <= WINDOW
        if case == 0:
            ok = ok & (kj >= blk)
        if case == 2:
            ok = ok & (kj < 2 * blk)
        for hh in range(C_HEADS):
            slope = 2.0 ** (-8.0 * (hh + 1) / C_HEADS)
            t = np.where(ok, -slope * LOG2E * dist, -np.inf)
            kvh, g = divmod(hh, C_GROUP)
            tabs[case, kvh, g * blk:(g + 1) * blk] = t
    return jnp.asarray(tabs)


def _swa_kernel(sink_ref, q_ref, kp_ref, kc_ref, kn_ref, vp_ref, vc_ref, vn_ref, bias_ref, o_ref):
    blk = C_BLOCK
    hd = C_HEAD_DIM
    kcat = jnp.concatenate([kp_ref[0], kc_ref[0], kn_ref[0]], axis=0)
    vcat = jnp.concatenate([vp_ref[0], vc_ref[0], vn_ref[0]], axis=0)
    q = q_ref[0]
    rblk = lax.broadcasted_iota(jnp.int32, (C_GROUP * blk, 1), 0) // blk
    lane = lax.broadcasted_iota(jnp.int32, (3 * blk, LANE - hd), 1)
    ones_pad = jnp.where(lane == 0, 1.0, 0.0).astype(bf16)
    q4s, sinks = [], []
    for kvh in range(C_KV_HEADS):
        q4s.append(jnp.concatenate(
            [q[:, (kvh * C_GROUP + g) * hd:(kvh * C_GROUP + g + 1) * hd] for g in range(C_GROUP)], axis=0))
        sk = jnp.zeros((C_GROUP * blk, 1), f32)
        for g in range(C_GROUP):
            sk = jnp.where(rblk == g, sink_ref[kvh * C_GROUP + g] * LOG2E, sk)
        sinks.append(sk)
    outs = []
    for h0 in range(0, C_KV_HEADS, SWA_LOCKSTEP):
        hs = range(h0, h0 + SWA_LOCKSTEP)
        ss = [_dot_nt(q4s[h], kcat[:, h * hd:(h + 1) * hd]) + bias_ref[0, h] for h in hs]
        ms = [jnp.maximum(jnp.max(s, axis=-1, keepdims=True), sinks[h]) for s, h in zip(ss, hs)]
        ps = [jnp.exp2(s - m).astype(bf16) for s, m in zip(ss, ms)]
        pvs = [_dot(p, jnp.concatenate([vcat[:, h * hd:(h + 1) * hd], ones_pad], axis=1)) for p, h in zip(ps, hs)]
        for pv, m, h in zip(pvs, ms, hs):
            den = pv[:, hd:hd + 1] + jnp.exp2(sinks[h] - m)
            o4 = pv[:, 0:hd] / den
            outs += [o4[g * blk:(g + 1) * blk, :] for g in range(C_GROUP)]
    o_ref[0] = jnp.concatenate(outs, axis=1).astype(o_ref.dtype)


def _swa(qkv, sinks):
    B, L, _ = qkv.shape
    nb = L // C_BLOCK
    dq = C_HEADS * C_HEAD_DIM
    dkv = C_KV_HEADS * C_HEAD_DIM
    k_col = dq // dkv
    v_col = k_col + 1

    def kv_spec(col, shift):
        return pl.BlockSpec((1, C_BLOCK, dkv),
                            lambda n, b: (b, jnp.clip(n + shift, 0, nb - 1), col))

    def edge_case(n, b):
        return (jnp.where(n == 0, 0, jnp.where(n == nb - 1, 2, 1)), 0, 0, 0)

    return pl.pallas_call(
        _swa_kernel,
        grid=(nb, B),
        in_specs=[pl.BlockSpec(memory_space=pltpu.SMEM),
                  pl.BlockSpec((1, C_BLOCK, dq), lambda n, b: (b, n, 0)),
                  kv_spec(k_col, -1), kv_spec(k_col, 0), kv_spec(k_col, 1),
                  kv_spec(v_col, -1), kv_spec(v_col, 0), kv_spec(v_col, 1),
                  pl.BlockSpec((1, C_KV_HEADS, C_GROUP * C_BLOCK, 3 * C_BLOCK), edge_case)],
        out_specs=pl.BlockSpec((1, C_BLOCK, dq), lambda n, b: (b, n, 0)),
        out_shape=jax.ShapeDtypeStruct((B, L, dq), bf16),
        compiler_params=_cparams("parallel", "parallel"),
        name="swa",
    )(sinks, qkv, qkv, qkv, qkv, qkv, qkv, qkv, _swa_bias_tables())


def _hyena_filters(L, w1, b1, w2, b2, w3, b3, w4, freq):
    hp = lax.Precision.HIGHEST
    t = jnp.linspace(0.0, 1.0, L, dtype=f32)[:, None]
    wpos = (2.0 * math.pi / L) * jnp.arange(L, dtype=f32)[:, None]
    bands = jnp.linspace(1e-4, HY_BANDS - 1, HY_BANDS, dtype=f32)[None, :]
    feats = jnp.concatenate([t, jnp.cos(bands * wpos), -jnp.sin(bands * wpos)], axis=-1)
    h = jnp.sin(freq[0] * (jnp.dot(feats, w1, precision=hp) + b1))
    h = jnp.sin(freq[1] * (jnp.dot(h, w2, precision=hp) + b2))
    h = jnp.sin(freq[2] * (jnp.dot(h, w3, precision=hp) + b3))
    h = jnp.dot(h, w4, precision=hp).reshape(L, HY_ORDER, 2, B_WIDTH)
    deltas = jnp.abs(jnp.linspace(HY_MIN_DECAY, HY_MAX_DECAY, B_WIDTH, dtype=f32))
    window = jnp.exp(-t * deltas[None, :])
    return h * window[:, None, None, :]


def _pad_rows(w, rows=8):
    return jnp.pad(w, ((0, rows - w.shape[0]), (0, 0)))


def _even_in_weights(w_in):
    hk = A_HEADS * A_DK
    q = w_in[:, 0:hk].reshape(D_MODEL, A_HEADS, A_DK)
    k = w_in[:, hk:2 * hk].reshape(D_MODEL, A_HEADS, A_DK)
    v = w_in[:, 2 * hk:A_QKV].reshape(D_MODEL, A_HEADS, A_DV)
    z = w_in[:, A_QKV:A_QKV + A_WIDTH].reshape(D_MODEL, A_HEADS, A_DV)
    qkvz = jnp.concatenate([q, k, v, z], axis=-1).reshape(D_MODEL, A_HEADS * 4 * A_DK)
    ab = w_in[:, A_QKV + A_WIDTH:A_QKV + A_WIDTH + 4 * A_HEADS].reshape(D_MODEL, 4, A_HEADS)
    ab = jnp.transpose(ab, (0, 2, 1))
    ab = jnp.pad(ab, ((0, 0), (0, 0), (0, LANE - 4))).reshape(D_MODEL, A_HEADS * LANE)
    hx = w_in[:, A_QKV + A_WIDTH + 4 * A_HEADS:]
    return jnp.concatenate([qkvz, ab, hx], axis=1).astype(bf16)


def _router_weights(w_group, w_expert):
    w = jnp.concatenate([w_group, w_expert], axis=1)
    w = jnp.pad(w, ((0, 0), (0, ROUTER_LANES - w.shape[1])))
    hi = w.astype(bf16)
    lo = (w - hi.astype(f32)).astype(bf16)
    return jnp.concatenate([hi, lo], axis=1)


def kernel(x_prompt, x_sample, norm_mix, norm_ffn, norm_final, ev_w_in, ev_conv_a, ev_alog_f, ev_alog_b, ev_dtb_f, ev_dtb_b, ev_onorm, ev_conv_b, hy_w1, hy_b1, hy_w2, hy_b2, hy_w3, hy_b3, hy_w4, hy_freq, hy_dbias, ev_w_out, od_w_qkv, od_sinks, od_w_out, moe_w_group, moe_w_expert, moe_w_gate, moe_w_up, moe_w_down):
    L = x_prompt.shape[1]
    n_layers = moe_w_gate.shape[0]
    wg = moe_w_gate.reshape(n_layers * N_EXPERTS, D_MODEL, D_EXPERT)
    wu = moe_w_up.reshape(n_layers * N_EXPERTS, D_MODEL, D_EXPERT)
    wd = moe_w_down.reshape(n_layers * N_EXPERTS, D_EXPERT, D_MODEL)

    w_in = _even_in_weights(ev_w_in[0])
    n_qkvz = A_HEADS * 4 * A_DK
    n_ab = A_HEADS * LANE
    n_hx = 3 * B_WIDTH
    hk = A_HEADS * A_DK
    ca = ev_conv_a[0]
    conv_a = jnp.concatenate([ca[0:hk].reshape(A_HEADS, A_DK, 3), ca[hk:2 * hk].reshape(A_HEADS, A_DK, 3),
                              ca[2 * hk:].reshape(A_HEADS, A_DV, 3)], axis=1)
    conv_a = jnp.pad(jnp.transpose(conv_a, (0, 2, 1)), ((0, 0), (0, 5), (0, 0)))
    gate_sc = jnp.stack([ev_alog_f[0], ev_alog_b[0], ev_dtb_f[0], ev_dtb_b[0]], axis=1)
    filt = _hyena_filters(L, hy_w1[0], hy_b1[0], hy_w2[0], hy_b2[0], hy_w3[0], hy_b3[0], hy_w4[0], hy_freq[0])
    hsum = (filt[:, :, 0] + filt[:, :, 1]).reshape(L, HY_ORDER * B_WIDTH)
    hdiff = (filt[:, :, 0] - filt[:, :, 1]).reshape(L, HY_ORDER * B_WIDTH)
    kr, ki = _filter_spectrum(hsum, hdiff)
    fft_tables = _hyena_tables(L)
    conv_b = _pad_rows(ev_conv_b[0].T)
    dbias = _pad_rows(hy_dbias[0])
    w_out = ev_w_out[0].astype(bf16)
    routers = [_router_weights(moe_w_group[l], moe_w_expert[l]) for l in range(n_layers)]
    n_qkv = od_w_qkv.shape[-1]
    dq = C_HEADS * C_HEAD_DIM
    w_qkv = jnp.concatenate([od_w_qkv[0][:, :dq] * (LOG2E * C_HEAD_DIM ** -0.5), od_w_qkv[0][:, dq:]], axis=1)
    w_qkv = w_qkv.astype(bf16)
    w_att_out = od_w_out[0].astype(bf16)

    def trunk(xb):
        B = xb.shape[0]
        T = B * L
        x = xb.reshape(T, D_MODEL)
        qkvz, abh, hx = _norm_matmul(x, norm_mix[0], w_in, (n_qkvz, n_ab, n_hx), (bf16, f32, bf16))
        o_a = _deltanet(qkvz.reshape(B, L, n_qkvz), abh.reshape(B, L, n_ab), conv_a, gate_sc, ev_onorm[0])
        o_b = _hyena_fft(hx.reshape(B, L, n_hx), conv_b, fft_tables, kr, ki, dbias)
        x, hpa, hpb, route, route_t, counts = _proj_route(
            x, (o_a.reshape(T, A_WIDTH), o_b.reshape(T, B_WIDTH)), (w_out[:A_WIDTH], w_out[A_WIDTH:]),
            norm_ffn[0], routers[0])
        x, qkv = _moe(0, x, hpa, hpb, route, route_t, counts, wg, wu, wd, norm_mix[1], w_qkv)
        o_c = _swa(qkv.reshape(B, L, n_qkv), od_sinks[0])
        x, hpa, hpb, route, route_t, counts = _proj_route(
            x, (o_c.reshape(T, C_HEADS * C_HEAD_DIM),), (w_att_out,), norm_ffn[1], routers[1])
        y = _moe(1, x, hpa, hpb, route, route_t, counts, wg, wu, wd, norm_final)
        return y.reshape(B, L, D_MODEL)

    return (trunk(x_prompt), trunk(x_sample))
```

```python
import functools
import math

import jax
import jax.numpy as jnp
import numpy as np
from jax import lax
from jax.experimental import pallas as pl
from jax.experimental.pallas import tpu as pltpu
from jax.experimental.pallas import tpu_sc as plsc

f32 = jnp.float32
bf16 = jnp.bfloat16

EPS = 1e-6
D_MODEL = 1024

A_HEADS = 4
A_DK = 128
A_DV = 128
A_WIDTH = A_HEADS * A_DV
A_QKV = 2 * A_HEADS * A_DK + A_HEADS * A_DV
DELTA_CHUNK = 128
INV_BASE = 8
DELTA_GROUP = 8

B_WIDTH = D_MODEL - A_WIDTH
HY_ORDER = 2
HY_EMB = 33
HY_BANDS = (HY_EMB - 1) // 2
HY_TARGET = 1e-2
HY_MIN_DECAY = math.log(HY_TARGET) / 1.5
HY_MAX_DECAY = math.log(HY_TARGET) / 0.3
HY_CT = 256
HY_ROWS = 256

C_HEADS = 16
C_KV_HEADS = 4
C_HEAD_DIM = 64
C_GROUP = C_HEADS // C_KV_HEADS
WINDOW = 128
C_BLOCK = 128

N_GROUPS = 4
EXPERTS_PER_GROUP = 8
N_EXPERTS = N_GROUPS * EXPERTS_PER_GROUP
D_EXPERT = 256
ROUTER_LANES = 128

LANE = 128
ROW_TM = 512
VMEM_LIMIT = 56 * 1024 * 1024
LOG2E = 1.4426950408889634


def _cparams(*sem):
    return pltpu.CompilerParams(dimension_semantics=sem, vmem_limit_bytes=VMEM_LIMIT)


def _dot(a, b):
    return jnp.dot(a, b, preferred_element_type=f32)


def _dot_nt(a, b):
    return lax.dot_general(a, b, (((1,), (1,)), ((), ())), preferred_element_type=f32)


def _dot_tn(a, b):
    return lax.dot_general(a, b, (((0,), (0,)), ((), ())), preferred_element_type=f32)


def _sigmoid(x):
    return 1.0 / (1.0 + jnp.exp(-x))


def _rms(x, g):
    return x * lax.rsqrt(jnp.mean(x * x, axis=-1, keepdims=True) + EPS) * g


def _norm_matmul_kernel(x_ref, g_ref, w_ref, *o_refs):
    h = _rms(x_ref[...], g_ref[...]).astype(bf16)
    off = 0
    for o_ref in o_refs:
        n = o_ref.shape[-1]
        o_ref[...] = _dot(h, w_ref[:, off:off + n]).astype(o_ref.dtype)
        off += n


def _norm_matmul(x, gain, w, splits, dtypes):
    T, D = x.shape
    tm = ROW_TM
    n_all = w.shape[1]
    return pl.pallas_call(
        _norm_matmul_kernel,
        grid=(T // tm,),
        in_specs=[pl.BlockSpec((tm, D), lambda i: (i, 0)),
                  pl.BlockSpec((1, D), lambda i: (0, 0)),
                  pl.BlockSpec((D, n_all), lambda i: (0, 0))],
        out_specs=[pl.BlockSpec((tm, n), lambda i: (i, 0)) for n in splits],
        out_shape=[jax.ShapeDtypeStruct((T, n), dt) for n, dt in zip(splits, dtypes)],
        compiler_params=_cparams("parallel"),
        name="norm_matmul",
    )(x, gain.reshape(1, D), w)


MK_LOWER, MK_UPPER, MK_BASE, MK_LEVEL0 = 0, 1, 2, 3
N_LEVELS = int(math.log2(DELTA_CHUNK // INV_BASE))
MK_EYE = MK_LEVEL0 + N_LEVELS
N_MASKS = MK_EYE + 1


def _delta_masks():
    c = np.arange(DELTA_CHUNK)[:, None]
    s = np.arange(DELTA_CHUNK)[None, :]
    m = [s <= c, s >= c, (c // INV_BASE == s // INV_BASE) & (c != s)]
    b = INV_BASE
    while b < DELTA_CHUNK:
        m.append((c // (2 * b) == s // (2 * b)) & (c // b != s // b))
        b *= 2
    m.append(c == s)
    m = np.stack(m).astype(np.float32)
    return jnp.asarray(m), jnp.asarray(m, dtype=bf16)


def _tri_inverse(mats, mk_ref, mk16_ref):
    ps = [a * mk_ref[MK_BASE] for a in mats]
    ts = [mk_ref[MK_EYE] - p for p in ps]
    n = 2
    while n < INV_BASE:
        p16s = [p.astype(bf16) for p in ps]
        ps = [_dot(p16, p16) for p16 in p16s]
        ts = [t + _dot(t.astype(bf16), p.astype(bf16)) for t, p in zip(ts, ps)]
        n *= 2
    a16s = [a.astype(bf16) for a in mats]
    t16s = [t.astype(bf16) for t in ts]
    for lvl in range(N_LEVELS):
        xs = [_dot(a16 * mk16_ref[MK_LEVEL0 + lvl], t16) for a16, t16 in zip(a16s, t16s)]
        t16s = [t16 - _dot(t16, x.astype(bf16)).astype(bf16) for t16, x in zip(t16s, xs)]
    return t16s


def _deltanet_kernel(sc_ref, x_ref, ab_ref, cw_ref, on_ref, mk_ref, mk16_ref, o_ref, mp_s, nr_s, gl_s, o_s):
    hd = pl.program_id(1)
    L = x_ref.shape[1]
    C = DELTA_CHUNK
    nchunk = L // C
    group = math.gcd(DELTA_GROUP, nchunk)
    nqkv = 3 * A_DK
    halo = 16

    lane = lax.broadcasted_iota(jnp.int32, (1, LANE), 1)
    alog = jnp.where(lane == 2, sc_ref[hd, 0], jnp.where(lane == 3, sc_ref[hd, 1], 0.0))
    dtb = jnp.where(lane == 2, sc_ref[hd, 2], jnp.where(lane == 3, sc_ref[hd, 3], 0.0))
    neg_a = jnp.where((lane == 2) | (lane == 3), -jnp.exp(alog), 0.0)
    rowi = lax.broadcasted_iota(jnp.int32, (C, 1), 0)
    cw = cw_ref[0]

    def chunk_inputs(i):
        r0 = pl.multiple_of(i * C, C)
        xc = x_ref[0, pl.ds(r0, C), 0:nqkv].astype(f32)
        rp = pl.multiple_of(jnp.maximum(r0 - halo, 0), halo)
        rn = pl.multiple_of(jnp.minimum(r0 + C, L - halo), halo)
        prow = x_ref[0, pl.ds(rp, halo), 0:nqkv][halo - 1:halo, :].astype(f32)
        nrow = x_ref[0, pl.ds(rn, halo), 0:nqkv][0:1, :].astype(f32)
        prow = jnp.where(i > 0, prow, 0.0)
        nrow = jnp.where(i < nchunk - 1, nrow, 0.0)
        prev = jnp.where(rowi == 0, prow, pltpu.roll(xc, 1, 0))
        nxt = jnp.where(rowi == C - 1, nrow, pltpu.roll(xc, C - 1, 0))
        y = prev * cw[0:1, :] + xc * cw[1:2, :] + nxt * cw[2:3, :]
        y = y * _sigmoid(y)
        q = y[:, 0:A_DK]
        k = y[:, A_DK:2 * A_DK]
        vc = y[:, 2 * A_DK:nqkv]
        qc = q * lax.rsqrt(jnp.sum(q * q, axis=-1, keepdims=True) + EPS) * (A_DK ** -0.5)
        kc = k * lax.rsqrt(jnp.sum(k * k, axis=-1, keepdims=True) + EPS)

        abc = ab_ref[0, pl.ds(r0, C), :]
        zg = abc + dtb
        g = neg_a * (jnp.maximum(zg, 0.0) + jnp.log(1.0 + jnp.exp(-jnp.abs(zg))))
        low16 = mk16_ref[MK_LOWER]
        g1 = g.astype(bf16)
        e1 = g - g1.astype(f32)
        g2 = e1.astype(bf16)
        g3 = (e1 - g2.astype(f32)).astype(bf16)
        pre = _dot(low16, g1) + _dot(low16, g2) + _dot(low16, g3)
        tot = pre[C - 1:C, :]
        suf = tot - pre + g
        beta_all = _sigmoid(abc)

        k16 = kc.astype(bf16)
        kk = _dot_nt(k16, k16)
        qk = _dot_nt(qc.astype(bf16), k16)

        chains = []
        for rev in (0, 1):
            beta = beta_all[:, rev:rev + 1]
            gcc = (suf if rev else pre)[:, 2 + rev:3 + rev]
            tot11 = tot[:, 2 + rev:3 + rev]
            keep = mk_ref[MK_UPPER if rev else MK_LOWER]
            bc = jnp.broadcast_to(gcc * LOG2E, (C, C))
            decay = jnp.exp2((bc - bc.T) * keep) * keep
            a = (beta * kk) * decay
            eg = jnp.exp(gcc)
            chains.append(dict(
                i=i, rev=rev, a=a, gl=jnp.exp(tot11), qd=qc * eg,
                rhs=jnp.concatenate([vc * beta, kc * (beta * eg)], axis=1).astype(bf16),
                kd16=(kc * jnp.exp(tot11 - gcc)).astype(bf16),
                qk16=(qk * decay).astype(bf16)))
        return chains

    def prep(it, carry):
        chains = []
        for gi in range(group):
            chains += chunk_inputs(it * group + gi)
        t16s = _tri_inverse([c["a"] for c in chains], mk_ref, mk16_ref)
        uws = [_dot(t16, c["rhs"]).astype(bf16) for t16, c in zip(t16s, chains)]
        kts = [_dot_tn(c["kd16"], uw) for c, uw in zip(chains, uws)]
        prs = [_dot(c["qk16"], uw) for c, uw in zip(chains, uws)]
        for c, kt, pr in zip(chains, kts, prs):
            rev, i = c["rev"], c["i"]
            mp_s[rev, i, 0:A_DK, :] = (-kt[:, A_DV:]).astype(bf16)
            mp_s[rev, i, A_DK:A_DK + C, :] = (c["qd"] - pr[:, A_DV:]).astype(bf16)
            nr_s[rev, i, 0:A_DK, :] = kt[:, 0:A_DV]
            nr_s[rev, i, A_DK:A_DK + C, :] = pr[:, 0:A_DV]
            gl_s[rev, i] = jnp.broadcast_to(c["gl"], (8, LANE))
        return carry

    lax.fori_loop(0, nchunk // group, prep, 0)

    def scan(i, carry):
        out = []
        for rev, S in enumerate(carry):
            j = (nchunk - 1 - i) if rev else i
            y = _dot(mp_s[rev, j], S.astype(bf16))
            nr = nr_s[rev, j]
            o_s[rev, pl.ds(pl.multiple_of(j * C, C), C), :] = y[A_DK:, :] + nr[A_DK:, :]
            out.append(S * gl_s[rev, j][0:1, :] + y[0:A_DK, :] + nr[0:A_DK, :])
        return tuple(out)

    S0 = jnp.zeros((A_DK, A_DV), f32)
    lax.fori_loop(0, nchunk, scan, (S0, S0))

    def finish(it, carry):
        for gi in range(group):
            r0 = pl.multiple_of((it * group + gi) * C, C)
            o = o_s[0, pl.ds(r0, C), :] + o_s[1, pl.ds(r0, C), :]
            zg = x_ref[0, pl.ds(r0, C), nqkv:nqkv + A_DV].astype(f32)
            o_ref[0, pl.ds(r0, C), :] = (_rms(o, on_ref[...]) * (zg * _sigmoid(zg))).astype(o_ref.dtype)
        return carry

    lax.fori_loop(0, nchunk // group, finish, 0)


def _deltanet(qkvz, abh, conv_w, gate_sc, onorm):
    B, L, _ = qkvz.shape
    C = DELTA_CHUNK
    nchunk = L // C
    hw = 4 * A_DK
    scratch = [pltpu.VMEM((2, nchunk, A_DK + C, A_DV), bf16),
               pltpu.VMEM((2, nchunk, A_DK + C, A_DV), f32),
               pltpu.VMEM((2, nchunk, 8, LANE), f32),
               pltpu.VMEM((2, L, A_DV), f32)]
    return pl.pallas_call(
        _deltanet_kernel,
        grid=(B, A_HEADS),
        in_specs=[pl.BlockSpec(memory_space=pltpu.SMEM),
                  pl.BlockSpec((1, L, hw), lambda b, h: (b, 0, h)),
                  pl.BlockSpec((1, L, LANE), lambda b, h: (b, 0, h)),
                  pl.BlockSpec((1, 8, 3 * A_DK), lambda b, h: (h, 0, 0)),
                  pl.BlockSpec((1, A_DV), lambda b, h: (0, 0)),
                  pl.BlockSpec((N_MASKS, C, C), lambda b, h: (0, 0, 0)),
                  pl.BlockSpec((N_MASKS, C, C), lambda b, h: (0, 0, 0))],
        out_specs=pl.BlockSpec((1, L, A_DV), lambda b, h: (b, 0, h)),
        out_shape=jax.ShapeDtypeStruct((B, L, A_WIDTH), bf16),
        scratch_shapes=scratch,
        compiler_params=_cparams("parallel", "parallel"),
        name="deltanet",
    )(gate_sc, qkvz, abh, conv_w, onorm.reshape(1, A_DV), *_delta_masks())


def _filter_spectrum_kernel(c_ref, s_ref, hs_ref, hd_ref, kr_ref, ki_ref):
    L = c_ref.shape[1]
    rows = kr_ref.shape[1]
    scale = 1.0 / (2 * L)

    def split(x):
        hi = x.astype(bf16)
        return hi, (x - hi.astype(f32)).astype(bf16)

    hs_hi, hs_lo = split(hs_ref[...])
    hd_hi, hd_lo = split(hd_ref[...])
    for k1 in range(kr_ref.shape[0]):
        c = c_ref[k1 * rows:(k1 + 1) * rows, :]
        s = s_ref[k1 * rows:(k1 + 1) * rows, :]
        kr_ref[k1] = (_dot(c, hs_hi) + _dot(c, hs_lo)) * scale
        ki_ref[k1] = (_dot(s, hd_hi) + _dot(s, hd_lo)) * (-scale)


def _filter_spectrum(hsum, hdiff):
    L, n = hsum.shape
    ct = HY_CT
    n_fft = 2 * L
    n2 = L // FFT_Z1
    d = jnp.arange(L, dtype=jnp.int32)[None, :]
    a1 = ((jnp.arange(FFT_K1, dtype=jnp.int32)[:, None] * d) % n_fft).astype(f32) * (2.0 * math.pi / n_fft)
    a2 = ((jnp.arange(n2, dtype=jnp.int32)[:, None] * d) % n2).astype(f32) * (2.0 * math.pi / n2)
    c1, s1 = jnp.cos(a1)[:, None, :], jnp.sin(a1)[:, None, :]
    c2, s2 = jnp.cos(a2)[None, :, :], jnp.sin(a2)[None, :, :]
    cmat = (c1 * c2 - s1 * s2).reshape(FFT_K1 * n2, L).astype(bf16)
    smat = (s1 * c2 + c1 * s2).reshape(FFT_K1 * n2, L).astype(bf16)
    once = pl.Buffered(1)
    return pl.pallas_call(
        _filter_spectrum_kernel,
        grid=(n // ct,),
        in_specs=[pl.BlockSpec((FFT_K1 * n2, L), lambda j: (0, 0), pipeline_mode=once),
                  pl.BlockSpec((FFT_K1 * n2, L), lambda j: (0, 0), pipeline_mode=once),
                  pl.BlockSpec((L, ct), lambda j: (0, j)),
                  pl.BlockSpec((L, ct), lambda j: (0, j))],
        out_specs=[pl.BlockSpec((FFT_K1, n2, ct), lambda j: (0, 0, j)),
                   pl.BlockSpec((FFT_K1, n2, ct), lambda j: (0, 0, j))],
        out_shape=[jax.ShapeDtypeStruct((FFT_K1, n2, n), f32),
                   jax.ShapeDtypeStruct((FFT_K1, n2, n), f32)],
        compiler_params=_cparams("parallel"),
        name="filter_spectrum",
    )(cmat, smat, hsum, hdiff)


FFT_N1 = 16
FFT_K1 = FFT_N1 // 2 + 1
FFT_Z1 = FFT_N1 // 2
FFT_ROWS = 32


def _lincomb(terms):
    groups = {}
    for c, a in terms:
        if abs(c) > 1e-9:
            groups.setdefault(round(abs(c), 9), []).append((c > 0, a))
    total = None
    for mag, items in groups.items():
        pos = [a for p, a in items if p]
        neg = [a for p, a in items if not p]
        s = None
        for a in pos:
            s = a if s is None else s + a
        for a in neg:
            s = -a if s is None else s - a
        if mag != 1.0:
            s = s * mag
        total = s if total is None else total + s
    return total


def _hyena_fft_kernel(m2f_ref, m2i_ref, twr_ref, twi_ref, x1_ref, x2_ref, v_ref, w1_ref, w2_ref, wv_ref,
                      kr0_ref, ki0_ref, kr1_ref, ki1_ref, db_ref, o_ref, z_s, g_s, d_s, y_s, b_s):
    L, ct = z_s.shape
    n2 = L // FFT_Z1
    R = min(HY_ROWS, L)
    nt = L // R
    r1 = min(FFT_ROWS, n2)
    halo = 16
    rowi = lax.broadcasted_iota(jnp.int32, (R, 1), 0)
    ang = 2.0 * math.pi / FFT_N1
    c16 = [[math.cos(ang * a * b) for b in range(FFT_K1)] for a in range(FFT_Z1)]
    s16 = [[math.sin(ang * a * b) for b in range(FFT_K1)] for a in range(FFT_Z1)]

    def conv_into(dst, x_ref, w_ref):
        def body(t, carry):
            r0 = pl.multiple_of(t * R, R)
            x = x_ref[0, pl.ds(r0, R), :].astype(f32)
            rp = pl.multiple_of(jnp.maximum(r0 - halo, 0), halo)
            rn = pl.multiple_of(jnp.minimum(r0 + R, L - halo), halo)
            prow = jnp.where(t > 0, x_ref[0, pl.ds(rp, halo), :][halo - 1:halo, :].astype(f32), 0.0)
            nrow = jnp.where(t < nt - 1, x_ref[0, pl.ds(rn, halo), :][0:1, :].astype(f32), 0.0)
            prev = jnp.where(rowi == 0, prow, pltpu.roll(x, 1, 0))
            nxt = jnp.where(rowi == R - 1, nrow, pltpu.roll(x, R - 1, 0))
            w = w_ref[...]
            dst[pl.ds(r0, R), :] = prev * w[0:1, :] + x * w[1:2, :] + nxt * w[2:3, :]
            return carry
        lax.fori_loop(0, nt, body, 0)

    def cols(k1):
        return slice(k1 * ct, (k1 + 1) * ct)

    conv_into(z_s, v_ref, wv_ref)
    stages = ((x1_ref, w1_ref, kr0_ref, ki0_ref), (x2_ref, w2_ref, kr1_ref, ki1_ref))
    for o, (g_ref, gw_ref, kr_ref, ki_ref) in enumerate(stages):
        conv_into(g_s, g_ref, gw_ref)

        def forward1(rt, carry):
            r0 = pl.multiple_of(rt * r1, r1)
            zs = [z_s[pl.ds(a * n2 + r0, r1), :] for a in range(FFT_Z1)]
            for k1 in range(FFT_K1):
                ar = _lincomb([(c16[a][k1], zs[a]) for a in range(FFT_Z1)])
                ai = _lincomb([(-s16[a][k1], zs[a]) for a in range(FFT_Z1)])
                twr = twr_ref[pl.ds(r0, r1), cols(k1)]
                twi = twi_ref[pl.ds(r0, r1), cols(k1)]
                if ai is None:
                    pr, pi = ar * twr, -(ar * twi)
                else:
                    pr, pi = ar * twr + ai * twi, ai * twr - ar * twi
                d_s[pl.ds(r0, r1), cols(k1)] = pr.astype(bf16)
                d_s[pl.ds(n2 + r0, r1), cols(k1)] = pi.astype(bf16)
            return carry

        lax.fori_loop(0, n2 // r1, forward1, 0)

        for k1 in range(FFT_K1):
            xk = _dot(m2f_ref[...], d_s[:, cols(k1)])
            xr, xi = xk[0:n2, :], xk[n2:, :]
            kr, ki = kr_ref[k1], ki_ref[k1]
            y_s[0:n2, cols(k1)] = (xr * kr - xi * ki).astype(bf16)
            y_s[n2:2 * n2, cols(k1)] = (xr * ki + xi * kr).astype(bf16)

        for k1 in range(FFT_K1):
            bk = _dot(m2i_ref[...], y_s[:, cols(k1)])
            br, bi = bk[0:n2, :], bk[n2:, :]
            twr, twi = twr_ref[:, cols(k1)], twi_ref[:, cols(k1)]
            b_s[0:n2, cols(k1)] = br * twr - bi * twi
            b_s[n2:2 * n2, cols(k1)] = br * twi + bi * twr

        def inverse1(rt, carry):
            r0 = pl.multiple_of(rt * r1, r1)
            brs = [b_s[pl.ds(r0, r1), cols(j)] for j in range(FFT_K1)]
            bis = [b_s[pl.ds(n2 + r0, r1), cols(j)] for j in range(FFT_K1)]
            for a in range(FFT_Z1):
                terms = [(1.0, brs[0]), (-1.0 if a % 2 else 1.0, brs[FFT_K1 - 1])]
                for j in range(1, FFT_K1 - 1):
                    terms += [(2.0 * c16[a][j], brs[j]), (-2.0 * s16[a][j], bis[j])]
                rows = pl.ds(a * n2 + r0, r1)
                z = g_s[rows, :] * (_lincomb(terms) + z_s[rows, :] * db_ref[o:o + 1, :])
                if o + 1 < len(stages):
                    z_s[rows, :] = z
                else:
                    o_ref[0, rows, :] = z.astype(o_ref.dtype)
            return carry

        lax.fori_loop(0, n2 // r1, inverse1, 0)


def _hyena_tables(L):
    n_fft = 2 * L
    n2 = L // FFT_Z1
    ct = HY_CT
    idx = jnp.arange(n2, dtype=jnp.int32)
    a2 = ((idx[:, None] * idx[None, :]) % n2).astype(f32) * (2.0 * math.pi / n2)
    c2, s2 = jnp.cos(a2), jnp.sin(a2)
    m2f = jnp.block([[c2, s2], [-s2, c2]]).astype(bf16)
    m2i = jnp.block([[c2, -s2], [s2, c2]]).astype(bf16)
    at = (idx[:, None] * jnp.arange(FFT_K1, dtype=jnp.int32)[None, :]).astype(f32) * (2.0 * math.pi / n_fft)
    twr = jnp.broadcast_to(jnp.cos(at)[:, :, None], (n2, FFT_K1, ct)).reshape(n2, FFT_K1 * ct)
    twi = jnp.broadcast_to(jnp.sin(at)[:, :, None], (n2, FFT_K1, ct)).reshape(n2, FFT_K1 * ct)
    return m2f, m2i, twr, twi


def _hyena_fft(hx, conv_w, tables, kr, ki, dbias):
    B, L, _ = hx.shape
    ct = HY_CT
    nct = B_WIDTH // ct
    n2 = L // FFT_Z1
    wide = FFT_K1 * ct
    once = pl.Buffered(1)
    m2f, m2i, twr, twi = tables

    def xspec(part):
        return pl.BlockSpec((1, L, ct), lambda c, b: (b, 0, part * nct + c))

    def wspec(part):
        return pl.BlockSpec((8, ct), lambda c, b: (0, part * nct + c))

    def kspec(order):
        return pl.BlockSpec((FFT_K1, n2, ct), lambda c, b: (0, 0, order * nct + c), pipeline_mode=once)

    def const(shape):
        return pl.BlockSpec(shape, lambda c, b: (0, 0), pipeline_mode=once)

    return pl.pallas_call(
        _hyena_fft_kernel,
        grid=(nct, B),
        in_specs=[const((2 * n2, 2 * n2)), const((2 * n2, 2 * n2)), const((n2, wide)), const((n2, wide)),
                  xspec(0), xspec(1), xspec(2), wspec(0), wspec(1), wspec(2),
                  kspec(0), kspec(0), kspec(1), kspec(1),
                  pl.BlockSpec((8, ct), lambda c, b: (0, c))],
        out_specs=pl.BlockSpec((1, L, ct), lambda c, b: (b, 0, c)),
        out_shape=jax.ShapeDtypeStruct((B, L, B_WIDTH), bf16),
        scratch_shapes=[pltpu.VMEM((L, ct), f32), pltpu.VMEM((L, ct), f32),
                        pltpu.VMEM((2 * n2, wide), bf16), pltpu.VMEM((2 * n2, wide), bf16),
                        pltpu.VMEM((2 * n2, wide), f32)],
        compiler_params=_cparams("parallel", "parallel"),
        name="hyena_fft",
    )(m2f, m2i, twr, twi, hx, hx, hx, conv_w, conv_w, conv_w, kr, ki, kr, ki, dbias)


RT_E1, RT_E2, RT_W1, RT_W2, RT_R1, RT_R2 = 0, 1, 2, 3, 4, 5
ROUTE_TM = ROW_TM
PROJ_PARTS = 2
EXPERT_TM = 512
SC_WINDOW = 128


def _pack_bf16_pairs(x):
    n = x.shape[1] // 2
    bits = lax.bitcast_convert_type(x.astype(bf16).astype(f32), jnp.uint32)
    return bits[:, :n] | (bits[:, n:] >> 16)


def _unpack_bf16_pairs(p):
    hi = lax.bitcast_convert_type(p & jnp.uint32(0xFFFF0000), f32)
    lo = lax.bitcast_convert_type(p << 16, f32)
    return hi, lo


ROUTER_ROWS = 40


def _route_topk(lt):
    row = lax.broadcasted_iota(jnp.int32, lt.shape, 0)
    neg = -jnp.inf
    big = ROUTER_LANES
    gl = jnp.where(row < N_GROUPS, lt, neg)
    gmax = jnp.max(gl, axis=0, keepdims=True)
    gsel = jnp.min(jnp.where(gl == gmax, row, big), axis=0, keepdims=True)
    psel = 1.0 / jnp.sum(jnp.exp(gl - gmax), axis=0, keepdims=True)
    first = N_GROUPS + gsel * EXPERTS_PER_GROUP
    el = jnp.where((row >= first) & (row < first + EXPERTS_PER_GROUP), lt, neg)
    t1 = jnp.max(el, axis=0, keepdims=True)
    i1 = jnp.min(jnp.where(el == t1, row, big), axis=0, keepdims=True)
    el2 = jnp.where(row == i1, neg, el)
    t2 = jnp.max(el2, axis=0, keepdims=True)
    i2 = jnp.min(jnp.where(el2 == t2, row, big), axis=0, keepdims=True)
    ex = jnp.exp(t2 - t1)
    w1 = psel / (1.0 + ex)
    w2 = ex * w1
    return i1 - N_GROUPS, i2 - N_GROUPS, w1, w2


def _proj_route_kernel(x_ref, *refs, n_in):
    a_refs = refs[0:n_in]
    w_refs = refs[n_in:2 * n_in]
    g_ref, wr_ref, up_ref, xo_ref, hpa_ref, hpb_ref, rt_ref, rtt_ref, cnt_ref = refs[2 * n_in:]
    tm = xo_ref.shape[0]
    rp = tm // PROJ_PARTS
    parts = [slice(p * rp, (p + 1) * rp) for p in range(PROJ_PARTS)]
    xs = []
    for r in parts:
        x = x_ref[r, :]
        for a_ref, w_ref in zip(a_refs, w_refs):
            x = x + _dot(a_ref[r, :], w_ref[...])
        xo_ref[r, :] = x
        xs.append(x)
    hs = [_rms(x, g_ref[...]) for x in xs]
    for r, h in zip(parts, hs):
        hp = _pack_bf16_pairs(h)
        quarter = hp.shape[1] // 2
        hpa_ref[r, :] = hp[:, :quarter]
        hpb_ref[r, :] = hp[:, quarter:]
    tops = []
    for h in hs:
        h_hi = h.astype(bf16)
        h_lo = (h - h_hi.astype(f32)).astype(bf16)
        both = _dot(h_hi, wr_ref[...])
        logits = both[:, :ROUTER_LANES] + both[:, ROUTER_LANES:] + _dot(h_lo, wr_ref[:, :ROUTER_LANES])
        tops.append(_route_topk(logits.T[0:ROUTER_ROWS, :]))
    row = lax.broadcasted_iota(jnp.int32, (ROUTER_LANES, rp), 0)
    onehots = [jnp.where((row == e1) | (row == e2), 1.0, 0.0) for e1, e2, _, _ in tops]
    onehot16 = jnp.concatenate(onehots, axis=1).astype(bf16)
    before = _dot(onehot16, up_ref[...])
    for p, (r, (e1, e2, w1, w2)) in enumerate(zip(parts, tops)):
        bp = before[:, r]
        r1 = jnp.sum(jnp.where(row == e1, bp, 0.0), axis=0, keepdims=True)
        r2 = jnp.sum(jnp.where(row == e2, bp, 0.0), axis=0, keepdims=True)
        rec = jnp.zeros((ROUTER_LANES, rp), f32)
        for ln, val in ((RT_E1, e1.astype(f32)), (RT_E2, e2.astype(f32)), (RT_W1, w1), (RT_W2, w2),
                        (RT_R1, r1), (RT_R2, r2)):
            rec = jnp.where(row == ln, val, rec)
        rtt_ref[:, r] = rec[0:rtt_ref.shape[0], :]
        rt_ref[r, :] = rec.T
    ones = jnp.ones((cnt_ref.shape[1], tm), bf16)
    cnt_ref[0] = _dot_nt(ones, onehot16)


def _proj_route(x, acts, ws, gain, w_router):
    T, D = x.shape
    tm = ROUTE_TM
    n_in = len(acts)
    up = jnp.asarray(np.triu(np.ones((tm, tm), np.float32), 1), dtype=bf16)
    in_specs = [pl.BlockSpec((tm, D), lambda i: (i, 0))]
    in_specs += [pl.BlockSpec((tm, a.shape[1]), lambda i: (i, 0)) for a in acts]
    in_specs += [pl.BlockSpec(w.shape, lambda i: (0, 0)) for w in ws]
    in_specs += [pl.BlockSpec((1, D), lambda i: (0, 0)),
                 pl.BlockSpec((D, 2 * ROUTER_LANES), lambda i: (0, 0)),
                 pl.BlockSpec((tm, tm), lambda i: (0, 0))]
    return pl.pallas_call(
        functools.partial(_proj_route_kernel, n_in=n_in),
        grid=(T // tm,),
        in_specs=in_specs,
        out_specs=[pl.BlockSpec((tm, D), lambda i: (i, 0)),
                   pl.BlockSpec((tm, D // 4), lambda i: (i, 0)),
                   pl.BlockSpec((tm, D // 4), lambda i: (i, 0)),
                   pl.BlockSpec((tm, ROUTER_LANES), lambda i: (i, 0)),
                   pl.BlockSpec((8, tm), lambda i: (0, i)),
                   pl.BlockSpec((1, 8, ROUTER_LANES), lambda i: (i, 0, 0))],
        out_shape=[jax.ShapeDtypeStruct((T, D), f32),
                   jax.ShapeDtypeStruct((T, D // 4), jnp.uint32),
                   jax.ShapeDtypeStruct((T, D // 4), jnp.uint32),
                   jax.ShapeDtypeStruct((T, ROUTER_LANES), f32),
                   jax.ShapeDtypeStruct((8, T), f32),
                   jax.ShapeDtypeStruct((T // tm, 8, ROUTER_LANES), f32)],
        compiler_params=_cparams("parallel"),
        name="proj_route",
    )(x, *acts, *ws, gain.reshape(1, D), w_router, up)


def _moe_rows(T):
    return 2 * T + N_EXPERTS * EXPERT_TM


def _moe_plan(route_t, counts):
    T = route_t.shape[1]
    tm = ROUTE_TM
    cnt = counts[:, 0, :N_EXPERTS].astype(jnp.int32)
    total = jnp.sum(cnt, axis=0)
    padded = ((total + EXPERT_TM - 1) // EXPERT_TM) * EXPERT_TM
    ends = jnp.cumsum(padded)
    tile_base = (ends - padded)[None, :] + jnp.cumsum(cnt, axis=0) - cnt
    base = jnp.broadcast_to(tile_base.T[:, :, None], (N_EXPERTS, T // tm, tm)).reshape(N_EXPERTS, T)
    ids = jnp.arange(N_EXPERTS, dtype=jnp.int32)[:, None]
    dest = []
    for e_row, r_row in ((RT_E1, RT_R1), (RT_E2, RT_R2)):
        e = route_t[e_row].astype(jnp.int32)
        dest.append(jnp.sum(jnp.where(ids == e[None, :], base, 0), axis=0) + route_t[r_row].astype(jnp.int32))
    n_tiles = _moe_rows(T) // EXPERT_TM
    n_used = ends[-1] // EXPERT_TM
    starts = jnp.arange(n_tiles, dtype=jnp.int32) * EXPERT_TM
    tile_expert = jnp.sum((starts[:, None] >= ends[None, :]).astype(jnp.int32), axis=1)
    last_expert = jnp.sum((jnp.maximum(n_used - 1, 0) * EXPERT_TM >= ends).astype(jnp.int32))
    tile_expert = jnp.minimum(tile_expert, last_expert)
    tiles = jnp.arange(n_tiles, dtype=jnp.int32)
    prev_expert = jnp.concatenate([jnp.full((1,), -1, jnp.int32), tile_expert[:-1]])
    seg_first = ((tile_expert != prev_expert) & (tiles < n_used)).astype(jnp.int32)
    seg_slot = (jnp.cumsum(seg_first) - 1) % 2
    seg_end = jnp.sum(jnp.where(ids == tile_expert[None, :], (ends // EXPERT_TM)[:, None], 0), axis=0)
    nxt = jnp.where(seg_end < n_used, tile_expert[jnp.minimum(seg_end, n_tiles - 1)], -1)
    plan = (tile_expert, seg_first, seg_slot.astype(jnp.int32), nxt.astype(jnp.int32),
            n_used.reshape(1).astype(jnp.int32))
    return dest, plan


def _sc_mesh():
    return plsc.VectorSubcoreMesh(core_axis_name="core", subcore_axis_name="subcore")


def _sc_scatter_pair(rows, d0, d1, n_out):
    T, W = rows.shape
    win = SC_WINDOW

    @pl.kernel(out_type=jax.ShapeDtypeStruct((n_out, W), rows.dtype), mesh=_sc_mesh(), name="moe_sc_scatter")
    def scatter(x_hbm, i0_hbm, i1_hbm, o_hbm):
        def body(x_vmem, i0_vmem, i1_vmem):
            pltpu.sync_copy(x_vmem, o_hbm.at[i0_vmem.at[0]])
            pltpu.sync_copy(x_vmem, o_hbm.at[i1_vmem.at[0]])

        pltpu.emit_pipeline(
            body, grid=(T // win,),
            in_specs=[pl.BlockSpec((win, W), index_map=lambda i: (i, 0)),
                      pl.BlockSpec((1, win), index_map=lambda i: (0, i)),
                      pl.BlockSpec((1, win), index_map=lambda i: (0, i))],
            out_specs=[],
            core_axis_name=("core", "subcore"), dimension_semantics=(pltpu.PARALLEL,),
        )(x_hbm, i0_hbm, i1_hbm)

    return scatter(rows, d0.reshape(1, T), d1.reshape(1, T))


def _sc_gather(table, idx):
    n = idx.shape[0]
    W = table.shape[1]
    win = SC_WINDOW

    @pl.kernel(out_type=jax.ShapeDtypeStruct((n, W), table.dtype), mesh=_sc_mesh(), name="moe_sc_gather")
    def gather(t_hbm, i_hbm, o_hbm):
        def body(i_vmem, o_vmem):
            pltpu.sync_copy(t_hbm.at[i_vmem.at[0]], o_vmem)

        pltpu.emit_pipeline(
            body, grid=(n // win,),
            in_specs=[pl.BlockSpec((1, win), index_map=lambda i: (0, i))],
            out_specs=[pl.BlockSpec((win, W), index_map=lambda i: (i, 0))],
            core_axis_name=("core", "subcore"), dimension_semantics=(pltpu.PARALLEL,),
        )(i_hbm, o_hbm)

    return gather(table, idx.reshape(1, n))


def _expert_kernel(te_ref, first_ref, slot_ref, nxt_ref, nu_ref, xa_ref, xb_ref, wg_hbm, wu_hbm, wd_hbm,
                   ya_ref, yb_ref, wg_s, wu_s, wd_s, wg16, wu16, wd16, sem):
    i = pl.program_id(0)
    active = i < nu_ref[0]

    def weight_copies(expert, s):
        return (pltpu.make_async_copy(wg_hbm.at[expert], wg_s.at[s], sem.at[s, 0]),
                pltpu.make_async_copy(wu_hbm.at[expert], wu_s.at[s], sem.at[s, 1]),
                pltpu.make_async_copy(wd_hbm.at[expert], wd_s.at[s], sem.at[s, 2]))

    @pl.when(i == 0)
    def _():
        for c in weight_copies(te_ref[0], slot_ref[0]):
            c.start()

    @pl.when(active & (first_ref[i] == 1))
    def _():
        s = slot_ref[i]
        for c in weight_copies(te_ref[i], s):
            c.wait()

        @pl.when(nxt_ref[i] >= 0)
        def _():
            for c in weight_copies(nxt_ref[i], 1 - s):
                c.start()

        wg16[...] = wg_s[s].astype(bf16)
        wu16[...] = wu_s[s].astype(bf16)
        wd16[...] = wd_s[s].astype(bf16)

    @pl.when(active)
    def _():
        hi, lo = _unpack_bf16_pairs(jnp.concatenate([xa_ref[...], xb_ref[...]], axis=1))
        x = jnp.concatenate([hi, lo], axis=1).astype(bf16)
        gate = _dot(x, wg16[...])
        up = _dot(x, wu16[...])
        hg = (gate * _sigmoid(gate) * up).astype(bf16)
        y = _pack_bf16_pairs(_dot(hg, wd16[...]))
        quarter = y.shape[1] // 2
        ya_ref[...] = y[:, :quarter]
        yb_ref[...] = y[:, quarter:]

    @pl.when(jnp.logical_not(active))
    def _():
        ya_ref[...] = jnp.zeros(ya_ref.shape, ya_ref.dtype)
        yb_ref[...] = jnp.zeros(yb_ref.shape, yb_ref.dtype)


def _experts(plan, layer, xa, xb, wg, wu, wd):
    tile_expert, seg_first, seg_slot, nxt, n_used = plan
    off = layer * N_EXPERTS
    n_rows, quarter = xa.shape
    D = 4 * quarter
    tm = EXPERT_TM
    row_spec = pl.BlockSpec((tm, quarter), lambda i, *_: (i, 0))
    grid_spec = pltpu.PrefetchScalarGridSpec(
        num_scalar_prefetch=5,
        grid=(n_rows // tm,),
        in_specs=[row_spec, row_spec,
                  pl.BlockSpec(memory_space=pl.ANY), pl.BlockSpec(memory_space=pl.ANY), pl.BlockSpec(memory_space=pl.ANY)],
        out_specs=[row_spec, row_spec],
        scratch_shapes=[pltpu.VMEM((2, D, D_EXPERT), wg.dtype), pltpu.VMEM((2, D, D_EXPERT), wu.dtype),
                        pltpu.VMEM((2, D_EXPERT, D), wd.dtype),
                        pltpu.VMEM((D, D_EXPERT), bf16), pltpu.VMEM((D, D_EXPERT), bf16), pltpu.VMEM((D_EXPERT, D), bf16),
                        pltpu.SemaphoreType.DMA((2, 3))])
    return pl.pallas_call(
        _expert_kernel,
        grid_spec=grid_spec,
        out_shape=[jax.ShapeDtypeStruct((n_rows, quarter), jnp.uint32)] * 2,
        compiler_params=_cparams("arbitrary"),
        name="moe_experts",
    )(tile_expert + off, seg_first, seg_slot, jnp.where(nxt >= 0, nxt + off, -1), n_used, xa, xb, wg, wu, wd)


def _combine_kernel(x_ref, rt_ref, g_ref, a1_ref, b1_ref, a2_ref, b2_ref, *rest, final):
    half = x_ref.shape[1] // 2
    h1, l1 = _unpack_bf16_pairs(jnp.concatenate([a1_ref[...], b1_ref[...]], axis=1))
    h2, l2 = _unpack_bf16_pairs(jnp.concatenate([a2_ref[...], b2_ref[...]], axis=1))
    w1 = rt_ref[:, RT_W1:RT_W1 + 1]
    w2 = rt_ref[:, RT_W2:RT_W2 + 1]
    out = jnp.concatenate([x_ref[:, 0:half] + w1 * h1 + w2 * h2, x_ref[:, half:] + w1 * l1 + w2 * l2], axis=1)
    if final:
        (o_ref,) = rest
        o_ref[...] = _rms(out, g_ref[...])
    else:
        w_ref, o_ref, p_ref = rest
        o_ref[...] = out
        p_ref[...] = _dot(_rms(out, g_ref[...]).astype(bf16), w_ref[...]).astype(p_ref.dtype)


def _combine(x, route, ga, gb, gain, w_next=None):
    T, D = x.shape
    tm = ROUTE_TM
    n = T // tm
    quarter = D // 4
    final = w_next is None
    slot1 = pl.BlockSpec((tm, quarter), lambda i: (i, 0))
    slot2 = pl.BlockSpec((tm, quarter), lambda i: (i + n, 0))
    row_spec = pl.BlockSpec((tm, D), lambda i: (i, 0))
    in_specs = [row_spec, pl.BlockSpec((tm, ROUTER_LANES), lambda i: (i, 0)), pl.BlockSpec((1, D), lambda i: (0, 0)),
                slot1, slot1, slot2, slot2]
    args = [x, route, gain.reshape(1, D), ga, gb, ga, gb]
    out_specs = [row_spec]
    out_shape = [jax.ShapeDtypeStruct((T, D), f32)]
    if not final:
        n_next = w_next.shape[1]
        in_specs.append(pl.BlockSpec((D, n_next), lambda i: (0, 0)))
        args.append(w_next)
        out_specs.append(pl.BlockSpec((tm, n_next), lambda i: (i, 0)))
        out_shape.append(jax.ShapeDtypeStruct((T, n_next), bf16))
    outs = pl.pallas_call(
        functools.partial(_combine_kernel, final=final),
        grid=(n,),
        in_specs=in_specs,
        out_specs=out_specs,
        out_shape=out_shape,
        compiler_params=_cparams("parallel"),
        name="moe_combine",
    )(*args)
    return outs[0] if final else outs


def _moe(layer, x, hpa, hpb, route, route_t, counts, wg, wu, wd, gain, w_next=None):
    (d0, d1), plan = _moe_plan(route_t, counts)
    n_rows = _moe_rows(x.shape[0])
    xa = _sc_scatter_pair(hpa, d0, d1, n_rows)
    xb = _sc_scatter_pair(hpb, d0, d1, n_rows)
    ya, yb = _experts(plan, layer, xa, xb, wg, wu, wd)
    dcat = jnp.concatenate([d0, d1])
    return _combine(x, route, _sc_gather(ya, dcat), _sc_gather(yb, dcat), gain, w_next)


SWA_LOCKSTEP = 2


def _swa_bias_tables():
    blk = C_BLOCK
    qi = np.arange(blk)[:, None]
    kj = np.arange(3 * blk)[None, :]
    dist = np.abs(blk + qi - kj)
    tabs = np.full((3, C_KV_HEADS, C_GROUP * blk, 3 * blk), -np.inf, np.float32)
    for case in range(3):
        ok = dist <= WINDOW
        if case == 0:
            ok = ok & (kj >= blk)
        if case == 2:
            ok = ok & (kj < 2 * blk)
        for hh in range(C_HEADS):
            slope = 2.0 ** (-8.0 * (hh + 1) / C_HEADS)
            t = np.where(ok, -slope * LOG2E * dist, -np.inf)
            kvh, g = divmod(hh, C_GROUP)
            tabs[case, kvh, g * blk:(g + 1) * blk] = t
    return jnp.asarray(tabs)


def _swa_kernel(sink_ref, q_ref, kp_ref, kc_ref, kn_ref, vp_ref, vc_ref, vn_ref, bias_ref, o_ref):
    blk = C_BLOCK
    hd = C_HEAD_DIM
    kcat = jnp.concatenate([kp_ref[0], kc_ref[0], kn_ref[0]], axis=0)
    vcat = jnp.concatenate([vp_ref[0], vc_ref[0], vn_ref[0]], axis=0)
    q = q_ref[0]
    rblk = lax.broadcasted_iota(jnp.int32, (C_GROUP * blk, 1), 0) // blk
    lane = lax.broadcasted_iota(jnp.int32, (3 * blk, LANE - hd), 1)
    ones_pad = jnp.where(lane == 0, 1.0, 0.0).astype(bf16)
    q4s, sinks = [], []
    for kvh in range(C_KV_HEADS):
        q4s.append(jnp.concatenate(
            [q[:, (kvh * C_GROUP + g) * hd:(kvh * C_GROUP + g + 1) * hd] for g in range(C_GROUP)], axis=0))
        sk = jnp.zeros((C_GROUP * blk, 1), f32)
        for g in range(C_GROUP):
            sk = jnp.where(rblk == g, sink_ref[kvh * C_GROUP + g] * LOG2E, sk)
        sinks.append(sk)
    outs = []
    for h0 in range(0, C_KV_HEADS, SWA_LOCKSTEP):
        hs = range(h0, h0 + SWA_LOCKSTEP)
        ss = [_dot_nt(q4s[h], kcat[:, h * hd:(h + 1) * hd]) + bias_ref[0, h] for h in hs]
        ms = [jnp.maximum(jnp.max(s, axis=-1, keepdims=True), sinks[h]) for s, h in zip(ss, hs)]
        ps = [jnp.exp2(s - m).astype(bf16) for s, m in zip(ss, ms)]
        pvs = [_dot(p, jnp.concatenate([vcat[:, h * hd:(h + 1) * hd], ones_pad], axis=1)) for p, h in zip(ps, hs)]
        for pv, m, h in zip(pvs, ms, hs):
            den = pv[:, hd:hd + 1] + jnp.exp2(sinks[h] - m)
            o4 = pv[:, 0:hd] / den
            outs += [o4[g * blk:(g + 1) * blk, :] for g in range(C_GROUP)]
    o_ref[0] = jnp.concatenate(outs, axis=1).astype(o_ref.dtype)


def _swa(qkv, sinks):
    B, L, _ = qkv.shape
    nb = L // C_BLOCK
    dq = C_HEADS * C_HEAD_DIM
    dkv = C_KV_HEADS * C_HEAD_DIM
    k_col = dq // dkv
    v_col = k_col + 1

    def kv_spec(col, shift):
        return pl.BlockSpec((1, C_BLOCK, dkv),
                            lambda n, b: (b, jnp.clip(n + shift, 0, nb - 1), col))

    def edge_case(n, b):
        return (jnp.where(n == 0, 0, jnp.where(n == nb - 1, 2, 1)), 0, 0, 0)

    return pl.pallas_call(
        _swa_kernel,
        grid=(nb, B),
        in_specs=[pl.BlockSpec(memory_space=pltpu.SMEM),
                  pl.BlockSpec((1, C_BLOCK, dq), lambda n, b: (b, n, 0)),
                  kv_spec(k_col, -1), kv_spec(k_col, 0), kv_spec(k_col, 1),
                  kv_spec(v_col, -1), kv_spec(v_col, 0), kv_spec(v_col, 1),
                  pl.BlockSpec((1, C_KV_HEADS, C_GROUP * C_BLOCK, 3 * C_BLOCK), edge_case)],
        out_specs=pl.BlockSpec((1, C_BLOCK, dq), lambda n, b: (b, n, 0)),
        out_shape=jax.ShapeDtypeStruct((B, L, dq), bf16),
        compiler_params=_cparams("parallel", "parallel"),
        name="swa",
    )(sinks, qkv, qkv, qkv, qkv, qkv, qkv, qkv, _swa_bias_tables())


def _hyena_filters(L, w1, b1, w2, b2, w3, b3, w4, freq):
    hp = lax.Precision.HIGHEST
    t = jnp.linspace(0.0, 1.0, L, dtype=f32)[:, None]
    wpos = (2.0 * math.pi / L) * jnp.arange(L, dtype=f32)[:, None]
    bands = jnp.linspace(1e-4, HY_BANDS - 1, HY_BANDS, dtype=f32)[None, :]
    feats = jnp.concatenate([t, jnp.cos(bands * wpos), -jnp.sin(bands * wpos)], axis=-1)
    h = jnp.sin(freq[0] * (jnp.dot(feats, w1, precision=hp) + b1))
    h = jnp.sin(freq[1] * (jnp.dot(h, w2, precision=hp) + b2))
    h = jnp.sin(freq[2] * (jnp.dot(h, w3, precision=hp) + b3))
    h = jnp.dot(h, w4, precision=hp).reshape(L, HY_ORDER, 2, B_WIDTH)
    deltas = jnp.abs(jnp.linspace(HY_MIN_DECAY, HY_MAX_DECAY, B_WIDTH, dtype=f32))
    window = jnp.exp(-t * deltas[None, :])
    return h * window[:, None, None, :]


def _pad_rows(w, rows=8):
    return jnp.pad(w, ((0, rows - w.shape[0]), (0, 0)))


def _even_in_weights(w_in):
    hk = A_HEADS * A_DK
    q = w_in[:, 0:hk].reshape(D_MODEL, A_HEADS, A_DK)
    k = w_in[:, hk:2 * hk].reshape(D_MODEL, A_HEADS, A_DK)
    v = w_in[:, 2 * hk:A_QKV].reshape(D_MODEL, A_HEADS, A_DV)
    z = w_in[:, A_QKV:A_QKV + A_WIDTH].reshape(D_MODEL, A_HEADS, A_DV)
    qkvz = jnp.concatenate([q, k, v, z], axis=-1).reshape(D_MODEL, A_HEADS * 4 * A_DK)
    ab = w_in[:, A_QKV + A_WIDTH:A_QKV + A_WIDTH + 4 * A_HEADS].reshape(D_MODEL, 4, A_HEADS)
    ab = jnp.transpose(ab, (0, 2, 1))
    ab = jnp.pad(ab, ((0, 0), (0, 0), (0, LANE - 4))).reshape(D_MODEL, A_HEADS * LANE)
    hx = w_in[:, A_QKV + A_WIDTH + 4 * A_HEADS:]
    return jnp.concatenate([qkvz, ab, hx], axis=1).astype(bf16)


def _router_weights(w_group, w_expert):
    w = jnp.concatenate([w_group, w_expert], axis=1)
    w = jnp.pad(w, ((0, 0), (0, ROUTER_LANES - w.shape[1])))
    hi = w.astype(bf16)
    lo = (w - hi.astype(f32)).astype(bf16)
    return jnp.concatenate([hi, lo], axis=1)


def kernel(x_prompt, x_sample, norm_mix, norm_ffn, norm_final, ev_w_in, ev_conv_a, ev_alog_f, ev_alog_b, ev_dtb_f, ev_dtb_b, ev_onorm, ev_conv_b, hy_w1, hy_b1, hy_w2, hy_b2, hy_w3, hy_b3, hy_w4, hy_freq, hy_dbias, ev_w_out, od_w_qkv, od_sinks, od_w_out, moe_w_group, moe_w_expert, moe_w_gate, moe_w_up, moe_w_down):
    L = x_prompt.shape[1]
    n_layers = moe_w_gate.shape[0]
    wg = moe_w_gate.reshape(n_layers * N_EXPERTS, D_MODEL, D_EXPERT)
    wu = moe_w_up.reshape(n_layers * N_EXPERTS, D_MODEL, D_EXPERT)
    wd = moe_w_down.reshape(n_layers * N_EXPERTS, D_EXPERT, D_MODEL)

    w_in = _even_in_weights(ev_w_in[0])
    n_qkvz = A_HEADS * 4 * A_DK
    n_ab = A_HEADS * LANE
    n_hx = 3 * B_WIDTH
    hk = A_HEADS * A_DK
    ca = ev_conv_a[0]
    conv_a = jnp.concatenate([ca[0:hk].reshape(A_HEADS, A_DK, 3), ca[hk:2 * hk].reshape(A_HEADS, A_DK, 3),
                              ca[2 * hk:].reshape(A_HEADS, A_DV, 3)], axis=1)
    conv_a = jnp.pad(jnp.transpose(conv_a, (0, 2, 1)), ((0, 0), (0, 5), (0, 0)))
    gate_sc = jnp.stack([ev_alog_f[0], ev_alog_b[0], ev_dtb_f[0], ev_dtb_b[0]], axis=1)
    filt = _hyena_filters(L, hy_w1[0], hy_b1[0], hy_w2[0], hy_b2[0], hy_w3[0], hy_b3[0], hy_w4[0], hy_freq[0])
    hsum = (filt[:, :, 0] + filt[:, :, 1]).reshape(L, HY_ORDER * B_WIDTH)
    hdiff = (filt[:, :, 0] - filt[:, :, 1]).reshape(L, HY_ORDER * B_WIDTH)
    kr, ki = _filter_spectrum(hsum, hdiff)
    fft_tables = _hyena_tables(L)
    conv_b = _pad_rows(ev_conv_b[0].T)
    dbias = _pad_rows(hy_dbias[0])
    w_out = ev_w_out[0].astype(bf16)
    routers = [_router_weights(moe_w_group[l], moe_w_expert[l]) for l in range(n_layers)]
    n_qkv = od_w_qkv.shape[-1]
    dq = C_HEADS * C_HEAD_DIM
    w_qkv = jnp.concatenate([od_w_qkv[0][:, :dq] * (LOG2E * C_HEAD_DIM ** -0.5), od_w_qkv[0][:, dq:]], axis=1)
    w_qkv = w_qkv.astype(bf16)
    w_att_out = od_w_out[0].astype(bf16)

    def trunk(xb):
        B = xb.shape[0]
        T = B * L
        x = xb.reshape(T, D_MODEL)
        qkvz, abh, hx = _norm_matmul(x, norm_mix[0], w_in, (n_qkvz, n_ab, n_hx), (bf16, f32, bf16))
        o_a = _deltanet(qkvz.reshape(B, L, n_qkvz), abh.reshape(B, L, n_ab), conv_a, gate_sc, ev_onorm[0])
        o_b = _hyena_fft(hx.reshape(B, L, n_hx), conv_b, fft_tables, kr, ki, dbias)
        x, hpa, hpb, route, route_t, counts = _proj_route(
            x, (o_a.reshape(T, A_WIDTH), o_b.reshape(T, B_WIDTH)), (w_out[:A_WIDTH], w_out[A_WIDTH:]),
            norm_ffn[0], routers[0])
        x, qkv = _moe(0, x, hpa, hpb, route, route_t, counts, wg, wu, wd, norm_mix[1], w_qkv)
        o_c = _swa(qkv.reshape(B, L, n_qkv), od_sinks[0])
        x, hpa, hpb, route, route_t, counts = _proj_route(
            x, (o_c.reshape(T, C_HEADS * C_HEAD_DIM),), (w_att_out,), norm_ffn[1], routers[1])
        y = _moe(1, x, hpa, hpb, route, route_t, counts, wg, wu, wd, norm_final)
        return y.reshape(B, L, D_MODEL)

    return (trunk(x_prompt), trunk(x_sample))
```

```python
import functools
import math

import jax
import jax.numpy as jnp
import numpy as np
from jax import lax
from jax.experimental import pallas as pl
from jax.experimental.pallas import tpu as pltpu
from jax.experimental.pallas import tpu_sc as plsc

f32 = jnp.float32
bf16 = jnp.bfloat16

EPS = 1e-6
D_MODEL = 1024

A_HEADS = 4
A_DK = 128
A_DV = 128
A_WIDTH = A_HEADS * A_DV
A_QKV = 2 * A_HEADS * A_DK + A_HEADS * A_DV
DELTA_CHUNK = 128
INV_BASE = 8
DELTA_GROUP = 8

B_WIDTH = D_MODEL - A_WIDTH
HY_ORDER = 2
HY_EMB = 33
HY_BANDS = (HY_EMB - 1) // 2
HY_TARGET = 1e-2
HY_MIN_DECAY = math.log(HY_TARGET) / 1.5
HY_MAX_DECAY = math.log(HY_TARGET) / 0.3
HY_CT = 256
HY_ROWS = 256

C_HEADS = 16
C_KV_HEADS = 4
C_HEAD_DIM = 64
C_GROUP = C_HEADS // C_KV_HEADS
WINDOW = 128
C_BLOCK = 128

N_GROUPS = 4
EXPERTS_PER_GROUP = 8
N_EXPERTS = N_GROUPS * EXPERTS_PER_GROUP
D_EXPERT = 256
ROUTER_LANES = 128

LANE = 128
ROW_TM = 512
VMEM_LIMIT = 56 * 1024 * 1024
LOG2E = 1.4426950408889634


def _cparams(*sem):
    return pltpu.CompilerParams(dimension_semantics=sem, vmem_limit_bytes=VMEM_LIMIT)


def _dot(a, b):
    return jnp.dot(a, b, preferred_element_type=f32)


def _dot_nt(a, b):
    return lax.dot_general(a, b, (((1,), (1,)), ((), ())), preferred_element_type=f32)


def _dot_tn(a, b):
    return lax.dot_general(a, b, (((0,), (0,)), ((), ())), preferred_element_type=f32)


def _sigmoid(x):
    return 1.0 / (1.0 + jnp.exp(-x))


def _rms(x, g):
    return x * lax.rsqrt(jnp.mean(x * x, axis=-1, keepdims=True) + EPS) * g


def _norm_matmul_kernel(x_ref, g_ref, w_ref, *o_refs):
    h = _rms(x_ref[...], g_ref[...]).astype(bf16)
    off = 0
    for o_ref in o_refs:
        n = o_ref.shape[-1]
        o_ref[...] = _dot(h, w_ref[:, off:off + n]).astype(o_ref.dtype)
        off += n


def _norm_matmul(x, gain, w, splits, dtypes):
    T, D = x.shape
    tm = ROW_TM
    n_all = w.shape[1]
    return pl.pallas_call(
        _norm_matmul_kernel,
        grid=(T // tm,),
        in_specs=[pl.BlockSpec((tm, D), lambda i: (i, 0)),
                  pl.BlockSpec((1, D), lambda i: (0, 0)),
                  pl.BlockSpec((D, n_all), lambda i: (0, 0))],
        out_specs=[pl.BlockSpec((tm, n), lambda i: (i, 0)) for n in splits],
        out_shape=[jax.ShapeDtypeStruct((T, n), dt) for n, dt in zip(splits, dtypes)],
        compiler_params=_cparams("parallel"),
        name="norm_matmul",
    )(x, gain.reshape(1, D), w)


MK_LOWER, MK_UPPER, MK_BASE, MK_LEVEL0 = 0, 1, 2, 3
N_LEVELS = int(math.log2(DELTA_CHUNK // INV_BASE))
MK_EYE = MK_LEVEL0 + N_LEVELS
N_MASKS = MK_EYE + 1


def _delta_masks():
    c = np.arange(DELTA_CHUNK)[:, None]
    s = np.arange(DELTA_CHUNK)[None, :]
    m = [s <= c, s >= c, (c // INV_BASE == s // INV_BASE) & (c != s)]
    b = INV_BASE
    while b < DELTA_CHUNK:
        m.append((c // (2 * b) == s // (2 * b)) & (c // b != s // b))
        b *= 2
    m.append(c == s)
    m = np.stack(m).astype(np.float32)
    return jnp.asarray(m), jnp.asarray(m, dtype=bf16)


def _tri_inverse(mats, mk_ref, mk16_ref):
    ps = [a * mk_ref[MK_BASE] for a in mats]
    ts = [mk_ref[MK_EYE] - p for p in ps]
    n = 2
    while n < INV_BASE:
        p16s = [p.astype(bf16) for p in ps]
        ps = [_dot(p16, p16) for p16 in p16s]
        ts = [t + _dot(t.astype(bf16), p.astype(bf16)) for t, p in zip(ts, ps)]
        n *= 2
    a16s = [a.astype(bf16) for a in mats]
    t16s = [t.astype(bf16) for t in ts]
    for lvl in range(N_LEVELS):
        xs = [_dot(a16 * mk16_ref[MK_LEVEL0 + lvl], t16) for a16, t16 in zip(a16s, t16s)]
        t16s = [t16 - _dot(t16, x.astype(bf16)).astype(bf16) for t16, x in zip(t16s, xs)]
    return t16s


def _deltanet_kernel(sc_ref, x_ref, ab_ref, cw_ref, on_ref, mk_ref, mk16_ref, o_ref, mp_s, nr_s, gl_s, o_s):
    hd = pl.program_id(1)
    L = x_ref.shape[1]
    C = DELTA_CHUNK
    nchunk = L // C
    group = math.gcd(DELTA_GROUP, nchunk)
    nqkv = 3 * A_DK
    halo = 16

    lane = lax.broadcasted_iota(jnp.int32, (1, LANE), 1)
    alog = jnp.where(lane == 2, sc_ref[hd, 0], jnp.where(lane == 3, sc_ref[hd, 1], 0.0))
    dtb = jnp.where(lane == 2, sc_ref[hd, 2], jnp.where(lane == 3, sc_ref[hd, 3], 0.0))
    neg_a = jnp.where((lane == 2) | (lane == 3), -jnp.exp(alog), 0.0)
    rowi = lax.broadcasted_iota(jnp.int32, (C, 1), 0)
    cw = cw_ref[0]

    def chunk_inputs(i):
        r0 = pl.multiple_of(i * C, C)
        xc = x_ref[0, pl.ds(r0, C), 0:nqkv].astype(f32)
        rp = pl.multiple_of(jnp.maximum(r0 - halo, 0), halo)
        rn = pl.multiple_of(jnp.minimum(r0 + C, L - halo), halo)
        prow = x_ref[0, pl.ds(rp, halo), 0:nqkv][halo - 1:halo, :].astype(f32)
        nrow = x_ref[0, pl.ds(rn, halo), 0:nqkv][0:1, :].astype(f32)
        prow = jnp.where(i > 0, prow, 0.0)
        nrow = jnp.where(i < nchunk - 1, nrow, 0.0)
        prev = jnp.where(rowi == 0, prow, pltpu.roll(xc, 1, 0))
        nxt = jnp.where(rowi == C - 1, nrow, pltpu.roll(xc, C - 1, 0))
        y = prev * cw[0:1, :] + xc * cw[1:2, :] + nxt * cw[2:3, :]
        y = y * _sigmoid(y)
        q = y[:, 0:A_DK]
        k = y[:, A_DK:2 * A_DK]
        vc = y[:, 2 * A_DK:nqkv]
        qc = q * lax.rsqrt(jnp.sum(q * q, axis=-1, keepdims=True) + EPS) * (A_DK ** -0.5)
        kc = k * lax.rsqrt(jnp.sum(k * k, axis=-1, keepdims=True) + EPS)

        abc = ab_ref[0, pl.ds(r0, C), :]
        zg = abc + dtb
        g = neg_a * (jnp.maximum(zg, 0.0) + jnp.log(1.0 + jnp.exp(-jnp.abs(zg))))
        low16 = mk16_ref[MK_LOWER]
        g1 = g.astype(bf16)
        e1 = g - g1.astype(f32)
        g2 = e1.astype(bf16)
        g3 = (e1 - g2.astype(f32)).astype(bf16)
        pre = _dot(low16, g1) + _dot(low16, g2) + _dot(low16, g3)
        tot = pre[C - 1:C, :]
        suf = tot - pre + g
        beta_all = _sigmoid(abc)

        k16 = kc.astype(bf16)
        kk = _dot_nt(k16, k16)
        qk = _dot_nt(qc.astype(bf16), k16)

        chains = []
        for rev in (0, 1):
            beta = beta_all[:, rev:rev + 1]
            gcc = (suf if rev else pre)[:, 2 + rev:3 + rev]
            tot11 = tot[:, 2 + rev:3 + rev]
            keep = mk_ref[MK_UPPER if rev else MK_LOWER]
            bc = jnp.broadcast_to(gcc * LOG2E, (C, C))
            decay = jnp.exp2((bc - bc.T) * keep) * keep
            a = (beta * kk) * decay
            eg = jnp.exp(gcc)
            chains.append(dict(
                i=i, rev=rev, a=a, gl=jnp.exp(tot11), qd=qc * eg,
                rhs=jnp.concatenate([vc * beta, kc * (beta * eg)], axis=1).astype(bf16),
                kd16=(kc * jnp.exp(tot11 - gcc)).astype(bf16),
                qk16=(qk * decay).astype(bf16)))
        return chains

    def prep(it, carry):
        chains = []
        for gi in range(group):
            chains += chunk_inputs(it * group + gi)
        t16s = _tri_inverse([c["a"] for c in chains], mk_ref, mk16_ref)
        uws = [_dot(t16, c["rhs"]).astype(bf16) for t16, c in zip(t16s, chains)]
        kts = [_dot_tn(c["kd16"], uw) for c, uw in zip(chains, uws)]
        prs = [_dot(c["qk16"], uw) for c, uw in zip(chains, uws)]
        for c, kt, pr in zip(chains, kts, prs):
            rev, i = c["rev"], c["i"]
            mp_s[rev, i, 0:A_DK, :] = (-kt[:, A_DV:]).astype(bf16)
            mp_s[rev, i, A_DK:A_DK + C, :] = (c["qd"] - pr[:, A_DV:]).astype(bf16)
            nr_s[rev, i, 0:A_DK, :] = kt[:, 0:A_DV]
            nr_s[rev, i, A_DK:A_DK + C, :] = pr[:, 0:A_DV]
            gl_s[rev, i] = jnp.broadcast_to(c["gl"], (8, LANE))
        return carry

    lax.fori_loop(0, nchunk // group, prep, 0)

    def scan(i, carry):
        out = []
        for rev, S in enumerate(carry):
            j = (nchunk - 1 - i) if rev else i
            y = _dot(mp_s[rev, j], S.astype(bf16))
            nr = nr_s[rev, j]
            o_s[rev, pl.ds(pl.multiple_of(j * C, C), C), :] = y[A_DK:, :] + nr[A_DK:, :]
            out.append(S * gl_s[rev, j][0:1, :] + y[0:A_DK, :] + nr[0:A_DK, :])
        return tuple(out)

    S0 = jnp.zeros((A_DK, A_DV), f32)
    lax.fori_loop(0, nchunk, scan, (S0, S0))

    def finish(it, carry):
        for gi in range(group):
            r0 = pl.multiple_of((it * group + gi) * C, C)
            o = o_s[0, pl.ds(r0, C), :] + o_s[1, pl.ds(r0, C), :]
            zg = x_ref[0, pl.ds(r0, C), nqkv:nqkv + A_DV].astype(f32)
            o_ref[0, pl.ds(r0, C), :] = (_rms(o, on_ref[...]) * (zg * _sigmoid(zg))).astype(o_ref.dtype)
        return carry

    lax.fori_loop(0, nchunk // group, finish, 0)


def _deltanet(qkvz, abh, conv_w, gate_sc, onorm):
    B, L, _ = qkvz.shape
    C = DELTA_CHUNK
    nchunk = L // C
    hw = 4 * A_DK
    scratch = [pltpu.VMEM((2, nchunk, A_DK + C, A_DV), bf16),
               pltpu.VMEM((2, nchunk, A_DK + C, A_DV), f32),
               pltpu.VMEM((2, nchunk, 8, LANE), f32),
               pltpu.VMEM((2, L, A_DV), f32)]
    return pl.pallas_call(
        _deltanet_kernel,
        grid=(B, A_HEADS),
        in_specs=[pl.BlockSpec(memory_space=pltpu.SMEM),
                  pl.BlockSpec((1, L, hw), lambda b, h: (b, 0, h)),
                  pl.BlockSpec((1, L, LANE), lambda b, h: (b, 0, h)),
                  pl.BlockSpec((1, 8, 3 * A_DK), lambda b, h: (h, 0, 0)),
                  pl.BlockSpec((1, A_DV), lambda b, h: (0, 0)),
                  pl.BlockSpec((N_MASKS, C, C), lambda b, h: (0, 0, 0)),
                  pl.BlockSpec((N_MASKS, C, C), lambda b, h: (0, 0, 0))],
        out_specs=pl.BlockSpec((1, L, A_DV), lambda b, h: (b, 0, h)),
        out_shape=jax.ShapeDtypeStruct((B, L, A_WIDTH), bf16),
        scratch_shapes=scratch,
        compiler_params=_cparams("parallel", "parallel"),
        name="deltanet",
    )(gate_sc, qkvz, abh, conv_w, onorm.reshape(1, A_DV), *_delta_masks())


def _filter_spectrum_kernel(c_ref, s_ref, hs_ref, hd_ref, kr_ref, ki_ref):
    L = c_ref.shape[1]
    rows = kr_ref.shape[1]
    scale = 1.0 / (2 * L)

    def split(x):
        hi = x.astype(bf16)
        return hi, (x - hi.astype(f32)).astype(bf16)

    hs_hi, hs_lo = split(hs_ref[...])
    hd_hi, hd_lo = split(hd_ref[...])
    for k1 in range(kr_ref.shape[0]):
        c = c_ref[k1 * rows:(k1 + 1) * rows, :]
        s = s_ref[k1 * rows:(k1 + 1) * rows, :]
        kr_ref[k1] = (_dot(c, hs_hi) + _dot(c, hs_lo)) * scale
        ki_ref[k1] = (_dot(s, hd_hi) + _dot(s, hd_lo)) * (-scale)


def _filter_spectrum(hsum, hdiff):
    L, n = hsum.shape
    ct = HY_CT
    n_fft = 2 * L
    n2 = L // FFT_Z1
    d = jnp.arange(L, dtype=jnp.int32)[None, :]
    a1 = ((jnp.arange(FFT_K1, dtype=jnp.int32)[:, None] * d) % n_fft).astype(f32) * (2.0 * math.pi / n_fft)
    a2 = ((jnp.arange(n2, dtype=jnp.int32)[:, None] * d) % n2).astype(f32) * (2.0 * math.pi / n2)
    c1, s1 = jnp.cos(a1)[:, None, :], jnp.sin(a1)[:, None, :]
    c2, s2 = jnp.cos(a2)[None, :, :], jnp.sin(a2)[None, :, :]
    cmat = (c1 * c2 - s1 * s2).reshape(FFT_K1 * n2, L).astype(bf16)
    smat = (s1 * c2 + c1 * s2).reshape(FFT_K1 * n2, L).astype(bf16)
    once = pl.Buffered(1)
    return pl.pallas_call(
        _filter_spectrum_kernel,
        grid=(n // ct,),
        in_specs=[pl.BlockSpec((FFT_K1 * n2, L), lambda j: (0, 0), pipeline_mode=once),
                  pl.BlockSpec((FFT_K1 * n2, L), lambda j: (0, 0), pipeline_mode=once),
                  pl.BlockSpec((L, ct), lambda j: (0, j)),
                  pl.BlockSpec((L, ct), lambda j: (0, j))],
        out_specs=[pl.BlockSpec((FFT_K1, n2, ct), lambda j: (0, 0, j)),
                   pl.BlockSpec((FFT_K1, n2, ct), lambda j: (0, 0, j))],
        out_shape=[jax.ShapeDtypeStruct((FFT_K1, n2, n), f32),
                   jax.ShapeDtypeStruct((FFT_K1, n2, n), f32)],
        compiler_params=_cparams("parallel"),
        name="filter_spectrum",
    )(cmat, smat, hsum, hdiff)


FFT_N1 = 16
FFT_K1 = FFT_N1 // 2 + 1
FFT_Z1 = FFT_N1 // 2
FFT_ROWS = 32


def _lincomb(terms):
    groups = {}
    for c, a in terms:
        if abs(c) > 1e-9:
            groups.setdefault(round(abs(c), 9), []).append((c > 0, a))
    total = None
    for mag, items in groups.items():
        pos = [a for p, a in items if p]
        neg = [a for p, a in items if not p]
        s = None
        for a in pos:
            s = a if s is None else s + a
        for a in neg:
            s = -a if s is None else s - a
        if mag != 1.0:
            s = s * mag
        total = s if total is None else total + s
    return total


def _hyena_fft_kernel(m2f_ref, m2i_ref, twr_ref, twi_ref, x1_ref, x2_ref, v_ref, w1_ref, w2_ref, wv_ref,
                      kr0_ref, ki0_ref, kr1_ref, ki1_ref, db_ref, o_ref, z_s, g_s, d_s, y_s, b_s):
    L, ct = z_s.shape
    n2 = L // FFT_Z1
    R = min(HY_ROWS, L)
    nt = L // R
    r1 = min(FFT_ROWS, n2)
    halo = 16
    rowi = lax.broadcasted_iota(jnp.int32, (R, 1), 0)
    ang = 2.0 * math.pi / FFT_N1
    c16 = [[math.cos(ang * a * b) for b in range(FFT_K1)] for a in range(FFT_Z1)]
    s16 = [[math.sin(ang * a * b) for b in range(FFT_K1)] for a in range(FFT_Z1)]

    def conv_into(dst, x_ref, w_ref):
        def body(t, carry):
            r0 = pl.multiple_of(t * R, R)
            x = x_ref[0, pl.ds(r0, R), :].astype(f32)
            rp = pl.multiple_of(jnp.maximum(r0 - halo, 0), halo)
            rn = pl.multiple_of(jnp.minimum(r0 + R, L - halo), halo)
            prow = jnp.where(t > 0, x_ref[0, pl.ds(rp, halo), :][halo - 1:halo, :].astype(f32), 0.0)
            nrow = jnp.where(t < nt - 1, x_ref[0, pl.ds(rn, halo), :][0:1, :].astype(f32), 0.0)
            prev = jnp.where(rowi == 0, prow, pltpu.roll(x, 1, 0))
            nxt = jnp.where(rowi == R - 1, nrow, pltpu.roll(x, R - 1, 0))
            w = w_ref[...]
            dst[pl.ds(r0, R), :] = prev * w[0:1, :] + x * w[1:2, :] + nxt * w[2:3, :]
            return carry
        lax.fori_loop(0, nt, body, 0)

    def cols(k1):
        return slice(k1 * ct, (k1 + 1) * ct)

    conv_into(z_s, v_ref, wv_ref)
    stages = ((x1_ref, w1_ref, kr0_ref, ki0_ref), (x2_ref, w2_ref, kr1_ref, ki1_ref))
    for o, (g_ref, gw_ref, kr_ref, ki_ref) in enumerate(stages):
        conv_into(g_s, g_ref, gw_ref)

        def forward1(rt, carry):
            r0 = pl.multiple_of(rt * r1, r1)
            zs = [z_s[pl.ds(a * n2 + r0, r1), :] for a in range(FFT_Z1)]
            for k1 in range(FFT_K1):
                ar = _lincomb([(c16[a][k1], zs[a]) for a in range(FFT_Z1)])
                ai = _lincomb([(-s16[a][k1], zs[a]) for a in range(FFT_Z1)])
                twr = twr_ref[pl.ds(r0, r1), cols(k1)]
                twi = twi_ref[pl.ds(r0, r1), cols(k1)]
                if ai is None:
                    pr, pi = ar * twr, -(ar * twi)
                else:
                    pr, pi = ar * twr + ai * twi, ai * twr - ar * twi
                d_s[pl.ds(r0, r1), cols(k1)] = pr.astype(bf16)
                d_s[pl.ds(n2 + r0, r1), cols(k1)] = pi.astype(bf16)
            return carry

        lax.fori_loop(0, n2 // r1, forward1, 0)

        for k1 in range(FFT_K1):
            xk = _dot(m2f_ref[...], d_s[:, cols(k1)])
            xr, xi = xk[0:n2, :], xk[n2:, :]
            kr, ki = kr_ref[k1], ki_ref[k1]
            y_s[0:n2, cols(k1)] = (xr * kr - xi * ki).astype(bf16)
            y_s[n2:2 * n2, cols(k1)] = (xr * ki + xi * kr).astype(bf16)

        for k1 in range(FFT_K1):
            bk = _dot(m2i_ref[...], y_s[:, cols(k1)])
            br, bi = bk[0:n2, :], bk[n2:, :]
            twr, twi = twr_ref[:, cols(k1)], twi_ref[:, cols(k1)]
            b_s[0:n2, cols(k1)] = br * twr - bi * twi
            b_s[n2:2 * n2, cols(k1)] = br * twi + bi * twr

        def inverse1(rt, carry):
            r0 = pl.multiple_of(rt * r1, r1)
            brs = [b_s[pl.ds(r0, r1), cols(j)] for j in range(FFT_K1)]
            bis = [b_s[pl.ds(n2 + r0, r1), cols(j)] for j in range(FFT_K1)]
            for a in range(FFT_Z1):
                terms = [(1.0, brs[0]), (-1.0 if a % 2 else 1.0, brs[FFT_K1 - 1])]
                for j in range(1, FFT_K1 - 1):
                    terms += [(2.0 * c16[a][j], brs[j]), (-2.0 * s16[a][j], bis[j])]
                rows = pl.ds(a * n2 + r0, r1)
                z = g_s[rows, :] * (_lincomb(terms) + z_s[rows, :] * db_ref[o:o + 1, :])
                if o + 1 < len(stages):
                    z_s[rows, :] = z
                else:
                    o_ref[0, rows, :] = z.astype(o_ref.dtype)
            return carry

        lax.fori_loop(0, n2 // r1, inverse1, 0)


def _hyena_tables(L):
    n_fft = 2 * L
    n2 = L // FFT_Z1
    ct = HY_CT
    idx = jnp.arange(n2, dtype=jnp.int32)
    a2 = ((idx[:, None] * idx[None, :]) % n2).astype(f32) * (2.0 * math.pi / n2)
    c2, s2 = jnp.cos(a2), jnp.sin(a2)
    m2f = jnp.block([[c2, s2], [-s2, c2]]).astype(bf16)
    m2i = jnp.block([[c2, -s2], [s2, c2]]).astype(bf16)
    at = (idx[:, None] * jnp.arange(FFT_K1, dtype=jnp.int32)[None, :]).astype(f32) * (2.0 * math.pi / n_fft)
    twr = jnp.broadcast_to(jnp.cos(at)[:, :, None], (n2, FFT_K1, ct)).reshape(n2, FFT_K1 * ct)
    twi = jnp.broadcast_to(jnp.sin(at)[:, :, None], (n2, FFT_K1, ct)).reshape(n2, FFT_K1 * ct)
    return m2f, m2i, twr, twi


def _hyena_fft(hx, conv_w, tables, kr, ki, dbias):
    B, L, _ = hx.shape
    ct = HY_CT
    nct = B_WIDTH // ct
    n2 = L // FFT_Z1
    wide = FFT_K1 * ct
    once = pl.Buffered(1)
    m2f, m2i, twr, twi = tables

    def xspec(part):
        return pl.BlockSpec((1, L, ct), lambda c, b: (b, 0, part * nct + c))

    def wspec(part):
        return pl.BlockSpec((8, ct), lambda c, b: (0, part * nct + c))

    def kspec(order):
        return pl.BlockSpec((FFT_K1, n2, ct), lambda c, b: (0, 0, order * nct + c), pipeline_mode=once)

    def const(shape):
        return pl.BlockSpec(shape, lambda c, b: (0, 0), pipeline_mode=once)

    return pl.pallas_call(
        _hyena_fft_kernel,
        grid=(nct, B),
        in_specs=[const((2 * n2, 2 * n2)), const((2 * n2, 2 * n2)), const((n2, wide)), const((n2, wide)),
                  xspec(0), xspec(1), xspec(2), wspec(0), wspec(1), wspec(2),
                  kspec(0), kspec(0), kspec(1), kspec(1),
                  pl.BlockSpec((8, ct), lambda c, b: (0, c))],
        out_specs=pl.BlockSpec((1, L, ct), lambda c, b: (b, 0, c)),
        out_shape=jax.ShapeDtypeStruct((B, L, B_WIDTH), bf16),
        scratch_shapes=[pltpu.VMEM((L, ct), f32), pltpu.VMEM((L, ct), f32),
                        pltpu.VMEM((2 * n2, wide), bf16), pltpu.VMEM((2 * n2, wide), bf16),
                        pltpu.VMEM((2 * n2, wide), f32)],
        compiler_params=_cparams("parallel", "parallel"),
        name="hyena_fft",
    )(m2f, m2i, twr, twi, hx, hx, hx, conv_w, conv_w, conv_w, kr, ki, kr, ki, dbias)


RT_E1, RT_E2, RT_W1, RT_W2, RT_R1, RT_R2 = 0, 1, 2, 3, 4, 5
ROUTE_TM = ROW_TM
PROJ_PARTS = 2
EXPERT_TM = 512
WEIGHT_DMA_CHUNKS = 4
SC_WINDOW = 128


def _pack_bf16_pairs(x):
    n = x.shape[1] // 2
    bits = lax.bitcast_convert_type(x.astype(bf16).astype(f32), jnp.uint32)
    return bits[:, :n] | (bits[:, n:] >> 16)


def _unpack_bf16_pairs(p):
    hi = lax.bitcast_convert_type(p & jnp.uint32(0xFFFF0000), f32)
    lo = lax.bitcast_convert_type(p << 16, f32)
    return hi, lo


ROUTER_ROWS = 40


def _route_topk(lt):
    row = lax.broadcasted_iota(jnp.int32, lt.shape, 0)
    neg = -jnp.inf
    big = ROUTER_LANES
    gl = jnp.where(row < N_GROUPS, lt, neg)
    gmax = jnp.max(gl, axis=0, keepdims=True)
    gsel = jnp.min(jnp.where(gl == gmax, row, big), axis=0, keepdims=True)
    psel = 1.0 / jnp.sum(jnp.exp(gl - gmax), axis=0, keepdims=True)
    first = N_GROUPS + gsel * EXPERTS_PER_GROUP
    el = jnp.where((row >= first) & (row < first + EXPERTS_PER_GROUP), lt, neg)
    t1 = jnp.max(el, axis=0, keepdims=True)
    i1 = jnp.min(jnp.where(el == t1, row, big), axis=0, keepdims=True)
    el2 = jnp.where(row == i1, neg, el)
    t2 = jnp.max(el2, axis=0, keepdims=True)
    i2 = jnp.min(jnp.where(el2 == t2, row, big), axis=0, keepdims=True)
    ex = jnp.exp(t2 - t1)
    w1 = psel / (1.0 + ex)
    w2 = ex * w1
    return i1 - N_GROUPS, i2 - N_GROUPS, w1, w2


def _proj_route_kernel(x_ref, *refs, n_in):
    a_refs = refs[0:n_in]
    w_refs = refs[n_in:2 * n_in]
    g_ref, wr_ref, up_ref, xo_ref, hpa_ref, hpb_ref, rt_ref, rtt_ref, cnt_ref = refs[2 * n_in:]
    tm = xo_ref.shape[0]
    rp = tm // PROJ_PARTS
    parts = [slice(p * rp, (p + 1) * rp) for p in range(PROJ_PARTS)]
    xs = []
    for r in parts:
        x = x_ref[r, :]
        for a_ref, w_ref in zip(a_refs, w_refs):
            x = x + _dot(a_ref[r, :], w_ref[...])
        xo_ref[r, :] = x
        xs.append(x)
    hs = [_rms(x, g_ref[...]) for x in xs]
    for r, h in zip(parts, hs):
        hp = _pack_bf16_pairs(h)
        quarter = hp.shape[1] // 2
        hpa_ref[r, :] = hp[:, :quarter]
        hpb_ref[r, :] = hp[:, quarter:]
    tops = []
    for h in hs:
        h_hi = h.astype(bf16)
        h_lo = (h - h_hi.astype(f32)).astype(bf16)
        both = _dot(h_hi, wr_ref[...])
        logits = both[:, :ROUTER_LANES] + both[:, ROUTER_LANES:] + _dot(h_lo, wr_ref[:, :ROUTER_LANES])
        tops.append(_route_topk(logits.T[0:ROUTER_ROWS, :]))
    row = lax.broadcasted_iota(jnp.int32, (ROUTER_LANES, rp), 0)
    onehots = [jnp.where((row == e1) | (row == e2), 1.0, 0.0) for e1, e2, _, _ in tops]
    onehot16 = jnp.concatenate(onehots, axis=1).astype(bf16)
    before = _dot(onehot16, up_ref[...])
    for p, (r, (e1, e2, w1, w2)) in enumerate(zip(parts, tops)):
        bp = before[:, r]
        r1 = jnp.sum(jnp.where(row == e1, bp, 0.0), axis=0, keepdims=True)
        r2 = jnp.sum(jnp.where(row == e2, bp, 0.0), axis=0, keepdims=True)
        rec = jnp.zeros((ROUTER_LANES, rp), f32)
        for ln, val in ((RT_E1, e1.astype(f32)), (RT_E2, e2.astype(f32)), (RT_W1, w1), (RT_W2, w2),
                        (RT_R1, r1), (RT_R2, r2)):
            rec = jnp.where(row == ln, val, rec)
        rtt_ref[:, r] = rec[0:rtt_ref.shape[0], :]
        rt_ref[r, :] = rec.T
    ones = jnp.ones((cnt_ref.shape[1], tm), bf16)
    cnt_ref[0] = _dot_nt(ones, onehot16)


def _proj_route(x, acts, ws, gain, w_router):
    T, D = x.shape
    tm = ROUTE_TM
    n_in = len(acts)
    up = jnp.asarray(np.triu(np.ones((tm, tm), np.float32), 1), dtype=bf16)
    in_specs = [pl.BlockSpec((tm, D), lambda i: (i, 0))]
    in_specs += [pl.BlockSpec((tm, a.shape[1]), lambda i: (i, 0)) for a in acts]
    in_specs += [pl.BlockSpec(w.shape, lambda i: (0, 0)) for w in ws]
    in_specs += [pl.BlockSpec((1, D), lambda i: (0, 0)),
                 pl.BlockSpec((D, 2 * ROUTER_LANES), lambda i: (0, 0)),
                 pl.BlockSpec((tm, tm), lambda i: (0, 0))]
    return pl.pallas_call(
        functools.partial(_proj_route_kernel, n_in=n_in),
        grid=(T // tm,),
        in_specs=in_specs,
        out_specs=[pl.BlockSpec((tm, D), lambda i: (i, 0)),
                   pl.BlockSpec((tm, D // 4), lambda i: (i, 0)),
                   pl.BlockSpec((tm, D // 4), lambda i: (i, 0)),
                   pl.BlockSpec((tm, ROUTER_LANES), lambda i: (i, 0)),
                   pl.BlockSpec((8, tm), lambda i: (0, i)),
                   pl.BlockSpec((1, 8, ROUTER_LANES), lambda i: (i, 0, 0))],
        out_shape=[jax.ShapeDtypeStruct((T, D), f32),
                   jax.ShapeDtypeStruct((T, D // 4), jnp.uint32),
                   jax.ShapeDtypeStruct((T, D // 4), jnp.uint32),
                   jax.ShapeDtypeStruct((T, ROUTER_LANES), f32),
                   jax.ShapeDtypeStruct((8, T), f32),
                   jax.ShapeDtypeStruct((T // tm, 8, ROUTER_LANES), f32)],
        compiler_params=_cparams("parallel"),
        name="proj_route",
    )(x, *acts, *ws, gain.reshape(1, D), w_router, up)


def _moe_rows(T):
    return 2 * T + N_EXPERTS * EXPERT_TM


def _moe_plan(route_t, counts):
    T = route_t.shape[1]
    tm = ROUTE_TM
    cnt = counts[:, 0, :N_EXPERTS].astype(jnp.int32)
    total = jnp.sum(cnt, axis=0)
    padded = ((total + EXPERT_TM - 1) // EXPERT_TM) * EXPERT_TM
    ends = jnp.cumsum(padded)
    tile_base = (ends - padded)[None, :] + jnp.cumsum(cnt, axis=0) - cnt
    base = jnp.broadcast_to(tile_base.T[:, :, None], (N_EXPERTS, T // tm, tm)).reshape(N_EXPERTS, T)
    ids = jnp.arange(N_EXPERTS, dtype=jnp.int32)[:, None]
    dest = []
    for e_row, r_row in ((RT_E1, RT_R1), (RT_E2, RT_R2)):
        e = route_t[e_row].astype(jnp.int32)
        dest.append(jnp.sum(jnp.where(ids == e[None, :], base, 0), axis=0) + route_t[r_row].astype(jnp.int32))
    n_tiles = _moe_rows(T) // EXPERT_TM
    n_used = ends[-1] // EXPERT_TM
    starts = jnp.arange(n_tiles, dtype=jnp.int32) * EXPERT_TM
    tile_expert = jnp.sum((starts[:, None] >= ends[None, :]).astype(jnp.int32), axis=1)
    last_expert = jnp.sum((jnp.maximum(n_used - 1, 0) * EXPERT_TM >= ends).astype(jnp.int32))
    tile_expert = jnp.minimum(tile_expert, last_expert)
    tiles = jnp.arange(n_tiles, dtype=jnp.int32)
    prev_expert = jnp.concatenate([jnp.full((1,), -1, jnp.int32), tile_expert[:-1]])
    seg_first = ((tile_expert != prev_expert) & (tiles < n_used)).astype(jnp.int32)
    seg_slot = (jnp.cumsum(seg_first) - 1) % 2
    seg_end = jnp.sum(jnp.where(ids == tile_expert[None, :], (ends // EXPERT_TM)[:, None], 0), axis=0)
    nxt = jnp.where(seg_end < n_used, tile_expert[jnp.minimum(seg_end, n_tiles - 1)], -1)
    plan = (tile_expert, seg_first, seg_slot.astype(jnp.int32), nxt.astype(jnp.int32),
            n_used.reshape(1).astype(jnp.int32))
    return dest, plan


def _sc_mesh():
    return plsc.VectorSubcoreMesh(core_axis_name="core", subcore_axis_name="subcore")


def _sc_scatter_pair(rows, d0, d1, n_out):
    T, W = rows.shape
    win = SC_WINDOW

    @pl.kernel(out_type=jax.ShapeDtypeStruct((n_out, W), rows.dtype), mesh=_sc_mesh(), name="moe_sc_scatter")
    def scatter(x_hbm, i0_hbm, i1_hbm, o_hbm):
        def body(x_vmem, i0_vmem, i1_vmem):
            pltpu.sync_copy(x_vmem, o_hbm.at[i0_vmem.at[0]])
            pltpu.sync_copy(x_vmem, o_hbm.at[i1_vmem.at[0]])

        pltpu.emit_pipeline(
            body, grid=(T // win,),
            in_specs=[pl.BlockSpec((win, W), index_map=lambda i: (i, 0)),
                      pl.BlockSpec((1, win), index_map=lambda i: (0, i)),
                      pl.BlockSpec((1, win), index_map=lambda i: (0, i))],
            out_specs=[],
            core_axis_name=("core", "subcore"), dimension_semantics=(pltpu.PARALLEL,),
        )(x_hbm, i0_hbm, i1_hbm)

    return scatter(rows, d0.reshape(1, T), d1.reshape(1, T))


def _sc_gather(table, idx):
    n = idx.shape[0]
    W = table.shape[1]
    win = SC_WINDOW

    @pl.kernel(out_type=jax.ShapeDtypeStruct((n, W), table.dtype), mesh=_sc_mesh(), name="moe_sc_gather")
    def gather(t_hbm, i_hbm, o_hbm):
        def body(i_vmem, o_vmem):
            pltpu.sync_copy(t_hbm.at[i_vmem.at[0]], o_vmem)

        pltpu.emit_pipeline(
            body, grid=(n // win,),
            in_specs=[pl.BlockSpec((1, win), index_map=lambda i: (0, i))],
            out_specs=[pl.BlockSpec((win, W), index_map=lambda i: (i, 0))],
            core_axis_name=("core", "subcore"), dimension_semantics=(pltpu.PARALLEL,),
        )(i_hbm, o_hbm)

    return gather(table, idx.reshape(1, n))


def _expert_kernel(te_ref, first_ref, slot_ref, nxt_ref, nu_ref, xa_ref, xb_ref, wg_hbm, wu_hbm, wd_hbm,
                   ya_ref, yb_ref, wg_s, wu_s, wd_s, wg16, wu16, wd16, sem):
    i = pl.program_id(0)
    active = i < nu_ref[0]

    def weight_copies(expert, s):
        out = []
        for m, (hbm, buf) in enumerate(((wg_hbm, wg_s), (wu_hbm, wu_s), (wd_hbm, wd_s))):
            rows = buf.shape[1] // WEIGHT_DMA_CHUNKS
            for c in range(WEIGHT_DMA_CHUNKS):
                out.append(pltpu.make_async_copy(hbm.at[expert, pl.ds(c * rows, rows)],
                                                 buf.at[s, pl.ds(c * rows, rows)], sem.at[s, m]))
        return out

    @pl.when(i == 0)
    def _():
        for c in weight_copies(te_ref[0], slot_ref[0]):
            c.start()

    @pl.when(active & (first_ref[i] == 1))
    def _():
        s = slot_ref[i]
        for c in weight_copies(te_ref[i], s):
            c.wait()

        @pl.when(nxt_ref[i] >= 0)
        def _():
            for c in weight_copies(nxt_ref[i], 1 - s):
                c.start()

        wg16[...] = wg_s[s].astype(bf16)
        wu16[...] = wu_s[s].astype(bf16)
        wd16[...] = wd_s[s].astype(bf16)

    @pl.when(active)
    def _():
        hi, lo = _unpack_bf16_pairs(jnp.concatenate([xa_ref[...], xb_ref[...]], axis=1))
        x = jnp.concatenate([hi, lo], axis=1).astype(bf16)
        gate = _dot(x, wg16[...])
        up = _dot(x, wu16[...])
        hg = (gate * _sigmoid(gate) * up).astype(bf16)
        y = _pack_bf16_pairs(_dot(hg, wd16[...]))
        quarter = y.shape[1] // 2
        ya_ref[...] = y[:, :quarter]
        yb_ref[...] = y[:, quarter:]

    @pl.when(jnp.logical_not(active))
    def _():
        ya_ref[...] = jnp.zeros(ya_ref.shape, ya_ref.dtype)
        yb_ref[...] = jnp.zeros(yb_ref.shape, yb_ref.dtype)


def _experts(plan, layer, xa, xb, wg, wu, wd):
    tile_expert, seg_first, seg_slot, nxt, n_used = plan
    off = layer * N_EXPERTS
    n_rows, quarter = xa.shape
    D = 4 * quarter
    tm = EXPERT_TM
    row_spec = pl.BlockSpec((tm, quarter), lambda i, *_: (i, 0))
    grid_spec = pltpu.PrefetchScalarGridSpec(
        num_scalar_prefetch=5,
        grid=(n_rows // tm,),
        in_specs=[row_spec, row_spec,
                  pl.BlockSpec(memory_space=pl.ANY), pl.BlockSpec(memory_space=pl.ANY), pl.BlockSpec(memory_space=pl.ANY)],
        out_specs=[row_spec, row_spec],
        scratch_shapes=[pltpu.VMEM((2, D, D_EXPERT), wg.dtype), pltpu.VMEM((2, D, D_EXPERT), wu.dtype),
                        pltpu.VMEM((2, D_EXPERT, D), wd.dtype),
                        pltpu.VMEM((D, D_EXPERT), bf16), pltpu.VMEM((D, D_EXPERT), bf16), pltpu.VMEM((D_EXPERT, D), bf16),
                        pltpu.SemaphoreType.DMA((2, 3))])
    return pl.pallas_call(
        _expert_kernel,
        grid_spec=grid_spec,
        out_shape=[jax.ShapeDtypeStruct((n_rows, quarter), jnp.uint32)] * 2,
        compiler_params=_cparams("arbitrary"),
        name="moe_experts",
    )(tile_expert + off, seg_first, seg_slot, jnp.where(nxt >= 0, nxt + off, -1), n_used, xa, xb, wg, wu, wd)


def _combine_kernel(x_ref, rt_ref, g_ref, a1_ref, b1_ref, a2_ref, b2_ref, *rest, final):
    half = x_ref.shape[1] // 2
    h1, l1 = _unpack_bf16_pairs(jnp.concatenate([a1_ref[...], b1_ref[...]], axis=1))
    h2, l2 = _unpack_bf16_pairs(jnp.concatenate([a2_ref[...], b2_ref[...]], axis=1))
    w1 = rt_ref[:, RT_W1:RT_W1 + 1]
    w2 = rt_ref[:, RT_W2:RT_W2 + 1]
    out = jnp.concatenate([x_ref[:, 0:half] + w1 * h1 + w2 * h2, x_ref[:, half:] + w1 * l1 + w2 * l2], axis=1)
    if final:
        (o_ref,) = rest
        o_ref[...] = _rms(out, g_ref[...])
    else:
        w_ref, o_ref, p_ref = rest
        o_ref[...] = out
        p_ref[...] = _dot(_rms(out, g_ref[...]).astype(bf16), w_ref[...]).astype(p_ref.dtype)


def _combine(x, route, ga, gb, gain, w_next=None):
    T, D = x.shape
    tm = ROUTE_TM
    n = T // tm
    quarter = D // 4
    final = w_next is None
    slot1 = pl.BlockSpec((tm, quarter), lambda i: (i, 0))
    slot2 = pl.BlockSpec((tm, quarter), lambda i: (i + n, 0))
    row_spec = pl.BlockSpec((tm, D), lambda i: (i, 0))
    in_specs = [row_spec, pl.BlockSpec((tm, ROUTER_LANES), lambda i: (i, 0)), pl.BlockSpec((1, D), lambda i: (0, 0)),
                slot1, slot1, slot2, slot2]
    args = [x, route, gain.reshape(1, D), ga, gb, ga, gb]
    out_specs = [row_spec]
    out_shape = [jax.ShapeDtypeStruct((T, D), f32)]
    if not final:
        n_next = w_next.shape[1]
        in_specs.append(pl.BlockSpec((D, n_next), lambda i: (0, 0)))
        args.append(w_next)
        out_specs.append(pl.BlockSpec((tm, n_next), lambda i: (i, 0)))
        out_shape.append(jax.ShapeDtypeStruct((T, n_next), bf16))
    outs = pl.pallas_call(
        functools.partial(_combine_kernel, final=final),
        grid=(n,),
        in_specs=in_specs,
        out_specs=out_specs,
        out_shape=out_shape,
        compiler_params=_cparams("parallel"),
        name="moe_combine",
    )(*args)
    return outs[0] if final else outs


def _moe(layer, x, hpa, hpb, route, route_t, counts, wg, wu, wd, gain, w_next=None):
    (d0, d1), plan = _moe_plan(route_t, counts)
    n_rows = _moe_rows(x.shape[0])
    xa = _sc_scatter_pair(hpa, d0, d1, n_rows)
    xb = _sc_scatter_pair(hpb, d0, d1, n_rows)
    ya, yb = _experts(plan, layer, xa, xb, wg, wu, wd)
    dcat = jnp.concatenate([d0, d1])
    return _combine(x, route, _sc_gather(ya, dcat), _sc_gather(yb, dcat), gain, w_next)


SWA_LOCKSTEP = 2


def _swa_bias_tables():
    blk = C_BLOCK
    qi = np.arange(blk)[:, None]
    kj = np.arange(3 * blk)[None, :]
    dist = np.abs(blk + qi - kj)
    tabs = np.full((3, C_KV_HEADS, C_GROUP * blk, 3 * blk), -np.inf, np.float32)
    for case in range(3):
        ok = dist <= WINDOW
        if case == 0:
            ok = ok & (kj >= blk)
        if case == 2:
            ok = ok & (kj < 2 * blk)
        for hh in range(C_HEADS):
            slope = 2.0 ** (-8.0 * (hh + 1) / C_HEADS)
            t = np.where(ok, -slope * LOG2E * dist, -np.inf)
            kvh, g = divmod(hh, C_GROUP)
            tabs[case, kvh, g * blk:(g + 1) * blk] = t
    return jnp.asarray(tabs)


def _swa_kernel(sink_ref, q_ref, kp_ref, kc_ref, kn_ref, vp_ref, vc_ref, vn_ref, bias_ref, o_ref):
    blk = C_BLOCK
    hd = C_HEAD_DIM
    kcat = jnp.concatenate([kp_ref[0], kc_ref[0], kn_ref[0]], axis=0)
    vcat = jnp.concatenate([vp_ref[0], vc_ref[0], vn_ref[0]], axis=0)
    q = q_ref[0]
    rblk = lax.broadcasted_iota(jnp.int32, (C_GROUP * blk, 1), 0) // blk
    lane = lax.broadcasted_iota(jnp.int32, (3 * blk, LANE - hd), 1)
    ones_pad = jnp.where(lane == 0, 1.0, 0.0).astype(bf16)
    q4s, sinks = [], []
    for kvh in range(C_KV_HEADS):
        q4s.append(jnp.concatenate(
            [q[:, (kvh * C_GROUP + g) * hd:(kvh * C_GROUP + g + 1) * hd] for g in range(C_GROUP)], axis=0))
        sk = jnp.zeros((C_GROUP * blk, 1), f32)
        for g in range(C_GROUP):
            sk = jnp.where(rblk == g, sink_ref[kvh * C_GROUP + g] * LOG2E, sk)
        sinks.append(sk)
    outs = []
    for h0 in range(0, C_KV_HEADS, SWA_LOCKSTEP):
        hs = range(h0, h0 + SWA_LOCKSTEP)
        ss = [_dot_nt(q4s[h], kcat[:, h * hd:(h + 1) * hd]) + bias_ref[0, h] for h in hs]
        ms = [jnp.maximum(jnp.max(s, axis=-1, keepdims=True), sinks[h]) for s, h in zip(ss, hs)]
        ps = [jnp.exp2(s - m).astype(bf16) for s, m in zip(ss, ms)]
        pvs = [_dot(p, jnp.concatenate([vcat[:, h * hd:(h + 1) * hd], ones_pad], axis=1)) for p, h in zip(ps, hs)]
        for pv, m, h in zip(pvs, ms, hs):
            den = pv[:, hd:hd + 1] + jnp.exp2(sinks[h] - m)
            o4 = pv[:, 0:hd] / den
            outs += [o4[g * blk:(g + 1) * blk, :] for g in range(C_GROUP)]
    o_ref[0] = jnp.concatenate(outs, axis=1).astype(o_ref.dtype)


def _swa(qkv, sinks):
    B, L, _ = qkv.shape
    nb = L // C_BLOCK
    dq = C_HEADS * C_HEAD_DIM
    dkv = C_KV_HEADS * C_HEAD_DIM
    k_col = dq // dkv
    v_col = k_col + 1

    def kv_spec(col, shift):
        return pl.BlockSpec((1, C_BLOCK, dkv),
                            lambda n, b: (b, jnp.clip(n + shift, 0, nb - 1), col))

    def edge_case(n, b):
        return (jnp.where(n == 0, 0, jnp.where(n == nb - 1, 2, 1)), 0, 0, 0)

    return pl.pallas_call(
        _swa_kernel,
        grid=(nb, B),
        in_specs=[pl.BlockSpec(memory_space=pltpu.SMEM),
                  pl.BlockSpec((1, C_BLOCK, dq), lambda n, b: (b, n, 0)),
                  kv_spec(k_col, -1), kv_spec(k_col, 0), kv_spec(k_col, 1),
                  kv_spec(v_col, -1), kv_spec(v_col, 0), kv_spec(v_col, 1),
                  pl.BlockSpec((1, C_KV_HEADS, C_GROUP * C_BLOCK, 3 * C_BLOCK), edge_case)],
        out_specs=pl.BlockSpec((1, C_BLOCK, dq), lambda n, b: (b, n, 0)),
        out_shape=jax.ShapeDtypeStruct((B, L, dq), bf16),
        compiler_params=_cparams("parallel", "parallel"),
        name="swa",
    )(sinks, qkv, qkv, qkv, qkv, qkv, qkv, qkv, _swa_bias_tables())


def _hyena_filters(L, w1, b1, w2, b2, w3, b3, w4, freq):
    hp = lax.Precision.HIGHEST
    t = jnp.linspace(0.0, 1.0, L, dtype=f32)[:, None]
    wpos = (2.0 * math.pi / L) * jnp.arange(L, dtype=f32)[:, None]
    bands = jnp.linspace(1e-4, HY_BANDS - 1, HY_BANDS, dtype=f32)[None, :]
    feats = jnp.concatenate([t, jnp.cos(bands * wpos), -jnp.sin(bands * wpos)], axis=-1)
    h = jnp.sin(freq[0] * (jnp.dot(feats, w1, precision=hp) + b1))
    h = jnp.sin(freq[1] * (jnp.dot(h, w2, precision=hp) + b2))
    h = jnp.sin(freq[2] * (jnp.dot(h, w3, precision=hp) + b3))
    h = jnp.dot(h, w4, precision=hp).reshape(L, HY_ORDER, 2, B_WIDTH)
    deltas = jnp.abs(jnp.linspace(HY_MIN_DECAY, HY_MAX_DECAY, B_WIDTH, dtype=f32))
    window = jnp.exp(-t * deltas[None, :])
    return h * window[:, None, None, :]


def _pad_rows(w, rows=8):
    return jnp.pad(w, ((0, rows - w.shape[0]), (0, 0)))


def _even_in_weights(w_in):
    hk = A_HEADS * A_DK
    q = w_in[:, 0:hk].reshape(D_MODEL, A_HEADS, A_DK)
    k = w_in[:, hk:2 * hk].reshape(D_MODEL, A_HEADS, A_DK)
    v = w_in[:, 2 * hk:A_QKV].reshape(D_MODEL, A_HEADS, A_DV)
    z = w_in[:, A_QKV:A_QKV + A_WIDTH].reshape(D_MODEL, A_HEADS, A_DV)
    qkvz = jnp.concatenate([q, k, v, z], axis=-1).reshape(D_MODEL, A_HEADS * 4 * A_DK)
    ab = w_in[:, A_QKV + A_WIDTH:A_QKV + A_WIDTH + 4 * A_HEADS].reshape(D_MODEL, 4, A_HEADS)
    ab = jnp.transpose(ab, (0, 2, 1))
    ab = jnp.pad(ab, ((0, 0), (0, 0), (0, LANE - 4))).reshape(D_MODEL, A_HEADS * LANE)
    hx = w_in[:, A_QKV + A_WIDTH + 4 * A_HEADS:]
    return jnp.concatenate([qkvz, ab, hx], axis=1).astype(bf16)


def _router_weights(w_group, w_expert):
    w = jnp.concatenate([w_group, w_expert], axis=1)
    w = jnp.pad(w, ((0, 0), (0, ROUTER_LANES - w.shape[1])))
    hi = w.astype(bf16)
    lo = (w - hi.astype(f32)).astype(bf16)
    return jnp.concatenate([hi, lo], axis=1)


def kernel(x_prompt, x_sample, norm_mix, norm_ffn, norm_final, ev_w_in, ev_conv_a, ev_alog_f, ev_alog_b, ev_dtb_f, ev_dtb_b, ev_onorm, ev_conv_b, hy_w1, hy_b1, hy_w2, hy_b2, hy_w3, hy_b3, hy_w4, hy_freq, hy_dbias, ev_w_out, od_w_qkv, od_sinks, od_w_out, moe_w_group, moe_w_expert, moe_w_gate, moe_w_up, moe_w_down):
    L = x_prompt.shape[1]
    n_layers = moe_w_gate.shape[0]
    wg = moe_w_gate.reshape(n_layers * N_EXPERTS, D_MODEL, D_EXPERT)
    wu = moe_w_up.reshape(n_layers * N_EXPERTS, D_MODEL, D_EXPERT)
    wd = moe_w_down.reshape(n_layers * N_EXPERTS, D_EXPERT, D_MODEL)

    w_in = _even_in_weights(ev_w_in[0])
    n_qkvz = A_HEADS * 4 * A_DK
    n_ab = A_HEADS * LANE
    n_hx = 3 * B_WIDTH
    hk = A_HEADS * A_DK
    ca = ev_conv_a[0]
    conv_a = jnp.concatenate([ca[0:hk].reshape(A_HEADS, A_DK, 3), ca[hk:2 * hk].reshape(A_HEADS, A_DK, 3),
                              ca[2 * hk:].reshape(A_HEADS, A_DV, 3)], axis=1)
    conv_a = jnp.pad(jnp.transpose(conv_a, (0, 2, 1)), ((0, 0), (0, 5), (0, 0)))
    gate_sc = jnp.stack([ev_alog_f[0], ev_alog_b[0], ev_dtb_f[0], ev_dtb_b[0]], axis=1)
    filt = _hyena_filters(L, hy_w1[0], hy_b1[0], hy_w2[0], hy_b2[0], hy_w3[0], hy_b3[0], hy_w4[0], hy_freq[0])
    hsum = (filt[:, :, 0] + filt[:, :, 1]).reshape(L, HY_ORDER * B_WIDTH)
    hdiff = (filt[:, :, 0] - filt[:, :, 1]).reshape(L, HY_ORDER * B_WIDTH)
    kr, ki = _filter_spectrum(hsum, hdiff)
    fft_tables = _hyena_tables(L)
    conv_b = _pad_rows(ev_conv_b[0].T)
    dbias = _pad_rows(hy_dbias[0])
    w_out = ev_w_out[0].astype(bf16)
    routers = [_router_weights(moe_w_group[l], moe_w_expert[l]) for l in range(n_layers)]
    n_qkv = od_w_qkv.shape[-1]
    dq = C_HEADS * C_HEAD_DIM
    w_qkv = jnp.concatenate([od_w_qkv[0][:, :dq] * (LOG2E * C_HEAD_DIM ** -0.5), od_w_qkv[0][:, dq:]], axis=1)
    w_qkv = w_qkv.astype(bf16)
    w_att_out = od_w_out[0].astype(bf16)

    def trunk(xb):
        B = xb.shape[0]
        T = B * L
        x = xb.reshape(T, D_MODEL)
        qkvz, abh, hx = _norm_matmul(x, norm_mix[0], w_in, (n_qkvz, n_ab, n_hx), (bf16, f32, bf16))
        o_a = _deltanet(qkvz.reshape(B, L, n_qkvz), abh.reshape(B, L, n_ab), conv_a, gate_sc, ev_onorm[0])
        o_b = _hyena_fft(hx.reshape(B, L, n_hx), conv_b, fft_tables, kr, ki, dbias)
        x, hpa, hpb, route, route_t, counts = _proj_route(
            x, (o_a.reshape(T, A_WIDTH), o_b.reshape(T, B_WIDTH)), (w_out[:A_WIDTH], w_out[A_WIDTH:]),
            norm_ffn[0], routers[0])
        x, qkv = _moe(0, x, hpa, hpb, route, route_t, counts, wg, wu, wd, norm_mix[1], w_qkv)
        o_c = _swa(qkv.reshape(B, L, n_qkv), od_sinks[0])
        x, hpa, hpb, route, route_t, counts = _proj_route(
            x, (o_c.reshape(T, C_HEADS * C_HEAD_DIM),), (w_att_out,), norm_ffn[1], routers[1])
        y = _moe(1, x, hpa, hpb, route, route_t, counts, wg, wu, wd, norm_final)
        return y.reshape(B, L, D_MODEL)

    return (trunk(x_prompt), trunk(x_sample))
```

```python
import functools
import math

import jax
import jax.numpy as jnp
import numpy as np
from jax import lax
from jax.experimental import pallas as pl
from jax.experimental.pallas import tpu as pltpu
from jax.experimental.pallas import tpu_sc as plsc

f32 = jnp.float32
bf16 = jnp.bfloat16

EPS = 1e-6
D_MODEL = 1024

A_HEADS = 4
A_DK = 128
A_DV = 128
A_WIDTH = A_HEADS * A_DV
A_QKV = 2 * A_HEADS * A_DK + A_HEADS * A_DV
DELTA_CHUNK = 128
INV_BASE = 8
DELTA_GROUP = 8
DELTA_WAVES = 2

B_WIDTH = D_MODEL - A_WIDTH
HY_ORDER = 2
HY_EMB = 33
HY_BANDS = (HY_EMB - 1) // 2
HY_TARGET = 1e-2
HY_MIN_DECAY = math.log(HY_TARGET) / 1.5
HY_MAX_DECAY = math.log(HY_TARGET) / 0.3
HY_CT = 256
HY_ROWS = 256

C_HEADS = 16
C_KV_HEADS = 4
C_HEAD_DIM = 64
C_GROUP = C_HEADS // C_KV_HEADS
WINDOW = 128
C_BLOCK = 128

N_GROUPS = 4
EXPERTS_PER_GROUP = 8
N_EXPERTS = N_GROUPS * EXPERTS_PER_GROUP
D_EXPERT = 256
ROUTER_LANES = 128

LANE = 128
ROW_TM = 512
VMEM_LIMIT = 56 * 1024 * 1024
LOG2E = 1.4426950408889634


def _cparams(*sem):
    return pltpu.CompilerParams(dimension_semantics=sem, vmem_limit_bytes=VMEM_LIMIT)


def _dot(a, b):
    return jnp.dot(a, b, preferred_element_type=f32)


def _dot_nt(a, b):
    return lax.dot_general(a, b, (((1,), (1,)), ((), ())), preferred_element_type=f32)


def _dot_tn(a, b):
    return lax.dot_general(a, b, (((0,), (0,)), ((), ())), preferred_element_type=f32)


def _sigmoid(x):
    return 1.0 / (1.0 + jnp.exp(-x))


def _rms(x, g):
    return x * lax.rsqrt(jnp.mean(x * x, axis=-1, keepdims=True) + EPS) * g


def _even_in_kernel(x_ref, g_ref, w_ref, qkvz_ref, ab_ref, hx_ref):
    h = _rms(x_ref[...], g_ref[...]).astype(bf16)
    n_qkvz = qkvz_ref.shape[1]
    qkvz_ref[...] = _dot(h, w_ref[:, 0:n_qkvz]).astype(qkvz_ref.dtype)
    ab = _dot(h, w_ref[:, n_qkvz:n_qkvz + LANE])
    for hd in range(A_HEADS):
        ab_ref[:, hd * LANE:(hd + 1) * LANE] = ab if hd == 0 else pltpu.roll(ab, LANE - 4 * hd, 1)
    hx_ref[...] = _dot(h, w_ref[:, n_qkvz + LANE:]).astype(hx_ref.dtype)


def _even_in_proj(x, gain, w):
    T, D = x.shape
    tm = ROW_TM
    n_qkvz = A_HEADS * 4 * A_DK
    n_ab = A_HEADS * LANE
    n_hx = 3 * B_WIDTH
    return pl.pallas_call(
        _even_in_kernel,
        grid=(T // tm,),
        in_specs=[pl.BlockSpec((tm, D), lambda i: (i, 0)),
                  pl.BlockSpec((1, D), lambda i: (0, 0)),
                  pl.BlockSpec(w.shape, lambda i: (0, 0))],
        out_specs=[pl.BlockSpec((tm, n_qkvz), lambda i: (i, 0)),
                   pl.BlockSpec((tm, n_ab), lambda i: (i, 0)),
                   pl.BlockSpec((tm, n_hx), lambda i: (i, 0))],
        out_shape=[jax.ShapeDtypeStruct((T, n_qkvz), bf16),
                   jax.ShapeDtypeStruct((T, n_ab), f32),
                   jax.ShapeDtypeStruct((T, n_hx), bf16)],
        compiler_params=_cparams("parallel"),
        name="even_in_proj",
    )(x, gain.reshape(1, D), w)


MK_LOWER, MK_UPPER, MK_BASE, MK_LEVEL0 = 0, 1, 2, 3
N_LEVELS = int(math.log2(DELTA_CHUNK // INV_BASE))
MK_EYE = MK_LEVEL0 + N_LEVELS
N_MASKS = MK_EYE + 1


def _delta_masks():
    c = np.arange(DELTA_CHUNK)[:, None]
    s = np.arange(DELTA_CHUNK)[None, :]
    m = [s <= c, s >= c, (c // INV_BASE == s // INV_BASE) & (c != s)]
    b = INV_BASE
    while b < DELTA_CHUNK:
        m.append((c // (2 * b) == s // (2 * b)) & (c // b != s // b))
        b *= 2
    m.append(c == s)
    m = np.stack(m).astype(np.float32)
    return jnp.asarray(m), jnp.asarray(m, dtype=bf16)


def _tri_inverse(mats, mk_ref, mk16_ref):
    ps = [a * mk_ref[MK_BASE] for a in mats]
    ts = [mk_ref[MK_EYE] - p for p in ps]
    n = 2
    while n < INV_BASE:
        p16s = [p.astype(bf16) for p in ps]
        ps = [_dot(p16, p16) for p16 in p16s]
        ts = [t + _dot(t.astype(bf16), p.astype(bf16)) for t, p in zip(ts, ps)]
        n *= 2
    a16s = [a.astype(bf16) for a in mats]
    t16s = [t.astype(bf16) for t in ts]
    for lvl in range(N_LEVELS):
        xs = [_dot(a16 * mk16_ref[MK_LEVEL0 + lvl], t16) for a16, t16 in zip(a16s, t16s)]
        t16s = [t16 - _dot(t16, x.astype(bf16)).astype(bf16) for t16, x in zip(t16s, xs)]
    return t16s


def _deltanet_kernel(sc_ref, x_ref, ab_ref, cw_ref, on_ref, mk_ref, mk16_ref, o_ref, mp_s, nr_s, gl_s, o_s):
    hd = pl.program_id(1)
    L = x_ref.shape[1]
    C = DELTA_CHUNK
    nchunk = L // C
    group = math.gcd(DELTA_GROUP, nchunk)
    nqkv = 3 * A_DK
    halo = 16

    lane = lax.broadcasted_iota(jnp.int32, (1, LANE), 1)
    alog = jnp.where(lane == 2, sc_ref[hd, 0], jnp.where(lane == 3, sc_ref[hd, 1], 0.0))
    dtb = jnp.where(lane == 2, sc_ref[hd, 2], jnp.where(lane == 3, sc_ref[hd, 3], 0.0))
    neg_a = jnp.where((lane == 2) | (lane == 3), -jnp.exp(alog), 0.0)
    rowi = lax.broadcasted_iota(jnp.int32, (C, 1), 0)
    cw = cw_ref[0]

    def chunk_inputs(i):
        r0 = pl.multiple_of(i * C, C)
        xc = x_ref[0, pl.ds(r0, C), 0:nqkv].astype(f32)
        rp = pl.multiple_of(jnp.maximum(r0 - halo, 0), halo)
        rn = pl.multiple_of(jnp.minimum(r0 + C, L - halo), halo)
        prow = x_ref[0, pl.ds(rp, halo), 0:nqkv][halo - 1:halo, :].astype(f32)
        nrow = x_ref[0, pl.ds(rn, halo), 0:nqkv][0:1, :].astype(f32)
        prow = jnp.where(i > 0, prow, 0.0)
        nrow = jnp.where(i < nchunk - 1, nrow, 0.0)
        prev = jnp.where(rowi == 0, prow, pltpu.roll(xc, 1, 0))
        nxt = jnp.where(rowi == C - 1, nrow, pltpu.roll(xc, C - 1, 0))
        y = prev * cw[0:1, :] + xc * cw[1:2, :] + nxt * cw[2:3, :]
        y = y * _sigmoid(y)
        q = y[:, 0:A_DK]
        k = y[:, A_DK:2 * A_DK]
        vc = y[:, 2 * A_DK:nqkv]
        qc = q * lax.rsqrt(jnp.sum(q * q, axis=-1, keepdims=True) + EPS) * (A_DK ** -0.5)
        kc = k * lax.rsqrt(jnp.sum(k * k, axis=-1, keepdims=True) + EPS)

        abc = ab_ref[0, pl.ds(r0, C), :]
        zg = abc + dtb
        g = neg_a * (jnp.maximum(zg, 0.0) + jnp.log(1.0 + jnp.exp(-jnp.abs(zg))))
        low16 = mk16_ref[MK_LOWER]
        g1 = g.astype(bf16)
        e1 = g - g1.astype(f32)
        g2 = e1.astype(bf16)
        g3 = (e1 - g2.astype(f32)).astype(bf16)
        pre = _dot(low16, g1) + _dot(low16, g2) + _dot(low16, g3)
        tot = pre[C - 1:C, :]
        suf = tot - pre + g
        beta_all = _sigmoid(abc)

        k16 = kc.astype(bf16)
        kk = _dot_nt(k16, k16)
        qk = _dot_nt(qc.astype(bf16), k16)

        chains = []
        for rev in (0, 1):
            beta = beta_all[:, rev:rev + 1]
            gcc = (suf if rev else pre)[:, 2 + rev:3 + rev]
            tot11 = tot[:, 2 + rev:3 + rev]
            keep = mk_ref[MK_UPPER if rev else MK_LOWER]
            bc = jnp.broadcast_to(gcc * LOG2E, (C, C))
            decay = jnp.exp2((bc - bc.T) * keep) * keep
            a = (beta * kk) * decay
            eg = jnp.exp(gcc)
            chains.append(dict(
                i=i, rev=rev, a=a, gl=jnp.exp(tot11), qd=qc * eg,
                rhs=jnp.concatenate([vc * beta, kc * (beta * eg)], axis=1).astype(bf16),
                kd16=(kc * jnp.exp(tot11 - gcc)).astype(bf16),
                qk16=(qk * decay).astype(bf16)))
        return chains

    def prep_group(first_chunk):
        chains = []
        for gi in range(group):
            chains += chunk_inputs(first_chunk + gi)
        t16s = _tri_inverse([c["a"] for c in chains], mk_ref, mk16_ref)
        uws = [_dot(t16, c["rhs"]).astype(bf16) for t16, c in zip(t16s, chains)]
        kts = [_dot_tn(c["kd16"], uw) for c, uw in zip(chains, uws)]
        prs = [_dot(c["qk16"], uw) for c, uw in zip(chains, uws)]
        for c, kt, pr in zip(chains, kts, prs):
            rev, i = c["rev"], c["i"]
            mp_s[rev, i, 0:A_DK, :] = (-kt[:, A_DV:]).astype(bf16)
            mp_s[rev, i, A_DK:A_DK + C, :] = (c["qd"] - pr[:, A_DV:]).astype(bf16)
            nr_s[rev, i, 0:A_DK, :] = kt[:, 0:A_DV]
            nr_s[rev, i, A_DK:A_DK + C, :] = pr[:, 0:A_DV]
            gl_s[rev, i] = jnp.broadcast_to(c["gl"], (8, LANE))

    def prep(it, carry):
        for wv in range(waves):
            prep_group((it * waves + wv) * group)
        return carry

    waves = math.gcd(DELTA_WAVES, nchunk // group)
    lax.fori_loop(0, nchunk // (group * waves), prep, 0)

    def scan(i, carry):
        out = []
        for rev, S in enumerate(carry):
            j = (nchunk - 1 - i) if rev else i
            y = _dot(mp_s[rev, j], S.astype(bf16))
            nr = nr_s[rev, j]
            o_s[rev, pl.ds(pl.multiple_of(j * C, C), C), :] = y[A_DK:, :] + nr[A_DK:, :]
            out.append(S * gl_s[rev, j][0:1, :] + y[0:A_DK, :] + nr[0:A_DK, :])
        return tuple(out)

    S0 = jnp.zeros((A_DK, A_DV), f32)
    lax.fori_loop(0, nchunk, scan, (S0, S0))

    def finish(it, carry):
        for gi in range(group):
            r0 = pl.multiple_of((it * group + gi) * C, C)
            o = o_s[0, pl.ds(r0, C), :] + o_s[1, pl.ds(r0, C), :]
            zg = x_ref[0, pl.ds(r0, C), nqkv:nqkv + A_DV].astype(f32)
            o_ref[0, pl.ds(r0, C), :] = (_rms(o, on_ref[...]) * (zg * _sigmoid(zg))).astype(o_ref.dtype)
        return carry

    lax.fori_loop(0, nchunk // group, finish, 0)


def _deltanet(qkvz, abh, conv_w, gate_sc, onorm):
    B, L, _ = qkvz.shape
    C = DELTA_CHUNK
    nchunk = L // C
    hw = 4 * A_DK
    scratch = [pltpu.VMEM((2, nchunk, A_DK + C, A_DV), bf16),
               pltpu.VMEM((2, nchunk, A_DK + C, A_DV), f32),
               pltpu.VMEM((2, nchunk, 8, LANE), f32),
               pltpu.VMEM((2, L, A_DV), f32)]
    return pl.pallas_call(
        _deltanet_kernel,
        grid=(B, A_HEADS),
        in_specs=[pl.BlockSpec(memory_space=pltpu.SMEM),
                  pl.BlockSpec((1, L, hw), lambda b, h: (b, 0, h)),
                  pl.BlockSpec((1, L, LANE), lambda b, h: (b, 0, h)),
                  pl.BlockSpec((1, 8, 3 * A_DK), lambda b, h: (h, 0, 0)),
                  pl.BlockSpec((1, A_DV), lambda b, h: (0, 0)),
                  pl.BlockSpec((N_MASKS, C, C), lambda b, h: (0, 0, 0)),
                  pl.BlockSpec((N_MASKS, C, C), lambda b, h: (0, 0, 0))],
        out_specs=pl.BlockSpec((1, L, A_DV), lambda b, h: (b, 0, h)),
        out_shape=jax.ShapeDtypeStruct((B, L, A_WIDTH), bf16),
        scratch_shapes=scratch,
        compiler_params=_cparams("parallel", "parallel"),
        name="deltanet",
    )(gate_sc, qkvz, abh, conv_w, onorm.reshape(1, A_DV), *_delta_masks())


def _filter_spectrum_kernel(c_ref, s_ref, hs_ref, hd_ref, kr_ref, ki_ref):
    L = c_ref.shape[1]
    rows = kr_ref.shape[1]
    scale = 1.0 / (2 * L)

    def split(x):
        hi = x.astype(bf16)
        return hi, (x - hi.astype(f32)).astype(bf16)

    hs_hi, hs_lo = split(hs_ref[...])
    hd_hi, hd_lo = split(hd_ref[...])
    for k1 in range(kr_ref.shape[0]):
        c = c_ref[k1 * rows:(k1 + 1) * rows, :]
        s = s_ref[k1 * rows:(k1 + 1) * rows, :]
        kr_ref[k1] = (_dot(c, hs_hi) + _dot(c, hs_lo)) * scale
        ki_ref[k1] = (_dot(s, hd_hi) + _dot(s, hd_lo)) * (-scale)


def _filter_spectrum(hsum, hdiff):
    L, n = hsum.shape
    ct = HY_CT
    n_fft = 2 * L
    n2 = L // FFT_Z1
    d = jnp.arange(L, dtype=jnp.int32)[None, :]
    a1 = ((jnp.arange(FFT_K1, dtype=jnp.int32)[:, None] * d) % n_fft).astype(f32) * (2.0 * math.pi / n_fft)
    a2 = ((jnp.arange(n2, dtype=jnp.int32)[:, None] * d) % n2).astype(f32) * (2.0 * math.pi / n2)
    c1, s1 = jnp.cos(a1)[:, None, :], jnp.sin(a1)[:, None, :]
    c2, s2 = jnp.cos(a2)[None, :, :], jnp.sin(a2)[None, :, :]
    cmat = (c1 * c2 - s1 * s2).reshape(FFT_K1 * n2, L).astype(bf16)
    smat = (s1 * c2 + c1 * s2).reshape(FFT_K1 * n2, L).astype(bf16)
    once = pl.Buffered(1)
    return pl.pallas_call(
        _filter_spectrum_kernel,
        grid=(n // ct,),
        in_specs=[pl.BlockSpec((FFT_K1 * n2, L), lambda j: (0, 0), pipeline_mode=once),
                  pl.BlockSpec((FFT_K1 * n2, L), lambda j: (0, 0), pipeline_mode=once),
                  pl.BlockSpec((L, ct), lambda j: (0, j)),
                  pl.BlockSpec((L, ct), lambda j: (0, j))],
        out_specs=[pl.BlockSpec((FFT_K1, n2, ct), lambda j: (0, 0, j)),
                   pl.BlockSpec((FFT_K1, n2, ct), lambda j: (0, 0, j))],
        out_shape=[jax.ShapeDtypeStruct((FFT_K1, n2, n), f32),
                   jax.ShapeDtypeStruct((FFT_K1, n2, n), f32)],
        compiler_params=_cparams("parallel"),
        name="filter_spectrum",
    )(cmat, smat, hsum, hdiff)


FFT_N1 = 16
FFT_K1 = FFT_N1 // 2 + 1
FFT_Z1 = FFT_N1 // 2
FFT_ROWS = 32


def _lincomb(terms):
    groups = {}
    for c, a in terms:
        if abs(c) > 1e-9:
            groups.setdefault(round(abs(c), 9), []).append((c > 0, a))
    total = None
    for mag, items in groups.items():
        pos = [a for p, a in items if p]
        neg = [a for p, a in items if not p]
        s = None
        for a in pos:
            s = a if s is None else s + a
        for a in neg:
            s = -a if s is None else s - a
        if mag != 1.0:
            s = s * mag
        total = s if total is None else total + s
    return total


def _hyena_fft_kernel(m2f_ref, m2i_ref, twr_ref, twi_ref, x1_ref, x2_ref, v_ref, w1_ref, w2_ref, wv_ref,
                      kr0_ref, ki0_ref, kr1_ref, ki1_ref, db_ref, o_ref, z_s, g_s, d_s, y_s, b_s):
    L, ct = z_s.shape
    n2 = L // FFT_Z1
    R = min(HY_ROWS, L)
    nt = L // R
    r1 = min(FFT_ROWS, n2)
    halo = 16
    rowi = lax.broadcasted_iota(jnp.int32, (R, 1), 0)
    ang = 2.0 * math.pi / FFT_N1
    c16 = [[math.cos(ang * a * b) for b in range(FFT_K1)] for a in range(FFT_Z1)]
    s16 = [[math.sin(ang * a * b) for b in range(FFT_K1)] for a in range(FFT_Z1)]

    def conv_into(dst, x_ref, w_ref):
        def body(t, carry):
            r0 = pl.multiple_of(t * R, R)
            x = x_ref[0, pl.ds(r0, R), :].astype(f32)
            rp = pl.multiple_of(jnp.maximum(r0 - halo, 0), halo)
            rn = pl.multiple_of(jnp.minimum(r0 + R, L - halo), halo)
            prow = jnp.where(t > 0, x_ref[0, pl.ds(rp, halo), :][halo - 1:halo, :].astype(f32), 0.0)
            nrow = jnp.where(t < nt - 1, x_ref[0, pl.ds(rn, halo), :][0:1, :].astype(f32), 0.0)
            prev = jnp.where(rowi == 0, prow, pltpu.roll(x, 1, 0))
            nxt = jnp.where(rowi == R - 1, nrow, pltpu.roll(x, R - 1, 0))
            w = w_ref[...]
            dst[pl.ds(r0, R), :] = prev * w[0:1, :] + x * w[1:2, :] + nxt * w[2:3, :]
            return carry
        lax.fori_loop(0, nt, body, 0)

    def cols(k1):
        return slice(k1 * ct, (k1 + 1) * ct)

    conv_into(z_s, v_ref, wv_ref)
    stages = ((x1_ref, w1_ref, kr0_ref, ki0_ref), (x2_ref, w2_ref, kr1_ref, ki1_ref))
    for o, (g_ref, gw_ref, kr_ref, ki_ref) in enumerate(stages):
        conv_into(g_s, g_ref, gw_ref)

        def forward1(rt, carry):
            r0 = pl.multiple_of(rt * r1, r1)
            zs = [z_s[pl.ds(a * n2 + r0, r1), :] for a in range(FFT_Z1)]
            for k1 in range(FFT_K1):
                ar = _lincomb([(c16[a][k1], zs[a]) for a in range(FFT_Z1)])
                ai = _lincomb([(-s16[a][k1], zs[a]) for a in range(FFT_Z1)])
                twr = twr_ref[pl.ds(r0, r1), cols(k1)]
                twi = twi_ref[pl.ds(r0, r1), cols(k1)]
                if ai is None:
                    pr, pi = ar * twr, -(ar * twi)
                else:
                    pr, pi = ar * twr + ai * twi, ai * twr - ar * twi
                d_s[pl.ds(r0, r1), cols(k1)] = pr.astype(bf16)
                d_s[pl.ds(n2 + r0, r1), cols(k1)] = pi.astype(bf16)
            return carry

        lax.fori_loop(0, n2 // r1, forward1, 0)

        for k1 in range(FFT_K1):
            xk = _dot(m2f_ref[...], d_s[:, cols(k1)])
            xr, xi = xk[0:n2, :], xk[n2:, :]
            kr, ki = kr_ref[k1], ki_ref[k1]
            y_s[0:n2, cols(k1)] = (xr * kr - xi * ki).astype(bf16)
            y_s[n2:2 * n2, cols(k1)] = (xr * ki + xi * kr).astype(bf16)

        for k1 in range(FFT_K1):
            bk = _dot(m2i_ref[...], y_s[:, cols(k1)])
            br, bi = bk[0:n2, :], bk[n2:, :]
            twr, twi = twr_ref[:, cols(k1)], twi_ref[:, cols(k1)]
            b_s[0:n2, cols(k1)] = br * twr - bi * twi
            b_s[n2:2 * n2, cols(k1)] = br * twi + bi * twr

        def inverse1(rt, carry):
            r0 = pl.multiple_of(rt * r1, r1)
            brs = [b_s[pl.ds(r0, r1), cols(j)] for j in range(FFT_K1)]
            bis = [b_s[pl.ds(n2 + r0, r1), cols(j)] for j in range(FFT_K1)]
            for a in range(FFT_Z1):
                terms = [(1.0, brs[0]), (-1.0 if a % 2 else 1.0, brs[FFT_K1 - 1])]
                for j in range(1, FFT_K1 - 1):
                    terms += [(2.0 * c16[a][j], brs[j]), (-2.0 * s16[a][j], bis[j])]
                rows = pl.ds(a * n2 + r0, r1)
                z = g_s[rows, :] * (_lincomb(terms) + z_s[rows, :] * db_ref[o:o + 1, :])
                if o + 1 < len(stages):
                    z_s[rows, :] = z
                else:
                    o_ref[0, rows, :] = z.astype(o_ref.dtype)
            return carry

        lax.fori_loop(0, n2 // r1, inverse1, 0)


def _hyena_tables(L):
    n_fft = 2 * L
    n2 = L // FFT_Z1
    ct = HY_CT
    idx = jnp.arange(n2, dtype=jnp.int32)
    a2 = ((idx[:, None] * idx[None, :]) % n2).astype(f32) * (2.0 * math.pi / n2)
    c2, s2 = jnp.cos(a2), jnp.sin(a2)
    m2f = jnp.block([[c2, s2], [-s2, c2]]).astype(bf16)
    m2i = jnp.block([[c2, -s2], [s2, c2]]).astype(bf16)
    at = (idx[:, None] * jnp.arange(FFT_K1, dtype=jnp.int32)[None, :]).astype(f32) * (2.0 * math.pi / n_fft)
    twr = jnp.broadcast_to(jnp.cos(at)[:, :, None], (n2, FFT_K1, ct)).reshape(n2, FFT_K1 * ct)
    twi = jnp.broadcast_to(jnp.sin(at)[:, :, None], (n2, FFT_K1, ct)).reshape(n2, FFT_K1 * ct)
    return m2f, m2i, twr, twi


def _hyena_fft(hx, conv_w, tables, kr, ki, dbias):
    B, L, _ = hx.shape
    ct = HY_CT
    nct = B_WIDTH // ct
    n2 = L // FFT_Z1
    wide = FFT_K1 * ct
    once = pl.Buffered(1)
    m2f, m2i, twr, twi = tables

    def xspec(part):
        return pl.BlockSpec((1, L, ct), lambda c, b: (b, 0, part * nct + c))

    def wspec(part):
        return pl.BlockSpec((8, ct), lambda c, b: (0, part * nct + c))

    def kspec(order):
        return pl.BlockSpec((FFT_K1, n2, ct), lambda c, b: (0, 0, order * nct + c), pipeline_mode=once)

    def const(shape):
        return pl.BlockSpec(shape, lambda c, b: (0, 0), pipeline_mode=once)

    return pl.pallas_call(
        _hyena_fft_kernel,
        grid=(nct, B),
        in_specs=[const((2 * n2, 2 * n2)), const((2 * n2, 2 * n2)), const((n2, wide)), const((n2, wide)),
                  xspec(0), xspec(1), xspec(2), wspec(0), wspec(1), wspec(2),
                  kspec(0), kspec(0), kspec(1), kspec(1),
                  pl.BlockSpec((8, ct), lambda c, b: (0, c))],
        out_specs=pl.BlockSpec((1, L, ct), lambda c, b: (b, 0, c)),
        out_shape=jax.ShapeDtypeStruct((B, L, B_WIDTH), bf16),
        scratch_shapes=[pltpu.VMEM((L, ct), f32), pltpu.VMEM((L, ct), f32),
                        pltpu.VMEM((2 * n2, wide), bf16), pltpu.VMEM((2 * n2, wide), bf16),
                        pltpu.VMEM((2 * n2, wide), f32)],
        compiler_params=_cparams("parallel", "parallel"),
        name="hyena_fft",
    )(m2f, m2i, twr, twi, hx, hx, hx, conv_w, conv_w, conv_w, kr, ki, kr, ki, dbias)


RT_E1, RT_E2, RT_W1, RT_W2, RT_R1, RT_R2 = 0, 1, 2, 3, 4, 5
ROUTE_TM = ROW_TM
PROJ_PARTS = 2
EXPERT_TM = 512
SC_WINDOW = 128


def _pack_bf16_pairs(x):
    n = x.shape[1] // 2
    bits = lax.bitcast_convert_type(x.astype(bf16).astype(f32), jnp.uint32)
    return bits[:, :n] | (bits[:, n:] >> 16)


def _unpack_bf16_pairs(p):
    hi = lax.bitcast_convert_type(p & jnp.uint32(0xFFFF0000), f32)
    lo = lax.bitcast_convert_type(p << 16, f32)
    return hi, lo


ROUTER_ROWS = 40


def _route_topk(lt):
    row = lax.broadcasted_iota(jnp.int32, lt.shape, 0)
    neg = -jnp.inf
    big = ROUTER_LANES
    gl = jnp.where(row < N_GROUPS, lt, neg)
    gmax = jnp.max(gl, axis=0, keepdims=True)
    gsel = jnp.min(jnp.where(gl == gmax, row, big), axis=0, keepdims=True)
    psel = 1.0 / jnp.sum(jnp.exp(gl - gmax), axis=0, keepdims=True)
    first = N_GROUPS + gsel * EXPERTS_PER_GROUP
    el = jnp.where((row >= first) & (row < first + EXPERTS_PER_GROUP), lt, neg)
    t1 = jnp.max(el, axis=0, keepdims=True)
    i1 = jnp.min(jnp.where(el == t1, row, big), axis=0, keepdims=True)
    el2 = jnp.where(row == i1, neg, el)
    t2 = jnp.max(el2, axis=0, keepdims=True)
    i2 = jnp.min(jnp.where(el2 == t2, row, big), axis=0, keepdims=True)
    ex = jnp.exp(t2 - t1)
    w1 = psel / (1.0 + ex)
    w2 = ex * w1
    return i1 - N_GROUPS, i2 - N_GROUPS, w1, w2


def _proj_route_kernel(x_ref, *refs, n_in):
    a_refs = refs[0:n_in]
    w_refs = refs[n_in:2 * n_in]
    g_ref, wr_ref, up_ref, xo_ref, hpa_ref, hpb_ref, rt_ref, rtt_ref, cnt_ref = refs[2 * n_in:]
    tm = xo_ref.shape[0]
    rp = tm // PROJ_PARTS
    parts = [slice(p * rp, (p + 1) * rp) for p in range(PROJ_PARTS)]
    xs = []
    for r in parts:
        x = x_ref[r, :]
        for a_ref, w_ref in zip(a_refs, w_refs):
            x = x + _dot(a_ref[r, :], w_ref[...])
        xo_ref[r, :] = x
        xs.append(x)
    hs = [_rms(x, g_ref[...]) for x in xs]
    for r, h in zip(parts, hs):
        hp = _pack_bf16_pairs(h)
        quarter = hp.shape[1] // 2
        hpa_ref[r, :] = hp[:, :quarter]
        hpb_ref[r, :] = hp[:, quarter:]
    tops = []
    for h in hs:
        h_hi = h.astype(bf16)
        h_lo = (h - h_hi.astype(f32)).astype(bf16)
        both = _dot(h_hi, wr_ref[...])
        logits = both[:, :ROUTER_LANES] + both[:, ROUTER_LANES:] + _dot(h_lo, wr_ref[:, :ROUTER_LANES])
        tops.append(_route_topk(logits.T[0:ROUTER_ROWS, :]))
    row = lax.broadcasted_iota(jnp.int32, (ROUTER_LANES, rp), 0)
    onehots = [jnp.where((row == e1) | (row == e2), 1.0, 0.0) for e1, e2, _, _ in tops]
    onehot16 = jnp.concatenate(onehots, axis=1).astype(bf16)
    before = _dot(onehot16, up_ref[...])
    for p, (r, (e1, e2, w1, w2)) in enumerate(zip(parts, tops)):
        bp = before[:, r]
        r1 = jnp.sum(jnp.where(row == e1, bp, 0.0), axis=0, keepdims=True)
        r2 = jnp.sum(jnp.where(row == e2, bp, 0.0), axis=0, keepdims=True)
        rec = jnp.zeros((ROUTER_LANES, rp), f32)
        for ln, val in ((RT_E1, e1.astype(f32)), (RT_E2, e2.astype(f32)), (RT_W1, w1), (RT_W2, w2),
                        (RT_R1, r1), (RT_R2, r2)):
            rec = jnp.where(row == ln, val, rec)
        rtt_ref[:, r] = rec[0:rtt_ref.shape[0], :]
        rt_ref[r, :] = rec.T
    ones = jnp.ones((cnt_ref.shape[1], tm), bf16)
    cnt_ref[0] = _dot_nt(ones, onehot16)


def _proj_route(x, acts, ws, gain, w_router):
    T, D = x.shape
    tm = ROUTE_TM
    n_in = len(acts)
    up = jnp.asarray(np.triu(np.ones((tm, tm), np.float32), 1), dtype=bf16)
    in_specs = [pl.BlockSpec((tm, D), lambda i: (i, 0))]
    in_specs += [pl.BlockSpec((tm, a.shape[1]), lambda i: (i, 0)) for a in acts]
    in_specs += [pl.BlockSpec(w.shape, lambda i: (0, 0)) for w in ws]
    in_specs += [pl.BlockSpec((1, D), lambda i: (0, 0)),
                 pl.BlockSpec((D, 2 * ROUTER_LANES), lambda i: (0, 0)),
                 pl.BlockSpec((tm, tm), lambda i: (0, 0))]
    return pl.pallas_call(
        functools.partial(_proj_route_kernel, n_in=n_in),
        grid=(T // tm,),
        in_specs=in_specs,
        out_specs=[pl.BlockSpec((tm, D), lambda i: (i, 0)),
                   pl.BlockSpec((tm, D // 4), lambda i: (i, 0)),
                   pl.BlockSpec((tm, D // 4), lambda i: (i, 0)),
                   pl.BlockSpec((tm, ROUTER_LANES), lambda i: (i, 0)),
                   pl.BlockSpec((8, tm), lambda i: (0, i)),
                   pl.BlockSpec((1, 8, ROUTER_LANES), lambda i: (i, 0, 0))],
        out_shape=[jax.ShapeDtypeStruct((T, D), f32),
                   jax.ShapeDtypeStruct((T, D // 4), jnp.uint32),
                   jax.ShapeDtypeStruct((T, D // 4), jnp.uint32),
                   jax.ShapeDtypeStruct((T, ROUTER_LANES), f32),
                   jax.ShapeDtypeStruct((8, T), f32),
                   jax.ShapeDtypeStruct((T // tm, 8, ROUTER_LANES), f32)],
        compiler_params=_cparams("parallel"),
        name="proj_route",
    )(x, *acts, *ws, gain.reshape(1, D), w_router, up)


def _moe_rows(T):
    return 2 * T + N_EXPERTS * EXPERT_TM


def _moe_plan(route_t, counts):
    T = route_t.shape[1]
    tm = ROUTE_TM
    cnt = counts[:, 0, :N_EXPERTS].astype(jnp.int32)
    total = jnp.sum(cnt, axis=0)
    padded = ((total + EXPERT_TM - 1) // EXPERT_TM) * EXPERT_TM
    ends = jnp.cumsum(padded)
    tile_base = (ends - padded)[None, :] + jnp.cumsum(cnt, axis=0) - cnt
    base = jnp.broadcast_to(tile_base.T[:, :, None], (N_EXPERTS, T // tm, tm)).reshape(N_EXPERTS, T)
    ids = jnp.arange(N_EXPERTS, dtype=jnp.int32)[:, None]
    dest = []
    for e_row, r_row in ((RT_E1, RT_R1), (RT_E2, RT_R2)):
        e = route_t[e_row].astype(jnp.int32)
        dest.append(jnp.sum(jnp.where(ids == e[None, :], base, 0), axis=0) + route_t[r_row].astype(jnp.int32))
    n_tiles = _moe_rows(T) // EXPERT_TM
    n_used = ends[-1] // EXPERT_TM
    starts = jnp.arange(n_tiles, dtype=jnp.int32) * EXPERT_TM
    tile_expert = jnp.sum((starts[:, None] >= ends[None, :]).astype(jnp.int32), axis=1)
    last_expert = jnp.sum((jnp.maximum(n_used - 1, 0) * EXPERT_TM >= ends).astype(jnp.int32))
    tile_expert = jnp.minimum(tile_expert, last_expert)
    tiles = jnp.arange(n_tiles, dtype=jnp.int32)
    prev_expert = jnp.concatenate([jnp.full((1,), -1, jnp.int32), tile_expert[:-1]])
    seg_first = ((tile_expert != prev_expert) & (tiles < n_used)).astype(jnp.int32)
    seg_slot = (jnp.cumsum(seg_first) - 1) % 2
    seg_end = jnp.sum(jnp.where(ids == tile_expert[None, :], (ends // EXPERT_TM)[:, None], 0), axis=0)
    nxt = jnp.where(seg_end < n_used, tile_expert[jnp.minimum(seg_end, n_tiles - 1)], -1)
    plan = (tile_expert, seg_first, seg_slot.astype(jnp.int32), nxt.astype(jnp.int32),
            n_used.reshape(1).astype(jnp.int32))
    return dest, plan


def _sc_mesh():
    return plsc.VectorSubcoreMesh(core_axis_name="core", subcore_axis_name="subcore")


def _sc_scatter_pair(rows, d0, d1, n_out):
    T, W = rows.shape
    win = SC_WINDOW

    @pl.kernel(out_type=jax.ShapeDtypeStruct((n_out, W), rows.dtype), mesh=_sc_mesh(), name="moe_sc_scatter")
    def scatter(x_hbm, i0_hbm, i1_hbm, o_hbm):
        def body(x_vmem, i0_vmem, i1_vmem):
            pltpu.sync_copy(x_vmem, o_hbm.at[i0_vmem.at[0]])
            pltpu.sync_copy(x_vmem, o_hbm.at[i1_vmem.at[0]])

        pltpu.emit_pipeline(
            body, grid=(T // win,),
            in_specs=[pl.BlockSpec((win, W), index_map=lambda i: (i, 0)),
                      pl.BlockSpec((1, win), index_map=lambda i: (0, i)),
                      pl.BlockSpec((1, win), index_map=lambda i: (0, i))],
            out_specs=[],
            core_axis_name=("core", "subcore"), dimension_semantics=(pltpu.PARALLEL,),
        )(x_hbm, i0_hbm, i1_hbm)

    return scatter(rows, d0.reshape(1, T), d1.reshape(1, T))


def _sc_gather(table, idx):
    n = idx.shape[0]
    W = table.shape[1]
    win = SC_WINDOW

    @pl.kernel(out_type=jax.ShapeDtypeStruct((n, W), table.dtype), mesh=_sc_mesh(), name="moe_sc_gather")
    def gather(t_hbm, i_hbm, o_hbm):
        def body(i_vmem, o_vmem):
            pltpu.sync_copy(t_hbm.at[i_vmem.at[0]], o_vmem)

        pltpu.emit_pipeline(
            body, grid=(n // win,),
            in_specs=[pl.BlockSpec((1, win), index_map=lambda i: (0, i))],
            out_specs=[pl.BlockSpec((win, W), index_map=lambda i: (i, 0))],
            core_axis_name=("core", "subcore"), dimension_semantics=(pltpu.PARALLEL,),
        )(i_hbm, o_hbm)

    return gather(table, idx.reshape(1, n))


def _expert_kernel(te_ref, first_ref, slot_ref, nxt_ref, nu_ref, xa_ref, xb_ref, wg_hbm, wu_hbm, wd_hbm,
                   ya_ref, yb_ref, wg_s, wu_s, wd_s, wg16, wu16, wd16, sem):
    i = pl.program_id(0)
    active = i < nu_ref[0]

    def weight_copies(expert, s):
        return (pltpu.make_async_copy(wg_hbm.at[expert], wg_s.at[s], sem.at[s, 0]),
                pltpu.make_async_copy(wu_hbm.at[expert], wu_s.at[s], sem.at[s, 1]),
                pltpu.make_async_copy(wd_hbm.at[expert], wd_s.at[s], sem.at[s, 2]))

    @pl.when(i == 0)
    def _():
        for c in weight_copies(te_ref[0], slot_ref[0]):
            c.start()

    @pl.when(active & (first_ref[i] == 1))
    def _():
        s = slot_ref[i]
        for c in weight_copies(te_ref[i], s):
            c.wait()

        @pl.when(nxt_ref[i] >= 0)
        def _():
            for c in weight_copies(nxt_ref[i], 1 - s):
                c.start()

        wg16[...] = wg_s[s].astype(bf16)
        wu16[...] = wu_s[s].astype(bf16)
        wd16[...] = wd_s[s].astype(bf16)

    @pl.when(active)
    def _():
        hi, lo = _unpack_bf16_pairs(jnp.concatenate([xa_ref[...], xb_ref[...]], axis=1))
        x = jnp.concatenate([hi, lo], axis=1).astype(bf16)
        gate = _dot(x, wg16[...])
        up = _dot(x, wu16[...])
        hg = (gate * _sigmoid(gate) * up).astype(bf16)
        y = _pack_bf16_pairs(_dot(hg, wd16[...]))
        quarter = y.shape[1] // 2
        ya_ref[...] = y[:, :quarter]
        yb_ref[...] = y[:, quarter:]

    @pl.when(jnp.logical_not(active))
    def _():
        ya_ref[...] = jnp.zeros(ya_ref.shape, ya_ref.dtype)
        yb_ref[...] = jnp.zeros(yb_ref.shape, yb_ref.dtype)


def _experts(plan, layer, xa, xb, wg, wu, wd):
    tile_expert, seg_first, seg_slot, nxt, n_used = plan
    off = layer * N_EXPERTS
    n_rows, quarter = xa.shape
    D = 4 * quarter
    tm = EXPERT_TM
    row_spec = pl.BlockSpec((tm, quarter), lambda i, *_: (i, 0))
    grid_spec = pltpu.PrefetchScalarGridSpec(
        num_scalar_prefetch=5,
        grid=(n_rows // tm,),
        in_specs=[row_spec, row_spec,
                  pl.BlockSpec(memory_space=pl.ANY), pl.BlockSpec(memory_space=pl.ANY), pl.BlockSpec(memory_space=pl.ANY)],
        out_specs=[row_spec, row_spec],
        scratch_shapes=[pltpu.VMEM((2, D, D_EXPERT), wg.dtype), pltpu.VMEM((2, D, D_EXPERT), wu.dtype),
                        pltpu.VMEM((2, D_EXPERT, D), wd.dtype),
                        pltpu.VMEM((D, D_EXPERT), bf16), pltpu.VMEM((D, D_EXPERT), bf16), pltpu.VMEM((D_EXPERT, D), bf16),
                        pltpu.SemaphoreType.DMA((2, 3))])
    return pl.pallas_call(
        _expert_kernel,
        grid_spec=grid_spec,
        out_shape=[jax.ShapeDtypeStruct((n_rows, quarter), jnp.uint32)] * 2,
        compiler_params=_cparams("arbitrary"),
        name="moe_experts",
    )(tile_expert + off, seg_first, seg_slot, jnp.where(nxt >= 0, nxt + off, -1), n_used, xa, xb, wg, wu, wd)


def _combine_kernel(x_ref, rt_ref, g_ref, a1_ref, b1_ref, a2_ref, b2_ref, *rest, final):
    half = x_ref.shape[1] // 2
    h1, l1 = _unpack_bf16_pairs(jnp.concatenate([a1_ref[...], b1_ref[...]], axis=1))
    h2, l2 = _unpack_bf16_pairs(jnp.concatenate([a2_ref[...], b2_ref[...]], axis=1))
    w1 = rt_ref[:, RT_W1:RT_W1 + 1]
    w2 = rt_ref[:, RT_W2:RT_W2 + 1]
    out = jnp.concatenate([x_ref[:, 0:half] + w1 * h1 + w2 * h2, x_ref[:, half:] + w1 * l1 + w2 * l2], axis=1)
    if final:
        (o_ref,) = rest
        o_ref[...] = _rms(out, g_ref[...])
    else:
        w_ref, o_ref, p_ref = rest
        o_ref[...] = out
        p_ref[...] = _dot(_rms(out, g_ref[...]).astype(bf16), w_ref[...]).astype(p_ref.dtype)


def _combine(x, route, ga, gb, gain, w_next=None):
    T, D = x.shape
    tm = ROUTE_TM
    n = T // tm
    quarter = D // 4
    final = w_next is None
    slot1 = pl.BlockSpec((tm, quarter), lambda i: (i, 0))
    slot2 = pl.BlockSpec((tm, quarter), lambda i: (i + n, 0))
    row_spec = pl.BlockSpec((tm, D), lambda i: (i, 0))
    in_specs = [row_spec, pl.BlockSpec((tm, ROUTER_LANES), lambda i: (i, 0)), pl.BlockSpec((1, D), lambda i: (0, 0)),
                slot1, slot1, slot2, slot2]
    args = [x, route, gain.reshape(1, D), ga, gb, ga, gb]
    out_specs = [row_spec]
    out_shape = [jax.ShapeDtypeStruct((T, D), f32)]
    if not final:
        n_next = w_next.shape[1]
        in_specs.append(pl.BlockSpec((D, n_next), lambda i: (0, 0)))
        args.append(w_next)
        out_specs.append(pl.BlockSpec((tm, n_next), lambda i: (i, 0)))
        out_shape.append(jax.ShapeDtypeStruct((T, n_next), bf16))
    outs = pl.pallas_call(
        functools.partial(_combine_kernel, final=final),
        grid=(n,),
        in_specs=in_specs,
        out_specs=out_specs,
        out_shape=out_shape,
        compiler_params=_cparams("parallel"),
        name="moe_combine",
    )(*args)
    return outs[0] if final else outs


def _moe(layer, x, hpa, hpb, route, route_t, counts, wg, wu, wd, gain, w_next=None):
    (d0, d1), plan = _moe_plan(route_t, counts)
    n_rows = _moe_rows(x.shape[0])
    xa = _sc_scatter_pair(hpa, d0, d1, n_rows)
    xb = _sc_scatter_pair(hpb, d0, d1, n_rows)
    ya, yb = _experts(plan, layer, xa, xb, wg, wu, wd)
    dcat = jnp.concatenate([d0, d1])
    return _combine(x, route, _sc_gather(ya, dcat), _sc_gather(yb, dcat), gain, w_next)


SWA_LOCKSTEP = 2


def _swa_bias_tables():
    blk = C_BLOCK
    qi = np.arange(blk)[:, None]
    kj = np.arange(3 * blk)[None, :]
    dist = np.abs(blk + qi - kj)
    tabs = np.full((3, C_KV_HEADS, C_GROUP * blk, 3 * blk), -np.inf, np.float32)
    for case in range(3):
        ok = dist <= WINDOW
        if case == 0:
            ok = ok & (kj >= blk)
        if case == 2:
            ok = ok & (kj < 2 * blk)
        for hh in range(C_HEADS):
            slope = 2.0 ** (-8.0 * (hh + 1) / C_HEADS)
            t = np.where(ok, -slope * LOG2E * dist, -np.inf)
            kvh, g = divmod(hh, C_GROUP)
            tabs[case, kvh, g * blk:(g + 1) * blk] = t
    return jnp.asarray(tabs)


def _swa_kernel(sink_ref, q_ref, kp_ref, kc_ref, kn_ref, vp_ref, vc_ref, vn_ref, bias_ref, o_ref):
    blk = C_BLOCK
    hd = C_HEAD_DIM
    kcat = jnp.concatenate([kp_ref[0], kc_ref[0], kn_ref[0]], axis=0)
    vcat = jnp.concatenate([vp_ref[0], vc_ref[0], vn_ref[0]], axis=0)
    q = q_ref[0]
    rblk = lax.broadcasted_iota(jnp.int32, (C_GROUP * blk, 1), 0) // blk
    lane = lax.broadcasted_iota(jnp.int32, (3 * blk, LANE - hd), 1)
    ones_pad = jnp.where(lane == 0, 1.0, 0.0).astype(bf16)
    q4s, sinks = [], []
    for kvh in range(C_KV_HEADS):
        q4s.append(jnp.concatenate(
            [q[:, (kvh * C_GROUP + g) * hd:(kvh * C_GROUP + g + 1) * hd] for g in range(C_GROUP)], axis=0))
        sk = jnp.zeros((C_GROUP * blk, 1), f32)
        for g in range(C_GROUP):
            sk = jnp.where(rblk == g, sink_ref[kvh * C_GROUP + g] * LOG2E, sk)
        sinks.append(sk)
    outs = []
    for h0 in range(0, C_KV_HEADS, SWA_LOCKSTEP):
        hs = range(h0, h0 + SWA_LOCKSTEP)
        ss = [_dot_nt(q4s[h], kcat[:, h * hd:(h + 1) * hd]) + bias_ref[0, h] for h in hs]
        ms = [jnp.maximum(jnp.max(s, axis=-1, keepdims=True), sinks[h]) for s, h in zip(ss, hs)]
        ps = [jnp.exp2(s - m).astype(bf16) for s, m in zip(ss, ms)]
        pvs = [_dot(p, jnp.concatenate([vcat[:, h * hd:(h + 1) * hd], ones_pad], axis=1)) for p, h in zip(ps, hs)]
        for pv, m, h in zip(pvs, ms, hs):
            den = pv[:, hd:hd + 1] + jnp.exp2(sinks[h] - m)
            o4 = pv[:, 0:hd] / den
            outs += [o4[g * blk:(g + 1) * blk, :] for g in range(C_GROUP)]
    o_ref[0] = jnp.concatenate(outs, axis=1).astype(o_ref.dtype)


def _swa(qkv, sinks):
    B, L, _ = qkv.shape
    nb = L // C_BLOCK
    dq = C_HEADS * C_HEAD_DIM
    dkv = C_KV_HEADS * C_HEAD_DIM
    k_col = dq // dkv
    v_col = k_col + 1

    def kv_spec(col, shift):
        return pl.BlockSpec((1, C_BLOCK, dkv),
                            lambda n, b: (b, jnp.clip(n + shift, 0, nb - 1), col))

    def edge_case(n, b):
        return (jnp.where(n == 0, 0, jnp.where(n == nb - 1, 2, 1)), 0, 0, 0)

    return pl.pallas_call(
        _swa_kernel,
        grid=(nb, B),
        in_specs=[pl.BlockSpec(memory_space=pltpu.SMEM),
                  pl.BlockSpec((1, C_BLOCK, dq), lambda n, b: (b, n, 0)),
                  kv_spec(k_col, -1), kv_spec(k_col, 0), kv_spec(k_col, 1),
                  kv_spec(v_col, -1), kv_spec(v_col, 0), kv_spec(v_col, 1),
                  pl.BlockSpec((1, C_KV_HEADS, C_GROUP * C_BLOCK, 3 * C_BLOCK), edge_case)],
        out_specs=pl.BlockSpec((1, C_BLOCK, dq), lambda n, b: (b, n, 0)),
        out_shape=jax.ShapeDtypeStruct((B, L, dq), bf16),
        compiler_params=_cparams("parallel", "parallel"),
        name="swa",
    )(sinks, qkv, qkv, qkv, qkv, qkv, qkv, qkv, _swa_bias_tables())


def _hyena_filters(L, w1, b1, w2, b2, w3, b3, w4, freq):
    hp = lax.Precision.HIGHEST
    t = jnp.linspace(0.0, 1.0, L, dtype=f32)[:, None]
    wpos = (2.0 * math.pi / L) * jnp.arange(L, dtype=f32)[:, None]
    bands = jnp.linspace(1e-4, HY_BANDS - 1, HY_BANDS, dtype=f32)[None, :]
    feats = jnp.concatenate([t, jnp.cos(bands * wpos), -jnp.sin(bands * wpos)], axis=-1)
    h = jnp.sin(freq[0] * (jnp.dot(feats, w1, precision=hp) + b1))
    h = jnp.sin(freq[1] * (jnp.dot(h, w2, precision=hp) + b2))
    h = jnp.sin(freq[2] * (jnp.dot(h, w3, precision=hp) + b3))
    h = jnp.dot(h, w4, precision=hp).reshape(L, HY_ORDER, 2, B_WIDTH)
    deltas = jnp.abs(jnp.linspace(HY_MIN_DECAY, HY_MAX_DECAY, B_WIDTH, dtype=f32))
    window = jnp.exp(-t * deltas[None, :])
    return h * window[:, None, None, :]


def _pad_rows(w, rows=8):
    return jnp.pad(w, ((0, rows - w.shape[0]), (0, 0)))


def _even_in_weights(w_in):
    hk = A_HEADS * A_DK
    q = w_in[:, 0:hk].reshape(D_MODEL, A_HEADS, A_DK)
    k = w_in[:, hk:2 * hk].reshape(D_MODEL, A_HEADS, A_DK)
    v = w_in[:, 2 * hk:A_QKV].reshape(D_MODEL, A_HEADS, A_DV)
    z = w_in[:, A_QKV:A_QKV + A_WIDTH].reshape(D_MODEL, A_HEADS, A_DV)
    qkvz = jnp.concatenate([q, k, v, z], axis=-1).reshape(D_MODEL, A_HEADS * 4 * A_DK)
    ab = w_in[:, A_QKV + A_WIDTH:A_QKV + A_WIDTH + 4 * A_HEADS].reshape(D_MODEL, 4, A_HEADS)
    ab = jnp.transpose(ab, (0, 2, 1))
    ab = jnp.pad(ab.reshape(D_MODEL, 4 * A_HEADS), ((0, 0), (0, LANE - 4 * A_HEADS)))
    hx = w_in[:, A_QKV + A_WIDTH + 4 * A_HEADS:]
    return jnp.concatenate([qkvz, ab, hx], axis=1).astype(bf16)


def _router_weights(w_group, w_expert):
    w = jnp.concatenate([w_group, w_expert], axis=1)
    w = jnp.pad(w, ((0, 0), (0, ROUTER_LANES - w.shape[1])))
    hi = w.astype(bf16)
    lo = (w - hi.astype(f32)).astype(bf16)
    return jnp.concatenate([hi, lo], axis=1)


def kernel(x_prompt, x_sample, norm_mix, norm_ffn, norm_final, ev_w_in, ev_conv_a, ev_alog_f, ev_alog_b, ev_dtb_f, ev_dtb_b, ev_onorm, ev_conv_b, hy_w1, hy_b1, hy_w2, hy_b2, hy_w3, hy_b3, hy_w4, hy_freq, hy_dbias, ev_w_out, od_w_qkv, od_sinks, od_w_out, moe_w_group, moe_w_expert, moe_w_gate, moe_w_up, moe_w_down):
    L = x_prompt.shape[1]
    n_layers = moe_w_gate.shape[0]
    wg = moe_w_gate.reshape(n_layers * N_EXPERTS, D_MODEL, D_EXPERT)
    wu = moe_w_up.reshape(n_layers * N_EXPERTS, D_MODEL, D_EXPERT)
    wd = moe_w_down.reshape(n_layers * N_EXPERTS, D_EXPERT, D_MODEL)

    w_in = _even_in_weights(ev_w_in[0])
    n_qkvz = A_HEADS * 4 * A_DK
    n_ab = A_HEADS * LANE
    n_hx = 3 * B_WIDTH
    hk = A_HEADS * A_DK
    ca = ev_conv_a[0]
    conv_a = jnp.concatenate([ca[0:hk].reshape(A_HEADS, A_DK, 3), ca[hk:2 * hk].reshape(A_HEADS, A_DK, 3),
                              ca[2 * hk:].reshape(A_HEADS, A_DV, 3)], axis=1)
    conv_a = jnp.pad(jnp.transpose(conv_a, (0, 2, 1)), ((0, 0), (0, 5), (0, 0)))
    gate_sc = jnp.stack([ev_alog_f[0], ev_alog_b[0], ev_dtb_f[0], ev_dtb_b[0]], axis=1)
    filt = _hyena_filters(L, hy_w1[0], hy_b1[0], hy_w2[0], hy_b2[0], hy_w3[0], hy_b3[0], hy_w4[0], hy_freq[0])
    hsum = (filt[:, :, 0] + filt[:, :, 1]).reshape(L, HY_ORDER * B_WIDTH)
    hdiff = (filt[:, :, 0] - filt[:, :, 1]).reshape(L, HY_ORDER * B_WIDTH)
    kr, ki = _filter_spectrum(hsum, hdiff)
    fft_tables = _hyena_tables(L)
    conv_b = _pad_rows(ev_conv_b[0].T)
    dbias = _pad_rows(hy_dbias[0])
    w_out = ev_w_out[0].astype(bf16)
    routers = [_router_weights(moe_w_group[l], moe_w_expert[l]) for l in range(n_layers)]
    n_qkv = od_w_qkv.shape[-1]
    dq = C_HEADS * C_HEAD_DIM
    w_qkv = jnp.concatenate([od_w_qkv[0][:, :dq] * (LOG2E * C_HEAD_DIM ** -0.5), od_w_qkv[0][:, dq:]], axis=1)
    w_qkv = w_qkv.astype(bf16)
    w_att_out = od_w_out[0].astype(bf16)

    def trunk(xb):
        B = xb.shape[0]
        T = B * L
        x = xb.reshape(T, D_MODEL)
        qkvz, abh, hx = _even_in_proj(x, norm_mix[0], w_in)
        o_a = _deltanet(qkvz.reshape(B, L, n_qkvz), abh.reshape(B, L, n_ab), conv_a, gate_sc, ev_onorm[0])
        o_b = _hyena_fft(hx.reshape(B, L, n_hx), conv_b, fft_tables, kr, ki, dbias)
        x, hpa, hpb, route, route_t, counts = _proj_route(
            x, (o_a.reshape(T, A_WIDTH), o_b.reshape(T, B_WIDTH)), (w_out[:A_WIDTH], w_out[A_WIDTH:]),
            norm_ffn[0], routers[0])
        x, qkv = _moe(0, x, hpa, hpb, route, route_t, counts, wg, wu, wd, norm_mix[1], w_qkv)
        o_c = _swa(qkv.reshape(B, L, n_qkv), od_sinks[0])
        x, hpa, hpb, route, route_t, counts = _proj_route(
            x, (o_c.reshape(T, C_HEADS * C_HEAD_DIM),), (w_att_out,), norm_ffn[1], routers[1])
        y = _moe(1, x, hpa, hpb, route, route_t, counts, wg, wu, wd, norm_final)
        return y.reshape(B, L, D_MODEL)

    return (trunk(x_prompt), trunk(x_sample))
```

```python
import functools
import math

import jax
import jax.numpy as jnp
import numpy as np
from jax import lax
from jax.experimental import pallas as pl
from jax.experimental.pallas import tpu as pltpu
from jax.experimental.pallas import tpu_sc as plsc

f32 = jnp.float32
bf16 = jnp.bfloat16

EPS = 1e-6
D_MODEL = 1024

A_HEADS = 4
A_DK = 128
A_DV = 128
A_WIDTH = A_HEADS * A_DV
A_QKV = 2 * A_HEADS * A_DK + A_HEADS * A_DV
DELTA_CHUNK = 128
INV_BASE = 8
DELTA_GROUP = 8
DELTA_WAVES = 2

B_WIDTH = D_MODEL - A_WIDTH
HY_ORDER = 2
HY_EMB = 33
HY_BANDS = (HY_EMB - 1) // 2
HY_TARGET = 1e-2
HY_MIN_DECAY = math.log(HY_TARGET) / 1.5
HY_MAX_DECAY = math.log(HY_TARGET) / 0.3
HY_CT = 256
HY_ROWS = 256

C_HEADS = 16
C_KV_HEADS = 4
C_HEAD_DIM = 64
C_GROUP = C_HEADS // C_KV_HEADS
WINDOW = 128
C_BLOCK = 128

N_GROUPS = 4
EXPERTS_PER_GROUP = 8
N_EXPERTS = N_GROUPS * EXPERTS_PER_GROUP
D_EXPERT = 256
ROUTER_LANES = 128

LANE = 128
ROW_TM = 512
VMEM_LIMIT = 56 * 1024 * 1024
LOG2E = 1.4426950408889634


def _cparams(*sem):
    return pltpu.CompilerParams(dimension_semantics=sem, vmem_limit_bytes=VMEM_LIMIT)


def _dot(a, b):
    return jnp.dot(a, b, preferred_element_type=f32)


def _dot_nt(a, b):
    return lax.dot_general(a, b, (((1,), (1,)), ((), ())), preferred_element_type=f32)


def _dot_tn(a, b):
    return lax.dot_general(a, b, (((0,), (0,)), ((), ())), preferred_element_type=f32)


def _sigmoid(x):
    return 1.0 / (1.0 + jnp.exp(-x))


def _rms(x, g):
    return x * lax.rsqrt(jnp.mean(x * x, axis=-1, keepdims=True) + EPS) * g


def _even_in_kernel(x_ref, g_ref, w_ref, qkvz_ref, ab_ref, hx_ref):
    h = _rms(x_ref[...], g_ref[...]).astype(bf16)
    n_qkvz = qkvz_ref.shape[1]
    qkvz_ref[...] = _dot(h, w_ref[:, 0:n_qkvz]).astype(qkvz_ref.dtype)
    ab = _dot(h, w_ref[:, n_qkvz:n_qkvz + LANE])
    for hd in range(A_HEADS):
        ab_ref[:, hd * LANE:(hd + 1) * LANE] = ab if hd == 0 else pltpu.roll(ab, LANE - 4 * hd, 1)
    hx_ref[...] = _dot(h, w_ref[:, n_qkvz + LANE:]).astype(hx_ref.dtype)


def _even_in_proj(x, gain, w):
    T, D = x.shape
    tm = ROW_TM
    n_qkvz = A_HEADS * 4 * A_DK
    n_ab = A_HEADS * LANE
    n_hx = 3 * B_WIDTH
    return pl.pallas_call(
        _even_in_kernel,
        grid=(T // tm,),
        in_specs=[pl.BlockSpec((tm, D), lambda i: (i, 0)),
                  pl.BlockSpec((1, D), lambda i: (0, 0)),
                  pl.BlockSpec(w.shape, lambda i: (0, 0))],
        out_specs=[pl.BlockSpec((tm, n_qkvz), lambda i: (i, 0)),
                   pl.BlockSpec((tm, n_ab), lambda i: (i, 0)),
                   pl.BlockSpec((tm, n_hx), lambda i: (i, 0))],
        out_shape=[jax.ShapeDtypeStruct((T, n_qkvz), bf16),
                   jax.ShapeDtypeStruct((T, n_ab), f32),
                   jax.ShapeDtypeStruct((T, n_hx), bf16)],
        compiler_params=_cparams("parallel"),
        name="even_in_proj",
    )(x, gain.reshape(1, D), w)


MK_LOWER, MK_UPPER, MK_BASE, MK_LEVEL0 = 0, 1, 2, 3
N_LEVELS = int(math.log2(DELTA_CHUNK // INV_BASE))
MK_EYE = MK_LEVEL0 + N_LEVELS
N_MASKS = MK_EYE + 1


def _delta_masks():
    c = np.arange(DELTA_CHUNK)[:, None]
    s = np.arange(DELTA_CHUNK)[None, :]
    m = [s <= c, s >= c, (c // INV_BASE == s // INV_BASE) & (c != s)]
    b = INV_BASE
    while b < DELTA_CHUNK:
        m.append((c // (2 * b) == s // (2 * b)) & (c // b != s // b))
        b *= 2
    m.append(c == s)
    m = np.stack(m).astype(np.float32)
    return jnp.asarray(m), jnp.asarray(m, dtype=bf16)


def _tri_inverse(mats, mk_ref, mk16_ref):
    ps = [a * mk_ref[MK_BASE] for a in mats]
    ts = [mk_ref[MK_EYE] - p for p in ps]
    n = 2
    while n < INV_BASE:
        p16s = [p.astype(bf16) for p in ps]
        ps = [_dot(p16, p16) for p16 in p16s]
        ts = [t + _dot(t.astype(bf16), p.astype(bf16)) for t, p in zip(ts, ps)]
        n *= 2
    a16s = [a.astype(bf16) for a in mats]
    t16s = [t.astype(bf16) for t in ts]
    for lvl in range(N_LEVELS):
        xs = [_dot(a16 * mk16_ref[MK_LEVEL0 + lvl], t16) for a16, t16 in zip(a16s, t16s)]
        t16s = [t16 - _dot(t16, x.astype(bf16)).astype(bf16) for t16, x in zip(t16s, xs)]
    return t16s


def _deltanet_kernel(sc_ref, x_ref, ab_ref, cw_ref, on_ref, mk_ref, mk16_ref, o_ref, mp_s, nr_s, gl_s, o_s):
    hd = pl.program_id(1)
    L = x_ref.shape[1]
    C = DELTA_CHUNK
    nchunk = L // C
    group = math.gcd(DELTA_GROUP, nchunk)
    nqkv = 3 * A_DK
    halo = 16

    lane = lax.broadcasted_iota(jnp.int32, (1, LANE), 1)
    alog = jnp.where(lane == 2, sc_ref[hd, 0], jnp.where(lane == 3, sc_ref[hd, 1], 0.0))
    dtb = jnp.where(lane == 2, sc_ref[hd, 2], jnp.where(lane == 3, sc_ref[hd, 3], 0.0))
    neg_a = jnp.where((lane == 2) | (lane == 3), -jnp.exp(alog), 0.0)
    rowi = lax.broadcasted_iota(jnp.int32, (C, 1), 0)
    cw = cw_ref[0]

    def chunk_inputs(i):
        r0 = pl.multiple_of(i * C, C)
        xc = x_ref[0, pl.ds(r0, C), 0:nqkv].astype(f32)
        rp = pl.multiple_of(jnp.maximum(r0 - halo, 0), halo)
        rn = pl.multiple_of(jnp.minimum(r0 + C, L - halo), halo)
        prow = x_ref[0, pl.ds(rp, halo), 0:nqkv][halo - 1:halo, :].astype(f32)
        nrow = x_ref[0, pl.ds(rn, halo), 0:nqkv][0:1, :].astype(f32)
        prow = jnp.where(i > 0, prow, 0.0)
        nrow = jnp.where(i < nchunk - 1, nrow, 0.0)
        prev = jnp.where(rowi == 0, prow, pltpu.roll(xc, 1, 0))
        nxt = jnp.where(rowi == C - 1, nrow, pltpu.roll(xc, C - 1, 0))
        y = prev * cw[0:1, :] + xc * cw[1:2, :] + nxt * cw[2:3, :]
        y = y * _sigmoid(y)
        q = y[:, 0:A_DK]
        k = y[:, A_DK:2 * A_DK]
        vc = y[:, 2 * A_DK:nqkv]
        qc = q * lax.rsqrt(jnp.sum(q * q, axis=-1, keepdims=True) + EPS) * (A_DK ** -0.5)
        kc = k * lax.rsqrt(jnp.sum(k * k, axis=-1, keepdims=True) + EPS)

        abc = ab_ref[0, pl.ds(r0, C), :]
        zg = abc + dtb
        g = neg_a * (jnp.maximum(zg, 0.0) + jnp.log(1.0 + jnp.exp(-jnp.abs(zg))))
        low16 = mk16_ref[MK_LOWER]
        g1 = g.astype(bf16)
        e1 = g - g1.astype(f32)
        g2 = e1.astype(bf16)
        g3 = (e1 - g2.astype(f32)).astype(bf16)
        pre = _dot(low16, g1) + _dot(low16, g2) + _dot(low16, g3)
        tot = pre[C - 1:C, :]
        suf = tot - pre + g
        beta_all = _sigmoid(abc)

        k16 = kc.astype(bf16)
        kk = _dot_nt(k16, k16)
        qk = _dot_nt(qc.astype(bf16), k16)

        chains = []
        for rev in (0, 1):
            beta = beta_all[:, rev:rev + 1]
            gcc = (suf if rev else pre)[:, 2 + rev:3 + rev]
            tot11 = tot[:, 2 + rev:3 + rev]
            keep = mk_ref[MK_UPPER if rev else MK_LOWER]
            bc = jnp.broadcast_to(gcc * LOG2E, (C, C))
            decay = jnp.exp2((bc - bc.T) * keep) * keep
            a = (beta * kk) * decay
            eg = jnp.exp(gcc)
            chains.append(dict(
                i=i, rev=rev, a=a, gl=jnp.exp(tot11), qd=qc * eg,
                rhs=jnp.concatenate([vc * beta, kc * (beta * eg)], axis=1).astype(bf16),
                kd16=(kc * jnp.exp(tot11 - gcc)).astype(bf16),
                qk16=(qk * decay).astype(bf16)))
        return chains

    def prep_group(first_chunk):
        chains = []
        for gi in range(group):
            chains += chunk_inputs(first_chunk + gi)
        t16s = _tri_inverse([c["a"] for c in chains], mk_ref, mk16_ref)
        uws = [_dot(t16, c["rhs"]).astype(bf16) for t16, c in zip(t16s, chains)]
        kts = [_dot_tn(c["kd16"], uw) for c, uw in zip(chains, uws)]
        prs = [_dot(c["qk16"], uw) for c, uw in zip(chains, uws)]
        for c, kt, pr in zip(chains, kts, prs):
            rev, i = c["rev"], c["i"]
            mp_s[rev, i, 0:A_DK, :] = (-kt[:, A_DV:]).astype(bf16)
            mp_s[rev, i, A_DK:A_DK + C, :] = (c["qd"] - pr[:, A_DV:]).astype(bf16)
            nr_s[rev, i, 0:A_DK, :] = kt[:, 0:A_DV]
            nr_s[rev, i, A_DK:A_DK + C, :] = pr[:, 0:A_DV]
            gl_s[rev, i] = jnp.broadcast_to(c["gl"], (8, LANE))

    def prep(it, carry):
        for wv in range(waves):
            prep_group((it * waves + wv) * group)
        return carry

    waves = math.gcd(DELTA_WAVES, nchunk // group)
    lax.fori_loop(0, nchunk // (group * waves), prep, 0)

    def scan(i, carry):
        out = []
        for rev, S in enumerate(carry):
            j = (nchunk - 1 - i) if rev else i
            y = _dot(mp_s[rev, j], S.astype(bf16))
            nr = nr_s[rev, j]
            o_s[rev, pl.ds(pl.multiple_of(j * C, C), C), :] = y[A_DK:, :] + nr[A_DK:, :]
            out.append(S * gl_s[rev, j][0:1, :] + y[0:A_DK, :] + nr[0:A_DK, :])
        return tuple(out)

    S0 = jnp.zeros((A_DK, A_DV), f32)
    lax.fori_loop(0, nchunk, scan, (S0, S0))

    def finish(it, carry):
        for gi in range(group):
            r0 = pl.multiple_of((it * group + gi) * C, C)
            o = o_s[0, pl.ds(r0, C), :] + o_s[1, pl.ds(r0, C), :]
            zg = x_ref[0, pl.ds(r0, C), nqkv:nqkv + A_DV].astype(f32)
            o_ref[0, pl.ds(r0, C), :] = (_rms(o, on_ref[...]) * (zg * _sigmoid(zg))).astype(o_ref.dtype)
        return carry

    lax.fori_loop(0, nchunk // group, finish, 0)


def _deltanet(qkvz, abh, conv_w, gate_sc, onorm):
    B, L, _ = qkvz.shape
    C = DELTA_CHUNK
    nchunk = L // C
    hw = 4 * A_DK
    scratch = [pltpu.VMEM((2, nchunk, A_DK + C, A_DV), bf16),
               pltpu.VMEM((2, nchunk, A_DK + C, A_DV), f32),
               pltpu.VMEM((2, nchunk, 8, LANE), f32),
               pltpu.VMEM((2, L, A_DV), f32)]
    return pl.pallas_call(
        _deltanet_kernel,
        grid=(B, A_HEADS),
        in_specs=[pl.BlockSpec(memory_space=pltpu.SMEM),
                  pl.BlockSpec((1, L, hw), lambda b, h: (b, 0, h)),
                  pl.BlockSpec((1, L, LANE), lambda b, h: (b, 0, h)),
                  pl.BlockSpec((1, 8, 3 * A_DK), lambda b, h: (h, 0, 0)),
                  pl.BlockSpec((1, A_DV), lambda b, h: (0, 0)),
                  pl.BlockSpec((N_MASKS, C, C), lambda b, h: (0, 0, 0)),
                  pl.BlockSpec((N_MASKS, C, C), lambda b, h: (0, 0, 0))],
        out_specs=pl.BlockSpec((1, L, A_DV), lambda b, h: (b, 0, h)),
        out_shape=jax.ShapeDtypeStruct((B, L, A_WIDTH), bf16),
        scratch_shapes=scratch,
        compiler_params=_cparams("parallel", "parallel"),
        name="deltanet",
    )(gate_sc, qkvz, abh, conv_w, onorm.reshape(1, A_DV), *_delta_masks())


def _filter_spectrum_kernel(c_ref, s_ref, hs_ref, hd_ref, kr_ref, ki_ref):
    L = c_ref.shape[1]
    rows = kr_ref.shape[1]
    scale = 1.0 / (2 * L)

    def split(x):
        hi = x.astype(bf16)
        return hi, (x - hi.astype(f32)).astype(bf16)

    hs_hi, hs_lo = split(hs_ref[...])
    hd_hi, hd_lo = split(hd_ref[...])
    for k1 in range(kr_ref.shape[0]):
        c = c_ref[k1 * rows:(k1 + 1) * rows, :]
        s = s_ref[k1 * rows:(k1 + 1) * rows, :]
        kr_ref[k1] = (_dot(c, hs_hi) + _dot(c, hs_lo)) * scale
        ki_ref[k1] = (_dot(s, hd_hi) + _dot(s, hd_lo)) * (-scale)


def _filter_spectrum(hsum, hdiff):
    L, n = hsum.shape
    ct = HY_CT
    n_fft = 2 * L
    n2 = L // FFT_Z1
    d = jnp.arange(L, dtype=jnp.int32)[None, :]
    a1 = ((jnp.arange(FFT_K1, dtype=jnp.int32)[:, None] * d) % n_fft).astype(f32) * (2.0 * math.pi / n_fft)
    a2 = ((jnp.arange(n2, dtype=jnp.int32)[:, None] * d) % n2).astype(f32) * (2.0 * math.pi / n2)
    c1, s1 = jnp.cos(a1)[:, None, :], jnp.sin(a1)[:, None, :]
    c2, s2 = jnp.cos(a2)[None, :, :], jnp.sin(a2)[None, :, :]
    cmat = (c1 * c2 - s1 * s2).reshape(FFT_K1 * n2, L).astype(bf16)
    smat = (s1 * c2 + c1 * s2).reshape(FFT_K1 * n2, L).astype(bf16)
    once = pl.Buffered(1)
    return pl.pallas_call(
        _filter_spectrum_kernel,
        grid=(n // ct,),
        in_specs=[pl.BlockSpec((FFT_K1 * n2, L), lambda j: (0, 0), pipeline_mode=once),
                  pl.BlockSpec((FFT_K1 * n2, L), lambda j: (0, 0), pipeline_mode=once),
                  pl.BlockSpec((L, ct), lambda j: (0, j)),
                  pl.BlockSpec((L, ct), lambda j: (0, j))],
        out_specs=[pl.BlockSpec((FFT_K1, n2, ct), lambda j: (0, 0, j)),
                   pl.BlockSpec((FFT_K1, n2, ct), lambda j: (0, 0, j))],
        out_shape=[jax.ShapeDtypeStruct((FFT_K1, n2, n), f32),
                   jax.ShapeDtypeStruct((FFT_K1, n2, n), f32)],
        compiler_params=_cparams("parallel"),
        name="filter_spectrum",
    )(cmat, smat, hsum, hdiff)


FFT_N1 = 16
FFT_K1 = FFT_N1 // 2 + 1
FFT_Z1 = FFT_N1 // 2
FFT_ROWS = 32


def _lincomb(terms):
    groups = {}
    for c, a in terms:
        if abs(c) > 1e-9:
            groups.setdefault(round(abs(c), 9), []).append((c > 0, a))
    total = None
    for mag, items in groups.items():
        pos = [a for p, a in items if p]
        neg = [a for p, a in items if not p]
        s = None
        for a in pos:
            s = a if s is None else s + a
        for a in neg:
            s = -a if s is None else s - a
        if mag != 1.0:
            s = s * mag
        total = s if total is None else total + s
    return total


def _hyena_fft_kernel(m2f_ref, m2i_ref, twr_ref, twi_ref, x1_ref, x2_ref, v_ref, w1_ref, w2_ref, wv_ref,
                      kr0_ref, ki0_ref, kr1_ref, ki1_ref, db_ref, o_ref, z_s, g_s, d_s, y_s, b_s):
    L, ct = z_s.shape
    n2 = L // FFT_Z1
    R = min(HY_ROWS, L)
    nt = L // R
    r1 = min(FFT_ROWS, n2)
    halo = 16
    rowi = lax.broadcasted_iota(jnp.int32, (R, 1), 0)
    ang = 2.0 * math.pi / FFT_N1
    c16 = [[math.cos(ang * a * b) for b in range(FFT_K1)] for a in range(FFT_Z1)]
    s16 = [[math.sin(ang * a * b) for b in range(FFT_K1)] for a in range(FFT_Z1)]

    def conv_into(dst, x_ref, w_ref):
        def body(t, carry):
            r0 = pl.multiple_of(t * R, R)
            x = x_ref[0, pl.ds(r0, R), :].astype(f32)
            rp = pl.multiple_of(jnp.maximum(r0 - halo, 0), halo)
            rn = pl.multiple_of(jnp.minimum(r0 + R, L - halo), halo)
            prow = jnp.where(t > 0, x_ref[0, pl.ds(rp, halo), :][halo - 1:halo, :].astype(f32), 0.0)
            nrow = jnp.where(t < nt - 1, x_ref[0, pl.ds(rn, halo), :][0:1, :].astype(f32), 0.0)
            prev = jnp.where(rowi == 0, prow, pltpu.roll(x, 1, 0))
            nxt = jnp.where(rowi == R - 1, nrow, pltpu.roll(x, R - 1, 0))
            w = w_ref[...]
            dst[pl.ds(r0, R), :] = prev * w[0:1, :] + x * w[1:2, :] + nxt * w[2:3, :]
            return carry
        lax.fori_loop(0, nt, body, 0)

    def cols(k1):
        return slice(k1 * ct, (k1 + 1) * ct)

    conv_into(z_s, v_ref, wv_ref)
    stages = ((x1_ref, w1_ref, kr0_ref, ki0_ref), (x2_ref, w2_ref, kr1_ref, ki1_ref))
    for o, (g_ref, gw_ref, kr_ref, ki_ref) in enumerate(stages):
        conv_into(g_s, g_ref, gw_ref)

        def forward1(rt, carry):
            r0 = pl.multiple_of(rt * r1, r1)
            zs = [z_s[pl.ds(a * n2 + r0, r1), :] for a in range(FFT_Z1)]
            for k1 in range(FFT_K1):
                ar = _lincomb([(c16[a][k1], zs[a]) for a in range(FFT_Z1)])
                ai = _lincomb([(-s16[a][k1], zs[a]) for a in range(FFT_Z1)])
                twr = twr_ref[pl.ds(r0, r1), cols(k1)]
                twi = twi_ref[pl.ds(r0, r1), cols(k1)]
                if ai is None:
                    pr, pi = ar * twr, -(ar * twi)
                else:
                    pr, pi = ar * twr + ai * twi, ai * twr - ar * twi
                d_s[pl.ds(r0, r1), cols(k1)] = pr.astype(bf16)
                d_s[pl.ds(n2 + r0, r1), cols(k1)] = pi.astype(bf16)
            return carry

        lax.fori_loop(0, n2 // r1, forward1, 0)

        for k1 in range(FFT_K1):
            xk = _dot(m2f_ref[...], d_s[:, cols(k1)])
            xr, xi = xk[0:n2, :], xk[n2:, :]
            kr, ki = kr_ref[k1], ki_ref[k1]
            y_s[0:n2, cols(k1)] = (xr * kr - xi * ki).astype(bf16)
            y_s[n2:2 * n2, cols(k1)] = (xr * ki + xi * kr).astype(bf16)

        for k1 in range(FFT_K1):
            bk = _dot(m2i_ref[...], y_s[:, cols(k1)])
            br, bi = bk[0:n2, :], bk[n2:, :]
            twr, twi = twr_ref[:, cols(k1)], twi_ref[:, cols(k1)]
            b_s[0:n2, cols(k1)] = br * twr - bi * twi
            b_s[n2:2 * n2, cols(k1)] = br * twi + bi * twr

        def inverse1(rt, carry):
            r0 = pl.multiple_of(rt * r1, r1)
            brs = [b_s[pl.ds(r0, r1), cols(j)] for j in range(FFT_K1)]
            bis = [b_s[pl.ds(n2 + r0, r1), cols(j)] for j in range(FFT_K1)]
            for a in range(FFT_Z1):
                terms = [(1.0, brs[0]), (-1.0 if a % 2 else 1.0, brs[FFT_K1 - 1])]
                for j in range(1, FFT_K1 - 1):
                    terms += [(2.0 * c16[a][j], brs[j]), (-2.0 * s16[a][j], bis[j])]
                rows = pl.ds(a * n2 + r0, r1)
                z = g_s[rows, :] * (_lincomb(terms) + z_s[rows, :] * db_ref[o:o + 1, :])
                if o + 1 < len(stages):
                    z_s[rows, :] = z
                else:
                    o_ref[0, rows, :] = z.astype(o_ref.dtype)
            return carry

        lax.fori_loop(0, n2 // r1, inverse1, 0)


def _hyena_tables(L):
    n_fft = 2 * L
    n2 = L // FFT_Z1
    ct = HY_CT
    idx = jnp.arange(n2, dtype=jnp.int32)
    a2 = ((idx[:, None] * idx[None, :]) % n2).astype(f32) * (2.0 * math.pi / n2)
    c2, s2 = jnp.cos(a2), jnp.sin(a2)
    m2f = jnp.block([[c2, s2], [-s2, c2]]).astype(bf16)
    m2i = jnp.block([[c2, -s2], [s2, c2]]).astype(bf16)
    at = (idx[:, None] * jnp.arange(FFT_K1, dtype=jnp.int32)[None, :]).astype(f32) * (2.0 * math.pi / n_fft)
    twr = jnp.broadcast_to(jnp.cos(at)[:, :, None], (n2, FFT_K1, ct)).reshape(n2, FFT_K1 * ct)
    twi = jnp.broadcast_to(jnp.sin(at)[:, :, None], (n2, FFT_K1, ct)).reshape(n2, FFT_K1 * ct)
    return m2f, m2i, twr, twi


def _hyena_fft(hx, conv_w, tables, kr, ki, dbias):
    B, L, _ = hx.shape
    ct = HY_CT
    nct = B_WIDTH // ct
    n2 = L // FFT_Z1
    wide = FFT_K1 * ct
    once = pl.Buffered(1)
    m2f, m2i, twr, twi = tables

    def xspec(part):
        return pl.BlockSpec((1, L, ct), lambda c, b: (b, 0, part * nct + c))

    def wspec(part):
        return pl.BlockSpec((8, ct), lambda c, b: (0, part * nct + c))

    def kspec(order):
        return pl.BlockSpec((FFT_K1, n2, ct), lambda c, b: (0, 0, order * nct + c), pipeline_mode=once)

    def const(shape):
        return pl.BlockSpec(shape, lambda c, b: (0, 0), pipeline_mode=once)

    return pl.pallas_call(
        _hyena_fft_kernel,
        grid=(nct, B),
        in_specs=[const((2 * n2, 2 * n2)), const((2 * n2, 2 * n2)), const((n2, wide)), const((n2, wide)),
                  xspec(0), xspec(1), xspec(2), wspec(0), wspec(1), wspec(2),
                  kspec(0), kspec(0), kspec(1), kspec(1),
                  pl.BlockSpec((8, ct), lambda c, b: (0, c))],
        out_specs=pl.BlockSpec((1, L, ct), lambda c, b: (b, 0, c)),
        out_shape=jax.ShapeDtypeStruct((B, L, B_WIDTH), bf16),
        scratch_shapes=[pltpu.VMEM((L, ct), f32), pltpu.VMEM((L, ct), f32),
                        pltpu.VMEM((2 * n2, wide), bf16), pltpu.VMEM((2 * n2, wide), bf16),
                        pltpu.VMEM((2 * n2, wide), f32)],
        compiler_params=_cparams("parallel", "parallel"),
        name="hyena_fft",
    )(m2f, m2i, twr, twi, hx, hx, hx, conv_w, conv_w, conv_w, kr, ki, kr, ki, dbias)


RT_E1, RT_E2, RT_W1, RT_W2, RT_R1, RT_R2 = 0, 1, 2, 3, 4, 5
ROUTE_TM = ROW_TM
PROJ_PARTS = 2
COMBINE_PARTS = 4
EXPERT_TM = 512
SC_WINDOW = 128


def _pack_bf16_pairs(x):
    n = x.shape[1] // 2
    bits = lax.bitcast_convert_type(x.astype(bf16).astype(f32), jnp.uint32)
    return bits[:, :n] | (bits[:, n:] >> 16)


def _unpack_bf16_pairs(p):
    hi = lax.bitcast_convert_type(p & jnp.uint32(0xFFFF0000), f32)
    lo = lax.bitcast_convert_type(p << 16, f32)
    return hi, lo


ROUTER_ROWS = 40


def _route_topk(lt):
    row = lax.broadcasted_iota(jnp.int32, lt.shape, 0)
    neg = -jnp.inf
    big = ROUTER_LANES
    gl = jnp.where(row < N_GROUPS, lt, neg)
    gmax = jnp.max(gl, axis=0, keepdims=True)
    gsel = jnp.min(jnp.where(gl == gmax, row, big), axis=0, keepdims=True)
    psel = 1.0 / jnp.sum(jnp.exp(gl - gmax), axis=0, keepdims=True)
    first = N_GROUPS + gsel * EXPERTS_PER_GROUP
    el = jnp.where((row >= first) & (row < first + EXPERTS_PER_GROUP), lt, neg)
    t1 = jnp.max(el, axis=0, keepdims=True)
    i1 = jnp.min(jnp.where(el == t1, row, big), axis=0, keepdims=True)
    el2 = jnp.where(row == i1, neg, el)
    t2 = jnp.max(el2, axis=0, keepdims=True)
    i2 = jnp.min(jnp.where(el2 == t2, row, big), axis=0, keepdims=True)
    ex = jnp.exp(t2 - t1)
    w1 = psel / (1.0 + ex)
    w2 = ex * w1
    return i1 - N_GROUPS, i2 - N_GROUPS, w1, w2


def _proj_route_kernel(x_ref, *refs, n_in):
    a_refs = refs[0:n_in]
    w_refs = refs[n_in:2 * n_in]
    g_ref, wr_ref, up_ref, xo_ref, hpa_ref, hpb_ref, rt_ref, rtt_ref, cnt_ref = refs[2 * n_in:]
    tm = xo_ref.shape[0]
    rp = tm // PROJ_PARTS
    parts = [slice(p * rp, (p + 1) * rp) for p in range(PROJ_PARTS)]
    xs = []
    for r in parts:
        x = x_ref[r, :]
        for a_ref, w_ref in zip(a_refs, w_refs):
            x = x + _dot(a_ref[r, :], w_ref[...])
        xo_ref[r, :] = x
        xs.append(x)
    hs = [_rms(x, g_ref[...]) for x in xs]
    for r, h in zip(parts, hs):
        hp = _pack_bf16_pairs(h)
        quarter = hp.shape[1] // 2
        hpa_ref[r, :] = hp[:, :quarter]
        hpb_ref[r, :] = hp[:, quarter:]
    tops = []
    for h in hs:
        h_hi = h.astype(bf16)
        h_lo = (h - h_hi.astype(f32)).astype(bf16)
        both = _dot(h_hi, wr_ref[...])
        logits = both[:, :ROUTER_LANES] + both[:, ROUTER_LANES:] + _dot(h_lo, wr_ref[:, :ROUTER_LANES])
        tops.append(_route_topk(logits.T[0:ROUTER_ROWS, :]))
    row = lax.broadcasted_iota(jnp.int32, (ROUTER_LANES, rp), 0)
    onehots = [jnp.where((row == e1) | (row == e2), 1.0, 0.0) for e1, e2, _, _ in tops]
    onehot16 = jnp.concatenate(onehots, axis=1).astype(bf16)
    before = _dot(onehot16, up_ref[...])
    for p, (r, (e1, e2, w1, w2)) in enumerate(zip(parts, tops)):
        bp = before[:, r]
        r1 = jnp.sum(jnp.where(row == e1, bp, 0.0), axis=0, keepdims=True)
        r2 = jnp.sum(jnp.where(row == e2, bp, 0.0), axis=0, keepdims=True)
        rec = jnp.zeros((ROUTER_LANES, rp), f32)
        for ln, val in ((RT_E1, e1.astype(f32)), (RT_E2, e2.astype(f32)), (RT_W1, w1), (RT_W2, w2),
                        (RT_R1, r1), (RT_R2, r2)):
            rec = jnp.where(row == ln, val, rec)
        rtt_ref[:, r] = rec[0:rtt_ref.shape[0], :]
        rt_ref[r, :] = rec.T
    ones = jnp.ones((cnt_ref.shape[1], tm), bf16)
    cnt_ref[0] = _dot_nt(ones, onehot16)


def _proj_route(x, acts, ws, gain, w_router):
    T, D = x.shape
    tm = ROUTE_TM
    n_in = len(acts)
    up = jnp.asarray(np.triu(np.ones((tm, tm), np.float32), 1), dtype=bf16)
    in_specs = [pl.BlockSpec((tm, D), lambda i: (i, 0))]
    in_specs += [pl.BlockSpec((tm, a.shape[1]), lambda i: (i, 0)) for a in acts]
    in_specs += [pl.BlockSpec(w.shape, lambda i: (0, 0)) for w in ws]
    in_specs += [pl.BlockSpec((1, D), lambda i: (0, 0)),
                 pl.BlockSpec((D, 2 * ROUTER_LANES), lambda i: (0, 0)),
                 pl.BlockSpec((tm, tm), lambda i: (0, 0))]
    return pl.pallas_call(
        functools.partial(_proj_route_kernel, n_in=n_in),
        grid=(T // tm,),
        in_specs=in_specs,
        out_specs=[pl.BlockSpec((tm, D), lambda i: (i, 0)),
                   pl.BlockSpec((tm, D // 4), lambda i: (i, 0)),
                   pl.BlockSpec((tm, D // 4), lambda i: (i, 0)),
                   pl.BlockSpec((tm, ROUTER_LANES), lambda i: (i, 0)),
                   pl.BlockSpec((8, tm), lambda i: (0, i)),
                   pl.BlockSpec((1, 8, ROUTER_LANES), lambda i: (i, 0, 0))],
        out_shape=[jax.ShapeDtypeStruct((T, D), f32),
                   jax.ShapeDtypeStruct((T, D // 4), jnp.uint32),
                   jax.ShapeDtypeStruct((T, D // 4), jnp.uint32),
                   jax.ShapeDtypeStruct((T, ROUTER_LANES), f32),
                   jax.ShapeDtypeStruct((8, T), f32),
                   jax.ShapeDtypeStruct((T // tm, 8, ROUTER_LANES), f32)],
        compiler_params=_cparams("parallel"),
        name="proj_route",
    )(x, *acts, *ws, gain.reshape(1, D), w_router, up)


def _moe_rows(T):
    return 2 * T + N_EXPERTS * EXPERT_TM


def _moe_plan(route_t, counts):
    T = route_t.shape[1]
    tm = ROUTE_TM
    cnt = counts[:, 0, :N_EXPERTS].astype(jnp.int32)
    total = jnp.sum(cnt, axis=0)
    padded = ((total + EXPERT_TM - 1) // EXPERT_TM) * EXPERT_TM
    ends = jnp.cumsum(padded)
    tile_base = (ends - padded)[None, :] + jnp.cumsum(cnt, axis=0) - cnt
    base = jnp.broadcast_to(tile_base.T[:, :, None], (N_EXPERTS, T // tm, tm)).reshape(N_EXPERTS, T)
    ids = jnp.arange(N_EXPERTS, dtype=jnp.int32)[:, None]
    dest = []
    for e_row, r_row in ((RT_E1, RT_R1), (RT_E2, RT_R2)):
        e = route_t[e_row].astype(jnp.int32)
        dest.append(jnp.sum(jnp.where(ids == e[None, :], base, 0), axis=0) + route_t[r_row].astype(jnp.int32))
    n_tiles = _moe_rows(T) // EXPERT_TM
    n_used = ends[-1] // EXPERT_TM
    starts = jnp.arange(n_tiles, dtype=jnp.int32) * EXPERT_TM
    tile_expert = jnp.sum((starts[:, None] >= ends[None, :]).astype(jnp.int32), axis=1)
    last_expert = jnp.sum((jnp.maximum(n_used - 1, 0) * EXPERT_TM >= ends).astype(jnp.int32))
    tile_expert = jnp.minimum(tile_expert, last_expert)
    tiles = jnp.arange(n_tiles, dtype=jnp.int32)
    prev_expert = jnp.concatenate([jnp.full((1,), -1, jnp.int32), tile_expert[:-1]])
    seg_first = ((tile_expert != prev_expert) & (tiles < n_used)).astype(jnp.int32)
    seg_slot = (jnp.cumsum(seg_first) - 1) % 2
    seg_end = jnp.sum(jnp.where(ids == tile_expert[None, :], (ends // EXPERT_TM)[:, None], 0), axis=0)
    nxt = jnp.where(seg_end < n_used, tile_expert[jnp.minimum(seg_end, n_tiles - 1)], -1)
    plan = (tile_expert, seg_first, seg_slot.astype(jnp.int32), nxt.astype(jnp.int32),
            n_used.reshape(1).astype(jnp.int32))
    return dest, plan


def _sc_mesh():
    return plsc.VectorSubcoreMesh(core_axis_name="core", subcore_axis_name="subcore")


def _sc_scatter_pair(rows, d0, d1, n_out):
    T, W = rows.shape
    win = SC_WINDOW

    @pl.kernel(out_type=jax.ShapeDtypeStruct((n_out, W), rows.dtype), mesh=_sc_mesh(), name="moe_sc_scatter")
    def scatter(x_hbm, i0_hbm, i1_hbm, o_hbm):
        def body(x_vmem, i0_vmem, i1_vmem):
            pltpu.sync_copy(x_vmem, o_hbm.at[i0_vmem.at[0]])
            pltpu.sync_copy(x_vmem, o_hbm.at[i1_vmem.at[0]])

        pltpu.emit_pipeline(
            body, grid=(T // win,),
            in_specs=[pl.BlockSpec((win, W), index_map=lambda i: (i, 0)),
                      pl.BlockSpec((1, win), index_map=lambda i: (0, i)),
                      pl.BlockSpec((1, win), index_map=lambda i: (0, i))],
            out_specs=[],
            core_axis_name=("core", "subcore"), dimension_semantics=(pltpu.PARALLEL,),
        )(x_hbm, i0_hbm, i1_hbm)

    return scatter(rows, d0.reshape(1, T), d1.reshape(1, T))


def _sc_gather(table, idx):
    n = idx.shape[0]
    W = table.shape[1]
    win = SC_WINDOW

    @pl.kernel(out_type=jax.ShapeDtypeStruct((n, W), table.dtype), mesh=_sc_mesh(), name="moe_sc_gather")
    def gather(t_hbm, i_hbm, o_hbm):
        def body(i_vmem, o_vmem):
            pltpu.sync_copy(t_hbm.at[i_vmem.at[0]], o_vmem)

        pltpu.emit_pipeline(
            body, grid=(n // win,),
            in_specs=[pl.BlockSpec((1, win), index_map=lambda i: (0, i))],
            out_specs=[pl.BlockSpec((win, W), index_map=lambda i: (i, 0))],
            core_axis_name=("core", "subcore"), dimension_semantics=(pltpu.PARALLEL,),
        )(i_hbm, o_hbm)

    return gather(table, idx.reshape(1, n))


def _expert_kernel(te_ref, first_ref, slot_ref, nxt_ref, nu_ref, xa_ref, xb_ref, wg_hbm, wu_hbm, wd_hbm,
                   ya_ref, yb_ref, wg_s, wu_s, wd_s, wg16, wu16, wd16, sem):
    i = pl.program_id(0)
    active = i < nu_ref[0]

    def weight_copies(expert, s):
        return (pltpu.make_async_copy(wg_hbm.at[expert], wg_s.at[s], sem.at[s, 0]),
                pltpu.make_async_copy(wu_hbm.at[expert], wu_s.at[s], sem.at[s, 1]),
                pltpu.make_async_copy(wd_hbm.at[expert], wd_s.at[s], sem.at[s, 2]))

    @pl.when(i == 0)
    def _():
        for c in weight_copies(te_ref[0], slot_ref[0]):
            c.start()

    @pl.when(active & (first_ref[i] == 1))
    def _():
        s = slot_ref[i]
        for c in weight_copies(te_ref[i], s):
            c.wait()

        @pl.when(nxt_ref[i] >= 0)
        def _():
            for c in weight_copies(nxt_ref[i], 1 - s):
                c.start()

        wg16[...] = wg_s[s].astype(bf16)
        wu16[...] = wu_s[s].astype(bf16)
        wd16[...] = wd_s[s].astype(bf16)

    @pl.when(active)
    def _():
        hi, lo = _unpack_bf16_pairs(jnp.concatenate([xa_ref[...], xb_ref[...]], axis=1))
        x = jnp.concatenate([hi, lo], axis=1).astype(bf16)
        gate = _dot(x, wg16[...])
        up = _dot(x, wu16[...])
        hg = (gate * _sigmoid(gate) * up).astype(bf16)
        y = _pack_bf16_pairs(_dot(hg, wd16[...]))
        quarter = y.shape[1] // 2
        ya_ref[...] = y[:, :quarter]
        yb_ref[...] = y[:, quarter:]

    @pl.when(jnp.logical_not(active))
    def _():
        ya_ref[...] = jnp.zeros(ya_ref.shape, ya_ref.dtype)
        yb_ref[...] = jnp.zeros(yb_ref.shape, yb_ref.dtype)


def _experts(plan, layer, xa, xb, wg, wu, wd):
    tile_expert, seg_first, seg_slot, nxt, n_used = plan
    off = layer * N_EXPERTS
    n_rows, quarter = xa.shape
    D = 4 * quarter
    tm = EXPERT_TM
    row_spec = pl.BlockSpec((tm, quarter), lambda i, *_: (i, 0))
    grid_spec = pltpu.PrefetchScalarGridSpec(
        num_scalar_prefetch=5,
        grid=(n_rows // tm,),
        in_specs=[row_spec, row_spec,
                  pl.BlockSpec(memory_space=pl.ANY), pl.BlockSpec(memory_space=pl.ANY), pl.BlockSpec(memory_space=pl.ANY)],
        out_specs=[row_spec, row_spec],
        scratch_shapes=[pltpu.VMEM((2, D, D_EXPERT), wg.dtype), pltpu.VMEM((2, D, D_EXPERT), wu.dtype),
                        pltpu.VMEM((2, D_EXPERT, D), wd.dtype),
                        pltpu.VMEM((D, D_EXPERT), bf16), pltpu.VMEM((D, D_EXPERT), bf16), pltpu.VMEM((D_EXPERT, D), bf16),
                        pltpu.SemaphoreType.DMA((2, 3))])
    return pl.pallas_call(
        _expert_kernel,
        grid_spec=grid_spec,
        out_shape=[jax.ShapeDtypeStruct((n_rows, quarter), jnp.uint32)] * 2,
        compiler_params=_cparams("arbitrary"),
        name="moe_experts",
    )(tile_expert + off, seg_first, seg_slot, jnp.where(nxt >= 0, nxt + off, -1), n_used, xa, xb, wg, wu, wd)


def _combine_kernel(x_ref, rt_ref, g_ref, a1_ref, b1_ref, a2_ref, b2_ref, *rest, final):
    half = x_ref.shape[1] // 2
    rp = x_ref.shape[0] // COMBINE_PARTS
    for p in range(COMBINE_PARTS):
        r = slice(p * rp, (p + 1) * rp)
        h1, l1 = _unpack_bf16_pairs(jnp.concatenate([a1_ref[r, :], b1_ref[r, :]], axis=1))
        h2, l2 = _unpack_bf16_pairs(jnp.concatenate([a2_ref[r, :], b2_ref[r, :]], axis=1))
        w1 = rt_ref[r, RT_W1:RT_W1 + 1]
        w2 = rt_ref[r, RT_W2:RT_W2 + 1]
        out = jnp.concatenate([x_ref[r, 0:half] + w1 * h1 + w2 * h2, x_ref[r, half:] + w1 * l1 + w2 * l2], axis=1)
        if final:
            (o_ref,) = rest
            o_ref[r, :] = _rms(out, g_ref[...])
        else:
            w_ref, o_ref, p_ref = rest
            o_ref[r, :] = out
            p_ref[r, :] = _dot(_rms(out, g_ref[...]).astype(bf16), w_ref[...]).astype(p_ref.dtype)


def _combine(x, route, ga, gb, gain, w_next=None):
    T, D = x.shape
    tm = ROUTE_TM
    n = T // tm
    quarter = D // 4
    final = w_next is None
    slot1 = pl.BlockSpec((tm, quarter), lambda i: (i, 0))
    slot2 = pl.BlockSpec((tm, quarter), lambda i: (i + n, 0))
    row_spec = pl.BlockSpec((tm, D), lambda i: (i, 0))
    in_specs = [row_spec, pl.BlockSpec((tm, ROUTER_LANES), lambda i: (i, 0)), pl.BlockSpec((1, D), lambda i: (0, 0)),
                slot1, slot1, slot2, slot2]
    args = [x, route, gain.reshape(1, D), ga, gb, ga, gb]
    out_specs = [row_spec]
    out_shape = [jax.ShapeDtypeStruct((T, D), f32)]
    if not final:
        n_next = w_next.shape[1]
        in_specs.append(pl.BlockSpec((D, n_next), lambda i: (0, 0)))
        args.append(w_next)
        out_specs.append(pl.BlockSpec((tm, n_next), lambda i: (i, 0)))
        out_shape.append(jax.ShapeDtypeStruct((T, n_next), bf16))
    outs = pl.pallas_call(
        functools.partial(_combine_kernel, final=final),
        grid=(n,),
        in_specs=in_specs,
        out_specs=out_specs,
        out_shape=out_shape,
        compiler_params=_cparams("parallel"),
        name="moe_combine",
    )(*args)
    return outs[0] if final else outs


def _moe(layer, x, hpa, hpb, route, route_t, counts, wg, wu, wd, gain, w_next=None):
    (d0, d1), plan = _moe_plan(route_t, counts)
    n_rows = _moe_rows(x.shape[0])
    xa = _sc_scatter_pair(hpa, d0, d1, n_rows)
    xb = _sc_scatter_pair(hpb, d0, d1, n_rows)
    ya, yb = _experts(plan, layer, xa, xb, wg, wu, wd)
    dcat = jnp.concatenate([d0, d1])
    return _combine(x, route, _sc_gather(ya, dcat), _sc_gather(yb, dcat), gain, w_next)


SWA_LOCKSTEP = 2


def _swa_bias_tables():
    blk = C_BLOCK
    qi = np.arange(blk)[:, None]
    kj = np.arange(3 * blk)[None, :]
    dist = np.abs(blk + qi - kj)
    tabs = np.full((3, C_KV_HEADS, C_GROUP * blk, 3 * blk), -np.inf, np.float32)
    for case in range(3):
        ok = dist <= WINDOW
        if case == 0:
            ok = ok & (kj >= blk)
        if case == 2:
            ok = ok & (kj < 2 * blk)
        for hh in range(C_HEADS):
            slope = 2.0 ** (-8.0 * (hh + 1) / C_HEADS)
            t = np.where(ok, -slope * LOG2E * dist, -np.inf)
            kvh, g = divmod(hh, C_GROUP)
            tabs[case, kvh, g * blk:(g + 1) * blk] = t
    return jnp.asarray(tabs)


def _swa_kernel(sink_ref, q_ref, kp_ref, kc_ref, kn_ref, vp_ref, vc_ref, vn_ref, bias_ref, o_ref):
    blk = C_BLOCK
    hd = C_HEAD_DIM
    kcat = jnp.concatenate([kp_ref[0], kc_ref[0], kn_ref[0]], axis=0)
    vcat = jnp.concatenate([vp_ref[0], vc_ref[0], vn_ref[0]], axis=0)
    q = q_ref[0]
    rblk = lax.broadcasted_iota(jnp.int32, (C_GROUP * blk, 1), 0) // blk
    lane = lax.broadcasted_iota(jnp.int32, (3 * blk, LANE - hd), 1)
    ones_pad = jnp.where(lane == 0, 1.0, 0.0).astype(bf16)
    q4s, sinks = [], []
    for kvh in range(C_KV_HEADS):
        q4s.append(jnp.concatenate(
            [q[:, (kvh * C_GROUP + g) * hd:(kvh * C_GROUP + g + 1) * hd] for g in range(C_GROUP)], axis=0))
        sk = jnp.zeros((C_GROUP * blk, 1), f32)
        for g in range(C_GROUP):
            sk = jnp.where(rblk == g, sink_ref[kvh * C_GROUP + g] * LOG2E, sk)
        sinks.append(sk)
    outs = []
    for h0 in range(0, C_KV_HEADS, SWA_LOCKSTEP):
        hs = range(h0, h0 + SWA_LOCKSTEP)
        ss = [_dot_nt(q4s[h], kcat[:, h * hd:(h + 1) * hd]) + bias_ref[0, h] for h in hs]
        ms = [jnp.maximum(jnp.max(s, axis=-1, keepdims=True), sinks[h]) for s, h in zip(ss, hs)]
        ps = [jnp.exp2(s - m).astype(bf16) for s, m in zip(ss, ms)]
        pvs = [_dot(p, jnp.concatenate([vcat[:, h * hd:(h + 1) * hd], ones_pad], axis=1)) for p, h in zip(ps, hs)]
        for pv, m, h in zip(pvs, ms, hs):
            den = pv[:, hd:hd + 1] + jnp.exp2(sinks[h] - m)
            o4 = pv[:, 0:hd] / den
            outs += [o4[g * blk:(g + 1) * blk, :] for g in range(C_GROUP)]
    o_ref[0] = jnp.concatenate(outs, axis=1).astype(o_ref.dtype)


def _swa(qkv, sinks):
    B, L, _ = qkv.shape
    nb = L // C_BLOCK
    dq = C_HEADS * C_HEAD_DIM
    dkv = C_KV_HEADS * C_HEAD_DIM
    k_col = dq // dkv
    v_col = k_col + 1

    def kv_spec(col, shift):
        return pl.BlockSpec((1, C_BLOCK, dkv),
                            lambda n, b: (b, jnp.clip(n + shift, 0, nb - 1), col))

    def edge_case(n, b):
        return (jnp.where(n == 0, 0, jnp.where(n == nb - 1, 2, 1)), 0, 0, 0)

    return pl.pallas_call(
        _swa_kernel,
        grid=(nb, B),
        in_specs=[pl.BlockSpec(memory_space=pltpu.SMEM),
                  pl.BlockSpec((1, C_BLOCK, dq), lambda n, b: (b, n, 0)),
                  kv_spec(k_col, -1), kv_spec(k_col, 0), kv_spec(k_col, 1),
                  kv_spec(v_col, -1), kv_spec(v_col, 0), kv_spec(v_col, 1),
                  pl.BlockSpec((1, C_KV_HEADS, C_GROUP * C_BLOCK, 3 * C_BLOCK), edge_case)],
        out_specs=pl.BlockSpec((1, C_BLOCK, dq), lambda n, b: (b, n, 0)),
        out_shape=jax.ShapeDtypeStruct((B, L, dq), bf16),
        compiler_params=_cparams("parallel", "parallel"),
        name="swa",
    )(sinks, qkv, qkv, qkv, qkv, qkv, qkv, qkv, _swa_bias_tables())


def _hyena_filters(L, w1, b1, w2, b2, w3, b3, w4, freq):
    hp = lax.Precision.HIGHEST
    t = jnp.linspace(0.0, 1.0, L, dtype=f32)[:, None]
    wpos = (2.0 * math.pi / L) * jnp.arange(L, dtype=f32)[:, None]
    bands = jnp.linspace(1e-4, HY_BANDS - 1, HY_BANDS, dtype=f32)[None, :]
    feats = jnp.concatenate([t, jnp.cos(bands * wpos), -jnp.sin(bands * wpos)], axis=-1)
    h = jnp.sin(freq[0] * (jnp.dot(feats, w1, precision=hp) + b1))
    h = jnp.sin(freq[1] * (jnp.dot(h, w2, precision=hp) + b2))
    h = jnp.sin(freq[2] * (jnp.dot(h, w3, precision=hp) + b3))
    h = jnp.dot(h, w4, precision=hp).reshape(L, HY_ORDER, 2, B_WIDTH)
    deltas = jnp.abs(jnp.linspace(HY_MIN_DECAY, HY_MAX_DECAY, B_WIDTH, dtype=f32))
    window = jnp.exp(-t * deltas[None, :])
    return h * window[:, None, None, :]


def _pad_rows(w, rows=8):
    return jnp.pad(w, ((0, rows - w.shape[0]), (0, 0)))


def _even_in_weights(w_in):
    hk = A_HEADS * A_DK
    q = w_in[:, 0:hk].reshape(D_MODEL, A_HEADS, A_DK)
    k = w_in[:, hk:2 * hk].reshape(D_MODEL, A_HEADS, A_DK)
    v = w_in[:, 2 * hk:A_QKV].reshape(D_MODEL, A_HEADS, A_DV)
    z = w_in[:, A_QKV:A_QKV + A_WIDTH].reshape(D_MODEL, A_HEADS, A_DV)
    qkvz = jnp.concatenate([q, k, v, z], axis=-1).reshape(D_MODEL, A_HEADS * 4 * A_DK)
    ab = w_in[:, A_QKV + A_WIDTH:A_QKV + A_WIDTH + 4 * A_HEADS].reshape(D_MODEL, 4, A_HEADS)
    ab = jnp.transpose(ab, (0, 2, 1))
    ab = jnp.pad(ab.reshape(D_MODEL, 4 * A_HEADS), ((0, 0), (0, LANE - 4 * A_HEADS)))
    hx = w_in[:, A_QKV + A_WIDTH + 4 * A_HEADS:]
    return jnp.concatenate([qkvz, ab, hx], axis=1).astype(bf16)


def _router_weights(w_group, w_expert):
    w = jnp.concatenate([w_group, w_expert], axis=1)
    w = jnp.pad(w, ((0, 0), (0, ROUTER_LANES - w.shape[1])))
    hi = w.astype(bf16)
    lo = (w - hi.astype(f32)).astype(bf16)
    return jnp.concatenate([hi, lo], axis=1)


def kernel(x_prompt, x_sample, norm_mix, norm_ffn, norm_final, ev_w_in, ev_conv_a, ev_alog_f, ev_alog_b, ev_dtb_f, ev_dtb_b, ev_onorm, ev_conv_b, hy_w1, hy_b1, hy_w2, hy_b2, hy_w3, hy_b3, hy_w4, hy_freq, hy_dbias, ev_w_out, od_w_qkv, od_sinks, od_w_out, moe_w_group, moe_w_expert, moe_w_gate, moe_w_up, moe_w_down):
    L = x_prompt.shape[1]
    n_layers = moe_w_gate.shape[0]
    wg = moe_w_gate.reshape(n_layers * N_EXPERTS, D_MODEL, D_EXPERT)
    wu = moe_w_up.reshape(n_layers * N_EXPERTS, D_MODEL, D_EXPERT)
    wd = moe_w_down.reshape(n_layers * N_EXPERTS, D_EXPERT, D_MODEL)

    w_in = _even_in_weights(ev_w_in[0])
    n_qkvz = A_HEADS * 4 * A_DK
    n_ab = A_HEADS * LANE
    n_hx = 3 * B_WIDTH
    hk = A_HEADS * A_DK
    ca = ev_conv_a[0]
    conv_a = jnp.concatenate([ca[0:hk].reshape(A_HEADS, A_DK, 3), ca[hk:2 * hk].reshape(A_HEADS, A_DK, 3),
                              ca[2 * hk:].reshape(A_HEADS, A_DV, 3)], axis=1)
    conv_a = jnp.pad(jnp.transpose(conv_a, (0, 2, 1)), ((0, 0), (0, 5), (0, 0)))
    gate_sc = jnp.stack([ev_alog_f[0], ev_alog_b[0], ev_dtb_f[0], ev_dtb_b[0]], axis=1)
    filt = _hyena_filters(L, hy_w1[0], hy_b1[0], hy_w2[0], hy_b2[0], hy_w3[0], hy_b3[0], hy_w4[0], hy_freq[0])
    hsum = (filt[:, :, 0] + filt[:, :, 1]).reshape(L, HY_ORDER * B_WIDTH)
    hdiff = (filt[:, :, 0] - filt[:, :, 1]).reshape(L, HY_ORDER * B_WIDTH)
    kr, ki = _filter_spectrum(hsum, hdiff)
    fft_tables = _hyena_tables(L)
    conv_b = _pad_rows(ev_conv_b[0].T)
    dbias = _pad_rows(hy_dbias[0])
    w_out = ev_w_out[0].astype(bf16)
    routers = [_router_weights(moe_w_group[l], moe_w_expert[l]) for l in range(n_layers)]
    n_qkv = od_w_qkv.shape[-1]
    dq = C_HEADS * C_HEAD_DIM
    w_qkv = jnp.concatenate([od_w_qkv[0][:, :dq] * (LOG2E * C_HEAD_DIM ** -0.5), od_w_qkv[0][:, dq:]], axis=1)
    w_qkv = w_qkv.astype(bf16)
    w_att_out = od_w_out[0].astype(bf16)

    def trunk(xb):
        B = xb.shape[0]
        T = B * L
        x = xb.reshape(T, D_MODEL)
        qkvz, abh, hx = _even_in_proj(x, norm_mix[0], w_in)
        o_a = _deltanet(qkvz.reshape(B, L, n_qkvz), abh.reshape(B, L, n_ab), conv_a, gate_sc, ev_onorm[0])
        o_b = _hyena_fft(hx.reshape(B, L, n_hx), conv_b, fft_tables, kr, ki, dbias)
        x, hpa, hpb, route, route_t, counts = _proj_route(
            x, (o_a.reshape(T, A_WIDTH), o_b.reshape(T, B_WIDTH)), (w_out[:A_WIDTH], w_out[A_WIDTH:]),
            norm_ffn[0], routers[0])
        x, qkv = _moe(0, x, hpa, hpb, route, route_t, counts, wg, wu, wd, norm_mix[1], w_qkv)
        o_c = _swa(qkv.reshape(B, L, n_qkv), od_sinks[0])
        x, hpa, hpb, route, route_t, counts = _proj_route(
            x, (o_c.reshape(T, C_HEADS * C_HEAD_DIM),), (w_att_out,), norm_ffn[1], routers[1])
        y = _moe(1, x, hpa, hpb, route, route_t, counts, wg, wu, wd, norm_final)
        return y.reshape(B, L, D_MODEL)

    return (trunk(x_prompt), trunk(x_sample))
```

```python
import functools
import math

import jax
import jax.numpy as jnp
import numpy as np
from jax import lax
from jax.experimental import pallas as pl
from jax.experimental.pallas import tpu as pltpu
from jax.experimental.pallas import tpu_sc as plsc

f32 = jnp.float32
bf16 = jnp.bfloat16

EPS = 1e-6
D_MODEL = 1024

A_HEADS = 4
A_DK = 128
A_DV = 128
A_WIDTH = A_HEADS * A_DV
A_QKV = 2 * A_HEADS * A_DK + A_HEADS * A_DV
DELTA_CHUNK = 128
INV_BASE = 8
DELTA_GROUP = 8
DELTA_WAVES = 2

B_WIDTH = D_MODEL - A_WIDTH
HY_ORDER = 2
HY_EMB = 33
HY_BANDS = (HY_EMB - 1) // 2
HY_TARGET = 1e-2
HY_MIN_DECAY = math.log(HY_TARGET) / 1.5
HY_MAX_DECAY = math.log(HY_TARGET) / 0.3
HY_CT = 256
HY_ROWS = 256

C_HEADS = 16
C_KV_HEADS = 4
C_HEAD_DIM = 64
C_GROUP = C_HEADS // C_KV_HEADS
WINDOW = 128
C_BLOCK = 128

N_GROUPS = 4
EXPERTS_PER_GROUP = 8
N_EXPERTS = N_GROUPS * EXPERTS_PER_GROUP
D_EXPERT = 256
ROUTER_LANES = 128

LANE = 128
SUBLANE = 8
SUBLANE_BF16 = 16
ROW_TM = 512
VMEM_LIMIT = 56 * 1024 * 1024
LOG2E = 1.4426950408889634


def _cparams(*sem):
    return pltpu.CompilerParams(dimension_semantics=sem, vmem_limit_bytes=VMEM_LIMIT)


def _dot(a, b):
    return jnp.dot(a, b, preferred_element_type=f32)


def _dot_nt(a, b):
    return lax.dot_general(a, b, (((1,), (1,)), ((), ())), preferred_element_type=f32)


def _dot_tn(a, b):
    return lax.dot_general(a, b, (((0,), (0,)), ((), ())), preferred_element_type=f32)


def _sigmoid(x):
    return 1.0 / (1.0 + jnp.exp(-x))


def _rms(x, g):
    return x * lax.rsqrt(jnp.mean(x * x, axis=-1, keepdims=True) + EPS) * g


def _even_in_kernel(x_ref, g_ref, w_ref, qkvz_ref, ab_ref, hx_ref):
    h = _rms(x_ref[...], g_ref[...]).astype(bf16)
    n_qkvz = qkvz_ref.shape[1]
    qkvz_ref[...] = _dot(h, w_ref[:, 0:n_qkvz]).astype(qkvz_ref.dtype)
    ab = _dot(h, w_ref[:, n_qkvz:n_qkvz + LANE])
    for hd in range(A_HEADS):
        ab_ref[:, hd * LANE:(hd + 1) * LANE] = ab if hd == 0 else pltpu.roll(ab, LANE - 4 * hd, 1)
    hx_ref[...] = _dot(h, w_ref[:, n_qkvz + LANE:]).astype(hx_ref.dtype)


def _even_in_proj(x, gain, w):
    T, D = x.shape
    tm = ROW_TM
    n_qkvz = A_HEADS * 4 * A_DK
    n_ab = A_HEADS * LANE
    n_hx = 3 * B_WIDTH
    return pl.pallas_call(
        _even_in_kernel,
        grid=(T // tm,),
        in_specs=[pl.BlockSpec((tm, D), lambda i: (i, 0)),
                  pl.BlockSpec((1, D), lambda i: (0, 0)),
                  pl.BlockSpec(w.shape, lambda i: (0, 0))],
        out_specs=[pl.BlockSpec((tm, n_qkvz), lambda i: (i, 0)),
                   pl.BlockSpec((tm, n_ab), lambda i: (i, 0)),
                   pl.BlockSpec((tm, n_hx), lambda i: (i, 0))],
        out_shape=[jax.ShapeDtypeStruct((T, n_qkvz), bf16),
                   jax.ShapeDtypeStruct((T, n_ab), f32),
                   jax.ShapeDtypeStruct((T, n_hx), bf16)],
        compiler_params=_cparams("parallel"),
        name="even_in_proj",
    )(x, gain.reshape(1, D), w)


MK_LOWER, MK_UPPER, MK_BASE, MK_LEVEL0 = 0, 1, 2, 3
N_LEVELS = int(math.log2(DELTA_CHUNK // INV_BASE))
MK_EYE = MK_LEVEL0 + N_LEVELS
N_MASKS = MK_EYE + 1


def _delta_masks():
    c = np.arange(DELTA_CHUNK)[:, None]
    s = np.arange(DELTA_CHUNK)[None, :]
    m = [s <= c, s >= c, (c // INV_BASE == s // INV_BASE) & (c != s)]
    b = INV_BASE
    while b < DELTA_CHUNK:
        m.append((c // (2 * b) == s // (2 * b)) & (c // b != s // b))
        b *= 2
    m.append(c == s)
    m = np.stack(m).astype(np.float32)
    return jnp.asarray(m), jnp.asarray(m, dtype=bf16)


def _tri_inverse(mats, mk_ref, mk16_ref):
    ps = [a * mk_ref[MK_BASE] for a in mats]
    ts = [mk_ref[MK_EYE] - p for p in ps]
    n = 2
    while n < INV_BASE:
        p16s = [p.astype(bf16) for p in ps]
        ps = [_dot(p16, p16) for p16 in p16s]
        ts = [t + _dot(t.astype(bf16), p.astype(bf16)) for t, p in zip(ts, ps)]
        n *= 2
    a16s = [a.astype(bf16) for a in mats]
    t16s = [t.astype(bf16) for t in ts]
    for lvl in range(N_LEVELS):
        xs = [_dot(a16 * mk16_ref[MK_LEVEL0 + lvl], t16) for a16, t16 in zip(a16s, t16s)]
        t16s = [t16 - _dot(t16, x.astype(bf16)).astype(bf16) for t16, x in zip(t16s, xs)]
    return t16s


def _deltanet_kernel(sc_ref, x_ref, ab_ref, cw_ref, on_ref, mk_ref, mk16_ref, o_ref, mp_s, nr_s, gl_s, o_s):
    hd = pl.program_id(1)
    L = x_ref.shape[1]
    C = DELTA_CHUNK
    nchunk = L // C
    group = math.gcd(DELTA_GROUP, nchunk)
    nqkv = 3 * A_DK
    halo = SUBLANE_BF16

    lane = lax.broadcasted_iota(jnp.int32, (1, LANE), 1)
    alog = jnp.where(lane == 2, sc_ref[hd, 0], jnp.where(lane == 3, sc_ref[hd, 1], 0.0))
    dtb = jnp.where(lane == 2, sc_ref[hd, 2], jnp.where(lane == 3, sc_ref[hd, 3], 0.0))
    neg_a = jnp.where((lane == 2) | (lane == 3), -jnp.exp(alog), 0.0)
    rowi = lax.broadcasted_iota(jnp.int32, (C, 1), 0)
    cw = cw_ref[0]

    def chunk_inputs(i):
        r0 = pl.multiple_of(i * C, C)
        xc = x_ref[0, pl.ds(r0, C), 0:nqkv].astype(f32)
        rp = pl.multiple_of(jnp.maximum(r0 - halo, 0), halo)
        rn = pl.multiple_of(jnp.minimum(r0 + C, L - halo), halo)
        prow = x_ref[0, pl.ds(rp, halo), 0:nqkv][halo - 1:halo, :].astype(f32)
        nrow = x_ref[0, pl.ds(rn, halo), 0:nqkv][0:1, :].astype(f32)
        prow = jnp.where(i > 0, prow, 0.0)
        nrow = jnp.where(i < nchunk - 1, nrow, 0.0)
        prev = jnp.where(rowi == 0, prow, pltpu.roll(xc, 1, 0))
        nxt = jnp.where(rowi == C - 1, nrow, pltpu.roll(xc, C - 1, 0))
        y = prev * cw[0:1, :] + xc * cw[1:2, :] + nxt * cw[2:3, :]
        y = y * _sigmoid(y)
        q = y[:, 0:A_DK]
        k = y[:, A_DK:2 * A_DK]
        vc = y[:, 2 * A_DK:nqkv]
        qc = q * lax.rsqrt(jnp.sum(q * q, axis=-1, keepdims=True) + EPS) * (A_DK ** -0.5)
        kc = k * lax.rsqrt(jnp.sum(k * k, axis=-1, keepdims=True) + EPS)

        abc = ab_ref[0, pl.ds(r0, C), :]
        zg = abc + dtb
        g = neg_a * (jnp.maximum(zg, 0.0) + jnp.log(1.0 + jnp.exp(-jnp.abs(zg))))
        low16 = mk16_ref[MK_LOWER]
        g1 = g.astype(bf16)
        e1 = g - g1.astype(f32)
        g2 = e1.astype(bf16)
        g3 = (e1 - g2.astype(f32)).astype(bf16)
        pre = _dot(low16, g1) + _dot(low16, g2) + _dot(low16, g3)
        tot = pre[C - 1:C, :]
        suf = tot - pre + g
        beta_all = _sigmoid(abc)

        k16 = kc.astype(bf16)
        kk = _dot_nt(k16, k16)
        qk = _dot_nt(qc.astype(bf16), k16)

        chains = []
        for rev in (0, 1):
            beta = beta_all[:, rev:rev + 1]
            gcc = (suf if rev else pre)[:, 2 + rev:3 + rev]
            tot11 = tot[:, 2 + rev:3 + rev]
            keep = mk_ref[MK_UPPER if rev else MK_LOWER]
            bc = jnp.broadcast_to(gcc * LOG2E, (C, C))
            decay = jnp.exp2((bc - bc.T) * keep) * keep
            a = (beta * kk) * decay
            eg = jnp.exp(gcc)
            chains.append(dict(
                i=i, rev=rev, a=a, gl=jnp.exp(tot11), qd=qc * eg,
                rhs=jnp.concatenate([vc * beta, kc * (beta * eg)], axis=1).astype(bf16),
                kd16=(kc * jnp.exp(tot11 - gcc)).astype(bf16),
                qk16=(qk * decay).astype(bf16)))
        return chains

    def prep_group(first_chunk):
        chains = []
        for gi in range(group):
            chains += chunk_inputs(first_chunk + gi)
        t16s = _tri_inverse([c["a"] for c in chains], mk_ref, mk16_ref)
        uws = [_dot(t16, c["rhs"]).astype(bf16) for t16, c in zip(t16s, chains)]
        kts = [_dot_tn(c["kd16"], uw) for c, uw in zip(chains, uws)]
        prs = [_dot(c["qk16"], uw) for c, uw in zip(chains, uws)]
        for c, kt, pr in zip(chains, kts, prs):
            rev, i = c["rev"], c["i"]
            mp_s[rev, i, 0:A_DK, :] = (-kt[:, A_DV:]).astype(bf16)
            mp_s[rev, i, A_DK:A_DK + C, :] = (c["qd"] - pr[:, A_DV:]).astype(bf16)
            nr_s[rev, i, 0:A_DK, :] = kt[:, 0:A_DV]
            nr_s[rev, i, A_DK:A_DK + C, :] = pr[:, 0:A_DV]
            gl_s[rev, i] = jnp.broadcast_to(c["gl"], (SUBLANE, LANE))

    def prep(it, carry):
        for wv in range(waves):
            prep_group((it * waves + wv) * group)
        return carry

    waves = math.gcd(DELTA_WAVES, nchunk // group)
    lax.fori_loop(0, nchunk // (group * waves), prep, 0)

    def scan(i, carry):
        out = []
        for rev, S in enumerate(carry):
            j = (nchunk - 1 - i) if rev else i
            y = _dot(mp_s[rev, j], S.astype(bf16))
            nr = nr_s[rev, j]
            o_s[rev, pl.ds(pl.multiple_of(j * C, C), C), :] = y[A_DK:, :] + nr[A_DK:, :]
            out.append(S * gl_s[rev, j][0:1, :] + y[0:A_DK, :] + nr[0:A_DK, :])
        return tuple(out)

    S0 = jnp.zeros((A_DK, A_DV), f32)
    lax.fori_loop(0, nchunk, scan, (S0, S0))

    def finish(it, carry):
        for gi in range(group):
            r0 = pl.multiple_of((it * group + gi) * C, C)
            o = o_s[0, pl.ds(r0, C), :] + o_s[1, pl.ds(r0, C), :]
            zg = x_ref[0, pl.ds(r0, C), nqkv:nqkv + A_DV].astype(f32)
            o_ref[0, pl.ds(r0, C), :] = (_rms(o, on_ref[...]) * (zg * _sigmoid(zg))).astype(o_ref.dtype)
        return carry

    lax.fori_loop(0, nchunk // group, finish, 0)


def _deltanet(qkvz, abh, conv_w, gate_sc, onorm):
    B, L, _ = qkvz.shape
    C = DELTA_CHUNK
    nchunk = L // C
    hw = 4 * A_DK
    scratch = [pltpu.VMEM((2, nchunk, A_DK + C, A_DV), bf16),
               pltpu.VMEM((2, nchunk, A_DK + C, A_DV), f32),
               pltpu.VMEM((2, nchunk, SUBLANE, LANE), f32),
               pltpu.VMEM((2, L, A_DV), f32)]
    return pl.pallas_call(
        _deltanet_kernel,
        grid=(B, A_HEADS),
        in_specs=[pl.BlockSpec(memory_space=pltpu.SMEM),
                  pl.BlockSpec((1, L, hw), lambda b, h: (b, 0, h)),
                  pl.BlockSpec((1, L, LANE), lambda b, h: (b, 0, h)),
                  pl.BlockSpec((1, SUBLANE, 3 * A_DK), lambda b, h: (h, 0, 0)),
                  pl.BlockSpec((1, A_DV), lambda b, h: (0, 0)),
                  pl.BlockSpec((N_MASKS, C, C), lambda b, h: (0, 0, 0)),
                  pl.BlockSpec((N_MASKS, C, C), lambda b, h: (0, 0, 0))],
        out_specs=pl.BlockSpec((1, L, A_DV), lambda b, h: (b, 0, h)),
        out_shape=jax.ShapeDtypeStruct((B, L, A_WIDTH), bf16),
        scratch_shapes=scratch,
        compiler_params=_cparams("parallel", "parallel"),
        name="deltanet",
    )(gate_sc, qkvz, abh, conv_w, onorm.reshape(1, A_DV), *_delta_masks())


def _filter_spectrum_kernel(c_ref, s_ref, hs_ref, hd_ref, kr_ref, ki_ref):
    L = c_ref.shape[1]
    rows = kr_ref.shape[1]
    scale = 1.0 / (2 * L)

    def split(x):
        hi = x.astype(bf16)
        return hi, (x - hi.astype(f32)).astype(bf16)

    hs_hi, hs_lo = split(hs_ref[...])
    hd_hi, hd_lo = split(hd_ref[...])
    for k1 in range(kr_ref.shape[0]):
        c = c_ref[k1 * rows:(k1 + 1) * rows, :]
        s = s_ref[k1 * rows:(k1 + 1) * rows, :]
        kr_ref[k1] = (_dot(c, hs_hi) + _dot(c, hs_lo)) * scale
        ki_ref[k1] = (_dot(s, hd_hi) + _dot(s, hd_lo)) * (-scale)


def _filter_spectrum(hsum, hdiff):
    L, n = hsum.shape
    ct = HY_CT
    n_fft = 2 * L
    n2 = L // FFT_Z1
    d = jnp.arange(L, dtype=jnp.int32)[None, :]
    a1 = ((jnp.arange(FFT_K1, dtype=jnp.int32)[:, None] * d) % n_fft).astype(f32) * (2.0 * math.pi / n_fft)
    a2 = ((jnp.arange(n2, dtype=jnp.int32)[:, None] * d) % n2).astype(f32) * (2.0 * math.pi / n2)
    c1, s1 = jnp.cos(a1)[:, None, :], jnp.sin(a1)[:, None, :]
    c2, s2 = jnp.cos(a2)[None, :, :], jnp.sin(a2)[None, :, :]
    cmat = (c1 * c2 - s1 * s2).reshape(FFT_K1 * n2, L).astype(bf16)
    smat = (s1 * c2 + c1 * s2).reshape(FFT_K1 * n2, L).astype(bf16)
    once = pl.Buffered(1)
    return pl.pallas_call(
        _filter_spectrum_kernel,
        grid=(n // ct,),
        in_specs=[pl.BlockSpec((FFT_K1 * n2, L), lambda j: (0, 0), pipeline_mode=once),
                  pl.BlockSpec((FFT_K1 * n2, L), lambda j: (0, 0), pipeline_mode=once),
                  pl.BlockSpec((L, ct), lambda j: (0, j)),
                  pl.BlockSpec((L, ct), lambda j: (0, j))],
        out_specs=[pl.BlockSpec((FFT_K1, n2, ct), lambda j: (0, 0, j)),
                   pl.BlockSpec((FFT_K1, n2, ct), lambda j: (0, 0, j))],
        out_shape=[jax.ShapeDtypeStruct((FFT_K1, n2, n), f32),
                   jax.ShapeDtypeStruct((FFT_K1, n2, n), f32)],
        compiler_params=_cparams("parallel"),
        name="filter_spectrum",
    )(cmat, smat, hsum, hdiff)


FFT_N1 = 16
FFT_K1 = FFT_N1 // 2 + 1
FFT_Z1 = FFT_N1 // 2
FFT_ROWS = 32


def _lincomb(terms):
    groups = {}
    for c, a in terms:
        if abs(c) > 1e-9:
            groups.setdefault(round(abs(c), 9), []).append((c > 0, a))
    total = None
    for mag, items in groups.items():
        pos = [a for p, a in items if p]
        neg = [a for p, a in items if not p]
        s = None
        for a in pos:
            s = a if s is None else s + a
        for a in neg:
            s = -a if s is None else s - a
        if mag != 1.0:
            s = s * mag
        total = s if total is None else total + s
    return total


def _hyena_fft_kernel(m2f_ref, m2i_ref, twr_ref, twi_ref, x1_ref, x2_ref, v_ref, w1_ref, w2_ref, wv_ref,
                      kr0_ref, ki0_ref, kr1_ref, ki1_ref, db_ref, o_ref, z_s, g_s, d_s, y_s, b_s):
    L, ct = z_s.shape
    n2 = L // FFT_Z1
    R = min(HY_ROWS, L)
    nt = L // R
    r1 = min(FFT_ROWS, n2)
    halo = SUBLANE_BF16
    rowi = lax.broadcasted_iota(jnp.int32, (R, 1), 0)
    ang = 2.0 * math.pi / FFT_N1
    c16 = [[math.cos(ang * a * b) for b in range(FFT_K1)] for a in range(FFT_Z1)]
    s16 = [[math.sin(ang * a * b) for b in range(FFT_K1)] for a in range(FFT_Z1)]

    def conv_into(dst, x_ref, w_ref):
        def body(t, carry):
            r0 = pl.multiple_of(t * R, R)
            x = x_ref[0, pl.ds(r0, R), :].astype(f32)
            rp = pl.multiple_of(jnp.maximum(r0 - halo, 0), halo)
            rn = pl.multiple_of(jnp.minimum(r0 + R, L - halo), halo)
            prow = jnp.where(t > 0, x_ref[0, pl.ds(rp, halo), :][halo - 1:halo, :].astype(f32), 0.0)
            nrow = jnp.where(t < nt - 1, x_ref[0, pl.ds(rn, halo), :][0:1, :].astype(f32), 0.0)
            prev = jnp.where(rowi == 0, prow, pltpu.roll(x, 1, 0))
            nxt = jnp.where(rowi == R - 1, nrow, pltpu.roll(x, R - 1, 0))
            w = w_ref[...]
            dst[pl.ds(r0, R), :] = prev * w[0:1, :] + x * w[1:2, :] + nxt * w[2:3, :]
            return carry
        lax.fori_loop(0, nt, body, 0)

    def cols(k1):
        return slice(k1 * ct, (k1 + 1) * ct)

    conv_into(z_s, v_ref, wv_ref)
    stages = ((x1_ref, w1_ref, kr0_ref, ki0_ref), (x2_ref, w2_ref, kr1_ref, ki1_ref))
    for o, (g_ref, gw_ref, kr_ref, ki_ref) in enumerate(stages):
        conv_into(g_s, g_ref, gw_ref)

        def forward1(rt, carry):
            r0 = pl.multiple_of(rt * r1, r1)
            zs = [z_s[pl.ds(a * n2 + r0, r1), :] for a in range(FFT_Z1)]
            for k1 in range(FFT_K1):
                ar = _lincomb([(c16[a][k1], zs[a]) for a in range(FFT_Z1)])
                ai = _lincomb([(-s16[a][k1], zs[a]) for a in range(FFT_Z1)])
                twr = twr_ref[pl.ds(r0, r1), cols(k1)]
                twi = twi_ref[pl.ds(r0, r1), cols(k1)]
                if ai is None:
                    pr, pi = ar * twr, -(ar * twi)
                else:
                    pr, pi = ar * twr + ai * twi, ai * twr - ar * twi
                d_s[pl.ds(r0, r1), cols(k1)] = pr.astype(bf16)
                d_s[pl.ds(n2 + r0, r1), cols(k1)] = pi.astype(bf16)
            return carry

        lax.fori_loop(0, n2 // r1, forward1, 0)

        for k1 in range(FFT_K1):
            xk = _dot(m2f_ref[...], d_s[:, cols(k1)])
            xr, xi = xk[0:n2, :], xk[n2:, :]
            kr, ki = kr_ref[k1], ki_ref[k1]
            y_s[0:n2, cols(k1)] = (xr * kr - xi * ki).astype(bf16)
            y_s[n2:2 * n2, cols(k1)] = (xr * ki + xi * kr).astype(bf16)

        for k1 in range(FFT_K1):
            bk = _dot(m2i_ref[...], y_s[:, cols(k1)])
            br, bi = bk[0:n2, :], bk[n2:, :]
            twr, twi = twr_ref[:, cols(k1)], twi_ref[:, cols(k1)]
            b_s[0:n2, cols(k1)] = br * twr - bi * twi
            b_s[n2:2 * n2, cols(k1)] = br * twi + bi * twr

        def inverse1(rt, carry):
            r0 = pl.multiple_of(rt * r1, r1)
            brs = [b_s[pl.ds(r0, r1), cols(j)] for j in range(FFT_K1)]
            bis = [b_s[pl.ds(n2 + r0, r1), cols(j)] for j in range(FFT_K1)]
            for a in range(FFT_Z1):
                terms = [(1.0, brs[0]), (-1.0 if a % 2 else 1.0, brs[FFT_K1 - 1])]
                for j in range(1, FFT_K1 - 1):
                    terms += [(2.0 * c16[a][j], brs[j]), (-2.0 * s16[a][j], bis[j])]
                rows = pl.ds(a * n2 + r0, r1)
                z = g_s[rows, :] * (_lincomb(terms) + z_s[rows, :] * db_ref[o:o + 1, :])
                if o + 1 < len(stages):
                    z_s[rows, :] = z
                else:
                    o_ref[0, rows, :] = z.astype(o_ref.dtype)
            return carry

        lax.fori_loop(0, n2 // r1, inverse1, 0)


def _hyena_tables(L):
    n_fft = 2 * L
    n2 = L // FFT_Z1
    ct = HY_CT
    idx = jnp.arange(n2, dtype=jnp.int32)
    a2 = ((idx[:, None] * idx[None, :]) % n2).astype(f32) * (2.0 * math.pi / n2)
    c2, s2 = jnp.cos(a2), jnp.sin(a2)
    m2f = jnp.block([[c2, s2], [-s2, c2]]).astype(bf16)
    m2i = jnp.block([[c2, -s2], [s2, c2]]).astype(bf16)
    at = (idx[:, None] * jnp.arange(FFT_K1, dtype=jnp.int32)[None, :]).astype(f32) * (2.0 * math.pi / n_fft)
    twr = jnp.broadcast_to(jnp.cos(at)[:, :, None], (n2, FFT_K1, ct)).reshape(n2, FFT_K1 * ct)
    twi = jnp.broadcast_to(jnp.sin(at)[:, :, None], (n2, FFT_K1, ct)).reshape(n2, FFT_K1 * ct)
    return m2f, m2i, twr, twi


def _hyena_fft(hx, conv_w, tables, kr, ki, dbias):
    B, L, _ = hx.shape
    ct = HY_CT
    nct = B_WIDTH // ct
    n2 = L // FFT_Z1
    wide = FFT_K1 * ct
    once = pl.Buffered(1)
    m2f, m2i, twr, twi = tables

    def xspec(part):
        return pl.BlockSpec((1, L, ct), lambda c, b: (b, 0, part * nct + c))

    def wspec(part):
        return pl.BlockSpec((SUBLANE, ct), lambda c, b: (0, part * nct + c))

    def kspec(order):
        return pl.BlockSpec((FFT_K1, n2, ct), lambda c, b: (0, 0, order * nct + c), pipeline_mode=once)

    def const(shape):
        return pl.BlockSpec(shape, lambda c, b: (0, 0), pipeline_mode=once)

    return pl.pallas_call(
        _hyena_fft_kernel,
        grid=(nct, B),
        in_specs=[const((2 * n2, 2 * n2)), const((2 * n2, 2 * n2)), const((n2, wide)), const((n2, wide)),
                  xspec(0), xspec(1), xspec(2), wspec(0), wspec(1), wspec(2),
                  kspec(0), kspec(0), kspec(1), kspec(1),
                  pl.BlockSpec((SUBLANE, ct), lambda c, b: (0, c))],
        out_specs=pl.BlockSpec((1, L, ct), lambda c, b: (b, 0, c)),
        out_shape=jax.ShapeDtypeStruct((B, L, B_WIDTH), bf16),
        scratch_shapes=[pltpu.VMEM((L, ct), f32), pltpu.VMEM((L, ct), f32),
                        pltpu.VMEM((2 * n2, wide), bf16), pltpu.VMEM((2 * n2, wide), bf16),
                        pltpu.VMEM((2 * n2, wide), f32)],
        compiler_params=_cparams("parallel", "parallel"),
        name="hyena_fft",
    )(m2f, m2i, twr, twi, hx, hx, hx, conv_w, conv_w, conv_w, kr, ki, kr, ki, dbias)


RT_E1, RT_E2, RT_W1, RT_W2, RT_R1, RT_R2 = 0, 1, 2, 3, 4, 5
ROUTE_TM = ROW_TM
PROJ_PARTS = 2
COMBINE_PARTS = 4
EXPERT_TM = 512
SC_WINDOW = 128


def _pack_bf16_pairs(x):
    n = x.shape[1] // 2
    bits = lax.bitcast_convert_type(x.astype(bf16).astype(f32), jnp.uint32)
    return bits[:, :n] | (bits[:, n:] >> 16)


def _unpack_bf16_pairs(p):
    hi = lax.bitcast_convert_type(p & jnp.uint32(0xFFFF0000), f32)
    lo = lax.bitcast_convert_type(p << 16, f32)
    return hi, lo


ROUTER_ROWS = 40


def _route_topk(lt):
    row = lax.broadcasted_iota(jnp.int32, lt.shape, 0)
    neg = -jnp.inf
    big = ROUTER_LANES
    gl = jnp.where(row < N_GROUPS, lt, neg)
    gmax = jnp.max(gl, axis=0, keepdims=True)
    gsel = jnp.min(jnp.where(gl == gmax, row, big), axis=0, keepdims=True)
    psel = 1.0 / jnp.sum(jnp.exp(gl - gmax), axis=0, keepdims=True)
    first = N_GROUPS + gsel * EXPERTS_PER_GROUP
    el = jnp.where((row >= first) & (row < first + EXPERTS_PER_GROUP), lt, neg)
    t1 = jnp.max(el, axis=0, keepdims=True)
    i1 = jnp.min(jnp.where(el == t1, row, big), axis=0, keepdims=True)
    el2 = jnp.where(row == i1, neg, el)
    t2 = jnp.max(el2, axis=0, keepdims=True)
    i2 = jnp.min(jnp.where(el2 == t2, row, big), axis=0, keepdims=True)
    ex = jnp.exp(t2 - t1)
    w1 = psel / (1.0 + ex)
    w2 = ex * w1
    return i1 - N_GROUPS, i2 - N_GROUPS, w1, w2


def _proj_route_kernel(x_ref, *refs, n_in):
    a_refs = refs[0:n_in]
    w_refs = refs[n_in:2 * n_in]
    g_ref, wr_ref, up_ref, xo_ref, hpa_ref, hpb_ref, rt_ref, rtt_ref, cnt_ref = refs[2 * n_in:]
    tm = xo_ref.shape[0]
    rp = tm // PROJ_PARTS
    parts = [slice(p * rp, (p + 1) * rp) for p in range(PROJ_PARTS)]
    xs = []
    for r in parts:
        x = x_ref[r, :]
        for a_ref, w_ref in zip(a_refs, w_refs):
            x = x + _dot(a_ref[r, :], w_ref[...])
        xo_ref[r, :] = x
        xs.append(x)
    hs = [_rms(x, g_ref[...]) for x in xs]
    for r, h in zip(parts, hs):
        hp = _pack_bf16_pairs(h)
        quarter = hp.shape[1] // 2
        hpa_ref[r, :] = hp[:, :quarter]
        hpb_ref[r, :] = hp[:, quarter:]
    tops = []
    for h in hs:
        h_hi = h.astype(bf16)
        h_lo = (h - h_hi.astype(f32)).astype(bf16)
        both = _dot(h_hi, wr_ref[...])
        logits = both[:, :ROUTER_LANES] + both[:, ROUTER_LANES:] + _dot(h_lo, wr_ref[:, :ROUTER_LANES])
        tops.append(_route_topk(logits.T[0:ROUTER_ROWS, :]))
    row = lax.broadcasted_iota(jnp.int32, (ROUTER_LANES, rp), 0)
    onehots = [jnp.where((row == e1) | (row == e2), 1.0, 0.0) for e1, e2, _, _ in tops]
    onehot16 = jnp.concatenate(onehots, axis=1).astype(bf16)
    before = _dot(onehot16, up_ref[...])
    for p, (r, (e1, e2, w1, w2)) in enumerate(zip(parts, tops)):
        bp = before[:, r]
        r1 = jnp.sum(jnp.where(row == e1, bp, 0.0), axis=0, keepdims=True)
        r2 = jnp.sum(jnp.where(row == e2, bp, 0.0), axis=0, keepdims=True)
        rec = jnp.zeros((ROUTER_LANES, rp), f32)
        for ln, val in ((RT_E1, e1.astype(f32)), (RT_E2, e2.astype(f32)), (RT_W1, w1), (RT_W2, w2),
                        (RT_R1, r1), (RT_R2, r2)):
            rec = jnp.where(row == ln, val, rec)
        rtt_ref[:, r] = rec[0:rtt_ref.shape[0], :]
        rt_ref[r, :] = rec.T
    ones = jnp.ones((cnt_ref.shape[1], tm), bf16)
    cnt_ref[0] = _dot_nt(ones, onehot16)


def _proj_route(x, acts, ws, gain, w_router):
    T, D = x.shape
    tm = ROUTE_TM
    n_in = len(acts)
    up = jnp.asarray(np.triu(np.ones((tm, tm), np.float32), 1), dtype=bf16)
    in_specs = [pl.BlockSpec((tm, D), lambda i: (i, 0))]
    in_specs += [pl.BlockSpec((tm, a.shape[1]), lambda i: (i, 0)) for a in acts]
    in_specs += [pl.BlockSpec(w.shape, lambda i: (0, 0)) for w in ws]
    in_specs += [pl.BlockSpec((1, D), lambda i: (0, 0)),
                 pl.BlockSpec((D, 2 * ROUTER_LANES), lambda i: (0, 0)),
                 pl.BlockSpec((tm, tm), lambda i: (0, 0))]
    return pl.pallas_call(
        functools.partial(_proj_route_kernel, n_in=n_in),
        grid=(T // tm,),
        in_specs=in_specs,
        out_specs=[pl.BlockSpec((tm, D), lambda i: (i, 0)),
                   pl.BlockSpec((tm, D // 4), lambda i: (i, 0)),
                   pl.BlockSpec((tm, D // 4), lambda i: (i, 0)),
                   pl.BlockSpec((tm, ROUTER_LANES), lambda i: (i, 0)),
                   pl.BlockSpec((SUBLANE, tm), lambda i: (0, i)),
                   pl.BlockSpec((1, SUBLANE, ROUTER_LANES), lambda i: (i, 0, 0))],
        out_shape=[jax.ShapeDtypeStruct((T, D), f32),
                   jax.ShapeDtypeStruct((T, D // 4), jnp.uint32),
                   jax.ShapeDtypeStruct((T, D // 4), jnp.uint32),
                   jax.ShapeDtypeStruct((T, ROUTER_LANES), f32),
                   jax.ShapeDtypeStruct((SUBLANE, T), f32),
                   jax.ShapeDtypeStruct((T // tm, SUBLANE, ROUTER_LANES), f32)],
        compiler_params=_cparams("parallel"),
        name="proj_route",
    )(x, *acts, *ws, gain.reshape(1, D), w_router, up)


def _moe_rows(T):
    return 2 * T + N_EXPERTS * EXPERT_TM


def _moe_plan(route_t, counts):
    T = route_t.shape[1]
    tm = ROUTE_TM
    cnt = counts[:, 0, :N_EXPERTS].astype(jnp.int32)
    total = jnp.sum(cnt, axis=0)
    padded = ((total + EXPERT_TM - 1) // EXPERT_TM) * EXPERT_TM
    ends = jnp.cumsum(padded)
    tile_base = (ends - padded)[None, :] + jnp.cumsum(cnt, axis=0) - cnt
    base = jnp.broadcast_to(tile_base.T[:, :, None], (N_EXPERTS, T // tm, tm)).reshape(N_EXPERTS, T)
    ids = jnp.arange(N_EXPERTS, dtype=jnp.int32)[:, None]
    dest = []
    for e_row, r_row in ((RT_E1, RT_R1), (RT_E2, RT_R2)):
        e = route_t[e_row].astype(jnp.int32)
        dest.append(jnp.sum(jnp.where(ids == e[None, :], base, 0), axis=0) + route_t[r_row].astype(jnp.int32))
    n_tiles = _moe_rows(T) // EXPERT_TM
    n_used = ends[-1] // EXPERT_TM
    starts = jnp.arange(n_tiles, dtype=jnp.int32) * EXPERT_TM
    tile_expert = jnp.sum((starts[:, None] >= ends[None, :]).astype(jnp.int32), axis=1)
    last_expert = jnp.sum((jnp.maximum(n_used - 1, 0) * EXPERT_TM >= ends).astype(jnp.int32))
    tile_expert = jnp.minimum(tile_expert, last_expert)
    tiles = jnp.arange(n_tiles, dtype=jnp.int32)
    prev_expert = jnp.concatenate([jnp.full((1,), -1, jnp.int32), tile_expert[:-1]])
    seg_first = ((tile_expert != prev_expert) & (tiles < n_used)).astype(jnp.int32)
    seg_slot = (jnp.cumsum(seg_first) - 1) % 2
    seg_end = jnp.sum(jnp.where(ids == tile_expert[None, :], (ends // EXPERT_TM)[:, None], 0), axis=0)
    nxt = jnp.where(seg_end < n_used, tile_expert[jnp.minimum(seg_end, n_tiles - 1)], -1)
    plan = (tile_expert, seg_first, seg_slot.astype(jnp.int32), nxt.astype(jnp.int32),
            n_used.reshape(1).astype(jnp.int32))
    return dest, plan


def _sc_mesh():
    return plsc.VectorSubcoreMesh(core_axis_name="core", subcore_axis_name="subcore")


def _sc_scatter_pair(rows, d0, d1, n_out):
    T, W = rows.shape
    win = SC_WINDOW

    @pl.kernel(out_type=jax.ShapeDtypeStruct((n_out, W), rows.dtype), mesh=_sc_mesh(), name="moe_sc_scatter")
    def scatter(x_hbm, i0_hbm, i1_hbm, o_hbm):
        def body(x_vmem, i0_vmem, i1_vmem):
            pltpu.sync_copy(x_vmem, o_hbm.at[i0_vmem.at[0]])
            pltpu.sync_copy(x_vmem, o_hbm.at[i1_vmem.at[0]])

        pltpu.emit_pipeline(
            body, grid=(T // win,),
            in_specs=[pl.BlockSpec((win, W), index_map=lambda i: (i, 0)),
                      pl.BlockSpec((1, win), index_map=lambda i: (0, i)),
                      pl.BlockSpec((1, win), index_map=lambda i: (0, i))],
            out_specs=[],
            core_axis_name=("core", "subcore"), dimension_semantics=(pltpu.PARALLEL,),
        )(x_hbm, i0_hbm, i1_hbm)

    return scatter(rows, d0.reshape(1, T), d1.reshape(1, T))


def _sc_gather(table, idx):
    n = idx.shape[0]
    W = table.shape[1]
    win = SC_WINDOW

    @pl.kernel(out_type=jax.ShapeDtypeStruct((n, W), table.dtype), mesh=_sc_mesh(), name="moe_sc_gather")
    def gather(t_hbm, i_hbm, o_hbm):
        def body(i_vmem, o_vmem):
            pltpu.sync_copy(t_hbm.at[i_vmem.at[0]], o_vmem)

        pltpu.emit_pipeline(
            body, grid=(n // win,),
            in_specs=[pl.BlockSpec((1, win), index_map=lambda i: (0, i))],
            out_specs=[pl.BlockSpec((win, W), index_map=lambda i: (i, 0))],
            core_axis_name=("core", "subcore"), dimension_semantics=(pltpu.PARALLEL,),
        )(i_hbm, o_hbm)

    return gather(table, idx.reshape(1, n))


def _expert_kernel(te_ref, first_ref, slot_ref, nxt_ref, nu_ref, xa_ref, xb_ref, wg_hbm, wu_hbm, wd_hbm,
                   ya_ref, yb_ref, wg_s, wu_s, wd_s, wg16, wu16, wd16, sem):
    i = pl.program_id(0)
    active = i < nu_ref[0]

    def weight_copies(expert, s):
        return (pltpu.make_async_copy(wg_hbm.at[expert], wg_s.at[s], sem.at[s, 0]),
                pltpu.make_async_copy(wu_hbm.at[expert], wu_s.at[s], sem.at[s, 1]),
                pltpu.make_async_copy(wd_hbm.at[expert], wd_s.at[s], sem.at[s, 2]))

    @pl.when(i == 0)
    def _():
        for c in weight_copies(te_ref[0], slot_ref[0]):
            c.start()

    @pl.when(active & (first_ref[i] == 1))
    def _():
        s = slot_ref[i]
        for c in weight_copies(te_ref[i], s):
            c.wait()

        @pl.when(nxt_ref[i] >= 0)
        def _():
            for c in weight_copies(nxt_ref[i], 1 - s):
                c.start()

        wg16[...] = wg_s[s].astype(bf16)
        wu16[...] = wu_s[s].astype(bf16)
        wd16[...] = wd_s[s].astype(bf16)

    @pl.when(active)
    def _():
        hi, lo = _unpack_bf16_pairs(jnp.concatenate([xa_ref[...], xb_ref[...]], axis=1))
        x = jnp.concatenate([hi, lo], axis=1).astype(bf16)
        gate = _dot(x, wg16[...])
        up = _dot(x, wu16[...])
        hg = (gate * _sigmoid(gate) * up).astype(bf16)
        y = _pack_bf16_pairs(_dot(hg, wd16[...]))
        quarter = y.shape[1] // 2
        ya_ref[...] = y[:, :quarter]
        yb_ref[...] = y[:, quarter:]

    @pl.when(jnp.logical_not(active))
    def _():
        ya_ref[...] = jnp.zeros(ya_ref.shape, ya_ref.dtype)
        yb_ref[...] = jnp.zeros(yb_ref.shape, yb_ref.dtype)


def _experts(plan, layer, xa, xb, wg, wu, wd):
    tile_expert, seg_first, seg_slot, nxt, n_used = plan
    off = layer * N_EXPERTS
    n_rows, quarter = xa.shape
    D = 4 * quarter
    tm = EXPERT_TM
    row_spec = pl.BlockSpec((tm, quarter), lambda i, *_: (i, 0))
    grid_spec = pltpu.PrefetchScalarGridSpec(
        num_scalar_prefetch=5,
        grid=(n_rows // tm,),
        in_specs=[row_spec, row_spec,
                  pl.BlockSpec(memory_space=pl.ANY), pl.BlockSpec(memory_space=pl.ANY), pl.BlockSpec(memory_space=pl.ANY)],
        out_specs=[row_spec, row_spec],
        scratch_shapes=[pltpu.VMEM((2, D, D_EXPERT), wg.dtype), pltpu.VMEM((2, D, D_EXPERT), wu.dtype),
                        pltpu.VMEM((2, D_EXPERT, D), wd.dtype),
                        pltpu.VMEM((D, D_EXPERT), bf16), pltpu.VMEM((D, D_EXPERT), bf16), pltpu.VMEM((D_EXPERT, D), bf16),
                        pltpu.SemaphoreType.DMA((2, 3))])
    return pl.pallas_call(
        _expert_kernel,
        grid_spec=grid_spec,
        out_shape=[jax.ShapeDtypeStruct((n_rows, quarter), jnp.uint32)] * 2,
        compiler_params=_cparams("arbitrary"),
        name="moe_experts",
    )(tile_expert + off, seg_first, seg_slot, jnp.where(nxt >= 0, nxt + off, -1), n_used, xa, xb, wg, wu, wd)


def _combine_kernel(x_ref, rt_ref, g_ref, a1_ref, b1_ref, a2_ref, b2_ref, *rest, final):
    half = x_ref.shape[1] // 2
    rp = x_ref.shape[0] // COMBINE_PARTS
    for p in range(COMBINE_PARTS):
        r = slice(p * rp, (p + 1) * rp)
        h1, l1 = _unpack_bf16_pairs(jnp.concatenate([a1_ref[r, :], b1_ref[r, :]], axis=1))
        h2, l2 = _unpack_bf16_pairs(jnp.concatenate([a2_ref[r, :], b2_ref[r, :]], axis=1))
        w1 = rt_ref[r, RT_W1:RT_W1 + 1]
        w2 = rt_ref[r, RT_W2:RT_W2 + 1]
        out = jnp.concatenate([x_ref[r, 0:half] + w1 * h1 + w2 * h2, x_ref[r, half:] + w1 * l1 + w2 * l2], axis=1)
        if final:
            (o_ref,) = rest
            o_ref[r, :] = _rms(out, g_ref[...])
        else:
            w_ref, o_ref, p_ref = rest
            o_ref[r, :] = out
            p_ref[r, :] = _dot(_rms(out, g_ref[...]).astype(bf16), w_ref[...]).astype(p_ref.dtype)


def _combine(x, route, ga, gb, gain, w_next=None):
    T, D = x.shape
    tm = ROUTE_TM
    n = T // tm
    quarter = D // 4
    final = w_next is None
    slot1 = pl.BlockSpec((tm, quarter), lambda i: (i, 0))
    slot2 = pl.BlockSpec((tm, quarter), lambda i: (i + n, 0))
    row_spec = pl.BlockSpec((tm, D), lambda i: (i, 0))
    in_specs = [row_spec, pl.BlockSpec((tm, ROUTER_LANES), lambda i: (i, 0)), pl.BlockSpec((1, D), lambda i: (0, 0)),
                slot1, slot1, slot2, slot2]
    args = [x, route, gain.reshape(1, D), ga, gb, ga, gb]
    out_specs = [row_spec]
    out_shape = [jax.ShapeDtypeStruct((T, D), f32)]
    if not final:
        n_next = w_next.shape[1]
        in_specs.append(pl.BlockSpec((D, n_next), lambda i: (0, 0)))
        args.append(w_next)
        out_specs.append(pl.BlockSpec((tm, n_next), lambda i: (i, 0)))
        out_shape.append(jax.ShapeDtypeStruct((T, n_next), bf16))
    outs = pl.pallas_call(
        functools.partial(_combine_kernel, final=final),
        grid=(n,),
        in_specs=in_specs,
        out_specs=out_specs,
        out_shape=out_shape,
        compiler_params=_cparams("parallel"),
        name="moe_combine",
    )(*args)
    return outs[0] if final else outs


def _moe(layer, x, hpa, hpb, route, route_t, counts, wg, wu, wd, gain, w_next=None):
    (d0, d1), plan = _moe_plan(route_t, counts)
    n_rows = _moe_rows(x.shape[0])
    xa = _sc_scatter_pair(hpa, d0, d1, n_rows)
    xb = _sc_scatter_pair(hpb, d0, d1, n_rows)
    ya, yb = _experts(plan, layer, xa, xb, wg, wu, wd)
    dcat = jnp.concatenate([d0, d1])
    return _combine(x, route, _sc_gather(ya, dcat), _sc_gather(yb, dcat), gain, w_next)


SWA_LOCKSTEP = 2


def _swa_bias_tables():
    blk = C_BLOCK
    qi = np.arange(blk)[:, None]
    kj = np.arange(3 * blk)[None, :]
    dist = np.abs(blk + qi - kj)
    tabs = np.full((3, C_KV_HEADS, C_GROUP * blk, 3 * blk), -np.inf, np.float32)
    for case in range(3):
        ok = dist <= WINDOW
        if case == 0:
            ok = ok & (kj >= blk)
        if case == 2:
            ok = ok & (kj < 2 * blk)
        for hh in range(C_HEADS):
            slope = 2.0 ** (-8.0 * (hh + 1) / C_HEADS)
            t = np.where(ok, -slope * LOG2E * dist, -np.inf)
            kvh, g = divmod(hh, C_GROUP)
            tabs[case, kvh, g * blk:(g + 1) * blk] = t
    return jnp.asarray(tabs)


def _swa_kernel(sink_ref, q_ref, kp_ref, kc_ref, kn_ref, vp_ref, vc_ref, vn_ref, bias_ref, o_ref):
    blk = C_BLOCK
    hd = C_HEAD_DIM
    kcat = jnp.concatenate([kp_ref[0], kc_ref[0], kn_ref[0]], axis=0)
    vcat = jnp.concatenate([vp_ref[0], vc_ref[0], vn_ref[0]], axis=0)
    q = q_ref[0]
    rblk = lax.broadcasted_iota(jnp.int32, (C_GROUP * blk, 1), 0) // blk
    lane = lax.broadcasted_iota(jnp.int32, (3 * blk, LANE - hd), 1)
    ones_pad = jnp.where(lane == 0, 1.0, 0.0).astype(bf16)
    q4s, sinks = [], []
    for kvh in range(C_KV_HEADS):
        q4s.append(jnp.concatenate(
            [q[:, (kvh * C_GROUP + g) * hd:(kvh * C_GROUP + g + 1) * hd] for g in range(C_GROUP)], axis=0))
        sk = jnp.zeros((C_GROUP * blk, 1), f32)
        for g in range(C_GROUP):
            sk = jnp.where(rblk == g, sink_ref[kvh * C_GROUP + g] * LOG2E, sk)
        sinks.append(sk)
    outs = []
    for h0 in range(0, C_KV_HEADS, SWA_LOCKSTEP):
        hs = range(h0, h0 + SWA_LOCKSTEP)
        ss = [_dot_nt(q4s[h], kcat[:, h * hd:(h + 1) * hd]) + bias_ref[0, h] for h in hs]
        ms = [jnp.maximum(jnp.max(s, axis=-1, keepdims=True), sinks[h]) for s, h in zip(ss, hs)]
        ps = [jnp.exp2(s - m).astype(bf16) for s, m in zip(ss, ms)]
        pvs = [_dot(p, jnp.concatenate([vcat[:, h * hd:(h + 1) * hd], ones_pad], axis=1)) for p, h in zip(ps, hs)]
        for pv, m, h in zip(pvs, ms, hs):
            den = pv[:, hd:hd + 1] + jnp.exp2(sinks[h] - m)
            o4 = pv[:, 0:hd] / den
            outs += [o4[g * blk:(g + 1) * blk, :] for g in range(C_GROUP)]
    o_ref[0] = jnp.concatenate(outs, axis=1).astype(o_ref.dtype)


def _swa(qkv, sinks):
    B, L, _ = qkv.shape
    nb = L // C_BLOCK
    dq = C_HEADS * C_HEAD_DIM
    dkv = C_KV_HEADS * C_HEAD_DIM
    k_col = dq // dkv
    v_col = k_col + 1

    def kv_spec(col, shift):
        return pl.BlockSpec((1, C_BLOCK, dkv),
                            lambda n, b: (b, jnp.clip(n + shift, 0, nb - 1), col))

    def edge_case(n, b):
        return (jnp.where(n == 0, 0, jnp.where(n == nb - 1, 2, 1)), 0, 0, 0)

    return pl.pallas_call(
        _swa_kernel,
        grid=(nb, B),
        in_specs=[pl.BlockSpec(memory_space=pltpu.SMEM),
                  pl.BlockSpec((1, C_BLOCK, dq), lambda n, b: (b, n, 0)),
                  kv_spec(k_col, -1), kv_spec(k_col, 0), kv_spec(k_col, 1),
                  kv_spec(v_col, -1), kv_spec(v_col, 0), kv_spec(v_col, 1),
                  pl.BlockSpec((1, C_KV_HEADS, C_GROUP * C_BLOCK, 3 * C_BLOCK), edge_case)],
        out_specs=pl.BlockSpec((1, C_BLOCK, dq), lambda n, b: (b, n, 0)),
        out_shape=jax.ShapeDtypeStruct((B, L, dq), bf16),
        compiler_params=_cparams("parallel", "parallel"),
        name="swa",
    )(sinks, qkv, qkv, qkv, qkv, qkv, qkv, qkv, _swa_bias_tables())


def _hyena_filters(L, w1, b1, w2, b2, w3, b3, w4, freq):
    hp = lax.Precision.HIGHEST
    t = jnp.linspace(0.0, 1.0, L, dtype=f32)[:, None]
    wpos = (2.0 * math.pi / L) * jnp.arange(L, dtype=f32)[:, None]
    bands = jnp.linspace(1e-4, HY_BANDS - 1, HY_BANDS, dtype=f32)[None, :]
    feats = jnp.concatenate([t, jnp.cos(bands * wpos), -jnp.sin(bands * wpos)], axis=-1)
    h = jnp.sin(freq[0] * (jnp.dot(feats, w1, precision=hp) + b1))
    h = jnp.sin(freq[1] * (jnp.dot(h, w2, precision=hp) + b2))
    h = jnp.sin(freq[2] * (jnp.dot(h, w3, precision=hp) + b3))
    h = jnp.dot(h, w4, precision=hp).reshape(L, HY_ORDER, 2, B_WIDTH)
    deltas = jnp.abs(jnp.linspace(HY_MIN_DECAY, HY_MAX_DECAY, B_WIDTH, dtype=f32))
    window = jnp.exp(-t * deltas[None, :])
    return h * window[:, None, None, :]


def _pad_rows(w):
    return jnp.pad(w, ((0, SUBLANE - w.shape[0]), (0, 0)))


def _even_in_weights(w_in):
    hk = A_HEADS * A_DK
    q = w_in[:, 0:hk].reshape(D_MODEL, A_HEADS, A_DK)
    k = w_in[:, hk:2 * hk].reshape(D_MODEL, A_HEADS, A_DK)
    v = w_in[:, 2 * hk:A_QKV].reshape(D_MODEL, A_HEADS, A_DV)
    z = w_in[:, A_QKV:A_QKV + A_WIDTH].reshape(D_MODEL, A_HEADS, A_DV)
    qkvz = jnp.concatenate([q, k, v, z], axis=-1).reshape(D_MODEL, A_HEADS * 4 * A_DK)
    ab = w_in[:, A_QKV + A_WIDTH:A_QKV + A_WIDTH + 4 * A_HEADS].reshape(D_MODEL, 4, A_HEADS)
    ab = jnp.transpose(ab, (0, 2, 1))
    ab = jnp.pad(ab.reshape(D_MODEL, 4 * A_HEADS), ((0, 0), (0, LANE - 4 * A_HEADS)))
    hx = w_in[:, A_QKV + A_WIDTH + 4 * A_HEADS:]
    return jnp.concatenate([qkvz, ab, hx], axis=1).astype(bf16)


def _router_weights(w_group, w_expert):
    w = jnp.concatenate([w_group, w_expert], axis=1)
    w = jnp.pad(w, ((0, 0), (0, ROUTER_LANES - w.shape[1])))
    hi = w.astype(bf16)
    lo = (w - hi.astype(f32)).astype(bf16)
    return jnp.concatenate([hi, lo], axis=1)


def kernel(x_prompt, x_sample, norm_mix, norm_ffn, norm_final, ev_w_in, ev_conv_a, ev_alog_f, ev_alog_b, ev_dtb_f, ev_dtb_b, ev_onorm, ev_conv_b, hy_w1, hy_b1, hy_w2, hy_b2, hy_w3, hy_b3, hy_w4, hy_freq, hy_dbias, ev_w_out, od_w_qkv, od_sinks, od_w_out, moe_w_group, moe_w_expert, moe_w_gate, moe_w_up, moe_w_down):
    L = x_prompt.shape[1]
    n_layers = moe_w_gate.shape[0]
    wg = moe_w_gate.reshape(n_layers * N_EXPERTS, D_MODEL, D_EXPERT)
    wu = moe_w_up.reshape(n_layers * N_EXPERTS, D_MODEL, D_EXPERT)
    wd = moe_w_down.reshape(n_layers * N_EXPERTS, D_EXPERT, D_MODEL)

    w_in = _even_in_weights(ev_w_in[0])
    n_qkvz = A_HEADS * 4 * A_DK
    n_ab = A_HEADS * LANE
    n_hx = 3 * B_WIDTH
    hk = A_HEADS * A_DK
    ca = ev_conv_a[0]
    conv_a = jnp.concatenate([ca[0:hk].reshape(A_HEADS, A_DK, 3), ca[hk:2 * hk].reshape(A_HEADS, A_DK, 3),
                              ca[2 * hk:].reshape(A_HEADS, A_DV, 3)], axis=1)
    conv_a = jnp.pad(jnp.transpose(conv_a, (0, 2, 1)), ((0, 0), (0, SUBLANE - 3), (0, 0)))
    gate_sc = jnp.stack([ev_alog_f[0], ev_alog_b[0], ev_dtb_f[0], ev_dtb_b[0]], axis=1)
    filt = _hyena_filters(L, hy_w1[0], hy_b1[0], hy_w2[0], hy_b2[0], hy_w3[0], hy_b3[0], hy_w4[0], hy_freq[0])
    hsum = (filt[:, :, 0] + filt[:, :, 1]).reshape(L, HY_ORDER * B_WIDTH)
    hdiff = (filt[:, :, 0] - filt[:, :, 1]).reshape(L, HY_ORDER * B_WIDTH)
    kr, ki = _filter_spectrum(hsum, hdiff)
    fft_tables = _hyena_tables(L)
    conv_b = _pad_rows(ev_conv_b[0].T)
    dbias = _pad_rows(hy_dbias[0])
    w_out = ev_w_out[0].astype(bf16)
    routers = [_router_weights(moe_w_group[l], moe_w_expert[l]) for l in range(n_layers)]
    n_qkv = od_w_qkv.shape[-1]
    dq = C_HEADS * C_HEAD_DIM
    w_qkv = jnp.concatenate([od_w_qkv[0][:, :dq] * (LOG2E * C_HEAD_DIM ** -0.5), od_w_qkv[0][:, dq:]], axis=1)
    w_qkv = w_qkv.astype(bf16)
    w_att_out = od_w_out[0].astype(bf16)

    def trunk(xb):
        B = xb.shape[0]
        T = B * L
        x = xb.reshape(T, D_MODEL)
        qkvz, abh, hx = _even_in_proj(x, norm_mix[0], w_in)
        o_a = _deltanet(qkvz.reshape(B, L, n_qkvz), abh.reshape(B, L, n_ab), conv_a, gate_sc, ev_onorm[0])
        o_b = _hyena_fft(hx.reshape(B, L, n_hx), conv_b, fft_tables, kr, ki, dbias)
        x, hpa, hpb, route, route_t, counts = _proj_route(
            x, (o_a.reshape(T, A_WIDTH), o_b.reshape(T, B_WIDTH)), (w_out[:A_WIDTH], w_out[A_WIDTH:]),
            norm_ffn[0], routers[0])
        x, qkv = _moe(0, x, hpa, hpb, route, route_t, counts, wg, wu, wd, norm_mix[1], w_qkv)
        o_c = _swa(qkv.reshape(B, L, n_qkv), od_sinks[0])
        x, hpa, hpb, route, route_t, counts = _proj_route(
            x, (o_c.reshape(T, C_HEADS * C_HEAD_DIM),), (w_att_out,), norm_ffn[1], routers[1])
        y = _moe(1, x, hpa, hpb, route, route_t, counts, wg, wu, wd, norm_final)
        return y.reshape(B, L, D_MODEL)

    return (trunk(x_prompt), trunk(x_sample))
```

```python
import functools
import math

import jax
import jax.numpy as jnp
import numpy as np
from jax import lax
from jax.experimental import pallas as pl
from jax.experimental.pallas import tpu as pltpu
from jax.experimental.pallas import tpu_sc as plsc

f32 = jnp.float32
bf16 = jnp.bfloat16

EPS = 1e-6
D_MODEL = 1024

A_HEADS = 4
A_DK = 128
A_DV = 128
A_WIDTH = A_HEADS * A_DV
A_QKV = 2 * A_HEADS * A_DK + A_HEADS * A_DV
DELTA_CHUNK = 128
INV_BASE = 8
DELTA_GROUP = 8
DELTA_WAVES = 2

B_WIDTH = D_MODEL - A_WIDTH
HY_ORDER = 2
HY_EMB = 33
HY_BANDS = (HY_EMB - 1) // 2
HY_TARGET = 1e-2
HY_MIN_DECAY = math.log(HY_TARGET) / 1.5
HY_MAX_DECAY = math.log(HY_TARGET) / 0.3
HY_CT = 256
HY_ROWS = 256

C_HEADS = 16
C_KV_HEADS = 4
C_HEAD_DIM = 64
C_GROUP = C_HEADS // C_KV_HEADS
WINDOW = 128
C_BLOCK = 128

N_GROUPS = 4
EXPERTS_PER_GROUP = 8
N_EXPERTS = N_GROUPS * EXPERTS_PER_GROUP
D_EXPERT = 256
ROUTER_LANES = 128

LANE = 128
SUBLANE = 8
SUBLANE_BF16 = 16
ROW_TM = 512
VMEM_LIMIT = 56 * 1024 * 1024
LOG2E = 1.4426950408889634


def _cparams(*sem):
    return pltpu.CompilerParams(dimension_semantics=sem, vmem_limit_bytes=VMEM_LIMIT)


def _dot(a, b):
    return jnp.dot(a, b, preferred_element_type=f32)


def _dot_nt(a, b):
    return lax.dot_general(a, b, (((1,), (1,)), ((), ())), preferred_element_type=f32)


def _dot_tn(a, b):
    return lax.dot_general(a, b, (((0,), (0,)), ((), ())), preferred_element_type=f32)


def _sigmoid(x):
    return 1.0 / (1.0 + jnp.exp(-x))


def _rms(x, g):
    return x * lax.rsqrt(jnp.mean(x * x, axis=-1, keepdims=True) + EPS) * g


def _even_in_kernel(x_ref, g_ref, w_ref, qkvz_ref, ab_ref, hx_ref):
    h = _rms(x_ref[...], g_ref[...]).astype(bf16)
    n_qkvz = qkvz_ref.shape[1]
    qkvz_ref[...] = _dot(h, w_ref[:, 0:n_qkvz]).astype(qkvz_ref.dtype)
    ab = _dot(h, w_ref[:, n_qkvz:n_qkvz + LANE])
    for hd in range(A_HEADS):
        ab_ref[:, hd * LANE:(hd + 1) * LANE] = ab if hd == 0 else pltpu.roll(ab, LANE - 4 * hd, 1)
    hx_ref[...] = _dot(h, w_ref[:, n_qkvz + LANE:]).astype(hx_ref.dtype)


def _even_in_proj(x, gain, w):
    T, D = x.shape
    tm = ROW_TM
    n_qkvz = A_HEADS * 4 * A_DK
    n_ab = A_HEADS * LANE
    n_hx = 3 * B_WIDTH
    return pl.pallas_call(
        _even_in_kernel,
        grid=(T // tm,),
        in_specs=[pl.BlockSpec((tm, D), lambda i: (i, 0)),
                  pl.BlockSpec((1, D), lambda i: (0, 0)),
                  pl.BlockSpec(w.shape, lambda i: (0, 0))],
        out_specs=[pl.BlockSpec((tm, n_qkvz), lambda i: (i, 0)),
                   pl.BlockSpec((tm, n_ab), lambda i: (i, 0)),
                   pl.BlockSpec((tm, n_hx), lambda i: (i, 0))],
        out_shape=[jax.ShapeDtypeStruct((T, n_qkvz), bf16),
                   jax.ShapeDtypeStruct((T, n_ab), f32),
                   jax.ShapeDtypeStruct((T, n_hx), bf16)],
        compiler_params=_cparams("parallel"),
        name="even_in_proj",
    )(x, gain.reshape(1, D), w)


MK_LOWER, MK_UPPER, MK_BASE, MK_LEVEL0 = 0, 1, 2, 3
N_LEVELS = int(math.log2(DELTA_CHUNK // INV_BASE))
MK_EYE = MK_LEVEL0 + N_LEVELS
N_MASKS = MK_EYE + 1


def _delta_masks():
    c = np.arange(DELTA_CHUNK)[:, None]
    s = np.arange(DELTA_CHUNK)[None, :]
    m = [s <= c, s >= c, (c // INV_BASE == s // INV_BASE) & (c != s)]
    b = INV_BASE
    while b < DELTA_CHUNK:
        m.append((c // (2 * b) == s // (2 * b)) & (c // b != s // b))
        b *= 2
    m.append(c == s)
    m = np.stack(m).astype(np.float32)
    return jnp.asarray(m), jnp.asarray(m, dtype=bf16)


def _tri_inverse(mats, mk_ref, mk16_ref):
    ps = [a * mk_ref[MK_BASE] for a in mats]
    ts = [mk_ref[MK_EYE] - p for p in ps]
    n = 2
    while n < INV_BASE:
        p16s = [p.astype(bf16) for p in ps]
        ps = [_dot(p16, p16) for p16 in p16s]
        ts = [t + _dot(t.astype(bf16), p.astype(bf16)) for t, p in zip(ts, ps)]
        n *= 2
    a16s = [a.astype(bf16) for a in mats]
    t16s = [t.astype(bf16) for t in ts]
    for lvl in range(N_LEVELS):
        xs = [_dot(a16 * mk16_ref[MK_LEVEL0 + lvl], t16) for a16, t16 in zip(a16s, t16s)]
        t16s = [t16 - _dot(t16, x.astype(bf16)).astype(bf16) for t16, x in zip(t16s, xs)]
    return t16s


def _deltanet_kernel(sc_ref, x_ref, ab_ref, cw_ref, on_ref, mk_ref, mk16_ref, o_ref, mp_s, nr_s, gl_s, o_s):
    hd = pl.program_id(1)
    L = x_ref.shape[1]
    C = DELTA_CHUNK
    nchunk = L // C
    group = math.gcd(DELTA_GROUP, nchunk)
    nqkv = 3 * A_DK
    halo = SUBLANE_BF16

    lane = lax.broadcasted_iota(jnp.int32, (1, LANE), 1)
    alog = jnp.where(lane == 2, sc_ref[hd, 0], jnp.where(lane == 3, sc_ref[hd, 1], 0.0))
    dtb = jnp.where(lane == 2, sc_ref[hd, 2], jnp.where(lane == 3, sc_ref[hd, 3], 0.0))
    neg_a = jnp.where((lane == 2) | (lane == 3), -jnp.exp(alog), 0.0)
    rowi = lax.broadcasted_iota(jnp.int32, (C, 1), 0)
    cw = cw_ref[0]

    def chunk_inputs(i):
        r0 = pl.multiple_of(i * C, C)
        xc = x_ref[0, pl.ds(r0, C), 0:nqkv].astype(f32)
        rp = pl.multiple_of(jnp.maximum(r0 - halo, 0), halo)
        rn = pl.multiple_of(jnp.minimum(r0 + C, L - halo), halo)
        prow = x_ref[0, pl.ds(rp, halo), 0:nqkv][halo - 1:halo, :].astype(f32)
        nrow = x_ref[0, pl.ds(rn, halo), 0:nqkv][0:1, :].astype(f32)
        prow = jnp.where(i > 0, prow, 0.0)
        nrow = jnp.where(i < nchunk - 1, nrow, 0.0)
        prev = jnp.where(rowi == 0, prow, pltpu.roll(xc, 1, 0))
        nxt = jnp.where(rowi == C - 1, nrow, pltpu.roll(xc, C - 1, 0))
        y = prev * cw[0:1, :] + xc * cw[1:2, :] + nxt * cw[2:3, :]
        y = y * _sigmoid(y)
        q = y[:, 0:A_DK]
        k = y[:, A_DK:2 * A_DK]
        vc = y[:, 2 * A_DK:nqkv]
        qc = q * lax.rsqrt(jnp.sum(q * q, axis=-1, keepdims=True) + EPS) * (A_DK ** -0.5)
        kc = k * lax.rsqrt(jnp.sum(k * k, axis=-1, keepdims=True) + EPS)

        abc = ab_ref[0, pl.ds(r0, C), :]
        zg = abc + dtb
        g = neg_a * (jnp.maximum(zg, 0.0) + jnp.log(1.0 + jnp.exp(-jnp.abs(zg))))
        low16 = mk16_ref[MK_LOWER]
        g1 = g.astype(bf16)
        e1 = g - g1.astype(f32)
        g2 = e1.astype(bf16)
        g3 = (e1 - g2.astype(f32)).astype(bf16)
        pre = _dot(low16, g1) + _dot(low16, g2) + _dot(low16, g3)
        tot = pre[C - 1:C, :]
        suf = tot - pre + g
        beta_all = _sigmoid(abc)

        k16 = kc.astype(bf16)
        kk = _dot_nt(k16, k16)
        qk = _dot_nt(qc.astype(bf16), k16)

        chains = []
        for rev in (0, 1):
            beta = beta_all[:, rev:rev + 1]
            gcc = (suf if rev else pre)[:, 2 + rev:3 + rev]
            tot11 = tot[:, 2 + rev:3 + rev]
            keep = mk_ref[MK_UPPER if rev else MK_LOWER]
            bc = jnp.broadcast_to(gcc * LOG2E, (C, C))
            decay = jnp.exp2((bc - bc.T) * keep) * keep
            a = (beta * kk) * decay
            eg = jnp.exp(gcc)
            chains.append(dict(
                i=i, rev=rev, a=a, gl=jnp.exp(tot11), qd=qc * eg,
                rhs=jnp.concatenate([vc * beta, kc * (beta * eg)], axis=1).astype(bf16),
                kd16=(kc * jnp.exp(tot11 - gcc)).astype(bf16),
                qk16=(qk * decay).astype(bf16)))
        return chains

    def prep_group(first_chunk):
        chains = []
        for gi in range(group):
            chains += chunk_inputs(first_chunk + gi)
        t16s = _tri_inverse([c["a"] for c in chains], mk_ref, mk16_ref)
        uws = [_dot(t16, c["rhs"]).astype(bf16) for t16, c in zip(t16s, chains)]
        kts = [_dot_tn(c["kd16"], uw) for c, uw in zip(chains, uws)]
        prs = [_dot(c["qk16"], uw) for c, uw in zip(chains, uws)]
        for c, kt, pr in zip(chains, kts, prs):
            rev, i = c["rev"], c["i"]
            mp_s[rev, i, 0:A_DK, :] = (-kt[:, A_DV:]).astype(bf16)
            mp_s[rev, i, A_DK:A_DK + C, :] = (c["qd"] - pr[:, A_DV:]).astype(bf16)
            nr_s[rev, i, 0:A_DK, :] = kt[:, 0:A_DV]
            nr_s[rev, i, A_DK:A_DK + C, :] = pr[:, 0:A_DV]
            gl_s[rev, i] = jnp.broadcast_to(c["gl"], (SUBLANE, LANE))

    def prep(it, carry):
        for wv in range(waves):
            prep_group((it * waves + wv) * group)
        return carry

    waves = math.gcd(DELTA_WAVES, nchunk // group)
    lax.fori_loop(0, nchunk // (group * waves), prep, 0)

    def scan(i, carry):
        out = []
        for rev, S in enumerate(carry):
            j = (nchunk - 1 - i) if rev else i
            y = _dot(mp_s[rev, j], S.astype(bf16))
            nr = nr_s[rev, j]
            o_s[rev, pl.ds(pl.multiple_of(j * C, C), C), :] = y[A_DK:, :] + nr[A_DK:, :]
            out.append(S * gl_s[rev, j][0:1, :] + y[0:A_DK, :] + nr[0:A_DK, :])
        return tuple(out)

    S0 = jnp.zeros((A_DK, A_DV), f32)
    lax.fori_loop(0, nchunk, scan, (S0, S0))

    def finish(it, carry):
        for gi in range(group):
            r0 = pl.multiple_of((it * group + gi) * C, C)
            o = o_s[0, pl.ds(r0, C), :] + o_s[1, pl.ds(r0, C), :]
            zg = x_ref[0, pl.ds(r0, C), nqkv:nqkv + A_DV].astype(f32)
            o_ref[0, pl.ds(r0, C), :] = (_rms(o, on_ref[...]) * (zg * _sigmoid(zg))).astype(o_ref.dtype)
        return carry

    lax.fori_loop(0, nchunk // group, finish, 0)


def _deltanet(qkvz, abh, conv_w, gate_sc, onorm):
    B, L, _ = qkvz.shape
    C = DELTA_CHUNK
    nchunk = L // C
    hw = 4 * A_DK
    scratch = [pltpu.VMEM((2, nchunk, A_DK + C, A_DV), bf16),
               pltpu.VMEM((2, nchunk, A_DK + C, A_DV), f32),
               pltpu.VMEM((2, nchunk, SUBLANE, LANE), f32),
               pltpu.VMEM((2, L, A_DV), f32)]
    return pl.pallas_call(
        _deltanet_kernel,
        grid=(B, A_HEADS),
        in_specs=[pl.BlockSpec(memory_space=pltpu.SMEM),
                  pl.BlockSpec((1, L, hw), lambda b, h: (b, 0, h)),
                  pl.BlockSpec((1, L, LANE), lambda b, h: (b, 0, h)),
                  pl.BlockSpec((1, SUBLANE, 3 * A_DK), lambda b, h: (h, 0, 0)),
                  pl.BlockSpec((1, A_DV), lambda b, h: (0, 0)),
                  pl.BlockSpec((N_MASKS, C, C), lambda b, h: (0, 0, 0)),
                  pl.BlockSpec((N_MASKS, C, C), lambda b, h: (0, 0, 0))],
        out_specs=pl.BlockSpec((1, L, A_DV), lambda b, h: (b, 0, h)),
        out_shape=jax.ShapeDtypeStruct((B, L, A_WIDTH), bf16),
        scratch_shapes=scratch,
        compiler_params=_cparams("parallel", "parallel"),
        name="deltanet",
    )(gate_sc, qkvz, abh, conv_w, onorm.reshape(1, A_DV), *_delta_masks())


def _filter_spectrum_kernel(c_ref, s_ref, hs_ref, hd_ref, kr_ref, ki_ref):
    L = c_ref.shape[1]
    rows = kr_ref.shape[1]
    scale = 1.0 / (2 * L)

    def split(x):
        hi = x.astype(bf16)
        return hi, (x - hi.astype(f32)).astype(bf16)

    hs_hi, hs_lo = split(hs_ref[...])
    hd_hi, hd_lo = split(hd_ref[...])
    for k1 in range(kr_ref.shape[0]):
        c = c_ref[k1 * rows:(k1 + 1) * rows, :]
        s = s_ref[k1 * rows:(k1 + 1) * rows, :]
        kr_ref[k1] = (_dot(c, hs_hi) + _dot(c, hs_lo)) * scale
        ki_ref[k1] = (_dot(s, hd_hi) + _dot(s, hd_lo)) * (-scale)


def _filter_spectrum(hsum, hdiff):
    L, n = hsum.shape
    ct = HY_CT
    n_fft = 2 * L
    n2 = L // FFT_Z1
    d = jnp.arange(L, dtype=jnp.int32)[None, :]
    a1 = ((jnp.arange(FFT_K1, dtype=jnp.int32)[:, None] * d) % n_fft).astype(f32) * (2.0 * math.pi / n_fft)
    a2 = ((jnp.arange(n2, dtype=jnp.int32)[:, None] * d) % n2).astype(f32) * (2.0 * math.pi / n2)
    c1, s1 = jnp.cos(a1)[:, None, :], jnp.sin(a1)[:, None, :]
    c2, s2 = jnp.cos(a2)[None, :, :], jnp.sin(a2)[None, :, :]
    cmat = (c1 * c2 - s1 * s2).reshape(FFT_K1 * n2, L).astype(bf16)
    smat = (s1 * c2 + c1 * s2).reshape(FFT_K1 * n2, L).astype(bf16)
    once = pl.Buffered(1)
    return pl.pallas_call(
        _filter_spectrum_kernel,
        grid=(n // ct,),
        in_specs=[pl.BlockSpec((FFT_K1 * n2, L), lambda j: (0, 0), pipeline_mode=once),
                  pl.BlockSpec((FFT_K1 * n2, L), lambda j: (0, 0), pipeline_mode=once),
                  pl.BlockSpec((L, ct), lambda j: (0, j)),
                  pl.BlockSpec((L, ct), lambda j: (0, j))],
        out_specs=[pl.BlockSpec((FFT_K1, n2, ct), lambda j: (0, 0, j)),
                   pl.BlockSpec((FFT_K1, n2, ct), lambda j: (0, 0, j))],
        out_shape=[jax.ShapeDtypeStruct((FFT_K1, n2, n), f32),
                   jax.ShapeDtypeStruct((FFT_K1, n2, n), f32)],
        compiler_params=_cparams("parallel"),
        name="filter_spectrum",
    )(cmat, smat, hsum, hdiff)


FFT_N1 = 16
FFT_K1 = FFT_N1 // 2 + 1
FFT_Z1 = FFT_N1 // 2
FFT_ROWS = 32


def _lincomb(terms):
    groups = {}
    for c, a in terms:
        if abs(c) > 1e-9:
            groups.setdefault(round(abs(c), 9), []).append((c > 0, a))
    total = None
    for mag, items in groups.items():
        pos = [a for p, a in items if p]
        neg = [a for p, a in items if not p]
        s = None
        for a in pos:
            s = a if s is None else s + a
        for a in neg:
            s = -a if s is None else s - a
        if mag != 1.0:
            s = s * mag
        total = s if total is None else total + s
    return total


def _hyena_fft_kernel(m2f_ref, m2i_ref, twr_ref, twi_ref, x1_ref, x2_ref, v_ref, w1_ref, w2_ref, wv_ref,
                      kr0_ref, ki0_ref, kr1_ref, ki1_ref, db_ref, o_ref, z_s, g_s, d_s, y_s, b_s):
    L, ct = z_s.shape
    n2 = L // FFT_Z1
    R = min(HY_ROWS, L)
    nt = L // R
    r1 = min(FFT_ROWS, n2)
    halo = SUBLANE_BF16
    rowi = lax.broadcasted_iota(jnp.int32, (R, 1), 0)
    ang = 2.0 * math.pi / FFT_N1
    c16 = [[math.cos(ang * a * b) for b in range(FFT_K1)] for a in range(FFT_Z1)]
    s16 = [[math.sin(ang * a * b) for b in range(FFT_K1)] for a in range(FFT_Z1)]

    def conv_into(dst, x_ref, w_ref):
        def body(t, carry):
            r0 = pl.multiple_of(t * R, R)
            x = x_ref[0, pl.ds(r0, R), :].astype(f32)
            rp = pl.multiple_of(jnp.maximum(r0 - halo, 0), halo)
            rn = pl.multiple_of(jnp.minimum(r0 + R, L - halo), halo)
            prow = jnp.where(t > 0, x_ref[0, pl.ds(rp, halo), :][halo - 1:halo, :].astype(f32), 0.0)
            nrow = jnp.where(t < nt - 1, x_ref[0, pl.ds(rn, halo), :][0:1, :].astype(f32), 0.0)
            prev = jnp.where(rowi == 0, prow, pltpu.roll(x, 1, 0))
            nxt = jnp.where(rowi == R - 1, nrow, pltpu.roll(x, R - 1, 0))
            w = w_ref[...]
            dst[pl.ds(r0, R), :] = prev * w[0:1, :] + x * w[1:2, :] + nxt * w[2:3, :]
            return carry
        lax.fori_loop(0, nt, body, 0)

    def cols(k1):
        return slice(k1 * ct, (k1 + 1) * ct)

    conv_into(z_s, v_ref, wv_ref)
    stages = ((x1_ref, w1_ref, kr0_ref, ki0_ref), (x2_ref, w2_ref, kr1_ref, ki1_ref))
    for o, (g_ref, gw_ref, kr_ref, ki_ref) in enumerate(stages):
        conv_into(g_s, g_ref, gw_ref)

        def forward1(rt, carry):
            r0 = pl.multiple_of(rt * r1, r1)
            zs = [z_s[pl.ds(a * n2 + r0, r1), :] for a in range(FFT_Z1)]
            for k1 in range(FFT_K1):
                ar = _lincomb([(c16[a][k1], zs[a]) for a in range(FFT_Z1)])
                ai = _lincomb([(-s16[a][k1], zs[a]) for a in range(FFT_Z1)])
                twr = twr_ref[pl.ds(r0, r1), cols(k1)]
                twi = twi_ref[pl.ds(r0, r1), cols(k1)]
                if ai is None:
                    pr, pi = ar * twr, -(ar * twi)
                else:
                    pr, pi = ar * twr + ai * twi, ai * twr - ar * twi
                d_s[pl.ds(r0, r1), cols(k1)] = pr.astype(bf16)
                d_s[pl.ds(n2 + r0, r1), cols(k1)] = pi.astype(bf16)
            return carry

        lax.fori_loop(0, n2 // r1, forward1, 0)

        for k1 in range(FFT_K1):
            xk = _dot(m2f_ref[...], d_s[:, cols(k1)])
            xr, xi = xk[0:n2, :], xk[n2:, :]
            kr, ki = kr_ref[k1], ki_ref[k1]
            y_s[0:n2, cols(k1)] = (xr * kr - xi * ki).astype(bf16)
            y_s[n2:2 * n2, cols(k1)] = (xr * ki + xi * kr).astype(bf16)

        for k1 in range(FFT_K1):
            bk = _dot(m2i_ref[...], y_s[:, cols(k1)])
            br, bi = bk[0:n2, :], bk[n2:, :]
            twr, twi = twr_ref[:, cols(k1)], twi_ref[:, cols(k1)]
            b_s[0:n2, cols(k1)] = br * twr - bi * twi
            b_s[n2:2 * n2, cols(k1)] = br * twi + bi * twr

        def inverse1(rt, carry):
            r0 = pl.multiple_of(rt * r1, r1)
            brs = [b_s[pl.ds(r0, r1), cols(j)] for j in range(FFT_K1)]
            bis = [b_s[pl.ds(n2 + r0, r1), cols(j)] for j in range(FFT_K1)]
            for a in range(FFT_Z1):
                terms = [(1.0, brs[0]), (-1.0 if a % 2 else 1.0, brs[FFT_K1 - 1])]
                for j in range(1, FFT_K1 - 1):
                    terms += [(2.0 * c16[a][j], brs[j]), (-2.0 * s16[a][j], bis[j])]
                rows = pl.ds(a * n2 + r0, r1)
                z = g_s[rows, :] * (_lincomb(terms) + z_s[rows, :] * db_ref[o:o + 1, :])
                if o + 1 < len(stages):
                    z_s[rows, :] = z
                else:
                    o_ref[0, rows, :] = z.astype(o_ref.dtype)
            return carry

        lax.fori_loop(0, n2 // r1, inverse1, 0)


def _hyena_tables(L):
    n_fft = 2 * L
    n2 = L // FFT_Z1
    ct = HY_CT
    idx = jnp.arange(n2, dtype=jnp.int32)
    a2 = ((idx[:, None] * idx[None, :]) % n2).astype(f32) * (2.0 * math.pi / n2)
    c2, s2 = jnp.cos(a2), jnp.sin(a2)
    m2f = jnp.block([[c2, s2], [-s2, c2]]).astype(bf16)
    m2i = jnp.block([[c2, -s2], [s2, c2]]).astype(bf16)
    at = (idx[:, None] * jnp.arange(FFT_K1, dtype=jnp.int32)[None, :]).astype(f32) * (2.0 * math.pi / n_fft)
    twr = jnp.broadcast_to(jnp.cos(at)[:, :, None], (n2, FFT_K1, ct)).reshape(n2, FFT_K1 * ct)
    twi = jnp.broadcast_to(jnp.sin(at)[:, :, None], (n2, FFT_K1, ct)).reshape(n2, FFT_K1 * ct)
    return m2f, m2i, twr, twi


def _hyena_fft(hx, conv_w, tables, kr, ki, dbias):
    B, L, _ = hx.shape
    ct = HY_CT
    nct = B_WIDTH // ct
    n2 = L // FFT_Z1
    wide = FFT_K1 * ct
    once = pl.Buffered(1)
    m2f, m2i, twr, twi = tables

    def xspec(part):
        return pl.BlockSpec((1, L, ct), lambda c, b: (b, 0, part * nct + c))

    def wspec(part):
        return pl.BlockSpec((SUBLANE, ct), lambda c, b: (0, part * nct + c))

    def kspec(order):
        return pl.BlockSpec((FFT_K1, n2, ct), lambda c, b: (0, 0, order * nct + c), pipeline_mode=once)

    def const(shape):
        return pl.BlockSpec(shape, lambda c, b: (0, 0), pipeline_mode=once)

    return pl.pallas_call(
        _hyena_fft_kernel,
        grid=(nct, B),
        in_specs=[const((2 * n2, 2 * n2)), const((2 * n2, 2 * n2)), const((n2, wide)), const((n2, wide)),
                  xspec(0), xspec(1), xspec(2), wspec(0), wspec(1), wspec(2),
                  kspec(0), kspec(0), kspec(1), kspec(1),
                  pl.BlockSpec((SUBLANE, ct), lambda c, b: (0, c))],
        out_specs=pl.BlockSpec((1, L, ct), lambda c, b: (b, 0, c)),
        out_shape=jax.ShapeDtypeStruct((B, L, B_WIDTH), bf16),
        scratch_shapes=[pltpu.VMEM((L, ct), f32), pltpu.VMEM((L, ct), f32),
                        pltpu.VMEM((2 * n2, wide), bf16), pltpu.VMEM((2 * n2, wide), bf16),
                        pltpu.VMEM((2 * n2, wide), f32)],
        compiler_params=_cparams("parallel", "parallel"),
        name="hyena_fft",
    )(m2f, m2i, twr, twi, hx, hx, hx, conv_w, conv_w, conv_w, kr, ki, kr, ki, dbias)


RT_E1, RT_E2, RT_W1, RT_W2, RT_R1, RT_R2 = 0, 1, 2, 3, 4, 5
ROUTE_TM = ROW_TM
PROJ_PARTS = 2
COMBINE_PARTS = 4
EXPERT_TM = 512
SC_WINDOW = 128


def _pack_bf16_pairs(x):
    n = x.shape[1] // 2
    bits = lax.bitcast_convert_type(x.astype(bf16).astype(f32), jnp.uint32)
    return bits[:, :n] | (bits[:, n:] >> 16)


def _unpack_bf16_pairs(p):
    hi = lax.bitcast_convert_type(p & jnp.uint32(0xFFFF0000), f32)
    lo = lax.bitcast_convert_type(p << 16, f32)
    return hi, lo


ROUTER_ROWS = 40


def _route_topk(lt):
    row = lax.broadcasted_iota(jnp.int32, lt.shape, 0)
    neg = -jnp.inf
    big = ROUTER_LANES
    gl = jnp.where(row < N_GROUPS, lt, neg)
    gmax = jnp.max(gl, axis=0, keepdims=True)
    gsel = jnp.min(jnp.where(gl == gmax, row, big), axis=0, keepdims=True)
    psel = 1.0 / jnp.sum(jnp.exp(gl - gmax), axis=0, keepdims=True)
    first = N_GROUPS + gsel * EXPERTS_PER_GROUP
    el = jnp.where((row >= first) & (row < first + EXPERTS_PER_GROUP), lt, neg)
    t1 = jnp.max(el, axis=0, keepdims=True)
    i1 = jnp.min(jnp.where(el == t1, row, big), axis=0, keepdims=True)
    el2 = jnp.where(row == i1, neg, el)
    t2 = jnp.max(el2, axis=0, keepdims=True)
    i2 = jnp.min(jnp.where(el2 == t2, row, big), axis=0, keepdims=True)
    ex = jnp.exp(t2 - t1)
    w1 = psel / (1.0 + ex)
    w2 = ex * w1
    return i1 - N_GROUPS, i2 - N_GROUPS, w1, w2


def _proj_route_kernel(x_ref, *refs, n_in):
    a_refs = refs[0:n_in]
    w_refs = refs[n_in:2 * n_in]
    g_ref, wr_ref, up_ref, xo_ref, hpa_ref, hpb_ref, rt_ref, rtt_ref, cnt_ref = refs[2 * n_in:]
    tm = xo_ref.shape[0]
    rp = tm // PROJ_PARTS
    parts = [slice(p * rp, (p + 1) * rp) for p in range(PROJ_PARTS)]
    xs = []
    for r in parts:
        x = x_ref[r, :]
        for a_ref, w_ref in zip(a_refs, w_refs):
            x = x + _dot(a_ref[r, :], w_ref[...])
        xo_ref[r, :] = x
        xs.append(x)
    hs = [_rms(x, g_ref[...]) for x in xs]
    for r, h in zip(parts, hs):
        hp = _pack_bf16_pairs(h)
        quarter = hp.shape[1] // 2
        hpa_ref[r, :] = hp[:, :quarter]
        hpb_ref[r, :] = hp[:, quarter:]
    tops = []
    for h in hs:
        h_hi = h.astype(bf16)
        h_lo = (h - h_hi.astype(f32)).astype(bf16)
        both = _dot(h_hi, wr_ref[...])
        logits = both[:, :ROUTER_LANES] + both[:, ROUTER_LANES:] + _dot(h_lo, wr_ref[:, :ROUTER_LANES])
        tops.append(_route_topk(logits.T[0:ROUTER_ROWS, :]))
    row = lax.broadcasted_iota(jnp.int32, (ROUTER_LANES, rp), 0)
    onehots = [jnp.where((row == e1) | (row == e2), 1.0, 0.0) for e1, e2, _, _ in tops]
    onehot16 = jnp.concatenate(onehots, axis=1).astype(bf16)
    before = _dot(onehot16, up_ref[...])
    for p, (r, (e1, e2, w1, w2)) in enumerate(zip(parts, tops)):
        bp = before[:, r]
        r1 = jnp.sum(jnp.where(row == e1, bp, 0.0), axis=0, keepdims=True)
        r2 = jnp.sum(jnp.where(row == e2, bp, 0.0), axis=0, keepdims=True)
        rec = jnp.zeros((ROUTER_LANES, rp), f32)
        for ln, val in ((RT_E1, e1.astype(f32)), (RT_E2, e2.astype(f32)), (RT_W1, w1), (RT_W2, w2),
                        (RT_R1, r1), (RT_R2, r2)):
            rec = jnp.where(row == ln, val, rec)
        rtt_ref[:, r] = rec[0:rtt_ref.shape[0], :]
        rt_ref[r, :] = rec.T
    ones = jnp.ones((cnt_ref.shape[1], tm), bf16)
    cnt_ref[0] = _dot_nt(ones, onehot16)


def _proj_route(x, acts, ws, gain, w_router):
    T, D = x.shape
    tm = ROUTE_TM
    n_in = len(acts)
    up = jnp.asarray(np.triu(np.ones((tm, tm), np.float32), 1), dtype=bf16)
    in_specs = [pl.BlockSpec((tm, D), lambda i: (i, 0))]
    in_specs += [pl.BlockSpec((tm, a.shape[1]), lambda i: (i, 0)) for a in acts]
    in_specs += [pl.BlockSpec(w.shape, lambda i: (0, 0)) for w in ws]
    in_specs += [pl.BlockSpec((1, D), lambda i: (0, 0)),
                 pl.BlockSpec((D, 2 * ROUTER_LANES), lambda i: (0, 0)),
                 pl.BlockSpec((tm, tm), lambda i: (0, 0))]
    return pl.pallas_call(
        functools.partial(_proj_route_kernel, n_in=n_in),
        grid=(T // tm,),
        in_specs=in_specs,
        out_specs=[pl.BlockSpec((tm, D), lambda i: (i, 0)),
                   pl.BlockSpec((tm, D // 4), lambda i: (i, 0)),
                   pl.BlockSpec((tm, D // 4), lambda i: (i, 0)),
                   pl.BlockSpec((tm, ROUTER_LANES), lambda i: (i, 0)),
                   pl.BlockSpec((SUBLANE, tm), lambda i: (0, i)),
                   pl.BlockSpec((1, SUBLANE, ROUTER_LANES), lambda i: (i, 0, 0))],
        out_shape=[jax.ShapeDtypeStruct((T, D), f32),
                   jax.ShapeDtypeStruct((T, D // 4), jnp.uint32),
                   jax.ShapeDtypeStruct((T, D // 4), jnp.uint32),
                   jax.ShapeDtypeStruct((T, ROUTER_LANES), f32),
                   jax.ShapeDtypeStruct((SUBLANE, T), f32),
                   jax.ShapeDtypeStruct((T // tm, SUBLANE, ROUTER_LANES), f32)],
        compiler_params=_cparams("parallel"),
        name="proj_route",
    )(x, *acts, *ws, gain.reshape(1, D), w_router, up)


def _moe_rows(T):
    return 2 * T + N_EXPERTS * EXPERT_TM


def _moe_plan(route_t, counts):
    T = route_t.shape[1]
    tm = ROUTE_TM
    cnt = counts[:, 0, :N_EXPERTS].astype(jnp.int32)
    total = jnp.sum(cnt, axis=0)
    padded = ((total + EXPERT_TM - 1) // EXPERT_TM) * EXPERT_TM
    ends = jnp.cumsum(padded)
    tile_base = (ends - padded)[None, :] + jnp.cumsum(cnt, axis=0) - cnt
    base = jnp.broadcast_to(tile_base.T[:, :, None], (N_EXPERTS, T // tm, tm)).reshape(N_EXPERTS, T)
    ids = jnp.arange(N_EXPERTS, dtype=jnp.int32)[:, None]
    dest = []
    for e_row, r_row in ((RT_E1, RT_R1), (RT_E2, RT_R2)):
        e = route_t[e_row].astype(jnp.int32)
        dest.append(jnp.sum(jnp.where(ids == e[None, :], base, 0), axis=0) + route_t[r_row].astype(jnp.int32))
    n_tiles = _moe_rows(T) // EXPERT_TM
    n_used = ends[-1] // EXPERT_TM
    starts = jnp.arange(n_tiles, dtype=jnp.int32) * EXPERT_TM
    tile_expert = jnp.sum((starts[:, None] >= ends[None, :]).astype(jnp.int32), axis=1)
    last_expert = jnp.sum((jnp.maximum(n_used - 1, 0) * EXPERT_TM >= ends).astype(jnp.int32))
    tile_expert = jnp.minimum(tile_expert, last_expert)
    tiles = jnp.arange(n_tiles, dtype=jnp.int32)
    prev_expert = jnp.concatenate([jnp.full((1,), -1, jnp.int32), tile_expert[:-1]])
    seg_first = ((tile_expert != prev_expert) & (tiles < n_used)).astype(jnp.int32)
    seg_slot = (jnp.cumsum(seg_first) - 1) % 2
    seg_end = jnp.sum(jnp.where(ids == tile_expert[None, :], (ends // EXPERT_TM)[:, None], 0), axis=0)
    nxt_tile = jnp.minimum(seg_end, n_tiles - 1)
    nxt = jnp.where(seg_end < n_used, tile_expert[nxt_tile], -1)
    nxt2 = jnp.where(nxt >= 0, nxt[nxt_tile], -1)
    plan = (tile_expert, seg_first, seg_slot.astype(jnp.int32), nxt.astype(jnp.int32), nxt2.astype(jnp.int32),
            n_used.reshape(1).astype(jnp.int32))
    return dest, plan


def _sc_mesh():
    return plsc.VectorSubcoreMesh(core_axis_name="core", subcore_axis_name="subcore")


def _sc_scatter_pair(rows, d0, d1, n_out):
    T, W = rows.shape
    win = SC_WINDOW

    @pl.kernel(out_type=jax.ShapeDtypeStruct((n_out, W), rows.dtype), mesh=_sc_mesh(), name="moe_sc_scatter")
    def scatter(x_hbm, i0_hbm, i1_hbm, o_hbm):
        def body(x_vmem, i0_vmem, i1_vmem):
            pltpu.sync_copy(x_vmem, o_hbm.at[i0_vmem.at[0]])
            pltpu.sync_copy(x_vmem, o_hbm.at[i1_vmem.at[0]])

        pltpu.emit_pipeline(
            body, grid=(T // win,),
            in_specs=[pl.BlockSpec((win, W), index_map=lambda i: (i, 0)),
                      pl.BlockSpec((1, win), index_map=lambda i: (0, i)),
                      pl.BlockSpec((1, win), index_map=lambda i: (0, i))],
            out_specs=[],
            core_axis_name=("core", "subcore"), dimension_semantics=(pltpu.PARALLEL,),
        )(x_hbm, i0_hbm, i1_hbm)

    return scatter(rows, d0.reshape(1, T), d1.reshape(1, T))


def _sc_gather(table, idx):
    n = idx.shape[0]
    W = table.shape[1]
    win = SC_WINDOW

    @pl.kernel(out_type=jax.ShapeDtypeStruct((n, W), table.dtype), mesh=_sc_mesh(), name="moe_sc_gather")
    def gather(t_hbm, i_hbm, o_hbm):
        def body(i_vmem, o_vmem):
            pltpu.sync_copy(t_hbm.at[i_vmem.at[0]], o_vmem)

        pltpu.emit_pipeline(
            body, grid=(n // win,),
            in_specs=[pl.BlockSpec((1, win), index_map=lambda i: (0, i))],
            out_specs=[pl.BlockSpec((win, W), index_map=lambda i: (i, 0))],
            core_axis_name=("core", "subcore"), dimension_semantics=(pltpu.PARALLEL,),
        )(i_hbm, o_hbm)

    return gather(table, idx.reshape(1, n))


def _expert_kernel(te_ref, first_ref, slot_ref, nxt_ref, nxt2_ref, nu_ref, xa_ref, xb_ref, wg_hbm, wu_hbm, wd_hbm,
                   ya_ref, yb_ref, wg_s, wu_s, wd_s, wg16, wu16, wd16, sem):
    i = pl.program_id(0)
    active = i < nu_ref[0]

    def weight_copies(expert, s):
        return (pltpu.make_async_copy(wg_hbm.at[expert], wg_s.at[s], sem.at[s, 0]),
                pltpu.make_async_copy(wu_hbm.at[expert], wu_s.at[s], sem.at[s, 1]),
                pltpu.make_async_copy(wd_hbm.at[expert], wd_s.at[s], sem.at[s, 2]))

    @pl.when(i == 0)
    def _():
        for c in weight_copies(te_ref[0], slot_ref[0]):
            c.start()

        @pl.when(nxt_ref[0] >= 0)
        def _():
            for c in weight_copies(nxt_ref[0], 1 - slot_ref[0]):
                c.start()

    @pl.when(active & (first_ref[i] == 1))
    def _():
        s = slot_ref[i]
        for c in weight_copies(te_ref[i], s):
            c.wait()
        wg16[...] = wg_s[s].astype(bf16)
        wu16[...] = wu_s[s].astype(bf16)
        wd16[...] = wd_s[s].astype(bf16)

        @pl.when(nxt2_ref[i] >= 0)
        def _():
            for c in weight_copies(nxt2_ref[i], s):
                c.start()

    @pl.when(active)
    def _():
        hi, lo = _unpack_bf16_pairs(jnp.concatenate([xa_ref[...], xb_ref[...]], axis=1))
        x = jnp.concatenate([hi, lo], axis=1).astype(bf16)
        gate = _dot(x, wg16[...])
        up = _dot(x, wu16[...])
        hg = (gate * _sigmoid(gate) * up).astype(bf16)
        y = _pack_bf16_pairs(_dot(hg, wd16[...]))
        quarter = y.shape[1] // 2
        ya_ref[...] = y[:, :quarter]
        yb_ref[...] = y[:, quarter:]

    @pl.when(jnp.logical_not(active))
    def _():
        ya_ref[...] = jnp.zeros(ya_ref.shape, ya_ref.dtype)
        yb_ref[...] = jnp.zeros(yb_ref.shape, yb_ref.dtype)


def _experts(plan, layer, xa, xb, wg, wu, wd):
    tile_expert, seg_first, seg_slot, nxt, nxt2, n_used = plan
    off = layer * N_EXPERTS
    ahead = [jnp.where(e >= 0, e + off, -1) for e in (nxt, nxt2)]
    n_rows, quarter = xa.shape
    D = 4 * quarter
    tm = EXPERT_TM
    row_spec = pl.BlockSpec((tm, quarter), lambda i, *_: (i, 0))
    grid_spec = pltpu.PrefetchScalarGridSpec(
        num_scalar_prefetch=6,
        grid=(n_rows // tm,),
        in_specs=[row_spec, row_spec,
                  pl.BlockSpec(memory_space=pl.ANY), pl.BlockSpec(memory_space=pl.ANY), pl.BlockSpec(memory_space=pl.ANY)],
        out_specs=[row_spec, row_spec],
        scratch_shapes=[pltpu.VMEM((2, D, D_EXPERT), wg.dtype), pltpu.VMEM((2, D, D_EXPERT), wu.dtype),
                        pltpu.VMEM((2, D_EXPERT, D), wd.dtype),
                        pltpu.VMEM((D, D_EXPERT), bf16), pltpu.VMEM((D, D_EXPERT), bf16), pltpu.VMEM((D_EXPERT, D), bf16),
                        pltpu.SemaphoreType.DMA((2, 3))])
    return pl.pallas_call(
        _expert_kernel,
        grid_spec=grid_spec,
        out_shape=[jax.ShapeDtypeStruct((n_rows, quarter), jnp.uint32)] * 2,
        compiler_params=_cparams("arbitrary"),
        name="moe_experts",
    )(tile_expert + off, seg_first, seg_slot, *ahead, n_used, xa, xb, wg, wu, wd)


def _combine_kernel(x_ref, rt_ref, g_ref, a1_ref, b1_ref, a2_ref, b2_ref, *rest, final):
    half = x_ref.shape[1] // 2
    rp = x_ref.shape[0] // COMBINE_PARTS
    for p in range(COMBINE_PARTS):
        r = slice(p * rp, (p + 1) * rp)
        h1, l1 = _unpack_bf16_pairs(jnp.concatenate([a1_ref[r, :], b1_ref[r, :]], axis=1))
        h2, l2 = _unpack_bf16_pairs(jnp.concatenate([a2_ref[r, :], b2_ref[r, :]], axis=1))
        w1 = rt_ref[r, RT_W1:RT_W1 + 1]
        w2 = rt_ref[r, RT_W2:RT_W2 + 1]
        out = jnp.concatenate([x_ref[r, 0:half] + w1 * h1 + w2 * h2, x_ref[r, half:] + w1 * l1 + w2 * l2], axis=1)
        if final:
            (o_ref,) = rest
            o_ref[r, :] = _rms(out, g_ref[...])
        else:
            w_ref, o_ref, p_ref = rest
            o_ref[r, :] = out
            p_ref[r, :] = _dot(_rms(out, g_ref[...]).astype(bf16), w_ref[...]).astype(p_ref.dtype)


def _combine(x, route, ga, gb, gain, w_next=None):
    T, D = x.shape
    tm = ROUTE_TM
    n = T // tm
    quarter = D // 4
    final = w_next is None
    slot1 = pl.BlockSpec((tm, quarter), lambda i: (i, 0))
    slot2 = pl.BlockSpec((tm, quarter), lambda i: (i + n, 0))
    row_spec = pl.BlockSpec((tm, D), lambda i: (i, 0))
    in_specs = [row_spec, pl.BlockSpec((tm, ROUTER_LANES), lambda i: (i, 0)), pl.BlockSpec((1, D), lambda i: (0, 0)),
                slot1, slot1, slot2, slot2]
    args = [x, route, gain.reshape(1, D), ga, gb, ga, gb]
    out_specs = [row_spec]
    out_shape = [jax.ShapeDtypeStruct((T, D), f32)]
    if not final:
        n_next = w_next.shape[1]
        in_specs.append(pl.BlockSpec((D, n_next), lambda i: (0, 0)))
        args.append(w_next)
        out_specs.append(pl.BlockSpec((tm, n_next), lambda i: (i, 0)))
        out_shape.append(jax.ShapeDtypeStruct((T, n_next), bf16))
    outs = pl.pallas_call(
        functools.partial(_combine_kernel, final=final),
        grid=(n,),
        in_specs=in_specs,
        out_specs=out_specs,
        out_shape=out_shape,
        compiler_params=_cparams("parallel"),
        name="moe_combine",
    )(*args)
    return outs[0] if final else outs


def _moe(layer, x, hpa, hpb, route, route_t, counts, wg, wu, wd, gain, w_next=None):
    (d0, d1), plan = _moe_plan(route_t, counts)
    n_rows = _moe_rows(x.shape[0])
    xa = _sc_scatter_pair(hpa, d0, d1, n_rows)
    xb = _sc_scatter_pair(hpb, d0, d1, n_rows)
    ya, yb = _experts(plan, layer, xa, xb, wg, wu, wd)
    dcat = jnp.concatenate([d0, d1])
    return _combine(x, route, _sc_gather(ya, dcat), _sc_gather(yb, dcat), gain, w_next)


SWA_LOCKSTEP = 2


def _swa_bias_tables():
    blk = C_BLOCK
    qi = np.arange(blk)[:, None]
    kj = np.arange(3 * blk)[None, :]
    dist = np.abs(blk + qi - kj)
    tabs = np.full((3, C_KV_HEADS, C_GROUP * blk, 3 * blk), -np.inf, np.float32)
    for case in range(3):
        ok = dist <= WINDOW
        if case == 0:
            ok = ok & (kj >= blk)
        if case == 2:
            ok = ok & (kj < 2 * blk)
        for hh in range(C_HEADS):
            slope = 2.0 ** (-8.0 * (hh + 1) / C_HEADS)
            t = np.where(ok, -slope * LOG2E * dist, -np.inf)
            kvh, g = divmod(hh, C_GROUP)
            tabs[case, kvh, g * blk:(g + 1) * blk] = t
    return jnp.asarray(tabs)


def _swa_kernel(sink_ref, q_ref, kp_ref, kc_ref, kn_ref, vp_ref, vc_ref, vn_ref, bias_ref, o_ref):
    blk = C_BLOCK
    hd = C_HEAD_DIM
    kcat = jnp.concatenate([kp_ref[0], kc_ref[0], kn_ref[0]], axis=0)
    vcat = jnp.concatenate([vp_ref[0], vc_ref[0], vn_ref[0]], axis=0)
    q = q_ref[0]
    rblk = lax.broadcasted_iota(jnp.int32, (C_GROUP * blk, 1), 0) // blk
    lane = lax.broadcasted_iota(jnp.int32, (3 * blk, LANE - hd), 1)
    ones_pad = jnp.where(lane == 0, 1.0, 0.0).astype(bf16)
    q4s, sinks = [], []
    for kvh in range(C_KV_HEADS):
        q4s.append(jnp.concatenate(
            [q[:, (kvh * C_GROUP + g) * hd:(kvh * C_GROUP + g + 1) * hd] for g in range(C_GROUP)], axis=0))
        sk = jnp.zeros((C_GROUP * blk, 1), f32)
        for g in range(C_GROUP):
            sk = jnp.where(rblk == g, sink_ref[kvh * C_GROUP + g] * LOG2E, sk)
        sinks.append(sk)
    outs = []
    for h0 in range(0, C_KV_HEADS, SWA_LOCKSTEP):
        hs = range(h0, h0 + SWA_LOCKSTEP)
        ss = [_dot_nt(q4s[h], kcat[:, h * hd:(h + 1) * hd]) + bias_ref[0, h] for h in hs]
        ms = [jnp.maximum(jnp.max(s, axis=-1, keepdims=True), sinks[h]) for s, h in zip(ss, hs)]
        ps = [jnp.exp2(s - m).astype(bf16) for s, m in zip(ss, ms)]
        pvs = [_dot(p, jnp.concatenate([vcat[:, h * hd:(h + 1) * hd], ones_pad], axis=1)) for p, h in zip(ps, hs)]
        for pv, m, h in zip(pvs, ms, hs):
            den = pv[:, hd:hd + 1] + jnp.exp2(sinks[h] - m)
            o4 = pv[:, 0:hd] / den
            outs += [o4[g * blk:(g + 1) * blk, :] for g in range(C_GROUP)]
    o_ref[0] = jnp.concatenate(outs, axis=1).astype(o_ref.dtype)


def _swa(qkv, sinks):
    B, L, _ = qkv.shape
    nb = L // C_BLOCK
    dq = C_HEADS * C_HEAD_DIM
    dkv = C_KV_HEADS * C_HEAD_DIM
    k_col = dq // dkv
    v_col = k_col + 1

    def kv_spec(col, shift):
        return pl.BlockSpec((1, C_BLOCK, dkv),
                            lambda n, b: (b, jnp.clip(n + shift, 0, nb - 1), col))

    def edge_case(n, b):
        return (jnp.where(n == 0, 0, jnp.where(n == nb - 1, 2, 1)), 0, 0, 0)

    return pl.pallas_call(
        _swa_kernel,
        grid=(nb, B),
        in_specs=[pl.BlockSpec(memory_space=pltpu.SMEM),
                  pl.BlockSpec((1, C_BLOCK, dq), lambda n, b: (b, n, 0)),
                  kv_spec(k_col, -1), kv_spec(k_col, 0), kv_spec(k_col, 1),
                  kv_spec(v_col, -1), kv_spec(v_col, 0), kv_spec(v_col, 1),
                  pl.BlockSpec((1, C_KV_HEADS, C_GROUP * C_BLOCK, 3 * C_BLOCK), edge_case)],
        out_specs=pl.BlockSpec((1, C_BLOCK, dq), lambda n, b: (b, n, 0)),
        out_shape=jax.ShapeDtypeStruct((B, L, dq), bf16),
        compiler_params=_cparams("parallel", "parallel"),
        name="swa",
    )(sinks, qkv, qkv, qkv, qkv, qkv, qkv, qkv, _swa_bias_tables())


def _hyena_filters(L, w1, b1, w2, b2, w3, b3, w4, freq):
    hp = lax.Precision.HIGHEST
    t = jnp.linspace(0.0, 1.0, L, dtype=f32)[:, None]
    wpos = (2.0 * math.pi / L) * jnp.arange(L, dtype=f32)[:, None]
    bands = jnp.linspace(1e-4, HY_BANDS - 1, HY_BANDS, dtype=f32)[None, :]
    feats = jnp.concatenate([t, jnp.cos(bands * wpos), -jnp.sin(bands * wpos)], axis=-1)
    h = jnp.sin(freq[0] * (jnp.dot(feats, w1, precision=hp) + b1))
    h = jnp.sin(freq[1] * (jnp.dot(h, w2, precision=hp) + b2))
    h = jnp.sin(freq[2] * (jnp.dot(h, w3, precision=hp) + b3))
    h = jnp.dot(h, w4, precision=hp).reshape(L, HY_ORDER, 2, B_WIDTH)
    deltas = jnp.abs(jnp.linspace(HY_MIN_DECAY, HY_MAX_DECAY, B_WIDTH, dtype=f32))
    window = jnp.exp(-t * deltas[None, :])
    return h * window[:, None, None, :]


def _pad_rows(w):
    return jnp.pad(w, ((0, SUBLANE - w.shape[0]), (0, 0)))


def _even_in_weights(w_in):
    hk = A_HEADS * A_DK
    q = w_in[:, 0:hk].reshape(D_MODEL, A_HEADS, A_DK)
    k = w_in[:, hk:2 * hk].reshape(D_MODEL, A_HEADS, A_DK)
    v = w_in[:, 2 * hk:A_QKV].reshape(D_MODEL, A_HEADS, A_DV)
    z = w_in[:, A_QKV:A_QKV + A_WIDTH].reshape(D_MODEL, A_HEADS, A_DV)
    qkvz = jnp.concatenate([q, k, v, z], axis=-1).reshape(D_MODEL, A_HEADS * 4 * A_DK)
    ab = w_in[:, A_QKV + A_WIDTH:A_QKV + A_WIDTH + 4 * A_HEADS].reshape(D_MODEL, 4, A_HEADS)
    ab = jnp.transpose(ab, (0, 2, 1))
    ab = jnp.pad(ab.reshape(D_MODEL, 4 * A_HEADS), ((0, 0), (0, LANE - 4 * A_HEADS)))
    hx = w_in[:, A_QKV + A_WIDTH + 4 * A_HEADS:]
    return jnp.concatenate([qkvz, ab, hx], axis=1).astype(bf16)


def _router_weights(w_group, w_expert):
    w = jnp.concatenate([w_group, w_expert], axis=1)
    w = jnp.pad(w, ((0, 0), (0, ROUTER_LANES - w.shape[1])))
    hi = w.astype(bf16)
    lo = (w - hi.astype(f32)).astype(bf16)
    return jnp.concatenate([hi, lo], axis=1)


def kernel(x_prompt, x_sample, norm_mix, norm_ffn, norm_final, ev_w_in, ev_conv_a, ev_alog_f, ev_alog_b, ev_dtb_f, ev_dtb_b, ev_onorm, ev_conv_b, hy_w1, hy_b1, hy_w2, hy_b2, hy_w3, hy_b3, hy_w4, hy_freq, hy_dbias, ev_w_out, od_w_qkv, od_sinks, od_w_out, moe_w_group, moe_w_expert, moe_w_gate, moe_w_up, moe_w_down):
    L = x_prompt.shape[1]
    n_layers = moe_w_gate.shape[0]
    wg = moe_w_gate.reshape(n_layers * N_EXPERTS, D_MODEL, D_EXPERT)
    wu = moe_w_up.reshape(n_layers * N_EXPERTS, D_MODEL, D_EXPERT)
    wd = moe_w_down.reshape(n_layers * N_EXPERTS, D_EXPERT, D_MODEL)

    w_in = _even_in_weights(ev_w_in[0])
    n_qkvz = A_HEADS * 4 * A_DK
    n_ab = A_HEADS * LANE
    n_hx = 3 * B_WIDTH
    hk = A_HEADS * A_DK
    ca = ev_conv_a[0]
    conv_a = jnp.concatenate([ca[0:hk].reshape(A_HEADS, A_DK, 3), ca[hk:2 * hk].reshape(A_HEADS, A_DK, 3),
                              ca[2 * hk:].reshape(A_HEADS, A_DV, 3)], axis=1)
    conv_a = jnp.pad(jnp.transpose(conv_a, (0, 2, 1)), ((0, 0), (0, SUBLANE - 3), (0, 0)))
    gate_sc = jnp.stack([ev_alog_f[0], ev_alog_b[0], ev_dtb_f[0], ev_dtb_b[0]], axis=1)
    filt = _hyena_filters(L, hy_w1[0], hy_b1[0], hy_w2[0], hy_b2[0], hy_w3[0], hy_b3[0], hy_w4[0], hy_freq[0])
    hsum = (filt[:, :, 0] + filt[:, :, 1]).reshape(L, HY_ORDER * B_WIDTH)
    hdiff = (filt[:, :, 0] - filt[:, :, 1]).reshape(L, HY_ORDER * B_WIDTH)
    kr, ki = _filter_spectrum(hsum, hdiff)
    fft_tables = _hyena_tables(L)
    conv_b = _pad_rows(ev_conv_b[0].T)
    dbias = _pad_rows(hy_dbias[0])
    w_out = ev_w_out[0].astype(bf16)
    routers = [_router_weights(moe_w_group[l], moe_w_expert[l]) for l in range(n_layers)]
    n_qkv = od_w_qkv.shape[-1]
    dq = C_HEADS * C_HEAD_DIM
    w_qkv = jnp.concatenate([od_w_qkv[0][:, :dq] * (LOG2E * C_HEAD_DIM ** -0.5), od_w_qkv[0][:, dq:]], axis=1)
    w_qkv = w_qkv.astype(bf16)
    w_att_out = od_w_out[0].astype(bf16)

    def trunk(xb):
        B = xb.shape[0]
        T = B * L
        x = xb.reshape(T, D_MODEL)
        qkvz, abh, hx = _even_in_proj(x, norm_mix[0], w_in)
        o_a = _deltanet(qkvz.reshape(B, L, n_qkvz), abh.reshape(B, L, n_ab), conv_a, gate_sc, ev_onorm[0])
        o_b = _hyena_fft(hx.reshape(B, L, n_hx), conv_b, fft_tables, kr, ki, dbias)
        x, hpa, hpb, route, route_t, counts = _proj_route(
            x, (o_a.reshape(T, A_WIDTH), o_b.reshape(T, B_WIDTH)), (w_out[:A_WIDTH], w_out[A_WIDTH:]),
            norm_ffn[0], routers[0])
        x, qkv = _moe(0, x, hpa, hpb, route, route_t, counts, wg, wu, wd, norm_mix[1], w_qkv)
        o_c = _swa(qkv.reshape(B, L, n_qkv), od_sinks[0])
        x, hpa, hpb, route, route_t, counts = _proj_route(
            x, (o_c.reshape(T, C_HEADS * C_HEAD_DIM),), (w_att_out,), norm_ffn[1], routers[1])
        y = _moe(1, x, hpa, hpb, route, route_t, counts, wg, wu, wd, norm_final)
        return y.reshape(B, L, D_MODEL)

    return (trunk(x_prompt), trunk(x_sample))
```

```python
import functools
import math

import jax
import jax.numpy as jnp
import numpy as np
from jax import lax
from jax.experimental import pallas as pl
from jax.experimental.pallas import tpu as pltpu
from jax.experimental.pallas import tpu_sc as plsc

f32 = jnp.float32
bf16 = jnp.bfloat16

EPS = 1e-6
D_MODEL = 1024

A_HEADS = 4
A_DK = 128
A_DV = 128
A_WIDTH = A_HEADS * A_DV
A_QKV = 2 * A_HEADS * A_DK + A_HEADS * A_DV
DELTA_CHUNK = 128
INV_BASE = 8
DELTA_GROUP = 8
DELTA_WAVES = 2

B_WIDTH = D_MODEL - A_WIDTH
HY_ORDER = 2
HY_EMB = 33
HY_BANDS = (HY_EMB - 1) // 2
HY_TARGET = 1e-2
HY_MIN_DECAY = math.log(HY_TARGET) / 1.5
HY_MAX_DECAY = math.log(HY_TARGET) / 0.3
HY_CT = 256
HY_ROWS = 256

C_HEADS = 16
C_KV_HEADS = 4
C_HEAD_DIM = 64
C_GROUP = C_HEADS // C_KV_HEADS
WINDOW = 128
C_BLOCK = 128

N_GROUPS = 4
EXPERTS_PER_GROUP = 8
N_EXPERTS = N_GROUPS * EXPERTS_PER_GROUP
D_EXPERT = 256
ROUTER_LANES = 128

LANE = 128
SUBLANE = 8
SUBLANE_BF16 = 16
ROW_TM = 512
VMEM_LIMIT = 56 * 1024 * 1024
LOG2E = 1.4426950408889634


def _cparams(*sem):
    return pltpu.CompilerParams(dimension_semantics=sem, vmem_limit_bytes=VMEM_LIMIT)


def _dot(a, b):
    return jnp.dot(a, b, preferred_element_type=f32)


def _dot_nt(a, b):
    return lax.dot_general(a, b, (((1,), (1,)), ((), ())), preferred_element_type=f32)


def _dot_tn(a, b):
    return lax.dot_general(a, b, (((0,), (0,)), ((), ())), preferred_element_type=f32)


def _sigmoid(x):
    return 1.0 / (1.0 + jnp.exp(-x))


def _rms(x, g):
    return x * lax.rsqrt(jnp.mean(x * x, axis=-1, keepdims=True) + EPS) * g


def _even_in_kernel(x_ref, g_ref, w_ref, qkvz_ref, ab_ref, hx_ref):
    h = _rms(x_ref[...], g_ref[...]).astype(bf16)
    n_qkvz = qkvz_ref.shape[1]
    qkvz_ref[...] = _dot(h, w_ref[:, 0:n_qkvz]).astype(qkvz_ref.dtype)
    ab = _dot(h, w_ref[:, n_qkvz:n_qkvz + LANE])
    for hd in range(A_HEADS):
        ab_ref[:, hd * LANE:(hd + 1) * LANE] = ab if hd == 0 else pltpu.roll(ab, LANE - 4 * hd, 1)
    hx_ref[...] = _dot(h, w_ref[:, n_qkvz + LANE:]).astype(hx_ref.dtype)


def _even_in_proj(x, gain, w):
    T, D = x.shape
    tm = ROW_TM
    n_qkvz = A_HEADS * 4 * A_DK
    n_ab = A_HEADS * LANE
    n_hx = 3 * B_WIDTH
    return pl.pallas_call(
        _even_in_kernel,
        grid=(T // tm,),
        in_specs=[pl.BlockSpec((tm, D), lambda i: (i, 0)),
                  pl.BlockSpec((1, D), lambda i: (0, 0)),
                  pl.BlockSpec(w.shape, lambda i: (0, 0))],
        out_specs=[pl.BlockSpec((tm, n_qkvz), lambda i: (i, 0)),
                   pl.BlockSpec((tm, n_ab), lambda i: (i, 0)),
                   pl.BlockSpec((tm, n_hx), lambda i: (i, 0))],
        out_shape=[jax.ShapeDtypeStruct((T, n_qkvz), bf16),
                   jax.ShapeDtypeStruct((T, n_ab), f32),
                   jax.ShapeDtypeStruct((T, n_hx), bf16)],
        compiler_params=_cparams("parallel"),
        name="even_in_proj",
    )(x, gain.reshape(1, D), w)


MK_LOWER, MK_UPPER, MK_BASE, MK_LEVEL0 = 0, 1, 2, 3
N_LEVELS = int(math.log2(DELTA_CHUNK // INV_BASE))
MK_EYE = MK_LEVEL0 + N_LEVELS
N_MASKS = MK_EYE + 1


def _delta_masks():
    c = np.arange(DELTA_CHUNK)[:, None]
    s = np.arange(DELTA_CHUNK)[None, :]
    m = [s <= c, s >= c, (c // INV_BASE == s // INV_BASE) & (c != s)]
    b = INV_BASE
    while b < DELTA_CHUNK:
        m.append((c // (2 * b) == s // (2 * b)) & (c // b != s // b))
        b *= 2
    m.append(c == s)
    m = np.stack(m).astype(np.float32)
    return jnp.asarray(m), jnp.asarray(m, dtype=bf16)


def _tri_inverse(mats, mk_ref, mk16_ref):
    ps = [a * mk_ref[MK_BASE] for a in mats]
    ts = [mk_ref[MK_EYE] - p for p in ps]
    n = 2
    while n < INV_BASE:
        p16s = [p.astype(bf16) for p in ps]
        ps = [_dot(p16, p16) for p16 in p16s]
        ts = [t + _dot(t.astype(bf16), p.astype(bf16)) for t, p in zip(ts, ps)]
        n *= 2
    a16s = [a.astype(bf16) for a in mats]
    t16s = [t.astype(bf16) for t in ts]
    for lvl in range(N_LEVELS):
        xs = [_dot(a16 * mk16_ref[MK_LEVEL0 + lvl], t16) for a16, t16 in zip(a16s, t16s)]
        t16s = [t16 - _dot(t16, x.astype(bf16)).astype(bf16) for t16, x in zip(t16s, xs)]
    return t16s


def _deltanet_kernel(sc_ref, x_ref, ab_ref, cw_ref, on_ref, mk_ref, mk16_ref, o_ref, mp_s, nr_s, gl_s, o_s):
    hd = pl.program_id(1)
    L = x_ref.shape[1]
    C = DELTA_CHUNK
    nchunk = L // C
    group = math.gcd(DELTA_GROUP, nchunk)
    nqkv = 3 * A_DK
    halo = SUBLANE_BF16

    lane = lax.broadcasted_iota(jnp.int32, (1, LANE), 1)
    alog = jnp.where(lane == 2, sc_ref[hd, 0], jnp.where(lane == 3, sc_ref[hd, 1], 0.0))
    dtb = jnp.where(lane == 2, sc_ref[hd, 2], jnp.where(lane == 3, sc_ref[hd, 3], 0.0))
    neg_a = jnp.where((lane == 2) | (lane == 3), -jnp.exp(alog), 0.0)
    rowi = lax.broadcasted_iota(jnp.int32, (C, 1), 0)
    cw = cw_ref[0]

    def chunk_inputs(i):
        r0 = pl.multiple_of(i * C, C)
        xc = x_ref[0, pl.ds(r0, C), 0:nqkv].astype(f32)
        rp = pl.multiple_of(jnp.maximum(r0 - halo, 0), halo)
        rn = pl.multiple_of(jnp.minimum(r0 + C, L - halo), halo)
        prow = x_ref[0, pl.ds(rp, halo), 0:nqkv][halo - 1:halo, :].astype(f32)
        nrow = x_ref[0, pl.ds(rn, halo), 0:nqkv][0:1, :].astype(f32)
        prow = jnp.where(i > 0, prow, 0.0)
        nrow = jnp.where(i < nchunk - 1, nrow, 0.0)
        prev = jnp.where(rowi == 0, prow, pltpu.roll(xc, 1, 0))
        nxt = jnp.where(rowi == C - 1, nrow, pltpu.roll(xc, C - 1, 0))
        y = prev * cw[0:1, :] + xc * cw[1:2, :] + nxt * cw[2:3, :]
        y = y * _sigmoid(y)
        q = y[:, 0:A_DK]
        k = y[:, A_DK:2 * A_DK]
        vc = y[:, 2 * A_DK:nqkv]
        qc = q * lax.rsqrt(jnp.sum(q * q, axis=-1, keepdims=True) + EPS) * (A_DK ** -0.5)
        kc = k * lax.rsqrt(jnp.sum(k * k, axis=-1, keepdims=True) + EPS)

        abc = ab_ref[0, pl.ds(r0, C), :]
        zg = abc + dtb
        g = neg_a * (jnp.maximum(zg, 0.0) + jnp.log(1.0 + jnp.exp(-jnp.abs(zg))))
        low16 = mk16_ref[MK_LOWER]
        g1 = g.astype(bf16)
        e1 = g - g1.astype(f32)
        g2 = e1.astype(bf16)
        g3 = (e1 - g2.astype(f32)).astype(bf16)
        pre = _dot(low16, g1) + _dot(low16, g2) + _dot(low16, g3)
        tot = pre[C - 1:C, :]
        suf = tot - pre + g
        beta_all = _sigmoid(abc)

        k16 = kc.astype(bf16)
        kk = _dot_nt(k16, k16)
        qk = _dot_nt(qc.astype(bf16), k16)

        chains = []
        for rev in (0, 1):
            beta = beta_all[:, rev:rev + 1]
            gcc = (suf if rev else pre)[:, 2 + rev:3 + rev]
            tot11 = tot[:, 2 + rev:3 + rev]
            keep = mk_ref[MK_UPPER if rev else MK_LOWER]
            bc = jnp.broadcast_to(gcc * LOG2E, (C, C))
            decay = jnp.exp2((bc - bc.T) * keep) * keep
            a = (beta * kk) * decay
            eg = jnp.exp(gcc)
            chains.append(dict(
                i=i, rev=rev, a=a, gl=jnp.exp(tot11), qd=qc * eg,
                rhs=jnp.concatenate([vc * beta, kc * (beta * eg)], axis=1).astype(bf16),
                kd16=(kc * jnp.exp(tot11 - gcc)).astype(bf16),
                qk16=(qk * decay).astype(bf16)))
        return chains

    def prep_group(first_chunk):
        chains = []
        for gi in range(group):
            chains += chunk_inputs(first_chunk + gi)
        t16s = _tri_inverse([c["a"] for c in chains], mk_ref, mk16_ref)
        uws = [_dot(t16, c["rhs"]).astype(bf16) for t16, c in zip(t16s, chains)]
        kts = [_dot_tn(c["kd16"], uw) for c, uw in zip(chains, uws)]
        prs = [_dot(c["qk16"], uw) for c, uw in zip(chains, uws)]
        for c, kt, pr in zip(chains, kts, prs):
            rev, i = c["rev"], c["i"]
            mp_s[rev, i, 0:A_DK, :] = (-kt[:, A_DV:]).astype(bf16)
            mp_s[rev, i, A_DK:A_DK + C, :] = (c["qd"] - pr[:, A_DV:]).astype(bf16)
            nr_s[rev, i, 0:A_DK, :] = kt[:, 0:A_DV]
            nr_s[rev, i, A_DK:A_DK + C, :] = pr[:, 0:A_DV]
            gl_s[rev, i] = jnp.broadcast_to(c["gl"], (SUBLANE, LANE))

    def prep(it, carry):
        for wv in range(waves):
            prep_group((it * waves + wv) * group)
        return carry

    waves = math.gcd(DELTA_WAVES, nchunk // group)
    lax.fori_loop(0, nchunk // (group * waves), prep, 0)

    def scan(i, carry):
        out = []
        for rev, S in enumerate(carry):
            j = (nchunk - 1 - i) if rev else i
            y = _dot(mp_s[rev, j], S.astype(bf16))
            nr = nr_s[rev, j]
            o_s[rev, pl.ds(pl.multiple_of(j * C, C), C), :] = y[A_DK:, :] + nr[A_DK:, :]
            out.append(S * gl_s[rev, j][0:1, :] + y[0:A_DK, :] + nr[0:A_DK, :])
        return tuple(out)

    S0 = jnp.zeros((A_DK, A_DV), f32)
    lax.fori_loop(0, nchunk, scan, (S0, S0))

    def finish(it, carry):
        for gi in range(group):
            r0 = pl.multiple_of((it * group + gi) * C, C)
            o = o_s[0, pl.ds(r0, C), :] + o_s[1, pl.ds(r0, C), :]
            zg = x_ref[0, pl.ds(r0, C), nqkv:nqkv + A_DV].astype(f32)
            o_ref[0, pl.ds(r0, C), :] = (_rms(o, on_ref[...]) * (zg * _sigmoid(zg))).astype(o_ref.dtype)
        return carry

    lax.fori_loop(0, nchunk // group, finish, 0)


def _deltanet(qkvz, abh, conv_w, gate_sc, onorm):
    B, L, _ = qkvz.shape
    C = DELTA_CHUNK
    nchunk = L // C
    hw = 4 * A_DK
    scratch = [pltpu.VMEM((2, nchunk, A_DK + C, A_DV), bf16),
               pltpu.VMEM((2, nchunk, A_DK + C, A_DV), f32),
               pltpu.VMEM((2, nchunk, SUBLANE, LANE), f32),
               pltpu.VMEM((2, L, A_DV), f32)]
    return pl.pallas_call(
        _deltanet_kernel,
        grid=(B, A_HEADS),
        in_specs=[pl.BlockSpec(memory_space=pltpu.SMEM),
                  pl.BlockSpec((1, L, hw), lambda b, h: (b, 0, h)),
                  pl.BlockSpec((1, L, LANE), lambda b, h: (b, 0, h)),
                  pl.BlockSpec((1, SUBLANE, 3 * A_DK), lambda b, h: (h, 0, 0)),
                  pl.BlockSpec((1, A_DV), lambda b, h: (0, 0)),
                  pl.BlockSpec((N_MASKS, C, C), lambda b, h: (0, 0, 0)),
                  pl.BlockSpec((N_MASKS, C, C), lambda b, h: (0, 0, 0))],
        out_specs=pl.BlockSpec((1, L, A_DV), lambda b, h: (b, 0, h)),
        out_shape=jax.ShapeDtypeStruct((B, L, A_WIDTH), bf16),
        scratch_shapes=scratch,
        compiler_params=_cparams("parallel", "parallel"),
        name="deltanet",
    )(gate_sc, qkvz, abh, conv_w, onorm.reshape(1, A_DV), *_delta_masks())


def _filter_spectrum_kernel(c_ref, s_ref, hs_ref, hd_ref, kr_ref, ki_ref):
    L = c_ref.shape[1]
    rows = kr_ref.shape[1]
    scale = 1.0 / (2 * L)

    def split(x):
        hi = x.astype(bf16)
        return hi, (x - hi.astype(f32)).astype(bf16)

    hs_hi, hs_lo = split(hs_ref[...])
    hd_hi, hd_lo = split(hd_ref[...])
    for k1 in range(kr_ref.shape[0]):
        c = c_ref[k1 * rows:(k1 + 1) * rows, :]
        s = s_ref[k1 * rows:(k1 + 1) * rows, :]
        kr_ref[k1] = (_dot(c, hs_hi) + _dot(c, hs_lo)) * scale
        ki_ref[k1] = (_dot(s, hd_hi) + _dot(s, hd_lo)) * (-scale)


def _filter_spectrum(hsum, hdiff):
    L, n = hsum.shape
    ct = HY_CT
    n_fft = 2 * L
    n2 = L // FFT_Z1
    d = jnp.arange(L, dtype=jnp.int32)[None, :]
    a1 = ((jnp.arange(FFT_K1, dtype=jnp.int32)[:, None] * d) % n_fft).astype(f32) * (2.0 * math.pi / n_fft)
    a2 = ((jnp.arange(n2, dtype=jnp.int32)[:, None] * d) % n2).astype(f32) * (2.0 * math.pi / n2)
    c1, s1 = jnp.cos(a1)[:, None, :], jnp.sin(a1)[:, None, :]
    c2, s2 = jnp.cos(a2)[None, :, :], jnp.sin(a2)[None, :, :]
    cmat = (c1 * c2 - s1 * s2).reshape(FFT_K1 * n2, L).astype(bf16)
    smat = (s1 * c2 + c1 * s2).reshape(FFT_K1 * n2, L).astype(bf16)
    once = pl.Buffered(1)
    return pl.pallas_call(
        _filter_spectrum_kernel,
        grid=(n // ct,),
        in_specs=[pl.BlockSpec((FFT_K1 * n2, L), lambda j: (0, 0), pipeline_mode=once),
                  pl.BlockSpec((FFT_K1 * n2, L), lambda j: (0, 0), pipeline_mode=once),
                  pl.BlockSpec((L, ct), lambda j: (0, j)),
                  pl.BlockSpec((L, ct), lambda j: (0, j))],
        out_specs=[pl.BlockSpec((FFT_K1, n2, ct), lambda j: (0, 0, j)),
                   pl.BlockSpec((FFT_K1, n2, ct), lambda j: (0, 0, j))],
        out_shape=[jax.ShapeDtypeStruct((FFT_K1, n2, n), f32),
                   jax.ShapeDtypeStruct((FFT_K1, n2, n), f32)],
        compiler_params=_cparams("parallel"),
        name="filter_spectrum",
    )(cmat, smat, hsum, hdiff)


FFT_N1 = 16
FFT_K1 = FFT_N1 // 2 + 1
FFT_Z1 = FFT_N1 // 2
FFT_ROWS = 32


def _lincomb(terms):
    groups = {}
    for c, a in terms:
        if abs(c) > 1e-9:
            groups.setdefault(round(abs(c), 9), []).append((c > 0, a))
    total = None
    for mag, items in groups.items():
        pos = [a for p, a in items if p]
        neg = [a for p, a in items if not p]
        s = None
        for a in pos:
            s = a if s is None else s + a
        for a in neg:
            s = -a if s is None else s - a
        if mag != 1.0:
            s = s * mag
        total = s if total is None else total + s
    return total


def _hyena_fft_kernel(m2f_ref, m2i_ref, twr_ref, twi_ref, x1_ref, x2_ref, v_ref, w1_ref, w2_ref, wv_ref,
                      kr0_ref, ki0_ref, kr1_ref, ki1_ref, db_ref, o_ref, z_s, g_s, d_s, y_s, b_s):
    L, ct = z_s.shape
    n2 = L // FFT_Z1
    R = min(HY_ROWS, L)
    nt = L // R
    r1 = min(FFT_ROWS, n2)
    halo = SUBLANE_BF16
    rowi = lax.broadcasted_iota(jnp.int32, (R, 1), 0)
    ang = 2.0 * math.pi / FFT_N1
    c16 = [[math.cos(ang * a * b) for b in range(FFT_K1)] for a in range(FFT_Z1)]
    s16 = [[math.sin(ang * a * b) for b in range(FFT_K1)] for a in range(FFT_Z1)]

    def conv_into(dst, x_ref, w_ref):
        def body(t, carry):
            r0 = pl.multiple_of(t * R, R)
            x = x_ref[0, pl.ds(r0, R), :].astype(f32)
            rp = pl.multiple_of(jnp.maximum(r0 - halo, 0), halo)
            rn = pl.multiple_of(jnp.minimum(r0 + R, L - halo), halo)
            prow = jnp.where(t > 0, x_ref[0, pl.ds(rp, halo), :][halo - 1:halo, :].astype(f32), 0.0)
            nrow = jnp.where(t < nt - 1, x_ref[0, pl.ds(rn, halo), :][0:1, :].astype(f32), 0.0)
            prev = jnp.where(rowi == 0, prow, pltpu.roll(x, 1, 0))
            nxt = jnp.where(rowi == R - 1, nrow, pltpu.roll(x, R - 1, 0))
            w = w_ref[...]
            dst[pl.ds(r0, R), :] = prev * w[0:1, :] + x * w[1:2, :] + nxt * w[2:3, :]
            return carry
        lax.fori_loop(0, nt, body, 0)

    def cols(k1):
        return slice(k1 * ct, (k1 + 1) * ct)

    conv_into(z_s, v_ref, wv_ref)
    stages = ((x1_ref, w1_ref, kr0_ref, ki0_ref), (x2_ref, w2_ref, kr1_ref, ki1_ref))
    for o, (g_ref, gw_ref, kr_ref, ki_ref) in enumerate(stages):
        conv_into(g_s, g_ref, gw_ref)

        def forward1(rt, carry):
            r0 = pl.multiple_of(rt * r1, r1)
            zs = [z_s[pl.ds(a * n2 + r0, r1), :] for a in range(FFT_Z1)]
            for k1 in range(FFT_K1):
                ar = _lincomb([(c16[a][k1], zs[a]) for a in range(FFT_Z1)])
                ai = _lincomb([(-s16[a][k1], zs[a]) for a in range(FFT_Z1)])
                twr = twr_ref[pl.ds(r0, r1), cols(k1)]
                twi = twi_ref[pl.ds(r0, r1), cols(k1)]
                if ai is None:
                    pr, pi = ar * twr, -(ar * twi)
                else:
                    pr, pi = ar * twr + ai * twi, ai * twr - ar * twi
                d_s[pl.ds(r0, r1), cols(k1)] = pr.astype(bf16)
                d_s[pl.ds(n2 + r0, r1), cols(k1)] = pi.astype(bf16)
            return carry

        lax.fori_loop(0, n2 // r1, forward1, 0)

        for k1 in range(FFT_K1):
            xk = _dot(m2f_ref[...], d_s[:, cols(k1)])
            xr, xi = xk[0:n2, :], xk[n2:, :]
            kr, ki = kr_ref[k1], ki_ref[k1]
            y_s[0:n2, cols(k1)] = (xr * kr - xi * ki).astype(bf16)
            y_s[n2:2 * n2, cols(k1)] = (xr * ki + xi * kr).astype(bf16)

        for k1 in range(FFT_K1):
            bk = _dot(m2i_ref[...], y_s[:, cols(k1)])
            br, bi = bk[0:n2, :], bk[n2:, :]
            twr, twi = twr_ref[:, cols(k1)], twi_ref[:, cols(k1)]
            b_s[0:n2, cols(k1)] = br * twr - bi * twi
            b_s[n2:2 * n2, cols(k1)] = br * twi + bi * twr

        def inverse1(rt, carry):
            r0 = pl.multiple_of(rt * r1, r1)
            brs = [b_s[pl.ds(r0, r1), cols(j)] for j in range(FFT_K1)]
            bis = [b_s[pl.ds(n2 + r0, r1), cols(j)] for j in range(FFT_K1)]
            for a in range(FFT_Z1):
                terms = [(1.0, brs[0]), (-1.0 if a % 2 else 1.0, brs[FFT_K1 - 1])]
                for j in range(1, FFT_K1 - 1):
                    terms += [(2.0 * c16[a][j], brs[j]), (-2.0 * s16[a][j], bis[j])]
                rows = pl.ds(a * n2 + r0, r1)
                z = g_s[rows, :] * (_lincomb(terms) + z_s[rows, :] * db_ref[o:o + 1, :])
                if o + 1 < len(stages):
                    z_s[rows, :] = z
                else:
                    o_ref[0, rows, :] = z.astype(o_ref.dtype)
            return carry

        lax.fori_loop(0, n2 // r1, inverse1, 0)


def _hyena_tables(L):
    n_fft = 2 * L
    n2 = L // FFT_Z1
    ct = HY_CT
    idx = jnp.arange(n2, dtype=jnp.int32)
    a2 = ((idx[:, None] * idx[None, :]) % n2).astype(f32) * (2.0 * math.pi / n2)
    c2, s2 = jnp.cos(a2), jnp.sin(a2)
    m2f = jnp.block([[c2, s2], [-s2, c2]]).astype(bf16)
    m2i = jnp.block([[c2, -s2], [s2, c2]]).astype(bf16)
    at = (idx[:, None] * jnp.arange(FFT_K1, dtype=jnp.int32)[None, :]).astype(f32) * (2.0 * math.pi / n_fft)
    twr = jnp.broadcast_to(jnp.cos(at)[:, :, None], (n2, FFT_K1, ct)).reshape(n2, FFT_K1 * ct)
    twi = jnp.broadcast_to(jnp.sin(at)[:, :, None], (n2, FFT_K1, ct)).reshape(n2, FFT_K1 * ct)
    return m2f, m2i, twr, twi


def _hyena_fft(hx, conv_w, tables, kr, ki, dbias):
    B, L, _ = hx.shape
    ct = HY_CT
    nct = B_WIDTH // ct
    n2 = L // FFT_Z1
    wide = FFT_K1 * ct
    once = pl.Buffered(1)
    m2f, m2i, twr, twi = tables

    def xspec(part):
        return pl.BlockSpec((1, L, ct), lambda c, b: (b, 0, part * nct + c))

    def wspec(part):
        return pl.BlockSpec((SUBLANE, ct), lambda c, b: (0, part * nct + c))

    def kspec(order):
        return pl.BlockSpec((FFT_K1, n2, ct), lambda c, b: (0, 0, order * nct + c), pipeline_mode=once)

    def const(shape):
        return pl.BlockSpec(shape, lambda c, b: (0, 0), pipeline_mode=once)

    return pl.pallas_call(
        _hyena_fft_kernel,
        grid=(nct, B),
        in_specs=[const((2 * n2, 2 * n2)), const((2 * n2, 2 * n2)), const((n2, wide)), const((n2, wide)),
                  xspec(0), xspec(1), xspec(2), wspec(0), wspec(1), wspec(2),
                  kspec(0), kspec(0), kspec(1), kspec(1),
                  pl.BlockSpec((SUBLANE, ct), lambda c, b: (0, c))],
        out_specs=pl.BlockSpec((1, L, ct), lambda c, b: (b, 0, c)),
        out_shape=jax.ShapeDtypeStruct((B, L, B_WIDTH), bf16),
        scratch_shapes=[pltpu.VMEM((L, ct), f32), pltpu.VMEM((L, ct), f32),
                        pltpu.VMEM((2 * n2, wide), bf16), pltpu.VMEM((2 * n2, wide), bf16),
                        pltpu.VMEM((2 * n2, wide), f32)],
        compiler_params=_cparams("parallel", "parallel"),
        name="hyena_fft",
    )(m2f, m2i, twr, twi, hx, hx, hx, conv_w, conv_w, conv_w, kr, ki, kr, ki, dbias)


RT_E1, RT_E2, RT_W1, RT_W2, RT_R1, RT_R2 = 0, 1, 2, 3, 4, 5
ROUTE_TM = ROW_TM
PROJ_PARTS = 2
COMBINE_PARTS = 4
EXPERT_TM = 512
SC_WINDOW = 128


def _pack_bf16_pairs(x):
    n = x.shape[1] // 2
    bits = lax.bitcast_convert_type(x.astype(bf16).astype(f32), jnp.uint32)
    return bits[:, :n] | (bits[:, n:] >> 16)


def _unpack_bf16_pairs(p):
    hi = lax.bitcast_convert_type(p & jnp.uint32(0xFFFF0000), f32)
    lo = lax.bitcast_convert_type(p << 16, f32)
    return hi, lo


ROUTER_ROWS = 40


def _route_topk(lt):
    row = lax.broadcasted_iota(jnp.int32, lt.shape, 0)
    neg = -jnp.inf
    big = ROUTER_LANES
    gl = jnp.where(row < N_GROUPS, lt, neg)
    gmax = jnp.max(gl, axis=0, keepdims=True)
    gsel = jnp.min(jnp.where(gl == gmax, row, big), axis=0, keepdims=True)
    psel = 1.0 / jnp.sum(jnp.exp(gl - gmax), axis=0, keepdims=True)
    first = N_GROUPS + gsel * EXPERTS_PER_GROUP
    el = jnp.where((row >= first) & (row < first + EXPERTS_PER_GROUP), lt, neg)
    t1 = jnp.max(el, axis=0, keepdims=True)
    i1 = jnp.min(jnp.where(el == t1, row, big), axis=0, keepdims=True)
    el2 = jnp.where(row == i1, neg, el)
    t2 = jnp.max(el2, axis=0, keepdims=True)
    i2 = jnp.min(jnp.where(el2 == t2, row, big), axis=0, keepdims=True)
    ex = jnp.exp(t2 - t1)
    w1 = psel / (1.0 + ex)
    w2 = ex * w1
    return i1 - N_GROUPS, i2 - N_GROUPS, w1, w2


def _proj_route_kernel(x_ref, *refs, n_in):
    a_refs = refs[0:n_in]
    w_refs = refs[n_in:2 * n_in]
    g_ref, wr_ref, up_ref, xo_ref, hpa_ref, hpb_ref, rt_ref, rtt_ref, cnt_ref = refs[2 * n_in:]
    tm = xo_ref.shape[0]
    rp = tm // PROJ_PARTS
    parts = [slice(p * rp, (p + 1) * rp) for p in range(PROJ_PARTS)]
    xs = []
    for r in parts:
        x = x_ref[r, :]
        for a_ref, w_ref in zip(a_refs, w_refs):
            x = x + _dot(a_ref[r, :], w_ref[...])
        xo_ref[r, :] = x
        xs.append(x)
    hs = [_rms(x, g_ref[...]) for x in xs]
    for r, h in zip(parts, hs):
        hp = _pack_bf16_pairs(h)
        quarter = hp.shape[1] // 2
        hpa_ref[r, :] = hp[:, :quarter]
        hpb_ref[r, :] = hp[:, quarter:]
    tops = []
    for h in hs:
        h_hi = h.astype(bf16)
        h_lo = (h - h_hi.astype(f32)).astype(bf16)
        both = _dot(h_hi, wr_ref[...])
        logits = both[:, :ROUTER_LANES] + both[:, ROUTER_LANES:] + _dot(h_lo, wr_ref[:, :ROUTER_LANES])
        tops.append(_route_topk(logits.T[0:ROUTER_ROWS, :]))
    row = lax.broadcasted_iota(jnp.int32, (ROUTER_LANES, rp), 0)
    onehots = [jnp.where((row == e1) | (row == e2), 1.0, 0.0) for e1, e2, _, _ in tops]
    onehot16 = jnp.concatenate(onehots, axis=1).astype(bf16)
    before = _dot(onehot16, up_ref[...])
    for p, (r, (e1, e2, w1, w2)) in enumerate(zip(parts, tops)):
        bp = before[:, r]
        r1 = jnp.sum(jnp.where(row == e1, bp, 0.0), axis=0, keepdims=True)
        r2 = jnp.sum(jnp.where(row == e2, bp, 0.0), axis=0, keepdims=True)
        rec = jnp.zeros((ROUTER_LANES, rp), f32)
        for ln, val in ((RT_E1, e1.astype(f32)), (RT_E2, e2.astype(f32)), (RT_W1, w1), (RT_W2, w2),
                        (RT_R1, r1), (RT_R2, r2)):
            rec = jnp.where(row == ln, val, rec)
        rtt_ref[:, r] = rec[0:rtt_ref.shape[0], :]
        rt_ref[r, :] = rec.T
    ones = jnp.ones((cnt_ref.shape[1], tm), bf16)
    cnt_ref[0] = _dot_nt(ones, onehot16)


def _proj_route(x, acts, ws, gain, w_router):
    T, D = x.shape
    tm = ROUTE_TM
    n_in = len(acts)
    up = jnp.asarray(np.triu(np.ones((tm, tm), np.float32), 1), dtype=bf16)
    in_specs = [pl.BlockSpec((tm, D), lambda i: (i, 0))]
    in_specs += [pl.BlockSpec((tm, a.shape[1]), lambda i: (i, 0)) for a in acts]
    in_specs += [pl.BlockSpec(w.shape, lambda i: (0, 0)) for w in ws]
    in_specs += [pl.BlockSpec((1, D), lambda i: (0, 0)),
                 pl.BlockSpec((D, 2 * ROUTER_LANES), lambda i: (0, 0)),
                 pl.BlockSpec((tm, tm), lambda i: (0, 0))]
    return pl.pallas_call(
        functools.partial(_proj_route_kernel, n_in=n_in),
        grid=(T // tm,),
        in_specs=in_specs,
        out_specs=[pl.BlockSpec((tm, D), lambda i: (i, 0)),
                   pl.BlockSpec((tm, D // 4), lambda i: (i, 0)),
                   pl.BlockSpec((tm, D // 4), lambda i: (i, 0)),
                   pl.BlockSpec((tm, ROUTER_LANES), lambda i: (i, 0)),
                   pl.BlockSpec((SUBLANE, tm), lambda i: (0, i)),
                   pl.BlockSpec((1, SUBLANE, ROUTER_LANES), lambda i: (i, 0, 0))],
        out_shape=[jax.ShapeDtypeStruct((T, D), f32),
                   jax.ShapeDtypeStruct((T, D // 4), jnp.uint32),
                   jax.ShapeDtypeStruct((T, D // 4), jnp.uint32),
                   jax.ShapeDtypeStruct((T, ROUTER_LANES), f32),
                   jax.ShapeDtypeStruct((SUBLANE, T), f32),
                   jax.ShapeDtypeStruct((T // tm, SUBLANE, ROUTER_LANES), f32)],
        compiler_params=_cparams("parallel"),
        name="proj_route",
    )(x, *acts, *ws, gain.reshape(1, D), w_router, up)


def _moe_rows(T):
    return 2 * T + N_EXPERTS * EXPERT_TM


def _moe_plan(route_t, counts):
    T = route_t.shape[1]
    tm = ROUTE_TM
    cnt = counts[:, 0, :N_EXPERTS].astype(jnp.int32)
    total = jnp.sum(cnt, axis=0)
    padded = ((total + EXPERT_TM - 1) // EXPERT_TM) * EXPERT_TM
    ends = jnp.cumsum(padded)
    tile_base = (ends - padded)[None, :] + jnp.cumsum(cnt, axis=0) - cnt
    base = jnp.broadcast_to(tile_base.T[:, :, None], (N_EXPERTS, T // tm, tm)).reshape(N_EXPERTS, T)
    ids = jnp.arange(N_EXPERTS, dtype=jnp.int32)[:, None]
    dest = []
    for e_row, r_row in ((RT_E1, RT_R1), (RT_E2, RT_R2)):
        e = route_t[e_row].astype(jnp.int32)
        dest.append(jnp.sum(jnp.where(ids == e[None, :], base, 0), axis=0) + route_t[r_row].astype(jnp.int32))
    n_tiles = _moe_rows(T) // EXPERT_TM
    n_used = ends[-1] // EXPERT_TM
    starts = jnp.arange(n_tiles, dtype=jnp.int32) * EXPERT_TM
    tile_expert = jnp.sum((starts[:, None] >= ends[None, :]).astype(jnp.int32), axis=1)
    last_expert = jnp.sum((jnp.maximum(n_used - 1, 0) * EXPERT_TM >= ends).astype(jnp.int32))
    tile_expert = jnp.minimum(tile_expert, last_expert)
    tiles = jnp.arange(n_tiles, dtype=jnp.int32)
    prev_expert = jnp.concatenate([jnp.full((1,), -1, jnp.int32), tile_expert[:-1]])
    seg_first = ((tile_expert != prev_expert) & (tiles < n_used)).astype(jnp.int32)
    seg_slot = (jnp.cumsum(seg_first) - 1) % 2
    seg_end = jnp.sum(jnp.where(ids == tile_expert[None, :], (ends // EXPERT_TM)[:, None], 0), axis=0)
    nxt = jnp.where(seg_end < n_used, tile_expert[jnp.minimum(seg_end, n_tiles - 1)], -1)
    plan = (tile_expert, seg_first, seg_slot.astype(jnp.int32), nxt.astype(jnp.int32),
            n_used.reshape(1).astype(jnp.int32))
    return dest, plan


def _sc_mesh():
    return plsc.VectorSubcoreMesh(core_axis_name="core", subcore_axis_name="subcore")


def _sc_scatter_pair(rows, d0, d1, n_out):
    T, W = rows.shape
    win = SC_WINDOW

    @pl.kernel(out_type=jax.ShapeDtypeStruct((n_out, W), rows.dtype), mesh=_sc_mesh(), name="moe_sc_scatter")
    def scatter(x_hbm, i0_hbm, i1_hbm, o_hbm):
        def body(x_vmem, i0_vmem, i1_vmem):
            pltpu.sync_copy(x_vmem, o_hbm.at[i0_vmem.at[0]])
            pltpu.sync_copy(x_vmem, o_hbm.at[i1_vmem.at[0]])

        pltpu.emit_pipeline(
            body, grid=(T // win,),
            in_specs=[pl.BlockSpec((win, W), index_map=lambda i: (i, 0)),
                      pl.BlockSpec((1, win), index_map=lambda i: (0, i)),
                      pl.BlockSpec((1, win), index_map=lambda i: (0, i))],
            out_specs=[],
            core_axis_name=("core", "subcore"), dimension_semantics=(pltpu.PARALLEL,),
        )(x_hbm, i0_hbm, i1_hbm)

    return scatter(rows, d0.reshape(1, T), d1.reshape(1, T))


def _sc_gather(table, idx):
    n = idx.shape[0]
    W = table.shape[1]
    win = SC_WINDOW

    @pl.kernel(out_type=jax.ShapeDtypeStruct((n, W), table.dtype), mesh=_sc_mesh(), name="moe_sc_gather")
    def gather(t_hbm, i_hbm, o_hbm):
        def body(i_vmem, o_vmem):
            pltpu.sync_copy(t_hbm.at[i_vmem.at[0]], o_vmem)

        pltpu.emit_pipeline(
            body, grid=(n // win,),
            in_specs=[pl.BlockSpec((1, win), index_map=lambda i: (0, i))],
            out_specs=[pl.BlockSpec((win, W), index_map=lambda i: (i, 0))],
            core_axis_name=("core", "subcore"), dimension_semantics=(pltpu.PARALLEL,),
        )(i_hbm, o_hbm)

    return gather(table, idx.reshape(1, n))


def _expert_kernel(te_ref, first_ref, slot_ref, nxt_ref, nu_ref, xa_ref, xb_ref, wg_hbm, wu_hbm, wd_hbm,
                   ya_ref, yb_ref, wg_s, wu_s, wd_s, wg16, wu16, wd16, sem):
    i = pl.program_id(0)
    active = i < nu_ref[0]

    def weight_copies(expert, s):
        return (pltpu.make_async_copy(wg_hbm.at[expert], wg_s.at[s], sem.at[s, 0]),
                pltpu.make_async_copy(wu_hbm.at[expert], wu_s.at[s], sem.at[s, 1]),
                pltpu.make_async_copy(wd_hbm.at[expert], wd_s.at[s], sem.at[s, 2]))

    @pl.when(i == 0)
    def _():
        for c in weight_copies(te_ref[0], slot_ref[0]):
            c.start()

    @pl.when(active & (first_ref[i] == 1))
    def _():
        s = slot_ref[i]
        for c in weight_copies(te_ref[i], s):
            c.wait()

        @pl.when(nxt_ref[i] >= 0)
        def _():
            for c in weight_copies(nxt_ref[i], 1 - s):
                c.start()

        wg16[...] = wg_s[s].astype(bf16)
        wu16[...] = wu_s[s].astype(bf16)
        wd16[...] = wd_s[s].astype(bf16)

    @pl.when(active)
    def _():
        hi, lo = _unpack_bf16_pairs(jnp.concatenate([xa_ref[...], xb_ref[...]], axis=1))
        x = jnp.concatenate([hi, lo], axis=1).astype(bf16)
        gate = _dot(x, wg16[...])
        up = _dot(x, wu16[...])
        hg = (gate * _sigmoid(gate) * up).astype(bf16)
        y = _pack_bf16_pairs(_dot(hg, wd16[...]))
        quarter = y.shape[1] // 2
        ya_ref[...] = y[:, :quarter]
        yb_ref[...] = y[:, quarter:]

    @pl.when(jnp.logical_not(active))
    def _():
        ya_ref[...] = jnp.zeros(ya_ref.shape, ya_ref.dtype)
        yb_ref[...] = jnp.zeros(yb_ref.shape, yb_ref.dtype)


def _experts(plan, layer, xa, xb, wg, wu, wd):
    tile_expert, seg_first, seg_slot, nxt, n_used = plan
    off = layer * N_EXPERTS
    n_rows, quarter = xa.shape
    D = 4 * quarter
    tm = EXPERT_TM
    row_spec = pl.BlockSpec((tm, quarter), lambda i, *_: (i, 0))
    used_spec = pl.BlockSpec((tm, quarter), lambda i, te, first, slot, nxt, nu: (jnp.maximum(jnp.minimum(i, nu[0] - 1), 0), 0))
    grid_spec = pltpu.PrefetchScalarGridSpec(
        num_scalar_prefetch=5,
        grid=(n_rows // tm,),
        in_specs=[used_spec, used_spec,
                  pl.BlockSpec(memory_space=pl.ANY), pl.BlockSpec(memory_space=pl.ANY), pl.BlockSpec(memory_space=pl.ANY)],
        out_specs=[row_spec, row_spec],
        scratch_shapes=[pltpu.VMEM((2, D, D_EXPERT), wg.dtype), pltpu.VMEM((2, D, D_EXPERT), wu.dtype),
                        pltpu.VMEM((2, D_EXPERT, D), wd.dtype),
                        pltpu.VMEM((D, D_EXPERT), bf16), pltpu.VMEM((D, D_EXPERT), bf16), pltpu.VMEM((D_EXPERT, D), bf16),
                        pltpu.SemaphoreType.DMA((2, 3))])
    return pl.pallas_call(
        _expert_kernel,
        grid_spec=grid_spec,
        out_shape=[jax.ShapeDtypeStruct((n_rows, quarter), jnp.uint32)] * 2,
        compiler_params=_cparams("arbitrary"),
        name="moe_experts",
    )(tile_expert + off, seg_first, seg_slot, jnp.where(nxt >= 0, nxt + off, -1), n_used, xa, xb, wg, wu, wd)


def _combine_kernel(x_ref, rt_ref, g_ref, a1_ref, b1_ref, a2_ref, b2_ref, *rest, final):
    half = x_ref.shape[1] // 2
    rp = x_ref.shape[0] // COMBINE_PARTS
    for p in range(COMBINE_PARTS):
        r = slice(p * rp, (p + 1) * rp)
        h1, l1 = _unpack_bf16_pairs(jnp.concatenate([a1_ref[r, :], b1_ref[r, :]], axis=1))
        h2, l2 = _unpack_bf16_pairs(jnp.concatenate([a2_ref[r, :], b2_ref[r, :]], axis=1))
        w1 = rt_ref[r, RT_W1:RT_W1 + 1]
        w2 = rt_ref[r, RT_W2:RT_W2 + 1]
        out = jnp.concatenate([x_ref[r, 0:half] + w1 * h1 + w2 * h2, x_ref[r, half:] + w1 * l1 + w2 * l2], axis=1)
        if final:
            (o_ref,) = rest
            o_ref[r, :] = _rms(out, g_ref[...])
        else:
            w_ref, o_ref, p_ref = rest
            o_ref[r, :] = out
            p_ref[r, :] = _dot(_rms(out, g_ref[...]).astype(bf16), w_ref[...]).astype(p_ref.dtype)


def _combine(x, route, ga, gb, gain, w_next=None):
    T, D = x.shape
    tm = ROUTE_TM
    n = T // tm
    quarter = D // 4
    final = w_next is None
    slot1 = pl.BlockSpec((tm, quarter), lambda i: (i, 0))
    slot2 = pl.BlockSpec((tm, quarter), lambda i: (i + n, 0))
    row_spec = pl.BlockSpec((tm, D), lambda i: (i, 0))
    in_specs = [row_spec, pl.BlockSpec((tm, ROUTER_LANES), lambda i: (i, 0)), pl.BlockSpec((1, D), lambda i: (0, 0)),
                slot1, slot1, slot2, slot2]
    args = [x, route, gain.reshape(1, D), ga, gb, ga, gb]
    out_specs = [row_spec]
    out_shape = [jax.ShapeDtypeStruct((T, D), f32)]
    if not final:
        n_next = w_next.shape[1]
        in_specs.append(pl.BlockSpec((D, n_next), lambda i: (0, 0)))
        args.append(w_next)
        out_specs.append(pl.BlockSpec((tm, n_next), lambda i: (i, 0)))
        out_shape.append(jax.ShapeDtypeStruct((T, n_next), bf16))
    outs = pl.pallas_call(
        functools.partial(_combine_kernel, final=final),
        grid=(n,),
        in_specs=in_specs,
        out_specs=out_specs,
        out_shape=out_shape,
        compiler_params=_cparams("parallel"),
        name="moe_combine",
    )(*args)
    return outs[0] if final else outs


def _moe(layer, x, hpa, hpb, route, route_t, counts, wg, wu, wd, gain, w_next=None):
    (d0, d1), plan = _moe_plan(route_t, counts)
    n_rows = _moe_rows(x.shape[0])
    xa = _sc_scatter_pair(hpa, d0, d1, n_rows)
    xb = _sc_scatter_pair(hpb, d0, d1, n_rows)
    ya, yb = _experts(plan, layer, xa, xb, wg, wu, wd)
    dcat = jnp.concatenate([d0, d1])
    return _combine(x, route, _sc_gather(ya, dcat), _sc_gather(yb, dcat), gain, w_next)


SWA_LOCKSTEP = 2


def _swa_bias_tables():
    blk = C_BLOCK
    qi = np.arange(blk)[:, None]
    kj = np.arange(3 * blk)[None, :]
    dist = np.abs(blk + qi - kj)
    tabs = np.full((3, C_KV_HEADS, C_GROUP * blk, 3 * blk), -np.inf, np.float32)
    for case in range(3):
        ok = dist <= WINDOW
        if case == 0:
            ok = ok & (kj >= blk)
        if case == 2:
            ok = ok & (kj < 2 * blk)
        for hh in range(C_HEADS):
            slope = 2.0 ** (-8.0 * (hh + 1) / C_HEADS)
            t = np.where(ok, -slope * LOG2E * dist, -np.inf)
            kvh, g = divmod(hh, C_GROUP)
            tabs[case, kvh, g * blk:(g + 1) * blk] = t
    return jnp.asarray(tabs)


def _swa_kernel(sink_ref, q_ref, kp_ref, kc_ref, kn_ref, vp_ref, vc_ref, vn_ref, bias_ref, o_ref):
    blk = C_BLOCK
    hd = C_HEAD_DIM
    kcat = jnp.concatenate([kp_ref[0], kc_ref[0], kn_ref[0]], axis=0)
    vcat = jnp.concatenate([vp_ref[0], vc_ref[0], vn_ref[0]], axis=0)
    q = q_ref[0]
    rblk = lax.broadcasted_iota(jnp.int32, (C_GROUP * blk, 1), 0) // blk
    lane = lax.broadcasted_iota(jnp.int32, (3 * blk, LANE - hd), 1)
    ones_pad = jnp.where(lane == 0, 1.0, 0.0).astype(bf16)
    q4s, sinks = [], []
    for kvh in range(C_KV_HEADS):
        q4s.append(jnp.concatenate(
            [q[:, (kvh * C_GROUP + g) * hd:(kvh * C_GROUP + g + 1) * hd] for g in range(C_GROUP)], axis=0))
        sk = jnp.zeros((C_GROUP * blk, 1), f32)
        for g in range(C_GROUP):
            sk = jnp.where(rblk == g, sink_ref[kvh * C_GROUP + g] * LOG2E, sk)
        sinks.append(sk)
    outs = []
    for h0 in range(0, C_KV_HEADS, SWA_LOCKSTEP):
        hs = range(h0, h0 + SWA_LOCKSTEP)
        ss = [_dot_nt(q4s[h], kcat[:, h * hd:(h + 1) * hd]) + bias_ref[0, h] for h in hs]
        ms = [jnp.maximum(jnp.max(s, axis=-1, keepdims=True), sinks[h]) for s, h in zip(ss, hs)]
        ps = [jnp.exp2(s - m).astype(bf16) for s, m in zip(ss, ms)]
        pvs = [_dot(p, jnp.concatenate([vcat[:, h * hd:(h + 1) * hd], ones_pad], axis=1)) for p, h in zip(ps, hs)]
        for pv, m, h in zip(pvs, ms, hs):
            den = pv[:, hd:hd + 1] + jnp.exp2(sinks[h] - m)
            o4 = pv[:, 0:hd] / den
            outs += [o4[g * blk:(g + 1) * blk, :] for g in range(C_GROUP)]
    o_ref[0] = jnp.concatenate(outs, axis=1).astype(o_ref.dtype)


def _swa(qkv, sinks):
    B, L, _ = qkv.shape
    nb = L // C_BLOCK
    dq = C_HEADS * C_HEAD_DIM
    dkv = C_KV_HEADS * C_HEAD_DIM
    k_col = dq // dkv
    v_col = k_col + 1

    def kv_spec(col, shift):
        return pl.BlockSpec((1, C_BLOCK, dkv),
                            lambda n, b: (b, jnp.clip(n + shift, 0, nb - 1), col))

    def edge_case(n, b):
        return (jnp.where(n == 0, 0, jnp.where(n == nb - 1, 2, 1)), 0, 0, 0)

    return pl.pallas_call(
        _swa_kernel,
        grid=(nb, B),
        in_specs=[pl.BlockSpec(memory_space=pltpu.SMEM),
                  pl.BlockSpec((1, C_BLOCK, dq), lambda n, b: (b, n, 0)),
                  kv_spec(k_col, -1), kv_spec(k_col, 0), kv_spec(k_col, 1),
                  kv_spec(v_col, -1), kv_spec(v_col, 0), kv_spec(v_col, 1),
                  pl.BlockSpec((1, C_KV_HEADS, C_GROUP * C_BLOCK, 3 * C_BLOCK), edge_case)],
        out_specs=pl.BlockSpec((1, C_BLOCK, dq), lambda n, b: (b, n, 0)),
        out_shape=jax.ShapeDtypeStruct((B, L, dq), bf16),
        compiler_params=_cparams("parallel", "parallel"),
        name="swa",
    )(sinks, qkv, qkv, qkv, qkv, qkv, qkv, qkv, _swa_bias_tables())


def _hyena_filters(L, w1, b1, w2, b2, w3, b3, w4, freq):
    hp = lax.Precision.HIGHEST
    t = jnp.linspace(0.0, 1.0, L, dtype=f32)[:, None]
    wpos = (2.0 * math.pi / L) * jnp.arange(L, dtype=f32)[:, None]
    bands = jnp.linspace(1e-4, HY_BANDS - 1, HY_BANDS, dtype=f32)[None, :]
    feats = jnp.concatenate([t, jnp.cos(bands * wpos), -jnp.sin(bands * wpos)], axis=-1)
    h = jnp.sin(freq[0] * (jnp.dot(feats, w1, precision=hp) + b1))
    h = jnp.sin(freq[1] * (jnp.dot(h, w2, precision=hp) + b2))
    h = jnp.sin(freq[2] * (jnp.dot(h, w3, precision=hp) + b3))
    h = jnp.dot(h, w4, precision=hp).reshape(L, HY_ORDER, 2, B_WIDTH)
    deltas = jnp.abs(jnp.linspace(HY_MIN_DECAY, HY_MAX_DECAY, B_WIDTH, dtype=f32))
    window = jnp.exp(-t * deltas[None, :])
    return h * window[:, None, None, :]


def _pad_rows(w):
    return jnp.pad(w, ((0, SUBLANE - w.shape[0]), (0, 0)))


def _even_in_weights(w_in):
    hk = A_HEADS * A_DK
    q = w_in[:, 0:hk].reshape(D_MODEL, A_HEADS, A_DK)
    k = w_in[:, hk:2 * hk].reshape(D_MODEL, A_HEADS, A_DK)
    v = w_in[:, 2 * hk:A_QKV].reshape(D_MODEL, A_HEADS, A_DV)
    z = w_in[:, A_QKV:A_QKV + A_WIDTH].reshape(D_MODEL, A_HEADS, A_DV)
    qkvz = jnp.concatenate([q, k, v, z], axis=-1).reshape(D_MODEL, A_HEADS * 4 * A_DK)
    ab = w_in[:, A_QKV + A_WIDTH:A_QKV + A_WIDTH + 4 * A_HEADS].reshape(D_MODEL, 4, A_HEADS)
    ab = jnp.transpose(ab, (0, 2, 1))
    ab = jnp.pad(ab.reshape(D_MODEL, 4 * A_HEADS), ((0, 0), (0, LANE - 4 * A_HEADS)))
    hx = w_in[:, A_QKV + A_WIDTH + 4 * A_HEADS:]
    return jnp.concatenate([qkvz, ab, hx], axis=1).astype(bf16)


def _router_weights(w_group, w_expert):
    w = jnp.concatenate([w_group, w_expert], axis=1)
    w = jnp.pad(w, ((0, 0), (0, ROUTER_LANES - w.shape[1])))
    hi = w.astype(bf16)
    lo = (w - hi.astype(f32)).astype(bf16)
    return jnp.concatenate([hi, lo], axis=1)


def kernel(x_prompt, x_sample, norm_mix, norm_ffn, norm_final, ev_w_in, ev_conv_a, ev_alog_f, ev_alog_b, ev_dtb_f, ev_dtb_b, ev_onorm, ev_conv_b, hy_w1, hy_b1, hy_w2, hy_b2, hy_w3, hy_b3, hy_w4, hy_freq, hy_dbias, ev_w_out, od_w_qkv, od_sinks, od_w_out, moe_w_group, moe_w_expert, moe_w_gate, moe_w_up, moe_w_down):
    L = x_prompt.shape[1]
    n_layers = moe_w_gate.shape[0]
    wg = moe_w_gate.reshape(n_layers * N_EXPERTS, D_MODEL, D_EXPERT)
    wu = moe_w_up.reshape(n_layers * N_EXPERTS, D_MODEL, D_EXPERT)
    wd = moe_w_down.reshape(n_layers * N_EXPERTS, D_EXPERT, D_MODEL)

    w_in = _even_in_weights(ev_w_in[0])
    n_qkvz = A_HEADS * 4 * A_DK
    n_ab = A_HEADS * LANE
    n_hx = 3 * B_WIDTH
    hk = A_HEADS * A_DK
    ca = ev_conv_a[0]
    conv_a = jnp.concatenate([ca[0:hk].reshape(A_HEADS, A_DK, 3), ca[hk:2 * hk].reshape(A_HEADS, A_DK, 3),
                              ca[2 * hk:].reshape(A_HEADS, A_DV, 3)], axis=1)
    conv_a = jnp.pad(jnp.transpose(conv_a, (0, 2, 1)), ((0, 0), (0, SUBLANE - 3), (0, 0)))
    gate_sc = jnp.stack([ev_alog_f[0], ev_alog_b[0], ev_dtb_f[0], ev_dtb_b[0]], axis=1)
    filt = _hyena_filters(L, hy_w1[0], hy_b1[0], hy_w2[0], hy_b2[0], hy_w3[0], hy_b3[0], hy_w4[0], hy_freq[0])
    hsum = (filt[:, :, 0] + filt[:, :, 1]).reshape(L, HY_ORDER * B_WIDTH)
    hdiff = (filt[:, :, 0] - filt[:, :, 1]).reshape(L, HY_ORDER * B_WIDTH)
    kr, ki = _filter_spectrum(hsum, hdiff)
    fft_tables = _hyena_tables(L)
    conv_b = _pad_rows(ev_conv_b[0].T)
    dbias = _pad_rows(hy_dbias[0])
    w_out = ev_w_out[0].astype(bf16)
    routers = [_router_weights(moe_w_group[l], moe_w_expert[l]) for l in range(n_layers)]
    n_qkv = od_w_qkv.shape[-1]
    dq = C_HEADS * C_HEAD_DIM
    w_qkv = jnp.concatenate([od_w_qkv[0][:, :dq] * (LOG2E * C_HEAD_DIM ** -0.5), od_w_qkv[0][:, dq:]], axis=1)
    w_qkv = w_qkv.astype(bf16)
    w_att_out = od_w_out[0].astype(bf16)

    def trunk(xb):
        B = xb.shape[0]
        T = B * L
        x = xb.reshape(T, D_MODEL)
        qkvz, abh, hx = _even_in_proj(x, norm_mix[0], w_in)
        o_a = _deltanet(qkvz.reshape(B, L, n_qkvz), abh.reshape(B, L, n_ab), conv_a, gate_sc, ev_onorm[0])
        o_b = _hyena_fft(hx.reshape(B, L, n_hx), conv_b, fft_tables, kr, ki, dbias)
        x, hpa, hpb, route, route_t, counts = _proj_route(
            x, (o_a.reshape(T, A_WIDTH), o_b.reshape(T, B_WIDTH)), (w_out[:A_WIDTH], w_out[A_WIDTH:]),
            norm_ffn[0], routers[0])
        x, qkv = _moe(0, x, hpa, hpb, route, route_t, counts, wg, wu, wd, norm_mix[1], w_qkv)
        o_c = _swa(qkv.reshape(B, L, n_qkv), od_sinks[0])
        x, hpa, hpb, route, route_t, counts = _proj_route(
            x, (o_c.reshape(T, C_HEADS * C_HEAD_DIM),), (w_att_out,), norm_ffn[1], routers[1])
        y = _moe(1, x, hpa, hpb, route, route_t, counts, wg, wu, wd, norm_final)
        return y.reshape(B, L, D_MODEL)

    return (trunk(x_prompt), trunk(x_sample))
```

```python
import functools
import math

import jax
import jax.numpy as jnp
import numpy as np
from jax import lax
from jax.experimental import pallas as pl
from jax.experimental.pallas import tpu as pltpu
from jax.experimental.pallas import tpu_sc as plsc

f32 = jnp.float32
bf16 = jnp.bfloat16

EPS = 1e-6
D_MODEL = 1024

A_HEADS = 4
A_DK = 128
A_DV = 128
A_WIDTH = A_HEADS * A_DV
A_QKV = 2 * A_HEADS * A_DK + A_HEADS * A_DV
DELTA_CHUNK = 128
INV_BASE = 8
DELTA_GROUP = 8
DELTA_WAVES = 2

B_WIDTH = D_MODEL - A_WIDTH
HY_ORDER = 2
HY_EMB = 33
HY_BANDS = (HY_EMB - 1) // 2
HY_TARGET = 1e-2
HY_MIN_DECAY = math.log(HY_TARGET) / 1.5
HY_MAX_DECAY = math.log(HY_TARGET) / 0.3
HY_CT = 256
HY_ROWS = 256

C_HEADS = 16
C_KV_HEADS = 4
C_HEAD_DIM = 64
C_GROUP = C_HEADS // C_KV_HEADS
WINDOW = 128
C_BLOCK = 128

N_GROUPS = 4
EXPERTS_PER_GROUP = 8
N_EXPERTS = N_GROUPS * EXPERTS_PER_GROUP
D_EXPERT = 256
ROUTER_LANES = 128

LANE = 128
SUBLANE = 8
SUBLANE_BF16 = 16
ROW_TM = 512
VMEM_LIMIT = 56 * 1024 * 1024
LOG2E = 1.4426950408889634


def _cparams(*sem):
    return pltpu.CompilerParams(dimension_semantics=sem, vmem_limit_bytes=VMEM_LIMIT)


def _dot(a, b):
    return jnp.dot(a, b, preferred_element_type=f32)


def _dot_nt(a, b):
    return lax.dot_general(a, b, (((1,), (1,)), ((), ())), preferred_element_type=f32)


def _dot_tn(a, b):
    return lax.dot_general(a, b, (((0,), (0,)), ((), ())), preferred_element_type=f32)


def _sigmoid(x):
    return 1.0 / (1.0 + jnp.exp(-x))


def _rms(x, g):
    return x * lax.rsqrt(jnp.mean(x * x, axis=-1, keepdims=True) + EPS) * g


def _even_in_kernel(x_ref, g_ref, w_ref, qkvz_ref, ab_ref, hx_ref):
    h = _rms(x_ref[...], g_ref[...]).astype(bf16)
    n_qkvz = qkvz_ref.shape[1]
    qkvz_ref[...] = _dot(h, w_ref[:, 0:n_qkvz]).astype(qkvz_ref.dtype)
    ab = _dot(h, w_ref[:, n_qkvz:n_qkvz + LANE])
    for hd in range(A_HEADS):
        ab_ref[:, hd * LANE:(hd + 1) * LANE] = ab if hd == 0 else pltpu.roll(ab, LANE - 4 * hd, 1)
    hx_ref[...] = _dot(h, w_ref[:, n_qkvz + LANE:]).astype(hx_ref.dtype)


def _even_in_proj(x, gain, w):
    T, D = x.shape
    tm = ROW_TM
    n_qkvz = A_HEADS * 4 * A_DK
    n_ab = A_HEADS * LANE
    n_hx = 3 * B_WIDTH
    return pl.pallas_call(
        _even_in_kernel,
        grid=(T // tm,),
        in_specs=[pl.BlockSpec((tm, D), lambda i: (i, 0)),
                  pl.BlockSpec((1, D), lambda i: (0, 0)),
                  pl.BlockSpec(w.shape, lambda i: (0, 0))],
        out_specs=[pl.BlockSpec((tm, n_qkvz), lambda i: (i, 0)),
                   pl.BlockSpec((tm, n_ab), lambda i: (i, 0)),
                   pl.BlockSpec((tm, n_hx), lambda i: (i, 0))],
        out_shape=[jax.ShapeDtypeStruct((T, n_qkvz), bf16),
                   jax.ShapeDtypeStruct((T, n_ab), f32),
                   jax.ShapeDtypeStruct((T, n_hx), bf16)],
        compiler_params=_cparams("parallel"),
        name="even_in_proj",
    )(x, gain.reshape(1, D), w)


MK_LOWER, MK_UPPER, MK_BASE, MK_LEVEL0 = 0, 1, 2, 3
N_LEVELS = int(math.log2(DELTA_CHUNK // INV_BASE))
MK_EYE = MK_LEVEL0 + N_LEVELS
N_MASKS = MK_EYE + 1


def _delta_masks():
    c = np.arange(DELTA_CHUNK)[:, None]
    s = np.arange(DELTA_CHUNK)[None, :]
    m = [s <= c, s >= c, (c // INV_BASE == s // INV_BASE) & (c != s)]
    b = INV_BASE
    while b < DELTA_CHUNK:
        m.append((c // (2 * b) == s // (2 * b)) & (c // b != s // b))
        b *= 2
    m.append(c == s)
    m = np.stack(m).astype(np.float32)
    return jnp.asarray(m), jnp.asarray(m, dtype=bf16)


def _tri_inverse(mats, mk_ref, mk16_ref):
    ps = [a * mk_ref[MK_BASE] for a in mats]
    ts = [mk_ref[MK_EYE] - p for p in ps]
    n = 2
    while n < INV_BASE:
        p16s = [p.astype(bf16) for p in ps]
        ps = [_dot(p16, p16) for p16 in p16s]
        ts = [t + _dot(t.astype(bf16), p.astype(bf16)) for t, p in zip(ts, ps)]
        n *= 2
    a16s = [a.astype(bf16) for a in mats]
    t16s = [t.astype(bf16) for t in ts]
    for lvl in range(N_LEVELS):
        xs = [_dot(a16 * mk16_ref[MK_LEVEL0 + lvl], t16) for a16, t16 in zip(a16s, t16s)]
        t16s = [t16 - _dot(t16, x.astype(bf16)).astype(bf16) for t16, x in zip(t16s, xs)]
    return t16s


def _deltanet_kernel(sc_ref, x_ref, ab_ref, cw_ref, on_ref, mk_ref, mk16_ref, o_ref, mp_s, nr_s, gl_s, o_s):
    hd = pl.program_id(1)
    L = x_ref.shape[1]
    C = DELTA_CHUNK
    nchunk = L // C
    group = math.gcd(DELTA_GROUP, nchunk)
    nqkv = 3 * A_DK
    halo = SUBLANE_BF16

    lane = lax.broadcasted_iota(jnp.int32, (1, LANE), 1)
    alog = jnp.where(lane == 2, sc_ref[hd, 0], jnp.where(lane == 3, sc_ref[hd, 1], 0.0))
    dtb = jnp.where(lane == 2, sc_ref[hd, 2], jnp.where(lane == 3, sc_ref[hd, 3], 0.0))
    neg_a = jnp.where((lane == 2) | (lane == 3), -jnp.exp(alog), 0.0)
    rowi = lax.broadcasted_iota(jnp.int32, (C, 1), 0)
    cw = cw_ref[0]

    def chunk_inputs(i):
        r0 = pl.multiple_of(i * C, C)
        xc = x_ref[0, pl.ds(r0, C), 0:nqkv].astype(f32)
        rp = pl.multiple_of(jnp.maximum(r0 - halo, 0), halo)
        rn = pl.multiple_of(jnp.minimum(r0 + C, L - halo), halo)
        prow = x_ref[0, pl.ds(rp, halo), 0:nqkv][halo - 1:halo, :].astype(f32)
        nrow = x_ref[0, pl.ds(rn, halo), 0:nqkv][0:1, :].astype(f32)
        prow = jnp.where(i > 0, prow, 0.0)
        nrow = jnp.where(i < nchunk - 1, nrow, 0.0)
        prev = jnp.where(rowi == 0, prow, pltpu.roll(xc, 1, 0))
        nxt = jnp.where(rowi == C - 1, nrow, pltpu.roll(xc, C - 1, 0))
        y = prev * cw[0:1, :] + xc * cw[1:2, :] + nxt * cw[2:3, :]
        y = y * _sigmoid(y)
        q = y[:, 0:A_DK]
        k = y[:, A_DK:2 * A_DK]
        vc = y[:, 2 * A_DK:nqkv]
        qc = q * lax.rsqrt(jnp.sum(q * q, axis=-1, keepdims=True) + EPS) * (A_DK ** -0.5)
        kc = k * lax.rsqrt(jnp.sum(k * k, axis=-1, keepdims=True) + EPS)

        abc = ab_ref[0, pl.ds(r0, C), :]
        zg = abc + dtb
        g = neg_a * (jnp.maximum(zg, 0.0) + jnp.log(1.0 + jnp.exp(-jnp.abs(zg))))
        low16 = mk16_ref[MK_LOWER]
        g1 = g.astype(bf16)
        e1 = g - g1.astype(f32)
        g2 = e1.astype(bf16)
        g3 = (e1 - g2.astype(f32)).astype(bf16)
        pre = _dot(low16, g1) + _dot(low16, g2) + _dot(low16, g3)
        tot = pre[C - 1:C, :]
        suf = tot - pre + g
        beta_all = _sigmoid(abc)

        k16 = kc.astype(bf16)
        kk = _dot_nt(k16, k16)
        qk = _dot_nt(qc.astype(bf16), k16)

        chains = []
        for rev in (0, 1):
            beta = beta_all[:, rev:rev + 1]
            gcc = (suf if rev else pre)[:, 2 + rev:3 + rev]
            tot11 = tot[:, 2 + rev:3 + rev]
            keep = mk_ref[MK_UPPER if rev else MK_LOWER]
            bc = jnp.broadcast_to(gcc * LOG2E, (C, C))
            decay = jnp.exp2((bc - bc.T) * keep) * keep
            a = (beta * kk) * decay
            eg = jnp.exp(gcc)
            chains.append(dict(
                i=i, rev=rev, a=a, gl=jnp.exp(tot11), qd=qc * eg,
                rhs=jnp.concatenate([vc * beta, kc * (beta * eg)], axis=1).astype(bf16),
                kd16=(kc * jnp.exp(tot11 - gcc)).astype(bf16),
                qk16=(qk * decay).astype(bf16)))
        return chains

    def prep_group(first_chunk):
        chains = []
        for gi in range(group):
            chains += chunk_inputs(first_chunk + gi)
        t16s = _tri_inverse([c["a"] for c in chains], mk_ref, mk16_ref)
        uws = [_dot(t16, c["rhs"]).astype(bf16) for t16, c in zip(t16s, chains)]
        kts = [_dot_tn(c["kd16"], uw) for c, uw in zip(chains, uws)]
        prs = [_dot(c["qk16"], uw) for c, uw in zip(chains, uws)]
        for c, kt, pr in zip(chains, kts, prs):
            rev, i = c["rev"], c["i"]
            mp_s[rev, i, 0:A_DK, :] = (-kt[:, A_DV:]).astype(bf16)
            mp_s[rev, i, A_DK:A_DK + C, :] = (c["qd"] - pr[:, A_DV:]).astype(bf16)
            nr_s[rev, i, 0:A_DK, :] = kt[:, 0:A_DV]
            nr_s[rev, i, A_DK:A_DK + C, :] = pr[:, 0:A_DV]
            gl_s[rev, i] = jnp.broadcast_to(c["gl"], (SUBLANE, LANE))

    def prep(it, carry):
        for wv in range(waves):
            prep_group((it * waves + wv) * group)
        return carry

    waves = math.gcd(DELTA_WAVES, nchunk // group)
    lax.fori_loop(0, nchunk // (group * waves), prep, 0)

    def scan(i, carry):
        out = []
        for rev, S in enumerate(carry):
            j = (nchunk - 1 - i) if rev else i
            y = _dot(mp_s[rev, j], S.astype(bf16))
            nr = nr_s[rev, j]
            o_s[rev, pl.ds(pl.multiple_of(j * C, C), C), :] = y[A_DK:, :] + nr[A_DK:, :]
            out.append(S * gl_s[rev, j][0:1, :] + y[0:A_DK, :] + nr[0:A_DK, :])
        return tuple(out)

    S0 = jnp.zeros((A_DK, A_DV), f32)
    lax.fori_loop(0, nchunk, scan, (S0, S0))

    def finish(it, carry):
        for gi in range(group):
            r0 = pl.multiple_of((it * group + gi) * C, C)
            o = o_s[0, pl.ds(r0, C), :] + o_s[1, pl.ds(r0, C), :]
            zg = x_ref[0, pl.ds(r0, C), nqkv:nqkv + A_DV].astype(f32)
            o_ref[0, pl.ds(r0, C), :] = (_rms(o, on_ref[...]) * (zg * _sigmoid(zg))).astype(o_ref.dtype)
        return carry

    lax.fori_loop(0, nchunk // group, finish, 0)


def _deltanet(qkvz, abh, conv_w, gate_sc, onorm):
    B, L, _ = qkvz.shape
    C = DELTA_CHUNK
    nchunk = L // C
    hw = 4 * A_DK
    scratch = [pltpu.VMEM((2, nchunk, A_DK + C, A_DV), bf16),
               pltpu.VMEM((2, nchunk, A_DK + C, A_DV), f32),
               pltpu.VMEM((2, nchunk, SUBLANE, LANE), f32),
               pltpu.VMEM((2, L, A_DV), f32)]
    return pl.pallas_call(
        _deltanet_kernel,
        grid=(B, A_HEADS),
        in_specs=[pl.BlockSpec(memory_space=pltpu.SMEM),
                  pl.BlockSpec((1, L, hw), lambda b, h: (b, 0, h)),
                  pl.BlockSpec((1, L, LANE), lambda b, h: (b, 0, h)),
                  pl.BlockSpec((1, SUBLANE, 3 * A_DK), lambda b, h: (h, 0, 0)),
                  pl.BlockSpec((1, A_DV), lambda b, h: (0, 0)),
                  pl.BlockSpec((N_MASKS, C, C), lambda b, h: (0, 0, 0)),
                  pl.BlockSpec((N_MASKS, C, C), lambda b, h: (0, 0, 0))],
        out_specs=pl.BlockSpec((1, L, A_DV), lambda b, h: (b, 0, h)),
        out_shape=jax.ShapeDtypeStruct((B, L, A_WIDTH), bf16),
        scratch_shapes=scratch,
        compiler_params=_cparams("parallel", "parallel"),
        name="deltanet",
    )(gate_sc, qkvz, abh, conv_w, onorm.reshape(1, A_DV), *_delta_masks())


def _filter_spectrum_kernel(c_ref, s_ref, hs_ref, hd_ref, kr_ref, ki_ref):
    L = c_ref.shape[1]
    rows = kr_ref.shape[1]
    scale = 1.0 / (2 * L)

    def split(x):
        hi = x.astype(bf16)
        return hi, (x - hi.astype(f32)).astype(bf16)

    hs_hi, hs_lo = split(hs_ref[...])
    hd_hi, hd_lo = split(hd_ref[...])
    for k1 in range(kr_ref.shape[0]):
        c = c_ref[k1 * rows:(k1 + 1) * rows, :]
        s = s_ref[k1 * rows:(k1 + 1) * rows, :]
        kr_ref[k1] = (_dot(c, hs_hi) + _dot(c, hs_lo)) * scale
        ki_ref[k1] = (_dot(s, hd_hi) + _dot(s, hd_lo)) * (-scale)


def _filter_spectrum(hsum, hdiff):
    L, n = hsum.shape
    ct = HY_CT
    n_fft = 2 * L
    n2 = L // FFT_Z1
    d = jnp.arange(L, dtype=jnp.int32)[None, :]
    a1 = ((jnp.arange(FFT_K1, dtype=jnp.int32)[:, None] * d) % n_fft).astype(f32) * (2.0 * math.pi / n_fft)
    a2 = ((jnp.arange(n2, dtype=jnp.int32)[:, None] * d) % n2).astype(f32) * (2.0 * math.pi / n2)
    c1, s1 = jnp.cos(a1)[:, None, :], jnp.sin(a1)[:, None, :]
    c2, s2 = jnp.cos(a2)[None, :, :], jnp.sin(a2)[None, :, :]
    cmat = (c1 * c2 - s1 * s2).reshape(FFT_K1 * n2, L).astype(bf16)
    smat = (s1 * c2 + c1 * s2).reshape(FFT_K1 * n2, L).astype(bf16)
    once = pl.Buffered(1)
    return pl.pallas_call(
        _filter_spectrum_kernel,
        grid=(n // ct,),
        in_specs=[pl.BlockSpec((FFT_K1 * n2, L), lambda j: (0, 0), pipeline_mode=once),
                  pl.BlockSpec((FFT_K1 * n2, L), lambda j: (0, 0), pipeline_mode=once),
                  pl.BlockSpec((L, ct), lambda j: (0, j)),
                  pl.BlockSpec((L, ct), lambda j: (0, j))],
        out_specs=[pl.BlockSpec((FFT_K1, n2, ct), lambda j: (0, 0, j)),
                   pl.BlockSpec((FFT_K1, n2, ct), lambda j: (0, 0, j))],
        out_shape=[jax.ShapeDtypeStruct((FFT_K1, n2, n), f32),
                   jax.ShapeDtypeStruct((FFT_K1, n2, n), f32)],
        compiler_params=_cparams("parallel"),
        name="filter_spectrum",
    )(cmat, smat, hsum, hdiff)


FFT_N1 = 16
FFT_K1 = FFT_N1 // 2 + 1
FFT_Z1 = FFT_N1 // 2
FFT_ROWS = 32


def _lincomb(terms):
    groups = {}
    for c, a in terms:
        if abs(c) > 1e-9:
            groups.setdefault(round(abs(c), 9), []).append((c > 0, a))
    total = None
    for mag, items in groups.items():
        pos = [a for p, a in items if p]
        neg = [a for p, a in items if not p]
        s = None
        for a in pos:
            s = a if s is None else s + a
        for a in neg:
            s = -a if s is None else s - a
        if mag != 1.0:
            s = s * mag
        total = s if total is None else total + s
    return total


def _hyena_fft_kernel(m2f_ref, m2i_ref, twr_ref, twi_ref, x1_ref, x2_ref, v_ref, w1_ref, w2_ref, wv_ref,
                      kr0_ref, ki0_ref, kr1_ref, ki1_ref, db_ref, o_ref, z_s, g_s, d_s, y_s, b_s):
    L, ct = z_s.shape
    n2 = L // FFT_Z1
    R = min(HY_ROWS, L)
    nt = L // R
    r1 = min(FFT_ROWS, n2)
    halo = SUBLANE_BF16
    rowi = lax.broadcasted_iota(jnp.int32, (R, 1), 0)
    ang = 2.0 * math.pi / FFT_N1
    c16 = [[math.cos(ang * a * b) for b in range(FFT_K1)] for a in range(FFT_Z1)]
    s16 = [[math.sin(ang * a * b) for b in range(FFT_K1)] for a in range(FFT_Z1)]

    def conv_into(dst, x_ref, w_ref):
        def body(t, carry):
            r0 = pl.multiple_of(t * R, R)
            x = x_ref[0, pl.ds(r0, R), :].astype(f32)
            rp = pl.multiple_of(jnp.maximum(r0 - halo, 0), halo)
            rn = pl.multiple_of(jnp.minimum(r0 + R, L - halo), halo)
            prow = jnp.where(t > 0, x_ref[0, pl.ds(rp, halo), :][halo - 1:halo, :].astype(f32), 0.0)
            nrow = jnp.where(t < nt - 1, x_ref[0, pl.ds(rn, halo), :][0:1, :].astype(f32), 0.0)
            prev = jnp.where(rowi == 0, prow, pltpu.roll(x, 1, 0))
            nxt = jnp.where(rowi == R - 1, nrow, pltpu.roll(x, R - 1, 0))
            w = w_ref[...]
            dst[pl.ds(r0, R), :] = prev * w[0:1, :] + x * w[1:2, :] + nxt * w[2:3, :]
            return carry
        lax.fori_loop(0, nt, body, 0)

    def cols(k1):
        return slice(k1 * ct, (k1 + 1) * ct)

    conv_into(z_s, v_ref, wv_ref)
    stages = ((x1_ref, w1_ref, kr0_ref, ki0_ref), (x2_ref, w2_ref, kr1_ref, ki1_ref))
    for o, (g_ref, gw_ref, kr_ref, ki_ref) in enumerate(stages):
        conv_into(g_s, g_ref, gw_ref)

        def forward1(rt, carry):
            r0 = pl.multiple_of(rt * r1, r1)
            zs = [z_s[pl.ds(a * n2 + r0, r1), :] for a in range(FFT_Z1)]
            for k1 in range(FFT_K1):
                ar = _lincomb([(c16[a][k1], zs[a]) for a in range(FFT_Z1)])
                ai = _lincomb([(-s16[a][k1], zs[a]) for a in range(FFT_Z1)])
                twr = twr_ref[pl.ds(r0, r1), cols(k1)]
                twi = twi_ref[pl.ds(r0, r1), cols(k1)]
                if ai is None:
                    pr, pi = ar * twr, -(ar * twi)
                else:
                    pr, pi = ar * twr + ai * twi, ai * twr - ar * twi
                d_s[pl.ds(r0, r1), cols(k1)] = pr.astype(bf16)
                d_s[pl.ds(n2 + r0, r1), cols(k1)] = pi.astype(bf16)
            return carry

        lax.fori_loop(0, n2 // r1, forward1, 0)

        for k1 in range(FFT_K1):
            xk = _dot(m2f_ref[...], d_s[:, cols(k1)])
            xr, xi = xk[0:n2, :], xk[n2:, :]
            kr, ki = kr_ref[k1], ki_ref[k1]
            y_s[0:n2, cols(k1)] = (xr * kr - xi * ki).astype(bf16)
            y_s[n2:2 * n2, cols(k1)] = (xr * ki + xi * kr).astype(bf16)

        for k1 in range(FFT_K1):
            bk = _dot(m2i_ref[...], y_s[:, cols(k1)])
            br, bi = bk[0:n2, :], bk[n2:, :]
            twr, twi = twr_ref[:, cols(k1)], twi_ref[:, cols(k1)]
            b_s[0:n2, cols(k1)] = br * twr - bi * twi
            b_s[n2:2 * n2, cols(k1)] = br * twi + bi * twr

        def inverse1(rt, carry):
            r0 = pl.multiple_of(rt * r1, r1)
            brs = [b_s[pl.ds(r0, r1), cols(j)] for j in range(FFT_K1)]
            bis = [b_s[pl.ds(n2 + r0, r1), cols(j)] for j in range(FFT_K1)]
            for a in range(FFT_Z1):
                terms = [(1.0, brs[0]), (-1.0 if a % 2 else 1.0, brs[FFT_K1 - 1])]
                for j in range(1, FFT_K1 - 1):
                    terms += [(2.0 * c16[a][j], brs[j]), (-2.0 * s16[a][j], bis[j])]
                rows = pl.ds(a * n2 + r0, r1)
                z = g_s[rows, :] * (_lincomb(terms) + z_s[rows, :] * db_ref[o:o + 1, :])
                if o + 1 < len(stages):
                    z_s[rows, :] = z
                else:
                    o_ref[0, rows, :] = z.astype(o_ref.dtype)
            return carry

        lax.fori_loop(0, n2 // r1, inverse1, 0)


def _hyena_tables(L):
    n_fft = 2 * L
    n2 = L // FFT_Z1
    ct = HY_CT
    idx = jnp.arange(n2, dtype=jnp.int32)
    a2 = ((idx[:, None] * idx[None, :]) % n2).astype(f32) * (2.0 * math.pi / n2)
    c2, s2 = jnp.cos(a2), jnp.sin(a2)
    m2f = jnp.block([[c2, s2], [-s2, c2]]).astype(bf16)
    m2i = jnp.block([[c2, -s2], [s2, c2]]).astype(bf16)
    at = (idx[:, None] * jnp.arange(FFT_K1, dtype=jnp.int32)[None, :]).astype(f32) * (2.0 * math.pi / n_fft)
    twr = jnp.broadcast_to(jnp.cos(at)[:, :, None], (n2, FFT_K1, ct)).reshape(n2, FFT_K1 * ct)
    twi = jnp.broadcast_to(jnp.sin(at)[:, :, None], (n2, FFT_K1, ct)).reshape(n2, FFT_K1 * ct)
    return m2f, m2i, twr, twi


def _hyena_fft(hx, conv_w, tables, kr, ki, dbias):
    B, L, _ = hx.shape
    ct = HY_CT
    nct = B_WIDTH // ct
    n2 = L // FFT_Z1
    wide = FFT_K1 * ct
    once = pl.Buffered(1)
    m2f, m2i, twr, twi = tables

    def xspec(part):
        return pl.BlockSpec((1, L, ct), lambda c, b: (b, 0, part * nct + c))

    def wspec(part):
        return pl.BlockSpec((SUBLANE, ct), lambda c, b: (0, part * nct + c))

    def kspec(order):
        return pl.BlockSpec((FFT_K1, n2, ct), lambda c, b: (0, 0, order * nct + c), pipeline_mode=once)

    def const(shape):
        return pl.BlockSpec(shape, lambda c, b: (0, 0), pipeline_mode=once)

    return pl.pallas_call(
        _hyena_fft_kernel,
        grid=(nct, B),
        in_specs=[const((2 * n2, 2 * n2)), const((2 * n2, 2 * n2)), const((n2, wide)), const((n2, wide)),
                  xspec(0), xspec(1), xspec(2), wspec(0), wspec(1), wspec(2),
                  kspec(0), kspec(0), kspec(1), kspec(1),
                  pl.BlockSpec((SUBLANE, ct), lambda c, b: (0, c))],
        out_specs=pl.BlockSpec((1, L, ct), lambda c, b: (b, 0, c)),
        out_shape=jax.ShapeDtypeStruct((B, L, B_WIDTH), bf16),
        scratch_shapes=[pltpu.VMEM((L, ct), f32), pltpu.VMEM((L, ct), f32),
                        pltpu.VMEM((2 * n2, wide), bf16), pltpu.VMEM((2 * n2, wide), bf16),
                        pltpu.VMEM((2 * n2, wide), f32)],
        compiler_params=_cparams("parallel", "parallel"),
        name="hyena_fft",
    )(m2f, m2i, twr, twi, hx, hx, hx, conv_w, conv_w, conv_w, kr, ki, kr, ki, dbias)


RT_E1, RT_E2, RT_W1, RT_W2, RT_R1, RT_R2 = 0, 1, 2, 3, 4, 5
ROUTE_TM = ROW_TM
PROJ_PARTS = 2
COMBINE_PARTS = 4
EXPERT_TM = 512
ROW_SLOTS = 3
SC_WINDOW = 128


def _pack_bf16_pairs(x):
    n = x.shape[1] // 2
    bits = lax.bitcast_convert_type(x.astype(bf16).astype(f32), jnp.uint32)
    return bits[:, :n] | (bits[:, n:] >> 16)


def _unpack_bf16_pairs(p):
    hi = lax.bitcast_convert_type(p & jnp.uint32(0xFFFF0000), f32)
    lo = lax.bitcast_convert_type(p << 16, f32)
    return hi, lo


ROUTER_ROWS = 40


def _route_topk(lt):
    row = lax.broadcasted_iota(jnp.int32, lt.shape, 0)
    neg = -jnp.inf
    big = ROUTER_LANES
    gl = jnp.where(row < N_GROUPS, lt, neg)
    gmax = jnp.max(gl, axis=0, keepdims=True)
    gsel = jnp.min(jnp.where(gl == gmax, row, big), axis=0, keepdims=True)
    psel = 1.0 / jnp.sum(jnp.exp(gl - gmax), axis=0, keepdims=True)
    first = N_GROUPS + gsel * EXPERTS_PER_GROUP
    el = jnp.where((row >= first) & (row < first + EXPERTS_PER_GROUP), lt, neg)
    t1 = jnp.max(el, axis=0, keepdims=True)
    i1 = jnp.min(jnp.where(el == t1, row, big), axis=0, keepdims=True)
    el2 = jnp.where(row == i1, neg, el)
    t2 = jnp.max(el2, axis=0, keepdims=True)
    i2 = jnp.min(jnp.where(el2 == t2, row, big), axis=0, keepdims=True)
    ex = jnp.exp(t2 - t1)
    w1 = psel / (1.0 + ex)
    w2 = ex * w1
    return i1 - N_GROUPS, i2 - N_GROUPS, w1, w2


def _proj_route_kernel(x_ref, *refs, n_in):
    a_refs = refs[0:n_in]
    w_refs = refs[n_in:2 * n_in]
    g_ref, wr_ref, up_ref, xo_ref, hpa_ref, hpb_ref, rt_ref, rtt_ref, cnt_ref = refs[2 * n_in:]
    tm = xo_ref.shape[0]
    rp = tm // PROJ_PARTS
    parts = [slice(p * rp, (p + 1) * rp) for p in range(PROJ_PARTS)]
    xs = []
    for r in parts:
        x = x_ref[r, :]
        for a_ref, w_ref in zip(a_refs, w_refs):
            x = x + _dot(a_ref[r, :], w_ref[...])
        xo_ref[r, :] = x
        xs.append(x)
    hs = [_rms(x, g_ref[...]) for x in xs]
    for r, h in zip(parts, hs):
        hp = _pack_bf16_pairs(h)
        quarter = hp.shape[1] // 2
        hpa_ref[r, :] = hp[:, :quarter]
        hpb_ref[r, :] = hp[:, quarter:]
    tops = []
    for h in hs:
        h_hi = h.astype(bf16)
        h_lo = (h - h_hi.astype(f32)).astype(bf16)
        both = _dot(h_hi, wr_ref[...])
        logits = both[:, :ROUTER_LANES] + both[:, ROUTER_LANES:] + _dot(h_lo, wr_ref[:, :ROUTER_LANES])
        tops.append(_route_topk(logits.T[0:ROUTER_ROWS, :]))
    row = lax.broadcasted_iota(jnp.int32, (ROUTER_LANES, rp), 0)
    onehots = [jnp.where((row == e1) | (row == e2), 1.0, 0.0) for e1, e2, _, _ in tops]
    onehot16 = jnp.concatenate(onehots, axis=1).astype(bf16)
    before = _dot(onehot16, up_ref[...])
    for p, (r, (e1, e2, w1, w2)) in enumerate(zip(parts, tops)):
        bp = before[:, r]
        r1 = jnp.sum(jnp.where(row == e1, bp, 0.0), axis=0, keepdims=True)
        r2 = jnp.sum(jnp.where(row == e2, bp, 0.0), axis=0, keepdims=True)
        rec = jnp.zeros((ROUTER_LANES, rp), f32)
        for ln, val in ((RT_E1, e1.astype(f32)), (RT_E2, e2.astype(f32)), (RT_W1, w1), (RT_W2, w2),
                        (RT_R1, r1), (RT_R2, r2)):
            rec = jnp.where(row == ln, val, rec)
        rtt_ref[:, r] = rec[0:rtt_ref.shape[0], :]
        rt_ref[r, :] = rec.T
    ones = jnp.ones((cnt_ref.shape[1], tm), bf16)
    cnt_ref[0] = _dot_nt(ones, onehot16)


def _proj_route(x, acts, ws, gain, w_router):
    T, D = x.shape
    tm = ROUTE_TM
    n_in = len(acts)
    up = jnp.asarray(np.triu(np.ones((tm, tm), np.float32), 1), dtype=bf16)
    in_specs = [pl.BlockSpec((tm, D), lambda i: (i, 0))]
    in_specs += [pl.BlockSpec((tm, a.shape[1]), lambda i: (i, 0)) for a in acts]
    in_specs += [pl.BlockSpec(w.shape, lambda i: (0, 0)) for w in ws]
    in_specs += [pl.BlockSpec((1, D), lambda i: (0, 0)),
                 pl.BlockSpec((D, 2 * ROUTER_LANES), lambda i: (0, 0)),
                 pl.BlockSpec((tm, tm), lambda i: (0, 0))]
    return pl.pallas_call(
        functools.partial(_proj_route_kernel, n_in=n_in),
        grid=(T // tm,),
        in_specs=in_specs,
        out_specs=[pl.BlockSpec((tm, D), lambda i: (i, 0)),
                   pl.BlockSpec((tm, D // 4), lambda i: (i, 0)),
                   pl.BlockSpec((tm, D // 4), lambda i: (i, 0)),
                   pl.BlockSpec((tm, ROUTER_LANES), lambda i: (i, 0)),
                   pl.BlockSpec((SUBLANE, tm), lambda i: (0, i)),
                   pl.BlockSpec((1, SUBLANE, ROUTER_LANES), lambda i: (i, 0, 0))],
        out_shape=[jax.ShapeDtypeStruct((T, D), f32),
                   jax.ShapeDtypeStruct((T, D // 4), jnp.uint32),
                   jax.ShapeDtypeStruct((T, D // 4), jnp.uint32),
                   jax.ShapeDtypeStruct((T, ROUTER_LANES), f32),
                   jax.ShapeDtypeStruct((SUBLANE, T), f32),
                   jax.ShapeDtypeStruct((T // tm, SUBLANE, ROUTER_LANES), f32)],
        compiler_params=_cparams("parallel"),
        name="proj_route",
    )(x, *acts, *ws, gain.reshape(1, D), w_router, up)


def _moe_rows(T):
    return 2 * T + N_EXPERTS * EXPERT_TM


def _moe_plan(route_t, counts):
    T = route_t.shape[1]
    tm = ROUTE_TM
    cnt = counts[:, 0, :N_EXPERTS].astype(jnp.int32)
    total = jnp.sum(cnt, axis=0)
    padded = ((total + EXPERT_TM - 1) // EXPERT_TM) * EXPERT_TM
    ends = jnp.cumsum(padded)
    tile_base = (ends - padded)[None, :] + jnp.cumsum(cnt, axis=0) - cnt
    base = jnp.broadcast_to(tile_base.T[:, :, None], (N_EXPERTS, T // tm, tm)).reshape(N_EXPERTS, T)
    ids = jnp.arange(N_EXPERTS, dtype=jnp.int32)[:, None]
    dest = []
    for e_row, r_row in ((RT_E1, RT_R1), (RT_E2, RT_R2)):
        e = route_t[e_row].astype(jnp.int32)
        dest.append(jnp.sum(jnp.where(ids == e[None, :], base, 0), axis=0) + route_t[r_row].astype(jnp.int32))
    n_tiles = _moe_rows(T) // EXPERT_TM
    n_used = ends[-1] // EXPERT_TM
    starts = jnp.arange(n_tiles, dtype=jnp.int32) * EXPERT_TM
    tile_expert = jnp.sum((starts[:, None] >= ends[None, :]).astype(jnp.int32), axis=1)
    last_expert = jnp.sum((jnp.maximum(n_used - 1, 0) * EXPERT_TM >= ends).astype(jnp.int32))
    tile_expert = jnp.minimum(tile_expert, last_expert)
    tiles = jnp.arange(n_tiles, dtype=jnp.int32)
    prev_expert = jnp.concatenate([jnp.full((1,), -1, jnp.int32), tile_expert[:-1]])
    seg_first = ((tile_expert != prev_expert) & (tiles < n_used)).astype(jnp.int32)
    seg_slot = (jnp.cumsum(seg_first) - 1) % 2
    seg_end = jnp.sum(jnp.where(ids == tile_expert[None, :], (ends // EXPERT_TM)[:, None], 0), axis=0)
    nxt = jnp.where(seg_end < n_used, tile_expert[jnp.minimum(seg_end, n_tiles - 1)], -1)
    plan = (tile_expert, seg_first, seg_slot.astype(jnp.int32), nxt.astype(jnp.int32),
            n_used.reshape(1).astype(jnp.int32))
    return dest, plan


def _sc_mesh():
    return plsc.VectorSubcoreMesh(core_axis_name="core", subcore_axis_name="subcore")


def _sc_scatter_pair(rows, d0, d1, n_out):
    T, W = rows.shape
    win = SC_WINDOW

    @pl.kernel(out_type=jax.ShapeDtypeStruct((n_out, W), rows.dtype), mesh=_sc_mesh(), name="moe_sc_scatter")
    def scatter(x_hbm, i0_hbm, i1_hbm, o_hbm):
        def body(x_vmem, i0_vmem, i1_vmem):
            pltpu.sync_copy(x_vmem, o_hbm.at[i0_vmem.at[0]])
            pltpu.sync_copy(x_vmem, o_hbm.at[i1_vmem.at[0]])

        pltpu.emit_pipeline(
            body, grid=(T // win,),
            in_specs=[pl.BlockSpec((win, W), index_map=lambda i: (i, 0)),
                      pl.BlockSpec((1, win), index_map=lambda i: (0, i)),
                      pl.BlockSpec((1, win), index_map=lambda i: (0, i))],
            out_specs=[],
            core_axis_name=("core", "subcore"), dimension_semantics=(pltpu.PARALLEL,),
        )(x_hbm, i0_hbm, i1_hbm)

    return scatter(rows, d0.reshape(1, T), d1.reshape(1, T))


def _sc_gather(table, idx):
    n = idx.shape[0]
    W = table.shape[1]
    win = SC_WINDOW

    @pl.kernel(out_type=jax.ShapeDtypeStruct((n, W), table.dtype), mesh=_sc_mesh(), name="moe_sc_gather")
    def gather(t_hbm, i_hbm, o_hbm):
        def body(i_vmem, o_vmem):
            pltpu.sync_copy(t_hbm.at[i_vmem.at[0]], o_vmem)

        pltpu.emit_pipeline(
            body, grid=(n // win,),
            in_specs=[pl.BlockSpec((1, win), index_map=lambda i: (0, i))],
            out_specs=[pl.BlockSpec((win, W), index_map=lambda i: (i, 0))],
            core_axis_name=("core", "subcore"), dimension_semantics=(pltpu.PARALLEL,),
        )(i_hbm, o_hbm)

    return gather(table, idx.reshape(1, n))


def _expert_kernel(te_ref, first_ref, slot_ref, nxt_ref, nu_ref, xa_hbm, xb_hbm, wg_hbm, wu_hbm, wd_hbm,
                   ya_ref, yb_ref, xa_s, xb_s, wg_s, wu_s, wd_s, wg16, wu16, wd16, xsem, sem):
    i = pl.program_id(0)
    n_used = nu_ref[0]
    active = i < n_used
    tm = xa_s.shape[1]
    ahead = ROW_SLOTS - 1

    def row_copies(tile):
        s = lax.rem(tile, ROW_SLOTS)
        rows = pl.ds(pl.multiple_of(tile * tm, tm), tm)
        return (pltpu.make_async_copy(xa_hbm.at[rows], xa_s.at[s], xsem.at[s, 0]),
                pltpu.make_async_copy(xb_hbm.at[rows], xb_s.at[s], xsem.at[s, 1]))

    def weight_copies(expert, s):
        return (pltpu.make_async_copy(wg_hbm.at[expert], wg_s.at[s], sem.at[s, 0]),
                pltpu.make_async_copy(wu_hbm.at[expert], wu_s.at[s], sem.at[s, 1]),
                pltpu.make_async_copy(wd_hbm.at[expert], wd_s.at[s], sem.at[s, 2]))

    @pl.when(i == 0)
    def _():
        for c in weight_copies(te_ref[0], slot_ref[0]):
            c.start()
        for t in range(ahead):
            @pl.when(t < n_used)
            def _():
                for c in row_copies(jnp.int32(t)):
                    c.start()

    @pl.when(i + ahead < n_used)
    def _():
        for c in row_copies(i + ahead):
            c.start()

    @pl.when(active & (first_ref[i] == 1))
    def _():
        s = slot_ref[i]
        for c in weight_copies(te_ref[i], s):
            c.wait()

        @pl.when(nxt_ref[i] >= 0)
        def _():
            for c in weight_copies(nxt_ref[i], 1 - s):
                c.start()

        wg16[...] = wg_s[s].astype(bf16)
        wu16[...] = wu_s[s].astype(bf16)
        wd16[...] = wd_s[s].astype(bf16)

    @pl.when(active)
    def _():
        for c in row_copies(i):
            c.wait()
        s = lax.rem(i, ROW_SLOTS)
        hi, lo = _unpack_bf16_pairs(jnp.concatenate([xa_s[s], xb_s[s]], axis=1))
        x = jnp.concatenate([hi, lo], axis=1).astype(bf16)
        gate = _dot(x, wg16[...])
        up = _dot(x, wu16[...])
        hg = (gate * _sigmoid(gate) * up).astype(bf16)
        y = _pack_bf16_pairs(_dot(hg, wd16[...]))
        quarter = y.shape[1] // 2
        ya_ref[...] = y[:, :quarter]
        yb_ref[...] = y[:, quarter:]

    @pl.when(jnp.logical_not(active))
    def _():
        ya_ref[...] = jnp.zeros(ya_ref.shape, ya_ref.dtype)
        yb_ref[...] = jnp.zeros(yb_ref.shape, yb_ref.dtype)


def _experts(plan, layer, xa, xb, wg, wu, wd):
    tile_expert, seg_first, seg_slot, nxt, n_used = plan
    off = layer * N_EXPERTS
    n_rows, quarter = xa.shape
    D = 4 * quarter
    tm = EXPERT_TM
    row_spec = pl.BlockSpec((tm, quarter), lambda i, *_: (i, 0))
    grid_spec = pltpu.PrefetchScalarGridSpec(
        num_scalar_prefetch=5,
        grid=(n_rows // tm,),
        in_specs=[pl.BlockSpec(memory_space=pl.ANY)] * 5,
        out_specs=[row_spec, row_spec],
        scratch_shapes=[pltpu.VMEM((ROW_SLOTS, tm, quarter), xa.dtype), pltpu.VMEM((ROW_SLOTS, tm, quarter), xb.dtype),
                        pltpu.VMEM((2, D, D_EXPERT), wg.dtype), pltpu.VMEM((2, D, D_EXPERT), wu.dtype),
                        pltpu.VMEM((2, D_EXPERT, D), wd.dtype),
                        pltpu.VMEM((D, D_EXPERT), bf16), pltpu.VMEM((D, D_EXPERT), bf16), pltpu.VMEM((D_EXPERT, D), bf16),
                        pltpu.SemaphoreType.DMA((ROW_SLOTS, 2)), pltpu.SemaphoreType.DMA((2, 3))])
    return pl.pallas_call(
        _expert_kernel,
        grid_spec=grid_spec,
        out_shape=[jax.ShapeDtypeStruct((n_rows, quarter), jnp.uint32)] * 2,
        compiler_params=_cparams("arbitrary"),
        name="moe_experts",
    )(tile_expert + off, seg_first, seg_slot, jnp.where(nxt >= 0, nxt + off, -1), n_used, xa, xb, wg, wu, wd)


def _combine_kernel(x_ref, rt_ref, g_ref, a1_ref, b1_ref, a2_ref, b2_ref, *rest, final):
    half = x_ref.shape[1] // 2
    rp = x_ref.shape[0] // COMBINE_PARTS
    for p in range(COMBINE_PARTS):
        r = slice(p * rp, (p + 1) * rp)
        h1, l1 = _unpack_bf16_pairs(jnp.concatenate([a1_ref[r, :], b1_ref[r, :]], axis=1))
        h2, l2 = _unpack_bf16_pairs(jnp.concatenate([a2_ref[r, :], b2_ref[r, :]], axis=1))
        w1 = rt_ref[r, RT_W1:RT_W1 + 1]
        w2 = rt_ref[r, RT_W2:RT_W2 + 1]
        out = jnp.concatenate([x_ref[r, 0:half] + w1 * h1 + w2 * h2, x_ref[r, half:] + w1 * l1 + w2 * l2], axis=1)
        if final:
            (o_ref,) = rest
            o_ref[r, :] = _rms(out, g_ref[...])
        else:
            w_ref, o_ref, p_ref = rest
            o_ref[r, :] = out
            p_ref[r, :] = _dot(_rms(out, g_ref[...]).astype(bf16), w_ref[...]).astype(p_ref.dtype)


def _combine(x, route, ga, gb, gain, w_next=None):
    T, D = x.shape
    tm = ROUTE_TM
    n = T // tm
    quarter = D // 4
    final = w_next is None
    slot1 = pl.BlockSpec((tm, quarter), lambda i: (i, 0))
    slot2 = pl.BlockSpec((tm, quarter), lambda i: (i + n, 0))
    row_spec = pl.BlockSpec((tm, D), lambda i: (i, 0))
    in_specs = [row_spec, pl.BlockSpec((tm, ROUTER_LANES), lambda i: (i, 0)), pl.BlockSpec((1, D), lambda i: (0, 0)),
                slot1, slot1, slot2, slot2]
    args = [x, route, gain.reshape(1, D), ga, gb, ga, gb]
    out_specs = [row_spec]
    out_shape = [jax.ShapeDtypeStruct((T, D), f32)]
    if not final:
        n_next = w_next.shape[1]
        in_specs.append(pl.BlockSpec((D, n_next), lambda i: (0, 0)))
        args.append(w_next)
        out_specs.append(pl.BlockSpec((tm, n_next), lambda i: (i, 0)))
        out_shape.append(jax.ShapeDtypeStruct((T, n_next), bf16))
    outs = pl.pallas_call(
        functools.partial(_combine_kernel, final=final),
        grid=(n,),
        in_specs=in_specs,
        out_specs=out_specs,
        out_shape=out_shape,
        compiler_params=_cparams("parallel"),
        name="moe_combine",
    )(*args)
    return outs[0] if final else outs


def _moe(layer, x, hpa, hpb, route, route_t, counts, wg, wu, wd, gain, w_next=None):
    (d0, d1), plan = _moe_plan(route_t, counts)
    n_rows = _moe_rows(x.shape[0])
    xa = _sc_scatter_pair(hpa, d0, d1, n_rows)
    xb = _sc_scatter_pair(hpb, d0, d1, n_rows)
    ya, yb = _experts(plan, layer, xa, xb, wg, wu, wd)
    dcat = jnp.concatenate([d0, d1])
    return _combine(x, route, _sc_gather(ya, dcat), _sc_gather(yb, dcat), gain, w_next)


SWA_LOCKSTEP = 2


def _swa_bias_tables():
    blk = C_BLOCK
    qi = np.arange(blk)[:, None]
    kj = np.arange(3 * blk)[None, :]
    dist = np.abs(blk + qi - kj)
    tabs = np.full((3, C_KV_HEADS, C_GROUP * blk, 3 * blk), -np.inf, np.float32)
    for case in range(3):
        ok = dist <= WINDOW
        if case == 0:
            ok = ok & (kj >= blk)
        if case == 2:
            ok = ok & (kj < 2 * blk)
        for hh in range(C_HEADS):
            slope = 2.0 ** (-8.0 * (hh + 1) / C_HEADS)
            t = np.where(ok, -slope * LOG2E * dist, -np.inf)
            kvh, g = divmod(hh, C_GROUP)
            tabs[case, kvh, g * blk:(g + 1) * blk] = t
    return jnp.asarray(tabs)


def _swa_kernel(sink_ref, q_ref, kp_ref, kc_ref, kn_ref, vp_ref, vc_ref, vn_ref, bias_ref, o_ref):
    blk = C_BLOCK
    hd = C_HEAD_DIM
    kcat = jnp.concatenate([kp_ref[0], kc_ref[0], kn_ref[0]], axis=0)
    vcat = jnp.concatenate([vp_ref[0], vc_ref[0], vn_ref[0]], axis=0)
    q = q_ref[0]
    rblk = lax.broadcasted_iota(jnp.int32, (C_GROUP * blk, 1), 0) // blk
    lane = lax.broadcasted_iota(jnp.int32, (3 * blk, LANE - hd), 1)
    ones_pad = jnp.where(lane == 0, 1.0, 0.0).astype(bf16)
    q4s, sinks = [], []
    for kvh in range(C_KV_HEADS):
        q4s.append(jnp.concatenate(
            [q[:, (kvh * C_GROUP + g) * hd:(kvh * C_GROUP + g + 1) * hd] for g in range(C_GROUP)], axis=0))
        sk = jnp.zeros((C_GROUP * blk, 1), f32)
        for g in range(C_GROUP):
            sk = jnp.where(rblk == g, sink_ref[kvh * C_GROUP + g] * LOG2E, sk)
        sinks.append(sk)
    outs = []
    for h0 in range(0, C_KV_HEADS, SWA_LOCKSTEP):
        hs = range(h0, h0 + SWA_LOCKSTEP)
        ss = [_dot_nt(q4s[h], kcat[:, h * hd:(h + 1) * hd]) + bias_ref[0, h] for h in hs]
        ms = [jnp.maximum(jnp.max(s, axis=-1, keepdims=True), sinks[h]) for s, h in zip(ss, hs)]
        ps = [jnp.exp2(s - m).astype(bf16) for s, m in zip(ss, ms)]
        pvs = [_dot(p, jnp.concatenate([vcat[:, h * hd:(h + 1) * hd], ones_pad], axis=1)) for p, h in zip(ps, hs)]
        for pv, m, h in zip(pvs, ms, hs):
            den = pv[:, hd:hd + 1] + jnp.exp2(sinks[h] - m)
            o4 = pv[:, 0:hd] / den
            outs += [o4[g * blk:(g + 1) * blk, :] for g in range(C_GROUP)]
    o_ref[0] = jnp.concatenate(outs, axis=1).astype(o_ref.dtype)


def _swa(qkv, sinks):
    B, L, _ = qkv.shape
    nb = L // C_BLOCK
    dq = C_HEADS * C_HEAD_DIM
    dkv = C_KV_HEADS * C_HEAD_DIM
    k_col = dq // dkv
    v_col = k_col + 1

    def kv_spec(col, shift):
        return pl.BlockSpec((1, C_BLOCK, dkv),
                            lambda n, b: (b, jnp.clip(n + shift, 0, nb - 1), col))

    def edge_case(n, b):
        return (jnp.where(n == 0, 0, jnp.where(n == nb - 1, 2, 1)), 0, 0, 0)

    return pl.pallas_call(
        _swa_kernel,
        grid=(nb, B),
        in_specs=[pl.BlockSpec(memory_space=pltpu.SMEM),
                  pl.BlockSpec((1, C_BLOCK, dq), lambda n, b: (b, n, 0)),
                  kv_spec(k_col, -1), kv_spec(k_col, 0), kv_spec(k_col, 1),
                  kv_spec(v_col, -1), kv_spec(v_col, 0), kv_spec(v_col, 1),
                  pl.BlockSpec((1, C_KV_HEADS, C_GROUP * C_BLOCK, 3 * C_BLOCK), edge_case)],
        out_specs=pl.BlockSpec((1, C_BLOCK, dq), lambda n, b: (b, n, 0)),
        out_shape=jax.ShapeDtypeStruct((B, L, dq), bf16),
        compiler_params=_cparams("parallel", "parallel"),
        name="swa",
    )(sinks, qkv, qkv, qkv, qkv, qkv, qkv, qkv, _swa_bias_tables())


def _hyena_filters(L, w1, b1, w2, b2, w3, b3, w4, freq):
    hp = lax.Precision.HIGHEST
    t = jnp.linspace(0.0, 1.0, L, dtype=f32)[:, None]
    wpos = (2.0 * math.pi / L) * jnp.arange(L, dtype=f32)[:, None]
    bands = jnp.linspace(1e-4, HY_BANDS - 1, HY_BANDS, dtype=f32)[None, :]
    feats = jnp.concatenate([t, jnp.cos(bands * wpos), -jnp.sin(bands * wpos)], axis=-1)
    h = jnp.sin(freq[0] * (jnp.dot(feats, w1, precision=hp) + b1))
    h = jnp.sin(freq[1] * (jnp.dot(h, w2, precision=hp) + b2))
    h = jnp.sin(freq[2] * (jnp.dot(h, w3, precision=hp) + b3))
    h = jnp.dot(h, w4, precision=hp).reshape(L, HY_ORDER, 2, B_WIDTH)
    deltas = jnp.abs(jnp.linspace(HY_MIN_DECAY, HY_MAX_DECAY, B_WIDTH, dtype=f32))
    window = jnp.exp(-t * deltas[None, :])
    return h * window[:, None, None, :]


def _pad_rows(w):
    return jnp.pad(w, ((0, SUBLANE - w.shape[0]), (0, 0)))


def _even_in_weights(w_in):
    hk = A_HEADS * A_DK
    q = w_in[:, 0:hk].reshape(D_MODEL, A_HEADS, A_DK)
    k = w_in[:, hk:2 * hk].reshape(D_MODEL, A_HEADS, A_DK)
    v = w_in[:, 2 * hk:A_QKV].reshape(D_MODEL, A_HEADS, A_DV)
    z = w_in[:, A_QKV:A_QKV + A_WIDTH].reshape(D_MODEL, A_HEADS, A_DV)
    qkvz = jnp.concatenate([q, k, v, z], axis=-1).reshape(D_MODEL, A_HEADS * 4 * A_DK)
    ab = w_in[:, A_QKV + A_WIDTH:A_QKV + A_WIDTH + 4 * A_HEADS].reshape(D_MODEL, 4, A_HEADS)
    ab = jnp.transpose(ab, (0, 2, 1))
    ab = jnp.pad(ab.reshape(D_MODEL, 4 * A_HEADS), ((0, 0), (0, LANE - 4 * A_HEADS)))
    hx = w_in[:, A_QKV + A_WIDTH + 4 * A_HEADS:]
    return jnp.concatenate([qkvz, ab, hx], axis=1).astype(bf16)


def _router_weights(w_group, w_expert):
    w = jnp.concatenate([w_group, w_expert], axis=1)
    w = jnp.pad(w, ((0, 0), (0, ROUTER_LANES - w.shape[1])))
    hi = w.astype(bf16)
    lo = (w - hi.astype(f32)).astype(bf16)
    return jnp.concatenate([hi, lo], axis=1)


def kernel(x_prompt, x_sample, norm_mix, norm_ffn, norm_final, ev_w_in, ev_conv_a, ev_alog_f, ev_alog_b, ev_dtb_f, ev_dtb_b, ev_onorm, ev_conv_b, hy_w1, hy_b1, hy_w2, hy_b2, hy_w3, hy_b3, hy_w4, hy_freq, hy_dbias, ev_w_out, od_w_qkv, od_sinks, od_w_out, moe_w_group, moe_w_expert, moe_w_gate, moe_w_up, moe_w_down):
    L = x_prompt.shape[1]
    n_layers = moe_w_gate.shape[0]
    wg = moe_w_gate.reshape(n_layers * N_EXPERTS, D_MODEL, D_EXPERT)
    wu = moe_w_up.reshape(n_layers * N_EXPERTS, D_MODEL, D_EXPERT)
    wd = moe_w_down.reshape(n_layers * N_EXPERTS, D_EXPERT, D_MODEL)

    w_in = _even_in_weights(ev_w_in[0])
    n_qkvz = A_HEADS * 4 * A_DK
    n_ab = A_HEADS * LANE
    n_hx = 3 * B_WIDTH
    hk = A_HEADS * A_DK
    ca = ev_conv_a[0]
    conv_a = jnp.concatenate([ca[0:hk].reshape(A_HEADS, A_DK, 3), ca[hk:2 * hk].reshape(A_HEADS, A_DK, 3),
                              ca[2 * hk:].reshape(A_HEADS, A_DV, 3)], axis=1)
    conv_a = jnp.pad(jnp.transpose(conv_a, (0, 2, 1)), ((0, 0), (0, SUBLANE - 3), (0, 0)))
    gate_sc = jnp.stack([ev_alog_f[0], ev_alog_b[0], ev_dtb_f[0], ev_dtb_b[0]], axis=1)
    filt = _hyena_filters(L, hy_w1[0], hy_b1[0], hy_w2[0], hy_b2[0], hy_w3[0], hy_b3[0], hy_w4[0], hy_freq[0])
    hsum = (filt[:, :, 0] + filt[:, :, 1]).reshape(L, HY_ORDER * B_WIDTH)
    hdiff = (filt[:, :, 0] - filt[:, :, 1]).reshape(L, HY_ORDER * B_WIDTH)
    kr, ki = _filter_spectrum(hsum, hdiff)
    fft_tables = _hyena_tables(L)
    conv_b = _pad_rows(ev_conv_b[0].T)
    dbias = _pad_rows(hy_dbias[0])
    w_out = ev_w_out[0].astype(bf16)
    routers = [_router_weights(moe_w_group[l], moe_w_expert[l]) for l in range(n_layers)]
    n_qkv = od_w_qkv.shape[-1]
    dq = C_HEADS * C_HEAD_DIM
    w_qkv = jnp.concatenate([od_w_qkv[0][:, :dq] * (LOG2E * C_HEAD_DIM ** -0.5), od_w_qkv[0][:, dq:]], axis=1)
    w_qkv = w_qkv.astype(bf16)
    w_att_out = od_w_out[0].astype(bf16)

    def trunk(xb):
        B = xb.shape[0]
        T = B * L
        x = xb.reshape(T, D_MODEL)
        qkvz, abh, hx = _even_in_proj(x, norm_mix[0], w_in)
        o_a = _deltanet(qkvz.reshape(B, L, n_qkvz), abh.reshape(B, L, n_ab), conv_a, gate_sc, ev_onorm[0])
        o_b = _hyena_fft(hx.reshape(B, L, n_hx), conv_b, fft_tables, kr, ki, dbias)
        x, hpa, hpb, route, route_t, counts = _proj_route(
            x, (o_a.reshape(T, A_WIDTH), o_b.reshape(T, B_WIDTH)), (w_out[:A_WIDTH], w_out[A_WIDTH:]),
            norm_ffn[0], routers[0])
        x, qkv = _moe(0, x, hpa, hpb, route, route_t, counts, wg, wu, wd, norm_mix[1], w_qkv)
        o_c = _swa(qkv.reshape(B, L, n_qkv), od_sinks[0])
        x, hpa, hpb, route, route_t, counts = _proj_route(
            x, (o_c.reshape(T, C_HEADS * C_HEAD_DIM),), (w_att_out,), norm_ffn[1], routers[1])
        y = _moe(1, x, hpa, hpb, route, route_t, counts, wg, wu, wd, norm_final)
        return y.reshape(B, L, D_MODEL)

    return (trunk(x_prompt), trunk(x_sample))
```

```python
import functools
import math

import jax
import jax.numpy as jnp
import numpy as np
from jax import lax
from jax.experimental import pallas as pl
from jax.experimental.pallas import tpu as pltpu
from jax.experimental.pallas import tpu_sc as plsc

f32 = jnp.float32
bf16 = jnp.bfloat16

EPS = 1e-6
D_MODEL = 1024

A_HEADS = 4
A_DK = 128
A_DV = 128
A_WIDTH = A_HEADS * A_DV
A_QKV = 2 * A_HEADS * A_DK + A_HEADS * A_DV
DELTA_CHUNK = 128
INV_BASE = 8
DELTA_GROUP = 8
DELTA_WAVES = 2

B_WIDTH = D_MODEL - A_WIDTH
HY_ORDER = 2
HY_EMB = 33
HY_BANDS = (HY_EMB - 1) // 2
HY_TARGET = 1e-2
HY_MIN_DECAY = math.log(HY_TARGET) / 1.5
HY_MAX_DECAY = math.log(HY_TARGET) / 0.3
HY_CT = 256
HY_ROWS = 256

C_HEADS = 16
C_KV_HEADS = 4
C_HEAD_DIM = 64
C_GROUP = C_HEADS // C_KV_HEADS
WINDOW = 128
C_BLOCK = 128

N_GROUPS = 4
EXPERTS_PER_GROUP = 8
N_EXPERTS = N_GROUPS * EXPERTS_PER_GROUP
D_EXPERT = 256
ROUTER_LANES = 128

LANE = 128
SUBLANE = 8
SUBLANE_BF16 = 16
ROW_TM = 512
VMEM_LIMIT = 56 * 1024 * 1024
LOG2E = 1.4426950408889634


def _cparams(*sem):
    return pltpu.CompilerParams(dimension_semantics=sem, vmem_limit_bytes=VMEM_LIMIT)


def _dot(a, b):
    return jnp.dot(a, b, preferred_element_type=f32)


def _dot_nt(a, b):
    return lax.dot_general(a, b, (((1,), (1,)), ((), ())), preferred_element_type=f32)


def _dot_tn(a, b):
    return lax.dot_general(a, b, (((0,), (0,)), ((), ())), preferred_element_type=f32)


def _sigmoid(x):
    return 1.0 / (1.0 + jnp.exp(-x))


def _rms(x, g):
    return x * lax.rsqrt(jnp.mean(x * x, axis=-1, keepdims=True) + EPS) * g


def _even_in_kernel(x_ref, g_ref, w_ref, qkvz_ref, ab_ref, hx_ref):
    h = _rms(x_ref[...], g_ref[...]).astype(bf16)
    n_qkvz = qkvz_ref.shape[1]
    qkvz_ref[...] = _dot(h, w_ref[:, 0:n_qkvz]).astype(qkvz_ref.dtype)
    ab = _dot(h, w_ref[:, n_qkvz:n_qkvz + LANE])
    for hd in range(A_HEADS):
        ab_ref[:, hd * LANE:(hd + 1) * LANE] = ab if hd == 0 else pltpu.roll(ab, LANE - 4 * hd, 1)
    hx_ref[...] = _dot(h, w_ref[:, n_qkvz + LANE:]).astype(hx_ref.dtype)


def _even_in_proj(x, gain, w):
    T, D = x.shape
    tm = ROW_TM
    n_qkvz = A_HEADS * 4 * A_DK
    n_ab = A_HEADS * LANE
    n_hx = 3 * B_WIDTH
    return pl.pallas_call(
        _even_in_kernel,
        grid=(T // tm,),
        in_specs=[pl.BlockSpec((tm, D), lambda i: (i, 0)),
                  pl.BlockSpec((1, D), lambda i: (0, 0)),
                  pl.BlockSpec(w.shape, lambda i: (0, 0))],
        out_specs=[pl.BlockSpec((tm, n_qkvz), lambda i: (i, 0)),
                   pl.BlockSpec((tm, n_ab), lambda i: (i, 0)),
                   pl.BlockSpec((tm, n_hx), lambda i: (i, 0))],
        out_shape=[jax.ShapeDtypeStruct((T, n_qkvz), bf16),
                   jax.ShapeDtypeStruct((T, n_ab), f32),
                   jax.ShapeDtypeStruct((T, n_hx), bf16)],
        compiler_params=_cparams("parallel"),
        name="even_in_proj",
    )(x, gain.reshape(1, D), w)


MK_LOWER, MK_UPPER, MK_BASE, MK_LEVEL0 = 0, 1, 2, 3
N_LEVELS = int(math.log2(DELTA_CHUNK // INV_BASE))
MK_EYE = MK_LEVEL0 + N_LEVELS
N_MASKS = MK_EYE + 1


def _delta_masks():
    c = np.arange(DELTA_CHUNK)[:, None]
    s = np.arange(DELTA_CHUNK)[None, :]
    m = [s <= c, s >= c, (c // INV_BASE == s // INV_BASE) & (c != s)]
    b = INV_BASE
    while b < DELTA_CHUNK:
        m.append((c // (2 * b) == s // (2 * b)) & (c // b != s // b))
        b *= 2
    m.append(c == s)
    m = np.stack(m).astype(np.float32)
    return jnp.asarray(m), jnp.asarray(m, dtype=bf16)


def _tri_inverse(mats, mk_ref, mk16_ref):
    ps = [a * mk_ref[MK_BASE] for a in mats]
    ts = [mk_ref[MK_EYE] - p for p in ps]
    n = 2
    while n < INV_BASE:
        p16s = [p.astype(bf16) for p in ps]
        ps = [_dot(p16, p16) for p16 in p16s]
        ts = [t + _dot(t.astype(bf16), p.astype(bf16)) for t, p in zip(ts, ps)]
        n *= 2
    a16s = [a.astype(bf16) for a in mats]
    t16s = [t.astype(bf16) for t in ts]
    for lvl in range(N_LEVELS):
        xs = [_dot(a16 * mk16_ref[MK_LEVEL0 + lvl], t16) for a16, t16 in zip(a16s, t16s)]
        t16s = [t16 - _dot(t16, x.astype(bf16)).astype(bf16) for t16, x in zip(t16s, xs)]
    return t16s


def _deltanet_kernel(sc_ref, x_ref, ab_ref, cw_ref, on_ref, mk_ref, mk16_ref, o_ref, mp_s, nr_s, gl_s, o_s):
    hd = pl.program_id(1)
    L = x_ref.shape[1]
    C = DELTA_CHUNK
    nchunk = L // C
    group = math.gcd(DELTA_GROUP, nchunk)
    nqkv = 3 * A_DK
    halo = SUBLANE_BF16

    lane = lax.broadcasted_iota(jnp.int32, (1, LANE), 1)
    alog = jnp.where(lane == 2, sc_ref[hd, 0], jnp.where(lane == 3, sc_ref[hd, 1], 0.0))
    dtb = jnp.where(lane == 2, sc_ref[hd, 2], jnp.where(lane == 3, sc_ref[hd, 3], 0.0))
    neg_a = jnp.where((lane == 2) | (lane == 3), -jnp.exp(alog), 0.0)
    rowi = lax.broadcasted_iota(jnp.int32, (C, 1), 0)
    cw = cw_ref[0]

    def chunk_inputs(i):
        r0 = pl.multiple_of(i * C, C)
        xc = x_ref[0, pl.ds(r0, C), 0:nqkv].astype(f32)
        rp = pl.multiple_of(jnp.maximum(r0 - halo, 0), halo)
        rn = pl.multiple_of(jnp.minimum(r0 + C, L - halo), halo)
        prow = x_ref[0, pl.ds(rp, halo), 0:nqkv][halo - 1:halo, :].astype(f32)
        nrow = x_ref[0, pl.ds(rn, halo), 0:nqkv][0:1, :].astype(f32)
        prow = jnp.where(i > 0, prow, 0.0)
        nrow = jnp.where(i < nchunk - 1, nrow, 0.0)
        prev = jnp.where(rowi == 0, prow, pltpu.roll(xc, 1, 0))
        nxt = jnp.where(rowi == C - 1, nrow, pltpu.roll(xc, C - 1, 0))
        y = prev * cw[0:1, :] + xc * cw[1:2, :] + nxt * cw[2:3, :]
        y = y * _sigmoid(y)
        q = y[:, 0:A_DK]
        k = y[:, A_DK:2 * A_DK]
        vc = y[:, 2 * A_DK:nqkv]
        qc = q * lax.rsqrt(jnp.sum(q * q, axis=-1, keepdims=True) + EPS) * (A_DK ** -0.5)
        kc = k * lax.rsqrt(jnp.sum(k * k, axis=-1, keepdims=True) + EPS)

        abc = ab_ref[0, pl.ds(r0, C), :]
        zg = abc + dtb
        g = neg_a * (jnp.maximum(zg, 0.0) + jnp.log(1.0 + jnp.exp(-jnp.abs(zg))))
        low16 = mk16_ref[MK_LOWER]
        g1 = g.astype(bf16)
        e1 = g - g1.astype(f32)
        g2 = e1.astype(bf16)
        g3 = (e1 - g2.astype(f32)).astype(bf16)
        pre = _dot(low16, g1) + _dot(low16, g2) + _dot(low16, g3)
        tot = pre[C - 1:C, :]
        suf = tot - pre + g
        beta_all = _sigmoid(abc)

        k16 = kc.astype(bf16)
        kk = _dot_nt(k16, k16)
        qk = _dot_nt(qc.astype(bf16), k16)

        chains = []
        for rev in (0, 1):
            beta = beta_all[:, rev:rev + 1]
            gcc = (suf if rev else pre)[:, 2 + rev:3 + rev]
            tot11 = tot[:, 2 + rev:3 + rev]
            keep = mk_ref[MK_UPPER if rev else MK_LOWER]
            bc = jnp.broadcast_to(gcc * LOG2E, (C, C))
            decay = jnp.exp2((bc - bc.T) * keep) * keep
            a = (beta * kk) * decay
            eg = jnp.exp(gcc)
            chains.append(dict(
                i=i, rev=rev, a=a, gl=jnp.exp(tot11), qd=qc * eg,
                rhs=jnp.concatenate([vc * beta, kc * (beta * eg)], axis=1).astype(bf16),
                kd16=(kc * jnp.exp(tot11 - gcc)).astype(bf16),
                qk16=(qk * decay).astype(bf16)))
        return chains

    def prep_group(first_chunk):
        chains = []
        for gi in range(group):
            chains += chunk_inputs(first_chunk + gi)
        t16s = _tri_inverse([c["a"] for c in chains], mk_ref, mk16_ref)
        uws = [_dot(t16, c["rhs"]).astype(bf16) for t16, c in zip(t16s, chains)]
        kts = [_dot_tn(c["kd16"], uw) for c, uw in zip(chains, uws)]
        prs = [_dot(c["qk16"], uw) for c, uw in zip(chains, uws)]
        for c, kt, pr in zip(chains, kts, prs):
            rev, i = c["rev"], c["i"]
            mp_s[rev, i, 0:A_DK, :] = (-kt[:, A_DV:]).astype(bf16)
            mp_s[rev, i, A_DK:A_DK + C, :] = (c["qd"] - pr[:, A_DV:]).astype(bf16)
            nr_s[rev, i, 0:A_DK, :] = kt[:, 0:A_DV]
            nr_s[rev, i, A_DK:A_DK + C, :] = pr[:, 0:A_DV]
            gl_s[rev, i] = jnp.broadcast_to(c["gl"], (SUBLANE, LANE))

    def prep(it, carry):
        for wv in range(waves):
            prep_group((it * waves + wv) * group)
        return carry

    waves = math.gcd(DELTA_WAVES, nchunk // group)
    lax.fori_loop(0, nchunk // (group * waves), prep, 0)

    def scan(i, carry):
        out = []
        for rev, S in enumerate(carry):
            j = (nchunk - 1 - i) if rev else i
            y = _dot(mp_s[rev, j], S.astype(bf16))
            nr = nr_s[rev, j]
            o_s[rev, pl.ds(pl.multiple_of(j * C, C), C), :] = y[A_DK:, :] + nr[A_DK:, :]
            out.append(S * gl_s[rev, j][0:1, :] + y[0:A_DK, :] + nr[0:A_DK, :])
        return tuple(out)

    S0 = jnp.zeros((A_DK, A_DV), f32)
    lax.fori_loop(0, nchunk, scan, (S0, S0))

    def finish(it, carry):
        for gi in range(group):
            r0 = pl.multiple_of((it * group + gi) * C, C)
            o = o_s[0, pl.ds(r0, C), :] + o_s[1, pl.ds(r0, C), :]
            zg = x_ref[0, pl.ds(r0, C), nqkv:nqkv + A_DV].astype(f32)
            o_ref[0, pl.ds(r0, C), :] = (_rms(o, on_ref[...]) * (zg * _sigmoid(zg))).astype(o_ref.dtype)
        return carry

    lax.fori_loop(0, nchunk // group, finish, 0)


def _deltanet(qkvz, abh, conv_w, gate_sc, onorm):
    B, L, _ = qkvz.shape
    C = DELTA_CHUNK
    nchunk = L // C
    hw = 4 * A_DK
    scratch = [pltpu.VMEM((2, nchunk, A_DK + C, A_DV), bf16),
               pltpu.VMEM((2, nchunk, A_DK + C, A_DV), f32),
               pltpu.VMEM((2, nchunk, SUBLANE, LANE), f32),
               pltpu.VMEM((2, L, A_DV), f32)]
    return pl.pallas_call(
        _deltanet_kernel,
        grid=(B, A_HEADS),
        in_specs=[pl.BlockSpec(memory_space=pltpu.SMEM),
                  pl.BlockSpec((1, L, hw), lambda b, h: (b, 0, h)),
                  pl.BlockSpec((1, L, LANE), lambda b, h: (b, 0, h)),
                  pl.BlockSpec((1, SUBLANE, 3 * A_DK), lambda b, h: (h, 0, 0)),
                  pl.BlockSpec((1, A_DV), lambda b, h: (0, 0)),
                  pl.BlockSpec((N_MASKS, C, C), lambda b, h: (0, 0, 0)),
                  pl.BlockSpec((N_MASKS, C, C), lambda b, h: (0, 0, 0))],
        out_specs=pl.BlockSpec((1, L, A_DV), lambda b, h: (b, 0, h)),
        out_shape=jax.ShapeDtypeStruct((B, L, A_WIDTH), bf16),
        scratch_shapes=scratch,
        compiler_params=_cparams("parallel", "parallel"),
        name="deltanet",
    )(gate_sc, qkvz, abh, conv_w, onorm.reshape(1, A_DV), *_delta_masks())


def _filter_spectrum_kernel(c_ref, s_ref, hs_ref, hd_ref, kr_ref, ki_ref):
    L = c_ref.shape[1]
    rows = kr_ref.shape[1]
    scale = 1.0 / (2 * L)

    def split(x):
        hi = x.astype(bf16)
        return hi, (x - hi.astype(f32)).astype(bf16)

    hs_hi, hs_lo = split(hs_ref[...])
    hd_hi, hd_lo = split(hd_ref[...])
    for k1 in range(kr_ref.shape[0]):
        c = c_ref[k1 * rows:(k1 + 1) * rows, :]
        s = s_ref[k1 * rows:(k1 + 1) * rows, :]
        kr_ref[k1] = (_dot(c, hs_hi) + _dot(c, hs_lo)) * scale
        ki_ref[k1] = (_dot(s, hd_hi) + _dot(s, hd_lo)) * (-scale)


def _filter_spectrum(hsum, hdiff):
    L, n = hsum.shape
    ct = HY_CT
    n_fft = 2 * L
    n2 = L // FFT_Z1
    d = jnp.arange(L, dtype=jnp.int32)[None, :]
    a1 = ((jnp.arange(FFT_K1, dtype=jnp.int32)[:, None] * d) % n_fft).astype(f32) * (2.0 * math.pi / n_fft)
    a2 = ((jnp.arange(n2, dtype=jnp.int32)[:, None] * d) % n2).astype(f32) * (2.0 * math.pi / n2)
    c1, s1 = jnp.cos(a1)[:, None, :], jnp.sin(a1)[:, None, :]
    c2, s2 = jnp.cos(a2)[None, :, :], jnp.sin(a2)[None, :, :]
    cmat = (c1 * c2 - s1 * s2).reshape(FFT_K1 * n2, L).astype(bf16)
    smat = (s1 * c2 + c1 * s2).reshape(FFT_K1 * n2, L).astype(bf16)
    once = pl.Buffered(1)
    return pl.pallas_call(
        _filter_spectrum_kernel,
        grid=(n // ct,),
        in_specs=[pl.BlockSpec((FFT_K1 * n2, L), lambda j: (0, 0), pipeline_mode=once),
                  pl.BlockSpec((FFT_K1 * n2, L), lambda j: (0, 0), pipeline_mode=once),
                  pl.BlockSpec((L, ct), lambda j: (0, j)),
                  pl.BlockSpec((L, ct), lambda j: (0, j))],
        out_specs=[pl.BlockSpec((FFT_K1, n2, ct), lambda j: (0, 0, j)),
                   pl.BlockSpec((FFT_K1, n2, ct), lambda j: (0, 0, j))],
        out_shape=[jax.ShapeDtypeStruct((FFT_K1, n2, n), f32),
                   jax.ShapeDtypeStruct((FFT_K1, n2, n), f32)],
        compiler_params=_cparams("parallel"),
        name="filter_spectrum",
    )(cmat, smat, hsum, hdiff)


FFT_N1 = 16
FFT_K1 = FFT_N1 // 2 + 1
FFT_Z1 = FFT_N1 // 2
FFT_ROWS = 32


def _lincomb(terms):
    groups = {}
    for c, a in terms:
        if abs(c) > 1e-9:
            groups.setdefault(round(abs(c), 9), []).append((c > 0, a))
    total = None
    for mag, items in groups.items():
        pos = [a for p, a in items if p]
        neg = [a for p, a in items if not p]
        s = None
        for a in pos:
            s = a if s is None else s + a
        for a in neg:
            s = -a if s is None else s - a
        if mag != 1.0:
            s = s * mag
        total = s if total is None else total + s
    return total


def _hyena_fft_kernel(m2f_ref, m2i_ref, twr_ref, twi_ref, x1_ref, x2_ref, v_ref, w1_ref, w2_ref, wv_ref,
                      kr0_ref, ki0_ref, kr1_ref, ki1_ref, db_ref, o_ref, z_s, g_s, d_s, y_s, b_s):
    L, ct = z_s.shape
    n2 = L // FFT_Z1
    R = min(HY_ROWS, L)
    nt = L // R
    r1 = min(FFT_ROWS, n2)
    halo = SUBLANE_BF16
    rowi = lax.broadcasted_iota(jnp.int32, (R, 1), 0)
    ang = 2.0 * math.pi / FFT_N1
    c16 = [[math.cos(ang * a * b) for b in range(FFT_K1)] for a in range(FFT_Z1)]
    s16 = [[math.sin(ang * a * b) for b in range(FFT_K1)] for a in range(FFT_Z1)]

    def conv_into(dst, x_ref, w_ref):
        def body(t, carry):
            r0 = pl.multiple_of(t * R, R)
            x = x_ref[0, pl.ds(r0, R), :].astype(f32)
            rp = pl.multiple_of(jnp.maximum(r0 - halo, 0), halo)
            rn = pl.multiple_of(jnp.minimum(r0 + R, L - halo), halo)
            prow = jnp.where(t > 0, x_ref[0, pl.ds(rp, halo), :][halo - 1:halo, :].astype(f32), 0.0)
            nrow = jnp.where(t < nt - 1, x_ref[0, pl.ds(rn, halo), :][0:1, :].astype(f32), 0.0)
            prev = jnp.where(rowi == 0, prow, pltpu.roll(x, 1, 0))
            nxt = jnp.where(rowi == R - 1, nrow, pltpu.roll(x, R - 1, 0))
            w = w_ref[...]
            dst[pl.ds(r0, R), :] = prev * w[0:1, :] + x * w[1:2, :] + nxt * w[2:3, :]
            return carry
        lax.fori_loop(0, nt, body, 0)

    def cols(k1):
        return slice(k1 * ct, (k1 + 1) * ct)

    conv_into(z_s, v_ref, wv_ref)
    stages = ((x1_ref, w1_ref, kr0_ref, ki0_ref), (x2_ref, w2_ref, kr1_ref, ki1_ref))
    for o, (g_ref, gw_ref, kr_ref, ki_ref) in enumerate(stages):
        conv_into(g_s, g_ref, gw_ref)

        def forward1(rt, carry):
            r0 = pl.multiple_of(rt * r1, r1)
            zs = [z_s[pl.ds(a * n2 + r0, r1), :] for a in range(FFT_Z1)]
            for k1 in range(FFT_K1):
                ar = _lincomb([(c16[a][k1], zs[a]) for a in range(FFT_Z1)])
                ai = _lincomb([(-s16[a][k1], zs[a]) for a in range(FFT_Z1)])
                twr = twr_ref[pl.ds(r0, r1), cols(k1)]
                twi = twi_ref[pl.ds(r0, r1), cols(k1)]
                if ai is None:
                    pr, pi = ar * twr, -(ar * twi)
                else:
                    pr, pi = ar * twr + ai * twi, ai * twr - ar * twi
                d_s[pl.ds(r0, r1), cols(k1)] = pr.astype(bf16)
                d_s[pl.ds(n2 + r0, r1), cols(k1)] = pi.astype(bf16)
            return carry

        lax.fori_loop(0, n2 // r1, forward1, 0)

        for k1 in range(FFT_K1):
            xk = _dot(m2f_ref[...], d_s[:, cols(k1)])
            xr, xi = xk[0:n2, :], xk[n2:, :]
            kr, ki = kr_ref[k1], ki_ref[k1]
            y_s[0:n2, cols(k1)] = (xr * kr - xi * ki).astype(bf16)
            y_s[n2:2 * n2, cols(k1)] = (xr * ki + xi * kr).astype(bf16)

        for k1 in range(FFT_K1):
            bk = _dot(m2i_ref[...], y_s[:, cols(k1)])
            br, bi = bk[0:n2, :], bk[n2:, :]
            twr, twi = twr_ref[:, cols(k1)], twi_ref[:, cols(k1)]
            b_s[0:n2, cols(k1)] = br * twr - bi * twi
            b_s[n2:2 * n2, cols(k1)] = br * twi + bi * twr

        def inverse1(rt, carry):
            r0 = pl.multiple_of(rt * r1, r1)
            brs = [b_s[pl.ds(r0, r1), cols(j)] for j in range(FFT_K1)]
            bis = [b_s[pl.ds(n2 + r0, r1), cols(j)] for j in range(FFT_K1)]
            for a in range(FFT_Z1):
                terms = [(1.0, brs[0]), (-1.0 if a % 2 else 1.0, brs[FFT_K1 - 1])]
                for j in range(1, FFT_K1 - 1):
                    terms += [(2.0 * c16[a][j], brs[j]), (-2.0 * s16[a][j], bis[j])]
                rows = pl.ds(a * n2 + r0, r1)
                z = g_s[rows, :] * (_lincomb(terms) + z_s[rows, :] * db_ref[o:o + 1, :])
                if o + 1 < len(stages):
                    z_s[rows, :] = z
                else:
                    o_ref[0, rows, :] = z.astype(o_ref.dtype)
            return carry

        lax.fori_loop(0, n2 // r1, inverse1, 0)


def _hyena_tables(L):
    n_fft = 2 * L
    n2 = L // FFT_Z1
    ct = HY_CT
    idx = jnp.arange(n2, dtype=jnp.int32)
    a2 = ((idx[:, None] * idx[None, :]) % n2).astype(f32) * (2.0 * math.pi / n2)
    c2, s2 = jnp.cos(a2), jnp.sin(a2)
    m2f = jnp.block([[c2, s2], [-s2, c2]]).astype(bf16)
    m2i = jnp.block([[c2, -s2], [s2, c2]]).astype(bf16)
    at = (idx[:, None] * jnp.arange(FFT_K1, dtype=jnp.int32)[None, :]).astype(f32) * (2.0 * math.pi / n_fft)
    twr = jnp.broadcast_to(jnp.cos(at)[:, :, None], (n2, FFT_K1, ct)).reshape(n2, FFT_K1 * ct)
    twi = jnp.broadcast_to(jnp.sin(at)[:, :, None], (n2, FFT_K1, ct)).reshape(n2, FFT_K1 * ct)
    return m2f, m2i, twr, twi


def _hyena_fft(hx, conv_w, tables, kr, ki, dbias):
    B, L, _ = hx.shape
    ct = HY_CT
    nct = B_WIDTH // ct
    n2 = L // FFT_Z1
    wide = FFT_K1 * ct
    once = pl.Buffered(1)
    m2f, m2i, twr, twi = tables

    def xspec(part):
        return pl.BlockSpec((1, L, ct), lambda c, b: (b, 0, part * nct + c))

    def wspec(part):
        return pl.BlockSpec((SUBLANE, ct), lambda c, b: (0, part * nct + c))

    def kspec(order):
        return pl.BlockSpec((FFT_K1, n2, ct), lambda c, b: (0, 0, order * nct + c), pipeline_mode=once)

    def const(shape):
        return pl.BlockSpec(shape, lambda c, b: (0, 0), pipeline_mode=once)

    return pl.pallas_call(
        _hyena_fft_kernel,
        grid=(nct, B),
        in_specs=[const((2 * n2, 2 * n2)), const((2 * n2, 2 * n2)), const((n2, wide)), const((n2, wide)),
                  xspec(0), xspec(1), xspec(2), wspec(0), wspec(1), wspec(2),
                  kspec(0), kspec(0), kspec(1), kspec(1),
                  pl.BlockSpec((SUBLANE, ct), lambda c, b: (0, c))],
        out_specs=pl.BlockSpec((1, L, ct), lambda c, b: (b, 0, c)),
        out_shape=jax.ShapeDtypeStruct((B, L, B_WIDTH), bf16),
        scratch_shapes=[pltpu.VMEM((L, ct), f32), pltpu.VMEM((L, ct), f32),
                        pltpu.VMEM((2 * n2, wide), bf16), pltpu.VMEM((2 * n2, wide), bf16),
                        pltpu.VMEM((2 * n2, wide), f32)],
        compiler_params=_cparams("parallel", "parallel"),
        name="hyena_fft",
    )(m2f, m2i, twr, twi, hx, hx, hx, conv_w, conv_w, conv_w, kr, ki, kr, ki, dbias)


RT_E1, RT_E2, RT_W1, RT_W2, RT_R1, RT_R2 = 0, 1, 2, 3, 4, 5
ROUTE_TM = ROW_TM
PROJ_PARTS = 2
COMBINE_PARTS = 4
EXPERT_TM = 512
WEIGHT_DMA_PRIORITY = 1
ROW_SLOTS = 3
SC_WINDOW = 128


def _pack_bf16_pairs(x):
    n = x.shape[1] // 2
    bits = lax.bitcast_convert_type(x.astype(bf16).astype(f32), jnp.uint32)
    return bits[:, :n] | (bits[:, n:] >> 16)


def _unpack_bf16_pairs(p):
    hi = lax.bitcast_convert_type(p & jnp.uint32(0xFFFF0000), f32)
    lo = lax.bitcast_convert_type(p << 16, f32)
    return hi, lo


ROUTER_ROWS = 40


def _route_topk(lt):
    row = lax.broadcasted_iota(jnp.int32, lt.shape, 0)
    neg = -jnp.inf
    big = ROUTER_LANES
    gl = jnp.where(row < N_GROUPS, lt, neg)
    gmax = jnp.max(gl, axis=0, keepdims=True)
    gsel = jnp.min(jnp.where(gl == gmax, row, big), axis=0, keepdims=True)
    psel = 1.0 / jnp.sum(jnp.exp(gl - gmax), axis=0, keepdims=True)
    first = N_GROUPS + gsel * EXPERTS_PER_GROUP
    el = jnp.where((row >= first) & (row < first + EXPERTS_PER_GROUP), lt, neg)
    t1 = jnp.max(el, axis=0, keepdims=True)
    i1 = jnp.min(jnp.where(el == t1, row, big), axis=0, keepdims=True)
    el2 = jnp.where(row == i1, neg, el)
    t2 = jnp.max(el2, axis=0, keepdims=True)
    i2 = jnp.min(jnp.where(el2 == t2, row, big), axis=0, keepdims=True)
    ex = jnp.exp(t2 - t1)
    w1 = psel / (1.0 + ex)
    w2 = ex * w1
    return i1 - N_GROUPS, i2 - N_GROUPS, w1, w2


def _proj_route_kernel(x_ref, *refs, n_in):
    a_refs = refs[0:n_in]
    w_refs = refs[n_in:2 * n_in]
    g_ref, wr_ref, up_ref, xo_ref, hpa_ref, hpb_ref, rt_ref, rtt_ref, cnt_ref = refs[2 * n_in:]
    tm = xo_ref.shape[0]
    rp = tm // PROJ_PARTS
    parts = [slice(p * rp, (p + 1) * rp) for p in range(PROJ_PARTS)]
    xs = []
    for r in parts:
        x = x_ref[r, :]
        for a_ref, w_ref in zip(a_refs, w_refs):
            x = x + _dot(a_ref[r, :], w_ref[...])
        xo_ref[r, :] = x
        xs.append(x)
    hs = [_rms(x, g_ref[...]) for x in xs]
    for r, h in zip(parts, hs):
        hp = _pack_bf16_pairs(h)
        quarter = hp.shape[1] // 2
        hpa_ref[r, :] = hp[:, :quarter]
        hpb_ref[r, :] = hp[:, quarter:]
    tops = []
    for h in hs:
        h_hi = h.astype(bf16)
        h_lo = (h - h_hi.astype(f32)).astype(bf16)
        both = _dot(h_hi, wr_ref[...])
        logits = both[:, :ROUTER_LANES] + both[:, ROUTER_LANES:] + _dot(h_lo, wr_ref[:, :ROUTER_LANES])
        tops.append(_route_topk(logits.T[0:ROUTER_ROWS, :]))
    row = lax.broadcasted_iota(jnp.int32, (ROUTER_LANES, rp), 0)
    onehots = [jnp.where((row == e1) | (row == e2), 1.0, 0.0) for e1, e2, _, _ in tops]
    onehot16 = jnp.concatenate(onehots, axis=1).astype(bf16)
    before = _dot(onehot16, up_ref[...])
    for p, (r, (e1, e2, w1, w2)) in enumerate(zip(parts, tops)):
        bp = before[:, r]
        r1 = jnp.sum(jnp.where(row == e1, bp, 0.0), axis=0, keepdims=True)
        r2 = jnp.sum(jnp.where(row == e2, bp, 0.0), axis=0, keepdims=True)
        rec = jnp.zeros((ROUTER_LANES, rp), f32)
        for ln, val in ((RT_E1, e1.astype(f32)), (RT_E2, e2.astype(f32)), (RT_W1, w1), (RT_W2, w2),
                        (RT_R1, r1), (RT_R2, r2)):
            rec = jnp.where(row == ln, val, rec)
        rtt_ref[:, r] = rec[0:rtt_ref.shape[0], :]
        rt_ref[r, :] = rec.T
    ones = jnp.ones((cnt_ref.shape[1], tm), bf16)
    cnt_ref[0] = _dot_nt(ones, onehot16)


def _proj_route(x, acts, ws, gain, w_router):
    T, D = x.shape
    tm = ROUTE_TM
    n_in = len(acts)
    up = jnp.asarray(np.triu(np.ones((tm, tm), np.float32), 1), dtype=bf16)
    in_specs = [pl.BlockSpec((tm, D), lambda i: (i, 0))]
    in_specs += [pl.BlockSpec((tm, a.shape[1]), lambda i: (i, 0)) for a in acts]
    in_specs += [pl.BlockSpec(w.shape, lambda i: (0, 0)) for w in ws]
    in_specs += [pl.BlockSpec((1, D), lambda i: (0, 0)),
                 pl.BlockSpec((D, 2 * ROUTER_LANES), lambda i: (0, 0)),
                 pl.BlockSpec((tm, tm), lambda i: (0, 0))]
    return pl.pallas_call(
        functools.partial(_proj_route_kernel, n_in=n_in),
        grid=(T // tm,),
        in_specs=in_specs,
        out_specs=[pl.BlockSpec((tm, D), lambda i: (i, 0)),
                   pl.BlockSpec((tm, D // 4), lambda i: (i, 0)),
                   pl.BlockSpec((tm, D // 4), lambda i: (i, 0)),
                   pl.BlockSpec((tm, ROUTER_LANES), lambda i: (i, 0)),
                   pl.BlockSpec((SUBLANE, tm), lambda i: (0, i)),
                   pl.BlockSpec((1, SUBLANE, ROUTER_LANES), lambda i: (i, 0, 0))],
        out_shape=[jax.ShapeDtypeStruct((T, D), f32),
                   jax.ShapeDtypeStruct((T, D // 4), jnp.uint32),
                   jax.ShapeDtypeStruct((T, D // 4), jnp.uint32),
                   jax.ShapeDtypeStruct((T, ROUTER_LANES), f32),
                   jax.ShapeDtypeStruct((SUBLANE, T), f32),
                   jax.ShapeDtypeStruct((T // tm, SUBLANE, ROUTER_LANES), f32)],
        compiler_params=_cparams("parallel"),
        name="proj_route",
    )(x, *acts, *ws, gain.reshape(1, D), w_router, up)


def _moe_rows(T):
    return 2 * T + N_EXPERTS * EXPERT_TM


def _moe_plan(route_t, counts):
    T = route_t.shape[1]
    tm = ROUTE_TM
    cnt = counts[:, 0, :N_EXPERTS].astype(jnp.int32)
    total = jnp.sum(cnt, axis=0)
    padded = ((total + EXPERT_TM - 1) // EXPERT_TM) * EXPERT_TM
    ends = jnp.cumsum(padded)
    tile_base = (ends - padded)[None, :] + jnp.cumsum(cnt, axis=0) - cnt
    base = jnp.broadcast_to(tile_base.T[:, :, None], (N_EXPERTS, T // tm, tm)).reshape(N_EXPERTS, T)
    ids = jnp.arange(N_EXPERTS, dtype=jnp.int32)[:, None]
    dest = []
    for e_row, r_row in ((RT_E1, RT_R1), (RT_E2, RT_R2)):
        e = route_t[e_row].astype(jnp.int32)
        dest.append(jnp.sum(jnp.where(ids == e[None, :], base, 0), axis=0) + route_t[r_row].astype(jnp.int32))
    n_tiles = _moe_rows(T) // EXPERT_TM
    n_used = ends[-1] // EXPERT_TM
    starts = jnp.arange(n_tiles, dtype=jnp.int32) * EXPERT_TM
    tile_expert = jnp.sum((starts[:, None] >= ends[None, :]).astype(jnp.int32), axis=1)
    last_expert = jnp.sum((jnp.maximum(n_used - 1, 0) * EXPERT_TM >= ends).astype(jnp.int32))
    tile_expert = jnp.minimum(tile_expert, last_expert)
    tiles = jnp.arange(n_tiles, dtype=jnp.int32)
    prev_expert = jnp.concatenate([jnp.full((1,), -1, jnp.int32), tile_expert[:-1]])
    seg_first = ((tile_expert != prev_expert) & (tiles < n_used)).astype(jnp.int32)
    seg_slot = (jnp.cumsum(seg_first) - 1) % 2
    seg_end = jnp.sum(jnp.where(ids == tile_expert[None, :], (ends // EXPERT_TM)[:, None], 0), axis=0)
    nxt = jnp.where(seg_end < n_used, tile_expert[jnp.minimum(seg_end, n_tiles - 1)], -1)
    plan = (tile_expert, seg_first, seg_slot.astype(jnp.int32), nxt.astype(jnp.int32),
            n_used.reshape(1).astype(jnp.int32))
    return dest, plan


def _sc_mesh():
    return plsc.VectorSubcoreMesh(core_axis_name="core", subcore_axis_name="subcore")


def _sc_scatter_pair(rows, d0, d1, n_out):
    T, W = rows.shape
    win = SC_WINDOW

    @pl.kernel(out_type=jax.ShapeDtypeStruct((n_out, W), rows.dtype), mesh=_sc_mesh(), name="moe_sc_scatter")
    def scatter(x_hbm, i0_hbm, i1_hbm, o_hbm):
        def body(x_vmem, i0_vmem, i1_vmem):
            pltpu.sync_copy(x_vmem, o_hbm.at[i0_vmem.at[0]])
            pltpu.sync_copy(x_vmem, o_hbm.at[i1_vmem.at[0]])

        pltpu.emit_pipeline(
            body, grid=(T // win,),
            in_specs=[pl.BlockSpec((win, W), index_map=lambda i: (i, 0)),
                      pl.BlockSpec((1, win), index_map=lambda i: (0, i)),
                      pl.BlockSpec((1, win), index_map=lambda i: (0, i))],
            out_specs=[],
            core_axis_name=("core", "subcore"), dimension_semantics=(pltpu.PARALLEL,),
        )(x_hbm, i0_hbm, i1_hbm)

    return scatter(rows, d0.reshape(1, T), d1.reshape(1, T))


def _sc_gather(table, idx):
    n = idx.shape[0]
    W = table.shape[1]
    win = SC_WINDOW

    @pl.kernel(out_type=jax.ShapeDtypeStruct((n, W), table.dtype), mesh=_sc_mesh(), name="moe_sc_gather")
    def gather(t_hbm, i_hbm, o_hbm):
        def body(i_vmem, o_vmem):
            pltpu.sync_copy(t_hbm.at[i_vmem.at[0]], o_vmem)

        pltpu.emit_pipeline(
            body, grid=(n // win,),
            in_specs=[pl.BlockSpec((1, win), index_map=lambda i: (0, i))],
            out_specs=[pl.BlockSpec((win, W), index_map=lambda i: (i, 0))],
            core_axis_name=("core", "subcore"), dimension_semantics=(pltpu.PARALLEL,),
        )(i_hbm, o_hbm)

    return gather(table, idx.reshape(1, n))


def _expert_kernel(te_ref, first_ref, slot_ref, nxt_ref, nu_ref, xa_hbm, xb_hbm, wg_hbm, wu_hbm, wd_hbm,
                   ya_ref, yb_ref, xa_s, xb_s, wg_s, wu_s, wd_s, wg16, wu16, wd16, xsem, sem):
    i = pl.program_id(0)
    n_used = nu_ref[0]
    active = i < n_used
    tm = xa_s.shape[1]
    ahead = ROW_SLOTS - 1

    def row_copies(tile):
        s = lax.rem(tile, ROW_SLOTS)
        rows = pl.ds(pl.multiple_of(tile * tm, tm), tm)
        return (pltpu.make_async_copy(xa_hbm.at[rows], xa_s.at[s], xsem.at[s, 0]),
                pltpu.make_async_copy(xb_hbm.at[rows], xb_s.at[s], xsem.at[s, 1]))

    def weight_copies(expert, s):
        return (pltpu.make_async_copy(wg_hbm.at[expert], wg_s.at[s], sem.at[s, 0]),
                pltpu.make_async_copy(wu_hbm.at[expert], wu_s.at[s], sem.at[s, 1]),
                pltpu.make_async_copy(wd_hbm.at[expert], wd_s.at[s], sem.at[s, 2]))

    @pl.when(i == 0)
    def _():
        for c in weight_copies(te_ref[0], slot_ref[0]):
            c.start(priority=WEIGHT_DMA_PRIORITY)
        for t in range(ahead):
            @pl.when(t < n_used)
            def _():
                for c in row_copies(jnp.int32(t)):
                    c.start()

    @pl.when(i + ahead < n_used)
    def _():
        for c in row_copies(i + ahead):
            c.start()

    @pl.when(active & (first_ref[i] == 1))
    def _():
        s = slot_ref[i]
        for c in weight_copies(te_ref[i], s):
            c.wait()

        @pl.when(nxt_ref[i] >= 0)
        def _():
            for c in weight_copies(nxt_ref[i], 1 - s):
                c.start(priority=WEIGHT_DMA_PRIORITY)

        wg16[...] = wg_s[s].astype(bf16)
        wu16[...] = wu_s[s].astype(bf16)
        wd16[...] = wd_s[s].astype(bf16)

    @pl.when(active)
    def _():
        for c in row_copies(i):
            c.wait()
        s = lax.rem(i, ROW_SLOTS)
        hi, lo = _unpack_bf16_pairs(jnp.concatenate([xa_s[s], xb_s[s]], axis=1))
        x = jnp.concatenate([hi, lo], axis=1).astype(bf16)
        gate = _dot(x, wg16[...])
        up = _dot(x, wu16[...])
        hg = (gate * _sigmoid(gate) * up).astype(bf16)
        y = _pack_bf16_pairs(_dot(hg, wd16[...]))
        quarter = y.shape[1] // 2
        ya_ref[...] = y[:, :quarter]
        yb_ref[...] = y[:, quarter:]

    @pl.when(jnp.logical_not(active))
    def _():
        ya_ref[...] = jnp.zeros(ya_ref.shape, ya_ref.dtype)
        yb_ref[...] = jnp.zeros(yb_ref.shape, yb_ref.dtype)


def _experts(plan, layer, xa, xb, wg, wu, wd):
    tile_expert, seg_first, seg_slot, nxt, n_used = plan
    off = layer * N_EXPERTS
    n_rows, quarter = xa.shape
    D = 4 * quarter
    tm = EXPERT_TM
    row_spec = pl.BlockSpec((tm, quarter), lambda i, *_: (i, 0))
    grid_spec = pltpu.PrefetchScalarGridSpec(
        num_scalar_prefetch=5,
        grid=(n_rows // tm,),
        in_specs=[pl.BlockSpec(memory_space=pl.ANY)] * 5,
        out_specs=[row_spec, row_spec],
        scratch_shapes=[pltpu.VMEM((ROW_SLOTS, tm, quarter), xa.dtype), pltpu.VMEM((ROW_SLOTS, tm, quarter), xb.dtype),
                        pltpu.VMEM((2, D, D_EXPERT), wg.dtype), pltpu.VMEM((2, D, D_EXPERT), wu.dtype),
                        pltpu.VMEM((2, D_EXPERT, D), wd.dtype),
                        pltpu.VMEM((D, D_EXPERT), bf16), pltpu.VMEM((D, D_EXPERT), bf16), pltpu.VMEM((D_EXPERT, D), bf16),
                        pltpu.SemaphoreType.DMA((ROW_SLOTS, 2)), pltpu.SemaphoreType.DMA((2, 3))])
    return pl.pallas_call(
        _expert_kernel,
        grid_spec=grid_spec,
        out_shape=[jax.ShapeDtypeStruct((n_rows, quarter), jnp.uint32)] * 2,
        compiler_params=_cparams("arbitrary"),
        name="moe_experts",
    )(tile_expert + off, seg_first, seg_slot, jnp.where(nxt >= 0, nxt + off, -1), n_used, xa, xb, wg, wu, wd)


def _combine_kernel(x_ref, rt_ref, g_ref, a1_ref, b1_ref, a2_ref, b2_ref, *rest, final):
    half = x_ref.shape[1] // 2
    rp = x_ref.shape[0] // COMBINE_PARTS
    for p in range(COMBINE_PARTS):
        r = slice(p * rp, (p + 1) * rp)
        h1, l1 = _unpack_bf16_pairs(jnp.concatenate([a1_ref[r, :], b1_ref[r, :]], axis=1))
        h2, l2 = _unpack_bf16_pairs(jnp.concatenate([a2_ref[r, :], b2_ref[r, :]], axis=1))
        w1 = rt_ref[r, RT_W1:RT_W1 + 1]
        w2 = rt_ref[r, RT_W2:RT_W2 + 1]
        out = jnp.concatenate([x_ref[r, 0:half] + w1 * h1 + w2 * h2, x_ref[r, half:] + w1 * l1 + w2 * l2], axis=1)
        if final:
            (o_ref,) = rest
            o_ref[r, :] = _rms(out, g_ref[...])
        else:
            w_ref, o_ref, p_ref = rest
            o_ref[r, :] = out
            p_ref[r, :] = _dot(_rms(out, g_ref[...]).astype(bf16), w_ref[...]).astype(p_ref.dtype)


def _combine(x, route, ga, gb, gain, w_next=None):
    T, D = x.shape
    tm = ROUTE_TM
    n = T // tm
    quarter = D // 4
    final = w_next is None
    slot1 = pl.BlockSpec((tm, quarter), lambda i: (i, 0))
    slot2 = pl.BlockSpec((tm, quarter), lambda i: (i + n, 0))
    row_spec = pl.BlockSpec((tm, D), lambda i: (i, 0))
    in_specs = [row_spec, pl.BlockSpec((tm, ROUTER_LANES), lambda i: (i, 0)), pl.BlockSpec((1, D), lambda i: (0, 0)),
                slot1, slot1, slot2, slot2]
    args = [x, route, gain.reshape(1, D), ga, gb, ga, gb]
    out_specs = [row_spec]
    out_shape = [jax.ShapeDtypeStruct((T, D), f32)]
    if not final:
        n_next = w_next.shape[1]
        in_specs.append(pl.BlockSpec((D, n_next), lambda i: (0, 0)))
        args.append(w_next)
        out_specs.append(pl.BlockSpec((tm, n_next), lambda i: (i, 0)))
        out_shape.append(jax.ShapeDtypeStruct((T, n_next), bf16))
    outs = pl.pallas_call(
        functools.partial(_combine_kernel, final=final),
        grid=(n,),
        in_specs=in_specs,
        out_specs=out_specs,
        out_shape=out_shape,
        compiler_params=_cparams("parallel"),
        name="moe_combine",
    )(*args)
    return outs[0] if final else outs


def _moe(layer, x, hpa, hpb, route, route_t, counts, wg, wu, wd, gain, w_next=None):
    (d0, d1), plan = _moe_plan(route_t, counts)
    n_rows = _moe_rows(x.shape[0])
    xa = _sc_scatter_pair(hpa, d0, d1, n_rows)
    xb = _sc_scatter_pair(hpb, d0, d1, n_rows)
    ya, yb = _experts(plan, layer, xa, xb, wg, wu, wd)
    dcat = jnp.concatenate([d0, d1])
    return _combine(x, route, _sc_gather(ya, dcat), _sc_gather(yb, dcat), gain, w_next)


SWA_LOCKSTEP = 2


def _swa_bias_tables():
    blk = C_BLOCK
    qi = np.arange(blk)[:, None]
    kj = np.arange(3 * blk)[None, :]
    dist = np.abs(blk + qi - kj)
    tabs = np.full((3, C_KV_HEADS, C_GROUP * blk, 3 * blk), -np.inf, np.float32)
    for case in range(3):
        ok = dist <= WINDOW
        if case == 0:
            ok = ok & (kj >= blk)
        if case == 2:
            ok = ok & (kj < 2 * blk)
        for hh in range(C_HEADS):
            slope = 2.0 ** (-8.0 * (hh + 1) / C_HEADS)
            t = np.where(ok, -slope * LOG2E * dist, -np.inf)
            kvh, g = divmod(hh, C_GROUP)
            tabs[case, kvh, g * blk:(g + 1) * blk] = t
    return jnp.asarray(tabs)


def _swa_kernel(sink_ref, q_ref, kp_ref, kc_ref, kn_ref, vp_ref, vc_ref, vn_ref, bias_ref, o_ref):
    blk = C_BLOCK
    hd = C_HEAD_DIM
    kcat = jnp.concatenate([kp_ref[0], kc_ref[0], kn_ref[0]], axis=0)
    vcat = jnp.concatenate([vp_ref[0], vc_ref[0], vn_ref[0]], axis=0)
    q = q_ref[0]
    rblk = lax.broadcasted_iota(jnp.int32, (C_GROUP * blk, 1), 0) // blk
    lane = lax.broadcasted_iota(jnp.int32, (3 * blk, LANE - hd), 1)
    ones_pad = jnp.where(lane == 0, 1.0, 0.0).astype(bf16)
    q4s, sinks = [], []
    for kvh in range(C_KV_HEADS):
        q4s.append(jnp.concatenate(
            [q[:, (kvh * C_GROUP + g) * hd:(kvh * C_GROUP + g + 1) * hd] for g in range(C_GROUP)], axis=0))
        sk = jnp.zeros((C_GROUP * blk, 1), f32)
        for g in range(C_GROUP):
            sk = jnp.where(rblk == g, sink_ref[kvh * C_GROUP + g] * LOG2E, sk)
        sinks.append(sk)
    outs = []
    for h0 in range(0, C_KV_HEADS, SWA_LOCKSTEP):
        hs = range(h0, h0 + SWA_LOCKSTEP)
        ss = [_dot_nt(q4s[h], kcat[:, h * hd:(h + 1) * hd]) + bias_ref[0, h] for h in hs]
        ms = [jnp.maximum(jnp.max(s, axis=-1, keepdims=True), sinks[h]) for s, h in zip(ss, hs)]
        ps = [jnp.exp2(s - m).astype(bf16) for s, m in zip(ss, ms)]
        pvs = [_dot(p, jnp.concatenate([vcat[:, h * hd:(h + 1) * hd], ones_pad], axis=1)) for p, h in zip(ps, hs)]
        for pv, m, h in zip(pvs, ms, hs):
            den = pv[:, hd:hd + 1] + jnp.exp2(sinks[h] - m)
            o4 = pv[:, 0:hd] / den
            outs += [o4[g * blk:(g + 1) * blk, :] for g in range(C_GROUP)]
    o_ref[0] = jnp.concatenate(outs, axis=1).astype(o_ref.dtype)


def _swa(qkv, sinks):
    B, L, _ = qkv.shape
    nb = L // C_BLOCK
    dq = C_HEADS * C_HEAD_DIM
    dkv = C_KV_HEADS * C_HEAD_DIM
    k_col = dq // dkv
    v_col = k_col + 1

    def kv_spec(col, shift):
        return pl.BlockSpec((1, C_BLOCK, dkv),
                            lambda n, b: (b, jnp.clip(n + shift, 0, nb - 1), col))

    def edge_case(n, b):
        return (jnp.where(n == 0, 0, jnp.where(n == nb - 1, 2, 1)), 0, 0, 0)

    return pl.pallas_call(
        _swa_kernel,
        grid=(nb, B),
        in_specs=[pl.BlockSpec(memory_space=pltpu.SMEM),
                  pl.BlockSpec((1, C_BLOCK, dq), lambda n, b: (b, n, 0)),
                  kv_spec(k_col, -1), kv_spec(k_col, 0), kv_spec(k_col, 1),
                  kv_spec(v_col, -1), kv_spec(v_col, 0), kv_spec(v_col, 1),
                  pl.BlockSpec((1, C_KV_HEADS, C_GROUP * C_BLOCK, 3 * C_BLOCK), edge_case)],
        out_specs=pl.BlockSpec((1, C_BLOCK, dq), lambda n, b: (b, n, 0)),
        out_shape=jax.ShapeDtypeStruct((B, L, dq), bf16),
        compiler_params=_cparams("parallel", "parallel"),
        name="swa",
    )(sinks, qkv, qkv, qkv, qkv, qkv, qkv, qkv, _swa_bias_tables())


def _hyena_filters(L, w1, b1, w2, b2, w3, b3, w4, freq):
    hp = lax.Precision.HIGHEST
    t = jnp.linspace(0.0, 1.0, L, dtype=f32)[:, None]
    wpos = (2.0 * math.pi / L) * jnp.arange(L, dtype=f32)[:, None]
    bands = jnp.linspace(1e-4, HY_BANDS - 1, HY_BANDS, dtype=f32)[None, :]
    feats = jnp.concatenate([t, jnp.cos(bands * wpos), -jnp.sin(bands * wpos)], axis=-1)
    h = jnp.sin(freq[0] * (jnp.dot(feats, w1, precision=hp) + b1))
    h = jnp.sin(freq[1] * (jnp.dot(h, w2, precision=hp) + b2))
    h = jnp.sin(freq[2] * (jnp.dot(h, w3, precision=hp) + b3))
    h = jnp.dot(h, w4, precision=hp).reshape(L, HY_ORDER, 2, B_WIDTH)
    deltas = jnp.abs(jnp.linspace(HY_MIN_DECAY, HY_MAX_DECAY, B_WIDTH, dtype=f32))
    window = jnp.exp(-t * deltas[None, :])
    return h * window[:, None, None, :]


def _pad_rows(w):
    return jnp.pad(w, ((0, SUBLANE - w.shape[0]), (0, 0)))


def _even_in_weights(w_in):
    hk = A_HEADS * A_DK
    q = w_in[:, 0:hk].reshape(D_MODEL, A_HEADS, A_DK)
    k = w_in[:, hk:2 * hk].reshape(D_MODEL, A_HEADS, A_DK)
    v = w_in[:, 2 * hk:A_QKV].reshape(D_MODEL, A_HEADS, A_DV)
    z = w_in[:, A_QKV:A_QKV + A_WIDTH].reshape(D_MODEL, A_HEADS, A_DV)
    qkvz = jnp.concatenate([q, k, v, z], axis=-1).reshape(D_MODEL, A_HEADS * 4 * A_DK)
    ab = w_in[:, A_QKV + A_WIDTH:A_QKV + A_WIDTH + 4 * A_HEADS].reshape(D_MODEL, 4, A_HEADS)
    ab = jnp.transpose(ab, (0, 2, 1))
    ab = jnp.pad(ab.reshape(D_MODEL, 4 * A_HEADS), ((0, 0), (0, LANE - 4 * A_HEADS)))
    hx = w_in[:, A_QKV + A_WIDTH + 4 * A_HEADS:]
    return jnp.concatenate([qkvz, ab, hx], axis=1).astype(bf16)


def _router_weights(w_group, w_expert):
    w = jnp.concatenate([w_group, w_expert], axis=1)
    w = jnp.pad(w, ((0, 0), (0, ROUTER_LANES - w.shape[1])))
    hi = w.astype(bf16)
    lo = (w - hi.astype(f32)).astype(bf16)
    return jnp.concatenate([hi, lo], axis=1)


def kernel(x_prompt, x_sample, norm_mix, norm_ffn, norm_final, ev_w_in, ev_conv_a, ev_alog_f, ev_alog_b, ev_dtb_f, ev_dtb_b, ev_onorm, ev_conv_b, hy_w1, hy_b1, hy_w2, hy_b2, hy_w3, hy_b3, hy_w4, hy_freq, hy_dbias, ev_w_out, od_w_qkv, od_sinks, od_w_out, moe_w_group, moe_w_expert, moe_w_gate, moe_w_up, moe_w_down):
    L = x_prompt.shape[1]
    n_layers = moe_w_gate.shape[0]
    wg = moe_w_gate.reshape(n_layers * N_EXPERTS, D_MODEL, D_EXPERT)
    wu = moe_w_up.reshape(n_layers * N_EXPERTS, D_MODEL, D_EXPERT)
    wd = moe_w_down.reshape(n_layers * N_EXPERTS, D_EXPERT, D_MODEL)

    w_in = _even_in_weights(ev_w_in[0])
    n_qkvz = A_HEADS * 4 * A_DK
    n_ab = A_HEADS * LANE
    n_hx = 3 * B_WIDTH
    hk = A_HEADS * A_DK
    ca = ev_conv_a[0]
    conv_a = jnp.concatenate([ca[0:hk].reshape(A_HEADS, A_DK, 3), ca[hk:2 * hk].reshape(A_HEADS, A_DK, 3),
                              ca[2 * hk:].reshape(A_HEADS, A_DV, 3)], axis=1)
    conv_a = jnp.pad(jnp.transpose(conv_a, (0, 2, 1)), ((0, 0), (0, SUBLANE - 3), (0, 0)))
    gate_sc = jnp.stack([ev_alog_f[0], ev_alog_b[0], ev_dtb_f[0], ev_dtb_b[0]], axis=1)
    filt = _hyena_filters(L, hy_w1[0], hy_b1[0], hy_w2[0], hy_b2[0], hy_w3[0], hy_b3[0], hy_w4[0], hy_freq[0])
    hsum = (filt[:, :, 0] + filt[:, :, 1]).reshape(L, HY_ORDER * B_WIDTH)
    hdiff = (filt[:, :, 0] - filt[:, :, 1]).reshape(L, HY_ORDER * B_WIDTH)
    kr, ki = _filter_spectrum(hsum, hdiff)
    fft_tables = _hyena_tables(L)
    conv_b = _pad_rows(ev_conv_b[0].T)
    dbias = _pad_rows(hy_dbias[0])
    w_out = ev_w_out[0].astype(bf16)
    routers = [_router_weights(moe_w_group[l], moe_w_expert[l]) for l in range(n_layers)]
    n_qkv = od_w_qkv.shape[-1]
    dq = C_HEADS * C_HEAD_DIM
    w_qkv = jnp.concatenate([od_w_qkv[0][:, :dq] * (LOG2E * C_HEAD_DIM ** -0.5), od_w_qkv[0][:, dq:]], axis=1)
    w_qkv = w_qkv.astype(bf16)
    w_att_out = od_w_out[0].astype(bf16)

    def trunk(xb):
        B = xb.shape[0]
        T = B * L
        x = xb.reshape(T, D_MODEL)
        qkvz, abh, hx = _even_in_proj(x, norm_mix[0], w_in)
        o_a = _deltanet(qkvz.reshape(B, L, n_qkvz), abh.reshape(B, L, n_ab), conv_a, gate_sc, ev_onorm[0])
        o_b = _hyena_fft(hx.reshape(B, L, n_hx), conv_b, fft_tables, kr, ki, dbias)
        x, hpa, hpb, route, route_t, counts = _proj_route(
            x, (o_a.reshape(T, A_WIDTH), o_b.reshape(T, B_WIDTH)), (w_out[:A_WIDTH], w_out[A_WIDTH:]),
            norm_ffn[0], routers[0])
        x, qkv = _moe(0, x, hpa, hpb, route, route_t, counts, wg, wu, wd, norm_mix[1], w_qkv)
        o_c = _swa(qkv.reshape(B, L, n_qkv), od_sinks[0])
        x, hpa, hpb, route, route_t, counts = _proj_route(
            x, (o_c.reshape(T, C_HEADS * C_HEAD_DIM),), (w_att_out,), norm_ffn[1], routers[1])
        y = _moe(1, x, hpa, hpb, route, route_t, counts, wg, wu, wd, norm_final)
        return y.reshape(B, L, D_MODEL)

    return (trunk(x_prompt), trunk(x_sample))
```

```python
import functools
import math

import jax
import jax.numpy as jnp
import numpy as np
from jax import lax
from jax.experimental import pallas as pl
from jax.experimental.pallas import tpu as pltpu
from jax.experimental.pallas import tpu_sc as plsc

f32 = jnp.float32
bf16 = jnp.bfloat16

EPS = 1e-6
D_MODEL = 1024

A_HEADS = 4
A_DK = 128
A_DV = 128
A_WIDTH = A_HEADS * A_DV
A_QKV = 2 * A_HEADS * A_DK + A_HEADS * A_DV
DELTA_CHUNK = 128
INV_BASE = 8
DELTA_GROUP = 8
DELTA_WAVES = 2

B_WIDTH = D_MODEL - A_WIDTH
HY_ORDER = 2
HY_EMB = 33
HY_BANDS = (HY_EMB - 1) // 2
HY_TARGET = 1e-2
HY_MIN_DECAY = math.log(HY_TARGET) / 1.5
HY_MAX_DECAY = math.log(HY_TARGET) / 0.3
HY_CT = 256
HY_ROWS = 256

C_HEADS = 16
C_KV_HEADS = 4
C_HEAD_DIM = 64
C_GROUP = C_HEADS // C_KV_HEADS
WINDOW = 128
C_BLOCK = 128

N_GROUPS = 4
EXPERTS_PER_GROUP = 8
N_EXPERTS = N_GROUPS * EXPERTS_PER_GROUP
D_EXPERT = 256
ROUTER_LANES = 128

LANE = 128
SUBLANE = 8
SUBLANE_BF16 = 16
ROW_TM = 512
VMEM_LIMIT = 56 * 1024 * 1024
LOG2E = 1.4426950408889634


def _cparams(*sem):
    return pltpu.CompilerParams(dimension_semantics=sem, vmem_limit_bytes=VMEM_LIMIT)


def _dot(a, b):
    return jnp.dot(a, b, preferred_element_type=f32)


def _dot_nt(a, b):
    return lax.dot_general(a, b, (((1,), (1,)), ((), ())), preferred_element_type=f32)


def _dot_tn(a, b):
    return lax.dot_general(a, b, (((0,), (0,)), ((), ())), preferred_element_type=f32)


def _sigmoid(x):
    return 1.0 / (1.0 + jnp.exp(-x))


def _rms(x, g):
    return x * lax.rsqrt(jnp.mean(x * x, axis=-1, keepdims=True) + EPS) * g


def _even_in_kernel(x_ref, g_ref, w_ref, qkvz_ref, ab_ref, hx_ref):
    h = _rms(x_ref[...], g_ref[...]).astype(bf16)
    n_qkvz = qkvz_ref.shape[1]
    qkvz_ref[...] = _dot(h, w_ref[:, 0:n_qkvz]).astype(qkvz_ref.dtype)
    ab = _dot(h, w_ref[:, n_qkvz:n_qkvz + LANE])
    for hd in range(A_HEADS):
        ab_ref[:, hd * LANE:(hd + 1) * LANE] = ab if hd == 0 else pltpu.roll(ab, LANE - 4 * hd, 1)
    hx_ref[...] = _dot(h, w_ref[:, n_qkvz + LANE:]).astype(hx_ref.dtype)


def _even_in_proj(x, gain, w):
    T, D = x.shape
    tm = ROW_TM
    n_qkvz = A_HEADS * 4 * A_DK
    n_ab = A_HEADS * LANE
    n_hx = 3 * B_WIDTH
    return pl.pallas_call(
        _even_in_kernel,
        grid=(T // tm,),
        in_specs=[pl.BlockSpec((tm, D), lambda i: (i, 0)),
                  pl.BlockSpec((1, D), lambda i: (0, 0)),
                  pl.BlockSpec(w.shape, lambda i: (0, 0))],
        out_specs=[pl.BlockSpec((tm, n_qkvz), lambda i: (i, 0)),
                   pl.BlockSpec((tm, n_ab), lambda i: (i, 0)),
                   pl.BlockSpec((tm, n_hx), lambda i: (i, 0))],
        out_shape=[jax.ShapeDtypeStruct((T, n_qkvz), bf16),
                   jax.ShapeDtypeStruct((T, n_ab), f32),
                   jax.ShapeDtypeStruct((T, n_hx), bf16)],
        compiler_params=_cparams("parallel"),
        name="even_in_proj",
    )(x, gain.reshape(1, D), w)


MK_LOWER, MK_UPPER, MK_BASE, MK_LEVEL0 = 0, 1, 2, 3
N_LEVELS = int(math.log2(DELTA_CHUNK // INV_BASE))
MK_EYE = MK_LEVEL0 + N_LEVELS
N_MASKS = MK_EYE + 1


def _delta_masks():
    c = np.arange(DELTA_CHUNK)[:, None]
    s = np.arange(DELTA_CHUNK)[None, :]
    m = [s <= c, s >= c, (c // INV_BASE == s // INV_BASE) & (c != s)]
    b = INV_BASE
    while b < DELTA_CHUNK:
        m.append((c // (2 * b) == s // (2 * b)) & (c // b != s // b))
        b *= 2
    m.append(c == s)
    m = np.stack(m).astype(np.float32)
    return jnp.asarray(m), jnp.asarray(m, dtype=bf16)


def _tri_inverse(mats, mk_ref, mk16_ref):
    ps = [a * mk_ref[MK_BASE] for a in mats]
    ts = [mk_ref[MK_EYE] - p for p in ps]
    n = 2
    while n < INV_BASE:
        p16s = [p.astype(bf16) for p in ps]
        ps = [_dot(p16, p16) for p16 in p16s]
        ts = [t + _dot(t.astype(bf16), p.astype(bf16)) for t, p in zip(ts, ps)]
        n *= 2
    a16s = [a.astype(bf16) for a in mats]
    t16s = [t.astype(bf16) for t in ts]
    for lvl in range(N_LEVELS):
        xs = [_dot(a16 * mk16_ref[MK_LEVEL0 + lvl], t16) for a16, t16 in zip(a16s, t16s)]
        t16s = [t16 - _dot(t16, x.astype(bf16)).astype(bf16) for t16, x in zip(t16s, xs)]
    return t16s


def _deltanet_kernel(sc_ref, x_ref, ab_ref, cw_ref, on_ref, mk_ref, mk16_ref, o_ref, mp_s, nr_s, gl_s, o_s):
    hd = pl.program_id(1)
    L = x_ref.shape[1]
    C = DELTA_CHUNK
    nchunk = L // C
    group = math.gcd(DELTA_GROUP, nchunk)
    nqkv = 3 * A_DK
    halo = SUBLANE_BF16

    lane = lax.broadcasted_iota(jnp.int32, (1, LANE), 1)
    alog = jnp.where(lane == 2, sc_ref[hd, 0], jnp.where(lane == 3, sc_ref[hd, 1], 0.0))
    dtb = jnp.where(lane == 2, sc_ref[hd, 2], jnp.where(lane == 3, sc_ref[hd, 3], 0.0))
    neg_a = jnp.where((lane == 2) | (lane == 3), -jnp.exp(alog), 0.0)
    rowi = lax.broadcasted_iota(jnp.int32, (C, 1), 0)
    cw = cw_ref[0]

    def chunk_inputs(i):
        r0 = pl.multiple_of(i * C, C)
        xc = x_ref[0, pl.ds(r0, C), 0:nqkv].astype(f32)
        rp = pl.multiple_of(jnp.maximum(r0 - halo, 0), halo)
        rn = pl.multiple_of(jnp.minimum(r0 + C, L - halo), halo)
        prow = x_ref[0, pl.ds(rp, halo), 0:nqkv][halo - 1:halo, :].astype(f32)
        nrow = x_ref[0, pl.ds(rn, halo), 0:nqkv][0:1, :].astype(f32)
        prow = jnp.where(i > 0, prow, 0.0)
        nrow = jnp.where(i < nchunk - 1, nrow, 0.0)
        prev = jnp.where(rowi == 0, prow, pltpu.roll(xc, 1, 0))
        nxt = jnp.where(rowi == C - 1, nrow, pltpu.roll(xc, C - 1, 0))
        y = prev * cw[0:1, :] + xc * cw[1:2, :] + nxt * cw[2:3, :]
        y = y * _sigmoid(y)
        q = y[:, 0:A_DK]
        k = y[:, A_DK:2 * A_DK]
        vc = y[:, 2 * A_DK:nqkv]
        qc = q * lax.rsqrt(jnp.sum(q * q, axis=-1, keepdims=True) + EPS) * (A_DK ** -0.5)
        kc = k * lax.rsqrt(jnp.sum(k * k, axis=-1, keepdims=True) + EPS)

        abc = ab_ref[0, pl.ds(r0, C), :]
        zg = abc + dtb
        g = neg_a * (jnp.maximum(zg, 0.0) + jnp.log(1.0 + jnp.exp(-jnp.abs(zg))))
        low16 = mk16_ref[MK_LOWER]
        g1 = g.astype(bf16)
        e1 = g - g1.astype(f32)
        g2 = e1.astype(bf16)
        g3 = (e1 - g2.astype(f32)).astype(bf16)
        pre = _dot(low16, g1) + _dot(low16, g2) + _dot(low16, g3)
        tot = pre[C - 1:C, :]
        suf = tot - pre + g
        beta_all = _sigmoid(abc)

        k16 = kc.astype(bf16)
        kk = _dot_nt(k16, k16)
        qk = _dot_nt(qc.astype(bf16), k16)

        chains = []
        for rev in (0, 1):
            beta = beta_all[:, rev:rev + 1]
            gcc = (suf if rev else pre)[:, 2 + rev:3 + rev]
            tot11 = tot[:, 2 + rev:3 + rev]
            keep = mk_ref[MK_UPPER if rev else MK_LOWER]
            bc = jnp.broadcast_to(gcc * LOG2E, (C, C))
            decay = jnp.exp2((bc - bc.T) * keep) * keep
            a = (beta * kk) * decay
            eg = jnp.exp(gcc)
            chains.append(dict(
                i=i, rev=rev, a=a, gl=jnp.exp(tot11), qd=qc * eg,
                rhs=jnp.concatenate([vc * beta, kc * (beta * eg)], axis=1).astype(bf16),
                kd16=(kc * jnp.exp(tot11 - gcc)).astype(bf16),
                qk16=(qk * decay).astype(bf16)))
        return chains

    def prep_group(first_chunk):
        chains = []
        for gi in range(group):
            chains += chunk_inputs(first_chunk + gi)
        t16s = _tri_inverse([c["a"] for c in chains], mk_ref, mk16_ref)
        uws = [_dot(t16, c["rhs"]).astype(bf16) for t16, c in zip(t16s, chains)]
        kts = [_dot_tn(c["kd16"], uw) for c, uw in zip(chains, uws)]
        prs = [_dot(c["qk16"], uw) for c, uw in zip(chains, uws)]
        for c, kt, pr in zip(chains, kts, prs):
            rev, i = c["rev"], c["i"]
            mp_s[rev, i, 0:A_DK, :] = (-kt[:, A_DV:]).astype(bf16)
            mp_s[rev, i, A_DK:A_DK + C, :] = (c["qd"] - pr[:, A_DV:]).astype(bf16)
            nr_s[rev, i, 0:A_DK, :] = kt[:, 0:A_DV]
            nr_s[rev, i, A_DK:A_DK + C, :] = pr[:, 0:A_DV]
            gl_s[rev, i] = jnp.broadcast_to(c["gl"], (SUBLANE, LANE))

    def prep(it, carry):
        for wv in range(waves):
            prep_group((it * waves + wv) * group)
        return carry

    waves = math.gcd(DELTA_WAVES, nchunk // group)
    lax.fori_loop(0, nchunk // (group * waves), prep, 0)

    def scan(i, carry):
        out = []
        for rev, S in enumerate(carry):
            j = (nchunk - 1 - i) if rev else i
            y = _dot(mp_s[rev, j], S.astype(bf16))
            nr = nr_s[rev, j]
            o_s[rev, pl.ds(pl.multiple_of(j * C, C), C), :] = y[A_DK:, :] + nr[A_DK:, :]
            out.append(S * gl_s[rev, j][0:1, :] + y[0:A_DK, :] + nr[0:A_DK, :])
        return tuple(out)

    S0 = jnp.zeros((A_DK, A_DV), f32)
    lax.fori_loop(0, nchunk, scan, (S0, S0))

    def finish(it, carry):
        for gi in range(group):
            r0 = pl.multiple_of((it * group + gi) * C, C)
            o = o_s[0, pl.ds(r0, C), :] + o_s[1, pl.ds(r0, C), :]
            zg = x_ref[0, pl.ds(r0, C), nqkv:nqkv + A_DV].astype(f32)
            o_ref[0, pl.ds(r0, C), :] = (_rms(o, on_ref[...]) * (zg * _sigmoid(zg))).astype(o_ref.dtype)
        return carry

    lax.fori_loop(0, nchunk // group, finish, 0)


def _deltanet(qkvz, abh, conv_w, gate_sc, onorm):
    B, L, _ = qkvz.shape
    C = DELTA_CHUNK
    nchunk = L // C
    hw = 4 * A_DK
    scratch = [pltpu.VMEM((2, nchunk, A_DK + C, A_DV), bf16),
               pltpu.VMEM((2, nchunk, A_DK + C, A_DV), f32),
               pltpu.VMEM((2, nchunk, SUBLANE, LANE), f32),
               pltpu.VMEM((2, L, A_DV), f32)]
    return pl.pallas_call(
        _deltanet_kernel,
        grid=(B, A_HEADS),
        in_specs=[pl.BlockSpec(memory_space=pltpu.SMEM),
                  pl.BlockSpec((1, L, hw), lambda b, h: (b, 0, h)),
                  pl.BlockSpec((1, L, LANE), lambda b, h: (b, 0, h)),
                  pl.BlockSpec((1, SUBLANE, 3 * A_DK), lambda b, h: (h, 0, 0)),
                  pl.BlockSpec((1, A_DV), lambda b, h: (0, 0)),
                  pl.BlockSpec((N_MASKS, C, C), lambda b, h: (0, 0, 0)),
                  pl.BlockSpec((N_MASKS, C, C), lambda b, h: (0, 0, 0))],
        out_specs=pl.BlockSpec((1, L, A_DV), lambda b, h: (b, 0, h)),
        out_shape=jax.ShapeDtypeStruct((B, L, A_WIDTH), bf16),
        scratch_shapes=scratch,
        compiler_params=_cparams("parallel", "parallel"),
        name="deltanet",
    )(gate_sc, qkvz, abh, conv_w, onorm.reshape(1, A_DV), *_delta_masks())


def _filter_spectrum_kernel(c_ref, s_ref, hs_ref, hd_ref, kr_ref, ki_ref):
    L = c_ref.shape[1]
    rows = kr_ref.shape[1]
    scale = 1.0 / (2 * L)

    def split(x):
        hi = x.astype(bf16)
        return hi, (x - hi.astype(f32)).astype(bf16)

    hs_hi, hs_lo = split(hs_ref[...])
    hd_hi, hd_lo = split(hd_ref[...])
    for k1 in range(kr_ref.shape[0]):
        c = c_ref[k1 * rows:(k1 + 1) * rows, :]
        s = s_ref[k1 * rows:(k1 + 1) * rows, :]
        kr_ref[k1] = (_dot(c, hs_hi) + _dot(c, hs_lo)) * scale
        ki_ref[k1] = (_dot(s, hd_hi) + _dot(s, hd_lo)) * (-scale)


def _filter_spectrum(hsum, hdiff):
    L, n = hsum.shape
    ct = HY_CT
    n_fft = 2 * L
    n2 = L // FFT_Z1
    d = jnp.arange(L, dtype=jnp.int32)[None, :]
    a1 = ((jnp.arange(FFT_K1, dtype=jnp.int32)[:, None] * d) % n_fft).astype(f32) * (2.0 * math.pi / n_fft)
    a2 = ((jnp.arange(n2, dtype=jnp.int32)[:, None] * d) % n2).astype(f32) * (2.0 * math.pi / n2)
    c1, s1 = jnp.cos(a1)[:, None, :], jnp.sin(a1)[:, None, :]
    c2, s2 = jnp.cos(a2)[None, :, :], jnp.sin(a2)[None, :, :]
    cmat = (c1 * c2 - s1 * s2).reshape(FFT_K1 * n2, L).astype(bf16)
    smat = (s1 * c2 + c1 * s2).reshape(FFT_K1 * n2, L).astype(bf16)
    once = pl.Buffered(1)
    return pl.pallas_call(
        _filter_spectrum_kernel,
        grid=(n // ct,),
        in_specs=[pl.BlockSpec((FFT_K1 * n2, L), lambda j: (0, 0), pipeline_mode=once),
                  pl.BlockSpec((FFT_K1 * n2, L), lambda j: (0, 0), pipeline_mode=once),
                  pl.BlockSpec((L, ct), lambda j: (0, j)),
                  pl.BlockSpec((L, ct), lambda j: (0, j))],
        out_specs=[pl.BlockSpec((FFT_K1, n2, ct), lambda j: (0, 0, j)),
                   pl.BlockSpec((FFT_K1, n2, ct), lambda j: (0, 0, j))],
        out_shape=[jax.ShapeDtypeStruct((FFT_K1, n2, n), f32),
                   jax.ShapeDtypeStruct((FFT_K1, n2, n), f32)],
        compiler_params=_cparams("parallel"),
        name="filter_spectrum",
    )(cmat, smat, hsum, hdiff)


FFT_N1 = 16
FFT_K1 = FFT_N1 // 2 + 1
FFT_Z1 = FFT_N1 // 2
FFT_ROWS = 32


def _lincomb(terms):
    groups = {}
    for c, a in terms:
        if abs(c) > 1e-9:
            groups.setdefault(round(abs(c), 9), []).append((c > 0, a))
    total = None
    for mag, items in groups.items():
        pos = [a for p, a in items if p]
        neg = [a for p, a in items if not p]
        s = None
        for a in pos:
            s = a if s is None else s + a
        for a in neg:
            s = -a if s is None else s - a
        if mag != 1.0:
            s = s * mag
        total = s if total is None else total + s
    return total


def _hyena_fft_kernel(m2f_ref, m2i_ref, twr_ref, twi_ref, x1_ref, x2_ref, v_ref, w1_ref, w2_ref, wv_ref,
                      kr0_ref, ki0_ref, kr1_ref, ki1_ref, db_ref, o_ref, z_s, g_s, d_s, y_s, b_s):
    L, ct = z_s.shape
    n2 = L // FFT_Z1
    R = min(HY_ROWS, L)
    nt = L // R
    r1 = min(FFT_ROWS, n2)
    halo = SUBLANE_BF16
    rowi = lax.broadcasted_iota(jnp.int32, (R, 1), 0)
    ang = 2.0 * math.pi / FFT_N1
    c16 = [[math.cos(ang * a * b) for b in range(FFT_K1)] for a in range(FFT_Z1)]
    s16 = [[math.sin(ang * a * b) for b in range(FFT_K1)] for a in range(FFT_Z1)]

    def conv_into(dst, x_ref, w_ref):
        def body(t, carry):
            r0 = pl.multiple_of(t * R, R)
            x = x_ref[0, pl.ds(r0, R), :].astype(f32)
            rp = pl.multiple_of(jnp.maximum(r0 - halo, 0), halo)
            rn = pl.multiple_of(jnp.minimum(r0 + R, L - halo), halo)
            prow = jnp.where(t > 0, x_ref[0, pl.ds(rp, halo), :][halo - 1:halo, :].astype(f32), 0.0)
            nrow = jnp.where(t < nt - 1, x_ref[0, pl.ds(rn, halo), :][0:1, :].astype(f32), 0.0)
            prev = jnp.where(rowi == 0, prow, pltpu.roll(x, 1, 0))
            nxt = jnp.where(rowi == R - 1, nrow, pltpu.roll(x, R - 1, 0))
            w = w_ref[...]
            dst[pl.ds(r0, R), :] = prev * w[0:1, :] + x * w[1:2, :] + nxt * w[2:3, :]
            return carry
        lax.fori_loop(0, nt, body, 0)

    def cols(k1):
        return slice(k1 * ct, (k1 + 1) * ct)

    conv_into(z_s, v_ref, wv_ref)
    stages = ((x1_ref, w1_ref, kr0_ref, ki0_ref), (x2_ref, w2_ref, kr1_ref, ki1_ref))
    for o, (g_ref, gw_ref, kr_ref, ki_ref) in enumerate(stages):
        conv_into(g_s, g_ref, gw_ref)

        def forward1(rt, carry):
            r0 = pl.multiple_of(rt * r1, r1)
            zs = [z_s[pl.ds(a * n2 + r0, r1), :] for a in range(FFT_Z1)]
            for k1 in range(FFT_K1):
                ar = _lincomb([(c16[a][k1], zs[a]) for a in range(FFT_Z1)])
                ai = _lincomb([(-s16[a][k1], zs[a]) for a in range(FFT_Z1)])
                twr = twr_ref[pl.ds(r0, r1), cols(k1)]
                twi = twi_ref[pl.ds(r0, r1), cols(k1)]
                if ai is None:
                    pr, pi = ar * twr, -(ar * twi)
                else:
                    pr, pi = ar * twr + ai * twi, ai * twr - ar * twi
                d_s[pl.ds(r0, r1), cols(k1)] = pr.astype(bf16)
                d_s[pl.ds(n2 + r0, r1), cols(k1)] = pi.astype(bf16)
            return carry

        lax.fori_loop(0, n2 // r1, forward1, 0)

        for k1 in range(FFT_K1):
            xk = _dot(m2f_ref[...], d_s[:, cols(k1)])
            xr, xi = xk[0:n2, :], xk[n2:, :]
            kr, ki = kr_ref[k1], ki_ref[k1]
            y_s[0:n2, cols(k1)] = (xr * kr - xi * ki).astype(bf16)
            y_s[n2:2 * n2, cols(k1)] = (xr * ki + xi * kr).astype(bf16)

        for k1 in range(FFT_K1):
            bk = _dot(m2i_ref[...], y_s[:, cols(k1)])
            br, bi = bk[0:n2, :], bk[n2:, :]
            twr, twi = twr_ref[:, cols(k1)], twi_ref[:, cols(k1)]
            b_s[0:n2, cols(k1)] = br * twr - bi * twi
            b_s[n2:2 * n2, cols(k1)] = br * twi + bi * twr

        def inverse1(rt, carry):
            r0 = pl.multiple_of(rt * r1, r1)
            brs = [b_s[pl.ds(r0, r1), cols(j)] for j in range(FFT_K1)]
            bis = [b_s[pl.ds(n2 + r0, r1), cols(j)] for j in range(FFT_K1)]
            for a in range(FFT_Z1):
                terms = [(1.0, brs[0]), (-1.0 if a % 2 else 1.0, brs[FFT_K1 - 1])]
                for j in range(1, FFT_K1 - 1):
                    terms += [(2.0 * c16[a][j], brs[j]), (-2.0 * s16[a][j], bis[j])]
                rows = pl.ds(a * n2 + r0, r1)
                z = g_s[rows, :] * (_lincomb(terms) + z_s[rows, :] * db_ref[o:o + 1, :])
                if o + 1 < len(stages):
                    z_s[rows, :] = z
                else:
                    o_ref[0, rows, :] = z.astype(o_ref.dtype)
            return carry

        lax.fori_loop(0, n2 // r1, inverse1, 0)


def _hyena_tables(L):
    n_fft = 2 * L
    n2 = L // FFT_Z1
    ct = HY_CT
    idx = jnp.arange(n2, dtype=jnp.int32)
    a2 = ((idx[:, None] * idx[None, :]) % n2).astype(f32) * (2.0 * math.pi / n2)
    c2, s2 = jnp.cos(a2), jnp.sin(a2)
    m2f = jnp.block([[c2, s2], [-s2, c2]]).astype(bf16)
    m2i = jnp.block([[c2, -s2], [s2, c2]]).astype(bf16)
    at = (idx[:, None] * jnp.arange(FFT_K1, dtype=jnp.int32)[None, :]).astype(f32) * (2.0 * math.pi / n_fft)
    twr = jnp.broadcast_to(jnp.cos(at)[:, :, None], (n2, FFT_K1, ct)).reshape(n2, FFT_K1 * ct)
    twi = jnp.broadcast_to(jnp.sin(at)[:, :, None], (n2, FFT_K1, ct)).reshape(n2, FFT_K1 * ct)
    return m2f, m2i, twr, twi


def _hyena_fft(hx, conv_w, tables, kr, ki, dbias):
    B, L, _ = hx.shape
    ct = HY_CT
    nct = B_WIDTH // ct
    n2 = L // FFT_Z1
    wide = FFT_K1 * ct
    once = pl.Buffered(1)
    m2f, m2i, twr, twi = tables

    def xspec(part):
        return pl.BlockSpec((1, L, ct), lambda c, b: (b, 0, part * nct + c))

    def wspec(part):
        return pl.BlockSpec((SUBLANE, ct), lambda c, b: (0, part * nct + c))

    def kspec(order):
        return pl.BlockSpec((FFT_K1, n2, ct), lambda c, b: (0, 0, order * nct + c), pipeline_mode=once)

    def const(shape):
        return pl.BlockSpec(shape, lambda c, b: (0, 0), pipeline_mode=once)

    return pl.pallas_call(
        _hyena_fft_kernel,
        grid=(nct, B),
        in_specs=[const((2 * n2, 2 * n2)), const((2 * n2, 2 * n2)), const((n2, wide)), const((n2, wide)),
                  xspec(0), xspec(1), xspec(2), wspec(0), wspec(1), wspec(2),
                  kspec(0), kspec(0), kspec(1), kspec(1),
                  pl.BlockSpec((SUBLANE, ct), lambda c, b: (0, c))],
        out_specs=pl.BlockSpec((1, L, ct), lambda c, b: (b, 0, c)),
        out_shape=jax.ShapeDtypeStruct((B, L, B_WIDTH), bf16),
        scratch_shapes=[pltpu.VMEM((L, ct), f32), pltpu.VMEM((L, ct), f32),
                        pltpu.VMEM((2 * n2, wide), bf16), pltpu.VMEM((2 * n2, wide), bf16),
                        pltpu.VMEM((2 * n2, wide), f32)],
        compiler_params=_cparams("parallel", "parallel"),
        name="hyena_fft",
    )(m2f, m2i, twr, twi, hx, hx, hx, conv_w, conv_w, conv_w, kr, ki, kr, ki, dbias)


RT_E1, RT_E2, RT_W1, RT_W2, RT_R1, RT_R2 = 0, 1, 2, 3, 4, 5
ROUTE_TM = ROW_TM
PROJ_PARTS = 2
COMBINE_PARTS = 4
EXPERT_TM = 512
ROW_SLOTS = 3
SC_WINDOW = 128


def _pack_bf16_pairs(x):
    n = x.shape[1] // 2
    bits = lax.bitcast_convert_type(x.astype(bf16).astype(f32), jnp.uint32)
    return bits[:, :n] | (bits[:, n:] >> 16)


def _unpack_bf16_pairs(p):
    hi = lax.bitcast_convert_type(p & jnp.uint32(0xFFFF0000), f32)
    lo = lax.bitcast_convert_type(p << 16, f32)
    return hi, lo


ROUTER_ROWS = 40


def _route_topk(lt):
    row = lax.broadcasted_iota(jnp.int32, lt.shape, 0)
    neg = -jnp.inf
    big = ROUTER_LANES
    gl = jnp.where(row < N_GROUPS, lt, neg)
    gmax = jnp.max(gl, axis=0, keepdims=True)
    gsel = jnp.min(jnp.where(gl == gmax, row, big), axis=0, keepdims=True)
    psel = 1.0 / jnp.sum(jnp.exp(gl - gmax), axis=0, keepdims=True)
    first = N_GROUPS + gsel * EXPERTS_PER_GROUP
    el = jnp.where((row >= first) & (row < first + EXPERTS_PER_GROUP), lt, neg)
    t1 = jnp.max(el, axis=0, keepdims=True)
    i1 = jnp.min(jnp.where(el == t1, row, big), axis=0, keepdims=True)
    el2 = jnp.where(row == i1, neg, el)
    t2 = jnp.max(el2, axis=0, keepdims=True)
    i2 = jnp.min(jnp.where(el2 == t2, row, big), axis=0, keepdims=True)
    ex = jnp.exp(t2 - t1)
    w1 = psel / (1.0 + ex)
    w2 = ex * w1
    return i1 - N_GROUPS, i2 - N_GROUPS, w1, w2


def _proj_route_kernel(x_ref, *refs, n_in):
    a_refs = refs[0:n_in]
    w_refs = refs[n_in:2 * n_in]
    g_ref, wr_ref, up_ref, xo_ref, hpa_ref, hpb_ref, rt_ref, rtt_ref, cnt_ref = refs[2 * n_in:]
    tm = xo_ref.shape[0]
    rp = tm // PROJ_PARTS
    parts = [slice(p * rp, (p + 1) * rp) for p in range(PROJ_PARTS)]
    xs = []
    for r in parts:
        x = x_ref[r, :]
        for a_ref, w_ref in zip(a_refs, w_refs):
            x = x + _dot(a_ref[r, :], w_ref[...])
        xo_ref[r, :] = x
        xs.append(x)
    hs = [_rms(x, g_ref[...]) for x in xs]
    for r, h in zip(parts, hs):
        hp = _pack_bf16_pairs(h)
        quarter = hp.shape[1] // 2
        hpa_ref[r, :] = hp[:, :quarter]
        hpb_ref[r, :] = hp[:, quarter:]
    tops = []
    for h in hs:
        h_hi = h.astype(bf16)
        h_lo = (h - h_hi.astype(f32)).astype(bf16)
        both = _dot(h_hi, wr_ref[...])
        logits = both[:, :ROUTER_LANES] + both[:, ROUTER_LANES:] + _dot(h_lo, wr_ref[:, :ROUTER_LANES])
        tops.append(_route_topk(logits.T[0:ROUTER_ROWS, :]))
    row = lax.broadcasted_iota(jnp.int32, (ROUTER_LANES, rp), 0)
    onehots = [jnp.where((row == e1) | (row == e2), 1.0, 0.0) for e1, e2, _, _ in tops]
    onehot16 = jnp.concatenate(onehots, axis=1).astype(bf16)
    before = _dot(onehot16, up_ref[...])
    for p, (r, (e1, e2, w1, w2)) in enumerate(zip(parts, tops)):
        bp = before[:, r]
        r1 = jnp.sum(jnp.where(row == e1, bp, 0.0), axis=0, keepdims=True)
        r2 = jnp.sum(jnp.where(row == e2, bp, 0.0), axis=0, keepdims=True)
        rec = jnp.zeros((ROUTER_LANES, rp), f32)
        for ln, val in ((RT_E1, e1.astype(f32)), (RT_E2, e2.astype(f32)), (RT_W1, w1), (RT_W2, w2),
                        (RT_R1, r1), (RT_R2, r2)):
            rec = jnp.where(row == ln, val, rec)
        rtt_ref[:, r] = rec[0:rtt_ref.shape[0], :]
        rt_ref[r, :] = rec.T
    ones = jnp.ones((cnt_ref.shape[1], tm), bf16)
    cnt_ref[0] = _dot_nt(ones, onehot16)


def _proj_route(x, acts, ws, gain, w_router):
    T, D = x.shape
    tm = ROUTE_TM
    n_in = len(acts)
    up = jnp.asarray(np.triu(np.ones((tm, tm), np.float32), 1), dtype=bf16)
    in_specs = [pl.BlockSpec((tm, D), lambda i: (i, 0))]
    in_specs += [pl.BlockSpec((tm, a.shape[1]), lambda i: (i, 0)) for a in acts]
    in_specs += [pl.BlockSpec(w.shape, lambda i: (0, 0)) for w in ws]
    in_specs += [pl.BlockSpec((1, D), lambda i: (0, 0)),
                 pl.BlockSpec((D, 2 * ROUTER_LANES), lambda i: (0, 0)),
                 pl.BlockSpec((tm, tm), lambda i: (0, 0))]
    return pl.pallas_call(
        functools.partial(_proj_route_kernel, n_in=n_in),
        grid=(T // tm,),
        in_specs=in_specs,
        out_specs=[pl.BlockSpec((tm, D), lambda i: (i, 0)),
                   pl.BlockSpec((tm, D // 4), lambda i: (i, 0)),
                   pl.BlockSpec((tm, D // 4), lambda i: (i, 0)),
                   pl.BlockSpec((tm, ROUTER_LANES), lambda i: (i, 0)),
                   pl.BlockSpec((SUBLANE, tm), lambda i: (0, i)),
                   pl.BlockSpec((1, SUBLANE, ROUTER_LANES), lambda i: (i, 0, 0))],
        out_shape=[jax.ShapeDtypeStruct((T, D), f32),
                   jax.ShapeDtypeStruct((T, D // 4), jnp.uint32),
                   jax.ShapeDtypeStruct((T, D // 4), jnp.uint32),
                   jax.ShapeDtypeStruct((T, ROUTER_LANES), f32),
                   jax.ShapeDtypeStruct((SUBLANE, T), f32),
                   jax.ShapeDtypeStruct((T // tm, SUBLANE, ROUTER_LANES), f32)],
        compiler_params=_cparams("parallel"),
        name="proj_route",
    )(x, *acts, *ws, gain.reshape(1, D), w_router, up)


def _moe_rows(T):
    return 2 * T + N_EXPERTS * EXPERT_TM


def _moe_plan(route_t, counts):
    T = route_t.shape[1]
    tm = ROUTE_TM
    cnt = counts[:, 0, :N_EXPERTS].astype(jnp.int32)
    total = jnp.sum(cnt, axis=0)
    padded = ((total + EXPERT_TM - 1) // EXPERT_TM) * EXPERT_TM
    ends = jnp.cumsum(padded)
    tile_base = (ends - padded)[None, :] + jnp.cumsum(cnt, axis=0) - cnt
    base = jnp.broadcast_to(tile_base.T[:, :, None], (N_EXPERTS, T // tm, tm)).reshape(N_EXPERTS, T)
    ids = jnp.arange(N_EXPERTS, dtype=jnp.int32)[:, None]
    dest = []
    for e_row, r_row in ((RT_E1, RT_R1), (RT_E2, RT_R2)):
        e = route_t[e_row].astype(jnp.int32)
        dest.append(jnp.sum(jnp.where(ids == e[None, :], base, 0), axis=0) + route_t[r_row].astype(jnp.int32))
    n_tiles = _moe_rows(T) // EXPERT_TM
    n_used = ends[-1] // EXPERT_TM
    starts = jnp.arange(n_tiles, dtype=jnp.int32) * EXPERT_TM
    tile_expert = jnp.sum((starts[:, None] >= ends[None, :]).astype(jnp.int32), axis=1)
    last_expert = jnp.sum((jnp.maximum(n_used - 1, 0) * EXPERT_TM >= ends).astype(jnp.int32))
    tile_expert = jnp.minimum(tile_expert, last_expert)
    tiles = jnp.arange(n_tiles, dtype=jnp.int32)
    prev_expert = jnp.concatenate([jnp.full((1,), -1, jnp.int32), tile_expert[:-1]])
    seg_first = ((tile_expert != prev_expert) & (tiles < n_used)).astype(jnp.int32)
    seg_slot = (jnp.cumsum(seg_first) - 1) % 2
    seg_end = jnp.sum(jnp.where(ids == tile_expert[None, :], (ends // EXPERT_TM)[:, None], 0), axis=0)
    nxt = jnp.where(seg_end < n_used, tile_expert[jnp.minimum(seg_end, n_tiles - 1)], -1)
    plan = (tile_expert, seg_first, seg_slot.astype(jnp.int32), nxt.astype(jnp.int32),
            n_used.reshape(1).astype(jnp.int32))
    return dest, plan


def _sc_mesh():
    return plsc.VectorSubcoreMesh(core_axis_name="core", subcore_axis_name="subcore")


def _sc_scatter_pair(rows, d0, d1, n_out):
    T, W = rows.shape
    win = SC_WINDOW

    @pl.kernel(out_type=jax.ShapeDtypeStruct((n_out, W), rows.dtype), mesh=_sc_mesh(), name="moe_sc_scatter")
    def scatter(x_hbm, i0_hbm, i1_hbm, o_hbm):
        def body(x_vmem, i0_vmem, i1_vmem):
            pltpu.sync_copy(x_vmem, o_hbm.at[i0_vmem.at[0]])
            pltpu.sync_copy(x_vmem, o_hbm.at[i1_vmem.at[0]])

        pltpu.emit_pipeline(
            body, grid=(T // win,),
            in_specs=[pl.BlockSpec((win, W), index_map=lambda i: (i, 0)),
                      pl.BlockSpec((1, win), index_map=lambda i: (0, i)),
                      pl.BlockSpec((1, win), index_map=lambda i: (0, i))],
            out_specs=[],
            core_axis_name=("core", "subcore"), dimension_semantics=(pltpu.PARALLEL,),
        )(x_hbm, i0_hbm, i1_hbm)

    return scatter(rows, d0.reshape(1, T), d1.reshape(1, T))


def _sc_gather(table, idx):
    n = idx.shape[0]
    W = table.shape[1]
    win = SC_WINDOW

    @pl.kernel(out_type=jax.ShapeDtypeStruct((n, W), table.dtype), mesh=_sc_mesh(), name="moe_sc_gather")
    def gather(t_hbm, i_hbm, o_hbm):
        def body(i_vmem, o_vmem):
            pltpu.sync_copy(t_hbm.at[i_vmem.at[0]], o_vmem)

        pltpu.emit_pipeline(
            body, grid=(n // win,),
            in_specs=[pl.BlockSpec((1, win), index_map=lambda i: (0, i))],
            out_specs=[pl.BlockSpec((win, W), index_map=lambda i: (i, 0))],
            core_axis_name=("core", "subcore"), dimension_semantics=(pltpu.PARALLEL,),
        )(i_hbm, o_hbm)

    return gather(table, idx.reshape(1, n))


def _expert_kernel(te_ref, first_ref, slot_ref, nxt_ref, nu_ref, xa_hbm, xb_hbm, wg_hbm, wu_hbm, wd_hbm,
                   ya_ref, yb_ref, xa_s, xb_s, wg_s, wu_s, wd_s, xsem, sem):
    i = pl.program_id(0)
    n_used = nu_ref[0]
    active = i < n_used
    tm = xa_s.shape[1]
    ahead = ROW_SLOTS - 1

    def row_copies(tile):
        s = lax.rem(tile, ROW_SLOTS)
        rows = pl.ds(pl.multiple_of(tile * tm, tm), tm)
        return (pltpu.make_async_copy(xa_hbm.at[rows], xa_s.at[s], xsem.at[s, 0]),
                pltpu.make_async_copy(xb_hbm.at[rows], xb_s.at[s], xsem.at[s, 1]))

    def weight_copies(expert, s):
        return (pltpu.make_async_copy(wg_hbm.at[expert], wg_s.at[s], sem.at[s, 0]),
                pltpu.make_async_copy(wu_hbm.at[expert], wu_s.at[s], sem.at[s, 1]),
                pltpu.make_async_copy(wd_hbm.at[expert], wd_s.at[s], sem.at[s, 2]))

    @pl.when(i == 0)
    def _():
        for c in weight_copies(te_ref[0], slot_ref[0]):
            c.start()
        for t in range(ahead):
            @pl.when(t < n_used)
            def _():
                for c in row_copies(jnp.int32(t)):
                    c.start()

    @pl.when(i + ahead < n_used)
    def _():
        for c in row_copies(i + ahead):
            c.start()

    @pl.when(active & (first_ref[i] == 1))
    def _():
        s = slot_ref[i]
        for c in weight_copies(te_ref[i], s):
            c.wait()

        @pl.when(nxt_ref[i] >= 0)
        def _():
            for c in weight_copies(nxt_ref[i], 1 - s):
                c.start()

    @pl.when(active)
    def _():
        for c in row_copies(i):
            c.wait()
        s = lax.rem(i, ROW_SLOTS)
        hi, lo = _unpack_bf16_pairs(jnp.concatenate([xa_s[s], xb_s[s]], axis=1))
        x = jnp.concatenate([hi, lo], axis=1).astype(bf16)
        w = slot_ref[i]
        gate = _dot(x, wg_s[w])
        up = _dot(x, wu_s[w])
        hg = (gate * _sigmoid(gate) * up).astype(bf16)
        y = _pack_bf16_pairs(_dot(hg, wd_s[w]))
        quarter = y.shape[1] // 2
        ya_ref[...] = y[:, :quarter]
        yb_ref[...] = y[:, quarter:]

    @pl.when(jnp.logical_not(active))
    def _():
        ya_ref[...] = jnp.zeros(ya_ref.shape, ya_ref.dtype)
        yb_ref[...] = jnp.zeros(yb_ref.shape, yb_ref.dtype)


def _experts(plan, layer, xa, xb, wg, wu, wd):
    tile_expert, seg_first, seg_slot, nxt, n_used = plan
    off = layer * N_EXPERTS
    n_rows, quarter = xa.shape
    D = 4 * quarter
    tm = EXPERT_TM
    row_spec = pl.BlockSpec((tm, quarter), lambda i, *_: (i, 0))
    grid_spec = pltpu.PrefetchScalarGridSpec(
        num_scalar_prefetch=5,
        grid=(n_rows // tm,),
        in_specs=[pl.BlockSpec(memory_space=pl.ANY)] * 5,
        out_specs=[row_spec, row_spec],
        scratch_shapes=[pltpu.VMEM((ROW_SLOTS, tm, quarter), xa.dtype), pltpu.VMEM((ROW_SLOTS, tm, quarter), xb.dtype),
                        pltpu.VMEM((2, D, D_EXPERT), wg.dtype), pltpu.VMEM((2, D, D_EXPERT), wu.dtype),
                        pltpu.VMEM((2, D_EXPERT, D), wd.dtype),
                        pltpu.SemaphoreType.DMA((ROW_SLOTS, 2)), pltpu.SemaphoreType.DMA((2, 3))])
    return pl.pallas_call(
        _expert_kernel,
        grid_spec=grid_spec,
        out_shape=[jax.ShapeDtypeStruct((n_rows, quarter), jnp.uint32)] * 2,
        compiler_params=_cparams("arbitrary"),
        name="moe_experts",
    )(tile_expert + off, seg_first, seg_slot, jnp.where(nxt >= 0, nxt + off, -1), n_used, xa, xb, wg, wu, wd)


def _combine_kernel(x_ref, rt_ref, g_ref, a1_ref, b1_ref, a2_ref, b2_ref, *rest, final):
    half = x_ref.shape[1] // 2
    rp = x_ref.shape[0] // COMBINE_PARTS
    for p in range(COMBINE_PARTS):
        r = slice(p * rp, (p + 1) * rp)
        h1, l1 = _unpack_bf16_pairs(jnp.concatenate([a1_ref[r, :], b1_ref[r, :]], axis=1))
        h2, l2 = _unpack_bf16_pairs(jnp.concatenate([a2_ref[r, :], b2_ref[r, :]], axis=1))
        w1 = rt_ref[r, RT_W1:RT_W1 + 1]
        w2 = rt_ref[r, RT_W2:RT_W2 + 1]
        out = jnp.concatenate([x_ref[r, 0:half] + w1 * h1 + w2 * h2, x_ref[r, half:] + w1 * l1 + w2 * l2], axis=1)
        if final:
            (o_ref,) = rest
            o_ref[r, :] = _rms(out, g_ref[...])
        else:
            w_ref, o_ref, p_ref = rest
            o_ref[r, :] = out
            p_ref[r, :] = _dot(_rms(out, g_ref[...]).astype(bf16), w_ref[...]).astype(p_ref.dtype)


def _combine(x, route, ga, gb, gain, w_next=None):
    T, D = x.shape
    tm = ROUTE_TM
    n = T // tm
    quarter = D // 4
    final = w_next is None
    slot1 = pl.BlockSpec((tm, quarter), lambda i: (i, 0))
    slot2 = pl.BlockSpec((tm, quarter), lambda i: (i + n, 0))
    row_spec = pl.BlockSpec((tm, D), lambda i: (i, 0))
    in_specs = [row_spec, pl.BlockSpec((tm, ROUTER_LANES), lambda i: (i, 0)), pl.BlockSpec((1, D), lambda i: (0, 0)),
                slot1, slot1, slot2, slot2]
    args = [x, route, gain.reshape(1, D), ga, gb, ga, gb]
    out_specs = [row_spec]
    out_shape = [jax.ShapeDtypeStruct((T, D), f32)]
    if not final:
        n_next = w_next.shape[1]
        in_specs.append(pl.BlockSpec((D, n_next), lambda i: (0, 0)))
        args.append(w_next)
        out_specs.append(pl.BlockSpec((tm, n_next), lambda i: (i, 0)))
        out_shape.append(jax.ShapeDtypeStruct((T, n_next), bf16))
    outs = pl.pallas_call(
        functools.partial(_combine_kernel, final=final),
        grid=(n,),
        in_specs=in_specs,
        out_specs=out_specs,
        out_shape=out_shape,
        compiler_params=_cparams("parallel"),
        name="moe_combine",
    )(*args)
    return outs[0] if final else outs


def _moe(layer, x, hpa, hpb, route, route_t, counts, wg, wu, wd, gain, w_next=None):
    (d0, d1), plan = _moe_plan(route_t, counts)
    n_rows = _moe_rows(x.shape[0])
    xa = _sc_scatter_pair(hpa, d0, d1, n_rows)
    xb = _sc_scatter_pair(hpb, d0, d1, n_rows)
    ya, yb = _experts(plan, layer, xa, xb, wg, wu, wd)
    dcat = jnp.concatenate([d0, d1])
    return _combine(x, route, _sc_gather(ya, dcat), _sc_gather(yb, dcat), gain, w_next)


SWA_LOCKSTEP = 2


def _swa_bias_tables():
    blk = C_BLOCK
    qi = np.arange(blk)[:, None]
    kj = np.arange(3 * blk)[None, :]
    dist = np.abs(blk + qi - kj)
    tabs = np.full((3, C_KV_HEADS, C_GROUP * blk, 3 * blk), -np.inf, np.float32)
    for case in range(3):
        ok = dist <= WINDOW
        if case == 0:
            ok = ok & (kj >= blk)
        if case == 2:
            ok = ok & (kj < 2 * blk)
        for hh in range(C_HEADS):
            slope = 2.0 ** (-8.0 * (hh + 1) / C_HEADS)
            t = np.where(ok, -slope * LOG2E * dist, -np.inf)
            kvh, g = divmod(hh, C_GROUP)
            tabs[case, kvh, g * blk:(g + 1) * blk] = t
    return jnp.asarray(tabs)


def _swa_kernel(sink_ref, q_ref, kp_ref, kc_ref, kn_ref, vp_ref, vc_ref, vn_ref, bias_ref, o_ref):
    blk = C_BLOCK
    hd = C_HEAD_DIM
    kcat = jnp.concatenate([kp_ref[0], kc_ref[0], kn_ref[0]], axis=0)
    vcat = jnp.concatenate([vp_ref[0], vc_ref[0], vn_ref[0]], axis=0)
    q = q_ref[0]
    rblk = lax.broadcasted_iota(jnp.int32, (C_GROUP * blk, 1), 0) // blk
    lane = lax.broadcasted_iota(jnp.int32, (3 * blk, LANE - hd), 1)
    ones_pad = jnp.where(lane == 0, 1.0, 0.0).astype(bf16)
    q4s, sinks = [], []
    for kvh in range(C_KV_HEADS):
        q4s.append(jnp.concatenate(
            [q[:, (kvh * C_GROUP + g) * hd:(kvh * C_GROUP + g + 1) * hd] for g in range(C_GROUP)], axis=0))
        sk = jnp.zeros((C_GROUP * blk, 1), f32)
        for g in range(C_GROUP):
            sk = jnp.where(rblk == g, sink_ref[kvh * C_GROUP + g] * LOG2E, sk)
        sinks.append(sk)
    outs = []
    for h0 in range(0, C_KV_HEADS, SWA_LOCKSTEP):
        hs = range(h0, h0 + SWA_LOCKSTEP)
        ss = [_dot_nt(q4s[h], kcat[:, h * hd:(h + 1) * hd]) + bias_ref[0, h] for h in hs]
        ms = [jnp.maximum(jnp.max(s, axis=-1, keepdims=True), sinks[h]) for s, h in zip(ss, hs)]
        ps = [jnp.exp2(s - m).astype(bf16) for s, m in zip(ss, ms)]
        pvs = [_dot(p, jnp.concatenate([vcat[:, h * hd:(h + 1) * hd], ones_pad], axis=1)) for p, h in zip(ps, hs)]
        for pv, m, h in zip(pvs, ms, hs):
            den = pv[:, hd:hd + 1] + jnp.exp2(sinks[h] - m)
            o4 = pv[:, 0:hd] / den
            outs += [o4[g * blk:(g + 1) * blk, :] for g in range(C_GROUP)]
    o_ref[0] = jnp.concatenate(outs, axis=1).astype(o_ref.dtype)


def _swa(qkv, sinks):
    B, L, _ = qkv.shape
    nb = L // C_BLOCK
    dq = C_HEADS * C_HEAD_DIM
    dkv = C_KV_HEADS * C_HEAD_DIM
    k_col = dq // dkv
    v_col = k_col + 1

    def kv_spec(col, shift):
        return pl.BlockSpec((1, C_BLOCK, dkv),
                            lambda n, b: (b, jnp.clip(n + shift, 0, nb - 1), col))

    def edge_case(n, b):
        return (jnp.where(n == 0, 0, jnp.where(n == nb - 1, 2, 1)), 0, 0, 0)

    return pl.pallas_call(
        _swa_kernel,
        grid=(nb, B),
        in_specs=[pl.BlockSpec(memory_space=pltpu.SMEM),
                  pl.BlockSpec((1, C_BLOCK, dq), lambda n, b: (b, n, 0)),
                  kv_spec(k_col, -1), kv_spec(k_col, 0), kv_spec(k_col, 1),
                  kv_spec(v_col, -1), kv_spec(v_col, 0), kv_spec(v_col, 1),
                  pl.BlockSpec((1, C_KV_HEADS, C_GROUP * C_BLOCK, 3 * C_BLOCK), edge_case)],
        out_specs=pl.BlockSpec((1, C_BLOCK, dq), lambda n, b: (b, n, 0)),
        out_shape=jax.ShapeDtypeStruct((B, L, dq), bf16),
        compiler_params=_cparams("parallel", "parallel"),
        name="swa",
    )(sinks, qkv, qkv, qkv, qkv, qkv, qkv, qkv, _swa_bias_tables())


def _hyena_filters(L, w1, b1, w2, b2, w3, b3, w4, freq):
    hp = lax.Precision.HIGHEST
    t = jnp.linspace(0.0, 1.0, L, dtype=f32)[:, None]
    wpos = (2.0 * math.pi / L) * jnp.arange(L, dtype=f32)[:, None]
    bands = jnp.linspace(1e-4, HY_BANDS - 1, HY_BANDS, dtype=f32)[None, :]
    feats = jnp.concatenate([t, jnp.cos(bands * wpos), -jnp.sin(bands * wpos)], axis=-1)
    h = jnp.sin(freq[0] * (jnp.dot(feats, w1, precision=hp) + b1))
    h = jnp.sin(freq[1] * (jnp.dot(h, w2, precision=hp) + b2))
    h = jnp.sin(freq[2] * (jnp.dot(h, w3, precision=hp) + b3))
    h = jnp.dot(h, w4, precision=hp).reshape(L, HY_ORDER, 2, B_WIDTH)
    deltas = jnp.abs(jnp.linspace(HY_MIN_DECAY, HY_MAX_DECAY, B_WIDTH, dtype=f32))
    window = jnp.exp(-t * deltas[None, :])
    return h * window[:, None, None, :]


def _pad_rows(w):
    return jnp.pad(w, ((0, SUBLANE - w.shape[0]), (0, 0)))


def _even_in_weights(w_in):
    hk = A_HEADS * A_DK
    q = w_in[:, 0:hk].reshape(D_MODEL, A_HEADS, A_DK)
    k = w_in[:, hk:2 * hk].reshape(D_MODEL, A_HEADS, A_DK)
    v = w_in[:, 2 * hk:A_QKV].reshape(D_MODEL, A_HEADS, A_DV)
    z = w_in[:, A_QKV:A_QKV + A_WIDTH].reshape(D_MODEL, A_HEADS, A_DV)
    qkvz = jnp.concatenate([q, k, v, z], axis=-1).reshape(D_MODEL, A_HEADS * 4 * A_DK)
    ab = w_in[:, A_QKV + A_WIDTH:A_QKV + A_WIDTH + 4 * A_HEADS].reshape(D_MODEL, 4, A_HEADS)
    ab = jnp.transpose(ab, (0, 2, 1))
    ab = jnp.pad(ab.reshape(D_MODEL, 4 * A_HEADS), ((0, 0), (0, LANE - 4 * A_HEADS)))
    hx = w_in[:, A_QKV + A_WIDTH + 4 * A_HEADS:]
    return jnp.concatenate([qkvz, ab, hx], axis=1).astype(bf16)


def _router_weights(w_group, w_expert):
    w = jnp.concatenate([w_group, w_expert], axis=1)
    w = jnp.pad(w, ((0, 0), (0, ROUTER_LANES - w.shape[1])))
    hi = w.astype(bf16)
    lo = (w - hi.astype(f32)).astype(bf16)
    return jnp.concatenate([hi, lo], axis=1)


def kernel(x_prompt, x_sample, norm_mix, norm_ffn, norm_final, ev_w_in, ev_conv_a, ev_alog_f, ev_alog_b, ev_dtb_f, ev_dtb_b, ev_onorm, ev_conv_b, hy_w1, hy_b1, hy_w2, hy_b2, hy_w3, hy_b3, hy_w4, hy_freq, hy_dbias, ev_w_out, od_w_qkv, od_sinks, od_w_out, moe_w_group, moe_w_expert, moe_w_gate, moe_w_up, moe_w_down):
    L = x_prompt.shape[1]
    n_layers = moe_w_gate.shape[0]
    wg = moe_w_gate.reshape(n_layers * N_EXPERTS, D_MODEL, D_EXPERT).astype(bf16)
    wu = moe_w_up.reshape(n_layers * N_EXPERTS, D_MODEL, D_EXPERT).astype(bf16)
    wd = moe_w_down.reshape(n_layers * N_EXPERTS, D_EXPERT, D_MODEL).astype(bf16)

    w_in = _even_in_weights(ev_w_in[0])
    n_qkvz = A_HEADS * 4 * A_DK
    n_ab = A_HEADS * LANE
    n_hx = 3 * B_WIDTH
    hk = A_HEADS * A_DK
    ca = ev_conv_a[0]
    conv_a = jnp.concatenate([ca[0:hk].reshape(A_HEADS, A_DK, 3), ca[hk:2 * hk].reshape(A_HEADS, A_DK, 3),
                              ca[2 * hk:].reshape(A_HEADS, A_DV, 3)], axis=1)
    conv_a = jnp.pad(jnp.transpose(conv_a, (0, 2, 1)), ((0, 0), (0, SUBLANE - 3), (0, 0)))
    gate_sc = jnp.stack([ev_alog_f[0], ev_alog_b[0], ev_dtb_f[0], ev_dtb_b[0]], axis=1)
    filt = _hyena_filters(L, hy_w1[0], hy_b1[0], hy_w2[0], hy_b2[0], hy_w3[0], hy_b3[0], hy_w4[0], hy_freq[0])
    hsum = (filt[:, :, 0] + filt[:, :, 1]).reshape(L, HY_ORDER * B_WIDTH)
    hdiff = (filt[:, :, 0] - filt[:, :, 1]).reshape(L, HY_ORDER * B_WIDTH)
    kr, ki = _filter_spectrum(hsum, hdiff)
    fft_tables = _hyena_tables(L)
    conv_b = _pad_rows(ev_conv_b[0].T)
    dbias = _pad_rows(hy_dbias[0])
    w_out = ev_w_out[0].astype(bf16)
    routers = [_router_weights(moe_w_group[l], moe_w_expert[l]) for l in range(n_layers)]
    n_qkv = od_w_qkv.shape[-1]
    dq = C_HEADS * C_HEAD_DIM
    w_qkv = jnp.concatenate([od_w_qkv[0][:, :dq] * (LOG2E * C_HEAD_DIM ** -0.5), od_w_qkv[0][:, dq:]], axis=1)
    w_qkv = w_qkv.astype(bf16)
    w_att_out = od_w_out[0].astype(bf16)

    def trunk(xb):
        B = xb.shape[0]
        T = B * L
        x = xb.reshape(T, D_MODEL)
        qkvz, abh, hx = _even_in_proj(x, norm_mix[0], w_in)
        o_a = _deltanet(qkvz.reshape(B, L, n_qkvz), abh.reshape(B, L, n_ab), conv_a, gate_sc, ev_onorm[0])
        o_b = _hyena_fft(hx.reshape(B, L, n_hx), conv_b, fft_tables, kr, ki, dbias)
        x, hpa, hpb, route, route_t, counts = _proj_route(
            x, (o_a.reshape(T, A_WIDTH), o_b.reshape(T, B_WIDTH)), (w_out[:A_WIDTH], w_out[A_WIDTH:]),
            norm_ffn[0], routers[0])
        x, qkv = _moe(0, x, hpa, hpb, route, route_t, counts, wg, wu, wd, norm_mix[1], w_qkv)
        o_c = _swa(qkv.reshape(B, L, n_qkv), od_sinks[0])
        x, hpa, hpb, route, route_t, counts = _proj_route(
            x, (o_c.reshape(T, C_HEADS * C_HEAD_DIM),), (w_att_out,), norm_ffn[1], routers[1])
        y = _moe(1, x, hpa, hpb, route, route_t, counts, wg, wu, wd, norm_final)
        return y.reshape(B, L, D_MODEL)

    return (trunk(x_prompt), trunk(x_sample))
```

```python
import functools
import math

import jax
import jax.numpy as jnp
import numpy as np
from jax import lax
from jax.experimental import pallas as pl
from jax.experimental.pallas import tpu as pltpu
from jax.experimental.pallas import tpu_sc as plsc

f32 = jnp.float32
bf16 = jnp.bfloat16

EPS = 1e-6
D_MODEL = 1024

A_HEADS = 4
A_DK = 128
A_DV = 128
A_WIDTH = A_HEADS * A_DV
A_QKV = 2 * A_HEADS * A_DK + A_HEADS * A_DV
DELTA_CHUNK = 128
INV_BASE = 8
DELTA_GROUP = 8
DELTA_WAVES = 2

B_WIDTH = D_MODEL - A_WIDTH
HY_ORDER = 2
HY_EMB = 33
HY_BANDS = (HY_EMB - 1) // 2
HY_TARGET = 1e-2
HY_MIN_DECAY = math.log(HY_TARGET) / 1.5
HY_MAX_DECAY = math.log(HY_TARGET) / 0.3
HY_CT = 256
HY_ROWS = 256

C_HEADS = 16
C_KV_HEADS = 4
C_HEAD_DIM = 64
C_GROUP = C_HEADS // C_KV_HEADS
WINDOW = 128
C_BLOCK = 128

N_GROUPS = 4
EXPERTS_PER_GROUP = 8
N_EXPERTS = N_GROUPS * EXPERTS_PER_GROUP
D_EXPERT = 256
ROUTER_LANES = 128

LANE = 128
SUBLANE = 8
SUBLANE_BF16 = 16
ROW_TM = 512
VMEM_LIMIT = 56 * 1024 * 1024
LOG2E = 1.4426950408889634


def _cparams(*sem):
    return pltpu.CompilerParams(dimension_semantics=sem, vmem_limit_bytes=VMEM_LIMIT)


def _dot(a, b):
    return jnp.dot(a, b, preferred_element_type=f32)


def _dot_nt(a, b):
    return lax.dot_general(a, b, (((1,), (1,)), ((), ())), preferred_element_type=f32)


def _dot_tn(a, b):
    return lax.dot_general(a, b, (((0,), (0,)), ((), ())), preferred_element_type=f32)


def _sigmoid(x):
    return 1.0 / (1.0 + jnp.exp(-x))


def _rms(x, g):
    return x * lax.rsqrt(jnp.mean(x * x, axis=-1, keepdims=True) + EPS) * g


def _even_in_kernel(x_ref, g_ref, w_ref, qkvz_ref, ab_ref, hx_ref):
    h = _rms(x_ref[...], g_ref[...]).astype(bf16)
    n_qkvz = qkvz_ref.shape[1]
    qkvz_ref[...] = _dot(h, w_ref[:, 0:n_qkvz]).astype(qkvz_ref.dtype)
    ab = _dot(h, w_ref[:, n_qkvz:n_qkvz + LANE])
    for hd in range(A_HEADS):
        ab_ref[:, hd * LANE:(hd + 1) * LANE] = ab if hd == 0 else pltpu.roll(ab, LANE - 4 * hd, 1)
    hx_ref[...] = _dot(h, w_ref[:, n_qkvz + LANE:]).astype(hx_ref.dtype)


def _even_in_proj(x, gain, w):
    T, D = x.shape
    tm = ROW_TM
    n_qkvz = A_HEADS * 4 * A_DK
    n_ab = A_HEADS * LANE
    n_hx = 3 * B_WIDTH
    return pl.pallas_call(
        _even_in_kernel,
        grid=(T // tm,),
        in_specs=[pl.BlockSpec((tm, D), lambda i: (i, 0)),
                  pl.BlockSpec((1, D), lambda i: (0, 0)),
                  pl.BlockSpec(w.shape, lambda i: (0, 0))],
        out_specs=[pl.BlockSpec((tm, n_qkvz), lambda i: (i, 0)),
                   pl.BlockSpec((tm, n_ab), lambda i: (i, 0)),
                   pl.BlockSpec((tm, n_hx), lambda i: (i, 0))],
        out_shape=[jax.ShapeDtypeStruct((T, n_qkvz), bf16),
                   jax.ShapeDtypeStruct((T, n_ab), f32),
                   jax.ShapeDtypeStruct((T, n_hx), bf16)],
        compiler_params=_cparams("parallel"),
        name="even_in_proj",
    )(x, gain.reshape(1, D), w)


MK_LOWER, MK_UPPER, MK_BASE, MK_LEVEL0 = 0, 1, 2, 3
N_LEVELS = int(math.log2(DELTA_CHUNK // INV_BASE))
MK_EYE = MK_LEVEL0 + N_LEVELS
N_MASKS = MK_EYE + 1


def _delta_masks():
    c = np.arange(DELTA_CHUNK)[:, None]
    s = np.arange(DELTA_CHUNK)[None, :]
    m = [s <= c, s >= c, (c // INV_BASE == s // INV_BASE) & (c != s)]
    b = INV_BASE
    while b < DELTA_CHUNK:
        m.append((c // (2 * b) == s // (2 * b)) & (c // b != s // b))
        b *= 2
    m.append(c == s)
    m = np.stack(m).astype(np.float32)
    return jnp.asarray(m), jnp.asarray(m, dtype=bf16)


def _tri_inverse(mats, mk_ref, mk16_ref):
    ps = [a * mk_ref[MK_BASE] for a in mats]
    ts = [mk_ref[MK_EYE] - p for p in ps]
    n = 2
    while n < INV_BASE:
        p16s = [p.astype(bf16) for p in ps]
        ps = [_dot(p16, p16) for p16 in p16s]
        ts = [t + _dot(t.astype(bf16), p.astype(bf16)) for t, p in zip(ts, ps)]
        n *= 2
    a16s = [a.astype(bf16) for a in mats]
    t16s = [t.astype(bf16) for t in ts]
    for lvl in range(N_LEVELS):
        xs = [_dot(a16 * mk16_ref[MK_LEVEL0 + lvl], t16) for a16, t16 in zip(a16s, t16s)]
        t16s = [t16 - _dot(t16, x.astype(bf16)).astype(bf16) for t16, x in zip(t16s, xs)]
    return t16s


def _deltanet_kernel(sc_ref, x_ref, ab_ref, cw_ref, on_ref, mk_ref, mk16_ref, o_ref, mp_s, nr_s, gl_s, o_s):
    hd = pl.program_id(1)
    L = x_ref.shape[1]
    C = DELTA_CHUNK
    nchunk = L // C
    group = math.gcd(DELTA_GROUP, nchunk)
    nqkv = 3 * A_DK
    halo = SUBLANE_BF16

    lane = lax.broadcasted_iota(jnp.int32, (1, LANE), 1)
    alog = jnp.where(lane == 2, sc_ref[hd, 0], jnp.where(lane == 3, sc_ref[hd, 1], 0.0))
    dtb = jnp.where(lane == 2, sc_ref[hd, 2], jnp.where(lane == 3, sc_ref[hd, 3], 0.0))
    neg_a = jnp.where((lane == 2) | (lane == 3), -jnp.exp(alog), 0.0)
    rowi = lax.broadcasted_iota(jnp.int32, (C, 1), 0)
    cw = cw_ref[0]

    def chunk_inputs(i):
        r0 = pl.multiple_of(i * C, C)
        xc = x_ref[0, pl.ds(r0, C), 0:nqkv].astype(f32)
        rp = pl.multiple_of(jnp.maximum(r0 - halo, 0), halo)
        rn = pl.multiple_of(jnp.minimum(r0 + C, L - halo), halo)
        prow = x_ref[0, pl.ds(rp, halo), 0:nqkv][halo - 1:halo, :].astype(f32)
        nrow = x_ref[0, pl.ds(rn, halo), 0:nqkv][0:1, :].astype(f32)
        prow = jnp.where(i > 0, prow, 0.0)
        nrow = jnp.where(i < nchunk - 1, nrow, 0.0)
        prev = jnp.where(rowi == 0, prow, pltpu.roll(xc, 1, 0))
        nxt = jnp.where(rowi == C - 1, nrow, pltpu.roll(xc, C - 1, 0))
        y = prev * cw[0:1, :] + xc * cw[1:2, :] + nxt * cw[2:3, :]
        y = y * _sigmoid(y)
        q = y[:, 0:A_DK]
        k = y[:, A_DK:2 * A_DK]
        vc = y[:, 2 * A_DK:nqkv]
        qc = q * lax.rsqrt(jnp.sum(q * q, axis=-1, keepdims=True) + EPS) * (A_DK ** -0.5)
        kc = k * lax.rsqrt(jnp.sum(k * k, axis=-1, keepdims=True) + EPS)

        abc = ab_ref[0, pl.ds(r0, C), :]
        zg = abc + dtb
        g = neg_a * (jnp.maximum(zg, 0.0) + jnp.log(1.0 + jnp.exp(-jnp.abs(zg))))
        low16 = mk16_ref[MK_LOWER]
        g1 = g.astype(bf16)
        e1 = g - g1.astype(f32)
        g2 = e1.astype(bf16)
        g3 = (e1 - g2.astype(f32)).astype(bf16)
        pre = _dot(low16, g1) + _dot(low16, g2) + _dot(low16, g3)
        tot = pre[C - 1:C, :]
        suf = tot - pre + g
        beta_all = _sigmoid(abc)

        k16 = kc.astype(bf16)
        kk = _dot_nt(k16, k16)
        qk = _dot_nt(qc.astype(bf16), k16)

        chains = []
        for rev in (0, 1):
            beta = beta_all[:, rev:rev + 1]
            gcc = (suf if rev else pre)[:, 2 + rev:3 + rev]
            tot11 = tot[:, 2 + rev:3 + rev]
            keep = mk_ref[MK_UPPER if rev else MK_LOWER]
            bc = jnp.broadcast_to(gcc * LOG2E, (C, C))
            decay = jnp.exp2((bc - bc.T) * keep) * keep
            a = (beta * kk) * decay
            eg = jnp.exp(gcc)
            chains.append(dict(
                i=i, rev=rev, a=a, gl=jnp.exp(tot11), qd=qc * eg,
                rhs=jnp.concatenate([vc * beta, kc * (beta * eg)], axis=1).astype(bf16),
                kd16=(kc * jnp.exp(tot11 - gcc)).astype(bf16),
                qk16=(qk * decay).astype(bf16)))
        return chains

    def prep_group(first_chunk):
        chains = []
        for gi in range(group):
            chains += chunk_inputs(first_chunk + gi)
        t16s = _tri_inverse([c["a"] for c in chains], mk_ref, mk16_ref)
        uws = [_dot(t16, c["rhs"]).astype(bf16) for t16, c in zip(t16s, chains)]
        kts = [_dot_tn(c["kd16"], uw) for c, uw in zip(chains, uws)]
        prs = [_dot(c["qk16"], uw) for c, uw in zip(chains, uws)]
        for c, kt, pr in zip(chains, kts, prs):
            rev, i = c["rev"], c["i"]
            mp_s[rev, i, 0:A_DK, :] = (-kt[:, A_DV:]).astype(bf16)
            mp_s[rev, i, A_DK:A_DK + C, :] = (c["qd"] - pr[:, A_DV:]).astype(bf16)
            nr_s[rev, i, 0:A_DK, :] = kt[:, 0:A_DV]
            nr_s[rev, i, A_DK:A_DK + C, :] = pr[:, 0:A_DV]
            gl_s[rev, i] = jnp.broadcast_to(c["gl"], (SUBLANE, LANE))

    def prep(it, carry):
        for wv in range(waves):
            prep_group((it * waves + wv) * group)
        return carry

    waves = math.gcd(DELTA_WAVES, nchunk // group)
    lax.fori_loop(0, nchunk // (group * waves), prep, 0)

    def scan(i, carry):
        out = []
        for rev, S in enumerate(carry):
            j = (nchunk - 1 - i) if rev else i
            y = _dot(mp_s[rev, j], S.astype(bf16))
            nr = nr_s[rev, j]
            o_s[rev, pl.ds(pl.multiple_of(j * C, C), C), :] = y[A_DK:, :] + nr[A_DK:, :]
            out.append(S * gl_s[rev, j][0:1, :] + y[0:A_DK, :] + nr[0:A_DK, :])
        return tuple(out)

    S0 = jnp.zeros((A_DK, A_DV), f32)
    lax.fori_loop(0, nchunk, scan, (S0, S0))

    def finish(it, carry):
        for gi in range(group):
            r0 = pl.multiple_of((it * group + gi) * C, C)
            o = o_s[0, pl.ds(r0, C), :] + o_s[1, pl.ds(r0, C), :]
            zg = x_ref[0, pl.ds(r0, C), nqkv:nqkv + A_DV].astype(f32)
            o_ref[0, pl.ds(r0, C), :] = (_rms(o, on_ref[...]) * (zg * _sigmoid(zg))).astype(o_ref.dtype)
        return carry

    lax.fori_loop(0, nchunk // group, finish, 0)


def _deltanet(qkvz, abh, conv_w, gate_sc, onorm):
    B, L, _ = qkvz.shape
    C = DELTA_CHUNK
    nchunk = L // C
    hw = 4 * A_DK
    scratch = [pltpu.VMEM((2, nchunk, A_DK + C, A_DV), bf16),
               pltpu.VMEM((2, nchunk, A_DK + C, A_DV), f32),
               pltpu.VMEM((2, nchunk, SUBLANE, LANE), f32),
               pltpu.VMEM((2, L, A_DV), f32)]
    return pl.pallas_call(
        _deltanet_kernel,
        grid=(B, A_HEADS),
        in_specs=[pl.BlockSpec(memory_space=pltpu.SMEM),
                  pl.BlockSpec((1, L, hw), lambda b, h: (b, 0, h)),
                  pl.BlockSpec((1, L, LANE), lambda b, h: (b, 0, h)),
                  pl.BlockSpec((1, SUBLANE, 3 * A_DK), lambda b, h: (h, 0, 0)),
                  pl.BlockSpec((1, A_DV), lambda b, h: (0, 0)),
                  pl.BlockSpec((N_MASKS, C, C), lambda b, h: (0, 0, 0)),
                  pl.BlockSpec((N_MASKS, C, C), lambda b, h: (0, 0, 0))],
        out_specs=pl.BlockSpec((1, L, A_DV), lambda b, h: (b, 0, h)),
        out_shape=jax.ShapeDtypeStruct((B, L, A_WIDTH), bf16),
        scratch_shapes=scratch,
        compiler_params=_cparams("parallel", "parallel"),
        name="deltanet",
    )(gate_sc, qkvz, abh, conv_w, onorm.reshape(1, A_DV), *_delta_masks())


def _filter_spectrum_kernel(c_ref, s_ref, hs_ref, hd_ref, kr_ref, ki_ref):
    L = c_ref.shape[1]
    rows = kr_ref.shape[1]
    scale = 1.0 / (2 * L)

    def split(x):
        hi = x.astype(bf16)
        return hi, (x - hi.astype(f32)).astype(bf16)

    hs_hi, hs_lo = split(hs_ref[...])
    hd_hi, hd_lo = split(hd_ref[...])
    for k1 in range(kr_ref.shape[0]):
        c = c_ref[k1 * rows:(k1 + 1) * rows, :]
        s = s_ref[k1 * rows:(k1 + 1) * rows, :]
        kr_ref[k1] = (_dot(c, hs_hi) + _dot(c, hs_lo)) * scale
        ki_ref[k1] = (_dot(s, hd_hi) + _dot(s, hd_lo)) * (-scale)


def _filter_spectrum(hsum, hdiff):
    L, n = hsum.shape
    ct = HY_CT
    n_fft = 2 * L
    n2 = L // FFT_Z1
    d = jnp.arange(L, dtype=jnp.int32)[None, :]
    a1 = ((jnp.arange(FFT_K1, dtype=jnp.int32)[:, None] * d) % n_fft).astype(f32) * (2.0 * math.pi / n_fft)
    a2 = ((jnp.arange(n2, dtype=jnp.int32)[:, None] * d) % n2).astype(f32) * (2.0 * math.pi / n2)
    c1, s1 = jnp.cos(a1)[:, None, :], jnp.sin(a1)[:, None, :]
    c2, s2 = jnp.cos(a2)[None, :, :], jnp.sin(a2)[None, :, :]
    cmat = (c1 * c2 - s1 * s2).reshape(FFT_K1 * n2, L).astype(bf16)
    smat = (s1 * c2 + c1 * s2).reshape(FFT_K1 * n2, L).astype(bf16)
    once = pl.Buffered(1)
    return pl.pallas_call(
        _filter_spectrum_kernel,
        grid=(n // ct,),
        in_specs=[pl.BlockSpec((FFT_K1 * n2, L), lambda j: (0, 0), pipeline_mode=once),
                  pl.BlockSpec((FFT_K1 * n2, L), lambda j: (0, 0), pipeline_mode=once),
                  pl.BlockSpec((L, ct), lambda j: (0, j)),
                  pl.BlockSpec((L, ct), lambda j: (0, j))],
        out_specs=[pl.BlockSpec((FFT_K1, n2, ct), lambda j: (0, 0, j)),
                   pl.BlockSpec((FFT_K1, n2, ct), lambda j: (0, 0, j))],
        out_shape=[jax.ShapeDtypeStruct((FFT_K1, n2, n), f32),
                   jax.ShapeDtypeStruct((FFT_K1, n2, n), f32)],
        compiler_params=_cparams("parallel"),
        name="filter_spectrum",
    )(cmat, smat, hsum, hdiff)


FFT_N1 = 16
FFT_K1 = FFT_N1 // 2 + 1
FFT_Z1 = FFT_N1 // 2
FFT_ROWS = 32


def _lincomb(terms):
    groups = {}
    for c, a in terms:
        if abs(c) > 1e-9:
            groups.setdefault(round(abs(c), 9), []).append((c > 0, a))
    total = None
    for mag, items in groups.items():
        pos = [a for p, a in items if p]
        neg = [a for p, a in items if not p]
        s = None
        for a in pos:
            s = a if s is None else s + a
        for a in neg:
            s = -a if s is None else s - a
        if mag != 1.0:
            s = s * mag
        total = s if total is None else total + s
    return total


def _hyena_fft_kernel(m2f_ref, m2i_ref, twr_ref, twi_ref, x1_ref, x2_ref, v_ref, w1_ref, w2_ref, wv_ref,
                      kr0_ref, ki0_ref, kr1_ref, ki1_ref, db_ref, o_ref, z_s, g_s, d_s, y_s, b_s):
    L, ct = z_s.shape
    n2 = L // FFT_Z1
    R = min(HY_ROWS, L)
    nt = L // R
    r1 = min(FFT_ROWS, n2)
    halo = SUBLANE_BF16
    rowi = lax.broadcasted_iota(jnp.int32, (R, 1), 0)
    ang = 2.0 * math.pi / FFT_N1
    c16 = [[math.cos(ang * a * b) for b in range(FFT_K1)] for a in range(FFT_Z1)]
    s16 = [[math.sin(ang * a * b) for b in range(FFT_K1)] for a in range(FFT_Z1)]

    def conv_into(dst, x_ref, w_ref):
        def body(t, carry):
            r0 = pl.multiple_of(t * R, R)
            x = x_ref[0, pl.ds(r0, R), :].astype(f32)
            rp = pl.multiple_of(jnp.maximum(r0 - halo, 0), halo)
            rn = pl.multiple_of(jnp.minimum(r0 + R, L - halo), halo)
            prow = jnp.where(t > 0, x_ref[0, pl.ds(rp, halo), :][halo - 1:halo, :].astype(f32), 0.0)
            nrow = jnp.where(t < nt - 1, x_ref[0, pl.ds(rn, halo), :][0:1, :].astype(f32), 0.0)
            prev = jnp.where(rowi == 0, prow, pltpu.roll(x, 1, 0))
            nxt = jnp.where(rowi == R - 1, nrow, pltpu.roll(x, R - 1, 0))
            w = w_ref[...]
            dst[pl.ds(r0, R), :] = prev * w[0:1, :] + x * w[1:2, :] + nxt * w[2:3, :]
            return carry
        lax.fori_loop(0, nt, body, 0)

    def cols(k1):
        return slice(k1 * ct, (k1 + 1) * ct)

    conv_into(z_s, v_ref, wv_ref)
    stages = ((x1_ref, w1_ref, kr0_ref, ki0_ref), (x2_ref, w2_ref, kr1_ref, ki1_ref))
    for o, (g_ref, gw_ref, kr_ref, ki_ref) in enumerate(stages):
        conv_into(g_s, g_ref, gw_ref)

        def forward1(rt, carry):
            r0 = pl.multiple_of(rt * r1, r1)
            zs = [z_s[pl.ds(a * n2 + r0, r1), :] for a in range(FFT_Z1)]
            for k1 in range(FFT_K1):
                ar = _lincomb([(c16[a][k1], zs[a]) for a in range(FFT_Z1)])
                ai = _lincomb([(-s16[a][k1], zs[a]) for a in range(FFT_Z1)])
                twr = twr_ref[pl.ds(r0, r1), cols(k1)]
                twi = twi_ref[pl.ds(r0, r1), cols(k1)]
                if ai is None:
                    pr, pi = ar * twr, -(ar * twi)
                else:
                    pr, pi = ar * twr + ai * twi, ai * twr - ar * twi
                d_s[pl.ds(r0, r1), cols(k1)] = pr.astype(bf16)
                d_s[pl.ds(n2 + r0, r1), cols(k1)] = pi.astype(bf16)
            return carry

        lax.fori_loop(0, n2 // r1, forward1, 0)

        for k1 in range(FFT_K1):
            xk = _dot(m2f_ref[...], d_s[:, cols(k1)])
            xr, xi = xk[0:n2, :], xk[n2:, :]
            kr, ki = kr_ref[k1], ki_ref[k1]
            y_s[0:n2, cols(k1)] = (xr * kr - xi * ki).astype(bf16)
            y_s[n2:2 * n2, cols(k1)] = (xr * ki + xi * kr).astype(bf16)

        for k1 in range(FFT_K1):
            bk = _dot(m2i_ref[...], y_s[:, cols(k1)])
            br, bi = bk[0:n2, :], bk[n2:, :]
            twr, twi = twr_ref[:, cols(k1)], twi_ref[:, cols(k1)]
            b_s[0:n2, cols(k1)] = br * twr - bi * twi
            b_s[n2:2 * n2, cols(k1)] = br * twi + bi * twr

        def inverse1(rt, carry):
            r0 = pl.multiple_of(rt * r1, r1)
            brs = [b_s[pl.ds(r0, r1), cols(j)] for j in range(FFT_K1)]
            bis = [b_s[pl.ds(n2 + r0, r1), cols(j)] for j in range(FFT_K1)]
            for a in range(FFT_Z1):
                terms = [(1.0, brs[0]), (-1.0 if a % 2 else 1.0, brs[FFT_K1 - 1])]
                for j in range(1, FFT_K1 - 1):
                    terms += [(2.0 * c16[a][j], brs[j]), (-2.0 * s16[a][j], bis[j])]
                rows = pl.ds(a * n2 + r0, r1)
                z = g_s[rows, :] * (_lincomb(terms) + z_s[rows, :] * db_ref[o:o + 1, :])
                if o + 1 < len(stages):
                    z_s[rows, :] = z
                else:
                    o_ref[0, rows, :] = z.astype(o_ref.dtype)
            return carry

        lax.fori_loop(0, n2 // r1, inverse1, 0)


def _hyena_tables(L):
    n_fft = 2 * L
    n2 = L // FFT_Z1
    ct = HY_CT
    idx = jnp.arange(n2, dtype=jnp.int32)
    a2 = ((idx[:, None] * idx[None, :]) % n2).astype(f32) * (2.0 * math.pi / n2)
    c2, s2 = jnp.cos(a2), jnp.sin(a2)
    m2f = jnp.block([[c2, s2], [-s2, c2]]).astype(bf16)
    m2i = jnp.block([[c2, -s2], [s2, c2]]).astype(bf16)
    at = (idx[:, None] * jnp.arange(FFT_K1, dtype=jnp.int32)[None, :]).astype(f32) * (2.0 * math.pi / n_fft)
    twr = jnp.broadcast_to(jnp.cos(at)[:, :, None], (n2, FFT_K1, ct)).reshape(n2, FFT_K1 * ct)
    twi = jnp.broadcast_to(jnp.sin(at)[:, :, None], (n2, FFT_K1, ct)).reshape(n2, FFT_K1 * ct)
    return m2f, m2i, twr, twi


def _hyena_fft(hx, conv_w, tables, kr, ki, dbias):
    B, L, _ = hx.shape
    ct = HY_CT
    nct = B_WIDTH // ct
    n2 = L // FFT_Z1
    wide = FFT_K1 * ct
    once = pl.Buffered(1)
    m2f, m2i, twr, twi = tables

    def xspec(part):
        return pl.BlockSpec((1, L, ct), lambda c, b: (b, 0, part * nct + c))

    def wspec(part):
        return pl.BlockSpec((SUBLANE, ct), lambda c, b: (0, part * nct + c))

    def kspec(order):
        return pl.BlockSpec((FFT_K1, n2, ct), lambda c, b: (0, 0, order * nct + c), pipeline_mode=once)

    def const(shape):
        return pl.BlockSpec(shape, lambda c, b: (0, 0), pipeline_mode=once)

    return pl.pallas_call(
        _hyena_fft_kernel,
        grid=(nct, B),
        in_specs=[const((2 * n2, 2 * n2)), const((2 * n2, 2 * n2)), const((n2, wide)), const((n2, wide)),
                  xspec(0), xspec(1), xspec(2), wspec(0), wspec(1), wspec(2),
                  kspec(0), kspec(0), kspec(1), kspec(1),
                  pl.BlockSpec((SUBLANE, ct), lambda c, b: (0, c))],
        out_specs=pl.BlockSpec((1, L, ct), lambda c, b: (b, 0, c)),
        out_shape=jax.ShapeDtypeStruct((B, L, B_WIDTH), bf16),
        scratch_shapes=[pltpu.VMEM((L, ct), f32), pltpu.VMEM((L, ct), f32),
                        pltpu.VMEM((2 * n2, wide), bf16), pltpu.VMEM((2 * n2, wide), bf16),
                        pltpu.VMEM((2 * n2, wide), f32)],
        compiler_params=_cparams("parallel", "parallel"),
        name="hyena_fft",
    )(m2f, m2i, twr, twi, hx, hx, hx, conv_w, conv_w, conv_w, kr, ki, kr, ki, dbias)


RT_E1, RT_E2, RT_W1, RT_W2, RT_R1, RT_R2 = 0, 1, 2, 3, 4, 5
ROUTE_TM = ROW_TM
PROJ_PARTS = 2
COMBINE_PARTS = 4
EXPERT_TM = 512
ROW_SLOTS = 4
SC_WINDOW = 128


def _pack_bf16_pairs(x):
    n = x.shape[1] // 2
    bits = lax.bitcast_convert_type(x.astype(bf16).astype(f32), jnp.uint32)
    return bits[:, :n] | (bits[:, n:] >> 16)


def _unpack_bf16_pairs(p):
    hi = lax.bitcast_convert_type(p & jnp.uint32(0xFFFF0000), f32)
    lo = lax.bitcast_convert_type(p << 16, f32)
    return hi, lo


ROUTER_ROWS = 40


def _route_topk(lt):
    row = lax.broadcasted_iota(jnp.int32, lt.shape, 0)
    neg = -jnp.inf
    big = ROUTER_LANES
    gl = jnp.where(row < N_GROUPS, lt, neg)
    gmax = jnp.max(gl, axis=0, keepdims=True)
    gsel = jnp.min(jnp.where(gl == gmax, row, big), axis=0, keepdims=True)
    psel = 1.0 / jnp.sum(jnp.exp(gl - gmax), axis=0, keepdims=True)
    first = N_GROUPS + gsel * EXPERTS_PER_GROUP
    el = jnp.where((row >= first) & (row < first + EXPERTS_PER_GROUP), lt, neg)
    t1 = jnp.max(el, axis=0, keepdims=True)
    i1 = jnp.min(jnp.where(el == t1, row, big), axis=0, keepdims=True)
    el2 = jnp.where(row == i1, neg, el)
    t2 = jnp.max(el2, axis=0, keepdims=True)
    i2 = jnp.min(jnp.where(el2 == t2, row, big), axis=0, keepdims=True)
    ex = jnp.exp(t2 - t1)
    w1 = psel / (1.0 + ex)
    w2 = ex * w1
    return i1 - N_GROUPS, i2 - N_GROUPS, w1, w2


def _proj_route_kernel(x_ref, *refs, n_in):
    a_refs = refs[0:n_in]
    w_refs = refs[n_in:2 * n_in]
    g_ref, wr_ref, up_ref, xo_ref, hpa_ref, hpb_ref, rt_ref, rtt_ref, cnt_ref = refs[2 * n_in:]
    tm = xo_ref.shape[0]
    rp = tm // PROJ_PARTS
    parts = [slice(p * rp, (p + 1) * rp) for p in range(PROJ_PARTS)]
    xs = []
    for r in parts:
        x = x_ref[r, :]
        for a_ref, w_ref in zip(a_refs, w_refs):
            x = x + _dot(a_ref[r, :], w_ref[...])
        xo_ref[r, :] = x
        xs.append(x)
    hs = [_rms(x, g_ref[...]) for x in xs]
    for r, h in zip(parts, hs):
        hp = _pack_bf16_pairs(h)
        quarter = hp.shape[1] // 2
        hpa_ref[r, :] = hp[:, :quarter]
        hpb_ref[r, :] = hp[:, quarter:]
    tops = []
    for h in hs:
        h_hi = h.astype(bf16)
        h_lo = (h - h_hi.astype(f32)).astype(bf16)
        both = _dot(h_hi, wr_ref[...])
        logits = both[:, :ROUTER_LANES] + both[:, ROUTER_LANES:] + _dot(h_lo, wr_ref[:, :ROUTER_LANES])
        tops.append(_route_topk(logits.T[0:ROUTER_ROWS, :]))
    row = lax.broadcasted_iota(jnp.int32, (ROUTER_LANES, rp), 0)
    onehots = [jnp.where((row == e1) | (row == e2), 1.0, 0.0) for e1, e2, _, _ in tops]
    onehot16 = jnp.concatenate(onehots, axis=1).astype(bf16)
    before = _dot(onehot16, up_ref[...])
    for p, (r, (e1, e2, w1, w2)) in enumerate(zip(parts, tops)):
        bp = before[:, r]
        r1 = jnp.sum(jnp.where(row == e1, bp, 0.0), axis=0, keepdims=True)
        r2 = jnp.sum(jnp.where(row == e2, bp, 0.0), axis=0, keepdims=True)
        rec = jnp.zeros((ROUTER_LANES, rp), f32)
        for ln, val in ((RT_E1, e1.astype(f32)), (RT_E2, e2.astype(f32)), (RT_W1, w1), (RT_W2, w2),
                        (RT_R1, r1), (RT_R2, r2)):
            rec = jnp.where(row == ln, val, rec)
        rtt_ref[:, r] = rec[0:rtt_ref.shape[0], :]
        rt_ref[r, :] = rec.T
    ones = jnp.ones((cnt_ref.shape[1], tm), bf16)
    cnt_ref[0] = _dot_nt(ones, onehot16)


def _proj_route(x, acts, ws, gain, w_router):
    T, D = x.shape
    tm = ROUTE_TM
    n_in = len(acts)
    up = jnp.asarray(np.triu(np.ones((tm, tm), np.float32), 1), dtype=bf16)
    in_specs = [pl.BlockSpec((tm, D), lambda i: (i, 0))]
    in_specs += [pl.BlockSpec((tm, a.shape[1]), lambda i: (i, 0)) for a in acts]
    in_specs += [pl.BlockSpec(w.shape, lambda i: (0, 0)) for w in ws]
    in_specs += [pl.BlockSpec((1, D), lambda i: (0, 0)),
                 pl.BlockSpec((D, 2 * ROUTER_LANES), lambda i: (0, 0)),
                 pl.BlockSpec((tm, tm), lambda i: (0, 0))]
    return pl.pallas_call(
        functools.partial(_proj_route_kernel, n_in=n_in),
        grid=(T // tm,),
        in_specs=in_specs,
        out_specs=[pl.BlockSpec((tm, D), lambda i: (i, 0)),
                   pl.BlockSpec((tm, D // 4), lambda i: (i, 0)),
                   pl.BlockSpec((tm, D // 4), lambda i: (i, 0)),
                   pl.BlockSpec((tm, ROUTER_LANES), lambda i: (i, 0)),
                   pl.BlockSpec((SUBLANE, tm), lambda i: (0, i)),
                   pl.BlockSpec((1, SUBLANE, ROUTER_LANES), lambda i: (i, 0, 0))],
        out_shape=[jax.ShapeDtypeStruct((T, D), f32),
                   jax.ShapeDtypeStruct((T, D // 4), jnp.uint32),
                   jax.ShapeDtypeStruct((T, D // 4), jnp.uint32),
                   jax.ShapeDtypeStruct((T, ROUTER_LANES), f32),
                   jax.ShapeDtypeStruct((SUBLANE, T), f32),
                   jax.ShapeDtypeStruct((T // tm, SUBLANE, ROUTER_LANES), f32)],
        compiler_params=_cparams("parallel"),
        name="proj_route",
    )(x, *acts, *ws, gain.reshape(1, D), w_router, up)


def _moe_rows(T):
    return 2 * T + N_EXPERTS * EXPERT_TM


def _moe_plan(route_t, counts):
    T = route_t.shape[1]
    tm = ROUTE_TM
    cnt = counts[:, 0, :N_EXPERTS].astype(jnp.int32)
    total = jnp.sum(cnt, axis=0)
    padded = ((total + EXPERT_TM - 1) // EXPERT_TM) * EXPERT_TM
    ends = jnp.cumsum(padded)
    tile_base = (ends - padded)[None, :] + jnp.cumsum(cnt, axis=0) - cnt
    base = jnp.broadcast_to(tile_base.T[:, :, None], (N_EXPERTS, T // tm, tm)).reshape(N_EXPERTS, T)
    ids = jnp.arange(N_EXPERTS, dtype=jnp.int32)[:, None]
    dest = []
    for e_row, r_row in ((RT_E1, RT_R1), (RT_E2, RT_R2)):
        e = route_t[e_row].astype(jnp.int32)
        dest.append(jnp.sum(jnp.where(ids == e[None, :], base, 0), axis=0) + route_t[r_row].astype(jnp.int32))
    n_tiles = _moe_rows(T) // EXPERT_TM
    n_used = ends[-1] // EXPERT_TM
    starts = jnp.arange(n_tiles, dtype=jnp.int32) * EXPERT_TM
    tile_expert = jnp.sum((starts[:, None] >= ends[None, :]).astype(jnp.int32), axis=1)
    last_expert = jnp.sum((jnp.maximum(n_used - 1, 0) * EXPERT_TM >= ends).astype(jnp.int32))
    tile_expert = jnp.minimum(tile_expert, last_expert)
    tiles = jnp.arange(n_tiles, dtype=jnp.int32)
    prev_expert = jnp.concatenate([jnp.full((1,), -1, jnp.int32), tile_expert[:-1]])
    seg_first = ((tile_expert != prev_expert) & (tiles < n_used)).astype(jnp.int32)
    seg_slot = (jnp.cumsum(seg_first) - 1) % 2
    seg_end = jnp.sum(jnp.where(ids == tile_expert[None, :], (ends // EXPERT_TM)[:, None], 0), axis=0)
    nxt = jnp.where(seg_end < n_used, tile_expert[jnp.minimum(seg_end, n_tiles - 1)], -1)
    plan = (tile_expert, seg_first, seg_slot.astype(jnp.int32), nxt.astype(jnp.int32),
            n_used.reshape(1).astype(jnp.int32))
    return dest, plan


def _sc_mesh():
    return plsc.VectorSubcoreMesh(core_axis_name="core", subcore_axis_name="subcore")


def _sc_scatter_pair(rows, d0, d1, n_out):
    T, W = rows.shape
    win = SC_WINDOW

    @pl.kernel(out_type=jax.ShapeDtypeStruct((n_out, W), rows.dtype), mesh=_sc_mesh(), name="moe_sc_scatter")
    def scatter(x_hbm, i0_hbm, i1_hbm, o_hbm):
        def body(x_vmem, i0_vmem, i1_vmem):
            pltpu.sync_copy(x_vmem, o_hbm.at[i0_vmem.at[0]])
            pltpu.sync_copy(x_vmem, o_hbm.at[i1_vmem.at[0]])

        pltpu.emit_pipeline(
            body, grid=(T // win,),
            in_specs=[pl.BlockSpec((win, W), index_map=lambda i: (i, 0)),
                      pl.BlockSpec((1, win), index_map=lambda i: (0, i)),
                      pl.BlockSpec((1, win), index_map=lambda i: (0, i))],
            out_specs=[],
            core_axis_name=("core", "subcore"), dimension_semantics=(pltpu.PARALLEL,),
        )(x_hbm, i0_hbm, i1_hbm)

    return scatter(rows, d0.reshape(1, T), d1.reshape(1, T))


def _sc_gather(table, idx):
    n = idx.shape[0]
    W = table.shape[1]
    win = SC_WINDOW

    @pl.kernel(out_type=jax.ShapeDtypeStruct((n, W), table.dtype), mesh=_sc_mesh(), name="moe_sc_gather")
    def gather(t_hbm, i_hbm, o_hbm):
        def body(i_vmem, o_vmem):
            pltpu.sync_copy(t_hbm.at[i_vmem.at[0]], o_vmem)

        pltpu.emit_pipeline(
            body, grid=(n // win,),
            in_specs=[pl.BlockSpec((1, win), index_map=lambda i: (0, i))],
            out_specs=[pl.BlockSpec((win, W), index_map=lambda i: (i, 0))],
            core_axis_name=("core", "subcore"), dimension_semantics=(pltpu.PARALLEL,),
        )(i_hbm, o_hbm)

    return gather(table, idx.reshape(1, n))


def _expert_kernel(te_ref, first_ref, slot_ref, nxt_ref, nu_ref, xa_hbm, xb_hbm, wg_hbm, wu_hbm, wd_hbm,
                   ya_ref, yb_ref, xa_s, xb_s, wg_s, wu_s, wd_s, wg16, wu16, wd16, xsem, sem):
    i = pl.program_id(0)
    n_used = nu_ref[0]
    active = i < n_used
    tm = xa_s.shape[1]
    ahead = ROW_SLOTS - 1

    def row_copies(tile):
        s = lax.rem(tile, ROW_SLOTS)
        rows = pl.ds(pl.multiple_of(tile * tm, tm), tm)
        return (pltpu.make_async_copy(xa_hbm.at[rows], xa_s.at[s], xsem.at[s, 0]),
                pltpu.make_async_copy(xb_hbm.at[rows], xb_s.at[s], xsem.at[s, 1]))

    def weight_copies(expert, s):
        return (pltpu.make_async_copy(wg_hbm.at[expert], wg_s.at[s], sem.at[s, 0]),
                pltpu.make_async_copy(wu_hbm.at[expert], wu_s.at[s], sem.at[s, 1]),
                pltpu.make_async_copy(wd_hbm.at[expert], wd_s.at[s], sem.at[s, 2]))

    @pl.when(i == 0)
    def _():
        for c in weight_copies(te_ref[0], slot_ref[0]):
            c.start()
        for t in range(ahead):
            @pl.when(t < n_used)
            def _():
                for c in row_copies(jnp.int32(t)):
                    c.start()

    @pl.when(i + ahead < n_used)
    def _():
        for c in row_copies(i + ahead):
            c.start()

    @pl.when(active & (first_ref[i] == 1))
    def _():
        s = slot_ref[i]
        for c in weight_copies(te_ref[i], s):
            c.wait()

        @pl.when(nxt_ref[i] >= 0)
        def _():
            for c in weight_copies(nxt_ref[i], 1 - s):
                c.start()

        wg16[...] = wg_s[s].astype(bf16)
        wu16[...] = wu_s[s].astype(bf16)
        wd16[...] = wd_s[s].astype(bf16)

    @pl.when(active)
    def _():
        for c in row_copies(i):
            c.wait()
        s = lax.rem(i, ROW_SLOTS)
        hi, lo = _unpack_bf16_pairs(jnp.concatenate([xa_s[s], xb_s[s]], axis=1))
        x = jnp.concatenate([hi, lo], axis=1).astype(bf16)
        gate = _dot(x, wg16[...])
        up = _dot(x, wu16[...])
        hg = (gate * _sigmoid(gate) * up).astype(bf16)
        y = _pack_bf16_pairs(_dot(hg, wd16[...]))
        quarter = y.shape[1] // 2
        ya_ref[...] = y[:, :quarter]
        yb_ref[...] = y[:, quarter:]

    @pl.when(jnp.logical_not(active))
    def _():
        ya_ref[...] = jnp.zeros(ya_ref.shape, ya_ref.dtype)
        yb_ref[...] = jnp.zeros(yb_ref.shape, yb_ref.dtype)


def _experts(plan, layer, xa, xb, wg, wu, wd):
    tile_expert, seg_first, seg_slot, nxt, n_used = plan
    off = layer * N_EXPERTS
    n_rows, quarter = xa.shape
    D = 4 * quarter
    tm = EXPERT_TM
    row_spec = pl.BlockSpec((tm, quarter), lambda i, *_: (i, 0))
    grid_spec = pltpu.PrefetchScalarGridSpec(
        num_scalar_prefetch=5,
        grid=(n_rows // tm,),
        in_specs=[pl.BlockSpec(memory_space=pl.ANY)] * 5,
        out_specs=[row_spec, row_spec],
        scratch_shapes=[pltpu.VMEM((ROW_SLOTS, tm, quarter), xa.dtype), pltpu.VMEM((ROW_SLOTS, tm, quarter), xb.dtype),
                        pltpu.VMEM((2, D, D_EXPERT), wg.dtype), pltpu.VMEM((2, D, D_EXPERT), wu.dtype),
                        pltpu.VMEM((2, D_EXPERT, D), wd.dtype),
                        pltpu.VMEM((D, D_EXPERT), bf16), pltpu.VMEM((D, D_EXPERT), bf16), pltpu.VMEM((D_EXPERT, D), bf16),
                        pltpu.SemaphoreType.DMA((ROW_SLOTS, 2)), pltpu.SemaphoreType.DMA((2, 3))])
    return pl.pallas_call(
        _expert_kernel,
        grid_spec=grid_spec,
        out_shape=[jax.ShapeDtypeStruct((n_rows, quarter), jnp.uint32)] * 2,
        compiler_params=_cparams("arbitrary"),
        name="moe_experts",
    )(tile_expert + off, seg_first, seg_slot, jnp.where(nxt >= 0, nxt + off, -1), n_used, xa, xb, wg, wu, wd)


def _combine_kernel(x_ref, rt_ref, g_ref, a1_ref, b1_ref, a2_ref, b2_ref, *rest, final):
    half = x_ref.shape[1] // 2
    rp = x_ref.shape[0] // COMBINE_PARTS
    for p in range(COMBINE_PARTS):
        r = slice(p * rp, (p + 1) * rp)
        h1, l1 = _unpack_bf16_pairs(jnp.concatenate([a1_ref[r, :], b1_ref[r, :]], axis=1))
        h2, l2 = _unpack_bf16_pairs(jnp.concatenate([a2_ref[r, :], b2_ref[r, :]], axis=1))
        w1 = rt_ref[r, RT_W1:RT_W1 + 1]
        w2 = rt_ref[r, RT_W2:RT_W2 + 1]
        out = jnp.concatenate([x_ref[r, 0:half] + w1 * h1 + w2 * h2, x_ref[r, half:] + w1 * l1 + w2 * l2], axis=1)
        if final:
            (o_ref,) = rest
            o_ref[r, :] = _rms(out, g_ref[...])
        else:
            w_ref, o_ref, p_ref = rest
            o_ref[r, :] = out
            p_ref[r, :] = _dot(_rms(out, g_ref[...]).astype(bf16), w_ref[...]).astype(p_ref.dtype)


def _combine(x, route, ga, gb, gain, w_next=None):
    T, D = x.shape
    tm = ROUTE_TM
    n = T // tm
    quarter = D // 4
    final = w_next is None
    slot1 = pl.BlockSpec((tm, quarter), lambda i: (i, 0))
    slot2 = pl.BlockSpec((tm, quarter), lambda i: (i + n, 0))
    row_spec = pl.BlockSpec((tm, D), lambda i: (i, 0))
    in_specs = [row_spec, pl.BlockSpec((tm, ROUTER_LANES), lambda i: (i, 0)), pl.BlockSpec((1, D), lambda i: (0, 0)),
                slot1, slot1, slot2, slot2]
    args = [x, route, gain.reshape(1, D), ga, gb, ga, gb]
    out_specs = [row_spec]
    out_shape = [jax.ShapeDtypeStruct((T, D), f32)]
    if not final:
        n_next = w_next.shape[1]
        in_specs.append(pl.BlockSpec((D, n_next), lambda i: (0, 0)))
        args.append(w_next)
        out_specs.append(pl.BlockSpec((tm, n_next), lambda i: (i, 0)))
        out_shape.append(jax.ShapeDtypeStruct((T, n_next), bf16))
    outs = pl.pallas_call(
        functools.partial(_combine_kernel, final=final),
        grid=(n,),
        in_specs=in_specs,
        out_specs=out_specs,
        out_shape=out_shape,
        compiler_params=_cparams("parallel"),
        name="moe_combine",
    )(*args)
    return outs[0] if final else outs


def _moe(layer, x, hpa, hpb, route, route_t, counts, wg, wu, wd, gain, w_next=None):
    (d0, d1), plan = _moe_plan(route_t, counts)
    n_rows = _moe_rows(x.shape[0])
    xa = _sc_scatter_pair(hpa, d0, d1, n_rows)
    xb = _sc_scatter_pair(hpb, d0, d1, n_rows)
    ya, yb = _experts(plan, layer, xa, xb, wg, wu, wd)
    dcat = jnp.concatenate([d0, d1])
    return _combine(x, route, _sc_gather(ya, dcat), _sc_gather(yb, dcat), gain, w_next)


SWA_LOCKSTEP = 2


def _swa_bias_tables():
    blk = C_BLOCK
    qi = np.arange(blk)[:, None]
    kj = np.arange(3 * blk)[None, :]
    dist = np.abs(blk + qi - kj)
    tabs = np.full((3, C_KV_HEADS, C_GROUP * blk, 3 * blk), -np.inf, np.float32)
    for case in range(3):
        ok = dist <= WINDOW
        if case == 0:
            ok = ok & (kj >= blk)
        if case == 2:
            ok = ok & (kj < 2 * blk)
        for hh in range(C_HEADS):
            slope = 2.0 ** (-8.0 * (hh + 1) / C_HEADS)
            t = np.where(ok, -slope * LOG2E * dist, -np.inf)
            kvh, g = divmod(hh, C_GROUP)
            tabs[case, kvh, g * blk:(g + 1) * blk] = t
    return jnp.asarray(tabs)


def _swa_kernel(sink_ref, q_ref, kp_ref, kc_ref, kn_ref, vp_ref, vc_ref, vn_ref, bias_ref, o_ref):
    blk = C_BLOCK
    hd = C_HEAD_DIM
    kcat = jnp.concatenate([kp_ref[0], kc_ref[0], kn_ref[0]], axis=0)
    vcat = jnp.concatenate([vp_ref[0], vc_ref[0], vn_ref[0]], axis=0)
    q = q_ref[0]
    rblk = lax.broadcasted_iota(jnp.int32, (C_GROUP * blk, 1), 0) // blk
    lane = lax.broadcasted_iota(jnp.int32, (3 * blk, LANE - hd), 1)
    ones_pad = jnp.where(lane == 0, 1.0, 0.0).astype(bf16)
    q4s, sinks = [], []
    for kvh in range(C_KV_HEADS):
        q4s.append(jnp.concatenate(
            [q[:, (kvh * C_GROUP + g) * hd:(kvh * C_GROUP + g + 1) * hd] for g in range(C_GROUP)], axis=0))
        sk = jnp.zeros((C_GROUP * blk, 1), f32)
        for g in range(C_GROUP):
            sk = jnp.where(rblk == g, sink_ref[kvh * C_GROUP + g] * LOG2E, sk)
        sinks.append(sk)
    outs = []
    for h0 in range(0, C_KV_HEADS, SWA_LOCKSTEP):
        hs = range(h0, h0 + SWA_LOCKSTEP)
        ss = [_dot_nt(q4s[h], kcat[:, h * hd:(h + 1) * hd]) + bias_ref[0, h] for h in hs]
        ms = [jnp.maximum(jnp.max(s, axis=-1, keepdims=True), sinks[h]) for s, h in zip(ss, hs)]
        ps = [jnp.exp2(s - m).astype(bf16) for s, m in zip(ss, ms)]
        pvs = [_dot(p, jnp.concatenate([vcat[:, h * hd:(h + 1) * hd], ones_pad], axis=1)) for p, h in zip(ps, hs)]
        for pv, m, h in zip(pvs, ms, hs):
            den = pv[:, hd:hd + 1] + jnp.exp2(sinks[h] - m)
            o4 = pv[:, 0:hd] / den
            outs += [o4[g * blk:(g + 1) * blk, :] for g in range(C_GROUP)]
    o_ref[0] = jnp.concatenate(outs, axis=1).astype(o_ref.dtype)


def _swa(qkv, sinks):
    B, L, _ = qkv.shape
    nb = L // C_BLOCK
    dq = C_HEADS * C_HEAD_DIM
    dkv = C_KV_HEADS * C_HEAD_DIM
    k_col = dq // dkv
    v_col = k_col + 1

    def kv_spec(col, shift):
        return pl.BlockSpec((1, C_BLOCK, dkv),
                            lambda n, b: (b, jnp.clip(n + shift, 0, nb - 1), col))

    def edge_case(n, b):
        return (jnp.where(n == 0, 0, jnp.where(n == nb - 1, 2, 1)), 0, 0, 0)

    return pl.pallas_call(
        _swa_kernel,
        grid=(nb, B),
        in_specs=[pl.BlockSpec(memory_space=pltpu.SMEM),
                  pl.BlockSpec((1, C_BLOCK, dq), lambda n, b: (b, n, 0)),
                  kv_spec(k_col, -1), kv_spec(k_col, 0), kv_spec(k_col, 1),
                  kv_spec(v_col, -1), kv_spec(v_col, 0), kv_spec(v_col, 1),
                  pl.BlockSpec((1, C_KV_HEADS, C_GROUP * C_BLOCK, 3 * C_BLOCK), edge_case)],
        out_specs=pl.BlockSpec((1, C_BLOCK, dq), lambda n, b: (b, n, 0)),
        out_shape=jax.ShapeDtypeStruct((B, L, dq), bf16),
        compiler_params=_cparams("parallel", "parallel"),
        name="swa",
    )(sinks, qkv, qkv, qkv, qkv, qkv, qkv, qkv, _swa_bias_tables())


def _hyena_filters(L, w1, b1, w2, b2, w3, b3, w4, freq):
    hp = lax.Precision.HIGHEST
    t = jnp.linspace(0.0, 1.0, L, dtype=f32)[:, None]
    wpos = (2.0 * math.pi / L) * jnp.arange(L, dtype=f32)[:, None]
    bands = jnp.linspace(1e-4, HY_BANDS - 1, HY_BANDS, dtype=f32)[None, :]
    feats = jnp.concatenate([t, jnp.cos(bands * wpos), -jnp.sin(bands * wpos)], axis=-1)
    h = jnp.sin(freq[0] * (jnp.dot(feats, w1, precision=hp) + b1))
    h = jnp.sin(freq[1] * (jnp.dot(h, w2, precision=hp) + b2))
    h = jnp.sin(freq[2] * (jnp.dot(h, w3, precision=hp) + b3))
    h = jnp.dot(h, w4, precision=hp).reshape(L, HY_ORDER, 2, B_WIDTH)
    deltas = jnp.abs(jnp.linspace(HY_MIN_DECAY, HY_MAX_DECAY, B_WIDTH, dtype=f32))
    window = jnp.exp(-t * deltas[None, :])
    return h * window[:, None, None, :]


def _pad_rows(w):
    return jnp.pad(w, ((0, SUBLANE - w.shape[0]), (0, 0)))


def _even_in_weights(w_in):
    hk = A_HEADS * A_DK
    q = w_in[:, 0:hk].reshape(D_MODEL, A_HEADS, A_DK)
    k = w_in[:, hk:2 * hk].reshape(D_MODEL, A_HEADS, A_DK)
    v = w_in[:, 2 * hk:A_QKV].reshape(D_MODEL, A_HEADS, A_DV)
    z = w_in[:, A_QKV:A_QKV + A_WIDTH].reshape(D_MODEL, A_HEADS, A_DV)
    qkvz = jnp.concatenate([q, k, v, z], axis=-1).reshape(D_MODEL, A_HEADS * 4 * A_DK)
    ab = w_in[:, A_QKV + A_WIDTH:A_QKV + A_WIDTH + 4 * A_HEADS].reshape(D_MODEL, 4, A_HEADS)
    ab = jnp.transpose(ab, (0, 2, 1))
    ab = jnp.pad(ab.reshape(D_MODEL, 4 * A_HEADS), ((0, 0), (0, LANE - 4 * A_HEADS)))
    hx = w_in[:, A_QKV + A_WIDTH + 4 * A_HEADS:]
    return jnp.concatenate([qkvz, ab, hx], axis=1).astype(bf16)


def _router_weights(w_group, w_expert):
    w = jnp.concatenate([w_group, w_expert], axis=1)
    w = jnp.pad(w, ((0, 0), (0, ROUTER_LANES - w.shape[1])))
    hi = w.astype(bf16)
    lo = (w - hi.astype(f32)).astype(bf16)
    return jnp.concatenate([hi, lo], axis=1)


def kernel(x_prompt, x_sample, norm_mix, norm_ffn, norm_final, ev_w_in, ev_conv_a, ev_alog_f, ev_alog_b, ev_dtb_f, ev_dtb_b, ev_onorm, ev_conv_b, hy_w1, hy_b1, hy_w2, hy_b2, hy_w3, hy_b3, hy_w4, hy_freq, hy_dbias, ev_w_out, od_w_qkv, od_sinks, od_w_out, moe_w_group, moe_w_expert, moe_w_gate, moe_w_up, moe_w_down):
    L = x_prompt.shape[1]
    n_layers = moe_w_gate.shape[0]
    wg = moe_w_gate.reshape(n_layers * N_EXPERTS, D_MODEL, D_EXPERT)
    wu = moe_w_up.reshape(n_layers * N_EXPERTS, D_MODEL, D_EXPERT)
    wd = moe_w_down.reshape(n_layers * N_EXPERTS, D_EXPERT, D_MODEL)

    w_in = _even_in_weights(ev_w_in[0])
    n_qkvz = A_HEADS * 4 * A_DK
    n_ab = A_HEADS * LANE
    n_hx = 3 * B_WIDTH
    hk = A_HEADS * A_DK
    ca = ev_conv_a[0]
    conv_a = jnp.concatenate([ca[0:hk].reshape(A_HEADS, A_DK, 3), ca[hk:2 * hk].reshape(A_HEADS, A_DK, 3),
                              ca[2 * hk:].reshape(A_HEADS, A_DV, 3)], axis=1)
    conv_a = jnp.pad(jnp.transpose(conv_a, (0, 2, 1)), ((0, 0), (0, SUBLANE - 3), (0, 0)))
    gate_sc = jnp.stack([ev_alog_f[0], ev_alog_b[0], ev_dtb_f[0], ev_dtb_b[0]], axis=1)
    filt = _hyena_filters(L, hy_w1[0], hy_b1[0], hy_w2[0], hy_b2[0], hy_w3[0], hy_b3[0], hy_w4[0], hy_freq[0])
    hsum = (filt[:, :, 0] + filt[:, :, 1]).reshape(L, HY_ORDER * B_WIDTH)
    hdiff = (filt[:, :, 0] - filt[:, :, 1]).reshape(L, HY_ORDER * B_WIDTH)
    kr, ki = _filter_spectrum(hsum, hdiff)
    fft_tables = _hyena_tables(L)
    conv_b = _pad_rows(ev_conv_b[0].T)
    dbias = _pad_rows(hy_dbias[0])
    w_out = ev_w_out[0].astype(bf16)
    routers = [_router_weights(moe_w_group[l], moe_w_expert[l]) for l in range(n_layers)]
    n_qkv = od_w_qkv.shape[-1]
    dq = C_HEADS * C_HEAD_DIM
    w_qkv = jnp.concatenate([od_w_qkv[0][:, :dq] * (LOG2E * C_HEAD_DIM ** -0.5), od_w_qkv[0][:, dq:]], axis=1)
    w_qkv = w_qkv.astype(bf16)
    w_att_out = od_w_out[0].astype(bf16)

    def trunk(xb):
        B = xb.shape[0]
        T = B * L
        x = xb.reshape(T, D_MODEL)
        qkvz, abh, hx = _even_in_proj(x, norm_mix[0], w_in)
        o_a = _deltanet(qkvz.reshape(B, L, n_qkvz), abh.reshape(B, L, n_ab), conv_a, gate_sc, ev_onorm[0])
        o_b = _hyena_fft(hx.reshape(B, L, n_hx), conv_b, fft_tables, kr, ki, dbias)
        x, hpa, hpb, route, route_t, counts = _proj_route(
            x, (o_a.reshape(T, A_WIDTH), o_b.reshape(T, B_WIDTH)), (w_out[:A_WIDTH], w_out[A_WIDTH:]),
            norm_ffn[0], routers[0])
        x, qkv = _moe(0, x, hpa, hpb, route, route_t, counts, wg, wu, wd, norm_mix[1], w_qkv)
        o_c = _swa(qkv.reshape(B, L, n_qkv), od_sinks[0])
        x, hpa, hpb, route, route_t, counts = _proj_route(
            x, (o_c.reshape(T, C_HEADS * C_HEAD_DIM),), (w_att_out,), norm_ffn[1], routers[1])
        y = _moe(1, x, hpa, hpb, route, route_t, counts, wg, wu, wd, norm_final)
        return y.reshape(B, L, D_MODEL)

    return (trunk(x_prompt), trunk(x_sample))
```
